```python
import math
import jax, jax.numpy as jnp
from jax import lax
import numpy as np

D_MODEL = 1024
BATCH = 8
SEQ = 4096
DEPTH = 4

PLE_DIM = 256
N_BRANCH = 4
BRANCH_W = D_MODEL // N_BRANCH
CONV_W = 31
MLA_HEADS = 4
MLA_NOPE = 64
MLA_ROPE = 32
MLA_V = 64
MLA_Q_RANK = D_MODEL // 4
MLA_KV_RANK = D_MODEL // 8
ROPE_THETA = 10000.0
ATTN_BLOCK = 128
SSM_GROUP = 16
SSM_GROUPS = BRANCH_W // SSM_GROUP
SSM_STATE = 64
DT_MIN = 1e-3
DT_MAX = 1e-1
SWA_HEADS = 4
SWA_KV_HEADS = 2
SWA_HEAD_DIM = 64
WINDOW = 128
DEEPNORM_ALPHA = (2.0 * DEPTH) ** 0.25
DEEPNORM_BETA = (8.0 * DEPTH) ** -0.25
LN_EPS = 1e-5
RMS_EPS = 1e-6

IN_SPLITS = (
    BRANCH_W, BRANCH_W, BRANCH_W,
    MLA_Q_RANK, MLA_KV_RANK, MLA_ROPE, BRANCH_W,
    BRANCH_W, BRANCH_W,
    SWA_HEADS * SWA_HEAD_DIM, SWA_KV_HEADS * SWA_HEAD_DIM,
    SWA_KV_HEADS * SWA_HEAD_DIM, BRANCH_W,
)
IN_WIDTH = sum(IN_SPLITS)
IN_OFFSETS = tuple(sum(IN_SPLITS[:i + 1]) for i in range(len(IN_SPLITS) - 1))

kernel_name = 'hybrid_gated_parallel_mixers'


def layer_norm(x, g, b):
    xf = x.astype(jnp.float32)
    mu = jnp.mean(xf, axis=-1, keepdims=True)
    var = jnp.mean(jnp.square(xf - mu), axis=-1, keepdims=True)
    return ((xf - mu) * lax.rsqrt(var + LN_EPS) * g.astype(jnp.float32) + b.astype(jnp.float32)).astype(x.dtype)


def rms_norm(x, g):
    xf = x.astype(jnp.float32)
    ms = jnp.mean(jnp.square(xf), axis=-1, keepdims=True)
    return (xf * lax.rsqrt(ms + RMS_EPS) * g.astype(jnp.float32)).astype(x.dtype)


def rope_tables(seq):
    pos = jnp.arange(seq, dtype=jnp.float32)
    inv_freq = ROPE_THETA ** (-jnp.arange(0, MLA_ROPE, 2, dtype=jnp.float32) / MLA_ROPE)
    ang = pos[:, None] * inv_freq[None, :]
    return jnp.cos(ang), jnp.sin(ang)


def rope(x, cos, sin):
    half = x.shape[-1] // 2
    xf = x.astype(jnp.float32)
    x1, x2 = xf[..., :half], xf[..., half:]
    return jnp.concatenate([x1 * cos - x2 * sin, x2 * cos + x1 * sin], axis=-1).astype(x.dtype)


def conv_module(a_val, a_gate, conv_w, conv_b, norm_g, norm_b, w_pw2):
    h = a_val * jax.nn.sigmoid(a_gate)
    h = jnp.pad(h, ((0, 0), (CONV_W - 1, 0), (0, 0)))
    h = lax.conv_general_dilated(h, conv_w[:, None, :].astype(h.dtype), window_strides=(1,),
                                 padding='VALID', dimension_numbers=('NWC', 'WIO', 'NWC'),
                                 feature_group_count=BRANCH_W) + conv_b
    h = jax.nn.silu(layer_norm(h, norm_g, norm_b))
    return h @ w_pw2


def mla(c_q, c_kv, k_r, q_norm_g, kv_norm_g, w_uq, w_ukv, cos, sin):
    B, S, _ = c_q.shape
    q = (rms_norm(c_q, q_norm_g) @ w_uq).reshape(B, S, MLA_HEADS, MLA_NOPE + MLA_ROPE)
    q_nope = q[..., :MLA_NOPE]
    q_rope = rope(q[..., MLA_NOPE:], cos[:, None, :], sin[:, None, :])
    kv = (rms_norm(c_kv, kv_norm_g) @ w_ukv).reshape(B, S, MLA_HEADS, MLA_NOPE + MLA_V)
    k_nope, v = kv[..., :MLA_NOPE], kv[..., MLA_NOPE:]
    k_rope = rope(k_r, cos, sin)
    scale = (MLA_NOPE + MLA_ROPE) ** -0.5
    nblk = S // ATTN_BLOCK
    qn = q_nope.reshape(B, nblk, ATTN_BLOCK, MLA_HEADS, MLA_NOPE).transpose(1, 0, 2, 3, 4)
    qr = q_rope.reshape(B, nblk, ATTN_BLOCK, MLA_HEADS, MLA_ROPE).transpose(1, 0, 2, 3, 4)
    k_pos = jnp.arange(S)

    def block(args):
        qn_b, qr_b, i = args
        s = (jnp.einsum('bqhd,bkhd->bhqk', qn_b, k_nope)
             + jnp.einsum('bqhr,bkr->bhqk', qr_b, k_rope)).astype(jnp.float32) * scale
        q_pos = i * ATTN_BLOCK + jnp.arange(ATTN_BLOCK)
        s = jnp.where(k_pos[None, :] <= q_pos[:, None], s, -jnp.inf)
        prob = jax.nn.softmax(s, axis=-1).astype(v.dtype)
        return jnp.einsum('bhqk,bkhd->bqhd', prob, v)

    o = lax.map(block, (qn, qr, jnp.arange(nblk)))
    return o.transpose(1, 0, 2, 3, 4).reshape(B, S, MLA_HEADS * MLA_V)


def s5_layer(u, a_re, a_im, log_dt, b_re, b_im, c_re, c_im, d, w_glu):
    B, S, W = u.shape
    f32 = jnp.float32
    uf = u.astype(f32)
    ug = uf.reshape(B, S, SSM_GROUPS, SSM_GROUP)
    dt = jnp.exp(log_dt.astype(f32))[:, None]
    lr, li = a_re.astype(f32), a_im.astype(f32)
    mag = jnp.exp(lr * dt)
    lb_re, lb_im = mag * jnp.cos(li * dt), mag * jnp.sin(li * dt)
    den = lr * lr + li * li
    nr, ni = lb_re - 1.0, lb_im
    f_re = ((nr * lr + ni * li) / den)[..., None]
    f_im = ((ni * lr - nr * li) / den)[..., None]
    br, bi = b_re.astype(f32), b_im.astype(f32)
    bb_re = f_re * br - f_im * bi
    bb_im = f_re * bi + f_im * br
    bu_re = jnp.einsum('bsgh,gph->bsgp', ug, bb_re)
    bu_im = jnp.einsum('bsgh,gph->bsgp', ug, bb_im)
    a_r = jnp.broadcast_to(lb_re[None, None], (1, S, SSM_GROUPS, SSM_STATE))
    a_i = jnp.broadcast_to(lb_im[None, None], (1, S, SSM_GROUPS, SSM_STATE))

    def combine(e1, e2):
        a1r, a1i, x1r, x1i = e1
        a2r, a2i, x2r, x2i = e2
        return (a2r * a1r - a2i * a1i, a2r * a1i + a2i * a1r,
                a2r * x1r - a2i * x1i + x2r, a2r * x1i + a2i * x1r + x2i)

    _, _, h_re, h_im = lax.associative_scan(combine, (a_r, a_i, bu_re, bu_im), axis=1)
    y = (jnp.einsum('bsgp,ghp->bsgh', h_re, c_re.astype(f32))
         - jnp.einsum('bsgp,ghp->bsgh', h_im, c_im.astype(f32))).reshape(B, S, W)
    y = jax.nn.gelu(y + d.astype(f32) * uf).astype(u.dtype)
    g = y @ w_glu
    return g[..., :W] * jax.nn.sigmoid(g[..., W:])


def swa(q, k, v, sinks):
    B, S, _ = q.shape
    nb = S // WINDOW
    G = SWA_HEADS // SWA_KV_HEADS
    q = q.reshape(B, nb, WINDOW, SWA_KV_HEADS, G, SWA_HEAD_DIM)
    k = k.reshape(B, nb, WINDOW, SWA_KV_HEADS, SWA_HEAD_DIM)
    v = v.reshape(B, nb, WINDOW, SWA_KV_HEADS, SWA_HEAD_DIM)
    prev = lambda t: jnp.concatenate([jnp.zeros_like(t[:, :1]), t[:, :-1]], axis=1)
    k2 = jnp.concatenate([prev(k), k], axis=2)
    v2 = jnp.concatenate([prev(v), v], axis=2)
    s = jnp.einsum('bnqhgd,bnkhd->bnhgqk', q, k2).astype(jnp.float32) * (SWA_HEAD_DIM ** -0.5)
    qi = jnp.arange(WINDOW)[:, None] + WINDOW
    kj = jnp.arange(2 * WINDOW)[None, :]
    rel = qi - kj
    band = (rel >= 0) & (rel < WINDOW)
    blk = jnp.arange(nb)[:, None, None]
    valid = band[None] & ((blk > 0) | (kj >= WINDOW)[None])
    s = jnp.where(valid[None, :, None, None], s, -jnp.inf)
    sink = sinks.astype(jnp.float32).reshape(SWA_KV_HEADS, G)
    sink_col = jnp.broadcast_to(sink[None, None, :, :, None, None], s.shape[:-1] + (1,))
    prob = jax.nn.softmax(jnp.concatenate([s, sink_col], axis=-1), axis=-1)[..., :-1]
    o = jnp.einsum('bnhgqk,bnkhd->bnqhgd', prob.astype(v.dtype), v2)
    return o.reshape(B, S, SWA_HEADS * SWA_HEAD_DIM)


def hybrid_layer(x, p_i, cos, sin, w_in, w_merge, b_merge, conv_w, conv_b, conv_norm_g, conv_norm_b,
                 w_pw2, mla_q_norm_g, mla_kv_norm_g, w_uq, w_ukv, ssm_a_re, ssm_a_im, ssm_log_dt,
                 ssm_b_re, ssm_b_im, ssm_c_re, ssm_c_im, ssm_d, w_glu, attn_sinks, w_branch, w_out,
                 ln_g, ln_b, w_ple, w_ple_gate, ple_norm_g):
    B, S, D = x.shape
    h = x @ w_in
    (a_val, a_gate, a_z, c_q, c_kv, k_r, b_z, u, c_z, q, k, v, d_z) = jnp.split(h, IN_OFFSETS, axis=-1)
    y_a = conv_module(a_val, a_gate, conv_w, conv_b, conv_norm_g, conv_norm_b, w_pw2) * jax.nn.silu(a_z)
    y_b = mla(c_q, c_kv, k_r, mla_q_norm_g, mla_kv_norm_g, w_uq, w_ukv, cos, sin) * jax.nn.silu(b_z)
    y_c = s5_layer(u, ssm_a_re, ssm_a_im, ssm_log_dt, ssm_b_re, ssm_b_im, ssm_c_re, ssm_c_im,
                   ssm_d, w_glu) * jax.nn.silu(c_z)
    y_d = swa(q, k, v, attn_sinks) * jax.nn.silu(d_z)
    ys = jnp.stack([y_a, y_b, y_c, y_d], axis=2)
    branch = jnp.einsum('bsnw,nwd->bsnd', ys, w_branch)
    gates = jax.nn.sigmoid(x @ w_merge + b_merge).reshape(B, S, N_BRANCH, D)
    merged = jnp.einsum('bsnd,bsnd->bsd', gates, branch)
    x = layer_norm(DEEPNORM_ALPHA * x + merged @ w_out, ln_g, ln_b)
    e = (p_i @ w_ple) * jax.nn.sigmoid(x @ w_ple_gate)
    return x + rms_norm(e, ple_norm_g)


def _fwd_setup_inputs(seed: int = 0) -> dict:
    key = jax.random.key(seed)
    ks = jax.random.split(key, 31)
    L, D, W = DEPTH, D_MODEL, BRANCH_W
    G, P, H = SSM_GROUPS, SSM_STATE, SSM_GROUP
    nrm = lambda k, shape, scale: jax.random.normal(k, shape, jnp.float32) * scale
    n_idx = jnp.arange(P, dtype=jnp.float32)
    return {
        'x': nrm(ks[0], (BATCH, SEQ, D), 1.0),
        'p': nrm(ks[1], (DEPTH, BATCH, SEQ, PLE_DIM), 1.0),
        'w_in': nrm(ks[2], (L, D, IN_WIDTH), D ** -0.5),
        'w_merge': nrm(ks[3], (L, D, N_BRANCH * D), D ** -0.5),
        'b_merge': nrm(ks[4], (L, N_BRANCH * D), 0.01),
        'conv_w': nrm(ks[5], (L, CONV_W, W), CONV_W ** -0.5),
        'conv_b': nrm(ks[6], (L, W), 0.01),
        'conv_norm_g': 1.0 + nrm(ks[7], (L, W), 0.01),
        'conv_norm_b': nrm(ks[8], (L, W), 0.01),
        'w_pw2': nrm(ks[9], (L, W, W), W ** -0.5),
        'mla_q_norm_g': 1.0 + nrm(ks[10], (L, MLA_Q_RANK), 0.01),
        'mla_kv_norm_g': 1.0 + nrm(ks[11], (L, MLA_KV_RANK), 0.01),
        'w_uq': nrm(ks[12], (L, MLA_Q_RANK, MLA_HEADS * (MLA_NOPE + MLA_ROPE)), MLA_Q_RANK ** -0.5),
        'w_ukv': nrm(ks[13], (L, MLA_KV_RANK, MLA_HEADS * (MLA_NOPE + MLA_V)), MLA_KV_RANK ** -0.5),
        'ssm_a_re': -0.5 + nrm(ks[14], (L, G, P), 0.01),
        'ssm_a_im': math.pi * n_idx + nrm(ks[15], (L, G, P), 0.01),
        'ssm_log_dt': jax.random.uniform(ks[16], (L, G), jnp.float32, math.log(DT_MIN), math.log(DT_MAX)),
        'ssm_b_re': nrm(ks[17], (L, G, P, H), (2 * H) ** -0.5),
        'ssm_b_im': nrm(ks[18], (L, G, P, H), (2 * H) ** -0.5),
        'ssm_c_re': nrm(ks[19], (L, G, H, P), P ** -0.5),
        'ssm_c_im': nrm(ks[20], (L, G, H, P), P ** -0.5),
        'ssm_d': nrm(ks[21], (L, W), 1.0),
        'w_glu': nrm(ks[22], (L, W, 2 * W), W ** -0.5),
        'attn_sinks': nrm(ks[23], (L, SWA_HEADS), 0.5),
        'w_branch': nrm(ks[24], (L, N_BRANCH, W, D), DEEPNORM_BETA * W ** -0.5),
        'w_out': nrm(ks[25], (L, D, D), DEEPNORM_BETA * D ** -0.5),
        'ln_g': 1.0 + nrm(ks[26], (L, D), 0.01),
        'ln_b': nrm(ks[27], (L, D), 0.01),
        'w_ple': nrm(ks[28], (L, PLE_DIM, D), PLE_DIM ** -0.5),
        'w_ple_gate': nrm(ks[29], (L, D, D), D ** -0.5),
        'ple_norm_g': 1.0 + nrm(ks[30], (L, D), 0.01),
    }


def _fwd_reference(x, p, w_in, w_merge, b_merge, conv_w, conv_b, conv_norm_g, conv_norm_b, w_pw2,
              mla_q_norm_g, mla_kv_norm_g, w_uq, w_ukv, ssm_a_re, ssm_a_im, ssm_log_dt,
              ssm_b_re, ssm_b_im, ssm_c_re, ssm_c_im, ssm_d, w_glu, attn_sinks, w_branch, w_out,
              ln_g, ln_b, w_ple, w_ple_gate, ple_norm_g):
    cos, sin = rope_tables(x.shape[1])
    for i in range(DEPTH):
        x = hybrid_layer(x, p[i], cos, sin, w_in[i], w_merge[i], b_merge[i], conv_w[i], conv_b[i],
                         conv_norm_g[i], conv_norm_b[i], w_pw2[i], mla_q_norm_g[i], mla_kv_norm_g[i],
                         w_uq[i], w_ukv[i], ssm_a_re[i], ssm_a_im[i], ssm_log_dt[i], ssm_b_re[i],
                         ssm_b_im[i], ssm_c_re[i], ssm_c_im[i], ssm_d[i], w_glu[i], attn_sinks[i],
                         w_branch[i], w_out[i], ln_g[i], ln_b[i], w_ple[i], w_ple_gate[i], ple_norm_g[i])
    return x


import jax as _jax
import jax.numpy as _jnp

TWIN_FORMAT = 'train_step'
FWD_PARAMS = ['x', 'p', 'w_in', 'w_merge', 'b_merge', 'conv_w', 'conv_b', 'conv_norm_g', 'conv_norm_b', 'w_pw2', 'mla_q_norm_g', 'mla_kv_norm_g', 'w_uq', 'w_ukv', 'ssm_a_re', 'ssm_a_im', 'ssm_log_dt', 'ssm_b_re', 'ssm_b_im', 'ssm_c_re', 'ssm_c_im', 'ssm_d', 'w_glu', 'attn_sinks', 'w_branch', 'w_out', 'ln_g', 'ln_b', 'w_ple', 'w_ple_gate', 'ple_norm_g']
TWIN_WEIGHTS = ['w_in', 'w_merge', 'b_merge', 'conv_w', 'conv_b', 'conv_norm_g', 'conv_norm_b', 'w_pw2', 'mla_q_norm_g', 'mla_kv_norm_g', 'w_uq', 'w_ukv', 'ssm_a_re', 'ssm_a_im', 'ssm_log_dt', 'ssm_b_re', 'ssm_b_im', 'ssm_c_re', 'ssm_c_im', 'ssm_d', 'w_glu', 'attn_sinks', 'w_branch', 'w_out', 'ln_g', 'ln_b', 'w_ple', 'w_ple_gate', 'ple_norm_g']
TWIN_DIFF_INPUT = 'x'
TWIN_INPUTS = ['x', 'p', 'w_in', 'w_merge', 'b_merge', 'conv_w', 'conv_b', 'conv_norm_g', 'conv_norm_b', 'w_pw2', 'mla_q_norm_g', 'mla_kv_norm_g', 'w_uq', 'w_ukv', 'ssm_a_re', 'ssm_a_im', 'ssm_log_dt', 'ssm_b_re', 'ssm_b_im', 'ssm_c_re', 'ssm_c_im', 'ssm_d', 'w_glu', 'attn_sinks', 'w_branch', 'w_out', 'ln_g', 'ln_b', 'w_ple', 'w_ple_gate', 'ple_norm_g', 'loss_target', 'm_w_in', 'm_w_merge', 'm_b_merge', 'm_conv_w', 'm_conv_b', 'm_conv_norm_g', 'm_conv_norm_b', 'm_w_pw2', 'm_mla_q_norm_g', 'm_mla_kv_norm_g', 'm_w_uq', 'm_w_ukv', 'm_ssm_a_re', 'm_ssm_a_im', 'm_ssm_log_dt', 'm_ssm_b_re', 'm_ssm_b_im', 'm_ssm_c_re', 'm_ssm_c_im', 'm_ssm_d', 'm_w_glu', 'm_attn_sinks', 'm_w_branch', 'm_w_out', 'm_ln_g', 'm_ln_b', 'm_w_ple', 'm_w_ple_gate', 'm_ple_norm_g', 'v_w_in', 'v_w_merge', 'v_b_merge', 'v_conv_w', 'v_conv_b', 'v_conv_norm_g', 'v_conv_norm_b', 'v_w_pw2', 'v_mla_q_norm_g', 'v_mla_kv_norm_g', 'v_w_uq', 'v_w_ukv', 'v_ssm_a_re', 'v_ssm_a_im', 'v_ssm_log_dt', 'v_ssm_b_re', 'v_ssm_b_im', 'v_ssm_c_re', 'v_ssm_c_im', 'v_ssm_d', 'v_w_glu', 'v_attn_sinks', 'v_w_branch', 'v_w_out', 'v_ln_g', 'v_ln_b', 'v_w_ple', 'v_w_ple_gate', 'v_ple_norm_g']
TWIN_OUTPUTS = ['loss', 'grad_x', 'grad_w_in', 'grad_w_merge', 'grad_b_merge', 'grad_conv_w', 'grad_conv_b', 'grad_conv_norm_g', 'grad_conv_norm_b', 'grad_w_pw2', 'grad_mla_q_norm_g', 'grad_mla_kv_norm_g', 'grad_w_uq', 'grad_w_ukv', 'grad_ssm_a_re', 'grad_ssm_a_im', 'grad_ssm_log_dt', 'grad_ssm_b_re', 'grad_ssm_b_im', 'grad_ssm_c_re', 'grad_ssm_c_im', 'grad_ssm_d', 'grad_w_glu', 'grad_attn_sinks', 'grad_w_branch', 'grad_w_out', 'grad_ln_g', 'grad_ln_b', 'grad_w_ple', 'grad_w_ple_gate', 'grad_ple_norm_g', 'delta_w_in', 'delta_w_merge', 'delta_b_merge', 'delta_conv_w', 'delta_conv_b', 'delta_conv_norm_g', 'delta_conv_norm_b', 'delta_w_pw2', 'delta_mla_q_norm_g', 'delta_mla_kv_norm_g', 'delta_w_uq', 'delta_w_ukv', 'delta_ssm_a_re', 'delta_ssm_a_im', 'delta_ssm_log_dt', 'delta_ssm_b_re', 'delta_ssm_b_im', 'delta_ssm_c_re', 'delta_ssm_c_im', 'delta_ssm_d', 'delta_w_glu', 'delta_attn_sinks', 'delta_w_branch', 'delta_w_out', 'delta_ln_g', 'delta_ln_b', 'delta_w_ple', 'delta_w_ple_gate', 'delta_ple_norm_g', 'new_m_w_in', 'new_m_w_merge', 'new_m_b_merge', 'new_m_conv_w', 'new_m_conv_b', 'new_m_conv_norm_g', 'new_m_conv_norm_b', 'new_m_w_pw2', 'new_m_mla_q_norm_g', 'new_m_mla_kv_norm_g', 'new_m_w_uq', 'new_m_w_ukv', 'new_m_ssm_a_re', 'new_m_ssm_a_im', 'new_m_ssm_log_dt', 'new_m_ssm_b_re', 'new_m_ssm_b_im', 'new_m_ssm_c_re', 'new_m_ssm_c_im', 'new_m_ssm_d', 'new_m_w_glu', 'new_m_attn_sinks', 'new_m_w_branch', 'new_m_w_out', 'new_m_ln_g', 'new_m_ln_b', 'new_m_w_ple', 'new_m_w_ple_gate', 'new_m_ple_norm_g', 'new_v_w_in', 'new_v_w_merge', 'new_v_b_merge', 'new_v_conv_w', 'new_v_conv_b', 'new_v_conv_norm_g', 'new_v_conv_norm_b', 'new_v_w_pw2', 'new_v_mla_q_norm_g', 'new_v_mla_kv_norm_g', 'new_v_w_uq', 'new_v_w_ukv', 'new_v_ssm_a_re', 'new_v_ssm_a_im', 'new_v_ssm_log_dt', 'new_v_ssm_b_re', 'new_v_ssm_b_im', 'new_v_ssm_c_re', 'new_v_ssm_c_im', 'new_v_ssm_d', 'new_v_w_glu', 'new_v_attn_sinks', 'new_v_w_branch', 'new_v_w_out', 'new_v_ln_g', 'new_v_ln_b', 'new_v_w_ple', 'new_v_w_ple_gate', 'new_v_ple_norm_g']
TWIN_LEAF_KINDS = {'loss': 'loss', 'grad_x': 'grad_x', 'grad_w_in': 'grad_w', 'grad_w_merge': 'grad_w', 'grad_b_merge': 'grad_w', 'grad_conv_w': 'grad_w', 'grad_conv_b': 'grad_w', 'grad_conv_norm_g': 'grad_w', 'grad_conv_norm_b': 'grad_w', 'grad_w_pw2': 'grad_w', 'grad_mla_q_norm_g': 'grad_w', 'grad_mla_kv_norm_g': 'grad_w', 'grad_w_uq': 'grad_w', 'grad_w_ukv': 'grad_w', 'grad_ssm_a_re': 'grad_w', 'grad_ssm_a_im': 'grad_w', 'grad_ssm_log_dt': 'grad_w', 'grad_ssm_b_re': 'grad_w', 'grad_ssm_b_im': 'grad_w', 'grad_ssm_c_re': 'grad_w', 'grad_ssm_c_im': 'grad_w', 'grad_ssm_d': 'grad_w', 'grad_w_glu': 'grad_w', 'grad_attn_sinks': 'grad_w', 'grad_w_branch': 'grad_w', 'grad_w_out': 'grad_w', 'grad_ln_g': 'grad_w', 'grad_ln_b': 'grad_w', 'grad_w_ple': 'grad_w', 'grad_w_ple_gate': 'grad_w', 'grad_ple_norm_g': 'grad_w', 'delta_w_in': 'delta_w', 'delta_w_merge': 'delta_w', 'delta_b_merge': 'delta_w', 'delta_conv_w': 'delta_w', 'delta_conv_b': 'delta_w', 'delta_conv_norm_g': 'delta_w', 'delta_conv_norm_b': 'delta_w', 'delta_w_pw2': 'delta_w', 'delta_mla_q_norm_g': 'delta_w', 'delta_mla_kv_norm_g': 'delta_w', 'delta_w_uq': 'delta_w', 'delta_w_ukv': 'delta_w', 'delta_ssm_a_re': 'delta_w', 'delta_ssm_a_im': 'delta_w', 'delta_ssm_log_dt': 'delta_w', 'delta_ssm_b_re': 'delta_w', 'delta_ssm_b_im': 'delta_w', 'delta_ssm_c_re': 'delta_w', 'delta_ssm_c_im': 'delta_w', 'delta_ssm_d': 'delta_w', 'delta_w_glu': 'delta_w', 'delta_attn_sinks': 'delta_w', 'delta_w_branch': 'delta_w', 'delta_w_out': 'delta_w', 'delta_ln_g': 'delta_w', 'delta_ln_b': 'delta_w', 'delta_w_ple': 'delta_w', 'delta_w_ple_gate': 'delta_w', 'delta_ple_norm_g': 'delta_w', 'new_m_w_in': 'new_m', 'new_m_w_merge': 'new_m', 'new_m_b_merge': 'new_m', 'new_m_conv_w': 'new_m', 'new_m_conv_b': 'new_m', 'new_m_conv_norm_g': 'new_m', 'new_m_conv_norm_b': 'new_m', 'new_m_w_pw2': 'new_m', 'new_m_mla_q_norm_g': 'new_m', 'new_m_mla_kv_norm_g': 'new_m', 'new_m_w_uq': 'new_m', 'new_m_w_ukv': 'new_m', 'new_m_ssm_a_re': 'new_m', 'new_m_ssm_a_im': 'new_m', 'new_m_ssm_log_dt': 'new_m', 'new_m_ssm_b_re': 'new_m', 'new_m_ssm_b_im': 'new_m', 'new_m_ssm_c_re': 'new_m', 'new_m_ssm_c_im': 'new_m', 'new_m_ssm_d': 'new_m', 'new_m_w_glu': 'new_m', 'new_m_attn_sinks': 'new_m', 'new_m_w_branch': 'new_m', 'new_m_w_out': 'new_m', 'new_m_ln_g': 'new_m', 'new_m_ln_b': 'new_m', 'new_m_w_ple': 'new_m', 'new_m_w_ple_gate': 'new_m', 'new_m_ple_norm_g': 'new_m', 'new_v_w_in': 'new_v', 'new_v_w_merge': 'new_v', 'new_v_b_merge': 'new_v', 'new_v_conv_w': 'new_v', 'new_v_conv_b': 'new_v', 'new_v_conv_norm_g': 'new_v', 'new_v_conv_norm_b': 'new_v', 'new_v_w_pw2': 'new_v', 'new_v_mla_q_norm_g': 'new_v', 'new_v_mla_kv_norm_g': 'new_v', 'new_v_w_uq': 'new_v', 'new_v_w_ukv': 'new_v', 'new_v_ssm_a_re': 'new_v', 'new_v_ssm_a_im': 'new_v', 'new_v_ssm_log_dt': 'new_v', 'new_v_ssm_b_re': 'new_v', 'new_v_ssm_b_im': 'new_v', 'new_v_ssm_c_re': 'new_v', 'new_v_ssm_c_im': 'new_v', 'new_v_ssm_d': 'new_v', 'new_v_w_glu': 'new_v', 'new_v_attn_sinks': 'new_v', 'new_v_w_branch': 'new_v', 'new_v_w_out': 'new_v', 'new_v_ln_g': 'new_v', 'new_v_ln_b': 'new_v', 'new_v_w_ple': 'new_v', 'new_v_w_ple_gate': 'new_v', 'new_v_ple_norm_g': 'new_v'}


def _forward(args):
    return _fwd_reference(*[args[k] for k in FWD_PARAMS])


def _output_shape():
    out = _jax.eval_shape(lambda: _forward(_fwd_setup_inputs(0)))
    return out.shape, out.dtype

N_MICROBATCH = 1
ADAM_LR = 0.001
ADAM_B1 = 0.9
ADAM_B2 = 0.999
ADAM_EPS = 1e-08
ADAM_WD = 0.01
ADAM_STEP = 10
PER_EXAMPLE_BATCH_AXIS = {'x': 0, 'p': 1, 'loss_target': 0}
SHARED_INPUTS = []
_WEIGHT_DTYPES = {'w_in': _jnp.float32, 'w_merge': _jnp.float32, 'b_merge': _jnp.float32, 'conv_w': _jnp.float32, 'conv_b': _jnp.float32, 'conv_norm_g': _jnp.float32, 'conv_norm_b': _jnp.float32, 'w_pw2': _jnp.float32, 'mla_q_norm_g': _jnp.float32, 'mla_kv_norm_g': _jnp.float32, 'w_uq': _jnp.float32, 'w_ukv': _jnp.float32, 'ssm_a_re': _jnp.float32, 'ssm_a_im': _jnp.float32, 'ssm_log_dt': _jnp.float32, 'ssm_b_re': _jnp.float32, 'ssm_b_im': _jnp.float32, 'ssm_c_re': _jnp.float32, 'ssm_c_im': _jnp.float32, 'ssm_d': _jnp.float32, 'w_glu': _jnp.float32, 'attn_sinks': _jnp.float32, 'w_branch': _jnp.float32, 'w_out': _jnp.float32, 'ln_g': _jnp.float32, 'ln_b': _jnp.float32, 'w_ple': _jnp.float32, 'w_ple_gate': _jnp.float32, 'ple_norm_g': _jnp.float32}
MOMENT_SCALE = {'w_in': 8.174666e-03, 'w_merge': 1.725368e-03, 'b_merge': 1.238577e-03, 'conv_w': 9.093775e-03, 'conv_b': 1.306521e-02, 'conv_norm_g': 1.023243e-02, 'conv_norm_b': 8.695845e-03, 'w_pw2': 8.701103e-03, 'mla_q_norm_g': 3.269854e-03, 'mla_kv_norm_g': 6.274722e-03, 'w_uq': 2.612147e-03, 'w_ukv': 3.003150e-03, 'ssm_a_re': 6.332891e-04, 'ssm_a_im': 7.257296e-04, 'ssm_log_dt': 4.116572e-01, 'ssm_b_re': 3.901432e-04, 'ssm_b_im': 3.780454e-04, 'ssm_c_re': 5.597706e-04, 'ssm_c_im': 5.588618e-04, 'ssm_d': 9.691805e-03, 'w_glu': 5.982423e-03, 'attn_sinks': 3.359517e-03, 'w_branch': 8.582473e-03, 'w_out': 1.716156e-02, 'ln_g': 1.594946e+01, 'ln_b': 2.906845e-01, 'w_ple': 1.829421e-01, 'w_ple_gate': 7.108469e-02, 'ple_norm_g': 1.599528e+01}


def _to_microbatches(a, axis):
    t = _jnp.moveaxis(a, axis, 0)
    t = t.reshape((N_MICROBATCH, t.shape[0] // N_MICROBATCH) + t.shape[1:])
    return _jnp.moveaxis(t, 1, axis + 1)


def setup_inputs(seed: int = 0) -> dict:
    inp = _fwd_setup_inputs(seed)
    key = _jax.random.fold_in(_jax.random.key(seed), 7919)
    shape, _ = _output_shape()
    out = dict(inp)
    out["loss_target"] = _jax.random.normal(_jax.random.fold_in(key, 0), shape, _jnp.float32)
    for i, name in enumerate(TWIN_WEIGHTS):
        w = inp[name].astype(_jnp.float32)
        if MOMENT_SCALE is None:
            s = _jnp.sqrt(_jnp.mean(_jnp.square(w)) + 1e-30)
        else:
            s = MOMENT_SCALE[name]
        km, kv = _jax.random.split(_jax.random.fold_in(key, i + 1))
        out[name] = w
        out["m_" + name] = s * _jax.random.normal(km, w.shape, _jnp.float32)
        out["v_" + name] = (s * s) * _jax.random.uniform(kv, w.shape, _jnp.float32, 0.5, 1.5)
    if N_MICROBATCH > 1:
        for name, axis in PER_EXAMPLE_BATCH_AXIS.items():
            out[name] = _to_microbatches(out[name], axis)
    return {'x': out['x'], 'p': out['p'], 'w_in': out['w_in'], 'w_merge': out['w_merge'], 'b_merge': out['b_merge'], 'conv_w': out['conv_w'], 'conv_b': out['conv_b'], 'conv_norm_g': out['conv_norm_g'], 'conv_norm_b': out['conv_norm_b'], 'w_pw2': out['w_pw2'], 'mla_q_norm_g': out['mla_q_norm_g'], 'mla_kv_norm_g': out['mla_kv_norm_g'], 'w_uq': out['w_uq'], 'w_ukv': out['w_ukv'], 'ssm_a_re': out['ssm_a_re'], 'ssm_a_im': out['ssm_a_im'], 'ssm_log_dt': out['ssm_log_dt'], 'ssm_b_re': out['ssm_b_re'], 'ssm_b_im': out['ssm_b_im'], 'ssm_c_re': out['ssm_c_re'], 'ssm_c_im': out['ssm_c_im'], 'ssm_d': out['ssm_d'], 'w_glu': out['w_glu'], 'attn_sinks': out['attn_sinks'], 'w_branch': out['w_branch'], 'w_out': out['w_out'], 'ln_g': out['ln_g'], 'ln_b': out['ln_b'], 'w_ple': out['w_ple'], 'w_ple_gate': out['w_ple_gate'], 'ple_norm_g': out['ple_norm_g'], 'loss_target': out['loss_target'], 'm_w_in': out['m_w_in'], 'm_w_merge': out['m_w_merge'], 'm_b_merge': out['m_b_merge'], 'm_conv_w': out['m_conv_w'], 'm_conv_b': out['m_conv_b'], 'm_conv_norm_g': out['m_conv_norm_g'], 'm_conv_norm_b': out['m_conv_norm_b'], 'm_w_pw2': out['m_w_pw2'], 'm_mla_q_norm_g': out['m_mla_q_norm_g'], 'm_mla_kv_norm_g': out['m_mla_kv_norm_g'], 'm_w_uq': out['m_w_uq'], 'm_w_ukv': out['m_w_ukv'], 'm_ssm_a_re': out['m_ssm_a_re'], 'm_ssm_a_im': out['m_ssm_a_im'], 'm_ssm_log_dt': out['m_ssm_log_dt'], 'm_ssm_b_re': out['m_ssm_b_re'], 'm_ssm_b_im': out['m_ssm_b_im'], 'm_ssm_c_re': out['m_ssm_c_re'], 'm_ssm_c_im': out['m_ssm_c_im'], 'm_ssm_d': out['m_ssm_d'], 'm_w_glu': out['m_w_glu'], 'm_attn_sinks': out['m_attn_sinks'], 'm_w_branch': out['m_w_branch'], 'm_w_out': out['m_w_out'], 'm_ln_g': out['m_ln_g'], 'm_ln_b': out['m_ln_b'], 'm_w_ple': out['m_w_ple'], 'm_w_ple_gate': out['m_w_ple_gate'], 'm_ple_norm_g': out['m_ple_norm_g'], 'v_w_in': out['v_w_in'], 'v_w_merge': out['v_w_merge'], 'v_b_merge': out['v_b_merge'], 'v_conv_w': out['v_conv_w'], 'v_conv_b': out['v_conv_b'], 'v_conv_norm_g': out['v_conv_norm_g'], 'v_conv_norm_b': out['v_conv_norm_b'], 'v_w_pw2': out['v_w_pw2'], 'v_mla_q_norm_g': out['v_mla_q_norm_g'], 'v_mla_kv_norm_g': out['v_mla_kv_norm_g'], 'v_w_uq': out['v_w_uq'], 'v_w_ukv': out['v_w_ukv'], 'v_ssm_a_re': out['v_ssm_a_re'], 'v_ssm_a_im': out['v_ssm_a_im'], 'v_ssm_log_dt': out['v_ssm_log_dt'], 'v_ssm_b_re': out['v_ssm_b_re'], 'v_ssm_b_im': out['v_ssm_b_im'], 'v_ssm_c_re': out['v_ssm_c_re'], 'v_ssm_c_im': out['v_ssm_c_im'], 'v_ssm_d': out['v_ssm_d'], 'v_w_glu': out['v_w_glu'], 'v_attn_sinks': out['v_attn_sinks'], 'v_w_branch': out['v_w_branch'], 'v_w_out': out['v_w_out'], 'v_ln_g': out['v_ln_g'], 'v_ln_b': out['v_ln_b'], 'v_w_ple': out['v_w_ple'], 'v_w_ple_gate': out['v_w_ple_gate'], 'v_ple_norm_g': out['v_ple_norm_g']}


def _loss(weights, diff, rest, loss_target):
    with _jax.named_scope("forward"):
        args = {**rest, TWIN_DIFF_INPUT: diff, **{k: w.astype(_WEIGHT_DTYPES[k]) for k, w in weights.items()}}
        y = _forward(args)
    with _jax.named_scope("loss_head"):
        err = _jnp.square(y.astype(_jnp.float32) - loss_target)
        return 0.5 * _jnp.sum(_jnp.mean(err, axis=-1)) if err.ndim else 0.5 * err


def _adamw(w, g, m, v):
    m = ADAM_B1 * m + (1.0 - ADAM_B1) * g
    v = ADAM_B2 * v + (1.0 - ADAM_B2) * _jnp.square(g)
    m_hat = m / (1.0 - ADAM_B1 ** ADAM_STEP)
    v_hat = v / (1.0 - ADAM_B2 ** ADAM_STEP)
    delta = -ADAM_LR * (m_hat / (_jnp.sqrt(v_hat) + ADAM_EPS) + ADAM_WD * w)
    return delta, m, v


def reference(x, p, w_in, w_merge, b_merge, conv_w, conv_b, conv_norm_g, conv_norm_b, w_pw2, mla_q_norm_g, mla_kv_norm_g, w_uq, w_ukv, ssm_a_re, ssm_a_im, ssm_log_dt, ssm_b_re, ssm_b_im, ssm_c_re, ssm_c_im, ssm_d, w_glu, attn_sinks, w_branch, w_out, ln_g, ln_b, w_ple, w_ple_gate, ple_norm_g, loss_target, m_w_in, m_w_merge, m_b_merge, m_conv_w, m_conv_b, m_conv_norm_g, m_conv_norm_b, m_w_pw2, m_mla_q_norm_g, m_mla_kv_norm_g, m_w_uq, m_w_ukv, m_ssm_a_re, m_ssm_a_im, m_ssm_log_dt, m_ssm_b_re, m_ssm_b_im, m_ssm_c_re, m_ssm_c_im, m_ssm_d, m_w_glu, m_attn_sinks, m_w_branch, m_w_out, m_ln_g, m_ln_b, m_w_ple, m_w_ple_gate, m_ple_norm_g, v_w_in, v_w_merge, v_b_merge, v_conv_w, v_conv_b, v_conv_norm_g, v_conv_norm_b, v_w_pw2, v_mla_q_norm_g, v_mla_kv_norm_g, v_w_uq, v_w_ukv, v_ssm_a_re, v_ssm_a_im, v_ssm_log_dt, v_ssm_b_re, v_ssm_b_im, v_ssm_c_re, v_ssm_c_im, v_ssm_d, v_w_glu, v_attn_sinks, v_w_branch, v_w_out, v_ln_g, v_ln_b, v_w_ple, v_w_ple_gate, v_ple_norm_g):
    given = dict(x=x, p=p, w_in=w_in, w_merge=w_merge, b_merge=b_merge, conv_w=conv_w, conv_b=conv_b, conv_norm_g=conv_norm_g, conv_norm_b=conv_norm_b, w_pw2=w_pw2, mla_q_norm_g=mla_q_norm_g, mla_kv_norm_g=mla_kv_norm_g, w_uq=w_uq, w_ukv=w_ukv, ssm_a_re=ssm_a_re, ssm_a_im=ssm_a_im, ssm_log_dt=ssm_log_dt, ssm_b_re=ssm_b_re, ssm_b_im=ssm_b_im, ssm_c_re=ssm_c_re, ssm_c_im=ssm_c_im, ssm_d=ssm_d, w_glu=w_glu, attn_sinks=attn_sinks, w_branch=w_branch, w_out=w_out, ln_g=ln_g, ln_b=ln_b, w_ple=w_ple, w_ple_gate=w_ple_gate, ple_norm_g=ple_norm_g, loss_target=loss_target, m_w_in=m_w_in, m_w_merge=m_w_merge, m_b_merge=m_b_merge, m_conv_w=m_conv_w, m_conv_b=m_conv_b, m_conv_norm_g=m_conv_norm_g, m_conv_norm_b=m_conv_norm_b, m_w_pw2=m_w_pw2, m_mla_q_norm_g=m_mla_q_norm_g, m_mla_kv_norm_g=m_mla_kv_norm_g, m_w_uq=m_w_uq, m_w_ukv=m_w_ukv, m_ssm_a_re=m_ssm_a_re, m_ssm_a_im=m_ssm_a_im, m_ssm_log_dt=m_ssm_log_dt, m_ssm_b_re=m_ssm_b_re, m_ssm_b_im=m_ssm_b_im, m_ssm_c_re=m_ssm_c_re, m_ssm_c_im=m_ssm_c_im, m_ssm_d=m_ssm_d, m_w_glu=m_w_glu, m_attn_sinks=m_attn_sinks, m_w_branch=m_w_branch, m_w_out=m_w_out, m_ln_g=m_ln_g, m_ln_b=m_ln_b, m_w_ple=m_w_ple, m_w_ple_gate=m_w_ple_gate, m_ple_norm_g=m_ple_norm_g, v_w_in=v_w_in, v_w_merge=v_w_merge, v_b_merge=v_b_merge, v_conv_w=v_conv_w, v_conv_b=v_conv_b, v_conv_norm_g=v_conv_norm_g, v_conv_norm_b=v_conv_norm_b, v_w_pw2=v_w_pw2, v_mla_q_norm_g=v_mla_q_norm_g, v_mla_kv_norm_g=v_mla_kv_norm_g, v_w_uq=v_w_uq, v_w_ukv=v_w_ukv, v_ssm_a_re=v_ssm_a_re, v_ssm_a_im=v_ssm_a_im, v_ssm_log_dt=v_ssm_log_dt, v_ssm_b_re=v_ssm_b_re, v_ssm_b_im=v_ssm_b_im, v_ssm_c_re=v_ssm_c_re, v_ssm_c_im=v_ssm_c_im, v_ssm_d=v_ssm_d, v_w_glu=v_w_glu, v_attn_sinks=v_attn_sinks, v_w_branch=v_w_branch, v_w_out=v_w_out, v_ln_g=v_ln_g, v_ln_b=v_ln_b, v_w_ple=v_w_ple, v_w_ple_gate=v_w_ple_gate, v_ple_norm_g=v_ple_norm_g)
    weights = {n: given[n] for n in TWIN_WEIGHTS}
    shared = {n: given[n] for n in SHARED_INPUTS}
    per_example = {n: given[n] for n in ['x', 'p']}
    grad_fn = _jax.value_and_grad(_loss, argnums=(0, 1))

    def one_microbatch(ex, loss_target):
        ex = dict(ex)
        diff = ex.pop(TWIN_DIFF_INPUT)
        return grad_fn(weights, diff, {**shared, **ex}, loss_target)

    if N_MICROBATCH == 1:
        loss, (grad_w, grad_x) = one_microbatch(per_example, given["loss_target"])
    else:
        def body(carry, xs):
            loss_sum, grad_sum = carry
            l_k, (gw_k, gx_k) = one_microbatch(xs[0], xs[1])
            with _jax.named_scope("update"):
                return (loss_sum + l_k, _jax.tree.map(_jnp.add, grad_sum, gw_k)), gx_k

        init = (_jnp.zeros((), _jnp.float32), _jax.tree.map(_jnp.zeros_like, weights))
        (loss, grad_w), grad_x = _jax.lax.scan(body, init, (per_example, given["loss_target"]))
    with _jax.named_scope("update"):
        delta_w, new_m, new_v = {}, {}, {}
        for n in TWIN_WEIGHTS:
            delta_w[n], new_m[n], new_v[n] = _adamw(weights[n], grad_w[n], given["m_" + n], given["v_" + n])
    return (loss, grad_x, *[grad_w[n] for n in TWIN_WEIGHTS], *[delta_w[n] for n in TWIN_WEIGHTS],
            *[new_m[n] for n in TWIN_WEIGHTS], *[new_v[n] for n in TWIN_WEIGHTS])
```

```python
import functools
import math

import jax
import jax.numpy as jnp
from jax import lax
from jax.experimental import pallas as pl
from jax.experimental.pallas import tpu as pltpu

F32 = jnp.float32
BF16 = jnp.bfloat16
MXU_DTYPE = BF16
V7X_VMEM_BYTES = 64 * 1024 * 1024
VMEM_LIMIT = V7X_VMEM_BYTES * 3 // 4
LANES = 128
SUBLANES = 8

D_MODEL = 1024
DEPTH = 4
BRANCH_W = 256
CONV_W = 31
CONV_HALO = 32
MLA_SCALE = (64 + 32) ** -0.5
SWA_SCALE = 64 ** -0.5
WINDOW = 128
ROPE_THETA = 10000.0
SSM_GROUPS, SSM_GROUP, SSM_STATE = 16, 16, 64
SSM_CH = SSM_GROUPS * SSM_STATE
SCAN_SEGMENTS = SUBLANES
SCAN_CB = 128
DEEPNORM_ALPHA = (2.0 * DEPTH) ** 0.25
LN_EPS = 1e-5
RMS_EPS = 1e-6
ADAM_LR, ADAM_B1, ADAM_B2, ADAM_EPS, ADAM_WD, ADAM_STEP = 0.001, 0.9, 0.999, 1e-08, 0.01, 10
NEG = -1e30
ROW_TILE = 512

NN = (((1,), (0,)), ((), ()))
NT = (((1,), (1,)), ((), ()))
TN = (((0,), (0,)), ((), ()))

MESH = pl.DeviceIdType.MESH
ANY = pl.BlockSpec(memory_space=pl.ANY)


def _dot(a, b, dims):
    return lax.dot_general(a.astype(MXU_DTYPE), b.astype(MXU_DTYPE), dims, preferred_element_type=F32)


def _pick(n, cands):
    for c in cands:
        if n % c == 0:
            return c
    return n


def _params(sem):
    return pltpu.CompilerParams(dimension_semantics=sem, vmem_limit_bytes=VMEM_LIMIT)


def _mm_call(a, b, mode, name):
    if mode == "nn":
        (m, k), (_, n) = a.shape, b.shape
    elif mode == "nt":
        (m, k), (n, _) = a.shape, b.shape
    else:
        (k, m), (_, n) = a.shape, b.shape
    tm = _pick(m, (512, 256, 128))
    tn = _pick(n, (512, 256, 128))
    tk = _pick(k, (1024, 512, 256, 128))
    nk = k // tk
    dims = {"nn": NN, "nt": NT, "tn": TN}[mode]
    a_spec = pl.BlockSpec((tk, tm), lambda i, j, kk: (kk, i)) if mode == "tn" else pl.BlockSpec((tm, tk), lambda i, j, kk: (i, kk))
    b_spec = pl.BlockSpec((tn, tk), lambda i, j, kk: (j, kk)) if mode == "nt" else pl.BlockSpec((tk, tn), lambda i, j, kk: (kk, j))

    def body(a_ref, b_ref, o_ref, acc_ref):
        kk = pl.program_id(2)

        @pl.when(kk == 0)
        def _():
            acc_ref[...] = jnp.zeros_like(acc_ref)

        acc_ref[...] += _dot(a_ref[...], b_ref[...], dims)

        @pl.when(kk == nk - 1)
        def _():
            o_ref[...] = acc_ref[...]

    return pl.pallas_call(
        body, name=name, grid=(m // tm, n // tn, nk),
        in_specs=[a_spec, b_spec], out_specs=pl.BlockSpec((tm, tn), lambda i, j, kk: (i, j)),
        out_shape=jax.ShapeDtypeStruct((m, n), F32),
        scratch_shapes=[pltpu.VMEM((tm, tn), F32)],
        compiler_params=_params(("parallel", "parallel", "arbitrary")),
    )(a, b)


@jax.custom_vjp
def op_mm(a, w):
    return _mm_call(a, w.astype(MXU_DTYPE), "nn", "mm_nn")


def _op_mm_fwd(a, w):
    wb = w.astype(MXU_DTYPE)
    return _mm_call(a, wb, "nn", "mm_nn"), (a, wb)


def _op_mm_bwd(res, g):
    a, wb = res
    return _mm_call(g, wb, "nt", "mm_nt"), _mm_call(a, g, "tn", "mm_tn")


op_mm.defvjp(_op_mm_fwd, _op_mm_bwd)


@jax.custom_vjp
def _mm(a, w):
    return _dot(a, w, NN)


def _mm_f(a, w):
    return _dot(a, w, NN), (a, w)


def _mm_b(res, g):
    a, w = res
    return _dot(g, w, NT), _dot(a, g, TN)


_mm.defvjp(_mm_f, _mm_b)


@functools.partial(jax.custom_vjp, nondiff_argnums=(1,))
def _roll(x, shift):
    return pltpu.roll(x, shift, 1)


def _roll_f(x, shift):
    return pltpu.roll(x, shift, 1), None


def _roll_b(shift, _, g):
    return (pltpu.roll(g, (g.shape[1] - shift) % g.shape[1], 1),)


_roll.defvjp(_roll_f, _roll_b)


def _ln(x, g, b):
    mu = jnp.mean(x, axis=-1, keepdims=True)
    xc = x - mu
    var = jnp.mean(xc * xc, axis=-1, keepdims=True)
    return xc * lax.rsqrt(var + LN_EPS) * g + b


def _rms(x, g):
    ms = jnp.mean(x * x, axis=-1, keepdims=True)
    return x * lax.rsqrt(ms + RMS_EPS) * g


def _sigmoid(x):
    return jax.nn.sigmoid(x)


def _silu(x):
    return x * _sigmoid(x)


def _gelu_tanh(x):
    return x * (0.5 * (1.0 + jnp.tanh(math.sqrt(2.0 / math.pi) * (x + 0.044715 * (x * x * x)))))


def _rowwise_fwd_call(fn, rows, consts, name, tile):
    t = rows[0].shape[0]
    tile = min(tile, t)
    nr = len(rows)
    outs = jax.eval_shape(fn, *[jax.ShapeDtypeStruct((tile, r.shape[1]), F32) for r in rows],
                          *[jax.ShapeDtypeStruct(c.shape, F32) for c in consts])

    def body(*refs):
        vals = [r[...] for r in refs[:nr + len(consts)]]
        res = fn(*vals)
        for o_ref, o in zip(refs[nr + len(consts):], res):
            o_ref[...] = o

    return pl.pallas_call(
        body, name=name, grid=(t // tile,),
        in_specs=[pl.BlockSpec((tile, r.shape[1]), lambda i: (i, 0)) for r in rows]
        + [pl.BlockSpec(c.shape, lambda i: (0, 0)) for c in consts],
        out_specs=[pl.BlockSpec((tile, o.shape[1]), lambda i: (i, 0)) for o in outs],
        out_shape=[jax.ShapeDtypeStruct((t, o.shape[1]), F32) for o in outs],
        compiler_params=_params(("parallel",)),
    )(*rows, *consts)


def _rowwise_bwd_call(fn, rows, consts, douts, row_diff, name, tile):
    t = rows[0].shape[0]
    tile = min(tile, t)
    nr, nc, nd = len(rows), len(consts), len(douts)
    diff_idx = [i for i in range(nr) if row_diff[i]]

    def body(*refs):
        rv = [r[...] for r in refs[:nr]]
        cv = [r[...] for r in refs[nr:nr + nc]]
        dv = [r[...] for r in refs[nr + nc:nr + nc + nd]]
        out_refs = refs[nr + nc + nd:]

        def f(*diff):
            full = list(rv)
            for k, i in enumerate(diff_idx):
                full[i] = diff[k]
            return fn(*full, *diff[len(diff_idx):])

        _, vjp = jax.vjp(f, *[rv[i] for i in diff_idx], *cv)
        grads = vjp(tuple(dv))
        for k in range(len(diff_idx)):
            out_refs[k][...] = grads[k]
        first = pl.program_id(0) == 0
        for k in range(nc):
            acc_ref = out_refs[len(diff_idx) + k]
            g = grads[len(diff_idx) + k]

            @pl.when(first)
            def _(acc_ref=acc_ref, g=g):
                acc_ref[...] = g

            @pl.when(jnp.logical_not(first))
            def _(acc_ref=acc_ref, g=g):
                acc_ref[...] += g

    res = pl.pallas_call(
        body, name=name, grid=(t // tile,),
        in_specs=[pl.BlockSpec((tile, r.shape[1]), lambda i: (i, 0)) for r in rows]
        + [pl.BlockSpec(c.shape, lambda i: (0, 0)) for c in consts]
        + [pl.BlockSpec((tile, d.shape[1]), lambda i: (i, 0)) for d in douts],
        out_specs=[pl.BlockSpec((tile, rows[i].shape[1]), lambda i_: (i_, 0)) for i in diff_idx]
        + [pl.BlockSpec(c.shape, lambda i: (0, 0)) for c in consts],
        out_shape=[jax.ShapeDtypeStruct(rows[i].shape, F32) for i in diff_idx]
        + [jax.ShapeDtypeStruct(c.shape, F32) for c in consts],
        compiler_params=_params(("arbitrary",)),
    )(*rows, *consts, *douts)
    return res[:len(diff_idx)], res[len(diff_idx):]


def make_rowwise(fn, name, row_diff, tile=ROW_TILE):
    @jax.custom_vjp
    def op(rows, consts):
        return tuple(_rowwise_fwd_call(fn, rows, consts, name + "_fwd", tile))

    def fwd(rows, consts):
        return op(rows, consts), (rows, consts)

    def bwd(res, douts):
        rows, consts = res
        drows, dconsts = _rowwise_bwd_call(fn, rows, consts, douts, row_diff, name + "_bwd", tile)
        it = iter(drows)
        full = tuple(next(it) if row_diff[i] else jnp.zeros_like(rows[i]) for i in range(len(rows)))
        return full, tuple(dconsts)

    op.defvjp(fwd, bwd)
    return op


def _conv_post_fn(cv, a_z, ng, nb, w_pw2):
    return (_mm(_silu(_ln(cv, ng, nb)), w_pw2) * _silu(a_z),)


def _mla_prep_fn(c_q, c_kv, krblk, cos4, sin4, qg, kvg, w_uq, w_uk, w_uv):
    qe = _mm(_rms(c_q, qg), w_uq)
    q = qe * cos4 + _roll(qe, qe.shape[1] - 32) * sin4
    cos1, sin1 = cos4[:, :LANES], sin4[:, :LANES]
    kr = krblk * cos1 + _roll(krblk, LANES - 32) * sin1
    kn = _rms(c_kv, kvg)
    k = _mm(kn, w_uk) + jnp.concatenate([kr, kr, kr, kr], axis=1)
    return q, k, _mm(kn, w_uv)


def _ssm_post_fn(y, u, c_z, d, w_a, w_b):
    y2 = _gelu_tanh(y + d * u)
    return (_mm(y2, w_a) * _sigmoid(_mm(y2, w_b)) * _silu(c_z),)


def _gate_fn(o, z):
    return (o * _silu(z),)


def _merge_fn(br0, br1, br2, br3, gl0, gl1, gl2, gl3, b0, b1, b2, b3):
    return (_sigmoid(gl0 + b0) * br0 + _sigmoid(gl1 + b1) * br1 + _sigmoid(gl2 + b2) * br2 + _sigmoid(gl3 + b3) * br3,)


def _ln_fn(x, mo, g, b):
    return (_ln(DEEPNORM_ALPHA * x + mo, g, b),)


def _ple_fn(x1, pe, gl, g):
    return (x1 + _rms(pe * _sigmoid(gl), g),)


op_conv_post = make_rowwise(_conv_post_fn, "conv_post", (True, True))
op_mla_prep = make_rowwise(_mla_prep_fn, "mla_prep", (True, True, True, False, False))
op_ssm_post = make_rowwise(_ssm_post_fn, "ssm_post", (True, True, True))
op_gate = make_rowwise(_gate_fn, "gate", (True, True))
op_merge = make_rowwise(_merge_fn, "merge", (True,) * 8, tile=ROW_TILE // 2)
op_ln = make_rowwise(_ln_fn, "post_ln", (True, True))
op_ple = make_rowwise(_ple_fn, "ple", (True, True, True))


def _conv_fwd_call(a_val, a_gate, w32, b):
    t, w = a_val.shape
    tile = min(ROW_TILE, t)
    per = tile // CONV_HALO
    cur = pl.BlockSpec((tile, w), lambda i: (i, 0))
    prev = pl.BlockSpec((CONV_HALO, w), lambda i: (jnp.maximum(i * per - 1, 0), 0))

    def body(av_ref, avh_ref, ag_ref, agh_ref, w_ref, b_ref, cv_ref, buf):
        i = pl.program_id(0)
        gh = avh_ref[...] * _sigmoid(agh_ref[...])
        buf[0:CONV_HALO, :] = jnp.where(i > 0, gh, 0.0)
        buf[CONV_HALO:, :] = av_ref[...] * _sigmoid(ag_ref[...])
        acc = jnp.zeros((tile, w), F32) + b_ref[...]
        for j in range(CONV_W):
            acc = acc + w_ref[j:j + 1, :] * buf[pl.ds(CONV_HALO - (CONV_W - 1) + j, tile), :]
        cv_ref[...] = acc

    return pl.pallas_call(
        body, name="conv_fwd", grid=(t // tile,),
        in_specs=[cur, prev, cur, prev, pl.BlockSpec((CONV_HALO, w), lambda i: (0, 0)), pl.BlockSpec((1, w), lambda i: (0, 0))],
        out_specs=cur, out_shape=jax.ShapeDtypeStruct((t, w), F32),
        scratch_shapes=[pltpu.VMEM((tile + CONV_HALO, w), F32)],
        compiler_params=_params(("parallel",)),
    )(a_val, a_val, a_gate, a_gate, w32, b)


def _conv_bwd_call(a_val, a_gate, w32, dcv):
    t, w = a_val.shape
    tile = min(ROW_TILE, t)
    n = t // tile
    per = tile // CONV_HALO
    cur = pl.BlockSpec((tile, w), lambda i: (i, 0))
    prev = pl.BlockSpec((CONV_HALO, w), lambda i: (jnp.maximum(i * per - 1, 0), 0))
    nxt = pl.BlockSpec((CONV_HALO, w), lambda i: (jnp.minimum((i + 1) * per, t // CONV_HALO - 1), 0))
    full = lambda r: pl.BlockSpec((r, w), lambda i: (0, 0))

    def body(av_ref, avh_ref, ag_ref, agh_ref, w_ref, d_ref, dn_ref, dav_ref, dag_ref, dw_ref, db_ref, gbuf, dbuf):
        i = pl.program_id(0)
        gh = avh_ref[...] * _sigmoid(agh_ref[...])
        gbuf[0:CONV_HALO, :] = jnp.where(i > 0, gh, 0.0)
        av = av_ref[...]
        sg = _sigmoid(ag_ref[...])
        gbuf[CONV_HALO:, :] = av * sg
        d = d_ref[...]
        dbuf[0:tile, :] = d
        dbuf[tile:, :] = jnp.where(i < n - 1, dn_ref[...], 0.0)

        @pl.when(i == 0)
        def _():
            dw_ref[...] = jnp.zeros_like(dw_ref)
            db_ref[...] = jnp.zeros_like(db_ref)

        dg = jnp.zeros((tile, w), F32)
        for j in range(CONV_W):
            dg = dg + w_ref[j:j + 1, :] * dbuf[pl.ds(CONV_W - 1 - j, tile), :]
            dw_ref[j:j + 1, :] += jnp.sum(d * gbuf[pl.ds(CONV_HALO - (CONV_W - 1) + j, tile), :], axis=0, keepdims=True)
        db_ref[...] += jnp.sum(d, axis=0, keepdims=True)
        dav_ref[...] = dg * sg
        dag_ref[...] = dg * av * sg * (1.0 - sg)

    return pl.pallas_call(
        body, name="conv_bwd", grid=(n,),
        in_specs=[cur, prev, cur, prev, full(CONV_HALO), cur, nxt],
        out_specs=[cur, cur, full(CONV_HALO), full(1)],
        out_shape=[jax.ShapeDtypeStruct((t, w), F32), jax.ShapeDtypeStruct((t, w), F32),
                   jax.ShapeDtypeStruct((CONV_HALO, w), F32), jax.ShapeDtypeStruct((1, w), F32)],
        scratch_shapes=[pltpu.VMEM((tile + CONV_HALO, w), F32), pltpu.VMEM((tile + CONV_HALO, w), F32)],
        compiler_params=_params(("arbitrary",)),
    )(a_val, a_val, a_gate, a_gate, w32, dcv, dcv)


def _pad_taps(conv_w):
    return jnp.concatenate([conv_w, jnp.zeros((CONV_HALO - CONV_W, conv_w.shape[1]), F32)], axis=0)


@jax.custom_vjp
def op_conv(a_val, a_gate, conv_w, conv_b):
    return _conv_fwd_call(a_val, a_gate, _pad_taps(conv_w), conv_b)


def _op_conv_fwd(a_val, a_gate, conv_w, conv_b):
    return op_conv(a_val, a_gate, conv_w, conv_b), (a_val, a_gate, conv_w)


def _op_conv_bwd(res, dcv):
    a_val, a_gate, conv_w = res
    dav, dag, dw, db = _conv_bwd_call(a_val, a_gate, _pad_taps(conv_w), dcv)
    return dav, dag, dw[:CONV_W], db


op_conv.defvjp(_op_conv_fwd, _op_conv_bwd)


def _head_masks(rows):
    lane = lax.broadcasted_iota(jnp.int32, (rows, LANES), 1)
    return lane < 64, lane >= 64


def _head_row(vals, mask):
    return jnp.max(jnp.where(mask, vals, NEG), axis=1, keepdims=True)


def _attn_valid(qpos, kpos, window):
    valid = kpos <= qpos
    if window is not None:
        valid = jnp.logical_and(valid, qpos - kpos < window)
    return valid


def _flash_fwd_call(q, k, v, sink, *, window, shared_k, scale, blk, name):
    t = q.shape[0]
    qw = LANES if shared_k else 2 * LANES
    pairs = v.shape[1] // LANES
    tq = tk = min(blk, t)
    has_sink = sink is not None

    def body(*refs):
        if has_sink:
            q_ref, k_ref, v_ref, s_ref, o_ref, lse_ref = refs
        else:
            q_ref, k_ref, v_ref, o_ref, lse_ref = refs
        i = pl.program_id(1)
        qb = q_ref[...]
        masks = _head_masks(tq)
        row_masks = _head_masks(1)
        qpos = i * tq + lax.broadcasted_iota(jnp.int32, (tq, tk), 0)
        lo = 0 if window is None else jnp.maximum(i * tq - (window - 1), 0) // tk
        hi = i + 1
        if has_sink:
            m_init = [jnp.zeros((tq, 1), F32) + _head_row(s_ref[...], row_masks[h]) for h in range(2)]
            l_init = [jnp.ones((tq, 1), F32)] * 2
        else:
            m_init = [jnp.full((tq, 1), NEG, F32)] * 2
            l_init = [jnp.zeros((tq, 1), F32)] * 2

        def step(j, carry):
            m0, l0, m1, l1, acc = carry
            start = pl.multiple_of(j * tk, tk)
            kb = k_ref[pl.ds(start, tk), :]
            vb = v_ref[pl.ds(start, tk), :]
            valid = _attn_valid(qpos, j * tk + lax.broadcasted_iota(jnp.int32, (tq, tk), 1), window)
            new, alphas, pv = [], [], []
            for h, (m, l) in enumerate(((m0, l0), (m1, l1))):
                qh = jnp.where(masks[h], qb, 0.0) if shared_k else qb[:, h * LANES:(h + 1) * LANES]
                kh = kb if shared_k else kb[:, h * LANES:(h + 1) * LANES]
                s = jnp.where(valid, _dot(qh, kh, NT) * scale, NEG)
                m_new = jnp.maximum(m, jnp.max(s, axis=1, keepdims=True))
                alpha = jnp.exp(m - m_new)
                p = jnp.exp(s - m_new)
                new += [m_new, alpha * l + jnp.sum(p, axis=1, keepdims=True)]
                alphas.append(alpha)
                pv.append(_dot(p, jnp.where(masks[h], vb, 0.0), NN))
            acc = acc * jnp.where(masks[0], alphas[0], alphas[1]) + pv[0] + pv[1]
            return new[0], new[1], new[2], new[3], acc

        m0, l0, m1, l1, acc = lax.fori_loop(lo, hi, step, (m_init[0], l_init[0], m_init[1], l_init[1], jnp.zeros((tq, LANES), F32)))
        o_ref[...] = acc / jnp.where(masks[0], l0, l1)
        lse_ref[...] = jnp.where(masks[0], m0 + jnp.log(l0), m1 + jnp.log(l1))

    in_specs = [pl.BlockSpec((tq, qw), lambda p, i: (i, p)), pl.BlockSpec((t, qw), lambda p, i: (0, p)),
                pl.BlockSpec((t, LANES), lambda p, i: (0, p))]
    args = [q, k, v]
    if has_sink:
        in_specs.append(pl.BlockSpec((1, LANES), lambda p, i: (0, p)))
        args.append(sink)
    blk_o = pl.BlockSpec((tq, LANES), lambda p, i: (i, p))
    return pl.pallas_call(
        body, name=name, grid=(pairs, t // tq), in_specs=in_specs, out_specs=[blk_o, blk_o],
        out_shape=[jax.ShapeDtypeStruct((t, pairs * LANES), F32)] * 2,
        compiler_params=_params(("parallel", "arbitrary")),
    )(*args)


def _flash_bwd_call(q, k, v, sink, o, lse, do, *, window, shared_k, scale, blk, name):
    t = q.shape[0]
    qw = LANES if shared_k else 2 * LANES
    pairs = v.shape[1] // LANES
    tq = tk = min(blk, t)
    nq = t // tq
    has_sink = sink is not None

    def body(*refs):
        if has_sink:
            q_ref, k_ref, v_ref, o_ref, lse_ref, do_ref, s_ref, dq_ref, dk_ref, dv_ref, ds_ref = refs
        else:
            q_ref, k_ref, v_ref, o_ref, lse_ref, do_ref, dq_ref, dk_ref, dv_ref = refs
        j = pl.program_id(1)
        masks = _head_masks(tq)
        row_masks = _head_masks(1)

        @pl.when(j == 0)
        def _():
            dq_ref[...] = jnp.zeros_like(dq_ref)
            if has_sink:
                full_masks = _head_masks(t)
                prod = do_ref[...] * o_ref[...]
                parts = []
                for h in range(2):
                    dsum = jnp.sum(jnp.where(full_masks[h], prod, 0.0), axis=1, keepdims=True)
                    ps = jnp.exp(_head_row(s_ref[...], row_masks[h]) - _head_row(lse_ref[...], full_masks[h]))
                    parts.append(-jnp.sum(ps * dsum, axis=0, keepdims=True))
                ds_ref[...] = jnp.zeros((SUBLANES, LANES), F32) + jnp.where(row_masks[0], parts[0], parts[1])

        kb = k_ref[...]
        vb = v_ref[...]
        kpos = j * tk + lax.broadcasted_iota(jnp.int32, (tq, tk), 1)
        hi = nq if window is None else jnp.minimum(nq, (j * tk + tk - 1 + window - 1) // tq + 1)

        def step(i, carry):
            dk0, dk1, dv = carry
            start = pl.multiple_of(i * tq, tq)
            qb = q_ref[pl.ds(start, tq), :]
            dob = do_ref[pl.ds(start, tq), :]
            ob = o_ref[pl.ds(start, tq), :]
            lseb = lse_ref[pl.ds(start, tq), :]
            valid = _attn_valid(i * tq + lax.broadcasted_iota(jnp.int32, (tq, tk), 0), kpos, window)
            dks, dqs = [], []
            for h in range(2):
                qh = jnp.where(masks[h], qb, 0.0) if shared_k else qb[:, h * LANES:(h + 1) * LANES]
                kh = kb if shared_k else kb[:, h * LANES:(h + 1) * LANES]
                s = _dot(qh, kh, NT) * scale
                p = jnp.where(valid, jnp.exp(s - _head_row(lseb, masks[h])), 0.0)
                doh = jnp.where(masks[h], dob, 0.0)
                dsum = jnp.sum(doh * ob, axis=1, keepdims=True)
                dp = _dot(doh, vb, NT)
                dsc = p * (dp - dsum) * scale
                dv = dv + _dot(p, doh, TN)
                dks.append(_dot(dsc, qh, TN))
                dq_h = _dot(dsc, kh, NN)
                dqs.append(jnp.where(masks[h], dq_h, 0.0) if shared_k else dq_h)
            if shared_k:
                dq_ref[pl.ds(start, tq), :] += dqs[0] + dqs[1]
            else:
                dq_ref[pl.ds(start, tq), :] += jnp.concatenate(dqs, axis=1)
            return dk0 + dks[0], dk1 + dks[1], dv

        zero = jnp.zeros((tk, LANES), F32)
        dk0, dk1, dv = lax.fori_loop(j, hi, step, (zero, zero, zero))
        dk_ref[...] = dk0 + dk1 if shared_k else jnp.concatenate([dk0, dk1], axis=1)
        dv_ref[...] = dv

    full = lambda w: pl.BlockSpec((t, w), lambda p, j: (0, p))
    blkspec = lambda w: pl.BlockSpec((tk, w), lambda p, j: (j, p))
    in_specs = [full(qw), blkspec(qw), blkspec(LANES), full(LANES), full(LANES), full(LANES)]
    args = [q, k, v, o, lse, do]
    out_specs = [full(qw), blkspec(qw), blkspec(LANES)]
    out_shape = [jax.ShapeDtypeStruct(q.shape, F32), jax.ShapeDtypeStruct(k.shape, F32), jax.ShapeDtypeStruct(v.shape, F32)]
    if has_sink:
        in_specs.append(pl.BlockSpec((1, LANES), lambda p, j: (0, p)))
        args.append(sink)
        out_specs.append(pl.BlockSpec((SUBLANES, LANES), lambda p, j: (0, p)))
        out_shape.append(jax.ShapeDtypeStruct((SUBLANES, pairs * LANES), F32))
    return pl.pallas_call(
        body, name=name, grid=(pairs, t // tk), in_specs=in_specs, out_specs=out_specs, out_shape=out_shape,
        compiler_params=_params(("arbitrary", "arbitrary")),
    )(*args)


_MLA_CFG = dict(window=None, shared_k=False, scale=MLA_SCALE, blk=256)
_SWA_CFG = dict(window=WINDOW, shared_k=True, scale=SWA_SCALE, blk=128)


@jax.custom_vjp
def op_mla_attn(q, k, v):
    return _flash_fwd_call(q, k, v, None, name="mla_fwd", **_MLA_CFG)[0]


def _op_mla_attn_fwd(q, k, v):
    o, lse = _flash_fwd_call(q, k, v, None, name="mla_fwd", **_MLA_CFG)
    return o, (q, k, v, o, lse)


def _op_mla_attn_bwd(res, do):
    q, k, v, o, lse = res
    return tuple(_flash_bwd_call(q, k, v, None, o, lse, do, name="mla_bwd", **_MLA_CFG))


op_mla_attn.defvjp(_op_mla_attn_fwd, _op_mla_attn_bwd)


@jax.custom_vjp
def op_swa_attn(q, k, v, sink):
    return _flash_fwd_call(q, k, v, sink, name="swa_fwd", **_SWA_CFG)[0]


def _op_swa_attn_fwd(q, k, v, sink):
    o, lse = _flash_fwd_call(q, k, v, sink, name="swa_fwd", **_SWA_CFG)
    return o, (q, k, v, sink, o, lse)


def _op_swa_attn_bwd(res, do):
    q, k, v, sink, o, lse = res
    dq, dk, dv, dsink = _flash_bwd_call(q, k, v, sink, o, lse, do, name="swa_bwd", **_SWA_CFG)
    first_lane = lax.broadcasted_iota(jnp.int32, (1, dsink.shape[1]), 1) % 64 == 0
    return dq, dk, dv, jnp.where(first_lane, dsink[:1], 0.0)


op_swa_attn.defvjp(_op_swa_attn_fwd, _op_swa_attn_bwd)


def _complex_power(ar, ai, n):
    for _ in range(int(math.log2(n))):
        ar, ai = ar * ar - ai * ai, 2.0 * ar * ai
    return ar, ai


def _scan_passes(load_b, a1r, a1i, n, store, e_ref, c_ref, reverse):
    cb = a1r.shape[1]
    ar = jnp.zeros((SCAN_SEGMENTS, cb), F32) + a1r
    ai = jnp.zeros((SCAN_SEGMENTS, cb), F32) + a1i
    idx = (lambda ii: n - 1 - ii) if reverse else (lambda ii: ii)

    def local(ii, h):
        br, bi = load_b(idx(ii))
        return ar * h[0] - ai * h[1] + br, ar * h[1] + ai * h[0] + bi

    zero = jnp.zeros((SCAN_SEGMENTS, cb), F32)
    er, ei = lax.fori_loop(0, n, local, (zero, zero))
    e_ref[:, 0:cb] = er
    e_ref[:, cb:] = ei
    pr, pi_ = _complex_power(a1r, a1i, n)
    cr = jnp.zeros((1, cb), F32)
    ci = jnp.zeros((1, cb), F32)
    order = range(SCAN_SEGMENTS - 1, -1, -1) if reverse else range(SCAN_SEGMENTS)
    for s in order:
        c_ref[s:s + 1, 0:cb] = cr
        c_ref[s:s + 1, cb:] = ci
        er1, ei1 = e_ref[s:s + 1, 0:cb], e_ref[s:s + 1, cb:]
        cr, ci = pr * cr - pi_ * ci + er1, pr * ci + pi_ * cr + ei1

    def second(ii, h):
        i = idx(ii)
        hr, hi = local(ii, h)
        store(i, hr, hi)
        return hr, hi

    lax.fori_loop(0, n, second, (c_ref[:, 0:cb], c_ref[:, cb:]))


def _scan_fwd_call(bu, lam):
    n = bu.shape[0]
    cb = SCAN_CB
    blk3 = pl.BlockSpec((n, SCAN_SEGMENTS, 2 * cb), lambda c: (0, 0, c))
    blk2 = lambda r: pl.BlockSpec((r, 2 * cb), lambda c: (0, c))

    def body(b_ref, lam_ref, h_ref, cin_ref, e_ref):
        def store(i, hr, hi):
            h_ref[i, :, 0:cb] = hr
            h_ref[i, :, cb:] = hi

        _scan_passes(lambda i: (b_ref[i, :, 0:cb], b_ref[i, :, cb:]), lam_ref[:, 0:cb], lam_ref[:, cb:], n, store,
                     e_ref, cin_ref, False)

    return pl.pallas_call(
        body, name="scan_fwd", grid=(SSM_CH // cb,), in_specs=[blk3, blk2(1)], out_specs=[blk3, blk2(SCAN_SEGMENTS)],
        out_shape=[jax.ShapeDtypeStruct(bu.shape, F32), jax.ShapeDtypeStruct((SCAN_SEGMENTS, 2 * SSM_CH), F32)],
        scratch_shapes=[pltpu.VMEM((SCAN_SEGMENTS, 2 * cb), F32)],
        compiler_params=_params(("parallel",)),
    )(bu, lam)


def _scan_bwd_call(dh, h, cin, lam):
    n = dh.shape[0]
    cb = SCAN_CB
    blk3 = pl.BlockSpec((n, SCAN_SEGMENTS, 2 * cb), lambda c: (0, 0, c))
    blk2 = lambda r: pl.BlockSpec((r, 2 * cb), lambda c: (0, c))

    def body(d_ref, h_ref, cin_ref, lam_ref, g_ref, dlam_ref, e_ref, c_ref, acc_ref):
        acc_ref[...] = jnp.zeros_like(acc_ref)

        def store(i, gr, gi):
            g_ref[i, :, 0:cb] = gr
            g_ref[i, :, cb:] = gi
            ip = jnp.maximum(i - 1, 0)
            hpr = jnp.where(i > 0, h_ref[ip, :, 0:cb], cin_ref[:, 0:cb])
            hpi = jnp.where(i > 0, h_ref[ip, :, cb:], cin_ref[:, cb:])
            acc_ref[:, 0:cb] += gr * hpr + gi * hpi
            acc_ref[:, cb:] += gi * hpr - gr * hpi

        _scan_passes(lambda i: (d_ref[i, :, 0:cb], d_ref[i, :, cb:]), lam_ref[:, 0:cb], -lam_ref[:, cb:], n, store,
                     e_ref, c_ref, True)
        dlam_ref[...] = acc_ref[...]

    return pl.pallas_call(
        body, name="scan_bwd", grid=(SSM_CH // cb,), in_specs=[blk3, blk3, blk2(SCAN_SEGMENTS), blk2(1)],
        out_specs=[blk3, blk2(SCAN_SEGMENTS)],
        out_shape=[jax.ShapeDtypeStruct(dh.shape, F32), jax.ShapeDtypeStruct((SCAN_SEGMENTS, 2 * SSM_CH), F32)],
        scratch_shapes=[pltpu.VMEM((SCAN_SEGMENTS, 2 * cb), F32)] * 3,
        compiler_params=_params(("parallel",)),
    )(dh, h, cin, lam)


@jax.custom_vjp
def op_scan(bu, lam):
    return _scan_fwd_call(bu, lam)[0]


def _op_scan_fwd(bu, lam):
    h, cin = _scan_fwd_call(bu, lam)
    return h, (h, cin, lam)


def _op_scan_bwd(res, dh):
    h, cin, lam = res
    g, dlam = _scan_bwd_call(dh, h, cin, lam)
    return g, jnp.sum(dlam, axis=0, keepdims=True)


op_scan.defvjp(_op_scan_fwd, _op_scan_bwd)


def _loss_call(y, target):
    t, d = y.shape
    tile = min(ROW_TILE, t)

    def body(y_ref, t_ref, dy_ref, acc_ref):
        @pl.when(pl.program_id(0) == 0)
        def _():
            acc_ref[...] = jnp.zeros_like(acc_ref)

        err = y_ref[...] - t_ref[...]
        dy_ref[...] = err * (1.0 / d)
        col = jnp.sum(err * err, axis=0, keepdims=True)
        part = col[:, 0:LANES]
        for c in range(1, d // LANES):
            part = part + col[:, c * LANES:(c + 1) * LANES]
        acc_ref[0:1, :] += part

    blk = pl.BlockSpec((tile, d), lambda i: (i, 0))
    dy, acc = pl.pallas_call(
        body, name="loss_head", grid=(t // tile,), in_specs=[blk, blk],
        out_specs=[blk, pl.BlockSpec((SUBLANES, LANES), lambda i: (0, 0))],
        out_shape=[jax.ShapeDtypeStruct((t, d), F32), jax.ShapeDtypeStruct((SUBLANES, LANES), F32)],
        compiler_params=_params(("arbitrary",)),
    )(y, target)
    return jnp.sum(acc) * (0.5 / d), dy


def _rot_cols(w):
    return jnp.concatenate([-w[:, 16:], w[:, :16]], axis=1)


def _ext_w_in(w):
    a_val, a_gate, a_z, c_q, c_kv, k_r, b_z, u, c_z, q, k, v, d_z = jnp.split(
        w, (256, 512, 768, 1024, 1152, 1184, 1440, 1696, 1952, 2208, 2336, 2464), axis=1)
    dup = lambda m: jnp.concatenate([m[:, :64], m[:, :64], m[:, 64:], m[:, 64:]], axis=1)
    krblk = jnp.concatenate([jnp.zeros((w.shape[0], 64), w.dtype), k_r, _rot_cols(k_r)], axis=1)
    return jnp.concatenate([a_val, a_gate, a_z, c_q, b_z, u, c_z, q, dup(k), dup(v), d_z, c_kv, krblk], axis=1)


H_COLS = dict(a_val=(0, 256), a_gate=(256, 512), a_z=(512, 768), c_q=(768, 1024), b_z=(1024, 1280), u=(1280, 1536),
              c_z=(1536, 1792), q=(1792, 2048), kdup=(2048, 2304), vdup=(2304, 2560), d_z=(2560, 2816),
              c_kv=(2816, 2944), krblk=(2944, 3072))


def _ext_mla(w_uq, w_ukv):
    zeros = jnp.zeros((w_ukv.shape[0], 64), w_ukv.dtype)
    uq, uk, uv = [], [], []
    for h in range(4):
        nope, rp = w_uq[:, 96 * h:96 * h + 64], w_uq[:, 96 * h + 64:96 * h + 96]
        uq += [nope, rp, _rot_cols(rp)]
        uk += [w_ukv[:, 128 * h:128 * h + 64], zeros]
        uv.append(w_ukv[:, 128 * h + 64:128 * h + 128])
    return jnp.concatenate(uq, axis=1), jnp.concatenate(uk, axis=1), jnp.concatenate(uv, axis=1)


def _scan_cols(re, im):
    parts = []
    for c in range(SSM_CH // SCAN_CB):
        parts += [re[..., c * SCAN_CB:(c + 1) * SCAN_CB], im[..., c * SCAN_CB:(c + 1) * SCAN_CB]]
    return jnp.concatenate(parts, axis=-1)


def _ext_ssm(a_re, a_im, log_dt, b_re, b_im, c_re, c_im):
    dt = jnp.exp(log_dt)[:, None]
    mag = jnp.exp(a_re * dt)
    lb_re, lb_im = mag * jnp.cos(a_im * dt), mag * jnp.sin(a_im * dt)
    den = a_re * a_re + a_im * a_im
    nr, ni = lb_re - 1.0, lb_im
    f_re = ((nr * a_re + ni * a_im) / den)[..., None]
    f_im = ((ni * a_re - nr * a_im) / den)[..., None]
    bb_re = f_re * b_re - f_im * b_im
    bb_im = f_re * b_im + f_im * b_re
    eye = jnp.eye(SSM_GROUPS, dtype=F32)
    spread = lambda a: a.transpose(0, 2, 1)[:, :, None, :] * eye[:, None, :, None]
    bd_in = lambda bb: spread(bb).reshape(SSM_GROUPS * SSM_GROUP, SSM_CH)
    bd_out = lambda cc: spread(cc).reshape(SSM_CH, SSM_GROUPS * SSM_GROUP)
    w_bu = _scan_cols(bd_in(bb_re), bd_in(bb_im))
    w_y = _scan_cols(bd_out(c_re).T, -bd_out(c_im).T).T
    lam = _scan_cols(lb_re.reshape(1, SSM_CH), lb_im.reshape(1, SSM_CH))
    return w_bu, w_y, lam


def _rope_tables(t):
    pos = jnp.arange(t, dtype=F32)
    inv_freq = ROPE_THETA ** (-jnp.arange(0, 32, 2, dtype=F32) / 32)
    ang = pos[:, None] * inv_freq[None, :]
    cos, sin = jnp.cos(ang), jnp.sin(ang)
    ones, z32, z64 = jnp.ones((t, 64), F32), jnp.zeros((t, 32), F32), jnp.zeros((t, 64), F32)
    cos1 = jnp.concatenate([ones, cos, cos, z32], axis=1)
    sin1 = jnp.concatenate([z64, sin, sin, z32], axis=1)
    return jnp.concatenate([cos1] * 4, axis=1), jnp.concatenate([sin1] * 4, axis=1)


def _to_segments(a):
    t, w = a.shape
    return a.reshape(SCAN_SEGMENTS, t // SCAN_SEGMENTS, w).transpose(1, 0, 2)


def _from_segments(a):
    n, s, w = a.shape
    return a.transpose(1, 0, 2).reshape(n * s, w)


def _layer(x, p_i, cos4, sin4, w):
    t = x.shape[0]
    row = lambda v: v.reshape(1, -1)
    h = op_mm(x, _ext_w_in(w["w_in"]))
    hs = {k: h[:, a:b] for k, (a, b) in H_COLS.items()}

    cv = op_conv(hs["a_val"], hs["a_gate"], w["conv_w"], row(w["conv_b"]))
    (y_a,) = op_conv_post((cv, hs["a_z"]), (row(w["conv_norm_g"]), row(w["conv_norm_b"]), w["w_pw2"]))

    w_uq, w_uk, w_uv = _ext_mla(w["w_uq"], w["w_ukv"])
    q, k, v = op_mla_prep((hs["c_q"], hs["c_kv"], hs["krblk"], cos4, sin4),
                          (row(w["mla_q_norm_g"]), row(w["mla_kv_norm_g"]), w_uq, w_uk, w_uv))
    (y_b,) = op_gate((op_mla_attn(q, k, v), hs["b_z"]), ())

    w_bu, w_y, lam = _ext_ssm(w["ssm_a_re"], w["ssm_a_im"], w["ssm_log_dt"], w["ssm_b_re"], w["ssm_b_im"],
                              w["ssm_c_re"], w["ssm_c_im"])
    u_seg = _to_segments(hs["u"]).reshape(t, BRANCH_W)
    bu = op_mm(u_seg, w_bu).reshape(t // SCAN_SEGMENTS, SCAN_SEGMENTS, 2 * SSM_CH)
    hstate = op_scan(bu, lam).reshape(t, 2 * SSM_CH)
    y_ssm = _from_segments(op_mm(hstate, w_y).reshape(t // SCAN_SEGMENTS, SCAN_SEGMENTS, BRANCH_W))
    (y_c,) = op_ssm_post((y_ssm, hs["u"], hs["c_z"]), (row(w["ssm_d"]), w["w_glu"][:, :BRANCH_W], w["w_glu"][:, BRANCH_W:]))

    sink = jnp.repeat(w["attn_sinks"], 64).reshape(1, 2 * LANES)
    (y_d,) = op_gate((op_swa_attn(hs["q"], hs["kdup"], hs["vdup"], sink), hs["d_z"]), ())

    br = [op_mm(y, w["w_branch"][n]) for n, y in enumerate((y_a, y_b, y_c, y_d))]
    gl = [op_mm(x, w["w_merge"][:, n * D_MODEL:(n + 1) * D_MODEL]) for n in range(4)]
    bm = [row(w["b_merge"][n * D_MODEL:(n + 1) * D_MODEL]) for n in range(4)]
    (merged,) = op_merge((*br, *gl), tuple(bm))
    (x1,) = op_ln((x, op_mm(merged, w["w_out"])), (row(w["ln_g"]), row(w["ln_b"])))
    (out,) = op_ple((x1, op_mm(p_i, w["w_ple"]), op_mm(x1, w["w_ple_gate"])), (row(w["ple_norm_g"]),))
    return out


def _forward(x, p, weights):
    cos4, sin4 = _rope_tables(x.shape[0])
    for i in range(DEPTH):
        x = _layer(x, p[i], cos4, sin4, {k: v[i] for k, v in weights.items()})
    return x


SHARD_AXIS = dict(w_in=2, w_merge=2, conv_w=2, w_pw2=1, w_uq=2, w_ukv=2, w_glu=2, w_branch=3, w_out=1, w_ple=2, w_ple_gate=1)
SHARDED = tuple(SHARD_AXIS)
REPLICATED = ("b_merge", "conv_b", "conv_norm_g", "conv_norm_b", "mla_q_norm_g", "mla_kv_norm_g", "ssm_a_re", "ssm_a_im",
              "ssm_log_dt", "ssm_b_re", "ssm_b_im", "ssm_c_re", "ssm_c_im", "ssm_d", "attn_sinks", "ln_g", "ln_b", "ple_norm_g")
WEIGHTS = ("w_in", "w_merge", "b_merge", "conv_w", "conv_b", "conv_norm_g", "conv_norm_b", "w_pw2", "mla_q_norm_g",
           "mla_kv_norm_g", "w_uq", "w_ukv", "ssm_a_re", "ssm_a_im", "ssm_log_dt", "ssm_b_re", "ssm_b_im", "ssm_c_re",
           "ssm_c_im", "ssm_d", "w_glu", "attn_sinks", "w_branch", "w_out", "ln_g", "ln_b", "w_ple", "w_ple_gate", "ple_norm_g")
PACK_COLS = 1024
PACK_ROWS = 512
CHIP_FLIPS = ((1, 0), (0, 1), (1, 1))
N_CHIPS = 4
N_DEV = 8


def _pack(arrays, dtype):
    flat = jnp.concatenate([a.reshape(-1).astype(dtype) for a in arrays])
    unit = PACK_ROWS * PACK_COLS
    total = -(-flat.shape[0] // unit) * unit
    return jnp.concatenate([flat, jnp.zeros((total - flat.shape[0],), dtype)]).reshape(-1, PACK_COLS)


def _unpack(buf, shapes):
    flat = buf.reshape(-1)
    out, off = [], 0
    for s in shapes:
        n = math.prod(s)
        out.append(flat[off:off + n].reshape(s))
        off += n
    return out


def _flip(v, bit):
    return 1 - v if bit else v


def _all_gather_chips(bufs):
    nb = len(bufs)

    def body(*refs):
        ins, outs = refs[:nb], refs[nb:2 * nb]
        send_sems, recv_sems, local_sems = refs[2 * nb:]
        x, y, c = lax.axis_index("x"), lax.axis_index("y"), lax.axis_index("c")
        me = 2 * x + y
        local = [pltpu.make_async_copy(ins[k], outs[k].at[me], local_sems.at[k]) for k in range(nb)]
        for cp in local:
            cp.start()
        sends = []
        for j, (bx, by) in enumerate(CHIP_FLIPS):
            px, py = _flip(x, bx), _flip(y, by)
            for k in range(nb):
                cp = pltpu.make_async_remote_copy(src_ref=ins[k], dst_ref=outs[k].at[me], send_sem=send_sems.at[j * nb + k],
                                                  recv_sem=recv_sems.at[j * nb + k], device_id=(px, py, c), device_id_type=MESH)
                cp.start()
                sends.append(cp)
        for j, (bx, by) in enumerate(CHIP_FLIPS):
            src = 2 * _flip(x, bx) + _flip(y, by)
            for k in range(nb):
                pltpu.make_async_remote_copy(src_ref=ins[k], dst_ref=outs[k].at[src], send_sem=send_sems.at[j * nb + k],
                                             recv_sem=recv_sems.at[j * nb + k], device_id=(x, y, c), device_id_type=MESH).wait_recv()
        for cp in sends:
            cp.wait_send()
        for cp in local:
            cp.wait()

    return pl.pallas_call(
        body, name="gather_weights", in_specs=[ANY] * nb, out_specs=[ANY] * nb,
        out_shape=[jax.ShapeDtypeStruct((N_CHIPS, *b.shape), b.dtype) for b in bufs],
        scratch_shapes=[pltpu.SemaphoreType.DMA((3 * nb,)), pltpu.SemaphoreType.DMA((3 * nb,)), pltpu.SemaphoreType.DMA((nb,))],
    )(*bufs)


def _exchange_grads(big, small):
    dev_flips = [(bx, by, bc) for bx in (0, 1) for by in (0, 1) for bc in (0, 1)][1:]

    def body(big_ref, small_ref, recv_ref, all_ref, send_sems, recv_sems, local_sem):
        x, y, c = lax.axis_index("x"), lax.axis_index("y"), lax.axis_index("c")
        me = 4 * x + 2 * y + c
        local = pltpu.make_async_copy(small_ref, all_ref.at[me], local_sem)
        local.start()
        sends = []
        for j, (bx, by) in enumerate(CHIP_FLIPS):
            px, py = _flip(x, bx), _flip(y, by)
            cp = pltpu.make_async_remote_copy(src_ref=big_ref.at[2 * px + py], dst_ref=recv_ref.at[j], send_sem=send_sems.at[j],
                                              recv_sem=recv_sems.at[j], device_id=(px, py, c), device_id_type=MESH)
            cp.start()
            sends.append(cp)
        for j, (bx, by, bc) in enumerate(dev_flips):
            cp = pltpu.make_async_remote_copy(src_ref=small_ref, dst_ref=all_ref.at[me], send_sem=send_sems.at[3 + j],
                                              recv_sem=recv_sems.at[3 + j], device_id=(_flip(x, bx), _flip(y, by), _flip(c, bc)),
                                              device_id_type=MESH)
            cp.start()
            sends.append(cp)
        for j in range(3):
            pltpu.make_async_remote_copy(src_ref=big_ref.at[0], dst_ref=recv_ref.at[j], send_sem=send_sems.at[j],
                                         recv_sem=recv_sems.at[j], device_id=(x, y, c), device_id_type=MESH).wait_recv()
        for j, (bx, by, bc) in enumerate(dev_flips):
            src = 4 * _flip(x, bx) + 2 * _flip(y, by) + _flip(c, bc)
            pltpu.make_async_remote_copy(src_ref=small_ref, dst_ref=all_ref.at[src], send_sem=send_sems.at[3 + j],
                                         recv_sem=recv_sems.at[3 + j], device_id=(x, y, c), device_id_type=MESH).wait_recv()
        for cp in sends:
            cp.wait_send()
        local.wait()

    return pl.pallas_call(
        body, name="exchange_grads", in_specs=[ANY, ANY], out_specs=[ANY, ANY],
        out_shape=[jax.ShapeDtypeStruct((3, *big.shape[1:]), big.dtype), jax.ShapeDtypeStruct((N_DEV, *small.shape), small.dtype)],
        scratch_shapes=[pltpu.SemaphoreType.DMA((10,)), pltpu.SemaphoreType.DMA((10,)), pltpu.SemaphoreType.DMA(())],
    )(big, small)


def _swap_cores(part):
    def body(p_ref, o_ref, send_sem, recv_sem):
        x, y, c = lax.axis_index("x"), lax.axis_index("y"), lax.axis_index("c")
        cp = pltpu.make_async_remote_copy(src_ref=p_ref, dst_ref=o_ref, send_sem=send_sem, recv_sem=recv_sem,
                                          device_id=(x, y, 1 - c), device_id_type=MESH)
        cp.start()
        cp.wait()

    return pl.pallas_call(
        body, name="swap_cores", in_specs=[ANY], out_specs=ANY, out_shape=jax.ShapeDtypeStruct(part.shape, part.dtype),
        scratch_shapes=[pltpu.SemaphoreType.DMA(()), pltpu.SemaphoreType.DMA(())],
    )(part)


def _sum_chips_call(own, recv):
    r, c = own.shape

    def body(own_ref, recv_ref, o_ref):
        o_ref[...] = ((own_ref[...] + recv_ref[0].astype(F32)) + recv_ref[1].astype(F32)) + recv_ref[2].astype(F32)

    return pl.pallas_call(
        body, name="sum_chips", grid=(r // PACK_ROWS,),
        in_specs=[pl.BlockSpec((PACK_ROWS, c), lambda i: (i, 0)), pl.BlockSpec((3, PACK_ROWS, c), lambda i: (0, i, 0))],
        out_specs=pl.BlockSpec((PACK_ROWS, c), lambda i: (i, 0)), out_shape=jax.ShapeDtypeStruct((r, c), F32),
        compiler_params=_params(("parallel",)),
    )(own, recv)


def _adamw_math(w, g, m, v):
    m = ADAM_B1 * m + (1.0 - ADAM_B1) * g
    v = ADAM_B2 * v + (1.0 - ADAM_B2) * (g * g)
    m_hat = m / (1.0 - ADAM_B1 ** ADAM_STEP)
    v_hat = v / (1.0 - ADAM_B2 ** ADAM_STEP)
    return -ADAM_LR * (m_hat / (jnp.sqrt(v_hat) + ADAM_EPS) + ADAM_WD * w), m, v


def _adamw_call(w, m, v, gparts, name):
    r, c = w.shape
    n = gparts.shape[0]
    tile = _pick(r, (512, 256, 128, 64, 32, 16, 8))

    def body(w_ref, m_ref, v_ref, g_ref, go_ref, d_ref, mo_ref, vo_ref):
        g = g_ref[0]
        for s in range(1, n):
            g = g + g_ref[s]
        go_ref[...] = g
        d_ref[...], mo_ref[...], vo_ref[...] = _adamw_math(w_ref[...], g, m_ref[...], v_ref[...])

    blk = pl.BlockSpec((tile, c), lambda i: (i, 0))
    return pl.pallas_call(
        body, name=name, grid=(r // tile,), in_specs=[blk, blk, blk, pl.BlockSpec((n, tile, c), lambda i: (0, i, 0))],
        out_specs=[blk] * 4, out_shape=[jax.ShapeDtypeStruct((r, c), F32)] * 4,
        compiler_params=_params(("parallel",)),
    )(w, m, v, gparts)


def _shard_slice(a, axis, d):
    n = a.shape[axis] // N_CHIPS
    return lax.slice_in_dim(a, d * n, (d + 1) * n, axis=axis)


def _train_local(x, p, weights, target):
    y, vjp = jax.vjp(lambda x_, w_: _forward(x_, p, w_), x, weights)
    loss, dy = _loss_call(y, target)
    dx, dw = vjp(dy)
    return loss, dx, dw


def kernel(x, p, w_in, w_merge, b_merge, conv_w, conv_b, conv_norm_g, conv_norm_b, w_pw2, mla_q_norm_g, mla_kv_norm_g, w_uq, w_ukv, ssm_a_re, ssm_a_im, ssm_log_dt, ssm_b_re, ssm_b_im, ssm_c_re, ssm_c_im, ssm_d, w_glu, attn_sinks, w_branch, w_out, ln_g, ln_b, w_ple, w_ple_gate, ple_norm_g, loss_target, m_w_in, m_w_merge, m_b_merge, m_conv_w, m_conv_b, m_conv_norm_g, m_conv_norm_b, m_w_pw2, m_mla_q_norm_g, m_mla_kv_norm_g, m_w_uq, m_w_ukv, m_ssm_a_re, m_ssm_a_im, m_ssm_log_dt, m_ssm_b_re, m_ssm_b_im, m_ssm_c_re, m_ssm_c_im, m_ssm_d, m_w_glu, m_attn_sinks, m_w_branch, m_w_out, m_ln_g, m_ln_b, m_w_ple, m_w_ple_gate, m_ple_norm_g, v_w_in, v_w_merge, v_b_merge, v_conv_w, v_conv_b, v_conv_norm_g, v_conv_norm_b, v_w_pw2, v_mla_q_norm_g, v_mla_kv_norm_g, v_w_uq, v_w_ukv, v_ssm_a_re, v_ssm_a_im, v_ssm_log_dt, v_ssm_b_re, v_ssm_b_im, v_ssm_c_re, v_ssm_c_im, v_ssm_d, v_w_glu, v_attn_sinks, v_w_branch, v_w_out, v_ln_g, v_ln_b, v_w_ple, v_w_ple_gate, v_ple_norm_g):
    given = dict(locals())
    w_loc = {n: given[n] for n in WEIGHTS}
    m_loc = {n: given["m_" + n] for n in WEIGHTS}
    v_loc = {n: given["v_" + n] for n in WEIGHTS}

    mxu_names = [n for n in SHARDED if n != "conv_w"]
    gathered, gathered_taps = _all_gather_chips([_pack([w_loc[n] for n in mxu_names], MXU_DTYPE), _pack([w_loc["conv_w"]], F32)])
    full = dict(w_loc)
    parts = [_unpack(gathered[s], [w_loc[n].shape for n in mxu_names]) for s in range(N_CHIPS)]
    for k, n in enumerate(mxu_names):
        full[n] = jnp.concatenate([parts[s][k] for s in range(N_CHIPS)], axis=SHARD_AXIS[n]).astype(F32)
    full["conv_w"] = jnp.concatenate([_unpack(gathered_taps[s], [conv_w.shape])[0] for s in range(N_CHIPS)], axis=SHARD_AXIS["conv_w"])

    loss, dx, dw = _train_local(x[0], p[:, 0], full, loss_target[0])
    loss = lax.psum(loss, ("x", "y", "c"))

    big32 = jnp.stack([_pack([_shard_slice(dw[n], SHARD_AXIS[n], d) for n in SHARDED], F32) for d in range(N_CHIPS)])
    small = _pack([dw[n] for n in REPLICATED], F32)
    recv, all_small = _exchange_grads(big32.astype(BF16), small)
    own = lax.dynamic_index_in_dim(big32, 2 * lax.axis_index("x") + lax.axis_index("y"), axis=0, keepdims=False)
    part = _sum_chips_call(own, recv)
    other = _swap_cores(part)
    shard_shapes = [w_loc[n].shape for n in SHARDED]
    part_t, other_t = _unpack(part, shard_shapes), _unpack(other, shard_shapes)

    grads, deltas, new_m, new_v = {}, {}, {}, {}
    for k, n in enumerate(SHARDED):
        shape = w_loc[n].shape
        two_d = lambda a: a.reshape(-1, shape[-1])
        res = _adamw_call(two_d(w_loc[n]), two_d(m_loc[n]), two_d(v_loc[n]), jnp.stack([two_d(part_t[k]), two_d(other_t[k])]), "adamw_" + n)
        grads[n], deltas[n], new_m[n], new_v[n] = [r.reshape(shape) for r in res]
    rep_shapes = [w_loc[n].shape for n in REPLICATED]
    res = _adamw_call(_pack([w_loc[n] for n in REPLICATED], F32), _pack([m_loc[n] for n in REPLICATED], F32),
                      _pack([v_loc[n] for n in REPLICATED], F32), all_small, "adamw_replicated")
    for dst, buf in zip((grads, deltas, new_m, new_v), res):
        for n, a in zip(REPLICATED, _unpack(buf, rep_shapes)):
            dst[n] = a

    return (loss, dx[None], *[grads[n] for n in WEIGHTS], *[deltas[n] for n in WEIGHTS],
            *[new_m[n] for n in WEIGHTS], *[new_v[n] for n in WEIGHTS])
```

```python
import functools
import math

import jax
import jax.numpy as jnp
from jax import lax
from jax.experimental import pallas as pl
from jax.experimental.pallas import tpu as pltpu

F32 = jnp.float32
BF16 = jnp.bfloat16
MXU_DTYPE = BF16
V7X_VMEM_BYTES = 64 * 1024 * 1024
VMEM_LIMIT = V7X_VMEM_BYTES * 3 // 4
LANES = 128
SUBLANES = 8

D_MODEL = 1024
DEPTH = 4
BRANCH_W = 256
CONV_W = 31
CONV_HALO = 32
MLA_SCALE = (64 + 32) ** -0.5
SWA_SCALE = 64 ** -0.5
WINDOW = 128
ROPE_THETA = 10000.0
SSM_GROUPS, SSM_GROUP, SSM_STATE = 16, 16, 64
SSM_CH = SSM_GROUPS * SSM_STATE
SCAN_SEGMENTS = SUBLANES
SCAN_CB = 128
DEEPNORM_ALPHA = (2.0 * DEPTH) ** 0.25
LN_EPS = 1e-5
RMS_EPS = 1e-6
ADAM_LR, ADAM_B1, ADAM_B2, ADAM_EPS, ADAM_WD, ADAM_STEP = 0.001, 0.9, 0.999, 1e-08, 0.01, 10
NEG = -1e30
ROW_TILE = 512

NN = (((1,), (0,)), ((), ()))
NT = (((1,), (1,)), ((), ()))
TN = (((0,), (0,)), ((), ()))

MESH = pl.DeviceIdType.MESH
ANY = pl.BlockSpec(memory_space=pl.ANY)


def _dot(a, b, dims):
    return lax.dot_general(a.astype(MXU_DTYPE), b.astype(MXU_DTYPE), dims, preferred_element_type=F32)


def _pick(n, cands):
    for c in cands:
        if n % c == 0:
            return c
    return n


def _params(sem):
    return pltpu.CompilerParams(dimension_semantics=sem, vmem_limit_bytes=VMEM_LIMIT)


def _mm_call(a, b, mode, name):
    if mode == "nn":
        (m, k), (_, n) = a.shape, b.shape
    elif mode == "nt":
        (m, k), (n, _) = a.shape, b.shape
    else:
        (k, m), (_, n) = a.shape, b.shape
    tm = _pick(m, (512, 256, 128))
    tn = _pick(n, (512, 256, 128))
    tk = _pick(k, (1024, 512, 256, 128))
    nk = k // tk
    dims = {"nn": NN, "nt": NT, "tn": TN}[mode]
    a_spec = pl.BlockSpec((tk, tm), lambda i, j, kk: (kk, i)) if mode == "tn" else pl.BlockSpec((tm, tk), lambda i, j, kk: (i, kk))
    b_spec = pl.BlockSpec((tn, tk), lambda i, j, kk: (j, kk)) if mode == "nt" else pl.BlockSpec((tk, tn), lambda i, j, kk: (kk, j))

    def body(a_ref, b_ref, o_ref, acc_ref):
        kk = pl.program_id(2)

        @pl.when(kk == 0)
        def _():
            acc_ref[...] = jnp.zeros_like(acc_ref)

        acc_ref[...] += _dot(a_ref[...], b_ref[...], dims)

        @pl.when(kk == nk - 1)
        def _():
            o_ref[...] = acc_ref[...]

    return pl.pallas_call(
        body, name=name, grid=(m // tm, n // tn, nk),
        in_specs=[a_spec, b_spec], out_specs=pl.BlockSpec((tm, tn), lambda i, j, kk: (i, j)),
        out_shape=jax.ShapeDtypeStruct((m, n), F32),
        scratch_shapes=[pltpu.VMEM((tm, tn), F32)],
        compiler_params=_params(("parallel", "parallel", "arbitrary")),
    )(a, b)


@jax.custom_vjp
def op_mm(a, w):
    return _mm_call(a, w.astype(MXU_DTYPE), "nn", "mm_nn")


def _op_mm_fwd(a, w):
    wb = w.astype(MXU_DTYPE)
    return _mm_call(a, wb, "nn", "mm_nn"), (a, wb)


def _op_mm_bwd(res, g):
    a, wb = res
    return _mm_call(g, wb, "nt", "mm_nt"), _mm_call(a, g, "tn", "mm_tn")


op_mm.defvjp(_op_mm_fwd, _op_mm_bwd)


def _col_offsets(widths):
    return [sum(widths[:j]) for j in range(len(widths))]


def _proj_fwd_call(x, wb, widths, name):
    t, k = x.shape
    tm = min(ROW_TILE, t)
    offs = _col_offsets(widths)

    def body(x_ref, w_ref, *o_refs):
        xb = x_ref[...].astype(MXU_DTYPE)
        for o_ref, off, wd in zip(o_refs, offs, widths):
            o_ref[...] = _dot(xb, w_ref[:, off:off + wd], NN)

    return pl.pallas_call(
        body, name=name, grid=(t // tm,),
        in_specs=[pl.BlockSpec((tm, k), lambda i: (i, 0)), pl.BlockSpec(wb.shape, lambda i: (0, 0))],
        out_specs=[pl.BlockSpec((tm, wd), lambda i: (i, 0)) for wd in widths],
        out_shape=[jax.ShapeDtypeStruct((t, wd), F32) for wd in widths],
        compiler_params=_params(("parallel",)),
    )(x, wb)


def _proj_dx_call(douts, wb, widths, name):
    t = douts[0].shape[0]
    k = wb.shape[0]
    tm = min(ROW_TILE, t)
    offs = _col_offsets(widths)

    def body(*refs):
        d_refs, w_ref, o_ref = refs[:-2], refs[-2], refs[-1]
        acc = jnp.zeros((tm, k), F32)
        for d_ref, off, wd in zip(d_refs, offs, widths):
            acc = acc + _dot(d_ref[...], w_ref[:, off:off + wd], NT)
        o_ref[...] = acc

    return pl.pallas_call(
        body, name=name, grid=(t // tm,),
        in_specs=[pl.BlockSpec((tm, wd), lambda i: (i, 0)) for wd in widths] + [pl.BlockSpec(wb.shape, lambda i: (0, 0))],
        out_specs=pl.BlockSpec((tm, k), lambda i: (i, 0)), out_shape=jax.ShapeDtypeStruct((t, k), F32),
        compiler_params=_params(("parallel",)),
    )(*douts, wb)


def _proj_dw_call(x, douts, widths, name):
    t, k = x.shape
    tk = min(ROW_TILE // 2, t)
    offs = _col_offsets(widths)
    n = sum(widths)

    def body(x_ref, *refs):
        d_refs, o_ref = refs[:-1], refs[-1]

        @pl.when(pl.program_id(0) == 0)
        def _():
            o_ref[...] = jnp.zeros_like(o_ref)

        xb = x_ref[...].astype(MXU_DTYPE)
        for d_ref, off, wd in zip(d_refs, offs, widths):
            o_ref[:, off:off + wd] += _dot(xb, d_ref[...], TN)

    return pl.pallas_call(
        body, name=name, grid=(t // tk,),
        in_specs=[pl.BlockSpec((tk, k), lambda i: (i, 0))] + [pl.BlockSpec((tk, wd), lambda i: (i, 0)) for wd in widths],
        out_specs=pl.BlockSpec((k, n), lambda i: (0, 0)), out_shape=jax.ShapeDtypeStruct((k, n), F32),
        compiler_params=_params(("arbitrary",)),
    )(x, *douts)


def make_proj(widths, name):
    @jax.custom_vjp
    def op(x, w):
        return tuple(_proj_fwd_call(x, w.astype(MXU_DTYPE), widths, name + "_fwd"))

    def fwd(x, w):
        wb = w.astype(MXU_DTYPE)
        return tuple(_proj_fwd_call(x, wb, widths, name + "_fwd")), (x, wb)

    def bwd(res, douts):
        x, wb = res
        return _proj_dx_call(douts, wb, widths, name + "_dx"), _proj_dw_call(x, douts, widths, name + "_dw")

    op.defvjp(fwd, bwd)
    return op


@jax.custom_vjp
def _mm(a, w):
    return _dot(a, w, NN)


def _mm_f(a, w):
    return _dot(a, w, NN), (a, w)


def _mm_b(res, g):
    a, w = res
    return _dot(g, w, NT), _dot(a, g, TN)


_mm.defvjp(_mm_f, _mm_b)


@functools.partial(jax.custom_vjp, nondiff_argnums=(1,))
def _roll(x, shift):
    return pltpu.roll(x, shift, 1)


def _roll_f(x, shift):
    return pltpu.roll(x, shift, 1), None


def _roll_b(shift, _, g):
    return (pltpu.roll(g, (g.shape[1] - shift) % g.shape[1], 1),)


_roll.defvjp(_roll_f, _roll_b)


def _ln(x, g, b):
    mu = jnp.mean(x, axis=-1, keepdims=True)
    xc = x - mu
    var = jnp.mean(xc * xc, axis=-1, keepdims=True)
    return xc * lax.rsqrt(var + LN_EPS) * g + b


def _rms(x, g):
    ms = jnp.mean(x * x, axis=-1, keepdims=True)
    return x * lax.rsqrt(ms + RMS_EPS) * g


def _sigmoid(x):
    return jax.nn.sigmoid(x)


def _silu(x):
    return x * _sigmoid(x)


def _gelu_tanh(x):
    return x * (0.5 * (1.0 + jnp.tanh(math.sqrt(2.0 / math.pi) * (x + 0.044715 * (x * x * x)))))


def _rowwise_fwd_call(fn, rows, consts, name, tile):
    t = rows[0].shape[0]
    tile = min(tile, t)
    nr = len(rows)
    outs = jax.eval_shape(fn, *[jax.ShapeDtypeStruct((tile, r.shape[1]), F32) for r in rows],
                          *[jax.ShapeDtypeStruct(c.shape, F32) for c in consts])

    def body(*refs):
        vals = [r[...] for r in refs[:nr + len(consts)]]
        res = fn(*vals)
        for o_ref, o in zip(refs[nr + len(consts):], res):
            o_ref[...] = o

    return pl.pallas_call(
        body, name=name, grid=(t // tile,),
        in_specs=[pl.BlockSpec((tile, r.shape[1]), lambda i: (i, 0)) for r in rows]
        + [pl.BlockSpec(c.shape, lambda i: (0, 0)) for c in consts],
        out_specs=[pl.BlockSpec((tile, o.shape[1]), lambda i: (i, 0)) for o in outs],
        out_shape=[jax.ShapeDtypeStruct((t, o.shape[1]), F32) for o in outs],
        compiler_params=_params(("parallel",)),
    )(*rows, *consts)


def _rowwise_bwd_call(fn, rows, consts, douts, row_diff, name, tile):
    t = rows[0].shape[0]
    tile = min(tile, t)
    nr, nc, nd = len(rows), len(consts), len(douts)
    diff_idx = [i for i in range(nr) if row_diff[i]]

    def body(*refs):
        rv = [r[...] for r in refs[:nr]]
        cv = [r[...] for r in refs[nr:nr + nc]]
        dv = [r[...] for r in refs[nr + nc:nr + nc + nd]]
        out_refs = refs[nr + nc + nd:]

        def f(*diff):
            full = list(rv)
            for k, i in enumerate(diff_idx):
                full[i] = diff[k]
            return fn(*full, *diff[len(diff_idx):])

        _, vjp = jax.vjp(f, *[rv[i] for i in diff_idx], *cv)
        grads = vjp(tuple(dv))
        for k in range(len(diff_idx)):
            out_refs[k][...] = grads[k]
        first = pl.program_id(0) == 0
        for k in range(nc):
            acc_ref = out_refs[len(diff_idx) + k]
            g = grads[len(diff_idx) + k]

            @pl.when(first)
            def _(acc_ref=acc_ref, g=g):
                acc_ref[...] = g

            @pl.when(jnp.logical_not(first))
            def _(acc_ref=acc_ref, g=g):
                acc_ref[...] += g

    res = pl.pallas_call(
        body, name=name, grid=(t // tile,),
        in_specs=[pl.BlockSpec((tile, r.shape[1]), lambda i: (i, 0)) for r in rows]
        + [pl.BlockSpec(c.shape, lambda i: (0, 0)) for c in consts]
        + [pl.BlockSpec((tile, d.shape[1]), lambda i: (i, 0)) for d in douts],
        out_specs=[pl.BlockSpec((tile, rows[i].shape[1]), lambda i_: (i_, 0)) for i in diff_idx]
        + [pl.BlockSpec(c.shape, lambda i: (0, 0)) for c in consts],
        out_shape=[jax.ShapeDtypeStruct(rows[i].shape, F32) for i in diff_idx]
        + [jax.ShapeDtypeStruct(c.shape, F32) for c in consts],
        compiler_params=_params(("arbitrary",)),
    )(*rows, *consts, *douts)
    return res[:len(diff_idx)], res[len(diff_idx):]


def make_rowwise(fn, name, row_diff, tile=ROW_TILE):
    @jax.custom_vjp
    def op(rows, consts):
        return tuple(_rowwise_fwd_call(fn, rows, consts, name + "_fwd", tile))

    def fwd(rows, consts):
        return op(rows, consts), (rows, consts)

    def bwd(res, douts):
        rows, consts = res
        drows, dconsts = _rowwise_bwd_call(fn, rows, consts, douts, row_diff, name + "_bwd", tile)
        it = iter(drows)
        full = tuple(next(it) if row_diff[i] else jnp.zeros_like(rows[i]) for i in range(len(rows)))
        return full, tuple(dconsts)

    op.defvjp(fwd, bwd)
    return op


def _conv_post_fn(cv, a_z, ng, nb, w_pw2):
    return (_mm(_silu(_ln(cv, ng, nb)), w_pw2) * _silu(a_z),)


def _mla_prep_fn(c_q, c_kv, krblk, cos4, sin4, qg, kvg, w_uq, w_uk, w_uv):
    qe = _mm(_rms(c_q, qg), w_uq)
    q = qe * cos4 + _roll(qe, qe.shape[1] - 32) * sin4
    cos1, sin1 = cos4[:, :LANES], sin4[:, :LANES]
    kr = krblk * cos1 + _roll(krblk, LANES - 32) * sin1
    kn = _rms(c_kv, kvg)
    k = _mm(kn, w_uk) + jnp.concatenate([kr, kr, kr, kr], axis=1)
    return q, k, _mm(kn, w_uv)


def _ssm_post_fn(y, u, c_z, d, w_a, w_b):
    y2 = _gelu_tanh(y + d * u)
    return (_mm(y2, w_a) * _sigmoid(_mm(y2, w_b)) * _silu(c_z),)


def _gate_fn(o, z):
    return (o * _silu(z),)


def _merge_fn(br0, br1, br2, br3, gl0, gl1, gl2, gl3, b0, b1, b2, b3):
    return (_sigmoid(gl0 + b0) * br0 + _sigmoid(gl1 + b1) * br1 + _sigmoid(gl2 + b2) * br2 + _sigmoid(gl3 + b3) * br3,)


def _ln_fn(x, mo, g, b):
    return (_ln(DEEPNORM_ALPHA * x + mo, g, b),)


def _ple_fn(x1, pe, gl, g):
    return (x1 + _rms(pe * _sigmoid(gl), g),)


op_conv_post = make_rowwise(_conv_post_fn, "conv_post", (True, True))
op_mla_prep = make_rowwise(_mla_prep_fn, "mla_prep", (True, True, True, False, False))
op_ssm_post = make_rowwise(_ssm_post_fn, "ssm_post", (True, True, True))
op_gate = make_rowwise(_gate_fn, "gate", (True, True))
op_merge = make_rowwise(_merge_fn, "merge", (True,) * 8, tile=ROW_TILE // 2)
op_ln = make_rowwise(_ln_fn, "post_ln", (True, True))
op_ple = make_rowwise(_ple_fn, "ple", (True, True, True))


def _conv_fwd_call(a_val, a_gate, w32, b):
    t, w = a_val.shape
    tile = min(ROW_TILE, t)
    per = tile // CONV_HALO
    cur = pl.BlockSpec((tile, w), lambda i: (i, 0))
    prev = pl.BlockSpec((CONV_HALO, w), lambda i: (jnp.maximum(i * per - 1, 0), 0))

    def body(av_ref, avh_ref, ag_ref, agh_ref, w_ref, b_ref, cv_ref, buf):
        i = pl.program_id(0)
        gh = avh_ref[...] * _sigmoid(agh_ref[...])
        buf[0:CONV_HALO, :] = jnp.where(i > 0, gh, 0.0)
        buf[CONV_HALO:, :] = av_ref[...] * _sigmoid(ag_ref[...])
        acc = jnp.zeros((tile, w), F32) + b_ref[...]
        for j in range(CONV_W):
            acc = acc + w_ref[j:j + 1, :] * buf[pl.ds(CONV_HALO - (CONV_W - 1) + j, tile), :]
        cv_ref[...] = acc

    return pl.pallas_call(
        body, name="conv_fwd", grid=(t // tile,),
        in_specs=[cur, prev, cur, prev, pl.BlockSpec((CONV_HALO, w), lambda i: (0, 0)), pl.BlockSpec((1, w), lambda i: (0, 0))],
        out_specs=cur, out_shape=jax.ShapeDtypeStruct((t, w), F32),
        scratch_shapes=[pltpu.VMEM((tile + CONV_HALO, w), F32)],
        compiler_params=_params(("parallel",)),
    )(a_val, a_val, a_gate, a_gate, w32, b)


def _conv_bwd_call(a_val, a_gate, w32, dcv):
    t, w = a_val.shape
    tile = min(ROW_TILE, t)
    n = t // tile
    per = tile // CONV_HALO
    cur = pl.BlockSpec((tile, w), lambda i: (i, 0))
    prev = pl.BlockSpec((CONV_HALO, w), lambda i: (jnp.maximum(i * per - 1, 0), 0))
    nxt = pl.BlockSpec((CONV_HALO, w), lambda i: (jnp.minimum((i + 1) * per, t // CONV_HALO - 1), 0))
    full = lambda r: pl.BlockSpec((r, w), lambda i: (0, 0))

    def body(av_ref, avh_ref, ag_ref, agh_ref, w_ref, d_ref, dn_ref, dav_ref, dag_ref, dw_ref, db_ref, gbuf, dbuf):
        i = pl.program_id(0)
        gh = avh_ref[...] * _sigmoid(agh_ref[...])
        gbuf[0:CONV_HALO, :] = jnp.where(i > 0, gh, 0.0)
        av = av_ref[...]
        sg = _sigmoid(ag_ref[...])
        gbuf[CONV_HALO:, :] = av * sg
        d = d_ref[...]
        dbuf[0:tile, :] = d
        dbuf[tile:, :] = jnp.where(i < n - 1, dn_ref[...], 0.0)

        @pl.when(i == 0)
        def _():
            dw_ref[...] = jnp.zeros_like(dw_ref)
            db_ref[...] = jnp.zeros_like(db_ref)

        dg = jnp.zeros((tile, w), F32)
        for j in range(CONV_W):
            dg = dg + w_ref[j:j + 1, :] * dbuf[pl.ds(CONV_W - 1 - j, tile), :]
            dw_ref[j:j + 1, :] += jnp.sum(d * gbuf[pl.ds(CONV_HALO - (CONV_W - 1) + j, tile), :], axis=0, keepdims=True)
        db_ref[...] += jnp.sum(d, axis=0, keepdims=True)
        dav_ref[...] = dg * sg
        dag_ref[...] = dg * av * sg * (1.0 - sg)

    return pl.pallas_call(
        body, name="conv_bwd", grid=(n,),
        in_specs=[cur, prev, cur, prev, full(CONV_HALO), cur, nxt],
        out_specs=[cur, cur, full(CONV_HALO), full(1)],
        out_shape=[jax.ShapeDtypeStruct((t, w), F32), jax.ShapeDtypeStruct((t, w), F32),
                   jax.ShapeDtypeStruct((CONV_HALO, w), F32), jax.ShapeDtypeStruct((1, w), F32)],
        scratch_shapes=[pltpu.VMEM((tile + CONV_HALO, w), F32), pltpu.VMEM((tile + CONV_HALO, w), F32)],
        compiler_params=_params(("arbitrary",)),
    )(a_val, a_val, a_gate, a_gate, w32, dcv, dcv)


def _pad_taps(conv_w):
    return jnp.concatenate([conv_w, jnp.zeros((CONV_HALO - CONV_W, conv_w.shape[1]), F32)], axis=0)


@jax.custom_vjp
def op_conv(a_val, a_gate, conv_w, conv_b):
    return _conv_fwd_call(a_val, a_gate, _pad_taps(conv_w), conv_b)


def _op_conv_fwd(a_val, a_gate, conv_w, conv_b):
    return op_conv(a_val, a_gate, conv_w, conv_b), (a_val, a_gate, conv_w)


def _op_conv_bwd(res, dcv):
    a_val, a_gate, conv_w = res
    dav, dag, dw, db = _conv_bwd_call(a_val, a_gate, _pad_taps(conv_w), dcv)
    return dav, dag, dw[:CONV_W], db


op_conv.defvjp(_op_conv_fwd, _op_conv_bwd)


def _head_masks(rows):
    lane = lax.broadcasted_iota(jnp.int32, (rows, LANES), 1)
    return lane < 64, lane >= 64


def _head_row(vals, mask):
    return jnp.max(jnp.where(mask, vals, NEG), axis=1, keepdims=True)


def _attn_valid(qpos, kpos, window):
    valid = kpos <= qpos
    if window is not None:
        valid = jnp.logical_and(valid, qpos - kpos < window)
    return valid


def _flash_fwd_call(q, k, v, sink, *, window, shared_k, scale, blk, name):
    t = q.shape[0]
    qw = LANES if shared_k else 2 * LANES
    pairs = v.shape[1] // LANES
    tq = tk = min(blk, t)
    has_sink = sink is not None

    def body(*refs):
        if has_sink:
            q_ref, k_ref, v_ref, s_ref, o_ref, lse_ref = refs
        else:
            q_ref, k_ref, v_ref, o_ref, lse_ref = refs
        i = pl.program_id(1)
        qb = q_ref[...]
        masks = _head_masks(tq)
        row_masks = _head_masks(1)
        qpos = i * tq + lax.broadcasted_iota(jnp.int32, (tq, tk), 0)
        lo = 0 if window is None else jnp.maximum(i * tq - (window - 1), 0) // tk
        hi = i + 1
        if has_sink:
            m_init = [jnp.zeros((tq, 1), F32) + _head_row(s_ref[...], row_masks[h]) for h in range(2)]
            l_init = [jnp.ones((tq, 1), F32)] * 2
        else:
            m_init = [jnp.full((tq, 1), NEG, F32)] * 2
            l_init = [jnp.zeros((tq, 1), F32)] * 2

        def step(j, carry):
            m0, l0, m1, l1, acc = carry
            start = pl.multiple_of(j * tk, tk)
            kb = k_ref[pl.ds(start, tk), :]
            vb = v_ref[pl.ds(start, tk), :]
            valid = _attn_valid(qpos, j * tk + lax.broadcasted_iota(jnp.int32, (tq, tk), 1), window)
            new, alphas, pv = [], [], []
            for h, (m, l) in enumerate(((m0, l0), (m1, l1))):
                qh = jnp.where(masks[h], qb, 0.0) if shared_k else qb[:, h * LANES:(h + 1) * LANES]
                kh = kb if shared_k else kb[:, h * LANES:(h + 1) * LANES]
                s = jnp.where(valid, _dot(qh, kh, NT) * scale, NEG)
                m_new = jnp.maximum(m, jnp.max(s, axis=1, keepdims=True))
                alpha = jnp.exp(m - m_new)
                p = jnp.exp(s - m_new)
                new += [m_new, alpha * l + jnp.sum(p, axis=1, keepdims=True)]
                alphas.append(alpha)
                pv.append(_dot(p, jnp.where(masks[h], vb, 0.0), NN))
            acc = acc * jnp.where(masks[0], alphas[0], alphas[1]) + pv[0] + pv[1]
            return new[0], new[1], new[2], new[3], acc

        m0, l0, m1, l1, acc = lax.fori_loop(lo, hi, step, (m_init[0], l_init[0], m_init[1], l_init[1], jnp.zeros((tq, LANES), F32)))
        o_ref[...] = acc / jnp.where(masks[0], l0, l1)
        lse_ref[...] = jnp.where(masks[0], m0 + jnp.log(l0), m1 + jnp.log(l1))

    in_specs = [pl.BlockSpec((tq, qw), lambda p, i: (i, p)), pl.BlockSpec((t, qw), lambda p, i: (0, p)),
                pl.BlockSpec((t, LANES), lambda p, i: (0, p))]
    args = [q, k, v]
    if has_sink:
        in_specs.append(pl.BlockSpec((1, LANES), lambda p, i: (0, p)))
        args.append(sink)
    blk_o = pl.BlockSpec((tq, LANES), lambda p, i: (i, p))
    return pl.pallas_call(
        body, name=name, grid=(pairs, t // tq), in_specs=in_specs, out_specs=[blk_o, blk_o],
        out_shape=[jax.ShapeDtypeStruct((t, pairs * LANES), F32)] * 2,
        compiler_params=_params(("parallel", "arbitrary")),
    )(*args)


def _flash_bwd_call(q, k, v, sink, o, lse, do, *, window, shared_k, scale, blk, name):
    t = q.shape[0]
    qw = LANES if shared_k else 2 * LANES
    pairs = v.shape[1] // LANES
    tq = tk = min(blk, t)
    nq = t // tq
    has_sink = sink is not None

    def body(*refs):
        if has_sink:
            q_ref, k_ref, v_ref, o_ref, lse_ref, do_ref, s_ref, dq_ref, dk_ref, dv_ref, ds_ref = refs
        else:
            q_ref, k_ref, v_ref, o_ref, lse_ref, do_ref, dq_ref, dk_ref, dv_ref = refs
        j = pl.program_id(1)
        masks = _head_masks(tq)
        row_masks = _head_masks(1)

        @pl.when(j == 0)
        def _():
            dq_ref[...] = jnp.zeros_like(dq_ref)
            if has_sink:
                full_masks = _head_masks(t)
                prod = do_ref[...] * o_ref[...]
                parts = []
                for h in range(2):
                    dsum = jnp.sum(jnp.where(full_masks[h], prod, 0.0), axis=1, keepdims=True)
                    ps = jnp.exp(_head_row(s_ref[...], row_masks[h]) - _head_row(lse_ref[...], full_masks[h]))
                    parts.append(-jnp.sum(ps * dsum, axis=0, keepdims=True))
                ds_ref[...] = jnp.zeros((SUBLANES, LANES), F32) + jnp.where(row_masks[0], parts[0], parts[1])

        kb = k_ref[...]
        vb = v_ref[...]
        kpos = j * tk + lax.broadcasted_iota(jnp.int32, (tq, tk), 1)
        hi = nq if window is None else jnp.minimum(nq, (j * tk + tk - 1 + window - 1) // tq + 1)

        def step(i, carry):
            dk0, dk1, dv = carry
            start = pl.multiple_of(i * tq, tq)
            qb = q_ref[pl.ds(start, tq), :]
            dob = do_ref[pl.ds(start, tq), :]
            ob = o_ref[pl.ds(start, tq), :]
            lseb = lse_ref[pl.ds(start, tq), :]
            valid = _attn_valid(i * tq + lax.broadcasted_iota(jnp.int32, (tq, tk), 0), kpos, window)
            dks, dqs = [], []
            for h in range(2):
                qh = jnp.where(masks[h], qb, 0.0) if shared_k else qb[:, h * LANES:(h + 1) * LANES]
                kh = kb if shared_k else kb[:, h * LANES:(h + 1) * LANES]
                s = _dot(qh, kh, NT) * scale
                p = jnp.where(valid, jnp.exp(s - _head_row(lseb, masks[h])), 0.0)
                doh = jnp.where(masks[h], dob, 0.0)
                dsum = jnp.sum(doh * ob, axis=1, keepdims=True)
                dp = _dot(doh, vb, NT)
                dsc = p * (dp - dsum) * scale
                dv = dv + _dot(p, doh, TN)
                dks.append(_dot(dsc, qh, TN))
                dq_h = _dot(dsc, kh, NN)
                dqs.append(jnp.where(masks[h], dq_h, 0.0) if shared_k else dq_h)
            if shared_k:
                dq_ref[pl.ds(start, tq), :] += dqs[0] + dqs[1]
            else:
                dq_ref[pl.ds(start, tq), :] += jnp.concatenate(dqs, axis=1)
            return dk0 + dks[0], dk1 + dks[1], dv

        zero = jnp.zeros((tk, LANES), F32)
        dk0, dk1, dv = lax.fori_loop(j, hi, step, (zero, zero, zero))
        dk_ref[...] = dk0 + dk1 if shared_k else jnp.concatenate([dk0, dk1], axis=1)
        dv_ref[...] = dv

    full = lambda w: pl.BlockSpec((t, w), lambda p, j: (0, p))
    blkspec = lambda w: pl.BlockSpec((tk, w), lambda p, j: (j, p))
    in_specs = [full(qw), blkspec(qw), blkspec(LANES), full(LANES), full(LANES), full(LANES)]
    args = [q, k, v, o, lse, do]
    out_specs = [full(qw), blkspec(qw), blkspec(LANES)]
    out_shape = [jax.ShapeDtypeStruct(q.shape, F32), jax.ShapeDtypeStruct(k.shape, F32), jax.ShapeDtypeStruct(v.shape, F32)]
    if has_sink:
        in_specs.append(pl.BlockSpec((1, LANES), lambda p, j: (0, p)))
        args.append(sink)
        out_specs.append(pl.BlockSpec((SUBLANES, LANES), lambda p, j: (0, p)))
        out_shape.append(jax.ShapeDtypeStruct((SUBLANES, pairs * LANES), F32))
    return pl.pallas_call(
        body, name=name, grid=(pairs, t // tk), in_specs=in_specs, out_specs=out_specs, out_shape=out_shape,
        compiler_params=_params(("arbitrary", "arbitrary")),
    )(*args)


_MLA_CFG = dict(window=None, shared_k=False, scale=MLA_SCALE, blk=256)
_SWA_CFG = dict(window=WINDOW, shared_k=True, scale=SWA_SCALE, blk=128)


@jax.custom_vjp
def op_mla_attn(q, k, v):
    return _flash_fwd_call(q, k, v, None, name="mla_fwd", **_MLA_CFG)[0]


def _op_mla_attn_fwd(q, k, v):
    o, lse = _flash_fwd_call(q, k, v, None, name="mla_fwd", **_MLA_CFG)
    return o, (q, k, v, o, lse)


def _op_mla_attn_bwd(res, do):
    q, k, v, o, lse = res
    return tuple(_flash_bwd_call(q, k, v, None, o, lse, do, name="mla_bwd", **_MLA_CFG))


op_mla_attn.defvjp(_op_mla_attn_fwd, _op_mla_attn_bwd)


@jax.custom_vjp
def op_swa_attn(q, k, v, sink):
    return _flash_fwd_call(q, k, v, sink, name="swa_fwd", **_SWA_CFG)[0]


def _op_swa_attn_fwd(q, k, v, sink):
    o, lse = _flash_fwd_call(q, k, v, sink, name="swa_fwd", **_SWA_CFG)
    return o, (q, k, v, sink, o, lse)


def _op_swa_attn_bwd(res, do):
    q, k, v, sink, o, lse = res
    dq, dk, dv, dsink = _flash_bwd_call(q, k, v, sink, o, lse, do, name="swa_bwd", **_SWA_CFG)
    first_lane = lax.broadcasted_iota(jnp.int32, (1, dsink.shape[1]), 1) % 64 == 0
    return dq, dk, dv, jnp.where(first_lane, dsink[:1], 0.0)


op_swa_attn.defvjp(_op_swa_attn_fwd, _op_swa_attn_bwd)


def _complex_power(ar, ai, n):
    for _ in range(int(math.log2(n))):
        ar, ai = ar * ar - ai * ai, 2.0 * ar * ai
    return ar, ai


def _scan_passes(load_b, a1r, a1i, n, store, e_ref, c_ref, reverse):
    cb = a1r.shape[1]
    ar = jnp.zeros((SCAN_SEGMENTS, cb), F32) + a1r
    ai = jnp.zeros((SCAN_SEGMENTS, cb), F32) + a1i
    idx = (lambda ii: n - 1 - ii) if reverse else (lambda ii: ii)

    def local(ii, h):
        br, bi = load_b(idx(ii))
        return ar * h[0] - ai * h[1] + br, ar * h[1] + ai * h[0] + bi

    zero = jnp.zeros((SCAN_SEGMENTS, cb), F32)
    er, ei = lax.fori_loop(0, n, local, (zero, zero))
    e_ref[:, 0:cb] = er
    e_ref[:, cb:] = ei
    pr, pi_ = _complex_power(a1r, a1i, n)
    cr = jnp.zeros((1, cb), F32)
    ci = jnp.zeros((1, cb), F32)
    order = range(SCAN_SEGMENTS - 1, -1, -1) if reverse else range(SCAN_SEGMENTS)
    for s in order:
        c_ref[s:s + 1, 0:cb] = cr
        c_ref[s:s + 1, cb:] = ci
        er1, ei1 = e_ref[s:s + 1, 0:cb], e_ref[s:s + 1, cb:]
        cr, ci = pr * cr - pi_ * ci + er1, pr * ci + pi_ * cr + ei1

    def second(ii, h):
        i = idx(ii)
        hr, hi = local(ii, h)
        store(i, hr, hi)
        return hr, hi

    lax.fori_loop(0, n, second, (c_ref[:, 0:cb], c_ref[:, cb:]))


def _scan_fwd_call(bu, lam):
    n = bu.shape[0]
    cb = SCAN_CB
    blk3 = pl.BlockSpec((n, SCAN_SEGMENTS, 2 * cb), lambda c: (0, 0, c))
    blk2 = lambda r: pl.BlockSpec((r, 2 * cb), lambda c: (0, c))

    def body(b_ref, lam_ref, h_ref, cin_ref, e_ref):
        def store(i, hr, hi):
            h_ref[i, :, 0:cb] = hr
            h_ref[i, :, cb:] = hi

        _scan_passes(lambda i: (b_ref[i, :, 0:cb], b_ref[i, :, cb:]), lam_ref[:, 0:cb], lam_ref[:, cb:], n, store,
                     e_ref, cin_ref, False)

    return pl.pallas_call(
        body, name="scan_fwd", grid=(SSM_CH // cb,), in_specs=[blk3, blk2(1)], out_specs=[blk3, blk2(SCAN_SEGMENTS)],
        out_shape=[jax.ShapeDtypeStruct(bu.shape, F32), jax.ShapeDtypeStruct((SCAN_SEGMENTS, 2 * SSM_CH), F32)],
        scratch_shapes=[pltpu.VMEM((SCAN_SEGMENTS, 2 * cb), F32)],
        compiler_params=_params(("parallel",)),
    )(bu, lam)


def _scan_bwd_call(dh, h, cin, lam):
    n = dh.shape[0]
    cb = SCAN_CB
    blk3 = pl.BlockSpec((n, SCAN_SEGMENTS, 2 * cb), lambda c: (0, 0, c))
    blk2 = lambda r: pl.BlockSpec((r, 2 * cb), lambda c: (0, c))

    def body(d_ref, h_ref, cin_ref, lam_ref, g_ref, dlam_ref, e_ref, c_ref, acc_ref):
        acc_ref[...] = jnp.zeros_like(acc_ref)

        def store(i, gr, gi):
            g_ref[i, :, 0:cb] = gr
            g_ref[i, :, cb:] = gi
            ip = jnp.maximum(i - 1, 0)
            hpr = jnp.where(i > 0, h_ref[ip, :, 0:cb], cin_ref[:, 0:cb])
            hpi = jnp.where(i > 0, h_ref[ip, :, cb:], cin_ref[:, cb:])
            acc_ref[:, 0:cb] += gr * hpr + gi * hpi
            acc_ref[:, cb:] += gi * hpr - gr * hpi

        _scan_passes(lambda i: (d_ref[i, :, 0:cb], d_ref[i, :, cb:]), lam_ref[:, 0:cb], -lam_ref[:, cb:], n, store,
                     e_ref, c_ref, True)
        dlam_ref[...] = acc_ref[...]

    return pl.pallas_call(
        body, name="scan_bwd", grid=(SSM_CH // cb,), in_specs=[blk3, blk3, blk2(SCAN_SEGMENTS), blk2(1)],
        out_specs=[blk3, blk2(SCAN_SEGMENTS)],
        out_shape=[jax.ShapeDtypeStruct(dh.shape, F32), jax.ShapeDtypeStruct((SCAN_SEGMENTS, 2 * SSM_CH), F32)],
        scratch_shapes=[pltpu.VMEM((SCAN_SEGMENTS, 2 * cb), F32)] * 3,
        compiler_params=_params(("parallel",)),
    )(dh, h, cin, lam)


@jax.custom_vjp
def op_scan(bu, lam):
    return _scan_fwd_call(bu, lam)[0]


def _op_scan_fwd(bu, lam):
    h, cin = _scan_fwd_call(bu, lam)
    return h, (h, cin, lam)


def _op_scan_bwd(res, dh):
    h, cin, lam = res
    g, dlam = _scan_bwd_call(dh, h, cin, lam)
    return g, jnp.sum(dlam, axis=0, keepdims=True)


op_scan.defvjp(_op_scan_fwd, _op_scan_bwd)


def _loss_call(y, target):
    t, d = y.shape
    tile = min(ROW_TILE, t)

    def body(y_ref, t_ref, dy_ref, acc_ref):
        @pl.when(pl.program_id(0) == 0)
        def _():
            acc_ref[...] = jnp.zeros_like(acc_ref)

        err = y_ref[...] - t_ref[...]
        dy_ref[...] = err * (1.0 / d)
        col = jnp.sum(err * err, axis=0, keepdims=True)
        part = col[:, 0:LANES]
        for c in range(1, d // LANES):
            part = part + col[:, c * LANES:(c + 1) * LANES]
        acc_ref[0:1, :] += part

    blk = pl.BlockSpec((tile, d), lambda i: (i, 0))
    dy, acc = pl.pallas_call(
        body, name="loss_head", grid=(t // tile,), in_specs=[blk, blk],
        out_specs=[blk, pl.BlockSpec((SUBLANES, LANES), lambda i: (0, 0))],
        out_shape=[jax.ShapeDtypeStruct((t, d), F32), jax.ShapeDtypeStruct((SUBLANES, LANES), F32)],
        compiler_params=_params(("arbitrary",)),
    )(y, target)
    return jnp.sum(acc) * (0.5 / d), dy


def _rot_cols(w):
    return jnp.concatenate([-w[:, 16:], w[:, :16]], axis=1)


def _ext_w_in(w):
    a_val, a_gate, a_z, c_q, c_kv, k_r, b_z, u, c_z, q, k, v, d_z = jnp.split(
        w, (256, 512, 768, 1024, 1152, 1184, 1440, 1696, 1952, 2208, 2336, 2464), axis=1)
    dup = lambda m: jnp.concatenate([m[:, :64], m[:, :64], m[:, 64:], m[:, 64:]], axis=1)
    krblk = jnp.concatenate([jnp.zeros((w.shape[0], 64), w.dtype), k_r, _rot_cols(k_r)], axis=1)
    return jnp.concatenate([a_val, a_gate, a_z, c_q, b_z, u, c_z, q, dup(k), dup(v), d_z, c_kv, krblk], axis=1)


H_COLS = dict(a_val=256, a_gate=256, a_z=256, c_q=256, b_z=256, u=256, c_z=256, q=256, kdup=256, vdup=256, d_z=256,
              c_kv=128, krblk=128)
op_in_proj = make_proj(tuple(H_COLS.values()), "in_proj")
op_merge_proj = make_proj((D_MODEL,) * 4, "merge_proj")


def _ext_mla(w_uq, w_ukv):
    zeros = jnp.zeros((w_ukv.shape[0], 64), w_ukv.dtype)
    uq, uk, uv = [], [], []
    for h in range(4):
        nope, rp = w_uq[:, 96 * h:96 * h + 64], w_uq[:, 96 * h + 64:96 * h + 96]
        uq += [nope, rp, _rot_cols(rp)]
        uk += [w_ukv[:, 128 * h:128 * h + 64], zeros]
        uv.append(w_ukv[:, 128 * h + 64:128 * h + 128])
    return jnp.concatenate(uq, axis=1), jnp.concatenate(uk, axis=1), jnp.concatenate(uv, axis=1)


def _scan_cols(re, im):
    parts = []
    for c in range(SSM_CH // SCAN_CB):
        parts += [re[..., c * SCAN_CB:(c + 1) * SCAN_CB], im[..., c * SCAN_CB:(c + 1) * SCAN_CB]]
    return jnp.concatenate(parts, axis=-1)


def _ext_ssm(a_re, a_im, log_dt, b_re, b_im, c_re, c_im):
    dt = jnp.exp(log_dt)[:, None]
    mag = jnp.exp(a_re * dt)
    lb_re, lb_im = mag * jnp.cos(a_im * dt), mag * jnp.sin(a_im * dt)
    den = a_re * a_re + a_im * a_im
    nr, ni = lb_re - 1.0, lb_im
    f_re = ((nr * a_re + ni * a_im) / den)[..., None]
    f_im = ((ni * a_re - nr * a_im) / den)[..., None]
    bb_re = f_re * b_re - f_im * b_im
    bb_im = f_re * b_im + f_im * b_re
    eye = jnp.eye(SSM_GROUPS, dtype=F32)
    spread = lambda a: a.transpose(0, 2, 1)[:, :, None, :] * eye[:, None, :, None]
    bd_in = lambda bb: spread(bb).reshape(SSM_GROUPS * SSM_GROUP, SSM_CH)
    bd_out = lambda cc: spread(cc).reshape(SSM_CH, SSM_GROUPS * SSM_GROUP)
    w_bu = _scan_cols(bd_in(bb_re), bd_in(bb_im))
    w_y = _scan_cols(bd_out(c_re).T, -bd_out(c_im).T).T
    lam = _scan_cols(lb_re.reshape(1, SSM_CH), lb_im.reshape(1, SSM_CH))
    return w_bu, w_y, lam


def _rope_tables(t):
    pos = jnp.arange(t, dtype=F32)
    inv_freq = ROPE_THETA ** (-jnp.arange(0, 32, 2, dtype=F32) / 32)
    ang = pos[:, None] * inv_freq[None, :]
    cos, sin = jnp.cos(ang), jnp.sin(ang)
    ones, z32, z64 = jnp.ones((t, 64), F32), jnp.zeros((t, 32), F32), jnp.zeros((t, 64), F32)
    cos1 = jnp.concatenate([ones, cos, cos, z32], axis=1)
    sin1 = jnp.concatenate([z64, sin, sin, z32], axis=1)
    return jnp.concatenate([cos1] * 4, axis=1), jnp.concatenate([sin1] * 4, axis=1)


def _to_segments(a):
    t, w = a.shape
    return a.reshape(SCAN_SEGMENTS, t // SCAN_SEGMENTS, w).transpose(1, 0, 2)


def _from_segments(a):
    n, s, w = a.shape
    return a.transpose(1, 0, 2).reshape(n * s, w)


def _layer(x, p_i, cos4, sin4, w):
    t = x.shape[0]
    row = lambda v: v.reshape(1, -1)
    hs = dict(zip(H_COLS, op_in_proj(x, _ext_w_in(w["w_in"]))))

    cv = op_conv(hs["a_val"], hs["a_gate"], w["conv_w"], row(w["conv_b"]))
    (y_a,) = op_conv_post((cv, hs["a_z"]), (row(w["conv_norm_g"]), row(w["conv_norm_b"]), w["w_pw2"]))

    w_uq, w_uk, w_uv = _ext_mla(w["w_uq"], w["w_ukv"])
    q, k, v = op_mla_prep((hs["c_q"], hs["c_kv"], hs["krblk"], cos4, sin4),
                          (row(w["mla_q_norm_g"]), row(w["mla_kv_norm_g"]), w_uq, w_uk, w_uv))
    (y_b,) = op_gate((op_mla_attn(q, k, v), hs["b_z"]), ())

    w_bu, w_y, lam = _ext_ssm(w["ssm_a_re"], w["ssm_a_im"], w["ssm_log_dt"], w["ssm_b_re"], w["ssm_b_im"],
                              w["ssm_c_re"], w["ssm_c_im"])
    u_seg = _to_segments(hs["u"]).reshape(t, BRANCH_W)
    bu = op_mm(u_seg, w_bu).reshape(t // SCAN_SEGMENTS, SCAN_SEGMENTS, 2 * SSM_CH)
    hstate = op_scan(bu, lam).reshape(t, 2 * SSM_CH)
    y_ssm = _from_segments(op_mm(hstate, w_y).reshape(t // SCAN_SEGMENTS, SCAN_SEGMENTS, BRANCH_W))
    (y_c,) = op_ssm_post((y_ssm, hs["u"], hs["c_z"]), (row(w["ssm_d"]), w["w_glu"][:, :BRANCH_W], w["w_glu"][:, BRANCH_W:]))

    sink = jnp.repeat(w["attn_sinks"], 64).reshape(1, 2 * LANES)
    (y_d,) = op_gate((op_swa_attn(hs["q"], hs["kdup"], hs["vdup"], sink), hs["d_z"]), ())

    br = [op_mm(y, w["w_branch"][n]) for n, y in enumerate((y_a, y_b, y_c, y_d))]
    gl = op_merge_proj(x, w["w_merge"])
    bm = [row(w["b_merge"][n * D_MODEL:(n + 1) * D_MODEL]) for n in range(4)]
    (merged,) = op_merge((*br, *gl), tuple(bm))
    (x1,) = op_ln((x, op_mm(merged, w["w_out"])), (row(w["ln_g"]), row(w["ln_b"])))
    (out,) = op_ple((x1, op_mm(p_i, w["w_ple"]), op_mm(x1, w["w_ple_gate"])), (row(w["ple_norm_g"]),))
    return out


def _forward(x, p, layers):
    cos4, sin4 = _rope_tables(x.shape[0])
    for i in range(DEPTH):
        x = _layer(x, p[i], cos4, sin4, layers[i])
    return x


SHARD_AXIS = dict(w_in=2, w_merge=2, conv_w=2, w_pw2=1, w_uq=2, w_ukv=2, w_glu=2, w_branch=3, w_out=1, w_ple=2, w_ple_gate=1)
SHARDED = tuple(SHARD_AXIS)
REPLICATED = ("b_merge", "conv_b", "conv_norm_g", "conv_norm_b", "mla_q_norm_g", "mla_kv_norm_g", "ssm_a_re", "ssm_a_im",
              "ssm_log_dt", "ssm_b_re", "ssm_b_im", "ssm_c_re", "ssm_c_im", "ssm_d", "attn_sinks", "ln_g", "ln_b", "ple_norm_g")
WEIGHTS = ("w_in", "w_merge", "b_merge", "conv_w", "conv_b", "conv_norm_g", "conv_norm_b", "w_pw2", "mla_q_norm_g",
           "mla_kv_norm_g", "w_uq", "w_ukv", "ssm_a_re", "ssm_a_im", "ssm_log_dt", "ssm_b_re", "ssm_b_im", "ssm_c_re",
           "ssm_c_im", "ssm_d", "w_glu", "attn_sinks", "w_branch", "w_out", "ln_g", "ln_b", "w_ple", "w_ple_gate", "ple_norm_g")
PACK_COLS = 1024
PACK_ROWS = 512
CHIP_FLIPS = ((1, 0), (0, 1), (1, 1))
N_CHIPS = 4
N_DEV = 8


def _pack(arrays, dtype):
    flat = jnp.concatenate([a.reshape(-1).astype(dtype) for a in arrays])
    unit = PACK_ROWS * PACK_COLS
    total = -(-flat.shape[0] // unit) * unit
    return jnp.concatenate([flat, jnp.zeros((total - flat.shape[0],), dtype)]).reshape(-1, PACK_COLS)


def _unpack(buf, shapes):
    flat = buf.reshape(-1)
    out, off = [], 0
    for s in shapes:
        n = math.prod(s)
        out.append(flat[off:off + n].reshape(s))
        off += n
    return out


def _flip(v, bit):
    return 1 - v if bit else v


def _all_gather_chips(bufs):
    nb = len(bufs)

    def body(*refs):
        ins, outs = refs[:nb], refs[nb:2 * nb]
        send_sems, recv_sems, local_sems = refs[2 * nb:]
        x, y, c = lax.axis_index("x"), lax.axis_index("y"), lax.axis_index("c")
        me = 2 * x + y
        local = [pltpu.make_async_copy(ins[k], outs[k].at[me], local_sems.at[k]) for k in range(nb)]
        for cp in local:
            cp.start()
        sends = []
        for j, (bx, by) in enumerate(CHIP_FLIPS):
            px, py = _flip(x, bx), _flip(y, by)
            for k in range(nb):
                cp = pltpu.make_async_remote_copy(src_ref=ins[k], dst_ref=outs[k].at[me], send_sem=send_sems.at[j * nb + k],
                                                  recv_sem=recv_sems.at[j * nb + k], device_id=(px, py, c), device_id_type=MESH)
                cp.start()
                sends.append(cp)
        for j, (bx, by) in enumerate(CHIP_FLIPS):
            src = 2 * _flip(x, bx) + _flip(y, by)
            for k in range(nb):
                pltpu.make_async_remote_copy(src_ref=ins[k], dst_ref=outs[k].at[src], send_sem=send_sems.at[j * nb + k],
                                             recv_sem=recv_sems.at[j * nb + k], device_id=(x, y, c), device_id_type=MESH).wait_recv()
        for cp in sends:
            cp.wait_send()
        for cp in local:
            cp.wait()

    return pl.pallas_call(
        body, name="gather_weights", in_specs=[ANY] * nb, out_specs=[ANY] * nb,
        out_shape=[jax.ShapeDtypeStruct((N_CHIPS, *b.shape), b.dtype) for b in bufs],
        scratch_shapes=[pltpu.SemaphoreType.DMA((3 * nb,)), pltpu.SemaphoreType.DMA((3 * nb,)), pltpu.SemaphoreType.DMA((nb,))],
    )(*bufs)


def _exchange_grads(big, small):
    dev_flips = [(bx, by, bc) for bx in (0, 1) for by in (0, 1) for bc in (0, 1)][1:]

    def body(big_ref, small_ref, recv_ref, all_ref, send_sems, recv_sems, local_sem):
        x, y, c = lax.axis_index("x"), lax.axis_index("y"), lax.axis_index("c")
        me = 4 * x + 2 * y + c
        local = pltpu.make_async_copy(small_ref, all_ref.at[me], local_sem)
        local.start()
        sends = []
        for j, (bx, by) in enumerate(CHIP_FLIPS):
            px, py = _flip(x, bx), _flip(y, by)
            cp = pltpu.make_async_remote_copy(src_ref=big_ref.at[2 * px + py], dst_ref=recv_ref.at[j], send_sem=send_sems.at[j],
                                              recv_sem=recv_sems.at[j], device_id=(px, py, c), device_id_type=MESH)
            cp.start()
            sends.append(cp)
        for j, (bx, by, bc) in enumerate(dev_flips):
            cp = pltpu.make_async_remote_copy(src_ref=small_ref, dst_ref=all_ref.at[me], send_sem=send_sems.at[3 + j],
                                              recv_sem=recv_sems.at[3 + j], device_id=(_flip(x, bx), _flip(y, by), _flip(c, bc)),
                                              device_id_type=MESH)
            cp.start()
            sends.append(cp)
        for j in range(3):
            pltpu.make_async_remote_copy(src_ref=big_ref.at[0], dst_ref=recv_ref.at[j], send_sem=send_sems.at[j],
                                         recv_sem=recv_sems.at[j], device_id=(x, y, c), device_id_type=MESH).wait_recv()
        for j, (bx, by, bc) in enumerate(dev_flips):
            src = 4 * _flip(x, bx) + 2 * _flip(y, by) + _flip(c, bc)
            pltpu.make_async_remote_copy(src_ref=small_ref, dst_ref=all_ref.at[src], send_sem=send_sems.at[3 + j],
                                         recv_sem=recv_sems.at[3 + j], device_id=(x, y, c), device_id_type=MESH).wait_recv()
        for cp in sends:
            cp.wait_send()
        local.wait()

    return pl.pallas_call(
        body, name="exchange_grads", in_specs=[ANY, ANY], out_specs=[ANY, ANY],
        out_shape=[jax.ShapeDtypeStruct((3, *big.shape[1:]), big.dtype), jax.ShapeDtypeStruct((N_DEV, *small.shape), small.dtype)],
        scratch_shapes=[pltpu.SemaphoreType.DMA((10,)), pltpu.SemaphoreType.DMA((10,)), pltpu.SemaphoreType.DMA(())],
    )(big, small)


def _swap_cores(part):
    def body(p_ref, o_ref, send_sem, recv_sem):
        x, y, c = lax.axis_index("x"), lax.axis_index("y"), lax.axis_index("c")
        cp = pltpu.make_async_remote_copy(src_ref=p_ref, dst_ref=o_ref, send_sem=send_sem, recv_sem=recv_sem,
                                          device_id=(x, y, 1 - c), device_id_type=MESH)
        cp.start()
        cp.wait()

    return pl.pallas_call(
        body, name="swap_cores", in_specs=[ANY], out_specs=ANY, out_shape=jax.ShapeDtypeStruct(part.shape, part.dtype),
        scratch_shapes=[pltpu.SemaphoreType.DMA(()), pltpu.SemaphoreType.DMA(())],
    )(part)


def _sum_chips_call(own, recv):
    r, c = own.shape

    def body(own_ref, recv_ref, o_ref):
        o_ref[...] = ((own_ref[...] + recv_ref[0].astype(F32)) + recv_ref[1].astype(F32)) + recv_ref[2].astype(F32)

    return pl.pallas_call(
        body, name="sum_chips", grid=(r // PACK_ROWS,),
        in_specs=[pl.BlockSpec((PACK_ROWS, c), lambda i: (i, 0)), pl.BlockSpec((3, PACK_ROWS, c), lambda i: (0, i, 0))],
        out_specs=pl.BlockSpec((PACK_ROWS, c), lambda i: (i, 0)), out_shape=jax.ShapeDtypeStruct((r, c), F32),
        compiler_params=_params(("parallel",)),
    )(own, recv)


def _adamw_math(w, g, m, v):
    m = ADAM_B1 * m + (1.0 - ADAM_B1) * g
    v = ADAM_B2 * v + (1.0 - ADAM_B2) * (g * g)
    m_hat = m / (1.0 - ADAM_B1 ** ADAM_STEP)
    v_hat = v / (1.0 - ADAM_B2 ** ADAM_STEP)
    return -ADAM_LR * (m_hat / (jnp.sqrt(v_hat) + ADAM_EPS) + ADAM_WD * w), m, v


def _adamw_call(w, m, v, gparts, name):
    r, c = w.shape
    n = gparts.shape[0]
    tile = _pick(r, (512, 256, 128, 64, 32, 16, 8))

    def body(w_ref, m_ref, v_ref, g_ref, go_ref, d_ref, mo_ref, vo_ref):
        g = g_ref[0]
        for s in range(1, n):
            g = g + g_ref[s]
        go_ref[...] = g
        d_ref[...], mo_ref[...], vo_ref[...] = _adamw_math(w_ref[...], g, m_ref[...], v_ref[...])

    blk = pl.BlockSpec((tile, c), lambda i: (i, 0))
    return pl.pallas_call(
        body, name=name, grid=(r // tile,), in_specs=[blk, blk, blk, pl.BlockSpec((n, tile, c), lambda i: (0, i, 0))],
        out_specs=[blk] * 4, out_shape=[jax.ShapeDtypeStruct((r, c), F32)] * 4,
        compiler_params=_params(("parallel",)),
    )(w, m, v, gparts)


def _shard_slice(a, axis, d):
    n = a.shape[axis] // N_CHIPS
    return lax.slice_in_dim(a, d * n, (d + 1) * n, axis=axis)


def _train_local(x, p, layers, target):
    y, vjp = jax.vjp(lambda x_, w_: _forward(x_, p, w_), x, layers)
    loss, dy = _loss_call(y, target)
    dx, dw = vjp(dy)
    return loss, dx, dw


def kernel(x, p, w_in, w_merge, b_merge, conv_w, conv_b, conv_norm_g, conv_norm_b, w_pw2, mla_q_norm_g, mla_kv_norm_g, w_uq, w_ukv, ssm_a_re, ssm_a_im, ssm_log_dt, ssm_b_re, ssm_b_im, ssm_c_re, ssm_c_im, ssm_d, w_glu, attn_sinks, w_branch, w_out, ln_g, ln_b, w_ple, w_ple_gate, ple_norm_g, loss_target, m_w_in, m_w_merge, m_b_merge, m_conv_w, m_conv_b, m_conv_norm_g, m_conv_norm_b, m_w_pw2, m_mla_q_norm_g, m_mla_kv_norm_g, m_w_uq, m_w_ukv, m_ssm_a_re, m_ssm_a_im, m_ssm_log_dt, m_ssm_b_re, m_ssm_b_im, m_ssm_c_re, m_ssm_c_im, m_ssm_d, m_w_glu, m_attn_sinks, m_w_branch, m_w_out, m_ln_g, m_ln_b, m_w_ple, m_w_ple_gate, m_ple_norm_g, v_w_in, v_w_merge, v_b_merge, v_conv_w, v_conv_b, v_conv_norm_g, v_conv_norm_b, v_w_pw2, v_mla_q_norm_g, v_mla_kv_norm_g, v_w_uq, v_w_ukv, v_ssm_a_re, v_ssm_a_im, v_ssm_log_dt, v_ssm_b_re, v_ssm_b_im, v_ssm_c_re, v_ssm_c_im, v_ssm_d, v_w_glu, v_attn_sinks, v_w_branch, v_w_out, v_ln_g, v_ln_b, v_w_ple, v_w_ple_gate, v_ple_norm_g):
    given = dict(locals())
    w_loc = {n: given[n] for n in WEIGHTS}
    m_loc = {n: given["m_" + n] for n in WEIGHTS}
    v_loc = {n: given["v_" + n] for n in WEIGHTS}

    mxu_names = [n for n in SHARDED if n != "conv_w"]
    gathered, gathered_taps = _all_gather_chips([_pack([w_loc[n] for n in mxu_names], MXU_DTYPE), _pack([w_loc["conv_w"]], F32)])
    full = dict(w_loc)
    parts = [_unpack(gathered[s], [w_loc[n].shape for n in mxu_names]) for s in range(N_CHIPS)]
    for k, n in enumerate(mxu_names):
        full[n] = jnp.concatenate([parts[s][k] for s in range(N_CHIPS)], axis=SHARD_AXIS[n]).astype(F32)
    full["conv_w"] = jnp.concatenate([_unpack(gathered_taps[s], [conv_w.shape])[0] for s in range(N_CHIPS)], axis=SHARD_AXIS["conv_w"])

    layers = [{n: full[n][i] for n in WEIGHTS} for i in range(DEPTH)]
    loss, dx, dw = _train_local(x[0], p[:, 0], layers, loss_target[0])
    loss = lax.psum(loss, ("x", "y", "c"))

    big32 = jnp.stack([_pack([_shard_slice(dw[i][n], SHARD_AXIS[n] - 1, d) for n in SHARDED for i in range(DEPTH)], F32)
                       for d in range(N_CHIPS)])
    small = _pack([dw[i][n] for n in REPLICATED for i in range(DEPTH)], F32)
    recv, all_small = _exchange_grads(big32.astype(BF16), small)
    own = lax.dynamic_index_in_dim(big32, 2 * lax.axis_index("x") + lax.axis_index("y"), axis=0, keepdims=False)
    part = _sum_chips_call(own, recv)
    other = _swap_cores(part)
    shard_shapes = [w_loc[n].shape for n in SHARDED]
    part_t, other_t = _unpack(part, shard_shapes), _unpack(other, shard_shapes)

    grads, deltas, new_m, new_v = {}, {}, {}, {}
    for k, n in enumerate(SHARDED):
        shape = w_loc[n].shape
        two_d = lambda a: a.reshape(-1, shape[-1])
        res = _adamw_call(two_d(w_loc[n]), two_d(m_loc[n]), two_d(v_loc[n]), jnp.stack([two_d(part_t[k]), two_d(other_t[k])]), "adamw_" + n)
        grads[n], deltas[n], new_m[n], new_v[n] = [r.reshape(shape) for r in res]
    rep_shapes = [w_loc[n].shape for n in REPLICATED]
    res = _adamw_call(_pack([w_loc[n] for n in REPLICATED], F32), _pack([m_loc[n] for n in REPLICATED], F32),
                      _pack([v_loc[n] for n in REPLICATED], F32), all_small, "adamw_replicated")
    for dst, buf in zip((grads, deltas, new_m, new_v), res):
        for n, a in zip(REPLICATED, _unpack(buf, rep_shapes)):
            dst[n] = a

    return (loss, dx[None], *[grads[n] for n in WEIGHTS], *[deltas[n] for n in WEIGHTS],
            *[new_m[n] for n in WEIGHTS], *[new_v[n] for n in WEIGHTS])
```

```python
import functools
import math

import jax
import jax.numpy as jnp
from jax import lax
from jax.experimental import pallas as pl
from jax.experimental.pallas import tpu as pltpu

F32 = jnp.float32
BF16 = jnp.bfloat16
MXU_DTYPE = BF16
V7X_VMEM_BYTES = 64 * 1024 * 1024
VMEM_LIMIT = V7X_VMEM_BYTES * 3 // 4
LANES = 128
SUBLANES = 8

D_MODEL = 1024
DEPTH = 4
BRANCH_W = 256
CONV_W = 31
CONV_HALO = 32
MLA_SCALE = (64 + 32) ** -0.5
SWA_SCALE = 64 ** -0.5
WINDOW = 128
ROPE_THETA = 10000.0
SSM_GROUPS, SSM_GROUP, SSM_STATE = 16, 16, 64
SSM_CH = SSM_GROUPS * SSM_STATE
SCAN_SEGMENTS = SUBLANES
SCAN_CB = 128
DEEPNORM_ALPHA = (2.0 * DEPTH) ** 0.25
LN_EPS = 1e-5
RMS_EPS = 1e-6
ADAM_LR, ADAM_B1, ADAM_B2, ADAM_EPS, ADAM_WD, ADAM_STEP = 0.001, 0.9, 0.999, 1e-08, 0.01, 10
NEG = -1e30
ROW_TILE = 512

NN = (((1,), (0,)), ((), ()))
NT = (((1,), (1,)), ((), ()))
TN = (((0,), (0,)), ((), ()))

MESH = pl.DeviceIdType.MESH
ANY = pl.BlockSpec(memory_space=pl.ANY)


def _dot(a, b, dims):
    return lax.dot_general(a.astype(MXU_DTYPE), b.astype(MXU_DTYPE), dims, preferred_element_type=F32)


def _pick(n, cands):
    for c in cands:
        if n % c == 0:
            return c
    return n


def _params(sem):
    return pltpu.CompilerParams(dimension_semantics=sem, vmem_limit_bytes=VMEM_LIMIT)


def _mm_call(a, b, mode, name, out_dtype=F32):
    if mode == "nn":
        (m, k), (_, n) = a.shape, b.shape
    elif mode == "nt":
        (m, k), (n, _) = a.shape, b.shape
    else:
        (k, m), (_, n) = a.shape, b.shape
    tm = _pick(m, (512, 256, 128))
    tn = _pick(n, (512, 256, 128))
    tk = _pick(k, (1024, 512, 256, 128))
    nk = k // tk
    dims = {"nn": NN, "nt": NT, "tn": TN}[mode]
    a_spec = pl.BlockSpec((tk, tm), lambda i, j, kk: (kk, i)) if mode == "tn" else pl.BlockSpec((tm, tk), lambda i, j, kk: (i, kk))
    b_spec = pl.BlockSpec((tn, tk), lambda i, j, kk: (j, kk)) if mode == "nt" else pl.BlockSpec((tk, tn), lambda i, j, kk: (kk, j))

    def body(a_ref, b_ref, o_ref, acc_ref):
        kk = pl.program_id(2)

        @pl.when(kk == 0)
        def _():
            acc_ref[...] = jnp.zeros_like(acc_ref)

        acc_ref[...] += _dot(a_ref[...], b_ref[...], dims)

        @pl.when(kk == nk - 1)
        def _():
            o_ref[...] = acc_ref[...].astype(out_dtype)

    return pl.pallas_call(
        body, name=name, grid=(m // tm, n // tn, nk),
        in_specs=[a_spec, b_spec], out_specs=pl.BlockSpec((tm, tn), lambda i, j, kk: (i, j)),
        out_shape=jax.ShapeDtypeStruct((m, n), out_dtype),
        scratch_shapes=[pltpu.VMEM((tm, tn), F32)],
        compiler_params=_params(("parallel", "parallel", "arbitrary")),
    )(a, b)


@jax.custom_vjp
def op_mm(a, w):
    return _mm_call(a, w.astype(MXU_DTYPE), "nn", "mm_nn")


def _op_mm_fwd(a, w):
    wb = w.astype(MXU_DTYPE)
    return _mm_call(a, wb, "nn", "mm_nn"), (a, wb, jnp.zeros((0,), w.dtype))


def _op_mm_bwd(res, g):
    a, wb, w_like = res
    return _mm_call(g, wb, "nt", "mm_nt", a.dtype), _mm_call(a, g, "tn", "mm_tn", w_like.dtype)


op_mm.defvjp(_op_mm_fwd, _op_mm_bwd)


def _col_offsets(widths):
    return [sum(widths[:j]) for j in range(len(widths))]


def _proj_fwd_call(x, wb, widths, name):
    t, k = x.shape
    tm = min(ROW_TILE, t)
    offs = _col_offsets(widths)

    def body(x_ref, w_ref, *o_refs):
        xb = x_ref[...].astype(MXU_DTYPE)
        for o_ref, off, wd in zip(o_refs, offs, widths):
            o_ref[...] = _dot(xb, w_ref[:, off:off + wd], NN)

    return pl.pallas_call(
        body, name=name, grid=(t // tm,),
        in_specs=[pl.BlockSpec((tm, k), lambda i: (i, 0)), pl.BlockSpec(wb.shape, lambda i: (0, 0))],
        out_specs=[pl.BlockSpec((tm, wd), lambda i: (i, 0)) for wd in widths],
        out_shape=[jax.ShapeDtypeStruct((t, wd), F32) for wd in widths],
        compiler_params=_params(("parallel",)),
    )(x, wb)


def _proj_dx_call(douts, wb, widths, name):
    t = douts[0].shape[0]
    k = wb.shape[0]
    tm = min(ROW_TILE, t)
    offs = _col_offsets(widths)

    def body(*refs):
        d_refs, w_ref, o_ref = refs[:-2], refs[-2], refs[-1]
        acc = jnp.zeros((tm, k), F32)
        for d_ref, off, wd in zip(d_refs, offs, widths):
            acc = acc + _dot(d_ref[...], w_ref[:, off:off + wd], NT)
        o_ref[...] = acc

    return pl.pallas_call(
        body, name=name, grid=(t // tm,),
        in_specs=[pl.BlockSpec((tm, wd), lambda i: (i, 0)) for wd in widths] + [pl.BlockSpec(wb.shape, lambda i: (0, 0))],
        out_specs=pl.BlockSpec((tm, k), lambda i: (i, 0)), out_shape=jax.ShapeDtypeStruct((t, k), F32),
        compiler_params=_params(("parallel",)),
    )(*douts, wb)


def _proj_dw_call(x, douts, widths, name, out_dtype):
    t, k = x.shape
    tk = min(ROW_TILE // 2, t)
    nk = t // tk
    offs = _col_offsets(widths)
    n = sum(widths)

    def body(x_ref, *refs):
        d_refs, o_ref, acc_ref = refs[:-2], refs[-2], refs[-1]

        @pl.when(pl.program_id(0) == 0)
        def _():
            acc_ref[...] = jnp.zeros_like(acc_ref)

        xb = x_ref[...].astype(MXU_DTYPE)
        for d_ref, off, wd in zip(d_refs, offs, widths):
            acc_ref[:, off:off + wd] += _dot(xb, d_ref[...], TN)

        @pl.when(pl.program_id(0) == nk - 1)
        def _():
            o_ref[...] = acc_ref[...].astype(out_dtype)

    return pl.pallas_call(
        body, name=name, grid=(nk,),
        in_specs=[pl.BlockSpec((tk, k), lambda i: (i, 0))] + [pl.BlockSpec((tk, wd), lambda i: (i, 0)) for wd in widths],
        out_specs=pl.BlockSpec((k, n), lambda i: (0, 0)), out_shape=jax.ShapeDtypeStruct((k, n), out_dtype),
        scratch_shapes=[pltpu.VMEM((k, n), F32)],
        compiler_params=_params(("arbitrary",)),
    )(x, *douts)


def make_proj(widths, name):
    @jax.custom_vjp
    def op(x, w):
        return tuple(_proj_fwd_call(x, w.astype(MXU_DTYPE), widths, name + "_fwd"))

    def fwd(x, w):
        wb = w.astype(MXU_DTYPE)
        return tuple(_proj_fwd_call(x, wb, widths, name + "_fwd")), (x, wb, jnp.zeros((0,), w.dtype))

    def bwd(res, douts):
        x, wb, w_like = res
        return _proj_dx_call(douts, wb, widths, name + "_dx"), _proj_dw_call(x, douts, widths, name + "_dw", w_like.dtype)

    op.defvjp(fwd, bwd)
    return op


@jax.custom_vjp
def _mm(a, w):
    return _dot(a, w, NN)


def _mm_f(a, w):
    return _dot(a, w, NN), (a, w)


def _mm_b(res, g):
    a, w = res
    return _dot(g, w, NT), _dot(a, g, TN)


_mm.defvjp(_mm_f, _mm_b)


@functools.partial(jax.custom_vjp, nondiff_argnums=(1,))
def _roll(x, shift):
    return pltpu.roll(x, shift, 1)


def _roll_f(x, shift):
    return pltpu.roll(x, shift, 1), None


def _roll_b(shift, _, g):
    return (pltpu.roll(g, (g.shape[1] - shift) % g.shape[1], 1),)


_roll.defvjp(_roll_f, _roll_b)


def _ln(x, g, b):
    mu = jnp.mean(x, axis=-1, keepdims=True)
    xc = x - mu
    var = jnp.mean(xc * xc, axis=-1, keepdims=True)
    return xc * lax.rsqrt(var + LN_EPS) * g + b


def _rms(x, g):
    ms = jnp.mean(x * x, axis=-1, keepdims=True)
    return x * lax.rsqrt(ms + RMS_EPS) * g


def _sigmoid(x):
    return jax.nn.sigmoid(x)


def _silu(x):
    return x * _sigmoid(x)


def _gelu_tanh(x):
    return x * (0.5 * (1.0 + jnp.tanh(math.sqrt(2.0 / math.pi) * (x + 0.044715 * (x * x * x)))))


def _rowwise_fwd_call(fn, rows, consts, name, tile):
    t = rows[0].shape[0]
    tile = min(tile, t)
    nr = len(rows)
    outs = jax.eval_shape(fn, *[jax.ShapeDtypeStruct((tile, r.shape[1]), F32) for r in rows],
                          *[jax.ShapeDtypeStruct(c.shape, F32) for c in consts])

    def body(*refs):
        vals = [r[...] for r in refs[:nr + len(consts)]]
        res = fn(*vals)
        for o_ref, o in zip(refs[nr + len(consts):], res):
            o_ref[...] = o

    return pl.pallas_call(
        body, name=name, grid=(t // tile,),
        in_specs=[pl.BlockSpec((tile, r.shape[1]), lambda i: (i, 0)) for r in rows]
        + [pl.BlockSpec(c.shape, lambda i: (0, 0)) for c in consts],
        out_specs=[pl.BlockSpec((tile, o.shape[1]), lambda i: (i, 0)) for o in outs],
        out_shape=[jax.ShapeDtypeStruct((t, o.shape[1]), F32) for o in outs],
        compiler_params=_params(("parallel",)),
    )(*rows, *consts)


def _rowwise_bwd_call(fn, rows, consts, douts, row_diff, name, tile):
    t = rows[0].shape[0]
    tile = min(tile, t)
    nr, nc, nd = len(rows), len(consts), len(douts)
    diff_idx = [i for i in range(nr) if row_diff[i]]

    def body(*refs):
        rv = [r[...] for r in refs[:nr]]
        cv = [r[...] for r in refs[nr:nr + nc]]
        dv = [r[...] for r in refs[nr + nc:nr + nc + nd]]
        out_refs = refs[nr + nc + nd:]

        def f(*diff):
            full = list(rv)
            for k, i in enumerate(diff_idx):
                full[i] = diff[k]
            return fn(*full, *diff[len(diff_idx):])

        _, vjp = jax.vjp(f, *[rv[i] for i in diff_idx], *cv)
        grads = vjp(tuple(dv))
        for k in range(len(diff_idx)):
            out_refs[k][...] = grads[k]
        first = pl.program_id(0) == 0
        for k in range(nc):
            acc_ref = out_refs[len(diff_idx) + k]
            g = grads[len(diff_idx) + k]

            @pl.when(first)
            def _(acc_ref=acc_ref, g=g):
                acc_ref[...] = g

            @pl.when(jnp.logical_not(first))
            def _(acc_ref=acc_ref, g=g):
                acc_ref[...] += g

    res = pl.pallas_call(
        body, name=name, grid=(t // tile,),
        in_specs=[pl.BlockSpec((tile, r.shape[1]), lambda i: (i, 0)) for r in rows]
        + [pl.BlockSpec(c.shape, lambda i: (0, 0)) for c in consts]
        + [pl.BlockSpec((tile, d.shape[1]), lambda i: (i, 0)) for d in douts],
        out_specs=[pl.BlockSpec((tile, rows[i].shape[1]), lambda i_: (i_, 0)) for i in diff_idx]
        + [pl.BlockSpec(c.shape, lambda i: (0, 0)) for c in consts],
        out_shape=[jax.ShapeDtypeStruct(rows[i].shape, F32) for i in diff_idx]
        + [jax.ShapeDtypeStruct(c.shape, F32) for c in consts],
        compiler_params=_params(("arbitrary",)),
    )(*rows, *consts, *douts)
    return res[:len(diff_idx)], res[len(diff_idx):]


def make_rowwise(fn, name, row_diff, tile=ROW_TILE):
    @jax.custom_vjp
    def op(rows, consts):
        return tuple(_rowwise_fwd_call(fn, rows, consts, name + "_fwd", tile))

    def fwd(rows, consts):
        return op(rows, consts), (rows, consts)

    def bwd(res, douts):
        rows, consts = res
        drows, dconsts = _rowwise_bwd_call(fn, rows, consts, douts, row_diff, name + "_bwd", tile)
        it = iter(drows)
        full = tuple(next(it) if row_diff[i] else jnp.zeros_like(rows[i]) for i in range(len(rows)))
        return full, tuple(dconsts)

    op.defvjp(fwd, bwd)
    return op


def _conv_post_fn(cv, a_z, ng, nb, w_pw2):
    return (_mm(_silu(_ln(cv, ng, nb)), w_pw2) * _silu(a_z),)


def _mla_prep_fn(c_q, c_kv, krblk, cos4, sin4, qg, kvg, w_uq, w_uk, w_uv):
    qe = _mm(_rms(c_q, qg), w_uq)
    q = qe * cos4 + _roll(qe, qe.shape[1] - 32) * sin4
    cos1, sin1 = cos4[:, :LANES], sin4[:, :LANES]
    kr = krblk * cos1 + _roll(krblk, LANES - 32) * sin1
    kn = _rms(c_kv, kvg)
    k = _mm(kn, w_uk) + jnp.concatenate([kr, kr, kr, kr], axis=1)
    return q, k, _mm(kn, w_uv)


def _ssm_post_fn(y, u, c_z, d, w_a, w_b):
    y2 = _gelu_tanh(y + d * u)
    return (_mm(y2, w_a) * _sigmoid(_mm(y2, w_b)) * _silu(c_z),)


def _gate_fn(o, z):
    return (o * _silu(z),)


def _merge_fn(br0, br1, br2, br3, gl0, gl1, gl2, gl3, b0, b1, b2, b3):
    return (_sigmoid(gl0 + b0) * br0 + _sigmoid(gl1 + b1) * br1 + _sigmoid(gl2 + b2) * br2 + _sigmoid(gl3 + b3) * br3,)


def _ln_fn(x, mo, g, b):
    return (_ln(DEEPNORM_ALPHA * x + mo, g, b),)


def _ple_fn(x1, pe, gl, g):
    return (x1 + _rms(pe * _sigmoid(gl), g),)


op_conv_post = make_rowwise(_conv_post_fn, "conv_post", (True, True))
op_mla_prep = make_rowwise(_mla_prep_fn, "mla_prep", (True, True, True, False, False))
op_ssm_post = make_rowwise(_ssm_post_fn, "ssm_post", (True, True, True))
op_gate = make_rowwise(_gate_fn, "gate", (True, True))
op_merge = make_rowwise(_merge_fn, "merge", (True,) * 8, tile=ROW_TILE // 2)
op_ln = make_rowwise(_ln_fn, "post_ln", (True, True))
op_ple = make_rowwise(_ple_fn, "ple", (True, True, True))


def _conv_fwd_call(a_val, a_gate, w32, b):
    t, w = a_val.shape
    tile = min(ROW_TILE, t)
    per = tile // CONV_HALO
    cur = pl.BlockSpec((tile, w), lambda i: (i, 0))
    prev = pl.BlockSpec((CONV_HALO, w), lambda i: (jnp.maximum(i * per - 1, 0), 0))

    def body(av_ref, avh_ref, ag_ref, agh_ref, w_ref, b_ref, cv_ref, buf):
        i = pl.program_id(0)
        gh = avh_ref[...] * _sigmoid(agh_ref[...])
        buf[0:CONV_HALO, :] = jnp.where(i > 0, gh, 0.0)
        buf[CONV_HALO:, :] = av_ref[...] * _sigmoid(ag_ref[...])
        acc = jnp.zeros((tile, w), F32) + b_ref[...]
        for j in range(CONV_W):
            acc = acc + w_ref[j:j + 1, :] * buf[pl.ds(CONV_HALO - (CONV_W - 1) + j, tile), :]
        cv_ref[...] = acc

    return pl.pallas_call(
        body, name="conv_fwd", grid=(t // tile,),
        in_specs=[cur, prev, cur, prev, pl.BlockSpec((CONV_HALO, w), lambda i: (0, 0)), pl.BlockSpec((1, w), lambda i: (0, 0))],
        out_specs=cur, out_shape=jax.ShapeDtypeStruct((t, w), F32),
        scratch_shapes=[pltpu.VMEM((tile + CONV_HALO, w), F32)],
        compiler_params=_params(("parallel",)),
    )(a_val, a_val, a_gate, a_gate, w32, b)


def _conv_bwd_call(a_val, a_gate, w32, dcv):
    t, w = a_val.shape
    tile = min(ROW_TILE, t)
    n = t // tile
    per = tile // CONV_HALO
    cur = pl.BlockSpec((tile, w), lambda i: (i, 0))
    prev = pl.BlockSpec((CONV_HALO, w), lambda i: (jnp.maximum(i * per - 1, 0), 0))
    nxt = pl.BlockSpec((CONV_HALO, w), lambda i: (jnp.minimum((i + 1) * per, t // CONV_HALO - 1), 0))
    full = lambda r: pl.BlockSpec((r, w), lambda i: (0, 0))

    def body(av_ref, avh_ref, ag_ref, agh_ref, w_ref, d_ref, dn_ref, dav_ref, dag_ref, dw_ref, db_ref, gbuf, dbuf):
        i = pl.program_id(0)
        gh = avh_ref[...] * _sigmoid(agh_ref[...])
        gbuf[0:CONV_HALO, :] = jnp.where(i > 0, gh, 0.0)
        av = av_ref[...]
        sg = _sigmoid(ag_ref[...])
        gbuf[CONV_HALO:, :] = av * sg
        d = d_ref[...]
        dbuf[0:tile, :] = d
        dbuf[tile:, :] = jnp.where(i < n - 1, dn_ref[...], 0.0)

        @pl.when(i == 0)
        def _():
            dw_ref[...] = jnp.zeros_like(dw_ref)
            db_ref[...] = jnp.zeros_like(db_ref)

        dg = jnp.zeros((tile, w), F32)
        for j in range(CONV_W):
            dg = dg + w_ref[j:j + 1, :] * dbuf[pl.ds(CONV_W - 1 - j, tile), :]
            dw_ref[j:j + 1, :] += jnp.sum(d * gbuf[pl.ds(CONV_HALO - (CONV_W - 1) + j, tile), :], axis=0, keepdims=True)
        db_ref[...] += jnp.sum(d, axis=0, keepdims=True)
        dav_ref[...] = dg * sg
        dag_ref[...] = dg * av * sg * (1.0 - sg)

    return pl.pallas_call(
        body, name="conv_bwd", grid=(n,),
        in_specs=[cur, prev, cur, prev, full(CONV_HALO), cur, nxt],
        out_specs=[cur, cur, full(CONV_HALO), full(1)],
        out_shape=[jax.ShapeDtypeStruct((t, w), F32), jax.ShapeDtypeStruct((t, w), F32),
                   jax.ShapeDtypeStruct((CONV_HALO, w), F32), jax.ShapeDtypeStruct((1, w), F32)],
        scratch_shapes=[pltpu.VMEM((tile + CONV_HALO, w), F32), pltpu.VMEM((tile + CONV_HALO, w), F32)],
        compiler_params=_params(("arbitrary",)),
    )(a_val, a_val, a_gate, a_gate, w32, dcv, dcv)


def _pad_taps(conv_w):
    return jnp.concatenate([conv_w, jnp.zeros((CONV_HALO - CONV_W, conv_w.shape[1]), F32)], axis=0)


@jax.custom_vjp
def op_conv(a_val, a_gate, conv_w, conv_b):
    return _conv_fwd_call(a_val, a_gate, _pad_taps(conv_w), conv_b)


def _op_conv_fwd(a_val, a_gate, conv_w, conv_b):
    return op_conv(a_val, a_gate, conv_w, conv_b), (a_val, a_gate, conv_w)


def _op_conv_bwd(res, dcv):
    a_val, a_gate, conv_w = res
    dav, dag, dw, db = _conv_bwd_call(a_val, a_gate, _pad_taps(conv_w), dcv)
    return dav, dag, dw[:CONV_W], db


op_conv.defvjp(_op_conv_fwd, _op_conv_bwd)


def _head_masks(rows):
    lane = lax.broadcasted_iota(jnp.int32, (rows, LANES), 1)
    return lane < 64, lane >= 64


def _head_row(vals, mask):
    return jnp.max(jnp.where(mask, vals, NEG), axis=1, keepdims=True)


def _attn_valid(qpos, kpos, window):
    valid = kpos <= qpos
    if window is not None:
        valid = jnp.logical_and(valid, qpos - kpos < window)
    return valid


def _flash_fwd_call(q, k, v, sink, *, window, shared_k, scale, blk, name):
    t = q.shape[0]
    qw = LANES if shared_k else 2 * LANES
    pairs = v.shape[1] // LANES
    tq = tk = min(blk, t)
    has_sink = sink is not None

    def body(*refs):
        if has_sink:
            q_ref, k_ref, v_ref, s_ref, o_ref, lse_ref = refs
        else:
            q_ref, k_ref, v_ref, o_ref, lse_ref = refs
        i = pl.program_id(1)
        qb = q_ref[...]
        masks = _head_masks(tq)
        row_masks = _head_masks(1)
        qpos = i * tq + lax.broadcasted_iota(jnp.int32, (tq, tk), 0)
        lo = 0 if window is None else jnp.maximum(i * tq - (window - 1), 0) // tk
        hi = i + 1
        if has_sink:
            m_init = [jnp.zeros((tq, 1), F32) + _head_row(s_ref[...], row_masks[h]) for h in range(2)]
            l_init = [jnp.ones((tq, 1), F32)] * 2
        else:
            m_init = [jnp.full((tq, 1), NEG, F32)] * 2
            l_init = [jnp.zeros((tq, 1), F32)] * 2

        def step(j, carry):
            m0, l0, m1, l1, acc = carry
            start = pl.multiple_of(j * tk, tk)
            kb = k_ref[pl.ds(start, tk), :]
            vb = v_ref[pl.ds(start, tk), :]
            valid = _attn_valid(qpos, j * tk + lax.broadcasted_iota(jnp.int32, (tq, tk), 1), window)
            new, alphas, pv = [], [], []
            for h, (m, l) in enumerate(((m0, l0), (m1, l1))):
                qh = jnp.where(masks[h], qb, 0.0) if shared_k else qb[:, h * LANES:(h + 1) * LANES]
                kh = kb if shared_k else kb[:, h * LANES:(h + 1) * LANES]
                s = jnp.where(valid, _dot(qh, kh, NT) * scale, NEG)
                m_new = jnp.maximum(m, jnp.max(s, axis=1, keepdims=True))
                alpha = jnp.exp(m - m_new)
                p = jnp.exp(s - m_new)
                new += [m_new, alpha * l + jnp.sum(p, axis=1, keepdims=True)]
                alphas.append(alpha)
                pv.append(_dot(p, jnp.where(masks[h], vb, 0.0), NN))
            acc = acc * jnp.where(masks[0], alphas[0], alphas[1]) + pv[0] + pv[1]
            return new[0], new[1], new[2], new[3], acc

        m0, l0, m1, l1, acc = lax.fori_loop(lo, hi, step, (m_init[0], l_init[0], m_init[1], l_init[1], jnp.zeros((tq, LANES), F32)))
        o_ref[...] = acc / jnp.where(masks[0], l0, l1)
        lse_ref[...] = jnp.where(masks[0], m0 + jnp.log(l0), m1 + jnp.log(l1))

    in_specs = [pl.BlockSpec((tq, qw), lambda p, i: (i, p)), pl.BlockSpec((t, qw), lambda p, i: (0, p)),
                pl.BlockSpec((t, LANES), lambda p, i: (0, p))]
    args = [q, k, v]
    if has_sink:
        in_specs.append(pl.BlockSpec((1, LANES), lambda p, i: (0, p)))
        args.append(sink)
    blk_o = pl.BlockSpec((tq, LANES), lambda p, i: (i, p))
    return pl.pallas_call(
        body, name=name, grid=(pairs, t // tq), in_specs=in_specs, out_specs=[blk_o, blk_o],
        out_shape=[jax.ShapeDtypeStruct((t, pairs * LANES), F32)] * 2,
        compiler_params=_params(("parallel", "arbitrary")),
    )(*args)


def _flash_bwd_call(q, k, v, sink, o, lse, do, *, window, shared_k, scale, blk, name):
    t = q.shape[0]
    qw = LANES if shared_k else 2 * LANES
    pairs = v.shape[1] // LANES
    tq = tk = min(blk, t)
    nq = t // tq
    has_sink = sink is not None

    def body(*refs):
        if has_sink:
            q_ref, k_ref, v_ref, o_ref, lse_ref, do_ref, s_ref, dq_ref, dk_ref, dv_ref, ds_ref = refs
        else:
            q_ref, k_ref, v_ref, o_ref, lse_ref, do_ref, dq_ref, dk_ref, dv_ref = refs
        j = pl.program_id(1)
        masks = _head_masks(tq)
        row_masks = _head_masks(1)

        @pl.when(j == 0)
        def _():
            dq_ref[...] = jnp.zeros_like(dq_ref)
            if has_sink:
                full_masks = _head_masks(t)
                prod = do_ref[...] * o_ref[...]
                parts = []
                for h in range(2):
                    dsum = jnp.sum(jnp.where(full_masks[h], prod, 0.0), axis=1, keepdims=True)
                    ps = jnp.exp(_head_row(s_ref[...], row_masks[h]) - _head_row(lse_ref[...], full_masks[h]))
                    parts.append(-jnp.sum(ps * dsum, axis=0, keepdims=True))
                ds_ref[...] = jnp.zeros((SUBLANES, LANES), F32) + jnp.where(row_masks[0], parts[0], parts[1])

        kb = k_ref[...]
        vb = v_ref[...]
        kpos = j * tk + lax.broadcasted_iota(jnp.int32, (tq, tk), 1)
        hi = nq if window is None else jnp.minimum(nq, (j * tk + tk - 1 + window - 1) // tq + 1)

        def step(i, carry):
            dk0, dk1, dv = carry
            start = pl.multiple_of(i * tq, tq)
            qb = q_ref[pl.ds(start, tq), :]
            dob = do_ref[pl.ds(start, tq), :]
            ob = o_ref[pl.ds(start, tq), :]
            lseb = lse_ref[pl.ds(start, tq), :]
            valid = _attn_valid(i * tq + lax.broadcasted_iota(jnp.int32, (tq, tk), 0), kpos, window)
            dks, dqs = [], []
            for h in range(2):
                qh = jnp.where(masks[h], qb, 0.0) if shared_k else qb[:, h * LANES:(h + 1) * LANES]
                kh = kb if shared_k else kb[:, h * LANES:(h + 1) * LANES]
                s = _dot(qh, kh, NT) * scale
                p = jnp.where(valid, jnp.exp(s - _head_row(lseb, masks[h])), 0.0)
                doh = jnp.where(masks[h], dob, 0.0)
                dsum = jnp.sum(doh * ob, axis=1, keepdims=True)
                dp = _dot(doh, vb, NT)
                dsc = p * (dp - dsum) * scale
                dv = dv + _dot(p, doh, TN)
                dks.append(_dot(dsc, qh, TN))
                dq_h = _dot(dsc, kh, NN)
                dqs.append(jnp.where(masks[h], dq_h, 0.0) if shared_k else dq_h)
            if shared_k:
                dq_ref[pl.ds(start, tq), :] += dqs[0] + dqs[1]
            else:
                dq_ref[pl.ds(start, tq), :] += jnp.concatenate(dqs, axis=1)
            return dk0 + dks[0], dk1 + dks[1], dv

        zero = jnp.zeros((tk, LANES), F32)
        dk0, dk1, dv = lax.fori_loop(j, hi, step, (zero, zero, zero))
        dk_ref[...] = dk0 + dk1 if shared_k else jnp.concatenate([dk0, dk1], axis=1)
        dv_ref[...] = dv

    full = lambda w: pl.BlockSpec((t, w), lambda p, j: (0, p))
    blkspec = lambda w: pl.BlockSpec((tk, w), lambda p, j: (j, p))
    in_specs = [full(qw), blkspec(qw), blkspec(LANES), full(LANES), full(LANES), full(LANES)]
    args = [q, k, v, o, lse, do]
    out_specs = [full(qw), blkspec(qw), blkspec(LANES)]
    out_shape = [jax.ShapeDtypeStruct(q.shape, F32), jax.ShapeDtypeStruct(k.shape, F32), jax.ShapeDtypeStruct(v.shape, F32)]
    if has_sink:
        in_specs.append(pl.BlockSpec((1, LANES), lambda p, j: (0, p)))
        args.append(sink)
        out_specs.append(pl.BlockSpec((SUBLANES, LANES), lambda p, j: (0, p)))
        out_shape.append(jax.ShapeDtypeStruct((SUBLANES, pairs * LANES), F32))
    return pl.pallas_call(
        body, name=name, grid=(pairs, t // tk), in_specs=in_specs, out_specs=out_specs, out_shape=out_shape,
        compiler_params=_params(("arbitrary", "arbitrary")),
    )(*args)


_MLA_CFG = dict(window=None, shared_k=False, scale=MLA_SCALE, blk=256)
_SWA_CFG = dict(window=WINDOW, shared_k=True, scale=SWA_SCALE, blk=128)


@jax.custom_vjp
def op_mla_attn(q, k, v):
    return _flash_fwd_call(q, k, v, None, name="mla_fwd", **_MLA_CFG)[0]


def _op_mla_attn_fwd(q, k, v):
    o, lse = _flash_fwd_call(q, k, v, None, name="mla_fwd", **_MLA_CFG)
    return o, (q, k, v, o, lse)


def _op_mla_attn_bwd(res, do):
    q, k, v, o, lse = res
    return tuple(_flash_bwd_call(q, k, v, None, o, lse, do, name="mla_bwd", **_MLA_CFG))


op_mla_attn.defvjp(_op_mla_attn_fwd, _op_mla_attn_bwd)


@jax.custom_vjp
def op_swa_attn(q, k, v, sink):
    return _flash_fwd_call(q, k, v, sink, name="swa_fwd", **_SWA_CFG)[0]


def _op_swa_attn_fwd(q, k, v, sink):
    o, lse = _flash_fwd_call(q, k, v, sink, name="swa_fwd", **_SWA_CFG)
    return o, (q, k, v, sink, o, lse)


def _op_swa_attn_bwd(res, do):
    q, k, v, sink, o, lse = res
    dq, dk, dv, dsink = _flash_bwd_call(q, k, v, sink, o, lse, do, name="swa_bwd", **_SWA_CFG)
    first_lane = lax.broadcasted_iota(jnp.int32, (1, dsink.shape[1]), 1) % 64 == 0
    return dq, dk, dv, jnp.where(first_lane, dsink[:1], 0.0)


op_swa_attn.defvjp(_op_swa_attn_fwd, _op_swa_attn_bwd)


def _complex_power(ar, ai, n):
    for _ in range(int(math.log2(n))):
        ar, ai = ar * ar - ai * ai, 2.0 * ar * ai
    return ar, ai


def _scan_passes(load_b, a1r, a1i, n, store, e_ref, c_ref, reverse):
    cb = a1r.shape[1]
    ar = jnp.zeros((SCAN_SEGMENTS, cb), F32) + a1r
    ai = jnp.zeros((SCAN_SEGMENTS, cb), F32) + a1i
    idx = (lambda ii: n - 1 - ii) if reverse else (lambda ii: ii)

    def local(ii, h):
        br, bi = load_b(idx(ii))
        return ar * h[0] - ai * h[1] + br, ar * h[1] + ai * h[0] + bi

    zero = jnp.zeros((SCAN_SEGMENTS, cb), F32)
    er, ei = lax.fori_loop(0, n, local, (zero, zero))
    e_ref[:, 0:cb] = er
    e_ref[:, cb:] = ei
    pr, pi_ = _complex_power(a1r, a1i, n)
    cr = jnp.zeros((1, cb), F32)
    ci = jnp.zeros((1, cb), F32)
    order = range(SCAN_SEGMENTS - 1, -1, -1) if reverse else range(SCAN_SEGMENTS)
    for s in order:
        c_ref[s:s + 1, 0:cb] = cr
        c_ref[s:s + 1, cb:] = ci
        er1, ei1 = e_ref[s:s + 1, 0:cb], e_ref[s:s + 1, cb:]
        cr, ci = pr * cr - pi_ * ci + er1, pr * ci + pi_ * cr + ei1

    def second(ii, h):
        i = idx(ii)
        hr, hi = local(ii, h)
        store(i, hr, hi)
        return hr, hi

    lax.fori_loop(0, n, second, (c_ref[:, 0:cb], c_ref[:, cb:]))


def _scan_fwd_call(bu, lam):
    n = bu.shape[0]
    cb = SCAN_CB
    blk3 = pl.BlockSpec((n, SCAN_SEGMENTS, 2 * cb), lambda c: (0, 0, c))
    blk2 = lambda r: pl.BlockSpec((r, 2 * cb), lambda c: (0, c))

    def body(b_ref, lam_ref, h_ref, cin_ref, e_ref):
        def store(i, hr, hi):
            h_ref[i, :, 0:cb] = hr
            h_ref[i, :, cb:] = hi

        _scan_passes(lambda i: (b_ref[i, :, 0:cb], b_ref[i, :, cb:]), lam_ref[:, 0:cb], lam_ref[:, cb:], n, store,
                     e_ref, cin_ref, False)

    return pl.pallas_call(
        body, name="scan_fwd", grid=(SSM_CH // cb,), in_specs=[blk3, blk2(1)], out_specs=[blk3, blk2(SCAN_SEGMENTS)],
        out_shape=[jax.ShapeDtypeStruct(bu.shape, F32), jax.ShapeDtypeStruct((SCAN_SEGMENTS, 2 * SSM_CH), F32)],
        scratch_shapes=[pltpu.VMEM((SCAN_SEGMENTS, 2 * cb), F32)],
        compiler_params=_params(("parallel",)),
    )(bu, lam)


def _scan_bwd_call(dh, h, cin, lam):
    n = dh.shape[0]
    cb = SCAN_CB
    blk3 = pl.BlockSpec((n, SCAN_SEGMENTS, 2 * cb), lambda c: (0, 0, c))
    blk2 = lambda r: pl.BlockSpec((r, 2 * cb), lambda c: (0, c))

    def body(d_ref, h_ref, cin_ref, lam_ref, g_ref, dlam_ref, e_ref, c_ref, acc_ref):
        acc_ref[...] = jnp.zeros_like(acc_ref)

        def store(i, gr, gi):
            g_ref[i, :, 0:cb] = gr
            g_ref[i, :, cb:] = gi
            ip = jnp.maximum(i - 1, 0)
            hpr = jnp.where(i > 0, h_ref[ip, :, 0:cb], cin_ref[:, 0:cb])
            hpi = jnp.where(i > 0, h_ref[ip, :, cb:], cin_ref[:, cb:])
            acc_ref[:, 0:cb] += gr * hpr + gi * hpi
            acc_ref[:, cb:] += gi * hpr - gr * hpi

        _scan_passes(lambda i: (d_ref[i, :, 0:cb], d_ref[i, :, cb:]), lam_ref[:, 0:cb], -lam_ref[:, cb:], n, store,
                     e_ref, c_ref, True)
        dlam_ref[...] = acc_ref[...]

    return pl.pallas_call(
        body, name="scan_bwd", grid=(SSM_CH // cb,), in_specs=[blk3, blk3, blk2(SCAN_SEGMENTS), blk2(1)],
        out_specs=[blk3, blk2(SCAN_SEGMENTS)],
        out_shape=[jax.ShapeDtypeStruct(dh.shape, F32), jax.ShapeDtypeStruct((SCAN_SEGMENTS, 2 * SSM_CH), F32)],
        scratch_shapes=[pltpu.VMEM((SCAN_SEGMENTS, 2 * cb), F32)] * 3,
        compiler_params=_params(("parallel",)),
    )(dh, h, cin, lam)


@jax.custom_vjp
def op_scan(bu, lam):
    return _scan_fwd_call(bu, lam)[0]


def _op_scan_fwd(bu, lam):
    h, cin = _scan_fwd_call(bu, lam)
    return h, (h, cin, lam)


def _op_scan_bwd(res, dh):
    h, cin, lam = res
    g, dlam = _scan_bwd_call(dh, h, cin, lam)
    return g, jnp.sum(dlam, axis=0, keepdims=True)


op_scan.defvjp(_op_scan_fwd, _op_scan_bwd)


def _loss_call(y, target):
    t, d = y.shape
    tile = min(ROW_TILE, t)

    def body(y_ref, t_ref, dy_ref, acc_ref):
        @pl.when(pl.program_id(0) == 0)
        def _():
            acc_ref[...] = jnp.zeros_like(acc_ref)

        err = y_ref[...] - t_ref[...]
        dy_ref[...] = err * (1.0 / d)
        col = jnp.sum(err * err, axis=0, keepdims=True)
        part = col[:, 0:LANES]
        for c in range(1, d // LANES):
            part = part + col[:, c * LANES:(c + 1) * LANES]
        acc_ref[0:1, :] += part

    blk = pl.BlockSpec((tile, d), lambda i: (i, 0))
    dy, acc = pl.pallas_call(
        body, name="loss_head", grid=(t // tile,), in_specs=[blk, blk],
        out_specs=[blk, pl.BlockSpec((SUBLANES, LANES), lambda i: (0, 0))],
        out_shape=[jax.ShapeDtypeStruct((t, d), F32), jax.ShapeDtypeStruct((SUBLANES, LANES), F32)],
        compiler_params=_params(("arbitrary",)),
    )(y, target)
    return jnp.sum(acc) * (0.5 / d), dy


def _rot_cols(w):
    return jnp.concatenate([-w[:, 16:], w[:, :16]], axis=1)


def _ext_w_in(w):
    a_val, a_gate, a_z, c_q, c_kv, k_r, b_z, u, c_z, q, k, v, d_z = jnp.split(
        w, (256, 512, 768, 1024, 1152, 1184, 1440, 1696, 1952, 2208, 2336, 2464), axis=1)
    dup = lambda m: jnp.concatenate([m[:, :64], m[:, :64], m[:, 64:], m[:, 64:]], axis=1)
    krblk = jnp.concatenate([jnp.zeros((w.shape[0], 64), w.dtype), k_r, _rot_cols(k_r)], axis=1)
    return jnp.concatenate([a_val, a_gate, a_z, c_q, b_z, u, c_z, q, dup(k), dup(v), d_z, c_kv, krblk], axis=1)


IN_WIDTH = 2720
IN_SHARD = IN_WIDTH // 4
IN_SHARD_PAD = 768
IN_EXT = 3072


def _w_in_layout_matrix():
    src = _ext_w_in(jnp.arange(1, IN_WIDTH + 1, dtype=F32)[None, :])[0]
    col = jnp.abs(src).astype(jnp.int32) - 1
    row_of_col = (col // IN_SHARD) * IN_SHARD_PAD + col % IN_SHARD
    rows = lax.broadcasted_iota(jnp.int32, (4 * IN_SHARD_PAD, IN_EXT), 0)
    return jnp.where(rows == row_of_col[None, :], jnp.sign(src)[None, :], 0.0).astype(MXU_DTYPE)


H_COLS = dict(a_val=256, a_gate=256, a_z=256, c_q=256, b_z=256, u=256, c_z=256, q=256, kdup=256, vdup=256, d_z=256,
              c_kv=128, krblk=128)
op_in_proj = make_proj(tuple(H_COLS.values()), "in_proj")
op_merge_proj = make_proj((D_MODEL,) * 4, "merge_proj")


def _ext_mla(w_uq, w_ukv):
    zeros = jnp.zeros((w_ukv.shape[0], 64), w_ukv.dtype)
    uq, uk, uv = [], [], []
    for h in range(4):
        nope, rp = w_uq[:, 96 * h:96 * h + 64], w_uq[:, 96 * h + 64:96 * h + 96]
        uq += [nope, rp, _rot_cols(rp)]
        uk += [w_ukv[:, 128 * h:128 * h + 64], zeros]
        uv.append(w_ukv[:, 128 * h + 64:128 * h + 128])
    return jnp.concatenate(uq, axis=1), jnp.concatenate(uk, axis=1), jnp.concatenate(uv, axis=1)


def _scan_cols(re, im):
    parts = []
    for c in range(SSM_CH // SCAN_CB):
        parts += [re[..., c * SCAN_CB:(c + 1) * SCAN_CB], im[..., c * SCAN_CB:(c + 1) * SCAN_CB]]
    return jnp.concatenate(parts, axis=-1)


def _ext_ssm(a_re, a_im, log_dt, b_re, b_im, c_re, c_im):
    dt = jnp.exp(log_dt)[:, None]
    mag = jnp.exp(a_re * dt)
    lb_re, lb_im = mag * jnp.cos(a_im * dt), mag * jnp.sin(a_im * dt)
    den = a_re * a_re + a_im * a_im
    nr, ni = lb_re - 1.0, lb_im
    f_re = ((nr * a_re + ni * a_im) / den)[..., None]
    f_im = ((ni * a_re - nr * a_im) / den)[..., None]
    bb_re = f_re * b_re - f_im * b_im
    bb_im = f_re * b_im + f_im * b_re
    eye = jnp.eye(SSM_GROUPS, dtype=F32)
    spread = lambda a: a.transpose(0, 2, 1)[:, :, None, :] * eye[:, None, :, None]
    bd_in = lambda bb: spread(bb).reshape(SSM_GROUPS * SSM_GROUP, SSM_CH)
    bd_out = lambda cc: spread(cc).reshape(SSM_CH, SSM_GROUPS * SSM_GROUP)
    w_bu = _scan_cols(bd_in(bb_re), bd_in(bb_im))
    w_y = _scan_cols(bd_out(c_re).T, -bd_out(c_im).T).T
    lam = _scan_cols(lb_re.reshape(1, SSM_CH), lb_im.reshape(1, SSM_CH))
    return w_bu, w_y, lam


def _rope_tables(t):
    pos = jnp.arange(t, dtype=F32)
    inv_freq = ROPE_THETA ** (-jnp.arange(0, 32, 2, dtype=F32) / 32)
    ang = pos[:, None] * inv_freq[None, :]
    cos, sin = jnp.cos(ang), jnp.sin(ang)
    ones, z32, z64 = jnp.ones((t, 64), F32), jnp.zeros((t, 32), F32), jnp.zeros((t, 64), F32)
    cos1 = jnp.concatenate([ones, cos, cos, z32], axis=1)
    sin1 = jnp.concatenate([z64, sin, sin, z32], axis=1)
    return jnp.concatenate([cos1] * 4, axis=1), jnp.concatenate([sin1] * 4, axis=1)


def _to_segments(a):
    t, w = a.shape
    return a.reshape(SCAN_SEGMENTS, t // SCAN_SEGMENTS, w).transpose(1, 0, 2)


def _from_segments(a):
    n, s, w = a.shape
    return a.transpose(1, 0, 2).reshape(n * s, w)


def _layer(x, p_i, cos4, sin4, e_mat, w):
    t = x.shape[0]
    row = lambda v: v.reshape(1, -1)
    f32 = lambda v: v.astype(F32)
    hs = dict(zip(H_COLS, op_in_proj(x, op_mm(w["w_in_pad"], e_mat))))

    cv = op_conv(hs["a_val"], hs["a_gate"], w["conv_w"], row(w["conv_b"]))
    (y_a,) = op_conv_post((cv, hs["a_z"]), (row(w["conv_norm_g"]), row(w["conv_norm_b"]), f32(w["w_pw2"])))

    w_uq, w_uk, w_uv = _ext_mla(w["w_uq"], f32(w["w_ukv"]))
    q, k, v = op_mla_prep((hs["c_q"], hs["c_kv"], hs["krblk"], cos4, sin4),
                          (row(w["mla_q_norm_g"]), row(w["mla_kv_norm_g"]), w_uq, w_uk, w_uv))
    (y_b,) = op_gate((op_mla_attn(q, k, v), hs["b_z"]), ())

    w_bu, w_y, lam = _ext_ssm(w["ssm_a_re"], w["ssm_a_im"], w["ssm_log_dt"], w["ssm_b_re"], w["ssm_b_im"],
                              w["ssm_c_re"], w["ssm_c_im"])
    u_seg = _to_segments(hs["u"]).reshape(t, BRANCH_W)
    bu = op_mm(u_seg, w_bu).reshape(t // SCAN_SEGMENTS, SCAN_SEGMENTS, 2 * SSM_CH)
    hstate = op_scan(bu, lam).reshape(t, 2 * SSM_CH)
    y_ssm = _from_segments(op_mm(hstate, w_y).reshape(t // SCAN_SEGMENTS, SCAN_SEGMENTS, BRANCH_W))
    w_glu = f32(w["w_glu"])
    (y_c,) = op_ssm_post((y_ssm, hs["u"], hs["c_z"]), (row(w["ssm_d"]), w_glu[:, :BRANCH_W], w_glu[:, BRANCH_W:]))

    sink = jnp.repeat(w["attn_sinks"], 64).reshape(1, 2 * LANES)
    (y_d,) = op_gate((op_swa_attn(hs["q"], hs["kdup"], hs["vdup"], sink), hs["d_z"]), ())

    br = [op_mm(y, w["w_branch"][n]) for n, y in enumerate((y_a, y_b, y_c, y_d))]
    gl = op_merge_proj(x, w["w_merge"])
    bm = [row(w["b_merge"][n * D_MODEL:(n + 1) * D_MODEL]) for n in range(4)]
    (merged,) = op_merge((*br, *gl), tuple(bm))
    (x1,) = op_ln((x, op_mm(merged, w["w_out"])), (row(w["ln_g"]), row(w["ln_b"])))
    (out,) = op_ple((x1, op_mm(p_i, w["w_ple"]), op_mm(x1, w["w_ple_gate"])), (row(w["ple_norm_g"]),))
    return out


def _forward(x, p, layers):
    cos4, sin4 = _rope_tables(x.shape[0])
    e_mat = _w_in_layout_matrix()
    for i in range(DEPTH):
        x = _layer(x, p[i], cos4, sin4, e_mat, layers[i])
    return x


SHARD_AXIS = dict(w_in=2, w_merge=2, conv_w=2, w_pw2=1, w_uq=2, w_ukv=2, w_glu=2, w_branch=3, w_out=1, w_ple=2, w_ple_gate=1)
ODD = ("w_uq", "conv_w")
BIG = tuple(n for n in SHARD_AXIS if n not in ODD)
REPLICATED = ("b_merge", "conv_b", "conv_norm_g", "conv_norm_b", "mla_q_norm_g", "mla_kv_norm_g", "ssm_a_re", "ssm_a_im",
              "ssm_log_dt", "ssm_b_re", "ssm_b_im", "ssm_c_re", "ssm_c_im", "ssm_d", "attn_sinks", "ln_g", "ln_b", "ple_norm_g")
WEIGHTS = ("w_in", "w_merge", "b_merge", "conv_w", "conv_b", "conv_norm_g", "conv_norm_b", "w_pw2", "mla_q_norm_g",
           "mla_kv_norm_g", "w_uq", "w_ukv", "ssm_a_re", "ssm_a_im", "ssm_log_dt", "ssm_b_re", "ssm_b_im", "ssm_c_re",
           "ssm_c_im", "ssm_d", "w_glu", "attn_sinks", "w_branch", "w_out", "ln_g", "ln_b", "w_ple", "w_ple_gate", "ple_norm_g")
PACK_COLS = 1024
PACK_ROWS = 512
CHIP_FLIPS = ((1, 0), (0, 1), (1, 1))
N_CHIPS = 4
N_DEV = 8


def _pack(arrays, dtype):
    flat = jnp.concatenate([a.reshape(-1).astype(dtype) for a in arrays])
    unit = PACK_ROWS * PACK_COLS
    total = -(-flat.shape[0] // unit) * unit
    return jnp.concatenate([flat, jnp.zeros((total - flat.shape[0],), dtype)]).reshape(-1, PACK_COLS)


def _unpack(buf, shapes):
    flat = buf.reshape(-1)
    out, off = [], 0
    for s in shapes:
        n = math.prod(s)
        out.append(flat[off:off + n].reshape(s))
        off += n
    return out


def _flip(v, bit):
    return 1 - v if bit else v


def _window(ref, axis, start, size):
    idx = [slice(None)] * len(ref.shape)
    idx[axis] = pl.ds(start, size)
    return ref.at[tuple(idx)]


def _gather_chips(srcs, axes):
    nb = len(srcs)
    sizes = [s.shape[a] for s, a in zip(srcs, axes)]

    def body(*refs):
        ins, outs = refs[:nb], refs[nb:2 * nb]
        send_sems, recv_sems, local_sems = refs[2 * nb:]
        x, y, c = lax.axis_index("x"), lax.axis_index("y"), lax.axis_index("c")
        me = 2 * x + y
        place = lambda k, chip: _window(outs[k], axes[k], chip * sizes[k], sizes[k])
        local = [pltpu.make_async_copy(ins[k], place(k, me), local_sems.at[k]) for k in range(nb)]
        for cp in local:
            cp.start()
        sends = []
        for j, (bx, by) in enumerate(CHIP_FLIPS):
            for k in range(nb):
                cp = pltpu.make_async_remote_copy(src_ref=ins[k], dst_ref=place(k, me), send_sem=send_sems.at[j * nb + k],
                                                  recv_sem=recv_sems.at[j * nb + k], device_id=(_flip(x, bx), _flip(y, by), c),
                                                  device_id_type=MESH)
                cp.start()
                sends.append(cp)
        for j, (bx, by) in enumerate(CHIP_FLIPS):
            src = 2 * _flip(x, bx) + _flip(y, by)
            for k in range(nb):
                pltpu.make_async_remote_copy(src_ref=ins[k], dst_ref=place(k, src), send_sem=send_sems.at[j * nb + k],
                                             recv_sem=recv_sems.at[j * nb + k], device_id=(x, y, c), device_id_type=MESH).wait_recv()
        for cp in sends:
            cp.wait_send()
        for cp in local:
            cp.wait()

    full = lambda s, a: tuple(N_CHIPS * d if i == a else d for i, d in enumerate(s.shape))
    return pl.pallas_call(
        body, name="gather_weights", in_specs=[ANY] * nb, out_specs=[ANY] * nb,
        out_shape=[jax.ShapeDtypeStruct(full(s, a), s.dtype) for s, a in zip(srcs, axes)],
        scratch_shapes=[pltpu.SemaphoreType.DMA((3 * nb,)), pltpu.SemaphoreType.DMA((3 * nb,)), pltpu.SemaphoreType.DMA((nb,))],
    )(*srcs)


def _exchange_grads(grads, axes, smalls):
    nt, ns = len(grads), len(smalls)
    sizes = [g[0].shape[a] // N_CHIPS for g, a in zip(grads, axes)]
    dev_flips = [(bx, by, bc) for bx in (0, 1) for by in (0, 1) for bc in (0, 1)][1:]
    n_remote = 3 * nt * DEPTH + 7 * ns
    n_local = nt * DEPTH + ns

    def body(*refs):
        g_refs = [refs[k * DEPTH:(k + 1) * DEPTH] for k in range(nt)]
        s_refs = refs[nt * DEPTH:nt * DEPTH + ns]
        outs = refs[nt * DEPTH + ns:nt * DEPTH + ns + nt + ns]
        recv_refs, all_refs = outs[:nt], outs[nt:]
        send_sems, recv_sems, local_sems = refs[-3:]
        x, y, c = lax.axis_index("x"), lax.axis_index("y"), lax.axis_index("c")
        me_chip = 2 * x + y
        me = 4 * x + 2 * y + c
        part = lambda k, i, chip: _window(g_refs[k][i], axes[k], chip * sizes[k], sizes[k])
        started, waits = [], []
        sem, lsem = 0, 0
        for k in range(nt):
            for i in range(DEPTH):
                cp = pltpu.make_async_copy(part(k, i, me_chip), recv_refs[k].at[i, 3], local_sems.at[lsem])
                cp.start()
                started.append(cp.wait)
                lsem += 1
                for j, (bx, by) in enumerate(CHIP_FLIPS):
                    px, py = _flip(x, bx), _flip(y, by)
                    cp = pltpu.make_async_remote_copy(src_ref=part(k, i, 2 * px + py), dst_ref=recv_refs[k].at[i, j],
                                                      send_sem=send_sems.at[sem], recv_sem=recv_sems.at[sem],
                                                      device_id=(px, py, c), device_id_type=MESH)
                    cp.start()
                    started.append(cp.wait_send)
                    waits.append(cp.wait_recv)
                    sem += 1
        for s in range(ns):
            cp = pltpu.make_async_copy(s_refs[s], all_refs[s].at[me], local_sems.at[lsem])
            cp.start()
            started.append(cp.wait)
            lsem += 1
            for bx, by, bc in dev_flips:
                peer = (_flip(x, bx), _flip(y, by), _flip(c, bc))
                cp = pltpu.make_async_remote_copy(src_ref=s_refs[s], dst_ref=all_refs[s].at[me], send_sem=send_sems.at[sem],
                                                  recv_sem=recv_sems.at[sem], device_id=peer, device_id_type=MESH)
                cp.start()
                started.append(cp.wait_send)
                src = 4 * peer[0] + 2 * peer[1] + peer[2]
                waits.append(pltpu.make_async_remote_copy(src_ref=s_refs[s], dst_ref=all_refs[s].at[src], send_sem=send_sems.at[sem],
                                                          recv_sem=recv_sems.at[sem], device_id=peer, device_id_type=MESH).wait_recv)
                sem += 1
        for w in waits + started:
            w()

    shard = lambda g, a: tuple(d // N_CHIPS if i == a else d for i, d in enumerate(g.shape))
    flat = [g for per_layer in grads for g in per_layer]
    return pl.pallas_call(
        body, name="exchange_grads", in_specs=[ANY] * (len(flat) + ns), out_specs=[ANY] * (nt + ns),
        out_shape=[jax.ShapeDtypeStruct((DEPTH, N_CHIPS, *shard(g[0], a)), g[0].dtype) for g, a in zip(grads, axes)]
        + [jax.ShapeDtypeStruct((N_DEV, *s.shape), s.dtype) for s in smalls],
        scratch_shapes=[pltpu.SemaphoreType.DMA((n_remote,)), pltpu.SemaphoreType.DMA((n_remote,)), pltpu.SemaphoreType.DMA((n_local,))],
    )(*flat, *smalls)


def _swap_cores(parts):
    nb = len(parts)

    def body(*refs):
        ins, outs, send_sems, recv_sems = refs[:nb], refs[nb:2 * nb], refs[-2], refs[-1]
        x, y, c = lax.axis_index("x"), lax.axis_index("y"), lax.axis_index("c")
        cps = [pltpu.make_async_remote_copy(src_ref=ins[k], dst_ref=outs[k], send_sem=send_sems.at[k], recv_sem=recv_sems.at[k],
                                            device_id=(x, y, 1 - c), device_id_type=MESH) for k in range(nb)]
        for cp in cps:
            cp.start()
        for cp in cps:
            cp.wait()

    return pl.pallas_call(
        body, name="swap_cores", in_specs=[ANY] * nb, out_specs=[ANY] * nb,
        out_shape=[jax.ShapeDtypeStruct(q.shape, q.dtype) for q in parts],
        scratch_shapes=[pltpu.SemaphoreType.DMA((nb,)), pltpu.SemaphoreType.DMA((nb,))],
    )(*parts)


def _sum_chips_call(recv, cols, name):
    depth, _, r, c = recv.shape
    tile = _pick(r, (512, 256, 128, 64, 32, 16))

    def body(r_ref, o_ref):
        slot = lambda s: r_ref[s, :, pl.ds(0, cols)].astype(F32)
        o_ref[...] = ((slot(3) + slot(0)) + slot(1)) + slot(2)

    return pl.pallas_call(
        body, name=name, grid=(depth, r // tile),
        in_specs=[pl.BlockSpec((None, N_CHIPS, tile, c), lambda l, i: (l, 0, i, 0))],
        out_specs=pl.BlockSpec((None, tile, cols), lambda l, i: (l, i, 0)), out_shape=jax.ShapeDtypeStruct((depth, r, cols), F32),
        compiler_params=_params(("parallel", "parallel")),
    )(recv)


def _sum_slots_call(slots, name):
    n, r, c = slots.shape
    tile = _pick(r, (512, 256, 128, 64, 32, 16, 8))

    def body(s_ref, o_ref):
        acc = s_ref[0]
        for s in range(1, n):
            acc = acc + s_ref[s]
        o_ref[...] = acc

    return pl.pallas_call(
        body, name=name, grid=(r // tile,), in_specs=[pl.BlockSpec((n, tile, c), lambda i: (0, i, 0))],
        out_specs=pl.BlockSpec((tile, c), lambda i: (i, 0)), out_shape=jax.ShapeDtypeStruct((r, c), F32),
        compiler_params=_params(("parallel",)),
    )(slots)


def _adamw_math(w, g, m, v):
    m = ADAM_B1 * m + (1.0 - ADAM_B1) * g
    v = ADAM_B2 * v + (1.0 - ADAM_B2) * (g * g)
    m_hat = m / (1.0 - ADAM_B1 ** ADAM_STEP)
    v_hat = v / (1.0 - ADAM_B2 ** ADAM_STEP)
    return -ADAM_LR * (m_hat / (jnp.sqrt(v_hat) + ADAM_EPS) + ADAM_WD * w), m, v


def _adamw_call(w, m, v, gparts, name):
    r, c = w.shape
    n = len(gparts)
    tile = _pick(r, (512, 256, 128, 64, 32, 16, 8))

    def body(w_ref, m_ref, v_ref, *refs):
        g_refs, (go_ref, d_ref, mo_ref, vo_ref) = refs[:n], refs[n:]
        g = g_refs[0][...]
        for g_ref in g_refs[1:]:
            g = g + g_ref[...]
        go_ref[...] = g
        d_ref[...], mo_ref[...], vo_ref[...] = _adamw_math(w_ref[...], g, m_ref[...], v_ref[...])

    blk = pl.BlockSpec((tile, c), lambda i: (i, 0))
    return pl.pallas_call(
        body, name=name, grid=(r // tile,), in_specs=[blk] * (3 + n),
        out_specs=[blk] * 4, out_shape=[jax.ShapeDtypeStruct((r, c), F32)] * 4,
        compiler_params=_params(("parallel",)),
    )(w, m, v, *gparts)


def _train_local(x, p, layers, target):
    y, vjp = jax.vjp(lambda x_, w_: _forward(x_, p, w_), x, layers)
    loss, dy = _loss_call(y, target)
    dx, dw = vjp(dy)
    return loss, dx, dw


def kernel(x, p, w_in, w_merge, b_merge, conv_w, conv_b, conv_norm_g, conv_norm_b, w_pw2, mla_q_norm_g, mla_kv_norm_g, w_uq, w_ukv, ssm_a_re, ssm_a_im, ssm_log_dt, ssm_b_re, ssm_b_im, ssm_c_re, ssm_c_im, ssm_d, w_glu, attn_sinks, w_branch, w_out, ln_g, ln_b, w_ple, w_ple_gate, ple_norm_g, loss_target, m_w_in, m_w_merge, m_b_merge, m_conv_w, m_conv_b, m_conv_norm_g, m_conv_norm_b, m_w_pw2, m_mla_q_norm_g, m_mla_kv_norm_g, m_w_uq, m_w_ukv, m_ssm_a_re, m_ssm_a_im, m_ssm_log_dt, m_ssm_b_re, m_ssm_b_im, m_ssm_c_re, m_ssm_c_im, m_ssm_d, m_w_glu, m_attn_sinks, m_w_branch, m_w_out, m_ln_g, m_ln_b, m_w_ple, m_w_ple_gate, m_ple_norm_g, v_w_in, v_w_merge, v_b_merge, v_conv_w, v_conv_b, v_conv_norm_g, v_conv_norm_b, v_w_pw2, v_mla_q_norm_g, v_mla_kv_norm_g, v_w_uq, v_w_ukv, v_ssm_a_re, v_ssm_a_im, v_ssm_log_dt, v_ssm_b_re, v_ssm_b_im, v_ssm_c_re, v_ssm_c_im, v_ssm_d, v_w_glu, v_attn_sinks, v_w_branch, v_w_out, v_ln_g, v_ln_b, v_w_ple, v_w_ple_gate, v_ple_norm_g):
    given = dict(locals())
    w_loc = {n: given[n] for n in WEIGHTS}
    m_loc = {n: given["m_" + n] for n in WEIGHTS}
    v_loc = {n: given["v_" + n] for n in WEIGHTS}

    me_chip = 2 * lax.axis_index("x") + lax.axis_index("y")

    wire = {n: w_loc[n].astype(MXU_DTYPE) for n in BIG}
    wire["w_in"] = jnp.pad(wire["w_in"], ((0, 0), (0, 0), (0, IN_SHARD_PAD - IN_SHARD)))
    odd_shapes = [w_loc[n].shape for n in ODD]
    gathered = _gather_chips([wire[n] for n in BIG] + [_pack([w_loc[n] for n in ODD], F32)], [SHARD_AXIS[n] for n in BIG] + [0])
    full = dict(zip(BIG, gathered[:-1]))
    odd_parts = [_unpack(part, odd_shapes) for part in jnp.split(gathered[-1], N_CHIPS, axis=0)]
    for k, n in enumerate(ODD):
        full[n] = jnp.concatenate([odd_parts[s][k] for s in range(N_CHIPS)], axis=SHARD_AXIS[n])
    layers = []
    for i in range(DEPTH):
        layer = {n: (full[n][i] if n in full else w_loc[n][i]) for n in WEIGHTS if n != "w_in"}
        layer["w_in_pad"] = full["w_in"][i]
        layers.append(layer)

    loss, dx, dw = _train_local(x[0], p[:, 0], layers, loss_target[0])
    loss = lax.psum(loss, ("x", "y", "c"))

    key = lambda n: "w_in_pad" if n == "w_in" else n
    small_rep = _pack([dw[i][n] for n in REPLICATED for i in range(DEPTH)], F32)
    small_odd = _pack([dw[i][n] for n in ODD for i in range(DEPTH)], F32)
    *recv, all_rep, all_odd = _exchange_grads([[dw[i][key(n)] for i in range(DEPTH)] for n in BIG],
                                              [SHARD_AXIS[n] - 1 for n in BIG], [small_rep, small_odd])
    parts = []
    for n, r in zip(BIG, recv):
        cols = w_loc[n].shape[-1]
        parts.append(_sum_chips_call(r.reshape(DEPTH, N_CHIPS, -1, r.shape[-1]), cols, "sum_chips_" + n))
    others = _swap_cores(parts)
    g_rep = _sum_slots_call(all_rep, "sum_replicated")
    g_odd = _unpack(_sum_slots_call(all_odd, "sum_odd"), [(DEPTH, *w_loc[n].shape[1:-1], N_CHIPS * w_loc[n].shape[-1]) for n in ODD])

    grads, deltas, new_m, new_v = {}, {}, {}, {}

    def adamw(n, gparts):
        shape = w_loc[n].shape
        two_d = lambda a: a.reshape(-1, shape[-1])
        res = _adamw_call(two_d(w_loc[n]), two_d(m_loc[n]), two_d(v_loc[n]), [two_d(g) for g in gparts], "adamw_" + n)
        grads[n], deltas[n], new_m[n], new_v[n] = [r.reshape(shape) for r in res]

    for n, part, other in zip(BIG, parts, others):
        adamw(n, [part, other])
    for n, g in zip(ODD, g_odd):
        size = w_loc[n].shape[-1]
        adamw(n, [lax.dynamic_slice_in_dim(g, me_chip * size, size, axis=g.ndim - 1)])
    rep_shapes = [w_loc[n].shape for n in REPLICATED]
    res = _adamw_call(_pack([w_loc[n] for n in REPLICATED], F32), _pack([m_loc[n] for n in REPLICATED], F32),
                      _pack([v_loc[n] for n in REPLICATED], F32), [g_rep], "adamw_replicated")
    for dst, buf in zip((grads, deltas, new_m, new_v), res):
        for n, a in zip(REPLICATED, _unpack(buf, rep_shapes)):
            dst[n] = a

    return (loss, dx[None], *[grads[n] for n in WEIGHTS], *[deltas[n] for n in WEIGHTS],
            *[new_m[n] for n in WEIGHTS], *[new_v[n] for n in WEIGHTS])
```

```python
import functools
import math

import jax
import jax.numpy as jnp
from jax import lax
from jax.experimental import pallas as pl
from jax.experimental.pallas import tpu as pltpu

F32 = jnp.float32
BF16 = jnp.bfloat16
MXU_DTYPE = BF16
V7X_VMEM_BYTES = 64 * 1024 * 1024
VMEM_LIMIT = V7X_VMEM_BYTES * 3 // 4
LANES = 128
SUBLANES = 8

D_MODEL = 1024
DEPTH = 4
BRANCH_W = 256
CONV_W = 31
CONV_HALO = 32
MLA_SCALE = (64 + 32) ** -0.5
SWA_SCALE = 64 ** -0.5
WINDOW = 128
ROPE_THETA = 10000.0
SSM_GROUPS, SSM_GROUP, SSM_STATE = 16, 16, 64
SSM_CH = SSM_GROUPS * SSM_STATE
SCAN_SEGMENTS = SUBLANES
SCAN_CB = 128
DEEPNORM_ALPHA = (2.0 * DEPTH) ** 0.25
LN_EPS = 1e-5
RMS_EPS = 1e-6
ADAM_LR, ADAM_B1, ADAM_B2, ADAM_EPS, ADAM_WD, ADAM_STEP = 0.001, 0.9, 0.999, 1e-08, 0.01, 10
NEG = -1e30
ROW_TILE = 512

NN = (((1,), (0,)), ((), ()))
NT = (((1,), (1,)), ((), ()))
TN = (((0,), (0,)), ((), ()))

MESH = pl.DeviceIdType.MESH
ANY = pl.BlockSpec(memory_space=pl.ANY)


def _dot(a, b, dims):
    return lax.dot_general(a.astype(MXU_DTYPE), b.astype(MXU_DTYPE), dims, preferred_element_type=F32)


def _pick(n, cands):
    for c in cands:
        if n % c == 0:
            return c
    return n


def _params(sem):
    return pltpu.CompilerParams(dimension_semantics=sem, vmem_limit_bytes=VMEM_LIMIT)


def _mm_call(a, b, mode, name, out_dtype=F32):
    if mode == "nn":
        (m, k), (_, n) = a.shape, b.shape
    elif mode == "nt":
        (m, k), (n, _) = a.shape, b.shape
    else:
        (k, m), (_, n) = a.shape, b.shape
    tm = _pick(m, (512, 256, 128))
    tn = _pick(n, (1024, 512, 256, 128) if k <= 512 else (512, 256, 128))
    tk = _pick(k, (1024, 512, 256, 128))
    nk = k // tk
    dims = {"nn": NN, "nt": NT, "tn": TN}[mode]
    a_spec = pl.BlockSpec((tk, tm), lambda i, j, kk: (kk, i)) if mode == "tn" else pl.BlockSpec((tm, tk), lambda i, j, kk: (i, kk))
    b_spec = pl.BlockSpec((tn, tk), lambda i, j, kk: (j, kk)) if mode == "nt" else pl.BlockSpec((tk, tn), lambda i, j, kk: (kk, j))

    def body(a_ref, b_ref, o_ref, acc_ref):
        kk = pl.program_id(2)

        @pl.when(kk == 0)
        def _():
            acc_ref[...] = jnp.zeros_like(acc_ref)

        acc_ref[...] += _dot(a_ref[...], b_ref[...], dims)

        @pl.when(kk == nk - 1)
        def _():
            o_ref[...] = acc_ref[...].astype(out_dtype)

    return pl.pallas_call(
        body, name=name, grid=(m // tm, n // tn, nk),
        in_specs=[a_spec, b_spec], out_specs=pl.BlockSpec((tm, tn), lambda i, j, kk: (i, j)),
        out_shape=jax.ShapeDtypeStruct((m, n), out_dtype),
        scratch_shapes=[pltpu.VMEM((tm, tn), F32)],
        compiler_params=_params(("parallel", "parallel", "arbitrary")),
    )(a, b)


@jax.custom_vjp
def op_mm(a, w):
    return _mm_call(a, w.astype(MXU_DTYPE), "nn", "mm_nn")


def _op_mm_fwd(a, w):
    wb = w.astype(MXU_DTYPE)
    return _mm_call(a, wb, "nn", "mm_nn"), (a, wb, jnp.zeros((0,), w.dtype))


def _op_mm_bwd(res, g):
    a, wb, w_like = res
    return _mm_call(g, wb, "nt", "mm_nt", a.dtype), _mm_call(a, g, "tn", "mm_tn", w_like.dtype)


op_mm.defvjp(_op_mm_fwd, _op_mm_bwd)


def _col_offsets(widths):
    return [sum(widths[:j]) for j in range(len(widths))]


def _proj_fwd_call(x, wb, widths, name):
    t, k = x.shape
    tm = min(ROW_TILE, t)
    offs = _col_offsets(widths)

    def body(x_ref, w_ref, *o_refs):
        xb = x_ref[...].astype(MXU_DTYPE)
        for o_ref, off, wd in zip(o_refs, offs, widths):
            o_ref[...] = _dot(xb, w_ref[:, off:off + wd], NN)

    return pl.pallas_call(
        body, name=name, grid=(t // tm,),
        in_specs=[pl.BlockSpec((tm, k), lambda i: (i, 0)), pl.BlockSpec(wb.shape, lambda i: (0, 0))],
        out_specs=[pl.BlockSpec((tm, wd), lambda i: (i, 0)) for wd in widths],
        out_shape=[jax.ShapeDtypeStruct((t, wd), F32) for wd in widths],
        compiler_params=_params(("parallel",)),
    )(x, wb)


def _proj_dx_call(douts, wb, widths, name):
    t = douts[0].shape[0]
    k = wb.shape[0]
    tm = min(ROW_TILE, t)
    offs = _col_offsets(widths)

    def body(*refs):
        d_refs, w_ref, o_ref = refs[:-2], refs[-2], refs[-1]
        acc = jnp.zeros((tm, k), F32)
        for d_ref, off, wd in zip(d_refs, offs, widths):
            acc = acc + _dot(d_ref[...], w_ref[:, off:off + wd], NT)
        o_ref[...] = acc

    return pl.pallas_call(
        body, name=name, grid=(t // tm,),
        in_specs=[pl.BlockSpec((tm, wd), lambda i: (i, 0)) for wd in widths] + [pl.BlockSpec(wb.shape, lambda i: (0, 0))],
        out_specs=pl.BlockSpec((tm, k), lambda i: (i, 0)), out_shape=jax.ShapeDtypeStruct((t, k), F32),
        compiler_params=_params(("parallel",)),
    )(*douts, wb)


def _proj_dw_call(x, douts, widths, name, out_dtype):
    t, k = x.shape
    tk = min(ROW_TILE // 2, t)
    nk = t // tk
    offs = _col_offsets(widths)
    n = sum(widths)

    def body(x_ref, *refs):
        d_refs, o_ref, acc_ref = refs[:-2], refs[-2], refs[-1]

        @pl.when(pl.program_id(0) == 0)
        def _():
            acc_ref[...] = jnp.zeros_like(acc_ref)

        xb = x_ref[...].astype(MXU_DTYPE)
        for d_ref, off, wd in zip(d_refs, offs, widths):
            acc_ref[:, off:off + wd] += _dot(xb, d_ref[...], TN)

        @pl.when(pl.program_id(0) == nk - 1)
        def _():
            o_ref[...] = acc_ref[...].astype(out_dtype)

    return pl.pallas_call(
        body, name=name, grid=(nk,),
        in_specs=[pl.BlockSpec((tk, k), lambda i: (i, 0))] + [pl.BlockSpec((tk, wd), lambda i: (i, 0)) for wd in widths],
        out_specs=pl.BlockSpec((k, n), lambda i: (0, 0)), out_shape=jax.ShapeDtypeStruct((k, n), out_dtype),
        scratch_shapes=[pltpu.VMEM((k, n), F32)],
        compiler_params=_params(("arbitrary",)),
    )(x, *douts)


def make_proj(widths, name):
    @jax.custom_vjp
    def op(x, w):
        return tuple(_proj_fwd_call(x, w.astype(MXU_DTYPE), widths, name + "_fwd"))

    def fwd(x, w):
        wb = w.astype(MXU_DTYPE)
        return tuple(_proj_fwd_call(x, wb, widths, name + "_fwd")), (x, wb, jnp.zeros((0,), w.dtype))

    def bwd(res, douts):
        x, wb, w_like = res
        return _proj_dx_call(douts, wb, widths, name + "_dx"), _proj_dw_call(x, douts, widths, name + "_dw", w_like.dtype)

    op.defvjp(fwd, bwd)
    return op


@jax.custom_vjp
def _mm(a, w):
    return _dot(a, w, NN)


def _mm_f(a, w):
    return _dot(a, w, NN), (a, w)


def _mm_b(res, g):
    a, w = res
    return _dot(g, w, NT), _dot(a, g, TN)


_mm.defvjp(_mm_f, _mm_b)


@functools.partial(jax.custom_vjp, nondiff_argnums=(1,))
def _roll(x, shift):
    return pltpu.roll(x, shift, 1)


def _roll_f(x, shift):
    return pltpu.roll(x, shift, 1), None


def _roll_b(shift, _, g):
    return (pltpu.roll(g, (g.shape[1] - shift) % g.shape[1], 1),)


_roll.defvjp(_roll_f, _roll_b)


def _ln(x, g, b):
    mu = jnp.mean(x, axis=-1, keepdims=True)
    xc = x - mu
    var = jnp.mean(xc * xc, axis=-1, keepdims=True)
    return xc * lax.rsqrt(var + LN_EPS) * g + b


def _rms(x, g):
    ms = jnp.mean(x * x, axis=-1, keepdims=True)
    return x * lax.rsqrt(ms + RMS_EPS) * g


def _sigmoid(x):
    return jax.nn.sigmoid(x)


def _silu(x):
    return x * _sigmoid(x)


def _gelu_tanh(x):
    return x * (0.5 * (1.0 + jnp.tanh(math.sqrt(2.0 / math.pi) * (x + 0.044715 * (x * x * x)))))


def _rowwise_fwd_call(fn, rows, consts, name, tile):
    t = rows[0].shape[0]
    tile = min(tile, t)
    nr = len(rows)
    outs = jax.eval_shape(fn, *[jax.ShapeDtypeStruct((tile, r.shape[1]), F32) for r in rows],
                          *[jax.ShapeDtypeStruct(c.shape, F32) for c in consts])

    def body(*refs):
        vals = [r[...] for r in refs[:nr + len(consts)]]
        res = fn(*vals)
        for o_ref, o in zip(refs[nr + len(consts):], res):
            o_ref[...] = o

    return pl.pallas_call(
        body, name=name, grid=(t // tile,),
        in_specs=[pl.BlockSpec((tile, r.shape[1]), lambda i: (i, 0)) for r in rows]
        + [pl.BlockSpec(c.shape, lambda i: (0, 0)) for c in consts],
        out_specs=[pl.BlockSpec((tile, o.shape[1]), lambda i: (i, 0)) for o in outs],
        out_shape=[jax.ShapeDtypeStruct((t, o.shape[1]), F32) for o in outs],
        compiler_params=_params(("parallel",)),
    )(*rows, *consts)


def _rowwise_bwd_call(fn, rows, consts, douts, row_diff, name, tile):
    t = rows[0].shape[0]
    tile = min(tile, t)
    nr, nc, nd = len(rows), len(consts), len(douts)
    diff_idx = [i for i in range(nr) if row_diff[i]]

    def body(*refs):
        rv = [r[...] for r in refs[:nr]]
        cv = [r[...] for r in refs[nr:nr + nc]]
        dv = [r[...] for r in refs[nr + nc:nr + nc + nd]]
        out_refs = refs[nr + nc + nd:]

        def f(*diff):
            full = list(rv)
            for k, i in enumerate(diff_idx):
                full[i] = diff[k]
            return fn(*full, *diff[len(diff_idx):])

        _, vjp = jax.vjp(f, *[rv[i] for i in diff_idx], *cv)
        grads = vjp(tuple(dv))
        for k in range(len(diff_idx)):
            out_refs[k][...] = grads[k]
        first = pl.program_id(0) == 0
        for k in range(nc):
            acc_ref = out_refs[len(diff_idx) + k]
            g = grads[len(diff_idx) + k]

            @pl.when(first)
            def _(acc_ref=acc_ref, g=g):
                acc_ref[...] = g

            @pl.when(jnp.logical_not(first))
            def _(acc_ref=acc_ref, g=g):
                acc_ref[...] += g

    res = pl.pallas_call(
        body, name=name, grid=(t // tile,),
        in_specs=[pl.BlockSpec((tile, r.shape[1]), lambda i: (i, 0)) for r in rows]
        + [pl.BlockSpec(c.shape, lambda i: (0, 0)) for c in consts]
        + [pl.BlockSpec((tile, d.shape[1]), lambda i: (i, 0)) for d in douts],
        out_specs=[pl.BlockSpec((tile, rows[i].shape[1]), lambda i_: (i_, 0)) for i in diff_idx]
        + [pl.BlockSpec(c.shape, lambda i: (0, 0)) for c in consts],
        out_shape=[jax.ShapeDtypeStruct(rows[i].shape, F32) for i in diff_idx]
        + [jax.ShapeDtypeStruct(c.shape, F32) for c in consts],
        compiler_params=_params(("arbitrary",)),
    )(*rows, *consts, *douts)
    return res[:len(diff_idx)], res[len(diff_idx):]


def make_rowwise(fn, name, row_diff, tile=ROW_TILE):
    @jax.custom_vjp
    def op(rows, consts):
        return tuple(_rowwise_fwd_call(fn, rows, consts, name + "_fwd", tile))

    def fwd(rows, consts):
        return op(rows, consts), (rows, consts)

    def bwd(res, douts):
        rows, consts = res
        drows, dconsts = _rowwise_bwd_call(fn, rows, consts, douts, row_diff, name + "_bwd", tile)
        it = iter(drows)
        full = tuple(next(it) if row_diff[i] else jnp.zeros_like(rows[i]) for i in range(len(rows)))
        return full, tuple(dconsts)

    op.defvjp(fwd, bwd)
    return op


def _conv_post_fn(cv, a_z, ng, nb, w_pw2):
    return (_mm(_silu(_ln(cv, ng, nb)), w_pw2) * _silu(a_z),)


def _mla_prep_fn(c_q, c_kv, krblk, cos4, sin4, qg, kvg, w_uq, w_uk, w_uv):
    qe = _mm(_rms(c_q, qg), w_uq)
    q = qe * cos4 + _roll(qe, qe.shape[1] - 32) * sin4
    cos1, sin1 = cos4[:, :LANES], sin4[:, :LANES]
    kr = krblk * cos1 + _roll(krblk, LANES - 32) * sin1
    kn = _rms(c_kv, kvg)
    k = _mm(kn, w_uk) + jnp.concatenate([kr, kr, kr, kr], axis=1)
    return q, k, _mm(kn, w_uv)


def _ssm_post_fn(y, u, c_z, d, w_a, w_b):
    y2 = _gelu_tanh(y + d * u)
    return (_mm(y2, w_a) * _sigmoid(_mm(y2, w_b)) * _silu(c_z),)


def _gate_fn(o, z):
    return (o * _silu(z),)


def _merge_fn(br0, br1, br2, br3, gl0, gl1, gl2, gl3, b0, b1, b2, b3):
    return (_sigmoid(gl0 + b0) * br0 + _sigmoid(gl1 + b1) * br1 + _sigmoid(gl2 + b2) * br2 + _sigmoid(gl3 + b3) * br3,)


def _ln_fn(x, mo, g, b):
    return (_ln(DEEPNORM_ALPHA * x + mo, g, b),)


def _ple_fn(x1, pe, gl, g):
    return (x1 + _rms(pe * _sigmoid(gl), g),)


op_conv_post = make_rowwise(_conv_post_fn, "conv_post", (True, True))
op_mla_prep = make_rowwise(_mla_prep_fn, "mla_prep", (True, True, True, False, False))
op_ssm_post = make_rowwise(_ssm_post_fn, "ssm_post", (True, True, True))
op_gate = make_rowwise(_gate_fn, "gate", (True, True))
op_merge = make_rowwise(_merge_fn, "merge", (True,) * 8, tile=ROW_TILE // 2)
op_ln = make_rowwise(_ln_fn, "post_ln", (True, True))
op_ple = make_rowwise(_ple_fn, "ple", (True, True, True))


def _conv_fwd_call(a_val, a_gate, w32, b):
    t, w = a_val.shape
    tile = min(ROW_TILE, t)
    per = tile // CONV_HALO
    cur = pl.BlockSpec((tile, w), lambda i: (i, 0))
    prev = pl.BlockSpec((CONV_HALO, w), lambda i: (jnp.maximum(i * per - 1, 0), 0))

    def body(av_ref, avh_ref, ag_ref, agh_ref, w_ref, b_ref, cv_ref, buf):
        i = pl.program_id(0)
        gh = avh_ref[...] * _sigmoid(agh_ref[...])
        buf[0:CONV_HALO, :] = jnp.where(i > 0, gh, 0.0)
        buf[CONV_HALO:, :] = av_ref[...] * _sigmoid(ag_ref[...])
        acc = jnp.zeros((tile, w), F32) + b_ref[...]
        for j in range(CONV_W):
            acc = acc + w_ref[j:j + 1, :] * buf[pl.ds(CONV_HALO - (CONV_W - 1) + j, tile), :]
        cv_ref[...] = acc

    return pl.pallas_call(
        body, name="conv_fwd", grid=(t // tile,),
        in_specs=[cur, prev, cur, prev, pl.BlockSpec((CONV_HALO, w), lambda i: (0, 0)), pl.BlockSpec((1, w), lambda i: (0, 0))],
        out_specs=cur, out_shape=jax.ShapeDtypeStruct((t, w), F32),
        scratch_shapes=[pltpu.VMEM((tile + CONV_HALO, w), F32)],
        compiler_params=_params(("parallel",)),
    )(a_val, a_val, a_gate, a_gate, w32, b)


def _conv_bwd_call(a_val, a_gate, w32, dcv):
    t, w = a_val.shape
    tile = min(ROW_TILE, t)
    n = t // tile
    per = tile // CONV_HALO
    cur = pl.BlockSpec((tile, w), lambda i: (i, 0))
    prev = pl.BlockSpec((CONV_HALO, w), lambda i: (jnp.maximum(i * per - 1, 0), 0))
    nxt = pl.BlockSpec((CONV_HALO, w), lambda i: (jnp.minimum((i + 1) * per, t // CONV_HALO - 1), 0))
    full = lambda r: pl.BlockSpec((r, w), lambda i: (0, 0))

    def body(av_ref, avh_ref, ag_ref, agh_ref, w_ref, d_ref, dn_ref, dav_ref, dag_ref, dw_ref, db_ref, gbuf, dbuf):
        i = pl.program_id(0)
        gh = avh_ref[...] * _sigmoid(agh_ref[...])
        gbuf[0:CONV_HALO, :] = jnp.where(i > 0, gh, 0.0)
        av = av_ref[...]
        sg = _sigmoid(ag_ref[...])
        gbuf[CONV_HALO:, :] = av * sg
        d = d_ref[...]
        dbuf[0:tile, :] = d
        dbuf[tile:, :] = jnp.where(i < n - 1, dn_ref[...], 0.0)

        @pl.when(i == 0)
        def _():
            dw_ref[...] = jnp.zeros_like(dw_ref)
            db_ref[...] = jnp.zeros_like(db_ref)

        dg = jnp.zeros((tile, w), F32)
        for j in range(CONV_W):
            dg = dg + w_ref[j:j + 1, :] * dbuf[pl.ds(CONV_W - 1 - j, tile), :]
            dw_ref[j:j + 1, :] += jnp.sum(d * gbuf[pl.ds(CONV_HALO - (CONV_W - 1) + j, tile), :], axis=0, keepdims=True)
        db_ref[...] += jnp.sum(d, axis=0, keepdims=True)
        dav_ref[...] = dg * sg
        dag_ref[...] = dg * av * sg * (1.0 - sg)

    return pl.pallas_call(
        body, name="conv_bwd", grid=(n,),
        in_specs=[cur, prev, cur, prev, full(CONV_HALO), cur, nxt],
        out_specs=[cur, cur, full(CONV_HALO), full(1)],
        out_shape=[jax.ShapeDtypeStruct((t, w), F32), jax.ShapeDtypeStruct((t, w), F32),
                   jax.ShapeDtypeStruct((CONV_HALO, w), F32), jax.ShapeDtypeStruct((1, w), F32)],
        scratch_shapes=[pltpu.VMEM((tile + CONV_HALO, w), F32), pltpu.VMEM((tile + CONV_HALO, w), F32)],
        compiler_params=_params(("arbitrary",)),
    )(a_val, a_val, a_gate, a_gate, w32, dcv, dcv)


def _pad_taps(conv_w):
    return jnp.concatenate([conv_w, jnp.zeros((CONV_HALO - CONV_W, conv_w.shape[1]), F32)], axis=0)


@jax.custom_vjp
def op_conv(a_val, a_gate, conv_w, conv_b):
    return _conv_fwd_call(a_val, a_gate, _pad_taps(conv_w), conv_b)


def _op_conv_fwd(a_val, a_gate, conv_w, conv_b):
    return op_conv(a_val, a_gate, conv_w, conv_b), (a_val, a_gate, conv_w)


def _op_conv_bwd(res, dcv):
    a_val, a_gate, conv_w = res
    dav, dag, dw, db = _conv_bwd_call(a_val, a_gate, _pad_taps(conv_w), dcv)
    return dav, dag, dw[:CONV_W], db


op_conv.defvjp(_op_conv_fwd, _op_conv_bwd)


def _head_masks(rows):
    lane = lax.broadcasted_iota(jnp.int32, (rows, LANES), 1)
    return lane < 64, lane >= 64


def _head_row(vals, mask):
    return jnp.max(jnp.where(mask, vals, NEG), axis=1, keepdims=True)


def _attn_valid(qpos, kpos, window):
    valid = kpos <= qpos
    if window is not None:
        valid = jnp.logical_and(valid, qpos - kpos < window)
    return valid


def _flash_fwd_call(q, k, v, sink, *, window, shared_k, scale, blk, name):
    t = q.shape[0]
    qw = LANES if shared_k else 2 * LANES
    pairs = v.shape[1] // LANES
    tq = tk = min(blk, t)
    has_sink = sink is not None

    def body(*refs):
        if has_sink:
            q_ref, k_ref, v_ref, s_ref, o_ref, lse_ref, k_mxu, v0_mxu, v1_mxu = refs
        else:
            q_ref, k_ref, v_ref, o_ref, lse_ref, k_mxu, v0_mxu, v1_mxu = refs
        v_mxu = (v0_mxu, v1_mxu)
        i = pl.program_id(1)

        @pl.when(i == 0)
        def _():
            full_masks = _head_masks(t)
            k_mxu[...] = k_ref[...].astype(MXU_DTYPE)
            for h in range(2):
                v_mxu[h][...] = jnp.where(full_masks[h], v_ref[...], 0.0).astype(MXU_DTYPE)

        qb = q_ref[...]
        masks = _head_masks(tq)
        row_masks = _head_masks(1)
        qh = [(jnp.where(masks[h], qb, 0.0) if shared_k else qb[:, h * LANES:(h + 1) * LANES]).astype(MXU_DTYPE) for h in range(2)]
        qpos = i * tq + lax.broadcasted_iota(jnp.int32, (tq, tk), 0)
        if has_sink:
            m_init = [jnp.zeros((tq, 1), F32) + _head_row(s_ref[...], row_masks[h]) for h in range(2)]
            l_init = [jnp.ones((tq, 1), F32)] * 2
        else:
            m_init = [jnp.full((tq, 1), NEG, F32)] * 2
            l_init = [jnp.zeros((tq, 1), F32)] * 2

        def make_step(masked):
            def step(j, carry):
                m0, l0, m1, l1, acc = carry
                start = pl.multiple_of(j * tk, tk)
                kb = k_mxu[pl.ds(start, tk), :]
                if masked:
                    valid = _attn_valid(qpos, j * tk + lax.broadcasted_iota(jnp.int32, (tq, tk), 1), window)
                new, alphas, pv = [], [], []
                for h, (m, l) in enumerate(((m0, l0), (m1, l1))):
                    kh = kb if shared_k else kb[:, h * LANES:(h + 1) * LANES]
                    s = _dot(qh[h], kh, NT) * scale
                    if masked:
                        s = jnp.where(valid, s, NEG)
                    m_new = jnp.maximum(m, jnp.max(s, axis=1, keepdims=True))
                    alpha = jnp.exp(m - m_new)
                    p = jnp.exp(s - m_new)
                    new += [m_new, alpha * l + jnp.sum(p, axis=1, keepdims=True)]
                    alphas.append(alpha)
                    pv.append(_dot(p, v_mxu[h][pl.ds(start, tk), :], NN))
                acc = acc * jnp.where(masks[0], alphas[0], alphas[1]) + pv[0] + pv[1]
                return new[0], new[1], new[2], new[3], acc
            return step

        carry = (m_init[0], l_init[0], m_init[1], l_init[1], jnp.zeros((tq, LANES), F32))
        if window is None:
            carry = lax.fori_loop(0, i, make_step(False), carry)
            carry = make_step(True)(i, carry)
        else:
            carry = lax.fori_loop(jnp.maximum(i * tq - (window - 1), 0) // tk, i + 1, make_step(True), carry)
        m0, l0, m1, l1, acc = carry
        o_ref[...] = acc / jnp.where(masks[0], l0, l1)
        lse_ref[...] = jnp.where(masks[0], m0 + jnp.log(l0), m1 + jnp.log(l1))

    in_specs = [pl.BlockSpec((tq, qw), lambda p, i: (i, p)), pl.BlockSpec((t, qw), lambda p, i: (0, p)),
                pl.BlockSpec((t, LANES), lambda p, i: (0, p))]
    args = [q, k, v]
    if has_sink:
        in_specs.append(pl.BlockSpec((1, LANES), lambda p, i: (0, p)))
        args.append(sink)
    blk_o = pl.BlockSpec((tq, LANES), lambda p, i: (i, p))
    return pl.pallas_call(
        body, name=name, grid=(pairs, t // tq), in_specs=in_specs, out_specs=[blk_o, blk_o],
        out_shape=[jax.ShapeDtypeStruct((t, pairs * LANES), F32)] * 2,
        scratch_shapes=[pltpu.VMEM((t, qw), MXU_DTYPE), pltpu.VMEM((t, LANES), MXU_DTYPE), pltpu.VMEM((t, LANES), MXU_DTYPE)],
        compiler_params=_params(("arbitrary", "arbitrary")),
    )(*args)


def _flash_bwd_call(q, k, v, sink, o, lse, do, *, window, shared_k, scale, blk, name):
    t = q.shape[0]
    qw = LANES if shared_k else 2 * LANES
    pairs = v.shape[1] // LANES
    tq = tk = min(blk, t)
    nq = t // tq
    has_sink = sink is not None

    def body(*refs):
        if has_sink:
            q_ref, k_ref, v_ref, o_ref, lse_ref, do_ref, s_ref, dq_ref, dk_ref, dv_ref, ds_ref = refs[:11]
        else:
            q_ref, k_ref, v_ref, o_ref, lse_ref, do_ref, dq_ref, dk_ref, dv_ref = refs[:9]
        q_mxu, do_mxu, lse_h, dsum_h = refs[-8:-6], refs[-6:-4], refs[-4:-2], refs[-2:]
        j = pl.program_id(1)
        masks = _head_masks(tq)
        row_masks = _head_masks(1)

        @pl.when(j == 0)
        def _():
            dq_ref[...] = jnp.zeros_like(dq_ref)
            full_masks = _head_masks(t)
            prod = do_ref[...] * o_ref[...]
            parts = []
            for h in range(2):
                qh = jnp.where(full_masks[h], q_ref[...], 0.0) if shared_k else q_ref[:, h * LANES:(h + 1) * LANES]
                q_mxu[h][...] = qh.astype(MXU_DTYPE)
                do_mxu[h][...] = jnp.where(full_masks[h], do_ref[...], 0.0).astype(MXU_DTYPE)
                dsum = jnp.sum(jnp.where(full_masks[h], prod, 0.0), axis=1, keepdims=True)
                lse = _head_row(lse_ref[...], full_masks[h])
                dsum_h[h][...] = jnp.zeros((t, LANES), F32) + dsum
                lse_h[h][...] = jnp.zeros((t, LANES), F32) + lse
                if has_sink:
                    ps = jnp.exp(_head_row(s_ref[...], row_masks[h]) - lse)
                    parts.append(-jnp.sum(ps * dsum, axis=0, keepdims=True))
            if has_sink:
                ds_ref[...] = jnp.zeros((SUBLANES, LANES), F32) + jnp.where(row_masks[0], parts[0], parts[1])

        kb = k_ref[...].astype(MXU_DTYPE)
        vb = v_ref[...].astype(MXU_DTYPE)
        kh = [kb if shared_k else kb[:, h * LANES:(h + 1) * LANES] for h in range(2)]
        kpos = j * tk + lax.broadcasted_iota(jnp.int32, (tq, tk), 1)
        lanes_of = lambda a: a if tk == LANES else jnp.concatenate([a] * (tk // LANES), axis=1)

        def make_step(masked):
            def step(i, carry):
                dk0, dk1, dv = carry
                start = pl.multiple_of(i * tq, tq)
                if masked:
                    valid = _attn_valid(i * tq + lax.broadcasted_iota(jnp.int32, (tq, tk), 0), kpos, window)
                dks, dqs = [], []
                for h in range(2):
                    qh = q_mxu[h][pl.ds(start, tq), :]
                    doh = do_mxu[h][pl.ds(start, tq), :]
                    s = _dot(qh, kh[h], NT) * scale
                    if masked:
                        s = jnp.where(valid, s, NEG)
                    p = jnp.exp(s - lanes_of(lse_h[h][pl.ds(start, tq), :]))
                    dp = _dot(doh, vb, NT)
                    dsc = p * (dp - lanes_of(dsum_h[h][pl.ds(start, tq), :])) * scale
                    dv = dv + _dot(p, doh, TN)
                    dks.append(_dot(dsc, qh, TN))
                    dq_h = _dot(dsc, kh[h], NN)
                    dqs.append(jnp.where(masks[h], dq_h, 0.0) if shared_k else dq_h)
                if shared_k:
                    dq_ref[pl.ds(start, tq), :] += dqs[0] + dqs[1]
                else:
                    dq_ref[pl.ds(start, tq), :] += jnp.concatenate(dqs, axis=1)
                return dk0 + dks[0], dk1 + dks[1], dv
            return step

        zero = jnp.zeros((tk, LANES), F32)
        carry = (zero, zero, zero)
        if window is None:
            carry = make_step(True)(j, carry)
            carry = lax.fori_loop(j + 1, nq, make_step(False), carry)
        else:
            carry = lax.fori_loop(j, jnp.minimum(nq, (j * tk + tk - 1 + window - 1) // tq + 1), make_step(True), carry)
        dk0, dk1, dv = carry
        dk_ref[...] = dk0 + dk1 if shared_k else jnp.concatenate([dk0, dk1], axis=1)
        dv_ref[...] = dv

    full = lambda w: pl.BlockSpec((t, w), lambda p, j: (0, p))
    blkspec = lambda w: pl.BlockSpec((tk, w), lambda p, j: (j, p))
    in_specs = [full(qw), blkspec(qw), blkspec(LANES), full(LANES), full(LANES), full(LANES)]
    args = [q, k, v, o, lse, do]
    out_specs = [full(qw), blkspec(qw), blkspec(LANES)]
    out_shape = [jax.ShapeDtypeStruct(q.shape, F32), jax.ShapeDtypeStruct(k.shape, F32), jax.ShapeDtypeStruct(v.shape, F32)]
    if has_sink:
        in_specs.append(pl.BlockSpec((1, LANES), lambda p, j: (0, p)))
        args.append(sink)
        out_specs.append(pl.BlockSpec((SUBLANES, LANES), lambda p, j: (0, p)))
        out_shape.append(jax.ShapeDtypeStruct((SUBLANES, pairs * LANES), F32))
    return pl.pallas_call(
        body, name=name, grid=(pairs, t // tk), in_specs=in_specs, out_specs=out_specs, out_shape=out_shape,
        scratch_shapes=[pltpu.VMEM((t, LANES), MXU_DTYPE)] * 4 + [pltpu.VMEM((t, LANES), F32)] * 4,
        compiler_params=_params(("arbitrary", "arbitrary")),
    )(*args)


_MLA_CFG = dict(window=None, shared_k=False, scale=MLA_SCALE, blk=256)
_SWA_CFG = dict(window=WINDOW, shared_k=True, scale=SWA_SCALE, blk=128)


@jax.custom_vjp
def op_mla_attn(q, k, v):
    return _flash_fwd_call(q, k, v, None, name="mla_fwd", **_MLA_CFG)[0]


def _op_mla_attn_fwd(q, k, v):
    o, lse = _flash_fwd_call(q, k, v, None, name="mla_fwd", **_MLA_CFG)
    return o, (q, k, v, o, lse)


def _op_mla_attn_bwd(res, do):
    q, k, v, o, lse = res
    return tuple(_flash_bwd_call(q, k, v, None, o, lse, do, name="mla_bwd", **_MLA_CFG))


op_mla_attn.defvjp(_op_mla_attn_fwd, _op_mla_attn_bwd)


@jax.custom_vjp
def op_swa_attn(q, k, v, sink):
    return _flash_fwd_call(q, k, v, sink, name="swa_fwd", **_SWA_CFG)[0]


def _op_swa_attn_fwd(q, k, v, sink):
    o, lse = _flash_fwd_call(q, k, v, sink, name="swa_fwd", **_SWA_CFG)
    return o, (q, k, v, sink, o, lse)


def _op_swa_attn_bwd(res, do):
    q, k, v, sink, o, lse = res
    dq, dk, dv, dsink = _flash_bwd_call(q, k, v, sink, o, lse, do, name="swa_bwd", **_SWA_CFG)
    first_lane = lax.broadcasted_iota(jnp.int32, (1, dsink.shape[1]), 1) % 64 == 0
    return dq, dk, dv, jnp.where(first_lane, dsink[:1], 0.0)


op_swa_attn.defvjp(_op_swa_attn_fwd, _op_swa_attn_bwd)


def _complex_power(ar, ai, n):
    for _ in range(int(math.log2(n))):
        ar, ai = ar * ar - ai * ai, 2.0 * ar * ai
    return ar, ai


def _scan_passes(load_b, a1r, a1i, n, store, e_ref, c_ref, reverse):
    cb = a1r.shape[1]
    ar = jnp.zeros((SCAN_SEGMENTS, cb), F32) + a1r
    ai = jnp.zeros((SCAN_SEGMENTS, cb), F32) + a1i
    idx = (lambda ii: n - 1 - ii) if reverse else (lambda ii: ii)

    def local(ii, h):
        br, bi = load_b(idx(ii))
        return ar * h[0] - ai * h[1] + br, ar * h[1] + ai * h[0] + bi

    zero = jnp.zeros((SCAN_SEGMENTS, cb), F32)
    er, ei = lax.fori_loop(0, n, local, (zero, zero))
    e_ref[:, 0:cb] = er
    e_ref[:, cb:] = ei
    pr, pi_ = _complex_power(a1r, a1i, n)
    cr = jnp.zeros((1, cb), F32)
    ci = jnp.zeros((1, cb), F32)
    order = range(SCAN_SEGMENTS - 1, -1, -1) if reverse else range(SCAN_SEGMENTS)
    for s in order:
        c_ref[s:s + 1, 0:cb] = cr
        c_ref[s:s + 1, cb:] = ci
        er1, ei1 = e_ref[s:s + 1, 0:cb], e_ref[s:s + 1, cb:]
        cr, ci = pr * cr - pi_ * ci + er1, pr * ci + pi_ * cr + ei1

    def second(ii, h):
        i = idx(ii)
        hr, hi = local(ii, h)
        store(i, hr, hi)
        return hr, hi

    lax.fori_loop(0, n, second, (c_ref[:, 0:cb], c_ref[:, cb:]))


def _scan_fwd_call(bu, lam):
    n = bu.shape[0]
    cb = SCAN_CB
    blk3 = pl.BlockSpec((n, SCAN_SEGMENTS, 2 * cb), lambda c: (0, 0, c))
    blk2 = lambda r: pl.BlockSpec((r, 2 * cb), lambda c: (0, c))

    def body(b_ref, lam_ref, h_ref, cin_ref, e_ref):
        def store(i, hr, hi):
            h_ref[i, :, 0:cb] = hr
            h_ref[i, :, cb:] = hi

        _scan_passes(lambda i: (b_ref[i, :, 0:cb], b_ref[i, :, cb:]), lam_ref[:, 0:cb], lam_ref[:, cb:], n, store,
                     e_ref, cin_ref, False)

    return pl.pallas_call(
        body, name="scan_fwd", grid=(SSM_CH // cb,), in_specs=[blk3, blk2(1)], out_specs=[blk3, blk2(SCAN_SEGMENTS)],
        out_shape=[jax.ShapeDtypeStruct(bu.shape, F32), jax.ShapeDtypeStruct((SCAN_SEGMENTS, 2 * SSM_CH), F32)],
        scratch_shapes=[pltpu.VMEM((SCAN_SEGMENTS, 2 * cb), F32)],
        compiler_params=_params(("parallel",)),
    )(bu, lam)


def _scan_bwd_call(dh, h, cin, lam):
    n = dh.shape[0]
    cb = SCAN_CB
    blk3 = pl.BlockSpec((n, SCAN_SEGMENTS, 2 * cb), lambda c: (0, 0, c))
    blk2 = lambda r: pl.BlockSpec((r, 2 * cb), lambda c: (0, c))

    def body(d_ref, h_ref, cin_ref, lam_ref, g_ref, dlam_ref, e_ref, c_ref, acc_ref):
        acc_ref[...] = jnp.zeros_like(acc_ref)

        def store(i, gr, gi):
            g_ref[i, :, 0:cb] = gr
            g_ref[i, :, cb:] = gi
            ip = jnp.maximum(i - 1, 0)
            hpr = jnp.where(i > 0, h_ref[ip, :, 0:cb], cin_ref[:, 0:cb])
            hpi = jnp.where(i > 0, h_ref[ip, :, cb:], cin_ref[:, cb:])
            acc_ref[:, 0:cb] += gr * hpr + gi * hpi
            acc_ref[:, cb:] += gi * hpr - gr * hpi

        _scan_passes(lambda i: (d_ref[i, :, 0:cb], d_ref[i, :, cb:]), lam_ref[:, 0:cb], -lam_ref[:, cb:], n, store,
                     e_ref, c_ref, True)
        dlam_ref[...] = acc_ref[...]

    return pl.pallas_call(
        body, name="scan_bwd", grid=(SSM_CH // cb,), in_specs=[blk3, blk3, blk2(SCAN_SEGMENTS), blk2(1)],
        out_specs=[blk3, blk2(SCAN_SEGMENTS)],
        out_shape=[jax.ShapeDtypeStruct(dh.shape, F32), jax.ShapeDtypeStruct((SCAN_SEGMENTS, 2 * SSM_CH), F32)],
        scratch_shapes=[pltpu.VMEM((SCAN_SEGMENTS, 2 * cb), F32)] * 3,
        compiler_params=_params(("parallel",)),
    )(dh, h, cin, lam)


@jax.custom_vjp
def op_scan(bu, lam):
    return _scan_fwd_call(bu, lam)[0]


def _op_scan_fwd(bu, lam):
    h, cin = _scan_fwd_call(bu, lam)
    return h, (h, cin, lam)


def _op_scan_bwd(res, dh):
    h, cin, lam = res
    g, dlam = _scan_bwd_call(dh, h, cin, lam)
    return g, jnp.sum(dlam, axis=0, keepdims=True)


op_scan.defvjp(_op_scan_fwd, _op_scan_bwd)


def _loss_call(y, target):
    t, d = y.shape
    tile = min(ROW_TILE, t)

    def body(y_ref, t_ref, dy_ref, acc_ref):
        @pl.when(pl.program_id(0) == 0)
        def _():
            acc_ref[...] = jnp.zeros_like(acc_ref)

        err = y_ref[...] - t_ref[...]
        dy_ref[...] = err * (1.0 / d)
        col = jnp.sum(err * err, axis=0, keepdims=True)
        part = col[:, 0:LANES]
        for c in range(1, d // LANES):
            part = part + col[:, c * LANES:(c + 1) * LANES]
        acc_ref[0:1, :] += part

    blk = pl.BlockSpec((tile, d), lambda i: (i, 0))
    dy, acc = pl.pallas_call(
        body, name="loss_head", grid=(t // tile,), in_specs=[blk, blk],
        out_specs=[blk, pl.BlockSpec((SUBLANES, LANES), lambda i: (0, 0))],
        out_shape=[jax.ShapeDtypeStruct((t, d), F32), jax.ShapeDtypeStruct((SUBLANES, LANES), F32)],
        compiler_params=_params(("arbitrary",)),
    )(y, target)
    return jnp.sum(acc) * (0.5 / d), dy


def _rot_cols(w):
    return jnp.concatenate([-w[:, 16:], w[:, :16]], axis=1)


def _ext_w_in(w):
    a_val, a_gate, a_z, c_q, c_kv, k_r, b_z, u, c_z, q, k, v, d_z = jnp.split(
        w, (256, 512, 768, 1024, 1152, 1184, 1440, 1696, 1952, 2208, 2336, 2464), axis=1)
    dup = lambda m: jnp.concatenate([m[:, :64], m[:, :64], m[:, 64:], m[:, 64:]], axis=1)
    krblk = jnp.concatenate([jnp.zeros((w.shape[0], 64), w.dtype), k_r, _rot_cols(k_r)], axis=1)
    return jnp.concatenate([a_val, a_gate, a_z, c_q, b_z, u, c_z, q, dup(k), dup(v), d_z, c_kv, krblk], axis=1)


IN_WIDTH = 2720
IN_SHARD = IN_WIDTH // 4
IN_SHARD_PAD = 768
IN_EXT = 3072


def _w_in_layout_matrix():
    src = _ext_w_in(jnp.arange(1, IN_WIDTH + 1, dtype=F32)[None, :])[0]
    col = jnp.abs(src).astype(jnp.int32) - 1
    row_of_col = (col // IN_SHARD) * IN_SHARD_PAD + col % IN_SHARD
    rows = lax.broadcasted_iota(jnp.int32, (4 * IN_SHARD_PAD, IN_EXT), 0)
    return jnp.where(rows == row_of_col[None, :], jnp.sign(src)[None, :], 0.0).astype(MXU_DTYPE)


H_COLS = dict(a_val=256, a_gate=256, a_z=256, c_q=256, b_z=256, u=256, c_z=256, q=256, kdup=256, vdup=256, d_z=256,
              c_kv=128, krblk=128)
op_in_proj = make_proj(tuple(H_COLS.values()), "in_proj")
op_merge_proj = make_proj((D_MODEL,) * 4, "merge_proj")


def _ext_mla(w_uq, w_ukv):
    zeros = jnp.zeros((w_ukv.shape[0], 64), w_ukv.dtype)
    uq, uk, uv = [], [], []
    for h in range(4):
        nope, rp = w_uq[:, 96 * h:96 * h + 64], w_uq[:, 96 * h + 64:96 * h + 96]
        uq += [nope, rp, _rot_cols(rp)]
        uk += [w_ukv[:, 128 * h:128 * h + 64], zeros]
        uv.append(w_ukv[:, 128 * h + 64:128 * h + 128])
    return jnp.concatenate(uq, axis=1), jnp.concatenate(uk, axis=1), jnp.concatenate(uv, axis=1)


def _scan_cols(re, im):
    parts = []
    for c in range(SSM_CH // SCAN_CB):
        parts += [re[..., c * SCAN_CB:(c + 1) * SCAN_CB], im[..., c * SCAN_CB:(c + 1) * SCAN_CB]]
    return jnp.concatenate(parts, axis=-1)


def _ext_ssm(a_re, a_im, log_dt, b_re, b_im, c_re, c_im):
    dt = jnp.exp(log_dt)[:, None]
    mag = jnp.exp(a_re * dt)
    lb_re, lb_im = mag * jnp.cos(a_im * dt), mag * jnp.sin(a_im * dt)
    den = a_re * a_re + a_im * a_im
    nr, ni = lb_re - 1.0, lb_im
    f_re = ((nr * a_re + ni * a_im) / den)[..., None]
    f_im = ((ni * a_re - nr * a_im) / den)[..., None]
    bb_re = f_re * b_re - f_im * b_im
    bb_im = f_re * b_im + f_im * b_re
    eye = jnp.eye(SSM_GROUPS, dtype=F32)
    spread = lambda a: a.transpose(0, 2, 1)[:, :, None, :] * eye[:, None, :, None]
    bd_in = lambda bb: spread(bb).reshape(SSM_GROUPS * SSM_GROUP, SSM_CH)
    bd_out = lambda cc: spread(cc).reshape(SSM_CH, SSM_GROUPS * SSM_GROUP)
    w_bu = _scan_cols(bd_in(bb_re), bd_in(bb_im))
    w_y = _scan_cols(bd_out(c_re).T, -bd_out(c_im).T).T
    lam = _scan_cols(lb_re.reshape(1, SSM_CH), lb_im.reshape(1, SSM_CH))
    return w_bu, w_y, lam


def _rope_tables(t):
    pos = jnp.arange(t, dtype=F32)
    inv_freq = ROPE_THETA ** (-jnp.arange(0, 32, 2, dtype=F32) / 32)
    ang = pos[:, None] * inv_freq[None, :]
    cos, sin = jnp.cos(ang), jnp.sin(ang)
    ones, z32, z64 = jnp.ones((t, 64), F32), jnp.zeros((t, 32), F32), jnp.zeros((t, 64), F32)
    cos1 = jnp.concatenate([ones, cos, cos, z32], axis=1)
    sin1 = jnp.concatenate([z64, sin, sin, z32], axis=1)
    return jnp.concatenate([cos1] * 4, axis=1), jnp.concatenate([sin1] * 4, axis=1)


def _to_segments(a):
    t, w = a.shape
    return a.reshape(SCAN_SEGMENTS, t // SCAN_SEGMENTS, w).transpose(1, 0, 2)


def _from_segments(a):
    n, s, w = a.shape
    return a.transpose(1, 0, 2).reshape(n * s, w)


def _layer(x, p_i, cos4, sin4, e_mat, w):
    t = x.shape[0]
    row = lambda v: v.reshape(1, -1)
    f32 = lambda v: v.astype(F32)
    hs = dict(zip(H_COLS, op_in_proj(x, op_mm(w["w_in_pad"], e_mat))))

    cv = op_conv(hs["a_val"], hs["a_gate"], w["conv_w"], row(w["conv_b"]))
    (y_a,) = op_conv_post((cv, hs["a_z"]), (row(w["conv_norm_g"]), row(w["conv_norm_b"]), f32(w["w_pw2"])))

    w_uq, w_uk, w_uv = _ext_mla(w["w_uq"], f32(w["w_ukv"]))
    q, k, v = op_mla_prep((hs["c_q"], hs["c_kv"], hs["krblk"], cos4, sin4),
                          (row(w["mla_q_norm_g"]), row(w["mla_kv_norm_g"]), w_uq, w_uk, w_uv))
    (y_b,) = op_gate((op_mla_attn(q, k, v), hs["b_z"]), ())

    w_bu, w_y, lam = _ext_ssm(w["ssm_a_re"], w["ssm_a_im"], w["ssm_log_dt"], w["ssm_b_re"], w["ssm_b_im"],
                              w["ssm_c_re"], w["ssm_c_im"])
    u_seg = _to_segments(hs["u"]).reshape(t, BRANCH_W)
    bu = op_mm(u_seg, w_bu).reshape(t // SCAN_SEGMENTS, SCAN_SEGMENTS, 2 * SSM_CH)
    hstate = op_scan(bu, lam).reshape(t, 2 * SSM_CH)
    y_ssm = _from_segments(op_mm(hstate, w_y).reshape(t // SCAN_SEGMENTS, SCAN_SEGMENTS, BRANCH_W))
    w_glu = f32(w["w_glu"])
    (y_c,) = op_ssm_post((y_ssm, hs["u"], hs["c_z"]), (row(w["ssm_d"]), w_glu[:, :BRANCH_W], w_glu[:, BRANCH_W:]))

    sink = jnp.repeat(w["attn_sinks"], 64).reshape(1, 2 * LANES)
    (y_d,) = op_gate((op_swa_attn(hs["q"], hs["kdup"], hs["vdup"], sink), hs["d_z"]), ())

    br = [op_mm(y, w["w_branch"][n]) for n, y in enumerate((y_a, y_b, y_c, y_d))]
    gl = op_merge_proj(x, w["w_merge"])
    bm = [row(w["b_merge"][n * D_MODEL:(n + 1) * D_MODEL]) for n in range(4)]
    (merged,) = op_merge((*br, *gl), tuple(bm))
    (x1,) = op_ln((x, op_mm(merged, w["w_out"])), (row(w["ln_g"]), row(w["ln_b"])))
    (out,) = op_ple((x1, op_mm(p_i, w["w_ple"]), op_mm(x1, w["w_ple_gate"])), (row(w["ple_norm_g"]),))
    return out


def _forward(x, p, layers):
    cos4, sin4 = _rope_tables(x.shape[0])
    e_mat = _w_in_layout_matrix()
    for i in range(DEPTH):
        x = _layer(x, p[i], cos4, sin4, e_mat, layers[i])
    return x


SHARD_AXIS = dict(w_in=2, w_merge=2, conv_w=2, w_pw2=1, w_uq=2, w_ukv=2, w_glu=2, w_branch=3, w_out=1, w_ple=2, w_ple_gate=1)
ODD = ("w_uq", "conv_w")
BIG = tuple(n for n in SHARD_AXIS if n not in ODD)
REPLICATED = ("b_merge", "conv_b", "conv_norm_g", "conv_norm_b", "mla_q_norm_g", "mla_kv_norm_g", "ssm_a_re", "ssm_a_im",
              "ssm_log_dt", "ssm_b_re", "ssm_b_im", "ssm_c_re", "ssm_c_im", "ssm_d", "attn_sinks", "ln_g", "ln_b", "ple_norm_g")
WEIGHTS = ("w_in", "w_merge", "b_merge", "conv_w", "conv_b", "conv_norm_g", "conv_norm_b", "w_pw2", "mla_q_norm_g",
           "mla_kv_norm_g", "w_uq", "w_ukv", "ssm_a_re", "ssm_a_im", "ssm_log_dt", "ssm_b_re", "ssm_b_im", "ssm_c_re",
           "ssm_c_im", "ssm_d", "w_glu", "attn_sinks", "w_branch", "w_out", "ln_g", "ln_b", "w_ple", "w_ple_gate", "ple_norm_g")
PACK_COLS = 1024
PACK_ROWS = 512
CHIP_FLIPS = ((1, 0), (0, 1), (1, 1))
N_CHIPS = 4
N_DEV = 8


def _pack(arrays, dtype):
    flat = jnp.concatenate([a.reshape(-1).astype(dtype) for a in arrays])
    unit = PACK_ROWS * PACK_COLS
    total = -(-flat.shape[0] // unit) * unit
    return jnp.concatenate([flat, jnp.zeros((total - flat.shape[0],), dtype)]).reshape(-1, PACK_COLS)


def _unpack(buf, shapes):
    flat = buf.reshape(-1)
    out, off = [], 0
    for s in shapes:
        n = math.prod(s)
        out.append(flat[off:off + n].reshape(s))
        off += n
    return out


def _flip(v, bit):
    return 1 - v if bit else v


def _window(ref, axis, start, size):
    idx = [slice(None)] * len(ref.shape)
    idx[axis] = pl.ds(start, size)
    return ref.at[tuple(idx)]


def _gather_chips(srcs, axes):
    nb = len(srcs)
    sizes = [s.shape[a] for s, a in zip(srcs, axes)]
    halves = [s.shape[0] // 2 for s in srcs]

    def body(*refs):
        ins, outs = refs[:nb], refs[nb:2 * nb]
        ici_send, ici_recv, d2d_send, d2d_recv, local_sems = refs[2 * nb:]
        x, y, c = lax.axis_index("x"), lax.axis_index("y"), lax.axis_index("c")
        me = 2 * x + y

        def place(k, chip, half=None):
            if half is None:
                return _window(outs[k], axes[k], chip * sizes[k], sizes[k])
            if axes[k] == 0:
                return outs[k].at[pl.ds(chip * sizes[k] + half * halves[k], halves[k])]
            return _window(outs[k].at[pl.ds(half * halves[k], halves[k])], axes[k], chip * sizes[k], sizes[k])

        local = [pltpu.make_async_copy(ins[k], place(k, me), local_sems.at[k]) for k in range(nb)]
        for cp in local:
            cp.start()
        sends = []
        for j, (bx, by) in enumerate(CHIP_FLIPS):
            for k in range(nb):
                cp = pltpu.make_async_remote_copy(src_ref=ins[k].at[pl.ds(c * halves[k], halves[k])], dst_ref=place(k, me, c),
                                                  send_sem=ici_send.at[j * nb + k], recv_sem=ici_recv.at[j * nb + k],
                                                  device_id=(_flip(x, bx), _flip(y, by), c), device_id_type=MESH)
                cp.start()
                sends.append(cp)
        for j, (bx, by) in enumerate(CHIP_FLIPS):
            src = 2 * _flip(x, bx) + _flip(y, by)
            for k in range(nb):
                got = place(k, src, c)
                pltpu.make_async_remote_copy(src_ref=got, dst_ref=got, send_sem=ici_send.at[j * nb + k],
                                             recv_sem=ici_recv.at[j * nb + k], device_id=(x, y, c), device_id_type=MESH).wait_recv()
                cp = pltpu.make_async_remote_copy(src_ref=got, dst_ref=got, send_sem=d2d_send.at[j * nb + k],
                                                  recv_sem=d2d_recv.at[j * nb + k], device_id=(x, y, 1 - c), device_id_type=MESH)
                cp.start()
                sends.append(cp)
        for j, (bx, by) in enumerate(CHIP_FLIPS):
            src = 2 * _flip(x, bx) + _flip(y, by)
            for k in range(nb):
                other = place(k, src, 1 - c)
                pltpu.make_async_remote_copy(src_ref=other, dst_ref=other, send_sem=d2d_send.at[j * nb + k],
                                             recv_sem=d2d_recv.at[j * nb + k], device_id=(x, y, c), device_id_type=MESH).wait_recv()
        for cp in sends:
            cp.wait_send()
        for cp in local:
            cp.wait()

    full = lambda s, a: tuple(N_CHIPS * d if i == a else d for i, d in enumerate(s.shape))
    return pl.pallas_call(
        body, name="gather_weights", in_specs=[ANY] * nb, out_specs=[ANY] * nb,
        out_shape=[jax.ShapeDtypeStruct(full(s, a), s.dtype) for s, a in zip(srcs, axes)],
        scratch_shapes=[pltpu.SemaphoreType.DMA((3 * nb,))] * 4 + [pltpu.SemaphoreType.DMA((nb,))],
    )(*srcs)


def _exchange_grads(grads, axes, smalls):
    nt, ns = len(grads), len(smalls)
    sizes = [g[0].shape[a] // N_CHIPS for g, a in zip(grads, axes)]
    dev_flips = [(bx, by, bc) for bx in (0, 1) for by in (0, 1) for bc in (0, 1)][1:]
    n_remote = 3 * nt * DEPTH + 7 * ns
    n_local = nt * DEPTH + ns

    def body(*refs):
        g_refs = [refs[k * DEPTH:(k + 1) * DEPTH] for k in range(nt)]
        s_refs = refs[nt * DEPTH:nt * DEPTH + ns]
        outs = refs[nt * DEPTH + ns:nt * DEPTH + ns + nt + ns]
        recv_refs, all_refs = outs[:nt], outs[nt:]
        send_sems, recv_sems, local_sems = refs[-3:]
        x, y, c = lax.axis_index("x"), lax.axis_index("y"), lax.axis_index("c")
        me_chip = 2 * x + y
        me = 4 * x + 2 * y + c
        part = lambda k, i, chip: _window(g_refs[k][i], axes[k], chip * sizes[k], sizes[k])
        started, waits = [], []
        sem, lsem = 0, 0
        for k in range(nt):
            for i in range(DEPTH):
                cp = pltpu.make_async_copy(part(k, i, me_chip), recv_refs[k].at[i, 3], local_sems.at[lsem])
                cp.start()
                started.append(cp.wait)
                lsem += 1
                for j, (bx, by) in enumerate(CHIP_FLIPS):
                    px, py = _flip(x, bx), _flip(y, by)
                    cp = pltpu.make_async_remote_copy(src_ref=part(k, i, 2 * px + py), dst_ref=recv_refs[k].at[i, j],
                                                      send_sem=send_sems.at[sem], recv_sem=recv_sems.at[sem],
                                                      device_id=(px, py, c), device_id_type=MESH)
                    cp.start()
                    started.append(cp.wait_send)
                    waits.append(cp.wait_recv)
                    sem += 1
        for s in range(ns):
            cp = pltpu.make_async_copy(s_refs[s], all_refs[s].at[me], local_sems.at[lsem])
            cp.start()
            started.append(cp.wait)
            lsem += 1
            for bx, by, bc in dev_flips:
                peer = (_flip(x, bx), _flip(y, by), _flip(c, bc))
                cp = pltpu.make_async_remote_copy(src_ref=s_refs[s], dst_ref=all_refs[s].at[me], send_sem=send_sems.at[sem],
                                                  recv_sem=recv_sems.at[sem], device_id=peer, device_id_type=MESH)
                cp.start()
                started.append(cp.wait_send)
                src = 4 * peer[0] + 2 * peer[1] + peer[2]
                waits.append(pltpu.make_async_remote_copy(src_ref=s_refs[s], dst_ref=all_refs[s].at[src], send_sem=send_sems.at[sem],
                                                          recv_sem=recv_sems.at[sem], device_id=peer, device_id_type=MESH).wait_recv)
                sem += 1
        for w in waits + started:
            w()

    shard = lambda g, a: tuple(d // N_CHIPS if i == a else d for i, d in enumerate(g.shape))
    flat = [g for per_layer in grads for g in per_layer]
    return pl.pallas_call(
        body, name="exchange_grads", in_specs=[ANY] * (len(flat) + ns), out_specs=[ANY] * (nt + ns),
        out_shape=[jax.ShapeDtypeStruct((DEPTH, N_CHIPS, *shard(g[0], a)), g[0].dtype) for g, a in zip(grads, axes)]
        + [jax.ShapeDtypeStruct((N_DEV, *s.shape), s.dtype) for s in smalls],
        scratch_shapes=[pltpu.SemaphoreType.DMA((n_remote,)), pltpu.SemaphoreType.DMA((n_remote,)), pltpu.SemaphoreType.DMA((n_local,))],
    )(*flat, *smalls)


def _swap_cores(parts):
    nb = len(parts)

    def body(*refs):
        ins, outs, send_sems, recv_sems = refs[:nb], refs[nb:2 * nb], refs[-2], refs[-1]
        x, y, c = lax.axis_index("x"), lax.axis_index("y"), lax.axis_index("c")
        cps = [pltpu.make_async_remote_copy(src_ref=ins[k], dst_ref=outs[k], send_sem=send_sems.at[k], recv_sem=recv_sems.at[k],
                                            device_id=(x, y, 1 - c), device_id_type=MESH) for k in range(nb)]
        for cp in cps:
            cp.start()
        for cp in cps:
            cp.wait()

    return pl.pallas_call(
        body, name="swap_cores", in_specs=[ANY] * nb, out_specs=[ANY] * nb,
        out_shape=[jax.ShapeDtypeStruct(q.shape, q.dtype) for q in parts],
        scratch_shapes=[pltpu.SemaphoreType.DMA((nb,)), pltpu.SemaphoreType.DMA((nb,))],
    )(*parts)


def _sum_chips_call(recv, cols, name):
    depth, _, r, c = recv.shape
    tile = _pick(r, (512, 256, 128, 64, 32, 16))

    def body(r_ref, o_ref):
        slot = lambda s: r_ref[s, :, pl.ds(0, cols)].astype(F32)
        o_ref[...] = ((slot(3) + slot(0)) + slot(1)) + slot(2)

    return pl.pallas_call(
        body, name=name, grid=(depth, r // tile),
        in_specs=[pl.BlockSpec((None, N_CHIPS, tile, c), lambda l, i: (l, 0, i, 0))],
        out_specs=pl.BlockSpec((None, tile, cols), lambda l, i: (l, i, 0)), out_shape=jax.ShapeDtypeStruct((depth, r, cols), F32),
        compiler_params=_params(("parallel", "parallel")),
    )(recv)


def _sum_slots_call(slots, name):
    n, r, c = slots.shape
    tile = _pick(r, (512, 256, 128, 64, 32, 16, 8))

    def body(s_ref, o_ref):
        acc = s_ref[0]
        for s in range(1, n):
            acc = acc + s_ref[s]
        o_ref[...] = acc

    return pl.pallas_call(
        body, name=name, grid=(r // tile,), in_specs=[pl.BlockSpec((n, tile, c), lambda i: (0, i, 0))],
        out_specs=pl.BlockSpec((tile, c), lambda i: (i, 0)), out_shape=jax.ShapeDtypeStruct((r, c), F32),
        compiler_params=_params(("parallel",)),
    )(slots)


def _adamw_math(w, g, m, v):
    m = ADAM_B1 * m + (1.0 - ADAM_B1) * g
    v = ADAM_B2 * v + (1.0 - ADAM_B2) * (g * g)
    m_hat = m / (1.0 - ADAM_B1 ** ADAM_STEP)
    v_hat = v / (1.0 - ADAM_B2 ** ADAM_STEP)
    return -ADAM_LR * (m_hat / (jnp.sqrt(v_hat) + ADAM_EPS) + ADAM_WD * w), m, v


def _adamw_call(w, m, v, gparts, name):
    r, c = w.shape
    n = len(gparts)
    tile = _pick(r, (512, 256, 128, 64, 32, 16, 8))

    def body(w_ref, m_ref, v_ref, *refs):
        g_refs, (go_ref, d_ref, mo_ref, vo_ref) = refs[:n], refs[n:]
        g = g_refs[0][...]
        for g_ref in g_refs[1:]:
            g = g + g_ref[...]
        go_ref[...] = g
        d_ref[...], mo_ref[...], vo_ref[...] = _adamw_math(w_ref[...], g, m_ref[...], v_ref[...])

    blk = pl.BlockSpec((tile, c), lambda i: (i, 0))
    return pl.pallas_call(
        body, name=name, grid=(r // tile,), in_specs=[blk] * (3 + n),
        out_specs=[blk] * 4, out_shape=[jax.ShapeDtypeStruct((r, c), F32)] * 4,
        compiler_params=_params(("parallel",)),
    )(w, m, v, *gparts)


def _train_local(x, p, layers, target):
    y, vjp = jax.vjp(lambda x_, w_: _forward(x_, p, w_), x, layers)
    loss, dy = _loss_call(y, target)
    dx, dw = vjp(dy)
    return loss, dx, dw


def kernel(x, p, w_in, w_merge, b_merge, conv_w, conv_b, conv_norm_g, conv_norm_b, w_pw2, mla_q_norm_g, mla_kv_norm_g, w_uq, w_ukv, ssm_a_re, ssm_a_im, ssm_log_dt, ssm_b_re, ssm_b_im, ssm_c_re, ssm_c_im, ssm_d, w_glu, attn_sinks, w_branch, w_out, ln_g, ln_b, w_ple, w_ple_gate, ple_norm_g, loss_target, m_w_in, m_w_merge, m_b_merge, m_conv_w, m_conv_b, m_conv_norm_g, m_conv_norm_b, m_w_pw2, m_mla_q_norm_g, m_mla_kv_norm_g, m_w_uq, m_w_ukv, m_ssm_a_re, m_ssm_a_im, m_ssm_log_dt, m_ssm_b_re, m_ssm_b_im, m_ssm_c_re, m_ssm_c_im, m_ssm_d, m_w_glu, m_attn_sinks, m_w_branch, m_w_out, m_ln_g, m_ln_b, m_w_ple, m_w_ple_gate, m_ple_norm_g, v_w_in, v_w_merge, v_b_merge, v_conv_w, v_conv_b, v_conv_norm_g, v_conv_norm_b, v_w_pw2, v_mla_q_norm_g, v_mla_kv_norm_g, v_w_uq, v_w_ukv, v_ssm_a_re, v_ssm_a_im, v_ssm_log_dt, v_ssm_b_re, v_ssm_b_im, v_ssm_c_re, v_ssm_c_im, v_ssm_d, v_w_glu, v_attn_sinks, v_w_branch, v_w_out, v_ln_g, v_ln_b, v_w_ple, v_w_ple_gate, v_ple_norm_g):
    given = dict(locals())
    w_loc = {n: given[n] for n in WEIGHTS}
    m_loc = {n: given["m_" + n] for n in WEIGHTS}
    v_loc = {n: given["v_" + n] for n in WEIGHTS}

    me_chip = 2 * lax.axis_index("x") + lax.axis_index("y")

    wire = {n: w_loc[n].astype(MXU_DTYPE) for n in BIG}
    wire["w_in"] = jnp.pad(wire["w_in"], ((0, 0), (0, 0), (0, IN_SHARD_PAD - IN_SHARD)))
    odd_shapes = [w_loc[n].shape for n in ODD]
    gathered = _gather_chips([wire[n] for n in BIG] + [_pack([w_loc[n] for n in ODD], F32)], [SHARD_AXIS[n] for n in BIG] + [0])
    full = dict(zip(BIG, gathered[:-1]))
    odd_parts = [_unpack(part, odd_shapes) for part in jnp.split(gathered[-1], N_CHIPS, axis=0)]
    for k, n in enumerate(ODD):
        full[n] = jnp.concatenate([odd_parts[s][k] for s in range(N_CHIPS)], axis=SHARD_AXIS[n])
    layers = []
    for i in range(DEPTH):
        layer = {n: (full[n][i] if n in full else w_loc[n][i]) for n in WEIGHTS if n != "w_in"}
        layer["w_in_pad"] = full["w_in"][i]
        layers.append(layer)

    loss, dx, dw = _train_local(x[0], p[:, 0], layers, loss_target[0])
    loss = lax.psum(loss, ("x", "y", "c"))

    key = lambda n: "w_in_pad" if n == "w_in" else n
    small_rep = _pack([dw[i][n] for n in REPLICATED for i in range(DEPTH)], F32)
    small_odd = _pack([dw[i][n] for n in ODD for i in range(DEPTH)], F32)
    *recv, all_rep, all_odd = _exchange_grads([[dw[i][key(n)] for i in range(DEPTH)] for n in BIG],
                                              [SHARD_AXIS[n] - 1 for n in BIG], [small_rep, small_odd])
    parts = []
    for n, r in zip(BIG, recv):
        cols = w_loc[n].shape[-1]
        parts.append(_sum_chips_call(r.reshape(DEPTH, N_CHIPS, -1, r.shape[-1]), cols, "sum_chips_" + n))
    others = _swap_cores(parts)
    g_rep = _sum_slots_call(all_rep, "sum_replicated")
    g_odd = _unpack(_sum_slots_call(all_odd, "sum_odd"), [(DEPTH, *w_loc[n].shape[1:-1], N_CHIPS * w_loc[n].shape[-1]) for n in ODD])

    grads, deltas, new_m, new_v = {}, {}, {}, {}

    def adamw(n, gparts):
        shape = w_loc[n].shape
        two_d = lambda a: a.reshape(-1, shape[-1])
        res = _adamw_call(two_d(w_loc[n]), two_d(m_loc[n]), two_d(v_loc[n]), [two_d(g) for g in gparts], "adamw_" + n)
        grads[n], deltas[n], new_m[n], new_v[n] = [r.reshape(shape) for r in res]

    for n, part, other in zip(BIG, parts, others):
        adamw(n, [part, other])
    for n, g in zip(ODD, g_odd):
        size = w_loc[n].shape[-1]
        adamw(n, [lax.dynamic_slice_in_dim(g, me_chip * size, size, axis=g.ndim - 1)])
    rep_shapes = [w_loc[n].shape for n in REPLICATED]
    res = _adamw_call(_pack([w_loc[n] for n in REPLICATED], F32), _pack([m_loc[n] for n in REPLICATED], F32),
                      _pack([v_loc[n] for n in REPLICATED], F32), [g_rep], "adamw_replicated")
    for dst, buf in zip((grads, deltas, new_m, new_v), res):
        for n, a in zip(REPLICATED, _unpack(buf, rep_shapes)):
            dst[n] = a

    return (loss, dx[None], *[grads[n] for n in WEIGHTS], *[deltas[n] for n in WEIGHTS],
            *[new_m[n] for n in WEIGHTS], *[new_v[n] for n in WEIGHTS])
```

```python
import functools
import math

import jax
import jax.numpy as jnp
from jax import lax
from jax.experimental import pallas as pl
from jax.experimental.pallas import tpu as pltpu

F32 = jnp.float32
BF16 = jnp.bfloat16
MXU_DTYPE = BF16
V7X_VMEM_BYTES = 64 * 1024 * 1024
VMEM_LIMIT = V7X_VMEM_BYTES * 3 // 4
LANES = 128
SUBLANES = 8

D_MODEL = 1024
DEPTH = 4
BRANCH_W = 256
CONV_W = 31
CONV_HALO = 32
MLA_SCALE = (64 + 32) ** -0.5
SWA_SCALE = 64 ** -0.5
WINDOW = 128
ROPE_THETA = 10000.0
SSM_GROUPS, SSM_GROUP, SSM_STATE = 16, 16, 64
SSM_CH = SSM_GROUPS * SSM_STATE
SCAN_SEGMENTS = SUBLANES
SCAN_CB = 128
DEEPNORM_ALPHA = (2.0 * DEPTH) ** 0.25
LN_EPS = 1e-5
RMS_EPS = 1e-6
ADAM_LR, ADAM_B1, ADAM_B2, ADAM_EPS, ADAM_WD, ADAM_STEP = 0.001, 0.9, 0.999, 1e-08, 0.01, 10
NEG = -1e30
ROW_TILE = 512

NN = (((1,), (0,)), ((), ()))
NT = (((1,), (1,)), ((), ()))
TN = (((0,), (0,)), ((), ()))

MESH = pl.DeviceIdType.MESH
ANY = pl.BlockSpec(memory_space=pl.ANY)


def _dot(a, b, dims):
    return lax.dot_general(a.astype(MXU_DTYPE), b.astype(MXU_DTYPE), dims, preferred_element_type=F32)


def _pick(n, cands):
    for c in cands:
        if n % c == 0:
            return c
    return n


def _params(sem):
    return pltpu.CompilerParams(dimension_semantics=sem, vmem_limit_bytes=VMEM_LIMIT)


def _mm_call(a, b, mode, name, out_dtype=F32):
    if mode == "nn":
        (m, k), (_, n) = a.shape, b.shape
    elif mode == "nt":
        (m, k), (n, _) = a.shape, b.shape
    else:
        (k, m), (_, n) = a.shape, b.shape
    tm = _pick(m, (512, 256, 128))
    tn = _pick(n, (1024, 512, 256, 128) if k <= 512 else (512, 256, 128))
    tk = _pick(k, (1024, 512, 256, 128))
    nk = k // tk
    dims = {"nn": NN, "nt": NT, "tn": TN}[mode]
    a_spec = pl.BlockSpec((tk, tm), lambda i, j, kk: (kk, i)) if mode == "tn" else pl.BlockSpec((tm, tk), lambda i, j, kk: (i, kk))
    b_spec = pl.BlockSpec((tn, tk), lambda i, j, kk: (j, kk)) if mode == "nt" else pl.BlockSpec((tk, tn), lambda i, j, kk: (kk, j))

    def body(a_ref, b_ref, o_ref, acc_ref):
        kk = pl.program_id(2)

        @pl.when(kk == 0)
        def _():
            acc_ref[...] = jnp.zeros_like(acc_ref)

        acc_ref[...] += _dot(a_ref[...], b_ref[...], dims)

        @pl.when(kk == nk - 1)
        def _():
            o_ref[...] = acc_ref[...].astype(out_dtype)

    return pl.pallas_call(
        body, name=name, grid=(m // tm, n // tn, nk),
        in_specs=[a_spec, b_spec], out_specs=pl.BlockSpec((tm, tn), lambda i, j, kk: (i, j)),
        out_shape=jax.ShapeDtypeStruct((m, n), out_dtype),
        scratch_shapes=[pltpu.VMEM((tm, tn), F32)],
        compiler_params=_params(("parallel", "parallel", "arbitrary")),
    )(a, b)


@jax.custom_vjp
def op_mm(a, w):
    return _mm_call(a, w.astype(MXU_DTYPE), "nn", "mm_nn")


def _op_mm_fwd(a, w):
    wb = w.astype(MXU_DTYPE)
    return _mm_call(a, wb, "nn", "mm_nn"), (a, wb, jnp.zeros((0,), w.dtype))


def _op_mm_bwd(res, g):
    a, wb, w_like = res
    return _mm_call(g, wb, "nt", "mm_nt", a.dtype), _mm_call(a, g, "tn", "mm_tn", w_like.dtype)


op_mm.defvjp(_op_mm_fwd, _op_mm_bwd)


def _col_offsets(widths):
    return [sum(widths[:j]) for j in range(len(widths))]


def _proj_fwd_call(x, wb, widths, name):
    t, k = x.shape
    tm = min(ROW_TILE, t)
    offs = _col_offsets(widths)

    def body(x_ref, w_ref, *o_refs):
        xb = x_ref[...].astype(MXU_DTYPE)
        for o_ref, off, wd in zip(o_refs, offs, widths):
            o_ref[...] = _dot(xb, w_ref[:, off:off + wd], NN)

    return pl.pallas_call(
        body, name=name, grid=(t // tm,),
        in_specs=[pl.BlockSpec((tm, k), lambda i: (i, 0)), pl.BlockSpec(wb.shape, lambda i: (0, 0))],
        out_specs=[pl.BlockSpec((tm, wd), lambda i: (i, 0)) for wd in widths],
        out_shape=[jax.ShapeDtypeStruct((t, wd), F32) for wd in widths],
        compiler_params=_params(("parallel",)),
    )(x, wb)


def _proj_dx_call(douts, wb, widths, name):
    t = douts[0].shape[0]
    k = wb.shape[0]
    tm = min(ROW_TILE, t)
    offs = _col_offsets(widths)

    def body(*refs):
        d_refs, w_ref, o_ref = refs[:-2], refs[-2], refs[-1]
        acc = jnp.zeros((tm, k), F32)
        for d_ref, off, wd in zip(d_refs, offs, widths):
            acc = acc + _dot(d_ref[...], w_ref[:, off:off + wd], NT)
        o_ref[...] = acc

    return pl.pallas_call(
        body, name=name, grid=(t // tm,),
        in_specs=[pl.BlockSpec((tm, wd), lambda i: (i, 0)) for wd in widths] + [pl.BlockSpec(wb.shape, lambda i: (0, 0))],
        out_specs=pl.BlockSpec((tm, k), lambda i: (i, 0)), out_shape=jax.ShapeDtypeStruct((t, k), F32),
        compiler_params=_params(("parallel",)),
    )(*douts, wb)


def _proj_dw_call(x, douts, widths, name, out_dtype):
    t, k = x.shape
    tk = min(ROW_TILE // 2, t)
    nk = t // tk
    offs = _col_offsets(widths)
    n = sum(widths)

    def body(x_ref, *refs):
        d_refs, o_ref, acc_ref = refs[:-2], refs[-2], refs[-1]

        @pl.when(pl.program_id(0) == 0)
        def _():
            acc_ref[...] = jnp.zeros_like(acc_ref)

        xb = x_ref[...].astype(MXU_DTYPE)
        for d_ref, off, wd in zip(d_refs, offs, widths):
            acc_ref[:, off:off + wd] += _dot(xb, d_ref[...], TN)

        @pl.when(pl.program_id(0) == nk - 1)
        def _():
            o_ref[...] = acc_ref[...].astype(out_dtype)

    return pl.pallas_call(
        body, name=name, grid=(nk,),
        in_specs=[pl.BlockSpec((tk, k), lambda i: (i, 0))] + [pl.BlockSpec((tk, wd), lambda i: (i, 0)) for wd in widths],
        out_specs=pl.BlockSpec((k, n), lambda i: (0, 0)), out_shape=jax.ShapeDtypeStruct((k, n), out_dtype),
        scratch_shapes=[pltpu.VMEM((k, n), F32)],
        compiler_params=_params(("arbitrary",)),
    )(x, *douts)


def make_proj(widths, name):
    @jax.custom_vjp
    def op(x, w):
        return tuple(_proj_fwd_call(x, w.astype(MXU_DTYPE), widths, name + "_fwd"))

    def fwd(x, w):
        wb = w.astype(MXU_DTYPE)
        return tuple(_proj_fwd_call(x, wb, widths, name + "_fwd")), (x, wb, jnp.zeros((0,), w.dtype))

    def bwd(res, douts):
        x, wb, w_like = res
        return _proj_dx_call(douts, wb, widths, name + "_dx"), _proj_dw_call(x, douts, widths, name + "_dw", w_like.dtype)

    op.defvjp(fwd, bwd)
    return op


@jax.custom_vjp
def _mm(a, w):
    return _dot(a, w, NN)


def _mm_f(a, w):
    return _dot(a, w, NN), (a, w)


def _mm_b(res, g):
    a, w = res
    return _dot(g, w, NT), _dot(a, g, TN)


_mm.defvjp(_mm_f, _mm_b)


@functools.partial(jax.custom_vjp, nondiff_argnums=(1,))
def _roll(x, shift):
    return pltpu.roll(x, shift, 1)


def _roll_f(x, shift):
    return pltpu.roll(x, shift, 1), None


def _roll_b(shift, _, g):
    return (pltpu.roll(g, (g.shape[1] - shift) % g.shape[1], 1),)


_roll.defvjp(_roll_f, _roll_b)


def _ln(x, g, b):
    mu = jnp.mean(x, axis=-1, keepdims=True)
    xc = x - mu
    var = jnp.mean(xc * xc, axis=-1, keepdims=True)
    return xc * lax.rsqrt(var + LN_EPS) * g + b


def _rms(x, g):
    ms = jnp.mean(x * x, axis=-1, keepdims=True)
    return x * lax.rsqrt(ms + RMS_EPS) * g


def _sigmoid(x):
    return jax.nn.sigmoid(x)


def _silu(x):
    return x * _sigmoid(x)


def _gelu_tanh(x):
    return x * (0.5 * (1.0 + jnp.tanh(math.sqrt(2.0 / math.pi) * (x + 0.044715 * (x * x * x)))))


def _rowwise_fwd_call(fn, rows, consts, name, tile):
    t = rows[0].shape[0]
    tile = min(tile, t)
    nr = len(rows)
    outs = jax.eval_shape(fn, *[jax.ShapeDtypeStruct((tile, r.shape[1]), F32) for r in rows],
                          *[jax.ShapeDtypeStruct(c.shape, F32) for c in consts])

    def body(*refs):
        vals = [r[...] for r in refs[:nr + len(consts)]]
        res = fn(*vals)
        for o_ref, o in zip(refs[nr + len(consts):], res):
            o_ref[...] = o

    return pl.pallas_call(
        body, name=name, grid=(t // tile,),
        in_specs=[pl.BlockSpec((tile, r.shape[1]), lambda i: (i, 0)) for r in rows]
        + [pl.BlockSpec(c.shape, lambda i: (0, 0)) for c in consts],
        out_specs=[pl.BlockSpec((tile, o.shape[1]), lambda i: (i, 0)) for o in outs],
        out_shape=[jax.ShapeDtypeStruct((t, o.shape[1]), F32) for o in outs],
        compiler_params=_params(("parallel",)),
    )(*rows, *consts)


def _rowwise_bwd_call(fn, rows, consts, douts, row_diff, name, tile):
    t = rows[0].shape[0]
    tile = min(tile, t)
    nr, nc, nd = len(rows), len(consts), len(douts)
    diff_idx = [i for i in range(nr) if row_diff[i]]

    def body(*refs):
        rv = [r[...] for r in refs[:nr]]
        cv = [r[...] for r in refs[nr:nr + nc]]
        dv = [r[...] for r in refs[nr + nc:nr + nc + nd]]
        out_refs = refs[nr + nc + nd:]

        def f(*diff):
            full = list(rv)
            for k, i in enumerate(diff_idx):
                full[i] = diff[k]
            return fn(*full, *diff[len(diff_idx):])

        _, vjp = jax.vjp(f, *[rv[i] for i in diff_idx], *cv)
        grads = vjp(tuple(dv))
        for k in range(len(diff_idx)):
            out_refs[k][...] = grads[k]
        first = pl.program_id(0) == 0
        for k in range(nc):
            acc_ref = out_refs[len(diff_idx) + k]
            g = grads[len(diff_idx) + k]

            @pl.when(first)
            def _(acc_ref=acc_ref, g=g):
                acc_ref[...] = g

            @pl.when(jnp.logical_not(first))
            def _(acc_ref=acc_ref, g=g):
                acc_ref[...] += g

    res = pl.pallas_call(
        body, name=name, grid=(t // tile,),
        in_specs=[pl.BlockSpec((tile, r.shape[1]), lambda i: (i, 0)) for r in rows]
        + [pl.BlockSpec(c.shape, lambda i: (0, 0)) for c in consts]
        + [pl.BlockSpec((tile, d.shape[1]), lambda i: (i, 0)) for d in douts],
        out_specs=[pl.BlockSpec((tile, rows[i].shape[1]), lambda i_: (i_, 0)) for i in diff_idx]
        + [pl.BlockSpec(c.shape, lambda i: (0, 0)) for c in consts],
        out_shape=[jax.ShapeDtypeStruct(rows[i].shape, F32) for i in diff_idx]
        + [jax.ShapeDtypeStruct(c.shape, F32) for c in consts],
        compiler_params=_params(("arbitrary",)),
    )(*rows, *consts, *douts)
    return res[:len(diff_idx)], res[len(diff_idx):]


def make_rowwise(fn, name, row_diff, tile=ROW_TILE):
    @jax.custom_vjp
    def op(rows, consts):
        return tuple(_rowwise_fwd_call(fn, rows, consts, name + "_fwd", tile))

    def fwd(rows, consts):
        return op(rows, consts), (rows, consts)

    def bwd(res, douts):
        rows, consts = res
        drows, dconsts = _rowwise_bwd_call(fn, rows, consts, douts, row_diff, name + "_bwd", tile)
        it = iter(drows)
        full = tuple(next(it) if row_diff[i] else jnp.zeros_like(rows[i]) for i in range(len(rows)))
        return full, tuple(dconsts)

    op.defvjp(fwd, bwd)
    return op


def _conv_post_fn(cv, a_z, ng, nb, w_pw2):
    return (_mm(_silu(_ln(cv, ng, nb)), w_pw2) * _silu(a_z),)


def _mla_prep_fn(c_q, c_kv, krblk, cos4, sin4, qg, kvg, w_uq, w_uk, w_uv):
    qe = _mm(_rms(c_q, qg), w_uq)
    q = qe * cos4 + _roll(qe, qe.shape[1] - 32) * sin4
    cos1, sin1 = cos4[:, :LANES], sin4[:, :LANES]
    kr = krblk * cos1 + _roll(krblk, LANES - 32) * sin1
    kn = _rms(c_kv, kvg)
    k = _mm(kn, w_uk) + jnp.concatenate([kr, kr, kr, kr], axis=1)
    return q, k, _mm(kn, w_uv)


def _ssm_post_fn(y, u, c_z, d, w_a, w_b):
    y2 = _gelu_tanh(y + d * u)
    return (_mm(y2, w_a) * _sigmoid(_mm(y2, w_b)) * _silu(c_z),)


def _gate_fn(o, z):
    return (o * _silu(z),)


def _merge_fn(br0, br1, br2, br3, gl0, gl1, gl2, gl3, b0, b1, b2, b3):
    return (_sigmoid(gl0 + b0) * br0 + _sigmoid(gl1 + b1) * br1 + _sigmoid(gl2 + b2) * br2 + _sigmoid(gl3 + b3) * br3,)


def _ln_fn(x, mo, g, b):
    return (_ln(DEEPNORM_ALPHA * x + mo, g, b),)


def _ple_fn(x1, pe, gl, g):
    return (x1 + _rms(pe * _sigmoid(gl), g),)


op_conv_post = make_rowwise(_conv_post_fn, "conv_post", (True, True))
op_mla_prep = make_rowwise(_mla_prep_fn, "mla_prep", (True, True, True, False, False))
op_ssm_post = make_rowwise(_ssm_post_fn, "ssm_post", (True, True, True))
op_gate = make_rowwise(_gate_fn, "gate", (True, True))
op_merge = make_rowwise(_merge_fn, "merge", (True,) * 8, tile=ROW_TILE // 2)
op_ln = make_rowwise(_ln_fn, "post_ln", (True, True))
op_ple = make_rowwise(_ple_fn, "ple", (True, True, True))


def _conv_fwd_call(a_val, a_gate, w32, b):
    t, w = a_val.shape
    tile = min(ROW_TILE, t)
    per = tile // CONV_HALO
    cur = pl.BlockSpec((tile, w), lambda i: (i, 0))
    prev = pl.BlockSpec((CONV_HALO, w), lambda i: (jnp.maximum(i * per - 1, 0), 0))

    def body(av_ref, avh_ref, ag_ref, agh_ref, w_ref, b_ref, cv_ref, buf):
        i = pl.program_id(0)
        gh = avh_ref[...] * _sigmoid(agh_ref[...])
        buf[0:CONV_HALO, :] = jnp.where(i > 0, gh, 0.0)
        buf[CONV_HALO:, :] = av_ref[...] * _sigmoid(ag_ref[...])
        acc = jnp.zeros((tile, w), F32) + b_ref[...]
        for j in range(CONV_W):
            acc = acc + w_ref[j:j + 1, :] * buf[pl.ds(CONV_HALO - (CONV_W - 1) + j, tile), :]
        cv_ref[...] = acc

    return pl.pallas_call(
        body, name="conv_fwd", grid=(t // tile,),
        in_specs=[cur, prev, cur, prev, pl.BlockSpec((CONV_HALO, w), lambda i: (0, 0)), pl.BlockSpec((1, w), lambda i: (0, 0))],
        out_specs=cur, out_shape=jax.ShapeDtypeStruct((t, w), F32),
        scratch_shapes=[pltpu.VMEM((tile + CONV_HALO, w), F32)],
        compiler_params=_params(("parallel",)),
    )(a_val, a_val, a_gate, a_gate, w32, b)


def _conv_bwd_call(a_val, a_gate, w32, dcv):
    t, w = a_val.shape
    tile = min(ROW_TILE, t)
    n = t // tile
    per = tile // CONV_HALO
    cur = pl.BlockSpec((tile, w), lambda i: (i, 0))
    prev = pl.BlockSpec((CONV_HALO, w), lambda i: (jnp.maximum(i * per - 1, 0), 0))
    nxt = pl.BlockSpec((CONV_HALO, w), lambda i: (jnp.minimum((i + 1) * per, t // CONV_HALO - 1), 0))
    full = lambda r: pl.BlockSpec((r, w), lambda i: (0, 0))

    def body(av_ref, avh_ref, ag_ref, agh_ref, w_ref, d_ref, dn_ref, dav_ref, dag_ref, dw_ref, db_ref, gbuf, dbuf):
        i = pl.program_id(0)
        gh = avh_ref[...] * _sigmoid(agh_ref[...])
        gbuf[0:CONV_HALO, :] = jnp.where(i > 0, gh, 0.0)
        av = av_ref[...]
        sg = _sigmoid(ag_ref[...])
        gbuf[CONV_HALO:, :] = av * sg
        d = d_ref[...]
        dbuf[0:tile, :] = d
        dbuf[tile:, :] = jnp.where(i < n - 1, dn_ref[...], 0.0)

        @pl.when(i == 0)
        def _():
            dw_ref[...] = jnp.zeros_like(dw_ref)
            db_ref[...] = jnp.zeros_like(db_ref)

        dg = jnp.zeros((tile, w), F32)
        for j in range(CONV_W):
            dg = dg + w_ref[j:j + 1, :] * dbuf[pl.ds(CONV_W - 1 - j, tile), :]
            dw_ref[j:j + 1, :] += jnp.sum(d * gbuf[pl.ds(CONV_HALO - (CONV_W - 1) + j, tile), :], axis=0, keepdims=True)
        db_ref[...] += jnp.sum(d, axis=0, keepdims=True)
        dav_ref[...] = dg * sg
        dag_ref[...] = dg * av * sg * (1.0 - sg)

    return pl.pallas_call(
        body, name="conv_bwd", grid=(n,),
        in_specs=[cur, prev, cur, prev, full(CONV_HALO), cur, nxt],
        out_specs=[cur, cur, full(CONV_HALO), full(1)],
        out_shape=[jax.ShapeDtypeStruct((t, w), F32), jax.ShapeDtypeStruct((t, w), F32),
                   jax.ShapeDtypeStruct((CONV_HALO, w), F32), jax.ShapeDtypeStruct((1, w), F32)],
        scratch_shapes=[pltpu.VMEM((tile + CONV_HALO, w), F32), pltpu.VMEM((tile + CONV_HALO, w), F32)],
        compiler_params=_params(("arbitrary",)),
    )(a_val, a_val, a_gate, a_gate, w32, dcv, dcv)


def _pad_taps(conv_w):
    return jnp.concatenate([conv_w, jnp.zeros((CONV_HALO - CONV_W, conv_w.shape[1]), F32)], axis=0)


@jax.custom_vjp
def op_conv(a_val, a_gate, conv_w, conv_b):
    return _conv_fwd_call(a_val, a_gate, _pad_taps(conv_w), conv_b)


def _op_conv_fwd(a_val, a_gate, conv_w, conv_b):
    return op_conv(a_val, a_gate, conv_w, conv_b), (a_val, a_gate, conv_w)


def _op_conv_bwd(res, dcv):
    a_val, a_gate, conv_w = res
    dav, dag, dw, db = _conv_bwd_call(a_val, a_gate, _pad_taps(conv_w), dcv)
    return dav, dag, dw[:CONV_W], db


op_conv.defvjp(_op_conv_fwd, _op_conv_bwd)


def _head_masks(rows):
    lane = lax.broadcasted_iota(jnp.int32, (rows, LANES), 1)
    return lane < 64, lane >= 64


def _head_row(vals, mask):
    return jnp.max(jnp.where(mask, vals, NEG), axis=1, keepdims=True)


def _attn_valid(qpos, kpos, window):
    valid = kpos <= qpos
    if window is not None:
        valid = jnp.logical_and(valid, qpos - kpos < window)
    return valid


def _flash_fwd_call(q, k, v, sink, *, window, shared_k, scale, blk, blk_q, name):
    t = q.shape[0]
    qw = LANES if shared_k else 2 * LANES
    pairs = v.shape[1] // LANES
    tk = min(blk, t)
    tq = min(blk_q, t)
    has_sink = sink is not None

    def body(*refs):
        if has_sink:
            q_ref, k_ref, v_ref, s_ref, o_ref, lse_ref, k_mxu, v0_mxu, v1_mxu = refs
        else:
            q_ref, k_ref, v_ref, o_ref, lse_ref, k_mxu, v0_mxu, v1_mxu = refs
        v_mxu = (v0_mxu, v1_mxu)
        i = pl.program_id(1)

        @pl.when(i == 0)
        def _():
            full_masks = _head_masks(t)
            k_mxu[...] = k_ref[...].astype(MXU_DTYPE)
            for h in range(2):
                v_mxu[h][...] = jnp.where(full_masks[h], v_ref[...], 0.0).astype(MXU_DTYPE)

        qb = q_ref[...]
        masks = _head_masks(tq)
        row_masks = _head_masks(1)
        qh = [(jnp.where(masks[h], qb, 0.0) if shared_k else qb[:, h * LANES:(h + 1) * LANES]).astype(MXU_DTYPE) for h in range(2)]
        qpos = i * tq + lax.broadcasted_iota(jnp.int32, (tq, tk), 0)
        if has_sink:
            m_init = [jnp.zeros((tq, 1), F32) + _head_row(s_ref[...], row_masks[h]) for h in range(2)]
            l_init = [jnp.ones((tq, 1), F32)] * 2
        else:
            m_init = [jnp.full((tq, 1), NEG, F32)] * 2
            l_init = [jnp.zeros((tq, 1), F32)] * 2

        def make_step(masked):
            def step(j, carry):
                m0, l0, m1, l1, acc = carry
                start = pl.multiple_of(j * tk, tk)
                kb = k_mxu[pl.ds(start, tk), :]
                if masked:
                    valid = _attn_valid(qpos, j * tk + lax.broadcasted_iota(jnp.int32, (tq, tk), 1), window)
                new, alphas, pv = [], [], []
                for h, (m, l) in enumerate(((m0, l0), (m1, l1))):
                    kh = kb if shared_k else kb[:, h * LANES:(h + 1) * LANES]
                    s = _dot(qh[h], kh, NT) * scale
                    if masked:
                        s = jnp.where(valid, s, NEG)
                    m_new = jnp.maximum(m, jnp.max(s, axis=1, keepdims=True))
                    alpha = jnp.exp(m - m_new)
                    p = jnp.exp(s - m_new)
                    new += [m_new, alpha * l + jnp.sum(p, axis=1, keepdims=True)]
                    alphas.append(alpha)
                    pv.append(_dot(p, v_mxu[h][pl.ds(start, tk), :], NN))
                acc = acc * jnp.where(masks[0], alphas[0], alphas[1]) + pv[0] + pv[1]
                return new[0], new[1], new[2], new[3], acc
            return step

        carry = (m_init[0], l_init[0], m_init[1], l_init[1], jnp.zeros((tq, LANES), F32))
        last = (i * tq + tq - 1) // tk
        if window is None:
            n_full = (i * tq + 1) // tk
            carry = lax.fori_loop(0, n_full, make_step(False), carry)
            carry = lax.fori_loop(n_full, last + 1, make_step(True), carry)
        else:
            carry = lax.fori_loop(jnp.maximum(i * tq - (window - 1), 0) // tk, last + 1, make_step(True), carry)
        m0, l0, m1, l1, acc = carry
        o_ref[...] = acc / jnp.where(masks[0], l0, l1)
        lse_ref[...] = jnp.where(masks[0], m0 + jnp.log(l0), m1 + jnp.log(l1))

    in_specs = [pl.BlockSpec((tq, qw), lambda p, i: (i, p)), pl.BlockSpec((t, qw), lambda p, i: (0, p)),
                pl.BlockSpec((t, LANES), lambda p, i: (0, p))]
    args = [q, k, v]
    if has_sink:
        in_specs.append(pl.BlockSpec((1, LANES), lambda p, i: (0, p)))
        args.append(sink)
    blk_o = pl.BlockSpec((tq, LANES), lambda p, i: (i, p))
    return pl.pallas_call(
        body, name=name, grid=(pairs, t // tq), in_specs=in_specs, out_specs=[blk_o, blk_o],
        out_shape=[jax.ShapeDtypeStruct((t, pairs * LANES), F32)] * 2,
        scratch_shapes=[pltpu.VMEM((t, qw), MXU_DTYPE), pltpu.VMEM((t, LANES), MXU_DTYPE), pltpu.VMEM((t, LANES), MXU_DTYPE)],
        compiler_params=_params(("arbitrary", "arbitrary")),
    )(*args)


def _flash_bwd_call(q, k, v, sink, o, lse, do, *, window, shared_k, scale, blk, blk_q, name):
    t = q.shape[0]
    qw = LANES if shared_k else 2 * LANES
    pairs = v.shape[1] // LANES
    tk = min(blk, t)
    tq = min(blk_q, t)
    assert tk % tq == 0 or tq % tk == 0
    nq = t // tq
    has_sink = sink is not None

    def body(*refs):
        if has_sink:
            q_ref, k_ref, v_ref, o_ref, lse_ref, do_ref, s_ref, dq_ref, dk_ref, dv_ref, ds_ref = refs[:11]
        else:
            q_ref, k_ref, v_ref, o_ref, lse_ref, do_ref, dq_ref, dk_ref, dv_ref = refs[:9]
        q_mxu, do_mxu, lse_h, dsum_h = refs[-8:-6], refs[-6:-4], refs[-4:-2], refs[-2:]
        j = pl.program_id(1)
        masks = _head_masks(tq)
        row_masks = _head_masks(1)

        @pl.when(j == 0)
        def _():
            dq_ref[...] = jnp.zeros_like(dq_ref)
            full_masks = _head_masks(t)
            prod = do_ref[...] * o_ref[...]
            parts = []
            for h in range(2):
                qh = jnp.where(full_masks[h], q_ref[...], 0.0) if shared_k else q_ref[:, h * LANES:(h + 1) * LANES]
                q_mxu[h][...] = qh.astype(MXU_DTYPE)
                do_mxu[h][...] = jnp.where(full_masks[h], do_ref[...], 0.0).astype(MXU_DTYPE)
                dsum = jnp.sum(jnp.where(full_masks[h], prod, 0.0), axis=1, keepdims=True)
                lse = _head_row(lse_ref[...], full_masks[h])
                dsum_h[h][...] = jnp.zeros((t, LANES), F32) + dsum
                lse_h[h][...] = jnp.zeros((t, LANES), F32) + lse
                if has_sink:
                    ps = jnp.exp(_head_row(s_ref[...], row_masks[h]) - lse)
                    parts.append(-jnp.sum(ps * dsum, axis=0, keepdims=True))
            if has_sink:
                ds_ref[...] = jnp.zeros((SUBLANES, LANES), F32) + jnp.where(row_masks[0], parts[0], parts[1])

        kb = k_ref[...].astype(MXU_DTYPE)
        vb = v_ref[...].astype(MXU_DTYPE)
        kh = [kb if shared_k else kb[:, h * LANES:(h + 1) * LANES] for h in range(2)]
        kpos = j * tk + lax.broadcasted_iota(jnp.int32, (tq, tk), 1)
        lanes_of = lambda a: a if tk == LANES else jnp.concatenate([a] * (tk // LANES), axis=1)

        def make_step(masked):
            def step(i, carry):
                dk0, dk1, dv = carry
                start = pl.multiple_of(i * tq, tq)
                if masked:
                    valid = _attn_valid(i * tq + lax.broadcasted_iota(jnp.int32, (tq, tk), 0), kpos, window)
                dks, dqs = [], []
                for h in range(2):
                    qh = q_mxu[h][pl.ds(start, tq), :]
                    doh = do_mxu[h][pl.ds(start, tq), :]
                    s = _dot(qh, kh[h], NT) * scale
                    if masked:
                        s = jnp.where(valid, s, NEG)
                    p = jnp.exp(s - lanes_of(lse_h[h][pl.ds(start, tq), :]))
                    dp = _dot(doh, vb, NT)
                    dsc = p * (dp - lanes_of(dsum_h[h][pl.ds(start, tq), :])) * scale
                    dv = dv + _dot(p, doh, TN)
                    dks.append(_dot(dsc, qh, TN))
                    dq_h = _dot(dsc, kh[h], NN)
                    dqs.append(jnp.where(masks[h], dq_h, 0.0) if shared_k else dq_h)
                if shared_k:
                    dq_ref[pl.ds(start, tq), :] += dqs[0] + dqs[1]
                else:
                    dq_ref[pl.ds(start, tq), :] += jnp.concatenate(dqs, axis=1)
                return dk0 + dks[0], dk1 + dks[1], dv
            return step

        zero = jnp.zeros((tk, LANES), F32)
        carry = (zero, zero, zero)
        first = (j * tk) // tq
        if window is None:
            n_full = jnp.minimum(((j + 1) * tk + tq - 2) // tq, nq)
            carry = lax.fori_loop(first, n_full, make_step(True), carry)
            carry = lax.fori_loop(n_full, nq, make_step(False), carry)
        else:
            carry = lax.fori_loop(first, jnp.minimum(nq, (j * tk + tk - 1 + window - 1) // tq + 1), make_step(True), carry)
        dk0, dk1, dv = carry
        dk_ref[...] = dk0 + dk1 if shared_k else jnp.concatenate([dk0, dk1], axis=1)
        dv_ref[...] = dv

    full = lambda w: pl.BlockSpec((t, w), lambda p, j: (0, p))
    blkspec = lambda w: pl.BlockSpec((tk, w), lambda p, j: (j, p))
    in_specs = [full(qw), blkspec(qw), blkspec(LANES), full(LANES), full(LANES), full(LANES)]
    args = [q, k, v, o, lse, do]
    out_specs = [full(qw), blkspec(qw), blkspec(LANES)]
    out_shape = [jax.ShapeDtypeStruct(q.shape, F32), jax.ShapeDtypeStruct(k.shape, F32), jax.ShapeDtypeStruct(v.shape, F32)]
    if has_sink:
        in_specs.append(pl.BlockSpec((1, LANES), lambda p, j: (0, p)))
        args.append(sink)
        out_specs.append(pl.BlockSpec((SUBLANES, LANES), lambda p, j: (0, p)))
        out_shape.append(jax.ShapeDtypeStruct((SUBLANES, pairs * LANES), F32))
    return pl.pallas_call(
        body, name=name, grid=(pairs, t // tk), in_specs=in_specs, out_specs=out_specs, out_shape=out_shape,
        scratch_shapes=[pltpu.VMEM((t, LANES), MXU_DTYPE)] * 4 + [pltpu.VMEM((t, LANES), F32)] * 4,
        compiler_params=_params(("arbitrary", "arbitrary")),
    )(*args)


_MLA_CFG = dict(window=None, shared_k=False, scale=MLA_SCALE, blk=256)
_SWA_CFG = dict(window=WINDOW, shared_k=True, scale=SWA_SCALE, blk=128)
_MLA_FWD_CFG = dict(_MLA_CFG, blk=512, blk_q=256)
_SWA_FWD_CFG = dict(_SWA_CFG, blk=128, blk_q=128)
_MLA_BWD_CFG = dict(_MLA_CFG, blk=256, blk_q=512)
_SWA_BWD_CFG = dict(_SWA_CFG, blk=128, blk_q=128)


@jax.custom_vjp
def op_mla_attn(q, k, v):
    return _flash_fwd_call(q, k, v, None, name="mla_fwd", **_MLA_FWD_CFG)[0]


def _op_mla_attn_fwd(q, k, v):
    o, lse = _flash_fwd_call(q, k, v, None, name="mla_fwd", **_MLA_FWD_CFG)
    return o, (q, k, v, o, lse)


def _op_mla_attn_bwd(res, do):
    q, k, v, o, lse = res
    return tuple(_flash_bwd_call(q, k, v, None, o, lse, do, name="mla_bwd", **_MLA_BWD_CFG))


op_mla_attn.defvjp(_op_mla_attn_fwd, _op_mla_attn_bwd)


@jax.custom_vjp
def op_swa_attn(q, k, v, sink):
    return _flash_fwd_call(q, k, v, sink, name="swa_fwd", **_SWA_FWD_CFG)[0]


def _op_swa_attn_fwd(q, k, v, sink):
    o, lse = _flash_fwd_call(q, k, v, sink, name="swa_fwd", **_SWA_FWD_CFG)
    return o, (q, k, v, sink, o, lse)


def _op_swa_attn_bwd(res, do):
    q, k, v, sink, o, lse = res
    dq, dk, dv, dsink = _flash_bwd_call(q, k, v, sink, o, lse, do, name="swa_bwd", **_SWA_BWD_CFG)
    first_lane = lax.broadcasted_iota(jnp.int32, (1, dsink.shape[1]), 1) % 64 == 0
    return dq, dk, dv, jnp.where(first_lane, dsink[:1], 0.0)


op_swa_attn.defvjp(_op_swa_attn_fwd, _op_swa_attn_bwd)


def _complex_power(ar, ai, n):
    for _ in range(int(math.log2(n))):
        ar, ai = ar * ar - ai * ai, 2.0 * ar * ai
    return ar, ai


def _scan_passes(load_b, a1r, a1i, n, store, e_ref, c_ref, reverse):
    cb = a1r.shape[1]
    ar = jnp.zeros((SCAN_SEGMENTS, cb), F32) + a1r
    ai = jnp.zeros((SCAN_SEGMENTS, cb), F32) + a1i
    idx = (lambda ii: n - 1 - ii) if reverse else (lambda ii: ii)

    def local(ii, h):
        br, bi = load_b(idx(ii))
        return ar * h[0] - ai * h[1] + br, ar * h[1] + ai * h[0] + bi

    zero = jnp.zeros((SCAN_SEGMENTS, cb), F32)
    er, ei = lax.fori_loop(0, n, local, (zero, zero))
    e_ref[:, 0:cb] = er
    e_ref[:, cb:] = ei
    pr, pi_ = _complex_power(a1r, a1i, n)
    cr = jnp.zeros((1, cb), F32)
    ci = jnp.zeros((1, cb), F32)
    order = range(SCAN_SEGMENTS - 1, -1, -1) if reverse else range(SCAN_SEGMENTS)
    for s in order:
        c_ref[s:s + 1, 0:cb] = cr
        c_ref[s:s + 1, cb:] = ci
        er1, ei1 = e_ref[s:s + 1, 0:cb], e_ref[s:s + 1, cb:]
        cr, ci = pr * cr - pi_ * ci + er1, pr * ci + pi_ * cr + ei1

    def second(ii, h):
        i = idx(ii)
        hr, hi = local(ii, h)
        store(i, hr, hi)
        return hr, hi

    lax.fori_loop(0, n, second, (c_ref[:, 0:cb], c_ref[:, cb:]))


def _scan_fwd_call(bu, lam):
    n = bu.shape[0]
    cb = SCAN_CB
    blk3 = pl.BlockSpec((n, SCAN_SEGMENTS, 2 * cb), lambda c: (0, 0, c))
    blk2 = lambda r: pl.BlockSpec((r, 2 * cb), lambda c: (0, c))

    def body(b_ref, lam_ref, h_ref, cin_ref, e_ref):
        def store(i, hr, hi):
            h_ref[i, :, 0:cb] = hr
            h_ref[i, :, cb:] = hi

        _scan_passes(lambda i: (b_ref[i, :, 0:cb], b_ref[i, :, cb:]), lam_ref[:, 0:cb], lam_ref[:, cb:], n, store,
                     e_ref, cin_ref, False)

    return pl.pallas_call(
        body, name="scan_fwd", grid=(SSM_CH // cb,), in_specs=[blk3, blk2(1)], out_specs=[blk3, blk2(SCAN_SEGMENTS)],
        out_shape=[jax.ShapeDtypeStruct(bu.shape, F32), jax.ShapeDtypeStruct((SCAN_SEGMENTS, 2 * SSM_CH), F32)],
        scratch_shapes=[pltpu.VMEM((SCAN_SEGMENTS, 2 * cb), F32)],
        compiler_params=_params(("parallel",)),
    )(bu, lam)


def _scan_bwd_call(dh, h, cin, lam):
    n = dh.shape[0]
    cb = SCAN_CB
    blk3 = pl.BlockSpec((n, SCAN_SEGMENTS, 2 * cb), lambda c: (0, 0, c))
    blk2 = lambda r: pl.BlockSpec((r, 2 * cb), lambda c: (0, c))

    def body(d_ref, h_ref, cin_ref, lam_ref, g_ref, dlam_ref, e_ref, c_ref, acc_ref):
        acc_ref[...] = jnp.zeros_like(acc_ref)

        def store(i, gr, gi):
            g_ref[i, :, 0:cb] = gr
            g_ref[i, :, cb:] = gi
            ip = jnp.maximum(i - 1, 0)
            hpr = jnp.where(i > 0, h_ref[ip, :, 0:cb], cin_ref[:, 0:cb])
            hpi = jnp.where(i > 0, h_ref[ip, :, cb:], cin_ref[:, cb:])
            acc_ref[:, 0:cb] += gr * hpr + gi * hpi
            acc_ref[:, cb:] += gi * hpr - gr * hpi

        _scan_passes(lambda i: (d_ref[i, :, 0:cb], d_ref[i, :, cb:]), lam_ref[:, 0:cb], -lam_ref[:, cb:], n, store,
                     e_ref, c_ref, True)
        dlam_ref[...] = acc_ref[...]

    return pl.pallas_call(
        body, name="scan_bwd", grid=(SSM_CH // cb,), in_specs=[blk3, blk3, blk2(SCAN_SEGMENTS), blk2(1)],
        out_specs=[blk3, blk2(SCAN_SEGMENTS)],
        out_shape=[jax.ShapeDtypeStruct(dh.shape, F32), jax.ShapeDtypeStruct((SCAN_SEGMENTS, 2 * SSM_CH), F32)],
        scratch_shapes=[pltpu.VMEM((SCAN_SEGMENTS, 2 * cb), F32)] * 3,
        compiler_params=_params(("parallel",)),
    )(dh, h, cin, lam)


@jax.custom_vjp
def op_scan(bu, lam):
    return _scan_fwd_call(bu, lam)[0]


def _op_scan_fwd(bu, lam):
    h, cin = _scan_fwd_call(bu, lam)
    return h, (h, cin, lam)


def _op_scan_bwd(res, dh):
    h, cin, lam = res
    g, dlam = _scan_bwd_call(dh, h, cin, lam)
    return g, jnp.sum(dlam, axis=0, keepdims=True)


op_scan.defvjp(_op_scan_fwd, _op_scan_bwd)


def _loss_call(y, target):
    t, d = y.shape
    tile = min(ROW_TILE, t)

    def body(y_ref, t_ref, dy_ref, acc_ref):
        @pl.when(pl.program_id(0) == 0)
        def _():
            acc_ref[...] = jnp.zeros_like(acc_ref)

        err = y_ref[...] - t_ref[...]
        dy_ref[...] = err * (1.0 / d)
        col = jnp.sum(err * err, axis=0, keepdims=True)
        part = col[:, 0:LANES]
        for c in range(1, d // LANES):
            part = part + col[:, c * LANES:(c + 1) * LANES]
        acc_ref[0:1, :] += part

    blk = pl.BlockSpec((tile, d), lambda i: (i, 0))
    dy, acc = pl.pallas_call(
        body, name="loss_head", grid=(t // tile,), in_specs=[blk, blk],
        out_specs=[blk, pl.BlockSpec((SUBLANES, LANES), lambda i: (0, 0))],
        out_shape=[jax.ShapeDtypeStruct((t, d), F32), jax.ShapeDtypeStruct((SUBLANES, LANES), F32)],
        compiler_params=_params(("arbitrary",)),
    )(y, target)
    return jnp.sum(acc) * (0.5 / d), dy


def _rot_cols(w):
    return jnp.concatenate([-w[:, 16:], w[:, :16]], axis=1)


def _ext_w_in(w):
    a_val, a_gate, a_z, c_q, c_kv, k_r, b_z, u, c_z, q, k, v, d_z = jnp.split(
        w, (256, 512, 768, 1024, 1152, 1184, 1440, 1696, 1952, 2208, 2336, 2464), axis=1)
    dup = lambda m: jnp.concatenate([m[:, :64], m[:, :64], m[:, 64:], m[:, 64:]], axis=1)
    krblk = jnp.concatenate([jnp.zeros((w.shape[0], 64), w.dtype), k_r, _rot_cols(k_r)], axis=1)
    return jnp.concatenate([a_val, a_gate, a_z, c_q, b_z, u, c_z, q, dup(k), dup(v), d_z, c_kv, krblk], axis=1)


IN_WIDTH = 2720
IN_SHARD = IN_WIDTH // 4
IN_SHARD_PAD = 768
IN_EXT = 3072


def _w_in_layout_matrix():
    src = _ext_w_in(jnp.arange(1, IN_WIDTH + 1, dtype=F32)[None, :])[0]
    col = jnp.abs(src).astype(jnp.int32) - 1
    row_of_col = (col // IN_SHARD) * IN_SHARD_PAD + col % IN_SHARD
    rows = lax.broadcasted_iota(jnp.int32, (4 * IN_SHARD_PAD, IN_EXT), 0)
    return jnp.where(rows == row_of_col[None, :], jnp.sign(src)[None, :], 0.0).astype(MXU_DTYPE)


H_COLS = dict(a_val=256, a_gate=256, a_z=256, c_q=256, b_z=256, u=256, c_z=256, q=256, kdup=256, vdup=256, d_z=256,
              c_kv=128, krblk=128)
op_in_proj = make_proj(tuple(H_COLS.values()), "in_proj")
op_merge_proj = make_proj((D_MODEL,) * 4, "merge_proj")


def _ext_mla(w_uq, w_ukv):
    zeros = jnp.zeros((w_ukv.shape[0], 64), w_ukv.dtype)
    uq, uk, uv = [], [], []
    for h in range(4):
        nope, rp = w_uq[:, 96 * h:96 * h + 64], w_uq[:, 96 * h + 64:96 * h + 96]
        uq += [nope, rp, _rot_cols(rp)]
        uk += [w_ukv[:, 128 * h:128 * h + 64], zeros]
        uv.append(w_ukv[:, 128 * h + 64:128 * h + 128])
    return jnp.concatenate(uq, axis=1), jnp.concatenate(uk, axis=1), jnp.concatenate(uv, axis=1)


def _scan_cols(re, im):
    parts = []
    for c in range(SSM_CH // SCAN_CB):
        parts += [re[..., c * SCAN_CB:(c + 1) * SCAN_CB], im[..., c * SCAN_CB:(c + 1) * SCAN_CB]]
    return jnp.concatenate(parts, axis=-1)


def _ext_ssm(a_re, a_im, log_dt, b_re, b_im, c_re, c_im):
    dt = jnp.exp(log_dt)[:, None]
    mag = jnp.exp(a_re * dt)
    lb_re, lb_im = mag * jnp.cos(a_im * dt), mag * jnp.sin(a_im * dt)
    den = a_re * a_re + a_im * a_im
    nr, ni = lb_re - 1.0, lb_im
    f_re = ((nr * a_re + ni * a_im) / den)[..., None]
    f_im = ((ni * a_re - nr * a_im) / den)[..., None]
    bb_re = f_re * b_re - f_im * b_im
    bb_im = f_re * b_im + f_im * b_re
    eye = jnp.eye(SSM_GROUPS, dtype=F32)
    spread = lambda a: a.transpose(0, 2, 1)[:, :, None, :] * eye[:, None, :, None]
    bd_in = lambda bb: spread(bb).reshape(SSM_GROUPS * SSM_GROUP, SSM_CH)
    bd_out = lambda cc: spread(cc).reshape(SSM_CH, SSM_GROUPS * SSM_GROUP)
    w_bu = _scan_cols(bd_in(bb_re), bd_in(bb_im))
    w_y = _scan_cols(bd_out(c_re).T, -bd_out(c_im).T).T
    lam = _scan_cols(lb_re.reshape(1, SSM_CH), lb_im.reshape(1, SSM_CH))
    return w_bu, w_y, lam


def _rope_tables(t):
    pos = jnp.arange(t, dtype=F32)
    inv_freq = ROPE_THETA ** (-jnp.arange(0, 32, 2, dtype=F32) / 32)
    ang = pos[:, None] * inv_freq[None, :]
    cos, sin = jnp.cos(ang), jnp.sin(ang)
    ones, z32, z64 = jnp.ones((t, 64), F32), jnp.zeros((t, 32), F32), jnp.zeros((t, 64), F32)
    cos1 = jnp.concatenate([ones, cos, cos, z32], axis=1)
    sin1 = jnp.concatenate([z64, sin, sin, z32], axis=1)
    return jnp.concatenate([cos1] * 4, axis=1), jnp.concatenate([sin1] * 4, axis=1)


def _to_segments(a):
    t, w = a.shape
    return a.reshape(SCAN_SEGMENTS, t // SCAN_SEGMENTS, w).transpose(1, 0, 2)


def _from_segments(a):
    n, s, w = a.shape
    return a.transpose(1, 0, 2).reshape(n * s, w)


def _layer(x, p_i, cos4, sin4, e_mat, w):
    t = x.shape[0]
    row = lambda v: v.reshape(1, -1)
    f32 = lambda v: v.astype(F32)
    hs = dict(zip(H_COLS, op_in_proj(x, op_mm(w["w_in_pad"], e_mat))))

    cv = op_conv(hs["a_val"], hs["a_gate"], w["conv_w"], row(w["conv_b"]))
    (y_a,) = op_conv_post((cv, hs["a_z"]), (row(w["conv_norm_g"]), row(w["conv_norm_b"]), f32(w["w_pw2"])))

    w_uq, w_uk, w_uv = _ext_mla(w["w_uq"], f32(w["w_ukv"]))
    q, k, v = op_mla_prep((hs["c_q"], hs["c_kv"], hs["krblk"], cos4, sin4),
                          (row(w["mla_q_norm_g"]), row(w["mla_kv_norm_g"]), w_uq, w_uk, w_uv))
    (y_b,) = op_gate((op_mla_attn(q, k, v), hs["b_z"]), ())

    w_bu, w_y, lam = _ext_ssm(w["ssm_a_re"], w["ssm_a_im"], w["ssm_log_dt"], w["ssm_b_re"], w["ssm_b_im"],
                              w["ssm_c_re"], w["ssm_c_im"])
    u_seg = _to_segments(hs["u"]).reshape(t, BRANCH_W)
    bu = op_mm(u_seg, w_bu).reshape(t // SCAN_SEGMENTS, SCAN_SEGMENTS, 2 * SSM_CH)
    hstate = op_scan(bu, lam).reshape(t, 2 * SSM_CH)
    y_ssm = _from_segments(op_mm(hstate, w_y).reshape(t // SCAN_SEGMENTS, SCAN_SEGMENTS, BRANCH_W))
    w_glu = f32(w["w_glu"])
    (y_c,) = op_ssm_post((y_ssm, hs["u"], hs["c_z"]), (row(w["ssm_d"]), w_glu[:, :BRANCH_W], w_glu[:, BRANCH_W:]))

    sink = jnp.repeat(w["attn_sinks"], 64).reshape(1, 2 * LANES)
    (y_d,) = op_gate((op_swa_attn(hs["q"], hs["kdup"], hs["vdup"], sink), hs["d_z"]), ())

    br = [op_mm(y, w["w_branch"][n]) for n, y in enumerate((y_a, y_b, y_c, y_d))]
    gl = op_merge_proj(x, w["w_merge"])
    bm = [row(w["b_merge"][n * D_MODEL:(n + 1) * D_MODEL]) for n in range(4)]
    (merged,) = op_merge((*br, *gl), tuple(bm))
    (x1,) = op_ln((x, op_mm(merged, w["w_out"])), (row(w["ln_g"]), row(w["ln_b"])))
    (out,) = op_ple((x1, op_mm(p_i, w["w_ple"]), op_mm(x1, w["w_ple_gate"])), (row(w["ple_norm_g"]),))
    return out


def _forward(x, p, layers):
    cos4, sin4 = _rope_tables(x.shape[0])
    e_mat = _w_in_layout_matrix()
    for i in range(DEPTH):
        x = _layer(x, p[i], cos4, sin4, e_mat, layers[i])
    return x


SHARD_AXIS = dict(w_in=2, w_merge=2, conv_w=2, w_pw2=1, w_uq=2, w_ukv=2, w_glu=2, w_branch=3, w_out=1, w_ple=2, w_ple_gate=1)
ODD = ("w_uq", "conv_w")
BIG = tuple(n for n in SHARD_AXIS if n not in ODD)
REPLICATED = ("b_merge", "conv_b", "conv_norm_g", "conv_norm_b", "mla_q_norm_g", "mla_kv_norm_g", "ssm_a_re", "ssm_a_im",
              "ssm_log_dt", "ssm_b_re", "ssm_b_im", "ssm_c_re", "ssm_c_im", "ssm_d", "attn_sinks", "ln_g", "ln_b", "ple_norm_g")
WEIGHTS = ("w_in", "w_merge", "b_merge", "conv_w", "conv_b", "conv_norm_g", "conv_norm_b", "w_pw2", "mla_q_norm_g",
           "mla_kv_norm_g", "w_uq", "w_ukv", "ssm_a_re", "ssm_a_im", "ssm_log_dt", "ssm_b_re", "ssm_b_im", "ssm_c_re",
           "ssm_c_im", "ssm_d", "w_glu", "attn_sinks", "w_branch", "w_out", "ln_g", "ln_b", "w_ple", "w_ple_gate", "ple_norm_g")
PACK_COLS = 1024
PACK_ROWS = 512
CHIP_FLIPS = ((1, 0), (0, 1), (1, 1))
N_CHIPS = 4
N_DEV = 8


def _pack(arrays, dtype):
    flat = jnp.concatenate([a.reshape(-1).astype(dtype) for a in arrays])
    unit = PACK_ROWS * PACK_COLS
    total = -(-flat.shape[0] // unit) * unit
    return jnp.concatenate([flat, jnp.zeros((total - flat.shape[0],), dtype)]).reshape(-1, PACK_COLS)


def _unpack(buf, shapes):
    flat = buf.reshape(-1)
    out, off = [], 0
    for s in shapes:
        n = math.prod(s)
        out.append(flat[off:off + n].reshape(s))
        off += n
    return out


def _flip(v, bit):
    return 1 - v if bit else v


def _window(ref, axis, start, size):
    idx = [slice(None)] * len(ref.shape)
    idx[axis] = pl.ds(start, size)
    return ref.at[tuple(idx)]


def _gather_chips(srcs, axes, stacked):
    units = []
    for k, (s, a) in enumerate(zip(srcs, axes)):
        if stacked[k]:
            units += [(k, l, s.shape[1:], a - 1) for l in range(s.shape[0])]
        else:
            units.append((k, None, s.shape, a))
    nu, nb = len(units), len(srcs)

    def body(*refs):
        ins, outs = refs[:nb], refs[nb:nb + nu]
        ici_send, ici_recv, d2d_send, d2d_recv, local_sems = refs[nb + nu:]
        x, y, c = lax.axis_index("x"), lax.axis_index("y"), lax.axis_index("c")
        me = 2 * x + y

        def mine(u, half=None):
            k, l, shape, _ = units[u]
            ref = ins[k] if l is None else ins[k].at[l]
            return ref if half is None else ref.at[pl.ds(half * (shape[0] // 2), shape[0] // 2)]

        def place(u, chip, half=None):
            _, _, shape, a = units[u]
            size, rows = shape[a], shape[0] // 2
            if half is None:
                return _window(outs[u], a, chip * size, size)
            if a == 0:
                return outs[u].at[pl.ds(chip * size + half * rows, rows)]
            return _window(outs[u].at[pl.ds(half * rows, rows)], a, chip * size, size)

        local = [pltpu.make_async_copy(mine(u), place(u, me), local_sems.at[u]) for u in range(nu)]
        for cp in local:
            cp.start()
        sends = []
        for j, (bx, by) in enumerate(CHIP_FLIPS):
            for u in range(nu):
                cp = pltpu.make_async_remote_copy(src_ref=mine(u, c), dst_ref=place(u, me, c),
                                                  send_sem=ici_send.at[j * nu + u], recv_sem=ici_recv.at[j * nu + u],
                                                  device_id=(_flip(x, bx), _flip(y, by), c), device_id_type=MESH)
                cp.start()
                sends.append(cp)
        for j, (bx, by) in enumerate(CHIP_FLIPS):
            src = 2 * _flip(x, bx) + _flip(y, by)
            for u in range(nu):
                got = place(u, src, c)
                pltpu.make_async_remote_copy(src_ref=got, dst_ref=got, send_sem=ici_send.at[j * nu + u],
                                             recv_sem=ici_recv.at[j * nu + u], device_id=(x, y, c), device_id_type=MESH).wait_recv()
                cp = pltpu.make_async_remote_copy(src_ref=got, dst_ref=got, send_sem=d2d_send.at[j * nu + u],
                                                  recv_sem=d2d_recv.at[j * nu + u], device_id=(x, y, 1 - c), device_id_type=MESH)
                cp.start()
                sends.append(cp)
        for j, (bx, by) in enumerate(CHIP_FLIPS):
            src = 2 * _flip(x, bx) + _flip(y, by)
            for u in range(nu):
                other = place(u, src, 1 - c)
                pltpu.make_async_remote_copy(src_ref=other, dst_ref=other, send_sem=d2d_send.at[j * nu + u],
                                             recv_sem=d2d_recv.at[j * nu + u], device_id=(x, y, c), device_id_type=MESH).wait_recv()
        for cp in sends:
            cp.wait_send()
        for cp in local:
            cp.wait()

    full = lambda shape, a: tuple(N_CHIPS * d if i == a else d for i, d in enumerate(shape))
    res = pl.pallas_call(
        body, name="gather_weights", in_specs=[ANY] * nb, out_specs=[ANY] * nu,
        out_shape=[jax.ShapeDtypeStruct(full(shape, a), srcs[k].dtype) for k, _, shape, a in units],
        scratch_shapes=[pltpu.SemaphoreType.DMA((3 * nu,))] * 4 + [pltpu.SemaphoreType.DMA((nu,))],
    )(*srcs)
    out, it = [], iter(res)
    for k in range(nb):
        out.append([next(it) for _ in range(srcs[k].shape[0])] if stacked[k] else next(it))
    return out


def _exchange_grads(grads, axes, smalls):
    nt, ns = len(grads), len(smalls)
    sizes = [g[0].shape[a] // N_CHIPS for g, a in zip(grads, axes)]
    dev_flips = [(bx, by, bc) for bx in (0, 1) for by in (0, 1) for bc in (0, 1)][1:]
    n_remote = 3 * nt * DEPTH + 7 * ns
    n_local = nt * DEPTH + ns

    def body(*refs):
        g_refs = [refs[k * DEPTH:(k + 1) * DEPTH] for k in range(nt)]
        s_refs = refs[nt * DEPTH:nt * DEPTH + ns]
        outs = refs[nt * DEPTH + ns:nt * DEPTH + ns + nt + ns]
        recv_refs, all_refs = outs[:nt], outs[nt:]
        send_sems, recv_sems, local_sems = refs[-3:]
        x, y, c = lax.axis_index("x"), lax.axis_index("y"), lax.axis_index("c")
        me_chip = 2 * x + y
        me = 4 * x + 2 * y + c
        part = lambda k, i, chip: _window(g_refs[k][i], axes[k], chip * sizes[k], sizes[k])
        started, waits = [], []
        sem, lsem = 0, 0
        for k in range(nt):
            for i in range(DEPTH):
                cp = pltpu.make_async_copy(part(k, i, me_chip), recv_refs[k].at[i, 3], local_sems.at[lsem])
                cp.start()
                started.append(cp.wait)
                lsem += 1
                for j, (bx, by) in enumerate(CHIP_FLIPS):
                    px, py = _flip(x, bx), _flip(y, by)
                    cp = pltpu.make_async_remote_copy(src_ref=part(k, i, 2 * px + py), dst_ref=recv_refs[k].at[i, j],
                                                      send_sem=send_sems.at[sem], recv_sem=recv_sems.at[sem],
                                                      device_id=(px, py, c), device_id_type=MESH)
                    cp.start()
                    started.append(cp.wait_send)
                    waits.append(cp.wait_recv)
                    sem += 1
        for s in range(ns):
            cp = pltpu.make_async_copy(s_refs[s], all_refs[s].at[me], local_sems.at[lsem])
            cp.start()
            started.append(cp.wait)
            lsem += 1
            for bx, by, bc in dev_flips:
                peer = (_flip(x, bx), _flip(y, by), _flip(c, bc))
                cp = pltpu.make_async_remote_copy(src_ref=s_refs[s], dst_ref=all_refs[s].at[me], send_sem=send_sems.at[sem],
                                                  recv_sem=recv_sems.at[sem], device_id=peer, device_id_type=MESH)
                cp.start()
                started.append(cp.wait_send)
                src = 4 * peer[0] + 2 * peer[1] + peer[2]
                waits.append(pltpu.make_async_remote_copy(src_ref=s_refs[s], dst_ref=all_refs[s].at[src], send_sem=send_sems.at[sem],
                                                          recv_sem=recv_sems.at[sem], device_id=peer, device_id_type=MESH).wait_recv)
                sem += 1
        for w in waits + started:
            w()

    shard = lambda g, a: tuple(d // N_CHIPS if i == a else d for i, d in enumerate(g.shape))
    flat = [g for per_layer in grads for g in per_layer]
    return pl.pallas_call(
        body, name="exchange_grads", in_specs=[ANY] * (len(flat) + ns), out_specs=[ANY] * (nt + ns),
        out_shape=[jax.ShapeDtypeStruct((DEPTH, N_CHIPS, *shard(g[0], a)), g[0].dtype) for g, a in zip(grads, axes)]
        + [jax.ShapeDtypeStruct((N_DEV, *s.shape), s.dtype) for s in smalls],
        scratch_shapes=[pltpu.SemaphoreType.DMA((n_remote,)), pltpu.SemaphoreType.DMA((n_remote,)), pltpu.SemaphoreType.DMA((n_local,))],
    )(*flat, *smalls)


def _swap_cores(parts):
    nb = len(parts)

    def body(*refs):
        ins, outs, send_sems, recv_sems = refs[:nb], refs[nb:2 * nb], refs[-2], refs[-1]
        x, y, c = lax.axis_index("x"), lax.axis_index("y"), lax.axis_index("c")
        cps = [pltpu.make_async_remote_copy(src_ref=ins[k], dst_ref=outs[k], send_sem=send_sems.at[k], recv_sem=recv_sems.at[k],
                                            device_id=(x, y, 1 - c), device_id_type=MESH) for k in range(nb)]
        for cp in cps:
            cp.start()
        for cp in cps:
            cp.wait()

    return pl.pallas_call(
        body, name="swap_cores", in_specs=[ANY] * nb, out_specs=[ANY] * nb,
        out_shape=[jax.ShapeDtypeStruct(q.shape, q.dtype) for q in parts],
        scratch_shapes=[pltpu.SemaphoreType.DMA((nb,)), pltpu.SemaphoreType.DMA((nb,))],
    )(*parts)


def _sum_chips_call(recv, cols, name):
    depth, _, r, c = recv.shape
    tile = _pick(r, (512, 256, 128, 64, 32, 16))

    def body(r_ref, o_ref):
        slot = lambda s: r_ref[s, :, pl.ds(0, cols)].astype(F32)
        o_ref[...] = ((slot(3) + slot(0)) + slot(1)) + slot(2)

    return pl.pallas_call(
        body, name=name, grid=(depth, r // tile),
        in_specs=[pl.BlockSpec((None, N_CHIPS, tile, c), lambda l, i: (l, 0, i, 0))],
        out_specs=pl.BlockSpec((None, tile, cols), lambda l, i: (l, i, 0)), out_shape=jax.ShapeDtypeStruct((depth, r, cols), F32),
        compiler_params=_params(("parallel", "parallel")),
    )(recv)


def _sum_slots_call(slots, name):
    n, r, c = slots.shape
    tile = _pick(r, (512, 256, 128, 64, 32, 16, 8))

    def body(s_ref, o_ref):
        acc = s_ref[0]
        for s in range(1, n):
            acc = acc + s_ref[s]
        o_ref[...] = acc

    return pl.pallas_call(
        body, name=name, grid=(r // tile,), in_specs=[pl.BlockSpec((n, tile, c), lambda i: (0, i, 0))],
        out_specs=pl.BlockSpec((tile, c), lambda i: (i, 0)), out_shape=jax.ShapeDtypeStruct((r, c), F32),
        compiler_params=_params(("parallel",)),
    )(slots)


def _adamw_math(w, g, m, v):
    m = ADAM_B1 * m + (1.0 - ADAM_B1) * g
    v = ADAM_B2 * v + (1.0 - ADAM_B2) * (g * g)
    m_hat = m / (1.0 - ADAM_B1 ** ADAM_STEP)
    v_hat = v / (1.0 - ADAM_B2 ** ADAM_STEP)
    return -ADAM_LR * (m_hat / (jnp.sqrt(v_hat) + ADAM_EPS) + ADAM_WD * w), m, v


def _adamw_call(w, m, v, gparts, name):
    r, c = w.shape
    n = len(gparts)
    tile = _pick(r, (512, 256, 128, 64, 32, 16, 8))

    def body(w_ref, m_ref, v_ref, *refs):
        g_refs, (go_ref, d_ref, mo_ref, vo_ref) = refs[:n], refs[n:]
        g = g_refs[0][...]
        for g_ref in g_refs[1:]:
            g = g + g_ref[...]
        go_ref[...] = g
        d_ref[...], mo_ref[...], vo_ref[...] = _adamw_math(w_ref[...], g, m_ref[...], v_ref[...])

    blk = pl.BlockSpec((tile, c), lambda i: (i, 0))
    return pl.pallas_call(
        body, name=name, grid=(r // tile,), in_specs=[blk] * (3 + n),
        out_specs=[blk] * 4, out_shape=[jax.ShapeDtypeStruct((r, c), F32)] * 4,
        compiler_params=_params(("parallel",)),
    )(w, m, v, *gparts)


def _train_local(x, p, layers, target):
    y, vjp = jax.vjp(lambda x_, w_: _forward(x_, p, w_), x, layers)
    loss, dy = _loss_call(y, target)
    dx, dw = vjp(dy)
    return loss, dx, dw


def kernel(x, p, w_in, w_merge, b_merge, conv_w, conv_b, conv_norm_g, conv_norm_b, w_pw2, mla_q_norm_g, mla_kv_norm_g, w_uq, w_ukv, ssm_a_re, ssm_a_im, ssm_log_dt, ssm_b_re, ssm_b_im, ssm_c_re, ssm_c_im, ssm_d, w_glu, attn_sinks, w_branch, w_out, ln_g, ln_b, w_ple, w_ple_gate, ple_norm_g, loss_target, m_w_in, m_w_merge, m_b_merge, m_conv_w, m_conv_b, m_conv_norm_g, m_conv_norm_b, m_w_pw2, m_mla_q_norm_g, m_mla_kv_norm_g, m_w_uq, m_w_ukv, m_ssm_a_re, m_ssm_a_im, m_ssm_log_dt, m_ssm_b_re, m_ssm_b_im, m_ssm_c_re, m_ssm_c_im, m_ssm_d, m_w_glu, m_attn_sinks, m_w_branch, m_w_out, m_ln_g, m_ln_b, m_w_ple, m_w_ple_gate, m_ple_norm_g, v_w_in, v_w_merge, v_b_merge, v_conv_w, v_conv_b, v_conv_norm_g, v_conv_norm_b, v_w_pw2, v_mla_q_norm_g, v_mla_kv_norm_g, v_w_uq, v_w_ukv, v_ssm_a_re, v_ssm_a_im, v_ssm_log_dt, v_ssm_b_re, v_ssm_b_im, v_ssm_c_re, v_ssm_c_im, v_ssm_d, v_w_glu, v_attn_sinks, v_w_branch, v_w_out, v_ln_g, v_ln_b, v_w_ple, v_w_ple_gate, v_ple_norm_g):
    given = dict(locals())
    w_loc = {n: given[n] for n in WEIGHTS}
    m_loc = {n: given["m_" + n] for n in WEIGHTS}
    v_loc = {n: given["v_" + n] for n in WEIGHTS}

    me_chip = 2 * lax.axis_index("x") + lax.axis_index("y")

    wire = {n: w_loc[n].astype(MXU_DTYPE) for n in BIG}
    wire["w_in"] = jnp.pad(wire["w_in"], ((0, 0), (0, 0), (0, IN_SHARD_PAD - IN_SHARD)))
    odd_shapes = [w_loc[n].shape for n in ODD]
    gathered = _gather_chips([wire[n] for n in BIG] + [_pack([w_loc[n] for n in ODD], F32)], [SHARD_AXIS[n] for n in BIG] + [0],
                             [True] * len(BIG) + [False])
    full = dict(zip(BIG, gathered[:-1]))
    odd_parts = [_unpack(part, odd_shapes) for part in jnp.split(gathered[-1], N_CHIPS, axis=0)]
    for k, n in enumerate(ODD):
        full[n] = jnp.concatenate([odd_parts[s][k] for s in range(N_CHIPS)], axis=SHARD_AXIS[n])
    layers = []
    for i in range(DEPTH):
        layer = {n: (full[n][i] if n in full else w_loc[n][i]) for n in WEIGHTS if n != "w_in"}
        layer["w_in_pad"] = full["w_in"][i]
        layers.append(layer)

    loss, dx, dw = _train_local(x[0], p[:, 0], layers, loss_target[0])
    loss = lax.psum(loss, ("x", "y", "c"))

    key = lambda n: "w_in_pad" if n == "w_in" else n
    small_rep = _pack([dw[i][n] for n in REPLICATED for i in range(DEPTH)], F32)
    small_odd = _pack([dw[i][n] for n in ODD for i in range(DEPTH)], F32)
    *recv, all_rep, all_odd = _exchange_grads([[dw[i][key(n)] for i in range(DEPTH)] for n in BIG],
                                              [SHARD_AXIS[n] - 1 for n in BIG], [small_rep, small_odd])
    parts = []
    for n, r in zip(BIG, recv):
        cols = w_loc[n].shape[-1]
        parts.append(_sum_chips_call(r.reshape(DEPTH, N_CHIPS, -1, r.shape[-1]), cols, "sum_chips_" + n))
    others = _swap_cores(parts)
    g_rep = _sum_slots_call(all_rep, "sum_replicated")
    g_odd = _unpack(_sum_slots_call(all_odd, "sum_odd"), [(DEPTH, *w_loc[n].shape[1:-1], N_CHIPS * w_loc[n].shape[-1]) for n in ODD])

    grads, deltas, new_m, new_v = {}, {}, {}, {}

    def adamw(n, gparts):
        shape = w_loc[n].shape
        two_d = lambda a: a.reshape(-1, shape[-1])
        res = _adamw_call(two_d(w_loc[n]), two_d(m_loc[n]), two_d(v_loc[n]), [two_d(g) for g in gparts], "adamw_" + n)
        grads[n], deltas[n], new_m[n], new_v[n] = [r.reshape(shape) for r in res]

    for n, part, other in zip(BIG, parts, others):
        adamw(n, [part, other])
    for n, g in zip(ODD, g_odd):
        size = w_loc[n].shape[-1]
        adamw(n, [lax.dynamic_slice_in_dim(g, me_chip * size, size, axis=g.ndim - 1)])
    rep_shapes = [w_loc[n].shape for n in REPLICATED]
    res = _adamw_call(_pack([w_loc[n] for n in REPLICATED], F32), _pack([m_loc[n] for n in REPLICATED], F32),
                      _pack([v_loc[n] for n in REPLICATED], F32), [g_rep], "adamw_replicated")
    for dst, buf in zip((grads, deltas, new_m, new_v), res):
        for n, a in zip(REPLICATED, _unpack(buf, rep_shapes)):
            dst[n] = a

    return (loss, dx[None], *[grads[n] for n in WEIGHTS], *[deltas[n] for n in WEIGHTS],
            *[new_m[n] for n in WEIGHTS], *[new_v[n] for n in WEIGHTS])
```

```python
import functools
import math

import jax
import jax.numpy as jnp
import numpy as np
from jax import lax
from jax.experimental import pallas as pl
from jax.experimental.pallas import tpu as pltpu

F32 = jnp.float32
BF16 = jnp.bfloat16
MXU_DTYPE = BF16
V7X_VMEM_BYTES = 64 * 1024 * 1024
VMEM_LIMIT = V7X_VMEM_BYTES * 3 // 4
LANES = 128
SUBLANES = 8

D_MODEL = 1024
DEPTH = 4
BRANCH_W = 256
CONV_W = 31
CONV_HALO = 32
MLA_SCALE = (64 + 32) ** -0.5
SWA_SCALE = 64 ** -0.5
WINDOW = 128
ROPE_THETA = 10000.0
SSM_GROUPS, SSM_GROUP, SSM_STATE = 16, 16, 64
SSM_CH = SSM_GROUPS * SSM_STATE
SCAN_SEGMENTS = SUBLANES
SCAN_CB = 128
DEEPNORM_ALPHA = (2.0 * DEPTH) ** 0.25
LN_EPS = 1e-5
RMS_EPS = 1e-6
ADAM_LR, ADAM_B1, ADAM_B2, ADAM_EPS, ADAM_WD, ADAM_STEP = 0.001, 0.9, 0.999, 1e-08, 0.01, 10
NEG = -1e30
ROW_TILE = 512

NN = (((1,), (0,)), ((), ()))
NT = (((1,), (1,)), ((), ()))
TN = (((0,), (0,)), ((), ()))

MESH = pl.DeviceIdType.MESH
ANY = pl.BlockSpec(memory_space=pl.ANY)


def _dot(a, b, dims):
    return lax.dot_general(a.astype(MXU_DTYPE), b.astype(MXU_DTYPE), dims, preferred_element_type=F32)


def _pick(n, cands):
    for c in cands:
        if n % c == 0:
            return c
    return n


def _params(sem):
    return pltpu.CompilerParams(dimension_semantics=sem, vmem_limit_bytes=VMEM_LIMIT)


def _col_offsets(widths):
    return [sum(widths[:j]) for j in range(len(widths))]


def _proj_fwd_call(x, wb, widths, name):
    t, k = x.shape
    tm = min(ROW_TILE, t)
    offs = _col_offsets(widths)

    def body(x_ref, w_ref, *o_refs):
        xb = x_ref[...].astype(MXU_DTYPE)
        for o_ref, off, wd in zip(o_refs, offs, widths):
            o_ref[...] = _dot(xb, w_ref[:, off:off + wd], NN)

    return pl.pallas_call(
        body, name=name, grid=(t // tm,),
        in_specs=[pl.BlockSpec((tm, k), lambda i: (i, 0)), pl.BlockSpec(wb.shape, lambda i: (0, 0))],
        out_specs=[pl.BlockSpec((tm, wd), lambda i: (i, 0)) for wd in widths],
        out_shape=[jax.ShapeDtypeStruct((t, wd), F32) for wd in widths],
        compiler_params=_params(("parallel",)),
    )(x, wb)


def _proj_dx_call(douts, wb, widths, name):
    t = douts[0].shape[0]
    k = wb.shape[0]
    tm = min(ROW_TILE, t)
    offs = _col_offsets(widths)

    def body(*refs):
        d_refs, w_ref, o_ref = refs[:-2], refs[-2], refs[-1]
        acc = jnp.zeros((tm, k), F32)
        for d_ref, off, wd in zip(d_refs, offs, widths):
            acc = acc + _dot(d_ref[...], w_ref[:, off:off + wd], NT)
        o_ref[...] = acc

    return pl.pallas_call(
        body, name=name, grid=(t // tm,),
        in_specs=[pl.BlockSpec((tm, wd), lambda i: (i, 0)) for wd in widths] + [pl.BlockSpec(wb.shape, lambda i: (0, 0))],
        out_specs=pl.BlockSpec((tm, k), lambda i: (i, 0)), out_shape=jax.ShapeDtypeStruct((t, k), F32),
        compiler_params=_params(("parallel",)),
    )(*douts, wb)


def _proj_dw_call(x, douts, widths, name, out_dtype):
    t, k = x.shape
    n = sum(widths)
    tk = min(ROW_TILE if k * n <= 2 * 1024 * 1024 else ROW_TILE // 2, t)
    nk = t // tk
    offs = _col_offsets(widths)
    n = sum(widths)

    def body(x_ref, *refs):
        d_refs, o_ref, acc_ref = refs[:-2], refs[-2], refs[-1]

        @pl.when(pl.program_id(0) == 0)
        def _():
            acc_ref[...] = jnp.zeros_like(acc_ref)

        xb = x_ref[...].astype(MXU_DTYPE)
        for d_ref, off, wd in zip(d_refs, offs, widths):
            acc_ref[:, off:off + wd] += _dot(xb, d_ref[...], TN)

        @pl.when(pl.program_id(0) == nk - 1)
        def _():
            o_ref[...] = acc_ref[...].astype(out_dtype)

    return pl.pallas_call(
        body, name=name, grid=(nk,),
        in_specs=[pl.BlockSpec((tk, k), lambda i: (i, 0))] + [pl.BlockSpec((tk, wd), lambda i: (i, 0)) for wd in widths],
        out_specs=pl.BlockSpec((k, n), lambda i: (0, 0)), out_shape=jax.ShapeDtypeStruct((k, n), out_dtype),
        scratch_shapes=[pltpu.VMEM((k, n), F32)],
        compiler_params=_params(("arbitrary",)),
    )(x, *douts)


def make_proj(widths, name):
    @jax.custom_vjp
    def op(x, w):
        return tuple(_proj_fwd_call(x, w.astype(MXU_DTYPE), widths, name + "_fwd"))

    def fwd(x, w):
        wb = w.astype(MXU_DTYPE)
        return tuple(_proj_fwd_call(x, wb, widths, name + "_fwd")), (x, wb, jnp.zeros((0,), w.dtype))

    def bwd(res, douts):
        x, wb, w_like = res
        return _proj_dx_call(douts, wb, widths, name + "_dx"), _proj_dw_call(x, douts, widths, name + "_dw", w_like.dtype)

    op.defvjp(fwd, bwd)
    return op


_MM_OPS = {}


def op_mm(a, w):
    n = w.shape[1]
    if n not in _MM_OPS:
        _MM_OPS[n] = make_proj((n,), "mm%d" % n)
    return _MM_OPS[n](a, w)[0]


@jax.custom_vjp
def _mm(a, w):
    return _dot(a, w, NN)


def _mm_f(a, w):
    return _dot(a, w, NN), (a, w)


def _mm_b(res, g):
    a, w = res
    return _dot(g, w, NT), _dot(a, g, TN)


_mm.defvjp(_mm_f, _mm_b)


@functools.partial(jax.custom_vjp, nondiff_argnums=(1,))
def _roll(x, shift):
    return pltpu.roll(x, shift, 1)


def _roll_f(x, shift):
    return pltpu.roll(x, shift, 1), None


def _roll_b(shift, _, g):
    return (pltpu.roll(g, (g.shape[1] - shift) % g.shape[1], 1),)


_roll.defvjp(_roll_f, _roll_b)


def _ln(x, g, b):
    mu = jnp.mean(x, axis=-1, keepdims=True)
    xc = x - mu
    var = jnp.mean(xc * xc, axis=-1, keepdims=True)
    return xc * lax.rsqrt(var + LN_EPS) * g + b


def _rms(x, g):
    ms = jnp.mean(x * x, axis=-1, keepdims=True)
    return x * lax.rsqrt(ms + RMS_EPS) * g


def _sigmoid(x):
    return jax.nn.sigmoid(x)


def _silu(x):
    return x * _sigmoid(x)


def _gelu_tanh(x):
    return x * (0.5 * (1.0 + jnp.tanh(math.sqrt(2.0 / math.pi) * (x + 0.044715 * (x * x * x)))))


def _rowwise_fwd_call(fn, rows, consts, name, tile):
    t = rows[0].shape[0]
    tile = min(tile, t)
    nr = len(rows)
    outs = jax.eval_shape(fn, *[jax.ShapeDtypeStruct((tile, r.shape[1]), F32) for r in rows],
                          *[jax.ShapeDtypeStruct(c.shape, F32) for c in consts])

    def body(*refs):
        vals = [r[...] for r in refs[:nr + len(consts)]]
        res = fn(*vals)
        for o_ref, o in zip(refs[nr + len(consts):], res):
            o_ref[...] = o

    return pl.pallas_call(
        body, name=name, grid=(t // tile,),
        in_specs=[pl.BlockSpec((tile, r.shape[1]), lambda i: (i, 0)) for r in rows]
        + [pl.BlockSpec(c.shape, lambda i: (0, 0)) for c in consts],
        out_specs=[pl.BlockSpec((tile, o.shape[1]), lambda i: (i, 0)) for o in outs],
        out_shape=[jax.ShapeDtypeStruct((t, o.shape[1]), F32) for o in outs],
        compiler_params=_params(("parallel",)),
    )(*rows, *consts)


def _rowwise_bwd_call(fn, rows, consts, douts, row_diff, name, tile):
    t = rows[0].shape[0]
    tile = min(tile, t)
    nr, nc, nd = len(rows), len(consts), len(douts)
    diff_idx = [i for i in range(nr) if row_diff[i]]

    def body(*refs):
        rv = [r[...] for r in refs[:nr]]
        cv = [r[...] for r in refs[nr:nr + nc]]
        dv = [r[...] for r in refs[nr + nc:nr + nc + nd]]
        out_refs = refs[nr + nc + nd:]

        def f(*diff):
            full = list(rv)
            for k, i in enumerate(diff_idx):
                full[i] = diff[k]
            return fn(*full, *diff[len(diff_idx):])

        _, vjp = jax.vjp(f, *[rv[i] for i in diff_idx], *cv)
        grads = vjp(tuple(dv))
        for k in range(len(diff_idx)):
            out_refs[k][...] = grads[k]
        first = pl.program_id(0) == 0
        for k in range(nc):
            acc_ref = out_refs[len(diff_idx) + k]
            g = grads[len(diff_idx) + k]

            @pl.when(first)
            def _(acc_ref=acc_ref, g=g):
                acc_ref[...] = g

            @pl.when(jnp.logical_not(first))
            def _(acc_ref=acc_ref, g=g):
                acc_ref[...] += g

    res = pl.pallas_call(
        body, name=name, grid=(t // tile,),
        in_specs=[pl.BlockSpec((tile, r.shape[1]), lambda i: (i, 0)) for r in rows]
        + [pl.BlockSpec(c.shape, lambda i: (0, 0)) for c in consts]
        + [pl.BlockSpec((tile, d.shape[1]), lambda i: (i, 0)) for d in douts],
        out_specs=[pl.BlockSpec((tile, rows[i].shape[1]), lambda i_: (i_, 0)) for i in diff_idx]
        + [pl.BlockSpec(c.shape, lambda i: (0, 0)) for c in consts],
        out_shape=[jax.ShapeDtypeStruct(rows[i].shape, F32) for i in diff_idx]
        + [jax.ShapeDtypeStruct(c.shape, F32) for c in consts],
        compiler_params=_params(("arbitrary",)),
    )(*rows, *consts, *douts)
    return res[:len(diff_idx)], res[len(diff_idx):]


def make_rowwise(fn, name, row_diff, tile=ROW_TILE):
    @jax.custom_vjp
    def op(rows, consts):
        return tuple(_rowwise_fwd_call(fn, rows, consts, name + "_fwd", tile))

    def fwd(rows, consts):
        return op(rows, consts), (rows, consts)

    def bwd(res, douts):
        rows, consts = res
        drows, dconsts = _rowwise_bwd_call(fn, rows, consts, douts, row_diff, name + "_bwd", tile)
        it = iter(drows)
        full = tuple(next(it) if row_diff[i] else jnp.zeros_like(rows[i]) for i in range(len(rows)))
        return full, tuple(dconsts)

    op.defvjp(fwd, bwd)
    return op


def _conv_post_fn(cv, a_z, ng, nb, w_pw2):
    return (_mm(_silu(_ln(cv, ng, nb)), w_pw2) * _silu(a_z),)


def _mla_prep_fn(c_q, c_kv, krblk, cos4, sin4, qg, kvg, w_uq, w_uk, w_uv):
    qe = _mm(_rms(c_q, qg), w_uq)
    q = qe * cos4 + _roll(qe, qe.shape[1] - 32) * sin4
    cos1, sin1 = cos4[:, :LANES], sin4[:, :LANES]
    kr = krblk * cos1 + _roll(krblk, LANES - 32) * sin1
    kn = _rms(c_kv, kvg)
    k = _mm(kn, w_uk) + jnp.concatenate([kr, kr, kr, kr], axis=1)
    return q, k, _mm(kn, w_uv)


def _ssm_post_fn(y, u, c_z, d, w_a, w_b):
    y2 = _gelu_tanh(y + d * u)
    return (_mm(y2, w_a) * _sigmoid(_mm(y2, w_b)) * _silu(c_z),)


def _gate_fn(o, z):
    return (o * _silu(z),)


def _merge_fn(br0, br1, br2, br3, gl0, gl1, gl2, gl3, b0, b1, b2, b3):
    return (_sigmoid(gl0 + b0) * br0 + _sigmoid(gl1 + b1) * br1 + _sigmoid(gl2 + b2) * br2 + _sigmoid(gl3 + b3) * br3,)


def _ln_fn(x, mo, g, b):
    return (_ln(DEEPNORM_ALPHA * x + mo, g, b),)


def _ple_fn(x1, pe, gl, g):
    return (x1 + _rms(pe * _sigmoid(gl), g),)


op_conv_post = make_rowwise(_conv_post_fn, "conv_post", (True, True))
op_mla_prep = make_rowwise(_mla_prep_fn, "mla_prep", (True, True, True, False, False))
op_ssm_post = make_rowwise(_ssm_post_fn, "ssm_post", (True, True, True))
op_gate = make_rowwise(_gate_fn, "gate", (True, True))
op_merge = make_rowwise(_merge_fn, "merge", (True,) * 8, tile=ROW_TILE // 2)
op_ln = make_rowwise(_ln_fn, "post_ln", (True, True))
op_ple = make_rowwise(_ple_fn, "ple", (True, True, True))


def _conv_fwd_call(a_val, a_gate, w32, b):
    t, w = a_val.shape
    tile = min(ROW_TILE, t)
    per = tile // CONV_HALO
    cur = pl.BlockSpec((tile, w), lambda i: (i, 0))
    prev = pl.BlockSpec((CONV_HALO, w), lambda i: (jnp.maximum(i * per - 1, 0), 0))

    def body(av_ref, avh_ref, ag_ref, agh_ref, w_ref, b_ref, cv_ref, buf):
        i = pl.program_id(0)
        gh = avh_ref[...] * _sigmoid(agh_ref[...])
        buf[0:CONV_HALO, :] = jnp.where(i > 0, gh, 0.0)
        buf[CONV_HALO:, :] = av_ref[...] * _sigmoid(ag_ref[...])
        acc = jnp.zeros((tile, w), F32) + b_ref[...]
        for j in range(CONV_W):
            acc = acc + w_ref[j:j + 1, :] * buf[pl.ds(CONV_HALO - (CONV_W - 1) + j, tile), :]
        cv_ref[...] = acc

    return pl.pallas_call(
        body, name="conv_fwd", grid=(t // tile,),
        in_specs=[cur, prev, cur, prev, pl.BlockSpec((CONV_HALO, w), lambda i: (0, 0)), pl.BlockSpec((1, w), lambda i: (0, 0))],
        out_specs=cur, out_shape=jax.ShapeDtypeStruct((t, w), F32),
        scratch_shapes=[pltpu.VMEM((tile + CONV_HALO, w), F32)],
        compiler_params=_params(("parallel",)),
    )(a_val, a_val, a_gate, a_gate, w32, b)


def _conv_bwd_call(a_val, a_gate, w32, dcv):
    t, w = a_val.shape
    tile = min(ROW_TILE, t)
    n = t // tile
    per = tile // CONV_HALO
    cur = pl.BlockSpec((tile, w), lambda i: (i, 0))
    prev = pl.BlockSpec((CONV_HALO, w), lambda i: (jnp.maximum(i * per - 1, 0), 0))
    nxt = pl.BlockSpec((CONV_HALO, w), lambda i: (jnp.minimum((i + 1) * per, t // CONV_HALO - 1), 0))
    full = lambda r: pl.BlockSpec((r, w), lambda i: (0, 0))

    def body(av_ref, avh_ref, ag_ref, agh_ref, w_ref, d_ref, dn_ref, dav_ref, dag_ref, dw_ref, db_ref, gbuf, dbuf):
        i = pl.program_id(0)
        gh = avh_ref[...] * _sigmoid(agh_ref[...])
        gbuf[0:CONV_HALO, :] = jnp.where(i > 0, gh, 0.0)
        av = av_ref[...]
        sg = _sigmoid(ag_ref[...])
        gbuf[CONV_HALO:, :] = av * sg
        d = d_ref[...]
        dbuf[0:tile, :] = d
        dbuf[tile:, :] = jnp.where(i < n - 1, dn_ref[...], 0.0)

        @pl.when(i == 0)
        def _():
            dw_ref[...] = jnp.zeros_like(dw_ref)
            db_ref[...] = jnp.zeros_like(db_ref)

        dg = jnp.zeros((tile, w), F32)
        for j in range(CONV_W):
            dg = dg + w_ref[j:j + 1, :] * dbuf[pl.ds(CONV_W - 1 - j, tile), :]
            dw_ref[j:j + 1, :] += jnp.sum(d * gbuf[pl.ds(CONV_HALO - (CONV_W - 1) + j, tile), :], axis=0, keepdims=True)
        db_ref[...] += jnp.sum(d, axis=0, keepdims=True)
        dav_ref[...] = dg * sg
        dag_ref[...] = dg * av * sg * (1.0 - sg)

    return pl.pallas_call(
        body, name="conv_bwd", grid=(n,),
        in_specs=[cur, prev, cur, prev, full(CONV_HALO), cur, nxt],
        out_specs=[cur, cur, full(CONV_HALO), full(1)],
        out_shape=[jax.ShapeDtypeStruct((t, w), F32), jax.ShapeDtypeStruct((t, w), F32),
                   jax.ShapeDtypeStruct((CONV_HALO, w), F32), jax.ShapeDtypeStruct((1, w), F32)],
        scratch_shapes=[pltpu.VMEM((tile + CONV_HALO, w), F32), pltpu.VMEM((tile + CONV_HALO, w), F32)],
        compiler_params=_params(("arbitrary",)),
    )(a_val, a_val, a_gate, a_gate, w32, dcv, dcv)


def _pad_taps(conv_w):
    return jnp.concatenate([conv_w, jnp.zeros((CONV_HALO - CONV_W, conv_w.shape[1]), F32)], axis=0)


@jax.custom_vjp
def op_conv(a_val, a_gate, conv_w, conv_b):
    return _conv_fwd_call(a_val, a_gate, _pad_taps(conv_w), conv_b)


def _op_conv_fwd(a_val, a_gate, conv_w, conv_b):
    return op_conv(a_val, a_gate, conv_w, conv_b), (a_val, a_gate, conv_w)


def _op_conv_bwd(res, dcv):
    a_val, a_gate, conv_w = res
    dav, dag, dw, db = _conv_bwd_call(a_val, a_gate, _pad_taps(conv_w), dcv)
    return dav, dag, dw[:CONV_W], db


op_conv.defvjp(_op_conv_fwd, _op_conv_bwd)


def _head_masks(rows):
    lane = lax.broadcasted_iota(jnp.int32, (rows, LANES), 1)
    return lane < 64, lane >= 64


def _head_row(vals, mask):
    return jnp.max(jnp.where(mask, vals, NEG), axis=1, keepdims=True)


def _attn_valid(qpos, kpos, window):
    valid = kpos <= qpos
    if window is not None:
        valid = jnp.logical_and(valid, qpos - kpos < window)
    return valid


def _flash_fwd_call(q, k, v, sink, *, window, shared_k, scale, blk, blk_q, name):
    t = q.shape[0]
    qw = LANES if shared_k else 2 * LANES
    pairs = v.shape[1] // LANES
    tk = min(blk, t)
    tq = min(blk_q, t)
    has_sink = sink is not None

    def body(*refs):
        if has_sink:
            q_ref, k_ref, v_ref, s_ref, o_ref, lse_ref, k_mxu, v0_mxu, v1_mxu = refs
        else:
            q_ref, k_ref, v_ref, o_ref, lse_ref, k_mxu, v0_mxu, v1_mxu = refs
        v_mxu = (v0_mxu, v1_mxu)
        i = pl.program_id(1)

        @pl.when(i == 0)
        def _():
            full_masks = _head_masks(t)
            k_mxu[...] = k_ref[...].astype(MXU_DTYPE)
            for h in range(2):
                v_mxu[h][...] = jnp.where(full_masks[h], v_ref[...], 0.0).astype(MXU_DTYPE)

        qb = q_ref[...]
        masks = _head_masks(tq)
        row_masks = _head_masks(1)
        qh = [(jnp.where(masks[h], qb, 0.0) if shared_k else qb[:, h * LANES:(h + 1) * LANES]).astype(MXU_DTYPE) for h in range(2)]
        qpos = i * tq + lax.broadcasted_iota(jnp.int32, (tq, tk), 0)
        if has_sink:
            m_init = [jnp.zeros((tq, 1), F32) + _head_row(s_ref[...], row_masks[h]) for h in range(2)]
            l_init = [jnp.ones((tq, 1), F32)] * 2
        else:
            m_init = [jnp.full((tq, 1), NEG, F32)] * 2
            l_init = [jnp.zeros((tq, 1), F32)] * 2

        def make_step(masked):
            def step(j, carry):
                m0, l0, m1, l1, acc = carry
                start = pl.multiple_of(j * tk, tk)
                kb = k_mxu[pl.ds(start, tk), :]
                if masked:
                    valid = _attn_valid(qpos, j * tk + lax.broadcasted_iota(jnp.int32, (tq, tk), 1), window)
                new, alphas, pv = [], [], []
                for h, (m, l) in enumerate(((m0, l0), (m1, l1))):
                    kh = kb if shared_k else kb[:, h * LANES:(h + 1) * LANES]
                    s = _dot(qh[h], kh, NT) * scale
                    if masked:
                        s = jnp.where(valid, s, NEG)
                    m_new = jnp.maximum(m, jnp.max(s, axis=1, keepdims=True))
                    alpha = jnp.exp(m - m_new)
                    p = jnp.exp(s - m_new)
                    new += [m_new, alpha * l + jnp.sum(p, axis=1, keepdims=True)]
                    alphas.append(alpha)
                    pv.append(_dot(p, v_mxu[h][pl.ds(start, tk), :], NN))
                acc = acc * jnp.where(masks[0], alphas[0], alphas[1]) + pv[0] + pv[1]
                return new[0], new[1], new[2], new[3], acc
            return step

        carry = (m_init[0], l_init[0], m_init[1], l_init[1], jnp.zeros((tq, LANES), F32))
        last = (i * tq + tq - 1) // tk
        if window is None:
            n_full = (i * tq + 1) // tk
            carry = lax.fori_loop(0, n_full, make_step(False), carry)
            carry = lax.fori_loop(n_full, last + 1, make_step(True), carry)
        else:
            carry = lax.fori_loop(jnp.maximum(i * tq - (window - 1), 0) // tk, last + 1, make_step(True), carry)
        m0, l0, m1, l1, acc = carry
        o_ref[...] = acc / jnp.where(masks[0], l0, l1)
        lse_ref[...] = jnp.where(masks[0], m0 + jnp.log(l0), m1 + jnp.log(l1))

    in_specs = [pl.BlockSpec((tq, qw), lambda p, i: (i, p)), pl.BlockSpec((t, qw), lambda p, i: (0, p)),
                pl.BlockSpec((t, LANES), lambda p, i: (0, p))]
    args = [q, k, v]
    if has_sink:
        in_specs.append(pl.BlockSpec((1, LANES), lambda p, i: (0, p)))
        args.append(sink)
    blk_o = pl.BlockSpec((tq, LANES), lambda p, i: (i, p))
    return pl.pallas_call(
        body, name=name, grid=(pairs, t // tq), in_specs=in_specs, out_specs=[blk_o, blk_o],
        out_shape=[jax.ShapeDtypeStruct((t, pairs * LANES), F32)] * 2,
        scratch_shapes=[pltpu.VMEM((t, qw), MXU_DTYPE), pltpu.VMEM((t, LANES), MXU_DTYPE), pltpu.VMEM((t, LANES), MXU_DTYPE)],
        compiler_params=_params(("arbitrary", "arbitrary")),
    )(*args)


def _flash_bwd_call(q, k, v, sink, o, lse, do, *, window, shared_k, scale, blk, blk_q, name):
    t = q.shape[0]
    qw = LANES if shared_k else 2 * LANES
    pairs = v.shape[1] // LANES
    tk = min(blk, t)
    tq = min(blk_q, t)
    assert tk % tq == 0 or tq % tk == 0
    nq = t // tq
    has_sink = sink is not None

    def body(*refs):
        if has_sink:
            q_ref, k_ref, v_ref, o_ref, lse_ref, do_ref, s_ref, dq_ref, dk_ref, dv_ref, ds_ref = refs[:11]
        else:
            q_ref, k_ref, v_ref, o_ref, lse_ref, do_ref, dq_ref, dk_ref, dv_ref = refs[:9]
        q_mxu, do_mxu, lse_h, dsum_h = refs[-8:-6], refs[-6:-4], refs[-4:-2], refs[-2:]
        j = pl.program_id(1)
        masks = _head_masks(tq)
        row_masks = _head_masks(1)

        @pl.when(j == 0)
        def _():
            dq_ref[...] = jnp.zeros_like(dq_ref)
            full_masks = _head_masks(t)
            prod = do_ref[...] * o_ref[...]
            parts = []
            for h in range(2):
                qh = jnp.where(full_masks[h], q_ref[...], 0.0) if shared_k else q_ref[:, h * LANES:(h + 1) * LANES]
                q_mxu[h][...] = qh.astype(MXU_DTYPE)
                do_mxu[h][...] = jnp.where(full_masks[h], do_ref[...], 0.0).astype(MXU_DTYPE)
                dsum = jnp.sum(jnp.where(full_masks[h], prod, 0.0), axis=1, keepdims=True)
                lse = _head_row(lse_ref[...], full_masks[h])
                dsum_h[h][...] = jnp.zeros((t, LANES), F32) + dsum
                lse_h[h][...] = jnp.zeros((t, LANES), F32) + lse
                if has_sink:
                    ps = jnp.exp(_head_row(s_ref[...], row_masks[h]) - lse)
                    parts.append(-jnp.sum(ps * dsum, axis=0, keepdims=True))
            if has_sink:
                ds_ref[...] = jnp.zeros((SUBLANES, LANES), F32) + jnp.where(row_masks[0], parts[0], parts[1])

        kb = k_ref[...].astype(MXU_DTYPE)
        vb = v_ref[...].astype(MXU_DTYPE)
        kh = [kb if shared_k else kb[:, h * LANES:(h + 1) * LANES] for h in range(2)]
        kpos = j * tk + lax.broadcasted_iota(jnp.int32, (tq, tk), 1)
        lanes_of = lambda a: a if tk == LANES else jnp.concatenate([a] * (tk // LANES), axis=1)

        def make_step(masked):
            def step(i, carry):
                dk0, dk1, dv = carry
                start = pl.multiple_of(i * tq, tq)
                if masked:
                    valid = _attn_valid(i * tq + lax.broadcasted_iota(jnp.int32, (tq, tk), 0), kpos, window)
                dks, dqs = [], []
                for h in range(2):
                    qh = q_mxu[h][pl.ds(start, tq), :]
                    doh = do_mxu[h][pl.ds(start, tq), :]
                    s = _dot(qh, kh[h], NT) * scale
                    if masked:
                        s = jnp.where(valid, s, NEG)
                    p = jnp.exp(s - lanes_of(lse_h[h][pl.ds(start, tq), :]))
                    dp = _dot(doh, vb, NT)
                    dsc = p * (dp - lanes_of(dsum_h[h][pl.ds(start, tq), :])) * scale
                    dv = dv + _dot(p, doh, TN)
                    dks.append(_dot(dsc, qh, TN))
                    dq_h = _dot(dsc, kh[h], NN)
                    dqs.append(jnp.where(masks[h], dq_h, 0.0) if shared_k else dq_h)
                if shared_k:
                    dq_ref[pl.ds(start, tq), :] += dqs[0] + dqs[1]
                else:
                    dq_ref[pl.ds(start, tq), :] += jnp.concatenate(dqs, axis=1)
                return dk0 + dks[0], dk1 + dks[1], dv
            return step

        zero = jnp.zeros((tk, LANES), F32)
        carry = (zero, zero, zero)
        first = (j * tk) // tq
        if window is None:
            n_full = jnp.minimum(((j + 1) * tk + tq - 2) // tq, nq)
            carry = lax.fori_loop(first, n_full, make_step(True), carry)
            carry = lax.fori_loop(n_full, nq, make_step(False), carry)
        else:
            carry = lax.fori_loop(first, jnp.minimum(nq, (j * tk + tk - 1 + window - 1) // tq + 1), make_step(True), carry)
        dk0, dk1, dv = carry
        dk_ref[...] = dk0 + dk1 if shared_k else jnp.concatenate([dk0, dk1], axis=1)
        dv_ref[...] = dv

    full = lambda w: pl.BlockSpec((t, w), lambda p, j: (0, p))
    blkspec = lambda w: pl.BlockSpec((tk, w), lambda p, j: (j, p))
    in_specs = [full(qw), blkspec(qw), blkspec(LANES), full(LANES), full(LANES), full(LANES)]
    args = [q, k, v, o, lse, do]
    out_specs = [full(qw), blkspec(qw), blkspec(LANES)]
    out_shape = [jax.ShapeDtypeStruct(q.shape, F32), jax.ShapeDtypeStruct(k.shape, F32), jax.ShapeDtypeStruct(v.shape, F32)]
    if has_sink:
        in_specs.append(pl.BlockSpec((1, LANES), lambda p, j: (0, p)))
        args.append(sink)
        out_specs.append(pl.BlockSpec((SUBLANES, LANES), lambda p, j: (0, p)))
        out_shape.append(jax.ShapeDtypeStruct((SUBLANES, pairs * LANES), F32))
    return pl.pallas_call(
        body, name=name, grid=(pairs, t // tk), in_specs=in_specs, out_specs=out_specs, out_shape=out_shape,
        scratch_shapes=[pltpu.VMEM((t, LANES), MXU_DTYPE)] * 4 + [pltpu.VMEM((t, LANES), F32)] * 4,
        compiler_params=_params(("arbitrary", "arbitrary")),
    )(*args)


_MLA_CFG = dict(window=None, shared_k=False, scale=MLA_SCALE, blk=256)
_SWA_CFG = dict(window=WINDOW, shared_k=True, scale=SWA_SCALE, blk=128)
_MLA_FWD_CFG = dict(_MLA_CFG, blk=512, blk_q=256)
_SWA_FWD_CFG = dict(_SWA_CFG, blk=128, blk_q=128)
_MLA_BWD_CFG = dict(_MLA_CFG, blk=256, blk_q=512)
_SWA_BWD_CFG = dict(_SWA_CFG, blk=128, blk_q=128)


@jax.custom_vjp
def op_mla_attn(q, k, v):
    return _flash_fwd_call(q, k, v, None, name="mla_fwd", **_MLA_FWD_CFG)[0]


def _op_mla_attn_fwd(q, k, v):
    o, lse = _flash_fwd_call(q, k, v, None, name="mla_fwd", **_MLA_FWD_CFG)
    return o, (q, k, v, o, lse)


def _op_mla_attn_bwd(res, do):
    q, k, v, o, lse = res
    return tuple(_flash_bwd_call(q, k, v, None, o, lse, do, name="mla_bwd", **_MLA_BWD_CFG))


op_mla_attn.defvjp(_op_mla_attn_fwd, _op_mla_attn_bwd)


@jax.custom_vjp
def op_swa_attn(q, k, v, sink):
    return _flash_fwd_call(q, k, v, sink, name="swa_fwd", **_SWA_FWD_CFG)[0]


def _op_swa_attn_fwd(q, k, v, sink):
    o, lse = _flash_fwd_call(q, k, v, sink, name="swa_fwd", **_SWA_FWD_CFG)
    return o, (q, k, v, sink, o, lse)


def _op_swa_attn_bwd(res, do):
    q, k, v, sink, o, lse = res
    dq, dk, dv, dsink = _flash_bwd_call(q, k, v, sink, o, lse, do, name="swa_bwd", **_SWA_BWD_CFG)
    first_lane = lax.broadcasted_iota(jnp.int32, (1, dsink.shape[1]), 1) % 64 == 0
    return dq, dk, dv, jnp.where(first_lane, dsink[:1], 0.0)


op_swa_attn.defvjp(_op_swa_attn_fwd, _op_swa_attn_bwd)


def _complex_power(ar, ai, n):
    for _ in range(int(math.log2(n))):
        ar, ai = ar * ar - ai * ai, 2.0 * ar * ai
    return ar, ai


def _scan_passes(load_b, a1r, a1i, n, store, e_ref, c_ref, reverse):
    cb = a1r.shape[1]
    ar = jnp.zeros((SCAN_SEGMENTS, cb), F32) + a1r
    ai = jnp.zeros((SCAN_SEGMENTS, cb), F32) + a1i
    idx = (lambda ii: n - 1 - ii) if reverse else (lambda ii: ii)

    def local(ii, h):
        br, bi = load_b(idx(ii))
        return ar * h[0] - ai * h[1] + br, ar * h[1] + ai * h[0] + bi

    zero = jnp.zeros((SCAN_SEGMENTS, cb), F32)
    er, ei = lax.fori_loop(0, n, local, (zero, zero))
    e_ref[:, 0:cb] = er
    e_ref[:, cb:] = ei
    pr, pi_ = _complex_power(a1r, a1i, n)
    cr = jnp.zeros((1, cb), F32)
    ci = jnp.zeros((1, cb), F32)
    order = range(SCAN_SEGMENTS - 1, -1, -1) if reverse else range(SCAN_SEGMENTS)
    for s in order:
        c_ref[s:s + 1, 0:cb] = cr
        c_ref[s:s + 1, cb:] = ci
        er1, ei1 = e_ref[s:s + 1, 0:cb], e_ref[s:s + 1, cb:]
        cr, ci = pr * cr - pi_ * ci + er1, pr * ci + pi_ * cr + ei1

    def second(ii, h):
        i = idx(ii)
        hr, hi = local(ii, h)
        store(i, hr, hi)
        return hr, hi

    lax.fori_loop(0, n, second, (c_ref[:, 0:cb], c_ref[:, cb:]))


def _scan_fwd_call(bu, lam):
    n = bu.shape[0]
    cb = SCAN_CB
    blk3 = pl.BlockSpec((n, SCAN_SEGMENTS, 2 * cb), lambda c: (0, 0, c))
    blk2 = lambda r: pl.BlockSpec((r, 2 * cb), lambda c: (0, c))

    def body(b_ref, lam_ref, h_ref, cin_ref, e_ref):
        def store(i, hr, hi):
            h_ref[i, :, 0:cb] = hr
            h_ref[i, :, cb:] = hi

        _scan_passes(lambda i: (b_ref[i, :, 0:cb], b_ref[i, :, cb:]), lam_ref[:, 0:cb], lam_ref[:, cb:], n, store,
                     e_ref, cin_ref, False)

    return pl.pallas_call(
        body, name="scan_fwd", grid=(SSM_CH // cb,), in_specs=[blk3, blk2(1)], out_specs=[blk3, blk2(SCAN_SEGMENTS)],
        out_shape=[jax.ShapeDtypeStruct(bu.shape, F32), jax.ShapeDtypeStruct((SCAN_SEGMENTS, 2 * SSM_CH), F32)],
        scratch_shapes=[pltpu.VMEM((SCAN_SEGMENTS, 2 * cb), F32)],
        compiler_params=_params(("parallel",)),
    )(bu, lam)


def _scan_bwd_call(dh, h, cin, lam):
    n = dh.shape[0]
    cb = SCAN_CB
    blk3 = pl.BlockSpec((n, SCAN_SEGMENTS, 2 * cb), lambda c: (0, 0, c))
    blk2 = lambda r: pl.BlockSpec((r, 2 * cb), lambda c: (0, c))

    def body(d_ref, h_ref, cin_ref, lam_ref, g_ref, dlam_ref, e_ref, c_ref, acc_ref):
        acc_ref[...] = jnp.zeros_like(acc_ref)

        def store(i, gr, gi):
            g_ref[i, :, 0:cb] = gr
            g_ref[i, :, cb:] = gi
            ip = jnp.maximum(i - 1, 0)
            hpr = jnp.where(i > 0, h_ref[ip, :, 0:cb], cin_ref[:, 0:cb])
            hpi = jnp.where(i > 0, h_ref[ip, :, cb:], cin_ref[:, cb:])
            acc_ref[:, 0:cb] += gr * hpr + gi * hpi
            acc_ref[:, cb:] += gi * hpr - gr * hpi

        _scan_passes(lambda i: (d_ref[i, :, 0:cb], d_ref[i, :, cb:]), lam_ref[:, 0:cb], -lam_ref[:, cb:], n, store,
                     e_ref, c_ref, True)
        dlam_ref[...] = acc_ref[...]

    return pl.pallas_call(
        body, name="scan_bwd", grid=(SSM_CH // cb,), in_specs=[blk3, blk3, blk2(SCAN_SEGMENTS), blk2(1)],
        out_specs=[blk3, blk2(SCAN_SEGMENTS)],
        out_shape=[jax.ShapeDtypeStruct(dh.shape, F32), jax.ShapeDtypeStruct((SCAN_SEGMENTS, 2 * SSM_CH), F32)],
        scratch_shapes=[pltpu.VMEM((SCAN_SEGMENTS, 2 * cb), F32)] * 3,
        compiler_params=_params(("parallel",)),
    )(dh, h, cin, lam)


@jax.custom_vjp
def op_scan(bu, lam):
    return _scan_fwd_call(bu, lam)[0]


def _op_scan_fwd(bu, lam):
    h, cin = _scan_fwd_call(bu, lam)
    return h, (h, cin, lam)


def _op_scan_bwd(res, dh):
    h, cin, lam = res
    g, dlam = _scan_bwd_call(dh, h, cin, lam)
    return g, jnp.sum(dlam, axis=0, keepdims=True)


op_scan.defvjp(_op_scan_fwd, _op_scan_bwd)


def _loss_call(y, target):
    t, d = y.shape
    tile = min(ROW_TILE, t)

    def body(y_ref, t_ref, dy_ref, acc_ref):
        @pl.when(pl.program_id(0) == 0)
        def _():
            acc_ref[...] = jnp.zeros_like(acc_ref)

        err = y_ref[...] - t_ref[...]
        dy_ref[...] = err * (1.0 / d)
        col = jnp.sum(err * err, axis=0, keepdims=True)
        part = col[:, 0:LANES]
        for c in range(1, d // LANES):
            part = part + col[:, c * LANES:(c + 1) * LANES]
        acc_ref[0:1, :] += part

    blk = pl.BlockSpec((tile, d), lambda i: (i, 0))
    dy, acc = pl.pallas_call(
        body, name="loss_head", grid=(t // tile,), in_specs=[blk, blk],
        out_specs=[blk, pl.BlockSpec((SUBLANES, LANES), lambda i: (0, 0))],
        out_shape=[jax.ShapeDtypeStruct((t, d), F32), jax.ShapeDtypeStruct((SUBLANES, LANES), F32)],
        compiler_params=_params(("arbitrary",)),
    )(y, target)
    return jnp.sum(acc) * (0.5 / d), dy


def _rot_cols(w, xp=jnp):
    return xp.concatenate([-w[:, 16:], w[:, :16]], axis=1)


def _ext_w_in(w, xp=jnp):
    a_val, a_gate, a_z, c_q, c_kv, k_r, b_z, u, c_z, q, k, v, d_z = xp.split(
        w, (256, 512, 768, 1024, 1152, 1184, 1440, 1696, 1952, 2208, 2336, 2464), axis=1)
    dup = lambda m: xp.concatenate([m[:, :64], m[:, :64], m[:, 64:], m[:, 64:]], axis=1)
    krblk = xp.concatenate([xp.zeros((w.shape[0], 64), w.dtype), k_r, _rot_cols(k_r, xp)], axis=1)
    return xp.concatenate([a_val, a_gate, a_z, c_q, b_z, u, c_z, q, dup(k), dup(v), d_z, c_kv, krblk], axis=1)


IN_WIDTH = 2720
IN_SHARD = IN_WIDTH // 4
IN_SHARD_PAD = 768
IN_EXT = 3072


BAND = 256


def _w_in_layout():
    src = _ext_w_in(np.arange(1, IN_WIDTH + 1, dtype=np.float32)[None, :], np)[0]
    col = np.abs(src).astype(np.int64) - 1
    row = np.where(col >= 0, (col // IN_SHARD) * IN_SHARD_PAD + col % IN_SHARD, -1)
    return row, np.sign(src)


def _w_in_layout_matrix():
    row, sign = _w_in_layout()
    rows = lax.broadcasted_iota(jnp.int32, (4 * IN_SHARD_PAD, IN_EXT), 0)
    return jnp.where(rows == jnp.asarray(row, jnp.int32)[None, :], jnp.asarray(sign, F32)[None, :], 0.0).astype(MXU_DTYPE)


def _band_tables():
    row, _ = _w_in_layout()
    nb = IN_EXT // BAND
    hit = np.zeros((nb, nb), bool)
    for c, r in enumerate(row):
        if r >= 0:
            hit[r // BAND, c // BAND] = True

    def table(h):
        depth = int(h.sum(axis=1).max())
        rows = []
        for o in range(nb):
            used = [int(b) for b in np.nonzero(h[o])[0]]
            spare = [b for b in range(nb) if not h[o, b]]
            rows.append(used + spare[:depth - len(used)])
        return np.asarray(rows, np.int32), depth

    return table(hit.T), table(hit)


def _band_mm_call(a, e, table, depth, e_transposed, name, out_dtype):
    m = a.shape[0]
    nb = IN_EXT // BAND
    dims = NT if e_transposed else NN

    def body(t_ref, a_ref, e_ref, o_ref, acc_ref):
        kk = pl.program_id(1)

        @pl.when(kk == 0)
        def _():
            acc_ref[...] = jnp.zeros_like(acc_ref)

        acc_ref[...] += _dot(a_ref[...], e_ref[...], dims)

        @pl.when(kk == depth - 1)
        def _():
            o_ref[...] = acc_ref[...].astype(out_dtype)

    blk = lambda o, kk, t: t[o * depth + kk]
    e_spec = pl.BlockSpec((BAND, BAND), (lambda o, kk, t: (o, blk(o, kk, t))) if e_transposed else (lambda o, kk, t: (blk(o, kk, t), o)))
    return pl.pallas_call(
        body, name=name, out_shape=jax.ShapeDtypeStruct((m, IN_EXT), out_dtype),
        grid_spec=pltpu.PrefetchScalarGridSpec(
            num_scalar_prefetch=1, grid=(nb, depth),
            in_specs=[pl.BlockSpec((m, BAND), lambda o, kk, t: (0, blk(o, kk, t))), e_spec],
            out_specs=pl.BlockSpec((m, BAND), lambda o, kk, t: (0, o)),
            scratch_shapes=[pltpu.VMEM((m, BAND), F32)]),
        compiler_params=_params(("parallel", "arbitrary")),
    )(jnp.asarray(table.reshape(-1)), a, e)


@jax.custom_vjp
def op_w_in_ext(w_pad, e):
    (table, depth), _ = _band_tables()
    return _band_mm_call(w_pad, e, table, depth, False, "w_in_ext", F32)


def _op_w_in_ext_fwd(w_pad, e):
    return op_w_in_ext(w_pad, e), (e, jnp.zeros((0,), w_pad.dtype))


def _op_w_in_ext_bwd(res, g):
    e, w_like = res
    _, (table, depth) = _band_tables()
    return _band_mm_call(g, e, table, depth, True, "w_in_ext_bwd", w_like.dtype), jnp.zeros_like(e)


op_w_in_ext.defvjp(_op_w_in_ext_fwd, _op_w_in_ext_bwd)


H_COLS = dict(a_val=256, a_gate=256, a_z=256, c_q=256, b_z=256, u=256, c_z=256, q=256, kdup=256, vdup=256, d_z=256,
              c_kv=128, krblk=128)
op_in_proj = make_proj(tuple(H_COLS.values()), "in_proj")
op_merge_proj = make_proj((D_MODEL,) * 4, "merge_proj")


def _ext_mla(w_uq, w_ukv):
    zeros = jnp.zeros((w_ukv.shape[0], 64), w_ukv.dtype)
    uq, uk, uv = [], [], []
    for h in range(4):
        nope, rp = w_uq[:, 96 * h:96 * h + 64], w_uq[:, 96 * h + 64:96 * h + 96]
        uq += [nope, rp, _rot_cols(rp)]
        uk += [w_ukv[:, 128 * h:128 * h + 64], zeros]
        uv.append(w_ukv[:, 128 * h + 64:128 * h + 128])
    return jnp.concatenate(uq, axis=1), jnp.concatenate(uk, axis=1), jnp.concatenate(uv, axis=1)


def _scan_cols(re, im):
    parts = []
    for c in range(SSM_CH // SCAN_CB):
        parts += [re[..., c * SCAN_CB:(c + 1) * SCAN_CB], im[..., c * SCAN_CB:(c + 1) * SCAN_CB]]
    return jnp.concatenate(parts, axis=-1)


def _ext_ssm(a_re, a_im, log_dt, b_re, b_im, c_re, c_im):
    dt = jnp.exp(log_dt)[:, None]
    mag = jnp.exp(a_re * dt)
    lb_re, lb_im = mag * jnp.cos(a_im * dt), mag * jnp.sin(a_im * dt)
    den = a_re * a_re + a_im * a_im
    nr, ni = lb_re - 1.0, lb_im
    f_re = ((nr * a_re + ni * a_im) / den)[..., None]
    f_im = ((ni * a_re - nr * a_im) / den)[..., None]
    bb_re = f_re * b_re - f_im * b_im
    bb_im = f_re * b_im + f_im * b_re
    eye = jnp.eye(SSM_GROUPS, dtype=F32)
    spread = lambda a: a.transpose(0, 2, 1)[:, :, None, :] * eye[:, None, :, None]
    bd_in = lambda bb: spread(bb).reshape(SSM_GROUPS * SSM_GROUP, SSM_CH)
    bd_out = lambda cc: spread(cc).reshape(SSM_CH, SSM_GROUPS * SSM_GROUP)
    w_bu = _scan_cols(bd_in(bb_re), bd_in(bb_im))
    w_y = _scan_cols(bd_out(c_re).T, -bd_out(c_im).T).T
    lam = _scan_cols(lb_re.reshape(1, SSM_CH), lb_im.reshape(1, SSM_CH))
    return w_bu, w_y, lam


def _rope_tables(t):
    pos = jnp.arange(t, dtype=F32)
    inv_freq = ROPE_THETA ** (-jnp.arange(0, 32, 2, dtype=F32) / 32)
    ang = pos[:, None] * inv_freq[None, :]
    cos, sin = jnp.cos(ang), jnp.sin(ang)
    ones, z32, z64 = jnp.ones((t, 64), F32), jnp.zeros((t, 32), F32), jnp.zeros((t, 64), F32)
    cos1 = jnp.concatenate([ones, cos, cos, z32], axis=1)
    sin1 = jnp.concatenate([z64, sin, sin, z32], axis=1)
    return jnp.concatenate([cos1] * 4, axis=1), jnp.concatenate([sin1] * 4, axis=1)


def _to_segments(a):
    t, w = a.shape
    return a.reshape(SCAN_SEGMENTS, t // SCAN_SEGMENTS, w).transpose(1, 0, 2)


def _from_segments(a):
    n, s, w = a.shape
    return a.transpose(1, 0, 2).reshape(n * s, w)


def _layer(x, p_i, cos4, sin4, e_mat, w):
    t = x.shape[0]
    row = lambda v: v.reshape(1, -1)
    f32 = lambda v: v.astype(F32)
    hs = dict(zip(H_COLS, op_in_proj(x, op_w_in_ext(w["w_in_pad"], e_mat))))

    cv = op_conv(hs["a_val"], hs["a_gate"], w["conv_w"], row(w["conv_b"]))
    (y_a,) = op_conv_post((cv, hs["a_z"]), (row(w["conv_norm_g"]), row(w["conv_norm_b"]), f32(w["w_pw2"])))

    w_uq, w_uk, w_uv = _ext_mla(w["w_uq"], f32(w["w_ukv"]))
    q, k, v = op_mla_prep((hs["c_q"], hs["c_kv"], hs["krblk"], cos4, sin4),
                          (row(w["mla_q_norm_g"]), row(w["mla_kv_norm_g"]), w_uq, w_uk, w_uv))
    (y_b,) = op_gate((op_mla_attn(q, k, v), hs["b_z"]), ())

    w_bu, w_y, lam = _ext_ssm(w["ssm_a_re"], w["ssm_a_im"], w["ssm_log_dt"], w["ssm_b_re"], w["ssm_b_im"],
                              w["ssm_c_re"], w["ssm_c_im"])
    u_seg = _to_segments(hs["u"]).reshape(t, BRANCH_W)
    bu = op_mm(u_seg, w_bu).reshape(t // SCAN_SEGMENTS, SCAN_SEGMENTS, 2 * SSM_CH)
    hstate = op_scan(bu, lam).reshape(t, 2 * SSM_CH)
    y_ssm = _from_segments(op_mm(hstate, w_y).reshape(t // SCAN_SEGMENTS, SCAN_SEGMENTS, BRANCH_W))
    w_glu = f32(w["w_glu"])
    (y_c,) = op_ssm_post((y_ssm, hs["u"], hs["c_z"]), (row(w["ssm_d"]), w_glu[:, :BRANCH_W], w_glu[:, BRANCH_W:]))

    sink = jnp.repeat(w["attn_sinks"], 64).reshape(1, 2 * LANES)
    (y_d,) = op_gate((op_swa_attn(hs["q"], hs["kdup"], hs["vdup"], sink), hs["d_z"]), ())

    br = [op_mm(y, w["w_branch"][n]) for n, y in enumerate((y_a, y_b, y_c, y_d))]
    gl = op_merge_proj(x, w["w_merge"])
    bm = [row(w["b_merge"][n * D_MODEL:(n + 1) * D_MODEL]) for n in range(4)]
    (merged,) = op_merge((*br, *gl), tuple(bm))
    (x1,) = op_ln((x, op_mm(merged, w["w_out"])), (row(w["ln_g"]), row(w["ln_b"])))
    (out,) = op_ple((x1, op_mm(p_i, w["w_ple"]), op_mm(x1, w["w_ple_gate"])), (row(w["ple_norm_g"]),))
    return out


def _forward(x, p, layers):
    cos4, sin4 = _rope_tables(x.shape[0])
    e_mat = _w_in_layout_matrix()
    for i in range(DEPTH):
        x = _layer(x, p[i], cos4, sin4, e_mat, layers[i])
    return x


SHARD_AXIS = dict(w_in=2, w_merge=2, conv_w=2, w_pw2=1, w_uq=2, w_ukv=2, w_glu=2, w_branch=3, w_out=1, w_ple=2, w_ple_gate=1)
ODD = ("w_uq", "conv_w")
BIG = tuple(n for n in SHARD_AXIS if n not in ODD)
REPLICATED = ("b_merge", "conv_b", "conv_norm_g", "conv_norm_b", "mla_q_norm_g", "mla_kv_norm_g", "ssm_a_re", "ssm_a_im",
              "ssm_log_dt", "ssm_b_re", "ssm_b_im", "ssm_c_re", "ssm_c_im", "ssm_d", "attn_sinks", "ln_g", "ln_b", "ple_norm_g")
WEIGHTS = ("w_in", "w_merge", "b_merge", "conv_w", "conv_b", "conv_norm_g", "conv_norm_b", "w_pw2", "mla_q_norm_g",
           "mla_kv_norm_g", "w_uq", "w_ukv", "ssm_a_re", "ssm_a_im", "ssm_log_dt", "ssm_b_re", "ssm_b_im", "ssm_c_re",
           "ssm_c_im", "ssm_d", "w_glu", "attn_sinks", "w_branch", "w_out", "ln_g", "ln_b", "w_ple", "w_ple_gate", "ple_norm_g")
PACK_COLS = 1024
PACK_ROWS = 16
CHIP_FLIPS = ((1, 0), (0, 1), (1, 1))
N_CHIPS = 4
N_DEV = 8


def _pack_rows(n):
    return -(-n // (SUBLANES * PACK_COLS)) * SUBLANES


def _pack(arrays, dtype):
    blocks, rows = [], 0
    for a in arrays:
        r = _pack_rows(a.size)
        flat = a.reshape(-1).astype(dtype)
        blocks.append(jnp.pad(flat, (0, r * PACK_COLS - a.size)).reshape(r, PACK_COLS))
        rows += r
    pad = -rows % PACK_ROWS
    if pad:
        blocks.append(jnp.zeros((pad, PACK_COLS), dtype))
    return jnp.concatenate(blocks, axis=0)


def _unpack(buf, shapes):
    out, row = [], 0
    for s in shapes:
        n = math.prod(s)
        r = _pack_rows(n)
        out.append(buf[row:row + r].reshape(-1)[:n].reshape(s))
        row += r
    return out


def _flip(v, bit):
    return 1 - v if bit else v


def _window(ref, axis, start, size):
    idx = [slice(None)] * len(ref.shape)
    idx[axis] = pl.ds(start, size)
    return ref.at[tuple(idx)]


def _gather_chips(srcs, axes, stacked):
    units = []
    for k, (s, a) in enumerate(zip(srcs, axes)):
        if stacked[k]:
            units += [(k, l, s.shape[1:], a - 1) for l in range(s.shape[0])]
        else:
            units.append((k, None, s.shape, a))
    nu, nb = len(units), len(srcs)

    def body(*refs):
        ins, outs = refs[:nb], refs[nb:nb + nu]
        ici_send, ici_recv, d2d_send, d2d_recv, local_sems = refs[nb + nu:]
        x, y, c = lax.axis_index("x"), lax.axis_index("y"), lax.axis_index("c")
        me = 2 * x + y

        def mine(u, half=None):
            k, l, shape, _ = units[u]
            ref = ins[k] if l is None else ins[k].at[l]
            return ref if half is None else ref.at[pl.ds(half * (shape[0] // 2), shape[0] // 2)]

        def place(u, chip, half=None):
            _, _, shape, a = units[u]
            size, rows = shape[a], shape[0] // 2
            if half is None:
                return _window(outs[u], a, chip * size, size)
            if a == 0:
                return outs[u].at[pl.ds(chip * size + half * rows, rows)]
            return _window(outs[u].at[pl.ds(half * rows, rows)], a, chip * size, size)

        local = [pltpu.make_async_copy(mine(u), place(u, me), local_sems.at[u]) for u in range(nu)]
        for cp in local:
            cp.start()
        sends = []
        for j, (bx, by) in enumerate(CHIP_FLIPS):
            for u in range(nu):
                cp = pltpu.make_async_remote_copy(src_ref=mine(u, c), dst_ref=place(u, me, c),
                                                  send_sem=ici_send.at[j * nu + u], recv_sem=ici_recv.at[j * nu + u],
                                                  device_id=(_flip(x, bx), _flip(y, by), c), device_id_type=MESH)
                cp.start()
                sends.append(cp)
        for j, (bx, by) in enumerate(CHIP_FLIPS):
            src = 2 * _flip(x, bx) + _flip(y, by)
            for u in range(nu):
                got = place(u, src, c)
                pltpu.make_async_remote_copy(src_ref=got, dst_ref=got, send_sem=ici_send.at[j * nu + u],
                                             recv_sem=ici_recv.at[j * nu + u], device_id=(x, y, c), device_id_type=MESH).wait_recv()
                cp = pltpu.make_async_remote_copy(src_ref=got, dst_ref=got, send_sem=d2d_send.at[j * nu + u],
                                                  recv_sem=d2d_recv.at[j * nu + u], device_id=(x, y, 1 - c), device_id_type=MESH)
                cp.start()
                sends.append(cp)
        for j, (bx, by) in enumerate(CHIP_FLIPS):
            src = 2 * _flip(x, bx) + _flip(y, by)
            for u in range(nu):
                other = place(u, src, 1 - c)
                pltpu.make_async_remote_copy(src_ref=other, dst_ref=other, send_sem=d2d_send.at[j * nu + u],
                                             recv_sem=d2d_recv.at[j * nu + u], device_id=(x, y, c), device_id_type=MESH).wait_recv()
        for cp in sends:
            cp.wait_send()
        for cp in local:
            cp.wait()

    full = lambda shape, a: tuple(N_CHIPS * d if i == a else d for i, d in enumerate(shape))
    res = pl.pallas_call(
        body, name="gather_weights", in_specs=[ANY] * nb, out_specs=[ANY] * nu,
        out_shape=[jax.ShapeDtypeStruct(full(shape, a), srcs[k].dtype) for k, _, shape, a in units],
        scratch_shapes=[pltpu.SemaphoreType.DMA((3 * nu,))] * 4 + [pltpu.SemaphoreType.DMA((nu,))],
    )(*srcs)
    out, it = [], iter(res)
    for k in range(nb):
        out.append([next(it) for _ in range(srcs[k].shape[0])] if stacked[k] else next(it))
    return out


def _exchange_grads(grads, axes, smalls):
    nt, ns = len(grads), len(smalls)
    sizes = [g[0].shape[a] // N_CHIPS for g, a in zip(grads, axes)]
    dev_flips = [(bx, by, bc) for bx in (0, 1) for by in (0, 1) for bc in (0, 1)][1:]
    n_remote = 3 * nt * DEPTH + 7 * ns
    n_local = nt * DEPTH + ns

    def body(*refs):
        g_refs = [refs[k * DEPTH:(k + 1) * DEPTH] for k in range(nt)]
        s_refs = refs[nt * DEPTH:nt * DEPTH + ns]
        outs = refs[nt * DEPTH + ns:nt * DEPTH + ns + nt + ns]
        recv_refs, all_refs = outs[:nt], outs[nt:]
        send_sems, recv_sems, local_sems = refs[-3:]
        x, y, c = lax.axis_index("x"), lax.axis_index("y"), lax.axis_index("c")
        me_chip = 2 * x + y
        me = 4 * x + 2 * y + c
        part = lambda k, i, chip: _window(g_refs[k][i], axes[k], chip * sizes[k], sizes[k])
        started, waits = [], []
        sem, lsem = 0, 0
        for k in range(nt):
            for i in range(DEPTH):
                cp = pltpu.make_async_copy(part(k, i, me_chip), recv_refs[k].at[i, 3], local_sems.at[lsem])
                cp.start()
                started.append(cp.wait)
                lsem += 1
                for j, (bx, by) in enumerate(CHIP_FLIPS):
                    px, py = _flip(x, bx), _flip(y, by)
                    cp = pltpu.make_async_remote_copy(src_ref=part(k, i, 2 * px + py), dst_ref=recv_refs[k].at[i, j],
                                                      send_sem=send_sems.at[sem], recv_sem=recv_sems.at[sem],
                                                      device_id=(px, py, c), device_id_type=MESH)
                    cp.start()
                    started.append(cp.wait_send)
                    waits.append(cp.wait_recv)
                    sem += 1
        for s in range(ns):
            cp = pltpu.make_async_copy(s_refs[s], all_refs[s].at[me], local_sems.at[lsem])
            cp.start()
            started.append(cp.wait)
            lsem += 1
            for bx, by, bc in dev_flips:
                peer = (_flip(x, bx), _flip(y, by), _flip(c, bc))
                cp = pltpu.make_async_remote_copy(src_ref=s_refs[s], dst_ref=all_refs[s].at[me], send_sem=send_sems.at[sem],
                                                  recv_sem=recv_sems.at[sem], device_id=peer, device_id_type=MESH)
                cp.start()
                started.append(cp.wait_send)
                src = 4 * peer[0] + 2 * peer[1] + peer[2]
                waits.append(pltpu.make_async_remote_copy(src_ref=s_refs[s], dst_ref=all_refs[s].at[src], send_sem=send_sems.at[sem],
                                                          recv_sem=recv_sems.at[sem], device_id=peer, device_id_type=MESH).wait_recv)
                sem += 1
        for w in waits + started:
            w()

    shard = lambda g, a: tuple(d // N_CHIPS if i == a else d for i, d in enumerate(g.shape))
    flat = [g for per_layer in grads for g in per_layer]
    return pl.pallas_call(
        body, name="exchange_grads", in_specs=[ANY] * (len(flat) + ns), out_specs=[ANY] * (nt + ns),
        out_shape=[jax.ShapeDtypeStruct((DEPTH, N_CHIPS, *shard(g[0], a)), g[0].dtype) for g, a in zip(grads, axes)]
        + [jax.ShapeDtypeStruct((N_DEV, *s.shape), s.dtype) for s in smalls],
        scratch_shapes=[pltpu.SemaphoreType.DMA((n_remote,)), pltpu.SemaphoreType.DMA((n_remote,)), pltpu.SemaphoreType.DMA((n_local,))],
    )(*flat, *smalls)


def _swap_cores(parts):
    nb = len(parts)

    def body(*refs):
        ins, outs, send_sems, recv_sems = refs[:nb], refs[nb:2 * nb], refs[-2], refs[-1]
        x, y, c = lax.axis_index("x"), lax.axis_index("y"), lax.axis_index("c")
        cps = [pltpu.make_async_remote_copy(src_ref=ins[k], dst_ref=outs[k], send_sem=send_sems.at[k], recv_sem=recv_sems.at[k],
                                            device_id=(x, y, 1 - c), device_id_type=MESH) for k in range(nb)]
        for cp in cps:
            cp.start()
        for cp in cps:
            cp.wait()

    return pl.pallas_call(
        body, name="swap_cores", in_specs=[ANY] * nb, out_specs=[ANY] * nb,
        out_shape=[jax.ShapeDtypeStruct(q.shape, q.dtype) for q in parts],
        scratch_shapes=[pltpu.SemaphoreType.DMA((nb,)), pltpu.SemaphoreType.DMA((nb,))],
    )(*parts)


def _sum_chips_call(recv, cols, name):
    depth, _, r, c = recv.shape
    tile = _pick(r, (512, 256, 128, 64, 32, 16))

    def body(r_ref, o_ref):
        slot = lambda s: r_ref[s, :, pl.ds(0, cols)].astype(F32)
        o_ref[...] = ((slot(3) + slot(0)) + slot(1)) + slot(2)

    return pl.pallas_call(
        body, name=name, grid=(depth, r // tile),
        in_specs=[pl.BlockSpec((None, N_CHIPS, tile, c), lambda l, i: (l, 0, i, 0))],
        out_specs=pl.BlockSpec((None, tile, cols), lambda l, i: (l, i, 0)), out_shape=jax.ShapeDtypeStruct((depth, r, cols), F32),
        compiler_params=_params(("parallel", "parallel")),
    )(recv)


def _sum_slots_call(slots, name):
    n, r, c = slots.shape
    tile = _pick(r, (512, 256, 128, 64, 32, 16, 8))

    def body(s_ref, o_ref):
        acc = s_ref[0]
        for s in range(1, n):
            acc = acc + s_ref[s]
        o_ref[...] = acc

    return pl.pallas_call(
        body, name=name, grid=(r // tile,), in_specs=[pl.BlockSpec((n, tile, c), lambda i: (0, i, 0))],
        out_specs=pl.BlockSpec((tile, c), lambda i: (i, 0)), out_shape=jax.ShapeDtypeStruct((r, c), F32),
        compiler_params=_params(("parallel",)),
    )(slots)


def _adamw_math(w, g, m, v):
    m = ADAM_B1 * m + (1.0 - ADAM_B1) * g
    v = ADAM_B2 * v + (1.0 - ADAM_B2) * (g * g)
    m_hat = m / (1.0 - ADAM_B1 ** ADAM_STEP)
    v_hat = v / (1.0 - ADAM_B2 ** ADAM_STEP)
    return -ADAM_LR * (m_hat / (jnp.sqrt(v_hat) + ADAM_EPS) + ADAM_WD * w), m, v


def _adamw_call(w, m, v, gparts, name):
    r, c = w.shape
    n = len(gparts)
    tile = _pick(r, (512, 256, 128, 64, 32, 16, 8))

    def body(w_ref, m_ref, v_ref, *refs):
        g_refs, (go_ref, d_ref, mo_ref, vo_ref) = refs[:n], refs[n:]
        g = g_refs[0][...]
        for g_ref in g_refs[1:]:
            g = g + g_ref[...]
        go_ref[...] = g
        d_ref[...], mo_ref[...], vo_ref[...] = _adamw_math(w_ref[...], g, m_ref[...], v_ref[...])

    blk = pl.BlockSpec((tile, c), lambda i: (i, 0))
    return pl.pallas_call(
        body, name=name, grid=(r // tile,), in_specs=[blk] * (3 + n),
        out_specs=[blk] * 4, out_shape=[jax.ShapeDtypeStruct((r, c), F32)] * 4,
        compiler_params=_params(("parallel",)),
    )(w, m, v, *gparts)


def _train_local(x, p, layers, target):
    y, vjp = jax.vjp(lambda x_, w_: _forward(x_, p, w_), x, layers)
    loss, dy = _loss_call(y, target)
    dx, dw = vjp(dy)
    return loss, dx, dw


def kernel(x, p, w_in, w_merge, b_merge, conv_w, conv_b, conv_norm_g, conv_norm_b, w_pw2, mla_q_norm_g, mla_kv_norm_g, w_uq, w_ukv, ssm_a_re, ssm_a_im, ssm_log_dt, ssm_b_re, ssm_b_im, ssm_c_re, ssm_c_im, ssm_d, w_glu, attn_sinks, w_branch, w_out, ln_g, ln_b, w_ple, w_ple_gate, ple_norm_g, loss_target, m_w_in, m_w_merge, m_b_merge, m_conv_w, m_conv_b, m_conv_norm_g, m_conv_norm_b, m_w_pw2, m_mla_q_norm_g, m_mla_kv_norm_g, m_w_uq, m_w_ukv, m_ssm_a_re, m_ssm_a_im, m_ssm_log_dt, m_ssm_b_re, m_ssm_b_im, m_ssm_c_re, m_ssm_c_im, m_ssm_d, m_w_glu, m_attn_sinks, m_w_branch, m_w_out, m_ln_g, m_ln_b, m_w_ple, m_w_ple_gate, m_ple_norm_g, v_w_in, v_w_merge, v_b_merge, v_conv_w, v_conv_b, v_conv_norm_g, v_conv_norm_b, v_w_pw2, v_mla_q_norm_g, v_mla_kv_norm_g, v_w_uq, v_w_ukv, v_ssm_a_re, v_ssm_a_im, v_ssm_log_dt, v_ssm_b_re, v_ssm_b_im, v_ssm_c_re, v_ssm_c_im, v_ssm_d, v_w_glu, v_attn_sinks, v_w_branch, v_w_out, v_ln_g, v_ln_b, v_w_ple, v_w_ple_gate, v_ple_norm_g):
    given = dict(locals())
    w_loc = {n: given[n] for n in WEIGHTS}
    m_loc = {n: given["m_" + n] for n in WEIGHTS}
    v_loc = {n: given["v_" + n] for n in WEIGHTS}

    me_chip = 2 * lax.axis_index("x") + lax.axis_index("y")

    wire = {n: w_loc[n].astype(MXU_DTYPE) for n in BIG}
    wire["w_in"] = jnp.pad(wire["w_in"], ((0, 0), (0, 0), (0, IN_SHARD_PAD - IN_SHARD)))
    odd_shapes = [w_loc[n].shape for n in ODD]
    gathered = _gather_chips([wire[n] for n in BIG] + [_pack([w_loc[n] for n in ODD], F32)], [SHARD_AXIS[n] for n in BIG] + [0],
                             [True] * len(BIG) + [False])
    full = dict(zip(BIG, gathered[:-1]))
    odd_parts = [_unpack(part, odd_shapes) for part in jnp.split(gathered[-1], N_CHIPS, axis=0)]
    for k, n in enumerate(ODD):
        full[n] = jnp.concatenate([odd_parts[s][k] for s in range(N_CHIPS)], axis=SHARD_AXIS[n])
    layers = []
    for i in range(DEPTH):
        layer = {n: (full[n][i] if n in full else w_loc[n][i]) for n in WEIGHTS if n != "w_in"}
        layer["w_in_pad"] = full["w_in"][i]
        layers.append(layer)

    loss, dx, dw = _train_local(x[0], p[:, 0], layers, loss_target[0])
    loss = lax.psum(loss, ("x", "y", "c"))

    key = lambda n: "w_in_pad" if n == "w_in" else n
    stacked = lambda n: jnp.stack([dw[i][n] for i in range(DEPTH)])
    small_rep = _pack([stacked(n) for n in REPLICATED], F32)
    small_odd = _pack([stacked(n) for n in ODD], F32)
    *recv, all_rep, all_odd = _exchange_grads([[dw[i][key(n)] for i in range(DEPTH)] for n in BIG],
                                              [SHARD_AXIS[n] - 1 for n in BIG], [small_rep, small_odd])
    parts = []
    for n, r in zip(BIG, recv):
        cols = w_loc[n].shape[-1]
        parts.append(_sum_chips_call(r.reshape(DEPTH, N_CHIPS, -1, r.shape[-1]), cols, "sum_chips_" + n))
    others = _swap_cores(parts)
    g_rep = _sum_slots_call(all_rep, "sum_replicated")
    g_odd = _unpack(_sum_slots_call(all_odd, "sum_odd"), [(DEPTH, *w_loc[n].shape[1:-1], N_CHIPS * w_loc[n].shape[-1]) for n in ODD])

    grads, deltas, new_m, new_v = {}, {}, {}, {}

    def adamw(n, gparts):
        shape = w_loc[n].shape
        two_d = lambda a: a.reshape(-1, shape[-1])
        res = _adamw_call(two_d(w_loc[n]), two_d(m_loc[n]), two_d(v_loc[n]), [two_d(g) for g in gparts], "adamw_" + n)
        grads[n], deltas[n], new_m[n], new_v[n] = [r.reshape(shape) for r in res]

    for n, part, other in zip(BIG, parts, others):
        adamw(n, [part, other])
    for n, g in zip(ODD, g_odd):
        size = w_loc[n].shape[-1]
        adamw(n, [lax.dynamic_slice_in_dim(g, me_chip * size, size, axis=g.ndim - 1)])
    rep_shapes = [w_loc[n].shape for n in REPLICATED]
    res = _adamw_call(_pack([w_loc[n] for n in REPLICATED], F32), _pack([m_loc[n] for n in REPLICATED], F32),
                      _pack([v_loc[n] for n in REPLICATED], F32), [g_rep], "adamw_replicated")
    for dst, buf in zip((grads, deltas, new_m, new_v), res):
        for n, a in zip(REPLICATED, _unpack(buf, rep_shapes)):
            dst[n] = a

    return (loss, dx[None], *[grads[n] for n in WEIGHTS], *[deltas[n] for n in WEIGHTS],
            *[new_m[n] for n in WEIGHTS], *[new_v[n] for n in WEIGHTS])
```

```python
import functools
import math

import jax
import jax.numpy as jnp
import numpy as np
from jax import lax
from jax.experimental import pallas as pl
from jax.experimental.pallas import tpu as pltpu

F32 = jnp.float32
BF16 = jnp.bfloat16
MXU_DTYPE = BF16
V7X_VMEM_BYTES = 64 * 1024 * 1024
VMEM_LIMIT = V7X_VMEM_BYTES * 3 // 4
LANES = 128
SUBLANES = 8

D_MODEL = 1024
DEPTH = 4
BRANCH_W = 256
CONV_W = 31
CONV_HALO = 32
MLA_SCALE = (64 + 32) ** -0.5
SWA_SCALE = 64 ** -0.5
WINDOW = 128
ROPE_THETA = 10000.0
SSM_GROUPS, SSM_GROUP, SSM_STATE = 16, 16, 64
SSM_CH = SSM_GROUPS * SSM_STATE
SCAN_SEGMENTS = SUBLANES
SCAN_CB = 128
SCAN_UNROLL = 8
DEEPNORM_ALPHA = (2.0 * DEPTH) ** 0.25
LN_EPS = 1e-5
RMS_EPS = 1e-6
ADAM_LR, ADAM_B1, ADAM_B2, ADAM_EPS, ADAM_WD, ADAM_STEP = 0.001, 0.9, 0.999, 1e-08, 0.01, 10
NEG = -1e30
ROW_TILE = 512

NN = (((1,), (0,)), ((), ()))
NT = (((1,), (1,)), ((), ()))
TN = (((0,), (0,)), ((), ()))

MESH = pl.DeviceIdType.MESH
ANY = pl.BlockSpec(memory_space=pl.ANY)


def _dot(a, b, dims):
    return lax.dot_general(a.astype(MXU_DTYPE), b.astype(MXU_DTYPE), dims, preferred_element_type=F32)


def _pick(n, cands):
    for c in cands:
        if n % c == 0:
            return c
    return n


def _params(sem):
    return pltpu.CompilerParams(dimension_semantics=sem, vmem_limit_bytes=VMEM_LIMIT)


def _col_offsets(widths):
    return [sum(widths[:j]) for j in range(len(widths))]


def _proj_fwd_call(x, wb, widths, name):
    t, k = x.shape
    tm = min(ROW_TILE, t)
    offs = _col_offsets(widths)

    def body(x_ref, w_ref, *o_refs):
        xb = x_ref[...].astype(MXU_DTYPE)
        for o_ref, off, wd in zip(o_refs, offs, widths):
            o_ref[...] = _dot(xb, w_ref[:, off:off + wd], NN)

    return pl.pallas_call(
        body, name=name, grid=(t // tm,),
        in_specs=[pl.BlockSpec((tm, k), lambda i: (i, 0)), pl.BlockSpec(wb.shape, lambda i: (0, 0))],
        out_specs=[pl.BlockSpec((tm, wd), lambda i: (i, 0)) for wd in widths],
        out_shape=[jax.ShapeDtypeStruct((t, wd), F32) for wd in widths],
        compiler_params=_params(("parallel",)),
    )(x, wb)


def _proj_dx_call(douts, wb, widths, name):
    t = douts[0].shape[0]
    k = wb.shape[0]
    tm = min(ROW_TILE, t)
    offs = _col_offsets(widths)

    def body(*refs):
        d_refs, w_ref, o_ref = refs[:-2], refs[-2], refs[-1]
        acc = jnp.zeros((tm, k), F32)
        for d_ref, off, wd in zip(d_refs, offs, widths):
            acc = acc + _dot(d_ref[...], w_ref[:, off:off + wd], NT)
        o_ref[...] = acc

    return pl.pallas_call(
        body, name=name, grid=(t // tm,),
        in_specs=[pl.BlockSpec((tm, wd), lambda i: (i, 0)) for wd in widths] + [pl.BlockSpec(wb.shape, lambda i: (0, 0))],
        out_specs=pl.BlockSpec((tm, k), lambda i: (i, 0)), out_shape=jax.ShapeDtypeStruct((t, k), F32),
        compiler_params=_params(("parallel",)),
    )(*douts, wb)


def _proj_dw_call(x, douts, widths, name, out_dtype):
    t, k = x.shape
    n = sum(widths)
    tk = min(ROW_TILE if k * n <= 2 * 1024 * 1024 else ROW_TILE // 2, t)
    nk = t // tk
    offs = _col_offsets(widths)
    n = sum(widths)

    def body(x_ref, *refs):
        d_refs, o_ref, acc_ref = refs[:-2], refs[-2], refs[-1]

        @pl.when(pl.program_id(0) == 0)
        def _():
            acc_ref[...] = jnp.zeros_like(acc_ref)

        xb = x_ref[...].astype(MXU_DTYPE)
        for d_ref, off, wd in zip(d_refs, offs, widths):
            acc_ref[:, off:off + wd] += _dot(xb, d_ref[...], TN)

        @pl.when(pl.program_id(0) == nk - 1)
        def _():
            o_ref[...] = acc_ref[...].astype(out_dtype)

    return pl.pallas_call(
        body, name=name, grid=(nk,),
        in_specs=[pl.BlockSpec((tk, k), lambda i: (i, 0))] + [pl.BlockSpec((tk, wd), lambda i: (i, 0)) for wd in widths],
        out_specs=pl.BlockSpec((k, n), lambda i: (0, 0)), out_shape=jax.ShapeDtypeStruct((k, n), out_dtype),
        scratch_shapes=[pltpu.VMEM((k, n), F32)],
        compiler_params=_params(("arbitrary",)),
    )(x, *douts)


def make_proj(widths, name):
    @jax.custom_vjp
    def op(x, w):
        return tuple(_proj_fwd_call(x, w.astype(MXU_DTYPE), widths, name + "_fwd"))

    def fwd(x, w):
        wb = w.astype(MXU_DTYPE)
        return tuple(_proj_fwd_call(x, wb, widths, name + "_fwd")), (x, wb, jnp.zeros((0,), w.dtype))

    def bwd(res, douts):
        x, wb, w_like = res
        return _proj_dx_call(douts, wb, widths, name + "_dx"), _proj_dw_call(x, douts, widths, name + "_dw", w_like.dtype)

    op.defvjp(fwd, bwd)
    return op


_MM_OPS = {}


def op_mm(a, w):
    n = w.shape[1]
    if n not in _MM_OPS:
        _MM_OPS[n] = make_proj((n,), "mm%d" % n)
    return _MM_OPS[n](a, w)[0]


@jax.custom_vjp
def _mm(a, w):
    return _dot(a, w, NN)


def _mm_f(a, w):
    return _dot(a, w, NN), (a, w)


def _mm_b(res, g):
    a, w = res
    return _dot(g, w, NT), _dot(a, g, TN)


_mm.defvjp(_mm_f, _mm_b)


@functools.partial(jax.custom_vjp, nondiff_argnums=(1,))
def _roll(x, shift):
    return pltpu.roll(x, shift, 1)


def _roll_f(x, shift):
    return pltpu.roll(x, shift, 1), None


def _roll_b(shift, _, g):
    return (pltpu.roll(g, (g.shape[1] - shift) % g.shape[1], 1),)


_roll.defvjp(_roll_f, _roll_b)


def _ln(x, g, b):
    mu = jnp.mean(x, axis=-1, keepdims=True)
    xc = x - mu
    var = jnp.mean(xc * xc, axis=-1, keepdims=True)
    return xc * lax.rsqrt(var + LN_EPS) * g + b


def _rms(x, g):
    ms = jnp.mean(x * x, axis=-1, keepdims=True)
    return x * lax.rsqrt(ms + RMS_EPS) * g


def _sigmoid(x):
    return jax.nn.sigmoid(x)


def _silu(x):
    return x * _sigmoid(x)


def _gelu_tanh(x):
    return x * (0.5 * (1.0 + jnp.tanh(math.sqrt(2.0 / math.pi) * (x + 0.044715 * (x * x * x)))))


def _rowwise_fwd_call(fn, rows, consts, name, tile):
    t = rows[0].shape[0]
    tile = min(tile, t)
    nr = len(rows)
    outs = jax.eval_shape(fn, *[jax.ShapeDtypeStruct((tile, r.shape[1]), F32) for r in rows],
                          *[jax.ShapeDtypeStruct(c.shape, F32) for c in consts])

    def body(*refs):
        vals = [r[...] for r in refs[:nr + len(consts)]]
        res = fn(*vals)
        for o_ref, o in zip(refs[nr + len(consts):], res):
            o_ref[...] = o

    return pl.pallas_call(
        body, name=name, grid=(t // tile,),
        in_specs=[pl.BlockSpec((tile, r.shape[1]), lambda i: (i, 0)) for r in rows]
        + [pl.BlockSpec(c.shape, lambda i: (0, 0)) for c in consts],
        out_specs=[pl.BlockSpec((tile, o.shape[1]), lambda i: (i, 0)) for o in outs],
        out_shape=[jax.ShapeDtypeStruct((t, o.shape[1]), F32) for o in outs],
        compiler_params=_params(("parallel",)),
    )(*rows, *consts)


def _rowwise_bwd_call(fn, rows, consts, douts, row_diff, name, tile, row_grad_dtype=F32):
    t = rows[0].shape[0]
    tile = min(tile, t)
    nr, nc, nd = len(rows), len(consts), len(douts)
    diff_idx = [i for i in range(nr) if row_diff[i]]

    def body(*refs):
        rv = [r[...] for r in refs[:nr]]
        cv = [r[...] for r in refs[nr:nr + nc]]
        dv = [r[...] for r in refs[nr + nc:nr + nc + nd]]
        out_refs = refs[nr + nc + nd:]

        def f(*diff):
            full = list(rv)
            for k, i in enumerate(diff_idx):
                full[i] = diff[k]
            return fn(*full, *diff[len(diff_idx):])

        _, vjp = jax.vjp(f, *[rv[i] for i in diff_idx], *cv)
        grads = vjp(tuple(dv))
        for k in range(len(diff_idx)):
            out_refs[k][...] = grads[k].astype(row_grad_dtype)
        first = pl.program_id(0) == 0
        for k in range(nc):
            acc_ref = out_refs[len(diff_idx) + k]
            g = grads[len(diff_idx) + k]

            @pl.when(first)
            def _(acc_ref=acc_ref, g=g):
                acc_ref[...] = g

            @pl.when(jnp.logical_not(first))
            def _(acc_ref=acc_ref, g=g):
                acc_ref[...] += g

    res = pl.pallas_call(
        body, name=name, grid=(t // tile,),
        in_specs=[pl.BlockSpec((tile, r.shape[1]), lambda i: (i, 0)) for r in rows]
        + [pl.BlockSpec(c.shape, lambda i: (0, 0)) for c in consts]
        + [pl.BlockSpec((tile, d.shape[1]), lambda i: (i, 0)) for d in douts],
        out_specs=[pl.BlockSpec((tile, rows[i].shape[1]), lambda i_: (i_, 0)) for i in diff_idx]
        + [pl.BlockSpec(c.shape, lambda i: (0, 0)) for c in consts],
        out_shape=[jax.ShapeDtypeStruct(rows[i].shape, row_grad_dtype) for i in diff_idx]
        + [jax.ShapeDtypeStruct(c.shape, F32) for c in consts],
        compiler_params=_params(("arbitrary",)),
    )(*rows, *consts, *douts)
    return res[:len(diff_idx)], res[len(diff_idx):]


def make_rowwise(fn, name, row_diff, tile=ROW_TILE):
    @jax.custom_vjp
    def op(rows, consts):
        return tuple(_rowwise_fwd_call(fn, rows, consts, name + "_fwd", tile))

    def fwd(rows, consts):
        return op(rows, consts), (rows, consts)

    def bwd(res, douts):
        rows, consts = res
        drows, dconsts = _rowwise_bwd_call(fn, rows, consts, douts, row_diff, name + "_bwd", tile)
        it = iter(drows)
        full = tuple(next(it) if row_diff[i] else jnp.zeros_like(rows[i]) for i in range(len(rows)))
        return full, tuple(dconsts)

    op.defvjp(fwd, bwd)
    return op


def _conv_post_fn(cv, a_z, ng, nb, w_pw2):
    return (_mm(_silu(_ln(cv, ng, nb)), w_pw2) * _silu(a_z),)


def _mla_prep_fn(c_q, c_kv, krblk, cos4, sin4, qg, kvg, w_uq, w_uk, w_uv):
    qe = _mm(_rms(c_q, qg), w_uq)
    q = qe * cos4 + _roll(qe, qe.shape[1] - 32) * sin4
    cos1, sin1 = cos4[:, :LANES], sin4[:, :LANES]
    kr = krblk * cos1 + _roll(krblk, LANES - 32) * sin1
    kn = _rms(c_kv, kvg)
    k = _mm(kn, w_uk) + jnp.concatenate([kr, kr, kr, kr], axis=1)
    return q, k, _mm(kn, w_uv)


def _ssm_post_fn(y, u, c_z, d, w_a, w_b):
    y2 = _gelu_tanh(y + d * u)
    return (_mm(y2, w_a) * _sigmoid(_mm(y2, w_b)) * _silu(c_z),)


def _gate_fn(o, z):
    return (o * _silu(z),)


def _merge_fn(br0, br1, br2, br3, gl0, gl1, gl2, gl3, b0, b1, b2, b3):
    return (_sigmoid(gl0 + b0) * br0 + _sigmoid(gl1 + b1) * br1 + _sigmoid(gl2 + b2) * br2 + _sigmoid(gl3 + b3) * br3,)


def _ln_fn(x, mo, g, b):
    return (_ln(DEEPNORM_ALPHA * x + mo, g, b),)


def _ple_fn(x1, pe, gl, g):
    return (x1 + _rms(pe * _sigmoid(gl), g),)


op_conv_post = make_rowwise(_conv_post_fn, "conv_post", (True, True))
op_mla_prep = make_rowwise(_mla_prep_fn, "mla_prep", (True, True, True, False, False))
op_ssm_post = make_rowwise(_ssm_post_fn, "ssm_post", (True, True, True))
op_gate = make_rowwise(_gate_fn, "gate", (True, True))
MERGE_TILE = ROW_TILE // 2
MERGE_WIDTHS = (D_MODEL,) * 4


@jax.custom_vjp
def op_merge_block(x, ys, w_merge, w_branch, b_merge):
    return _merge_block_fwd(x, ys, w_merge, w_branch, b_merge)[0]


def _merge_block_fwd(x, ys, w_merge, w_branch, b_merge):
    wm, wb = w_merge.astype(MXU_DTYPE), w_branch.astype(MXU_DTYPE)
    gl = _proj_fwd_call(x, wm, MERGE_WIDTHS, "merge_proj_fwd")
    br = [_proj_fwd_call(y, wb[n], (D_MODEL,), "branch_proj_fwd")[0] for n, y in enumerate(ys)]
    bm = tuple(b_merge[n * D_MODEL:(n + 1) * D_MODEL].reshape(1, -1) for n in range(4))
    (merged,) = _rowwise_fwd_call(_merge_fn, (*br, *gl), bm, "merge_fwd", MERGE_TILE)
    return merged, (x, ys, wm, wb, tuple(br), tuple(gl), bm, jnp.zeros((0,), w_merge.dtype), jnp.zeros((0,), w_branch.dtype))


def _merge_block_bwd(res, dmerged):
    x, ys, wm, wb, br, gl, bm, wm_like, wb_like = res
    drows, dbm = _rowwise_bwd_call(_merge_fn, (*br, *gl), bm, (dmerged,), (True,) * 8, "merge_bwd", MERGE_TILE, MXU_DTYPE)
    dbr, dgl = drows[:4], drows[4:]
    dx = _proj_dx_call(dgl, wm, MERGE_WIDTHS, "merge_proj_dx")
    dwm = _proj_dw_call(x, dgl, MERGE_WIDTHS, "merge_proj_dw", wm_like.dtype)
    dys = tuple(_proj_dx_call([dbr[n]], wb[n], (D_MODEL,), "branch_proj_dx") for n in range(4))
    dwb = jnp.stack([_proj_dw_call(ys[n], [dbr[n]], (D_MODEL,), "branch_proj_dw", wb_like.dtype) for n in range(4)])
    return dx, dys, dwm, dwb, jnp.concatenate([d.reshape(-1) for d in dbm])


op_merge_block.defvjp(_merge_block_fwd, _merge_block_bwd)
op_ln = make_rowwise(_ln_fn, "post_ln", (True, True))
op_ple = make_rowwise(_ple_fn, "ple", (True, True, True))


def _conv_fwd_call(a_val, a_gate, w32, b):
    t, w = a_val.shape
    tile = min(ROW_TILE, t)
    per = tile // CONV_HALO
    cur = pl.BlockSpec((tile, w), lambda i: (i, 0))
    prev = pl.BlockSpec((CONV_HALO, w), lambda i: (jnp.maximum(i * per - 1, 0), 0))

    def body(av_ref, avh_ref, ag_ref, agh_ref, w_ref, b_ref, cv_ref, buf):
        i = pl.program_id(0)
        gh = avh_ref[...] * _sigmoid(agh_ref[...])
        buf[0:CONV_HALO, :] = jnp.where(i > 0, gh, 0.0)
        buf[CONV_HALO:, :] = av_ref[...] * _sigmoid(ag_ref[...])
        acc = jnp.zeros((tile, w), F32) + b_ref[...]
        for j in range(CONV_W):
            acc = acc + w_ref[j:j + 1, :] * buf[pl.ds(CONV_HALO - (CONV_W - 1) + j, tile), :]
        cv_ref[...] = acc

    return pl.pallas_call(
        body, name="conv_fwd", grid=(t // tile,),
        in_specs=[cur, prev, cur, prev, pl.BlockSpec((CONV_HALO, w), lambda i: (0, 0)), pl.BlockSpec((1, w), lambda i: (0, 0))],
        out_specs=cur, out_shape=jax.ShapeDtypeStruct((t, w), F32),
        scratch_shapes=[pltpu.VMEM((tile + CONV_HALO, w), F32)],
        compiler_params=_params(("parallel",)),
    )(a_val, a_val, a_gate, a_gate, w32, b)


def _conv_bwd_call(a_val, a_gate, w32, dcv):
    t, w = a_val.shape
    tile = min(ROW_TILE, t)
    n = t // tile
    per = tile // CONV_HALO
    cur = pl.BlockSpec((tile, w), lambda i: (i, 0))
    prev = pl.BlockSpec((CONV_HALO, w), lambda i: (jnp.maximum(i * per - 1, 0), 0))
    nxt = pl.BlockSpec((CONV_HALO, w), lambda i: (jnp.minimum((i + 1) * per, t // CONV_HALO - 1), 0))
    full = lambda r: pl.BlockSpec((r, w), lambda i: (0, 0))

    def body(av_ref, avh_ref, ag_ref, agh_ref, w_ref, d_ref, dn_ref, dav_ref, dag_ref, dw_ref, db_ref, gbuf, dbuf):
        i = pl.program_id(0)
        gh = avh_ref[...] * _sigmoid(agh_ref[...])
        gbuf[0:CONV_HALO, :] = jnp.where(i > 0, gh, 0.0)
        av = av_ref[...]
        sg = _sigmoid(ag_ref[...])
        gbuf[CONV_HALO:, :] = av * sg
        d = d_ref[...]
        dbuf[0:tile, :] = d
        dbuf[tile:, :] = jnp.where(i < n - 1, dn_ref[...], 0.0)

        @pl.when(i == 0)
        def _():
            dw_ref[...] = jnp.zeros_like(dw_ref)
            db_ref[...] = jnp.zeros_like(db_ref)

        dg = jnp.zeros((tile, w), F32)
        for j in range(CONV_W):
            dg = dg + w_ref[j:j + 1, :] * dbuf[pl.ds(CONV_W - 1 - j, tile), :]
            dw_ref[j:j + 1, :] += jnp.sum(d * gbuf[pl.ds(CONV_HALO - (CONV_W - 1) + j, tile), :], axis=0, keepdims=True)
        db_ref[...] += jnp.sum(d, axis=0, keepdims=True)
        dav_ref[...] = dg * sg
        dag_ref[...] = dg * av * sg * (1.0 - sg)

    return pl.pallas_call(
        body, name="conv_bwd", grid=(n,),
        in_specs=[cur, prev, cur, prev, full(CONV_HALO), cur, nxt],
        out_specs=[cur, cur, full(CONV_HALO), full(1)],
        out_shape=[jax.ShapeDtypeStruct((t, w), F32), jax.ShapeDtypeStruct((t, w), F32),
                   jax.ShapeDtypeStruct((CONV_HALO, w), F32), jax.ShapeDtypeStruct((1, w), F32)],
        scratch_shapes=[pltpu.VMEM((tile + CONV_HALO, w), F32), pltpu.VMEM((tile + CONV_HALO, w), F32)],
        compiler_params=_params(("arbitrary",)),
    )(a_val, a_val, a_gate, a_gate, w32, dcv, dcv)


def _pad_taps(conv_w):
    return jnp.concatenate([conv_w, jnp.zeros((CONV_HALO - CONV_W, conv_w.shape[1]), F32)], axis=0)


@jax.custom_vjp
def op_conv(a_val, a_gate, conv_w, conv_b):
    return _conv_fwd_call(a_val, a_gate, _pad_taps(conv_w), conv_b)


def _op_conv_fwd(a_val, a_gate, conv_w, conv_b):
    return op_conv(a_val, a_gate, conv_w, conv_b), (a_val, a_gate, conv_w)


def _op_conv_bwd(res, dcv):
    a_val, a_gate, conv_w = res
    dav, dag, dw, db = _conv_bwd_call(a_val, a_gate, _pad_taps(conv_w), dcv)
    return dav, dag, dw[:CONV_W], db


op_conv.defvjp(_op_conv_fwd, _op_conv_bwd)


def _head_masks(rows):
    lane = lax.broadcasted_iota(jnp.int32, (rows, LANES), 1)
    return lane < 64, lane >= 64


def _head_row(vals, mask):
    return jnp.max(jnp.where(mask, vals, NEG), axis=1, keepdims=True)


def _attn_valid(qpos, kpos, window):
    valid = kpos <= qpos
    if window is not None:
        valid = jnp.logical_and(valid, qpos - kpos < window)
    return valid


def _flash_fwd_call(q, k, v, sink, *, window, shared_k, scale, blk, blk_q, name):
    t = q.shape[0]
    qw = LANES if shared_k else 2 * LANES
    pairs = v.shape[1] // LANES
    tk = min(blk, t)
    tq = min(blk_q, t)
    has_sink = sink is not None

    def body(*refs):
        if has_sink:
            q_ref, k_ref, v_ref, s_ref, o_ref, lse_ref, k_mxu, v0_mxu, v1_mxu = refs
        else:
            q_ref, k_ref, v_ref, o_ref, lse_ref, k_mxu, v0_mxu, v1_mxu = refs
        v_mxu = (v0_mxu, v1_mxu)
        i = pl.program_id(1)

        @pl.when(i == 0)
        def _():
            full_masks = _head_masks(t)
            k_mxu[...] = k_ref[...].astype(MXU_DTYPE)
            for h in range(2):
                v_mxu[h][...] = jnp.where(full_masks[h], v_ref[...], 0.0).astype(MXU_DTYPE)

        qb = q_ref[...]
        masks = _head_masks(tq)
        row_masks = _head_masks(1)
        qh = [(jnp.where(masks[h], qb, 0.0) if shared_k else qb[:, h * LANES:(h + 1) * LANES]).astype(MXU_DTYPE) for h in range(2)]
        qpos = i * tq + lax.broadcasted_iota(jnp.int32, (tq, tk), 0)
        if has_sink:
            m_init = [jnp.zeros((tq, 1), F32) + _head_row(s_ref[...], row_masks[h]) for h in range(2)]
            l_init = [jnp.ones((tq, 1), F32)] * 2
        else:
            m_init = [jnp.full((tq, 1), NEG, F32)] * 2
            l_init = [jnp.zeros((tq, 1), F32)] * 2

        def make_step(masked):
            def step(j, carry):
                m0, l0, m1, l1, acc = carry
                start = pl.multiple_of(j * tk, tk)
                kb = k_mxu[pl.ds(start, tk), :]
                if masked:
                    valid = _attn_valid(qpos, j * tk + lax.broadcasted_iota(jnp.int32, (tq, tk), 1), window)
                new, alphas, pv = [], [], []
                for h, (m, l) in enumerate(((m0, l0), (m1, l1))):
                    kh = kb if shared_k else kb[:, h * LANES:(h + 1) * LANES]
                    s = _dot(qh[h], kh, NT) * scale
                    if masked:
                        s = jnp.where(valid, s, NEG)
                    m_new = jnp.maximum(m, jnp.max(s, axis=1, keepdims=True))
                    alpha = jnp.exp(m - m_new)
                    p = jnp.exp(s - m_new)
                    new += [m_new, alpha * l + jnp.sum(p, axis=1, keepdims=True)]
                    alphas.append(alpha)
                    pv.append(_dot(p, v_mxu[h][pl.ds(start, tk), :], NN))
                acc = acc * jnp.where(masks[0], alphas[0], alphas[1]) + pv[0] + pv[1]
                return new[0], new[1], new[2], new[3], acc
            return step

        carry = (m_init[0], l_init[0], m_init[1], l_init[1], jnp.zeros((tq, LANES), F32))
        last = (i * tq + tq - 1) // tk
        if window is None:
            n_full = (i * tq + 1) // tk
            carry = lax.fori_loop(0, n_full, make_step(False), carry)
            carry = lax.fori_loop(n_full, last + 1, make_step(True), carry)
        else:
            carry = lax.fori_loop(jnp.maximum(i * tq - (window - 1), 0) // tk, last + 1, make_step(True), carry)
        m0, l0, m1, l1, acc = carry
        o_ref[...] = acc / jnp.where(masks[0], l0, l1)
        lse_ref[...] = jnp.where(masks[0], m0 + jnp.log(l0), m1 + jnp.log(l1))

    in_specs = [pl.BlockSpec((tq, qw), lambda p, i: (i, p)), pl.BlockSpec((t, qw), lambda p, i: (0, p)),
                pl.BlockSpec((t, LANES), lambda p, i: (0, p))]
    args = [q, k, v]
    if has_sink:
        in_specs.append(pl.BlockSpec((1, LANES), lambda p, i: (0, p)))
        args.append(sink)
    blk_o = pl.BlockSpec((tq, LANES), lambda p, i: (i, p))
    return pl.pallas_call(
        body, name=name, grid=(pairs, t // tq), in_specs=in_specs, out_specs=[blk_o, blk_o],
        out_shape=[jax.ShapeDtypeStruct((t, pairs * LANES), F32)] * 2,
        scratch_shapes=[pltpu.VMEM((t, qw), MXU_DTYPE), pltpu.VMEM((t, LANES), MXU_DTYPE), pltpu.VMEM((t, LANES), MXU_DTYPE)],
        compiler_params=_params(("arbitrary", "arbitrary")),
    )(*args)


def _flash_bwd_call(q, k, v, sink, o, lse, do, *, window, shared_k, scale, blk, blk_q, name):
    t = q.shape[0]
    qw = LANES if shared_k else 2 * LANES
    pairs = v.shape[1] // LANES
    tk = min(blk, t)
    tq = min(blk_q, t)
    assert tk % tq == 0 or tq % tk == 0
    nq = t // tq
    has_sink = sink is not None

    def body(*refs):
        if has_sink:
            q_ref, k_ref, v_ref, o_ref, lse_ref, do_ref, s_ref, dq_ref, dk_ref, dv_ref, ds_ref = refs[:11]
        else:
            q_ref, k_ref, v_ref, o_ref, lse_ref, do_ref, dq_ref, dk_ref, dv_ref = refs[:9]
        q_mxu, do_mxu, lse_h, dsum_h = refs[-8:-6], refs[-6:-4], refs[-4:-2], refs[-2:]
        j = pl.program_id(1)
        masks = _head_masks(tq)
        row_masks = _head_masks(1)

        @pl.when(j == 0)
        def _():
            dq_ref[...] = jnp.zeros_like(dq_ref)
            full_masks = _head_masks(t)
            prod = do_ref[...] * o_ref[...]
            parts = []
            for h in range(2):
                qh = jnp.where(full_masks[h], q_ref[...], 0.0) if shared_k else q_ref[:, h * LANES:(h + 1) * LANES]
                q_mxu[h][...] = qh.astype(MXU_DTYPE)
                do_mxu[h][...] = jnp.where(full_masks[h], do_ref[...], 0.0).astype(MXU_DTYPE)
                dsum = jnp.sum(jnp.where(full_masks[h], prod, 0.0), axis=1, keepdims=True)
                lse = _head_row(lse_ref[...], full_masks[h])
                dsum_h[h][...] = jnp.zeros((t, LANES), F32) + dsum
                lse_h[h][...] = jnp.zeros((t, LANES), F32) + lse
                if has_sink:
                    ps = jnp.exp(_head_row(s_ref[...], row_masks[h]) - lse)
                    parts.append(-jnp.sum(ps * dsum, axis=0, keepdims=True))
            if has_sink:
                ds_ref[...] = jnp.zeros((SUBLANES, LANES), F32) + jnp.where(row_masks[0], parts[0], parts[1])

        kb = k_ref[...].astype(MXU_DTYPE)
        vb = v_ref[...].astype(MXU_DTYPE)
        kh = [kb if shared_k else kb[:, h * LANES:(h + 1) * LANES] for h in range(2)]
        kpos = j * tk + lax.broadcasted_iota(jnp.int32, (tq, tk), 1)
        lanes_of = lambda a: a if tk == LANES else jnp.concatenate([a] * (tk // LANES), axis=1)

        def make_step(masked):
            def step(i, carry):
                dk0, dk1, dv = carry
                start = pl.multiple_of(i * tq, tq)
                if masked:
                    valid = _attn_valid(i * tq + lax.broadcasted_iota(jnp.int32, (tq, tk), 0), kpos, window)
                dks, dqs = [], []
                for h in range(2):
                    qh = q_mxu[h][pl.ds(start, tq), :]
                    doh = do_mxu[h][pl.ds(start, tq), :]
                    s = _dot(qh, kh[h], NT) * scale
                    if masked:
                        s = jnp.where(valid, s, NEG)
                    p = jnp.exp(s - lanes_of(lse_h[h][pl.ds(start, tq), :]))
                    dp = _dot(doh, vb, NT)
                    dsc = p * (dp - lanes_of(dsum_h[h][pl.ds(start, tq), :])) * scale
                    dv = dv + _dot(p, doh, TN)
                    dks.append(_dot(dsc, qh, TN))
                    dq_h = _dot(dsc, kh[h], NN)
                    dqs.append(jnp.where(masks[h], dq_h, 0.0) if shared_k else dq_h)
                if shared_k:
                    dq_ref[pl.ds(start, tq), :] += dqs[0] + dqs[1]
                else:
                    dq_ref[pl.ds(start, tq), :] += jnp.concatenate(dqs, axis=1)
                return dk0 + dks[0], dk1 + dks[1], dv
            return step

        zero = jnp.zeros((tk, LANES), F32)
        carry = (zero, zero, zero)
        first = (j * tk) // tq
        if window is None:
            n_full = jnp.minimum(((j + 1) * tk + tq - 2) // tq, nq)
            carry = lax.fori_loop(first, n_full, make_step(True), carry)
            carry = lax.fori_loop(n_full, nq, make_step(False), carry)
        else:
            carry = lax.fori_loop(first, jnp.minimum(nq, (j * tk + tk - 1 + window - 1) // tq + 1), make_step(True), carry)
        dk0, dk1, dv = carry
        dk_ref[...] = dk0 + dk1 if shared_k else jnp.concatenate([dk0, dk1], axis=1)
        dv_ref[...] = dv

    full = lambda w: pl.BlockSpec((t, w), lambda p, j: (0, p))
    blkspec = lambda w: pl.BlockSpec((tk, w), lambda p, j: (j, p))
    in_specs = [full(qw), blkspec(qw), blkspec(LANES), full(LANES), full(LANES), full(LANES)]
    args = [q, k, v, o, lse, do]
    out_specs = [full(qw), blkspec(qw), blkspec(LANES)]
    out_shape = [jax.ShapeDtypeStruct(q.shape, F32), jax.ShapeDtypeStruct(k.shape, F32), jax.ShapeDtypeStruct(v.shape, F32)]
    if has_sink:
        in_specs.append(pl.BlockSpec((1, LANES), lambda p, j: (0, p)))
        args.append(sink)
        out_specs.append(pl.BlockSpec((SUBLANES, LANES), lambda p, j: (0, p)))
        out_shape.append(jax.ShapeDtypeStruct((SUBLANES, pairs * LANES), F32))
    return pl.pallas_call(
        body, name=name, grid=(pairs, t // tk), in_specs=in_specs, out_specs=out_specs, out_shape=out_shape,
        scratch_shapes=[pltpu.VMEM((t, LANES), MXU_DTYPE)] * 4 + [pltpu.VMEM((t, LANES), F32)] * 4,
        compiler_params=_params(("arbitrary", "arbitrary")),
    )(*args)


_MLA_CFG = dict(window=None, shared_k=False, scale=MLA_SCALE, blk=256)
_SWA_CFG = dict(window=WINDOW, shared_k=True, scale=SWA_SCALE, blk=128)
_MLA_FWD_CFG = dict(_MLA_CFG, blk=512, blk_q=256)
_SWA_FWD_CFG = dict(_SWA_CFG, blk=256, blk_q=256)
_MLA_BWD_CFG = dict(_MLA_CFG, blk=256, blk_q=512)
_SWA_BWD_CFG = dict(_SWA_CFG, blk=256, blk_q=256)


@jax.custom_vjp
def op_mla_attn(q, k, v):
    return _flash_fwd_call(q, k, v, None, name="mla_fwd", **_MLA_FWD_CFG)[0]


def _op_mla_attn_fwd(q, k, v):
    o, lse = _flash_fwd_call(q, k, v, None, name="mla_fwd", **_MLA_FWD_CFG)
    return o, (q, k, v, o, lse)


def _op_mla_attn_bwd(res, do):
    q, k, v, o, lse = res
    return tuple(_flash_bwd_call(q, k, v, None, o, lse, do, name="mla_bwd", **_MLA_BWD_CFG))


op_mla_attn.defvjp(_op_mla_attn_fwd, _op_mla_attn_bwd)


@jax.custom_vjp
def op_swa_attn(q, k, v, sink):
    return _flash_fwd_call(q, k, v, sink, name="swa_fwd", **_SWA_FWD_CFG)[0]


def _op_swa_attn_fwd(q, k, v, sink):
    o, lse = _flash_fwd_call(q, k, v, sink, name="swa_fwd", **_SWA_FWD_CFG)
    return o, (q, k, v, sink, o, lse)


def _op_swa_attn_bwd(res, do):
    q, k, v, sink, o, lse = res
    dq, dk, dv, dsink = _flash_bwd_call(q, k, v, sink, o, lse, do, name="swa_bwd", **_SWA_BWD_CFG)
    first_lane = lax.broadcasted_iota(jnp.int32, (1, dsink.shape[1]), 1) % 64 == 0
    return dq, dk, dv, jnp.where(first_lane, dsink[:1], 0.0)


op_swa_attn.defvjp(_op_swa_attn_fwd, _op_swa_attn_bwd)


def _complex_power(ar, ai, n):
    for _ in range(int(math.log2(n))):
        ar, ai = ar * ar - ai * ai, 2.0 * ar * ai
    return ar, ai


def _scan_passes(load_b, a1r, a1i, n, store, e_ref, c_ref, reverse):
    cb = a1r.shape[1]
    ar = jnp.zeros((SCAN_SEGMENTS, cb), F32) + a1r
    ai = jnp.zeros((SCAN_SEGMENTS, cb), F32) + a1i
    idx = (lambda ii: n - 1 - ii) if reverse else (lambda ii: ii)

    def local(ii, h):
        br, bi = load_b(idx(ii))
        return ar * h[0] - ai * h[1] + br, ar * h[1] + ai * h[0] + bi

    zero = jnp.zeros((SCAN_SEGMENTS, cb), F32)
    er, ei = lax.fori_loop(0, n, local, (zero, zero), unroll=SCAN_UNROLL)
    e_ref[:, 0:cb] = er
    e_ref[:, cb:] = ei
    pr, pi_ = _complex_power(a1r, a1i, n)
    cr = jnp.zeros((1, cb), F32)
    ci = jnp.zeros((1, cb), F32)
    order = range(SCAN_SEGMENTS - 1, -1, -1) if reverse else range(SCAN_SEGMENTS)
    for s in order:
        c_ref[s:s + 1, 0:cb] = cr
        c_ref[s:s + 1, cb:] = ci
        er1, ei1 = e_ref[s:s + 1, 0:cb], e_ref[s:s + 1, cb:]
        cr, ci = pr * cr - pi_ * ci + er1, pr * ci + pi_ * cr + ei1

    def second(ii, h):
        i = idx(ii)
        hr, hi = local(ii, h)
        store(i, hr, hi)
        return hr, hi

    lax.fori_loop(0, n, second, (c_ref[:, 0:cb], c_ref[:, cb:]), unroll=SCAN_UNROLL)


def _scan_fwd_call(bu, lam):
    n = bu.shape[0]
    cb = SCAN_CB
    blk3 = pl.BlockSpec((n, SCAN_SEGMENTS, 2 * cb), lambda c: (0, 0, c))
    blk2 = lambda r: pl.BlockSpec((r, 2 * cb), lambda c: (0, c))

    def body(b_ref, lam_ref, h_ref, cin_ref, e_ref):
        def store(i, hr, hi):
            h_ref[i, :, 0:cb] = hr
            h_ref[i, :, cb:] = hi

        _scan_passes(lambda i: (b_ref[i, :, 0:cb], b_ref[i, :, cb:]), lam_ref[:, 0:cb], lam_ref[:, cb:], n, store,
                     e_ref, cin_ref, False)

    return pl.pallas_call(
        body, name="scan_fwd", grid=(SSM_CH // cb,), in_specs=[blk3, blk2(1)], out_specs=[blk3, blk2(SCAN_SEGMENTS)],
        out_shape=[jax.ShapeDtypeStruct(bu.shape, F32), jax.ShapeDtypeStruct((SCAN_SEGMENTS, 2 * SSM_CH), F32)],
        scratch_shapes=[pltpu.VMEM((SCAN_SEGMENTS, 2 * cb), F32)],
        compiler_params=_params(("parallel",)),
    )(bu, lam)


def _scan_bwd_call(dh, h, cin, lam):
    n = dh.shape[0]
    cb = SCAN_CB
    blk3 = pl.BlockSpec((n, SCAN_SEGMENTS, 2 * cb), lambda c: (0, 0, c))
    blk2 = lambda r: pl.BlockSpec((r, 2 * cb), lambda c: (0, c))

    def body(d_ref, h_ref, cin_ref, lam_ref, g_ref, dlam_ref, e_ref, c_ref, acc_ref):
        acc_ref[...] = jnp.zeros_like(acc_ref)

        def store(i, gr, gi):
            g_ref[i, :, 0:cb] = gr
            g_ref[i, :, cb:] = gi
            ip = jnp.maximum(i - 1, 0)
            hpr = jnp.where(i > 0, h_ref[ip, :, 0:cb], cin_ref[:, 0:cb])
            hpi = jnp.where(i > 0, h_ref[ip, :, cb:], cin_ref[:, cb:])
            acc_ref[:, 0:cb] += gr * hpr + gi * hpi
            acc_ref[:, cb:] += gi * hpr - gr * hpi

        _scan_passes(lambda i: (d_ref[i, :, 0:cb], d_ref[i, :, cb:]), lam_ref[:, 0:cb], -lam_ref[:, cb:], n, store,
                     e_ref, c_ref, True)
        dlam_ref[...] = acc_ref[...]

    return pl.pallas_call(
        body, name="scan_bwd", grid=(SSM_CH // cb,), in_specs=[blk3, blk3, blk2(SCAN_SEGMENTS), blk2(1)],
        out_specs=[blk3, blk2(SCAN_SEGMENTS)],
        out_shape=[jax.ShapeDtypeStruct(dh.shape, F32), jax.ShapeDtypeStruct((SCAN_SEGMENTS, 2 * SSM_CH), F32)],
        scratch_shapes=[pltpu.VMEM((SCAN_SEGMENTS, 2 * cb), F32)] * 3,
        compiler_params=_params(("parallel",)),
    )(dh, h, cin, lam)


@jax.custom_vjp
def op_scan(bu, lam):
    return _scan_fwd_call(bu, lam)[0]


def _op_scan_fwd(bu, lam):
    h, cin = _scan_fwd_call(bu, lam)
    return h, (h, cin, lam)


def _op_scan_bwd(res, dh):
    h, cin, lam = res
    g, dlam = _scan_bwd_call(dh, h, cin, lam)
    return g, jnp.sum(dlam, axis=0, keepdims=True)


op_scan.defvjp(_op_scan_fwd, _op_scan_bwd)


def _loss_call(y, target):
    t, d = y.shape
    tile = min(ROW_TILE, t)

    def body(y_ref, t_ref, dy_ref, acc_ref):
        @pl.when(pl.program_id(0) == 0)
        def _():
            acc_ref[...] = jnp.zeros_like(acc_ref)

        err = y_ref[...] - t_ref[...]
        dy_ref[...] = err * (1.0 / d)
        col = jnp.sum(err * err, axis=0, keepdims=True)
        part = col[:, 0:LANES]
        for c in range(1, d // LANES):
            part = part + col[:, c * LANES:(c + 1) * LANES]
        acc_ref[0:1, :] += part

    blk = pl.BlockSpec((tile, d), lambda i: (i, 0))
    dy, acc = pl.pallas_call(
        body, name="loss_head", grid=(t // tile,), in_specs=[blk, blk],
        out_specs=[blk, pl.BlockSpec((SUBLANES, LANES), lambda i: (0, 0))],
        out_shape=[jax.ShapeDtypeStruct((t, d), F32), jax.ShapeDtypeStruct((SUBLANES, LANES), F32)],
        compiler_params=_params(("arbitrary",)),
    )(y, target)
    return jnp.sum(acc) * (0.5 / d), dy


def _rot_cols(w, xp=jnp):
    return xp.concatenate([-w[:, 16:], w[:, :16]], axis=1)


def _ext_w_in(w, xp=jnp):
    a_val, a_gate, a_z, c_q, c_kv, k_r, b_z, u, c_z, q, k, v, d_z = xp.split(
        w, (256, 512, 768, 1024, 1152, 1184, 1440, 1696, 1952, 2208, 2336, 2464), axis=1)
    dup = lambda m: xp.concatenate([m[:, :64], m[:, :64], m[:, 64:], m[:, 64:]], axis=1)
    krblk = xp.concatenate([xp.zeros((w.shape[0], 64), w.dtype), k_r, _rot_cols(k_r, xp)], axis=1)
    return xp.concatenate([a_val, a_gate, a_z, c_q, b_z, u, c_z, q, dup(k), dup(v), d_z, c_kv, krblk], axis=1)


IN_WIDTH = 2720
IN_SHARD = IN_WIDTH // 4
IN_SHARD_PAD = 768
IN_EXT = 3072


BAND = 256


def _w_in_layout():
    src = _ext_w_in(np.arange(1, IN_WIDTH + 1, dtype=np.float32)[None, :], np)[0]
    col = np.abs(src).astype(np.int64) - 1
    row = np.where(col >= 0, (col // IN_SHARD) * IN_SHARD_PAD + col % IN_SHARD, -1)
    return row, np.sign(src)


def _w_in_layout_matrix():
    row, sign = _w_in_layout()
    rows = lax.broadcasted_iota(jnp.int32, (4 * IN_SHARD_PAD, IN_EXT), 0)
    return jnp.where(rows == jnp.asarray(row, jnp.int32)[None, :], jnp.asarray(sign, F32)[None, :], 0.0).astype(MXU_DTYPE)


def _band_tables():
    row, _ = _w_in_layout()
    nb = IN_EXT // BAND
    hit = np.zeros((nb, nb), bool)
    for c, r in enumerate(row):
        if r >= 0:
            hit[r // BAND, c // BAND] = True

    def table(h):
        depth = int(h.sum(axis=1).max())
        rows = []
        for o in range(nb):
            used = [int(b) for b in np.nonzero(h[o])[0]]
            spare = [b for b in range(nb) if not h[o, b]]
            rows.append(used + spare[:depth - len(used)])
        return np.asarray(rows, np.int32), depth

    return table(hit.T), table(hit)


def _band_mm_call(a, e, table, depth, e_transposed, name, out_dtype):
    m = a.shape[0]
    nb = IN_EXT // BAND
    dims = NT if e_transposed else NN

    def body(t_ref, a_ref, e_ref, o_ref, acc_ref):
        kk = pl.program_id(1)

        @pl.when(kk == 0)
        def _():
            acc_ref[...] = jnp.zeros_like(acc_ref)

        acc_ref[...] += _dot(a_ref[...], e_ref[...], dims)

        @pl.when(kk == depth - 1)
        def _():
            o_ref[...] = acc_ref[...].astype(out_dtype)

    blk = lambda o, kk, t: t[o * depth + kk]
    e_spec = pl.BlockSpec((BAND, BAND), (lambda o, kk, t: (o, blk(o, kk, t))) if e_transposed else (lambda o, kk, t: (blk(o, kk, t), o)))
    return pl.pallas_call(
        body, name=name, out_shape=jax.ShapeDtypeStruct((m, IN_EXT), out_dtype),
        grid_spec=pltpu.PrefetchScalarGridSpec(
            num_scalar_prefetch=1, grid=(nb, depth),
            in_specs=[pl.BlockSpec((m, BAND), lambda o, kk, t: (0, blk(o, kk, t))), e_spec],
            out_specs=pl.BlockSpec((m, BAND), lambda o, kk, t: (0, o)),
            scratch_shapes=[pltpu.VMEM((m, BAND), F32)]),
        compiler_params=_params(("parallel", "arbitrary")),
    )(jnp.asarray(table.reshape(-1)), a, e)


@jax.custom_vjp
def op_w_in_ext(w_pad, e):
    (table, depth), _ = _band_tables()
    return _band_mm_call(w_pad, e, table, depth, False, "w_in_ext", F32)


def _op_w_in_ext_fwd(w_pad, e):
    return op_w_in_ext(w_pad, e), (e, jnp.zeros((0,), w_pad.dtype))


def _op_w_in_ext_bwd(res, g):
    e, w_like = res
    _, (table, depth) = _band_tables()
    return _band_mm_call(g, e, table, depth, True, "w_in_ext_bwd", w_like.dtype), jnp.zeros_like(e)


op_w_in_ext.defvjp(_op_w_in_ext_fwd, _op_w_in_ext_bwd)


H_COLS = dict(a_val=256, a_gate=256, a_z=256, c_q=256, b_z=256, u=256, c_z=256, q=256, kdup=256, vdup=256, d_z=256,
              c_kv=128, krblk=128)
op_in_proj = make_proj(tuple(H_COLS.values()), "in_proj")


def _ext_mla(w_uq, w_ukv):
    zeros = jnp.zeros((w_ukv.shape[0], 64), w_ukv.dtype)
    uq, uk, uv = [], [], []
    for h in range(4):
        nope, rp = w_uq[:, 96 * h:96 * h + 64], w_uq[:, 96 * h + 64:96 * h + 96]
        uq += [nope, rp, _rot_cols(rp)]
        uk += [w_ukv[:, 128 * h:128 * h + 64], zeros]
        uv.append(w_ukv[:, 128 * h + 64:128 * h + 128])
    return jnp.concatenate(uq, axis=1), jnp.concatenate(uk, axis=1), jnp.concatenate(uv, axis=1)


def _scan_cols(re, im):
    parts = []
    for c in range(SSM_CH // SCAN_CB):
        parts += [re[..., c * SCAN_CB:(c + 1) * SCAN_CB], im[..., c * SCAN_CB:(c + 1) * SCAN_CB]]
    return jnp.concatenate(parts, axis=-1)


def _ext_ssm(a_re, a_im, log_dt, b_re, b_im, c_re, c_im):
    dt = jnp.exp(log_dt)[:, None]
    mag = jnp.exp(a_re * dt)
    lb_re, lb_im = mag * jnp.cos(a_im * dt), mag * jnp.sin(a_im * dt)
    den = a_re * a_re + a_im * a_im
    nr, ni = lb_re - 1.0, lb_im
    f_re = ((nr * a_re + ni * a_im) / den)[..., None]
    f_im = ((ni * a_re - nr * a_im) / den)[..., None]
    bb_re = f_re * b_re - f_im * b_im
    bb_im = f_re * b_im + f_im * b_re
    eye = jnp.eye(SSM_GROUPS, dtype=F32)
    spread = lambda a: a.transpose(0, 2, 1)[:, :, None, :] * eye[:, None, :, None]
    bd_in = lambda bb: spread(bb).reshape(SSM_GROUPS * SSM_GROUP, SSM_CH)
    bd_out = lambda cc: spread(cc).reshape(SSM_CH, SSM_GROUPS * SSM_GROUP)
    w_bu = _scan_cols(bd_in(bb_re), bd_in(bb_im))
    w_y = _scan_cols(bd_out(c_re).T, -bd_out(c_im).T).T
    lam = _scan_cols(lb_re.reshape(1, SSM_CH), lb_im.reshape(1, SSM_CH))
    return w_bu, w_y, lam


def _rope_tables(t):
    pos = jnp.arange(t, dtype=F32)
    inv_freq = ROPE_THETA ** (-jnp.arange(0, 32, 2, dtype=F32) / 32)
    ang = pos[:, None] * inv_freq[None, :]
    cos, sin = jnp.cos(ang), jnp.sin(ang)
    ones, z32, z64 = jnp.ones((t, 64), F32), jnp.zeros((t, 32), F32), jnp.zeros((t, 64), F32)
    cos1 = jnp.concatenate([ones, cos, cos, z32], axis=1)
    sin1 = jnp.concatenate([z64, sin, sin, z32], axis=1)
    return jnp.concatenate([cos1] * 4, axis=1), jnp.concatenate([sin1] * 4, axis=1)


def _to_segments(a):
    t, w = a.shape
    return a.reshape(SCAN_SEGMENTS, t // SCAN_SEGMENTS, w).transpose(1, 0, 2)


def _from_segments(a):
    n, s, w = a.shape
    return a.transpose(1, 0, 2).reshape(n * s, w)


def _layer(x, p_i, cos4, sin4, e_mat, w):
    t = x.shape[0]
    row = lambda v: v.reshape(1, -1)
    f32 = lambda v: v.astype(F32)
    hs = dict(zip(H_COLS, op_in_proj(x, op_w_in_ext(w["w_in_pad"], e_mat))))

    cv = op_conv(hs["a_val"], hs["a_gate"], w["conv_w"], row(w["conv_b"]))
    (y_a,) = op_conv_post((cv, hs["a_z"]), (row(w["conv_norm_g"]), row(w["conv_norm_b"]), f32(w["w_pw2"])))

    w_uq, w_uk, w_uv = _ext_mla(w["w_uq"], f32(w["w_ukv"]))
    q, k, v = op_mla_prep((hs["c_q"], hs["c_kv"], hs["krblk"], cos4, sin4),
                          (row(w["mla_q_norm_g"]), row(w["mla_kv_norm_g"]), w_uq, w_uk, w_uv))
    (y_b,) = op_gate((op_mla_attn(q, k, v), hs["b_z"]), ())

    w_bu, w_y, lam = _ext_ssm(w["ssm_a_re"], w["ssm_a_im"], w["ssm_log_dt"], w["ssm_b_re"], w["ssm_b_im"],
                              w["ssm_c_re"], w["ssm_c_im"])
    u_seg = _to_segments(hs["u"]).reshape(t, BRANCH_W)
    bu = op_mm(u_seg, w_bu).reshape(t // SCAN_SEGMENTS, SCAN_SEGMENTS, 2 * SSM_CH)
    hstate = op_scan(bu, lam).reshape(t, 2 * SSM_CH)
    y_ssm = _from_segments(op_mm(hstate, w_y).reshape(t // SCAN_SEGMENTS, SCAN_SEGMENTS, BRANCH_W))
    w_glu = f32(w["w_glu"])
    (y_c,) = op_ssm_post((y_ssm, hs["u"], hs["c_z"]), (row(w["ssm_d"]), w_glu[:, :BRANCH_W], w_glu[:, BRANCH_W:]))

    sink = jnp.repeat(w["attn_sinks"], 64).reshape(1, 2 * LANES)
    (y_d,) = op_gate((op_swa_attn(hs["q"], hs["kdup"], hs["vdup"], sink), hs["d_z"]), ())

    merged = op_merge_block(x, (y_a, y_b, y_c, y_d), w["w_merge"], w["w_branch"], w["b_merge"])
    (x1,) = op_ln((x, op_mm(merged, w["w_out"])), (row(w["ln_g"]), row(w["ln_b"])))
    (out,) = op_ple((x1, op_mm(p_i, w["w_ple"]), op_mm(x1, w["w_ple_gate"])), (row(w["ple_norm_g"]),))
    return out


def _forward(x, p, layers):
    cos4, sin4 = _rope_tables(x.shape[0])
    e_mat = _w_in_layout_matrix()
    for i in range(DEPTH):
        x = _layer(x, p[i], cos4, sin4, e_mat, layers[i])
    return x


SHARD_AXIS = dict(w_in=2, w_merge=2, conv_w=2, w_pw2=1, w_uq=2, w_ukv=2, w_glu=2, w_branch=3, w_out=1, w_ple=2, w_ple_gate=1)
ODD = ("w_uq", "conv_w")
BIG = tuple(n for n in SHARD_AXIS if n not in ODD)
REPLICATED = ("b_merge", "conv_b", "conv_norm_g", "conv_norm_b", "mla_q_norm_g", "mla_kv_norm_g", "ssm_a_re", "ssm_a_im",
              "ssm_log_dt", "ssm_b_re", "ssm_b_im", "ssm_c_re", "ssm_c_im", "ssm_d", "attn_sinks", "ln_g", "ln_b", "ple_norm_g")
WEIGHTS = ("w_in", "w_merge", "b_merge", "conv_w", "conv_b", "conv_norm_g", "conv_norm_b", "w_pw2", "mla_q_norm_g",
           "mla_kv_norm_g", "w_uq", "w_ukv", "ssm_a_re", "ssm_a_im", "ssm_log_dt", "ssm_b_re", "ssm_b_im", "ssm_c_re",
           "ssm_c_im", "ssm_d", "w_glu", "attn_sinks", "w_branch", "w_out", "ln_g", "ln_b", "w_ple", "w_ple_gate", "ple_norm_g")
PACK_COLS = 1024
PACK_ROWS = 16
CHIP_FLIPS = ((1, 0), (0, 1), (1, 1))
N_CHIPS = 4
N_DEV = 8


def _pack_rows(n):
    return -(-n // (SUBLANES * PACK_COLS)) * SUBLANES


def _pack(arrays, dtype):
    blocks, rows = [], 0
    for a in arrays:
        r = _pack_rows(a.size)
        flat = a.reshape(-1).astype(dtype)
        blocks.append(jnp.pad(flat, (0, r * PACK_COLS - a.size)).reshape(r, PACK_COLS))
        rows += r
    pad = -rows % PACK_ROWS
    if pad:
        blocks.append(jnp.zeros((pad, PACK_COLS), dtype))
    return jnp.concatenate(blocks, axis=0)


def _unpack(buf, shapes):
    out, row = [], 0
    for s in shapes:
        n = math.prod(s)
        r = _pack_rows(n)
        out.append(buf[row:row + r].reshape(-1)[:n].reshape(s))
        row += r
    return out


def _flip(v, bit):
    return 1 - v if bit else v


def _window(ref, axis, start, size):
    idx = [slice(None)] * len(ref.shape)
    idx[axis] = pl.ds(start, size)
    return ref.at[tuple(idx)]


def _gather_chips(srcs, axes, stacked):
    units = []
    for k, (s, a) in enumerate(zip(srcs, axes)):
        if stacked[k]:
            units += [(k, l, s.shape[1:], a - 1) for l in range(s.shape[0])]
        else:
            units.append((k, None, s.shape, a))
    nu, nb = len(units), len(srcs)

    def body(*refs):
        ins, outs = refs[:nb], refs[nb:nb + nu]
        ici_send, ici_recv, d2d_send, d2d_recv, local_sems = refs[nb + nu:]
        x, y, c = lax.axis_index("x"), lax.axis_index("y"), lax.axis_index("c")
        me = 2 * x + y

        def mine(u, half=None):
            k, l, shape, _ = units[u]
            ref = ins[k] if l is None else ins[k].at[l]
            return ref if half is None else ref.at[pl.ds(half * (shape[0] // 2), shape[0] // 2)]

        def place(u, chip, half=None):
            _, _, shape, a = units[u]
            size, rows = shape[a], shape[0] // 2
            if half is None:
                return _window(outs[u], a, chip * size, size)
            if a == 0:
                return outs[u].at[pl.ds(chip * size + half * rows, rows)]
            return _window(outs[u].at[pl.ds(half * rows, rows)], a, chip * size, size)

        local = [pltpu.make_async_copy(mine(u), place(u, me), local_sems.at[u]) for u in range(nu)]
        for cp in local:
            cp.start()
        sends = []
        for j, (bx, by) in enumerate(CHIP_FLIPS):
            for u in range(nu):
                cp = pltpu.make_async_remote_copy(src_ref=mine(u, c), dst_ref=place(u, me, c),
                                                  send_sem=ici_send.at[j * nu + u], recv_sem=ici_recv.at[j * nu + u],
                                                  device_id=(_flip(x, bx), _flip(y, by), c), device_id_type=MESH)
                cp.start()
                sends.append(cp)
        for j, (bx, by) in enumerate(CHIP_FLIPS):
            src = 2 * _flip(x, bx) + _flip(y, by)
            for u in range(nu):
                got = place(u, src, c)
                pltpu.make_async_remote_copy(src_ref=got, dst_ref=got, send_sem=ici_send.at[j * nu + u],
                                             recv_sem=ici_recv.at[j * nu + u], device_id=(x, y, c), device_id_type=MESH).wait_recv()
                cp = pltpu.make_async_remote_copy(src_ref=got, dst_ref=got, send_sem=d2d_send.at[j * nu + u],
                                                  recv_sem=d2d_recv.at[j * nu + u], device_id=(x, y, 1 - c), device_id_type=MESH)
                cp.start()
                sends.append(cp)
        for j, (bx, by) in enumerate(CHIP_FLIPS):
            src = 2 * _flip(x, bx) + _flip(y, by)
            for u in range(nu):
                other = place(u, src, 1 - c)
                pltpu.make_async_remote_copy(src_ref=other, dst_ref=other, send_sem=d2d_send.at[j * nu + u],
                                             recv_sem=d2d_recv.at[j * nu + u], device_id=(x, y, c), device_id_type=MESH).wait_recv()
        for cp in sends:
            cp.wait_send()
        for cp in local:
            cp.wait()

    full = lambda shape, a: tuple(N_CHIPS * d if i == a else d for i, d in enumerate(shape))
    res = pl.pallas_call(
        body, name="gather_weights", in_specs=[ANY] * nb, out_specs=[ANY] * nu,
        out_shape=[jax.ShapeDtypeStruct(full(shape, a), srcs[k].dtype) for k, _, shape, a in units],
        scratch_shapes=[pltpu.SemaphoreType.DMA((3 * nu,))] * 4 + [pltpu.SemaphoreType.DMA((nu,))],
    )(*srcs)
    out, it = [], iter(res)
    for k in range(nb):
        out.append([next(it) for _ in range(srcs[k].shape[0])] if stacked[k] else next(it))
    return out


def _exchange_grads(grads, axes, smalls):
    nt, ns = len(grads), len(smalls)
    sizes = [g[0].shape[a] // N_CHIPS for g, a in zip(grads, axes)]
    dev_flips = [(bx, by, bc) for bx in (0, 1) for by in (0, 1) for bc in (0, 1)][1:]
    n_remote = 3 * nt * DEPTH + 7 * ns
    n_local = nt * DEPTH + ns

    def body(*refs):
        g_refs = [refs[k * DEPTH:(k + 1) * DEPTH] for k in range(nt)]
        s_refs = refs[nt * DEPTH:nt * DEPTH + ns]
        outs = refs[nt * DEPTH + ns:nt * DEPTH + ns + nt + ns]
        recv_refs, all_refs = outs[:nt], outs[nt:]
        send_sems, recv_sems, local_sems = refs[-3:]
        x, y, c = lax.axis_index("x"), lax.axis_index("y"), lax.axis_index("c")
        me_chip = 2 * x + y
        me = 4 * x + 2 * y + c
        part = lambda k, i, chip: _window(g_refs[k][i], axes[k], chip * sizes[k], sizes[k])
        started, waits = [], []
        sem, lsem = 0, 0
        for k in range(nt):
            for i in range(DEPTH):
                cp = pltpu.make_async_copy(part(k, i, me_chip), recv_refs[k].at[i, 3], local_sems.at[lsem])
                cp.start()
                started.append(cp.wait)
                lsem += 1
                for j, (bx, by) in enumerate(CHIP_FLIPS):
                    px, py = _flip(x, bx), _flip(y, by)
                    cp = pltpu.make_async_remote_copy(src_ref=part(k, i, 2 * px + py), dst_ref=recv_refs[k].at[i, j],
                                                      send_sem=send_sems.at[sem], recv_sem=recv_sems.at[sem],
                                                      device_id=(px, py, c), device_id_type=MESH)
                    cp.start()
                    started.append(cp.wait_send)
                    waits.append(cp.wait_recv)
                    sem += 1
        for s in range(ns):
            cp = pltpu.make_async_copy(s_refs[s], all_refs[s].at[me], local_sems.at[lsem])
            cp.start()
            started.append(cp.wait)
            lsem += 1
            for bx, by, bc in dev_flips:
                peer = (_flip(x, bx), _flip(y, by), _flip(c, bc))
                cp = pltpu.make_async_remote_copy(src_ref=s_refs[s], dst_ref=all_refs[s].at[me], send_sem=send_sems.at[sem],
                                                  recv_sem=recv_sems.at[sem], device_id=peer, device_id_type=MESH)
                cp.start()
                started.append(cp.wait_send)
                src = 4 * peer[0] + 2 * peer[1] + peer[2]
                waits.append(pltpu.make_async_remote_copy(src_ref=s_refs[s], dst_ref=all_refs[s].at[src], send_sem=send_sems.at[sem],
                                                          recv_sem=recv_sems.at[sem], device_id=peer, device_id_type=MESH).wait_recv)
                sem += 1
        for w in waits + started:
            w()

    shard = lambda g, a: tuple(d // N_CHIPS if i == a else d for i, d in enumerate(g.shape))
    flat = [g for per_layer in grads for g in per_layer]
    return pl.pallas_call(
        body, name="exchange_grads", in_specs=[ANY] * (len(flat) + ns), out_specs=[ANY] * (nt + ns),
        out_shape=[jax.ShapeDtypeStruct((DEPTH, N_CHIPS, *shard(g[0], a)), g[0].dtype) for g, a in zip(grads, axes)]
        + [jax.ShapeDtypeStruct((N_DEV, *s.shape), s.dtype) for s in smalls],
        scratch_shapes=[pltpu.SemaphoreType.DMA((n_remote,)), pltpu.SemaphoreType.DMA((n_remote,)), pltpu.SemaphoreType.DMA((n_local,))],
    )(*flat, *smalls)


def _swap_cores(parts):
    nb = len(parts)

    def body(*refs):
        ins, outs, send_sems, recv_sems = refs[:nb], refs[nb:2 * nb], refs[-2], refs[-1]
        x, y, c = lax.axis_index("x"), lax.axis_index("y"), lax.axis_index("c")
        cps = [pltpu.make_async_remote_copy(src_ref=ins[k], dst_ref=outs[k], send_sem=send_sems.at[k], recv_sem=recv_sems.at[k],
                                            device_id=(x, y, 1 - c), device_id_type=MESH) for k in range(nb)]
        for cp in cps:
            cp.start()
        for cp in cps:
            cp.wait()

    return pl.pallas_call(
        body, name="swap_cores", in_specs=[ANY] * nb, out_specs=[ANY] * nb,
        out_shape=[jax.ShapeDtypeStruct(q.shape, q.dtype) for q in parts],
        scratch_shapes=[pltpu.SemaphoreType.DMA((nb,)), pltpu.SemaphoreType.DMA((nb,))],
    )(*parts)


def _sum_chips_call(recv, cols, name):
    depth, _, r, c = recv.shape
    tile = _pick(r, (512, 256, 128, 64, 32, 16))

    def body(r_ref, o_ref):
        slot = lambda s: r_ref[s, :, pl.ds(0, cols)].astype(F32)
        o_ref[...] = ((slot(3) + slot(0)) + slot(1)) + slot(2)

    return pl.pallas_call(
        body, name=name, grid=(depth, r // tile),
        in_specs=[pl.BlockSpec((None, N_CHIPS, tile, c), lambda l, i: (l, 0, i, 0))],
        out_specs=pl.BlockSpec((None, tile, cols), lambda l, i: (l, i, 0)), out_shape=jax.ShapeDtypeStruct((depth, r, cols), F32),
        compiler_params=_params(("parallel", "parallel")),
    )(recv)


def _sum_slots_call(slots, name):
    n, r, c = slots.shape
    tile = _pick(r, (512, 256, 128, 64, 32, 16, 8))

    def body(s_ref, o_ref):
        acc = s_ref[0]
        for s in range(1, n):
            acc = acc + s_ref[s]
        o_ref[...] = acc

    return pl.pallas_call(
        body, name=name, grid=(r // tile,), in_specs=[pl.BlockSpec((n, tile, c), lambda i: (0, i, 0))],
        out_specs=pl.BlockSpec((tile, c), lambda i: (i, 0)), out_shape=jax.ShapeDtypeStruct((r, c), F32),
        compiler_params=_params(("parallel",)),
    )(slots)


def _adamw_math(w, g, m, v):
    m = ADAM_B1 * m + (1.0 - ADAM_B1) * g
    v = ADAM_B2 * v + (1.0 - ADAM_B2) * (g * g)
    m_hat = m / (1.0 - ADAM_B1 ** ADAM_STEP)
    v_hat = v / (1.0 - ADAM_B2 ** ADAM_STEP)
    return -ADAM_LR * (m_hat / (jnp.sqrt(v_hat) + ADAM_EPS) + ADAM_WD * w), m, v


def _adamw_call(w, m, v, gparts, name):
    r, c = w.shape
    n = len(gparts)
    tile = _pick(r, (512, 256, 128, 64, 32, 16, 8))

    def body(w_ref, m_ref, v_ref, *refs):
        g_refs, (go_ref, d_ref, mo_ref, vo_ref) = refs[:n], refs[n:]
        g = g_refs[0][...]
        for g_ref in g_refs[1:]:
            g = g + g_ref[...]
        go_ref[...] = g
        d_ref[...], mo_ref[...], vo_ref[...] = _adamw_math(w_ref[...], g, m_ref[...], v_ref[...])

    blk = pl.BlockSpec((tile, c), lambda i: (i, 0))
    return pl.pallas_call(
        body, name=name, grid=(r // tile,), in_specs=[blk] * (3 + n),
        out_specs=[blk] * 4, out_shape=[jax.ShapeDtypeStruct((r, c), F32)] * 4,
        compiler_params=_params(("parallel",)),
    )(w, m, v, *gparts)


def _train_local(x, p, layers, target):
    y, vjp = jax.vjp(lambda x_, w_: _forward(x_, p, w_), x, layers)
    loss, dy = _loss_call(y, target)
    dx, dw = vjp(dy)
    return loss, dx, dw


def kernel(x, p, w_in, w_merge, b_merge, conv_w, conv_b, conv_norm_g, conv_norm_b, w_pw2, mla_q_norm_g, mla_kv_norm_g, w_uq, w_ukv, ssm_a_re, ssm_a_im, ssm_log_dt, ssm_b_re, ssm_b_im, ssm_c_re, ssm_c_im, ssm_d, w_glu, attn_sinks, w_branch, w_out, ln_g, ln_b, w_ple, w_ple_gate, ple_norm_g, loss_target, m_w_in, m_w_merge, m_b_merge, m_conv_w, m_conv_b, m_conv_norm_g, m_conv_norm_b, m_w_pw2, m_mla_q_norm_g, m_mla_kv_norm_g, m_w_uq, m_w_ukv, m_ssm_a_re, m_ssm_a_im, m_ssm_log_dt, m_ssm_b_re, m_ssm_b_im, m_ssm_c_re, m_ssm_c_im, m_ssm_d, m_w_glu, m_attn_sinks, m_w_branch, m_w_out, m_ln_g, m_ln_b, m_w_ple, m_w_ple_gate, m_ple_norm_g, v_w_in, v_w_merge, v_b_merge, v_conv_w, v_conv_b, v_conv_norm_g, v_conv_norm_b, v_w_pw2, v_mla_q_norm_g, v_mla_kv_norm_g, v_w_uq, v_w_ukv, v_ssm_a_re, v_ssm_a_im, v_ssm_log_dt, v_ssm_b_re, v_ssm_b_im, v_ssm_c_re, v_ssm_c_im, v_ssm_d, v_w_glu, v_attn_sinks, v_w_branch, v_w_out, v_ln_g, v_ln_b, v_w_ple, v_w_ple_gate, v_ple_norm_g):
    given = dict(locals())
    w_loc = {n: given[n] for n in WEIGHTS}
    m_loc = {n: given["m_" + n] for n in WEIGHTS}
    v_loc = {n: given["v_" + n] for n in WEIGHTS}

    me_chip = 2 * lax.axis_index("x") + lax.axis_index("y")

    wire = {n: w_loc[n].astype(MXU_DTYPE) for n in BIG}
    wire["w_in"] = jnp.pad(wire["w_in"], ((0, 0), (0, 0), (0, IN_SHARD_PAD - IN_SHARD)))
    odd_shapes = [w_loc[n].shape for n in ODD]
    gathered = _gather_chips([wire[n] for n in BIG] + [_pack([w_loc[n] for n in ODD], F32)], [SHARD_AXIS[n] for n in BIG] + [0],
                             [True] * len(BIG) + [False])
    full = dict(zip(BIG, gathered[:-1]))
    odd_parts = [_unpack(part, odd_shapes) for part in jnp.split(gathered[-1], N_CHIPS, axis=0)]
    for k, n in enumerate(ODD):
        full[n] = jnp.concatenate([odd_parts[s][k] for s in range(N_CHIPS)], axis=SHARD_AXIS[n])
    layers = []
    for i in range(DEPTH):
        layer = {n: (full[n][i] if n in full else w_loc[n][i]) for n in WEIGHTS if n != "w_in"}
        layer["w_in_pad"] = full["w_in"][i]
        layers.append(layer)

    loss, dx, dw = _train_local(x[0], p[:, 0], layers, loss_target[0])
    loss = lax.psum(loss, ("x", "y", "c"))

    key = lambda n: "w_in_pad" if n == "w_in" else n
    stacked = lambda n: jnp.stack([dw[i][n] for i in range(DEPTH)])
    small_rep = _pack([stacked(n) for n in REPLICATED], F32)
    small_odd = _pack([stacked(n) for n in ODD], F32)
    *recv, all_rep, all_odd = _exchange_grads([[dw[i][key(n)] for i in range(DEPTH)] for n in BIG],
                                              [SHARD_AXIS[n] - 1 for n in BIG], [small_rep, small_odd])
    parts = []
    for n, r in zip(BIG, recv):
        cols = w_loc[n].shape[-1]
        parts.append(_sum_chips_call(r.reshape(DEPTH, N_CHIPS, -1, r.shape[-1]), cols, "sum_chips_" + n))
    others = _swap_cores(parts)
    g_rep = _sum_slots_call(all_rep, "sum_replicated")
    g_odd = _unpack(_sum_slots_call(all_odd, "sum_odd"), [(DEPTH, *w_loc[n].shape[1:-1], N_CHIPS * w_loc[n].shape[-1]) for n in ODD])

    grads, deltas, new_m, new_v = {}, {}, {}, {}

    def adamw(n, gparts):
        shape = w_loc[n].shape
        two_d = lambda a: a.reshape(-1, shape[-1])
        res = _adamw_call(two_d(w_loc[n]), two_d(m_loc[n]), two_d(v_loc[n]), [two_d(g) for g in gparts], "adamw_" + n)
        grads[n], deltas[n], new_m[n], new_v[n] = [r.reshape(shape) for r in res]

    for n, part, other in zip(BIG, parts, others):
        adamw(n, [part, other])
    for n, g in zip(ODD, g_odd):
        size = w_loc[n].shape[-1]
        adamw(n, [lax.dynamic_slice_in_dim(g, me_chip * size, size, axis=g.ndim - 1)])
    rep_shapes = [w_loc[n].shape for n in REPLICATED]
    res = _adamw_call(_pack([w_loc[n] for n in REPLICATED], F32), _pack([m_loc[n] for n in REPLICATED], F32),
                      _pack([v_loc[n] for n in REPLICATED], F32), [g_rep], "adamw_replicated")
    for dst, buf in zip((grads, deltas, new_m, new_v), res):
        for n, a in zip(REPLICATED, _unpack(buf, rep_shapes)):
            dst[n] = a

    return (loss, dx[None], *[grads[n] for n in WEIGHTS], *[deltas[n] for n in WEIGHTS],
            *[new_m[n] for n in WEIGHTS], *[new_v[n] for n in WEIGHTS])
```

```python
import functools
import math

import jax
import jax.numpy as jnp
import numpy as np
from jax import lax
from jax.experimental import pallas as pl
from jax.experimental.pallas import tpu as pltpu

F32 = jnp.float32
BF16 = jnp.bfloat16
MXU_DTYPE = BF16
V7X_VMEM_BYTES = 64 * 1024 * 1024
VMEM_LIMIT = V7X_VMEM_BYTES * 3 // 4
LANES = 128
SUBLANES = 8

D_MODEL = 1024
DEPTH = 4
BRANCH_W = 256
CONV_W = 31
CONV_HALO = 32
MLA_SCALE = (64 + 32) ** -0.5
SWA_SCALE = 64 ** -0.5
WINDOW = 128
ROPE_THETA = 10000.0
SSM_GROUPS, SSM_GROUP, SSM_STATE = 16, 16, 64
SSM_CH = SSM_GROUPS * SSM_STATE
SCAN_SEGMENTS = SUBLANES
SCAN_CB = 128
DEEPNORM_ALPHA = (2.0 * DEPTH) ** 0.25
LN_EPS = 1e-5
RMS_EPS = 1e-6
ADAM_LR, ADAM_B1, ADAM_B2, ADAM_EPS, ADAM_WD, ADAM_STEP = 0.001, 0.9, 0.999, 1e-08, 0.01, 10
NEG = -1e30
ROW_TILE = 512

NN = (((1,), (0,)), ((), ()))
NT = (((1,), (1,)), ((), ()))
TN = (((0,), (0,)), ((), ()))

MESH = pl.DeviceIdType.MESH
ANY = pl.BlockSpec(memory_space=pl.ANY)


def _dot(a, b, dims):
    return lax.dot_general(a.astype(MXU_DTYPE), b.astype(MXU_DTYPE), dims, preferred_element_type=F32)


def _pick(n, cands):
    for c in cands:
        if n % c == 0:
            return c
    return n


def _params(sem):
    return pltpu.CompilerParams(dimension_semantics=sem, vmem_limit_bytes=VMEM_LIMIT)


def _col_offsets(widths):
    return [sum(widths[:j]) for j in range(len(widths))]


def _proj_fwd_call(x, wb, widths, name):
    t, k = x.shape
    tm = min(ROW_TILE, t)
    offs = _col_offsets(widths)

    def body(x_ref, w_ref, *o_refs):
        xb = x_ref[...].astype(MXU_DTYPE)
        for o_ref, off, wd in zip(o_refs, offs, widths):
            o_ref[...] = _dot(xb, w_ref[:, off:off + wd], NN)

    return pl.pallas_call(
        body, name=name, grid=(t // tm,),
        in_specs=[pl.BlockSpec((tm, k), lambda i: (i, 0)), pl.BlockSpec(wb.shape, lambda i: (0, 0))],
        out_specs=[pl.BlockSpec((tm, wd), lambda i: (i, 0)) for wd in widths],
        out_shape=[jax.ShapeDtypeStruct((t, wd), F32) for wd in widths],
        compiler_params=_params(("parallel",)),
    )(x, wb)


def _proj_dx_call(douts, wb, widths, name):
    t = douts[0].shape[0]
    k = wb.shape[0]
    tm = min(ROW_TILE, t)
    offs = _col_offsets(widths)

    def body(*refs):
        d_refs, w_ref, o_ref = refs[:-2], refs[-2], refs[-1]
        acc = jnp.zeros((tm, k), F32)
        for d_ref, off, wd in zip(d_refs, offs, widths):
            acc = acc + _dot(d_ref[...], w_ref[:, off:off + wd], NT)
        o_ref[...] = acc

    return pl.pallas_call(
        body, name=name, grid=(t // tm,),
        in_specs=[pl.BlockSpec((tm, wd), lambda i: (i, 0)) for wd in widths] + [pl.BlockSpec(wb.shape, lambda i: (0, 0))],
        out_specs=pl.BlockSpec((tm, k), lambda i: (i, 0)), out_shape=jax.ShapeDtypeStruct((t, k), F32),
        compiler_params=_params(("parallel",)),
    )(*douts, wb)


def _proj_dw_call(x, douts, widths, name, out_dtype):
    t, k = x.shape
    n = sum(widths)
    tk = min(ROW_TILE if k * n <= 2 * 1024 * 1024 else ROW_TILE // 2, t)
    nk = t // tk
    offs = _col_offsets(widths)
    n = sum(widths)

    def body(x_ref, *refs):
        d_refs, o_ref, acc_ref = refs[:-2], refs[-2], refs[-1]

        @pl.when(pl.program_id(0) == 0)
        def _():
            acc_ref[...] = jnp.zeros_like(acc_ref)

        xb = x_ref[...].astype(MXU_DTYPE)
        for d_ref, off, wd in zip(d_refs, offs, widths):
            acc_ref[:, off:off + wd] += _dot(xb, d_ref[...], TN)

        @pl.when(pl.program_id(0) == nk - 1)
        def _():
            o_ref[...] = acc_ref[...].astype(out_dtype)

    return pl.pallas_call(
        body, name=name, grid=(nk,),
        in_specs=[pl.BlockSpec((tk, k), lambda i: (i, 0))] + [pl.BlockSpec((tk, wd), lambda i: (i, 0)) for wd in widths],
        out_specs=pl.BlockSpec((k, n), lambda i: (0, 0)), out_shape=jax.ShapeDtypeStruct((k, n), out_dtype),
        scratch_shapes=[pltpu.VMEM((k, n), F32)],
        compiler_params=_params(("arbitrary",)),
    )(x, *douts)


def make_proj(widths, name):
    @jax.custom_vjp
    def op(x, w):
        return tuple(_proj_fwd_call(x, w.astype(MXU_DTYPE), widths, name + "_fwd"))

    def fwd(x, w):
        wb = w.astype(MXU_DTYPE)
        return tuple(_proj_fwd_call(x, wb, widths, name + "_fwd")), (x, wb, jnp.zeros((0,), w.dtype))

    def bwd(res, douts):
        x, wb, w_like = res
        return _proj_dx_call(douts, wb, widths, name + "_dx"), _proj_dw_call(x, douts, widths, name + "_dw", w_like.dtype)

    op.defvjp(fwd, bwd)
    return op


_MM_OPS = {}


def op_mm(a, w):
    n = w.shape[1]
    if n not in _MM_OPS:
        _MM_OPS[n] = make_proj((n,), "mm%d" % n)
    return _MM_OPS[n](a, w)[0]


@jax.custom_vjp
def _mm(a, w):
    return _dot(a, w, NN)


def _mm_f(a, w):
    return _dot(a, w, NN), (a, w)


def _mm_b(res, g):
    a, w = res
    return _dot(g, w, NT), _dot(a, g, TN)


_mm.defvjp(_mm_f, _mm_b)


@functools.partial(jax.custom_vjp, nondiff_argnums=(1,))
def _roll(x, shift):
    return pltpu.roll(x, shift, 1)


def _roll_f(x, shift):
    return pltpu.roll(x, shift, 1), None


def _roll_b(shift, _, g):
    return (pltpu.roll(g, (g.shape[1] - shift) % g.shape[1], 1),)


_roll.defvjp(_roll_f, _roll_b)


def _ln(x, g, b):
    mu = jnp.mean(x, axis=-1, keepdims=True)
    xc = x - mu
    var = jnp.mean(xc * xc, axis=-1, keepdims=True)
    return xc * lax.rsqrt(var + LN_EPS) * g + b


def _rms(x, g):
    ms = jnp.mean(x * x, axis=-1, keepdims=True)
    return x * lax.rsqrt(ms + RMS_EPS) * g


def _sigmoid(x):
    return jax.nn.sigmoid(x)


def _silu(x):
    return x * _sigmoid(x)


def _gelu_tanh(x):
    return x * (0.5 * (1.0 + jnp.tanh(math.sqrt(2.0 / math.pi) * (x + 0.044715 * (x * x * x)))))


def _rowwise_fwd_call(fn, rows, consts, name, tile):
    t = rows[0].shape[0]
    tile = min(tile, t)
    nr = len(rows)
    outs = jax.eval_shape(fn, *[jax.ShapeDtypeStruct((tile, r.shape[1]), F32) for r in rows],
                          *[jax.ShapeDtypeStruct(c.shape, F32) for c in consts])

    def body(*refs):
        vals = [r[...] for r in refs[:nr + len(consts)]]
        res = fn(*vals)
        for o_ref, o in zip(refs[nr + len(consts):], res):
            o_ref[...] = o

    return pl.pallas_call(
        body, name=name, grid=(t // tile,),
        in_specs=[pl.BlockSpec((tile, r.shape[1]), lambda i: (i, 0)) for r in rows]
        + [pl.BlockSpec(c.shape, lambda i: (0, 0)) for c in consts],
        out_specs=[pl.BlockSpec((tile, o.shape[1]), lambda i: (i, 0)) for o in outs],
        out_shape=[jax.ShapeDtypeStruct((t, o.shape[1]), F32) for o in outs],
        compiler_params=_params(("parallel",)),
    )(*rows, *consts)


def _rowwise_bwd_call(fn, rows, consts, douts, row_diff, name, tile, row_grad_dtype=F32):
    t = rows[0].shape[0]
    tile = min(tile, t)
    nr, nc, nd = len(rows), len(consts), len(douts)
    diff_idx = [i for i in range(nr) if row_diff[i]]

    def body(*refs):
        rv = [r[...] for r in refs[:nr]]
        cv = [r[...] for r in refs[nr:nr + nc]]
        dv = [r[...] for r in refs[nr + nc:nr + nc + nd]]
        out_refs = refs[nr + nc + nd:]

        def f(*diff):
            full = list(rv)
            for k, i in enumerate(diff_idx):
                full[i] = diff[k]
            return fn(*full, *diff[len(diff_idx):])

        _, vjp = jax.vjp(f, *[rv[i] for i in diff_idx], *cv)
        grads = vjp(tuple(dv))
        for k in range(len(diff_idx)):
            out_refs[k][...] = grads[k].astype(row_grad_dtype)
        first = pl.program_id(0) == 0
        for k in range(nc):
            acc_ref = out_refs[len(diff_idx) + k]
            g = grads[len(diff_idx) + k]

            @pl.when(first)
            def _(acc_ref=acc_ref, g=g):
                acc_ref[...] = g

            @pl.when(jnp.logical_not(first))
            def _(acc_ref=acc_ref, g=g):
                acc_ref[...] += g

    res = pl.pallas_call(
        body, name=name, grid=(t // tile,),
        in_specs=[pl.BlockSpec((tile, r.shape[1]), lambda i: (i, 0)) for r in rows]
        + [pl.BlockSpec(c.shape, lambda i: (0, 0)) for c in consts]
        + [pl.BlockSpec((tile, d.shape[1]), lambda i: (i, 0)) for d in douts],
        out_specs=[pl.BlockSpec((tile, rows[i].shape[1]), lambda i_: (i_, 0)) for i in diff_idx]
        + [pl.BlockSpec(c.shape, lambda i: (0, 0)) for c in consts],
        out_shape=[jax.ShapeDtypeStruct(rows[i].shape, row_grad_dtype) for i in diff_idx]
        + [jax.ShapeDtypeStruct(c.shape, F32) for c in consts],
        compiler_params=_params(("arbitrary",)),
    )(*rows, *consts, *douts)
    return res[:len(diff_idx)], res[len(diff_idx):]


def make_rowwise(fn, name, row_diff, tile=ROW_TILE):
    @jax.custom_vjp
    def op(rows, consts):
        return tuple(_rowwise_fwd_call(fn, rows, consts, name + "_fwd", tile))

    def fwd(rows, consts):
        return op(rows, consts), (rows, consts)

    def bwd(res, douts):
        rows, consts = res
        drows, dconsts = _rowwise_bwd_call(fn, rows, consts, douts, row_diff, name + "_bwd", tile)
        it = iter(drows)
        full = tuple(next(it) if row_diff[i] else jnp.zeros_like(rows[i]) for i in range(len(rows)))
        return full, tuple(dconsts)

    op.defvjp(fwd, bwd)
    return op


def _conv_post_fn(cv, a_z, ng, nb, w_pw2):
    return (_mm(_silu(_ln(cv, ng, nb)), w_pw2) * _silu(a_z),)


def _mla_prep_fn(c_q, c_kv, krblk, cos4, sin4, qg, kvg, w_uq, w_uk, w_uv):
    qe = _mm(_rms(c_q, qg), w_uq)
    q = qe * cos4 + _roll(qe, qe.shape[1] - 32) * sin4
    cos1, sin1 = cos4[:, :LANES], sin4[:, :LANES]
    kr = krblk * cos1 + _roll(krblk, LANES - 32) * sin1
    kn = _rms(c_kv, kvg)
    k = _mm(kn, w_uk) + jnp.concatenate([kr, kr, kr, kr], axis=1)
    return q, k, _mm(kn, w_uv)


def _ssm_post_fn(y, u, c_z, d, w_a, w_b):
    y2 = _gelu_tanh(y + d * u)
    return (_mm(y2, w_a) * _sigmoid(_mm(y2, w_b)) * _silu(c_z),)


def _gate_fn(o, z):
    return (o * _silu(z),)


def _merge_fn(br0, br1, br2, br3, gl0, gl1, gl2, gl3, b0, b1, b2, b3):
    return (_sigmoid(gl0 + b0) * br0 + _sigmoid(gl1 + b1) * br1 + _sigmoid(gl2 + b2) * br2 + _sigmoid(gl3 + b3) * br3,)


def _ln_fn(x, mo, g, b):
    return (_ln(DEEPNORM_ALPHA * x + mo, g, b),)


def _ple_fn(x1, pe, gl, g):
    return (x1 + _rms(pe * _sigmoid(gl), g),)


op_conv_post = make_rowwise(_conv_post_fn, "conv_post", (True, True))
op_mla_prep = make_rowwise(_mla_prep_fn, "mla_prep", (True, True, True, False, False))
op_ssm_post = make_rowwise(_ssm_post_fn, "ssm_post", (True, True, True))
op_gate = make_rowwise(_gate_fn, "gate", (True, True))
MERGE_TILE = ROW_TILE // 2
MERGE_WIDTHS = (D_MODEL,) * 4


@jax.custom_vjp
def op_merge_block(x, ys, w_merge, w_branch, b_merge):
    return _merge_block_fwd(x, ys, w_merge, w_branch, b_merge)[0]


def _merge_block_fwd(x, ys, w_merge, w_branch, b_merge):
    wm, wb = w_merge.astype(MXU_DTYPE), w_branch.astype(MXU_DTYPE)
    gl = _proj_fwd_call(x, wm, MERGE_WIDTHS, "merge_proj_fwd")
    br = [_proj_fwd_call(y, wb[n], (D_MODEL,), "branch_proj_fwd")[0] for n, y in enumerate(ys)]
    bm = tuple(b_merge[n * D_MODEL:(n + 1) * D_MODEL].reshape(1, -1) for n in range(4))
    (merged,) = _rowwise_fwd_call(_merge_fn, (*br, *gl), bm, "merge_fwd", MERGE_TILE)
    return merged, (x, ys, wm, wb, tuple(br), tuple(gl), bm, jnp.zeros((0,), w_merge.dtype), jnp.zeros((0,), w_branch.dtype))


def _merge_block_bwd(res, dmerged):
    x, ys, wm, wb, br, gl, bm, wm_like, wb_like = res
    drows, dbm = _rowwise_bwd_call(_merge_fn, (*br, *gl), bm, (dmerged,), (True,) * 8, "merge_bwd", MERGE_TILE, MXU_DTYPE)
    dbr, dgl = drows[:4], drows[4:]
    dx = _proj_dx_call(dgl, wm, MERGE_WIDTHS, "merge_proj_dx")
    dwm = _proj_dw_call(x, dgl, MERGE_WIDTHS, "merge_proj_dw", wm_like.dtype)
    dys = tuple(_proj_dx_call([dbr[n]], wb[n], (D_MODEL,), "branch_proj_dx") for n in range(4))
    dwb = jnp.stack([_proj_dw_call(ys[n], [dbr[n]], (D_MODEL,), "branch_proj_dw", wb_like.dtype) for n in range(4)])
    return dx, dys, dwm, dwb, jnp.concatenate([d.reshape(-1) for d in dbm])


op_merge_block.defvjp(_merge_block_fwd, _merge_block_bwd)
op_ln = make_rowwise(_ln_fn, "post_ln", (True, True))
op_ple = make_rowwise(_ple_fn, "ple", (True, True, True))


def _conv_fwd_call(a_val, a_gate, w32, b):
    t, w = a_val.shape
    tile = min(ROW_TILE, t)
    per = tile // CONV_HALO
    cur = pl.BlockSpec((tile, w), lambda i: (i, 0))
    prev = pl.BlockSpec((CONV_HALO, w), lambda i: (jnp.maximum(i * per - 1, 0), 0))

    def body(av_ref, avh_ref, ag_ref, agh_ref, w_ref, b_ref, cv_ref, buf):
        i = pl.program_id(0)
        gh = avh_ref[...] * _sigmoid(agh_ref[...])
        buf[0:CONV_HALO, :] = jnp.where(i > 0, gh, 0.0)
        buf[CONV_HALO:, :] = av_ref[...] * _sigmoid(ag_ref[...])
        acc = jnp.zeros((tile, w), F32) + b_ref[...]
        for j in range(CONV_W):
            acc = acc + w_ref[j:j + 1, :] * buf[pl.ds(CONV_HALO - (CONV_W - 1) + j, tile), :]
        cv_ref[...] = acc

    return pl.pallas_call(
        body, name="conv_fwd", grid=(t // tile,),
        in_specs=[cur, prev, cur, prev, pl.BlockSpec((CONV_HALO, w), lambda i: (0, 0)), pl.BlockSpec((1, w), lambda i: (0, 0))],
        out_specs=cur, out_shape=jax.ShapeDtypeStruct((t, w), F32),
        scratch_shapes=[pltpu.VMEM((tile + CONV_HALO, w), F32)],
        compiler_params=_params(("parallel",)),
    )(a_val, a_val, a_gate, a_gate, w32, b)


def _conv_bwd_call(a_val, a_gate, w32, dcv):
    t, w = a_val.shape
    tile = min(ROW_TILE, t)
    n = t // tile
    per = tile // CONV_HALO
    cur = pl.BlockSpec((tile, w), lambda i: (i, 0))
    prev = pl.BlockSpec((CONV_HALO, w), lambda i: (jnp.maximum(i * per - 1, 0), 0))
    nxt = pl.BlockSpec((CONV_HALO, w), lambda i: (jnp.minimum((i + 1) * per, t // CONV_HALO - 1), 0))
    full = lambda r: pl.BlockSpec((r, w), lambda i: (0, 0))

    def body(av_ref, avh_ref, ag_ref, agh_ref, w_ref, d_ref, dn_ref, dav_ref, dag_ref, dw_ref, db_ref, gbuf, dbuf):
        i = pl.program_id(0)
        gh = avh_ref[...] * _sigmoid(agh_ref[...])
        gbuf[0:CONV_HALO, :] = jnp.where(i > 0, gh, 0.0)
        av = av_ref[...]
        sg = _sigmoid(ag_ref[...])
        gbuf[CONV_HALO:, :] = av * sg
        d = d_ref[...]
        dbuf[0:tile, :] = d
        dbuf[tile:, :] = jnp.where(i < n - 1, dn_ref[...], 0.0)

        @pl.when(i == 0)
        def _():
            dw_ref[...] = jnp.zeros_like(dw_ref)
            db_ref[...] = jnp.zeros_like(db_ref)

        dg = jnp.zeros((tile, w), F32)
        for j in range(CONV_W):
            dg = dg + w_ref[j:j + 1, :] * dbuf[pl.ds(CONV_W - 1 - j, tile), :]
            dw_ref[j:j + 1, :] += jnp.sum(d * gbuf[pl.ds(CONV_HALO - (CONV_W - 1) + j, tile), :], axis=0, keepdims=True)
        db_ref[...] += jnp.sum(d, axis=0, keepdims=True)
        dav_ref[...] = dg * sg
        dag_ref[...] = dg * av * sg * (1.0 - sg)

    return pl.pallas_call(
        body, name="conv_bwd", grid=(n,),
        in_specs=[cur, prev, cur, prev, full(CONV_HALO), cur, nxt],
        out_specs=[cur, cur, full(CONV_HALO), full(1)],
        out_shape=[jax.ShapeDtypeStruct((t, w), F32), jax.ShapeDtypeStruct((t, w), F32),
                   jax.ShapeDtypeStruct((CONV_HALO, w), F32), jax.ShapeDtypeStruct((1, w), F32)],
        scratch_shapes=[pltpu.VMEM((tile + CONV_HALO, w), F32), pltpu.VMEM((tile + CONV_HALO, w), F32)],
        compiler_params=_params(("arbitrary",)),
    )(a_val, a_val, a_gate, a_gate, w32, dcv, dcv)


def _pad_taps(conv_w):
    return jnp.concatenate([conv_w, jnp.zeros((CONV_HALO - CONV_W, conv_w.shape[1]), F32)], axis=0)


@jax.custom_vjp
def op_conv(a_val, a_gate, conv_w, conv_b):
    return _conv_fwd_call(a_val, a_gate, _pad_taps(conv_w), conv_b)


def _op_conv_fwd(a_val, a_gate, conv_w, conv_b):
    return op_conv(a_val, a_gate, conv_w, conv_b), (a_val, a_gate, conv_w)


def _op_conv_bwd(res, dcv):
    a_val, a_gate, conv_w = res
    dav, dag, dw, db = _conv_bwd_call(a_val, a_gate, _pad_taps(conv_w), dcv)
    return dav, dag, dw[:CONV_W], db


op_conv.defvjp(_op_conv_fwd, _op_conv_bwd)


def _head_masks(rows):
    lane = lax.broadcasted_iota(jnp.int32, (rows, LANES), 1)
    return lane < 64, lane >= 64


def _head_row(vals, mask):
    return jnp.max(jnp.where(mask, vals, NEG), axis=1, keepdims=True)


def _attn_valid(qpos, kpos, window):
    valid = kpos <= qpos
    if window is not None:
        valid = jnp.logical_and(valid, qpos - kpos < window)
    return valid


def _flash_fwd_call(q, k, v, sink, *, window, shared_k, scale, blk, blk_q, name):
    t = q.shape[0]
    qw = LANES if shared_k else 2 * LANES
    pairs = v.shape[1] // LANES
    tk = min(blk, t)
    tq = min(blk_q, t)
    has_sink = sink is not None

    def body(*refs):
        if has_sink:
            q_ref, k_ref, v_ref, s_ref, o_ref, lse_ref, k_mxu, v0_mxu, v1_mxu = refs
        else:
            q_ref, k_ref, v_ref, o_ref, lse_ref, k_mxu, v0_mxu, v1_mxu = refs
        v_mxu = (v0_mxu, v1_mxu)
        i = pl.program_id(1)

        @pl.when(i == 0)
        def _():
            full_masks = _head_masks(t)
            k_mxu[...] = k_ref[...].astype(MXU_DTYPE)
            for h in range(2):
                v_mxu[h][...] = jnp.where(full_masks[h], v_ref[...], 0.0).astype(MXU_DTYPE)

        qb = q_ref[...]
        masks = _head_masks(tq)
        row_masks = _head_masks(1)
        qh = [(jnp.where(masks[h], qb, 0.0) if shared_k else qb[:, h * LANES:(h + 1) * LANES]).astype(MXU_DTYPE) for h in range(2)]
        qpos = i * tq + lax.broadcasted_iota(jnp.int32, (tq, tk), 0)
        if has_sink:
            m_init = [jnp.zeros((tq, 1), F32) + _head_row(s_ref[...], row_masks[h]) for h in range(2)]
            l_init = [jnp.ones((tq, 1), F32)] * 2
        else:
            m_init = [jnp.full((tq, 1), NEG, F32)] * 2
            l_init = [jnp.zeros((tq, 1), F32)] * 2

        def make_step(masked):
            def step(j, carry):
                m0, l0, m1, l1, acc = carry
                start = pl.multiple_of(j * tk, tk)
                kb = k_mxu[pl.ds(start, tk), :]
                if masked:
                    valid = _attn_valid(qpos, j * tk + lax.broadcasted_iota(jnp.int32, (tq, tk), 1), window)
                new, alphas, pv = [], [], []
                for h, (m, l) in enumerate(((m0, l0), (m1, l1))):
                    kh = kb if shared_k else kb[:, h * LANES:(h + 1) * LANES]
                    s = _dot(qh[h], kh, NT) * scale
                    if masked:
                        s = jnp.where(valid, s, NEG)
                    m_new = jnp.maximum(m, jnp.max(s, axis=1, keepdims=True))
                    alpha = jnp.exp(m - m_new)
                    p = jnp.exp(s - m_new)
                    new += [m_new, alpha * l + jnp.sum(p, axis=1, keepdims=True)]
                    alphas.append(alpha)
                    pv.append(_dot(p, v_mxu[h][pl.ds(start, tk), :], NN))
                acc = acc * jnp.where(masks[0], alphas[0], alphas[1]) + pv[0] + pv[1]
                return new[0], new[1], new[2], new[3], acc
            return step

        carry = (m_init[0], l_init[0], m_init[1], l_init[1], jnp.zeros((tq, LANES), F32))
        last = (i * tq + tq - 1) // tk
        if window is None:
            n_full = (i * tq + 1) // tk
            carry = lax.fori_loop(0, n_full, make_step(False), carry)
            carry = lax.fori_loop(n_full, last + 1, make_step(True), carry)
        else:
            carry = lax.fori_loop(jnp.maximum(i * tq - (window - 1), 0) // tk, last + 1, make_step(True), carry)
        m0, l0, m1, l1, acc = carry
        o_ref[...] = acc / jnp.where(masks[0], l0, l1)
        lse_ref[...] = jnp.where(masks[0], m0 + jnp.log(l0), m1 + jnp.log(l1))

    in_specs = [pl.BlockSpec((tq, qw), lambda p, i: (i, p)), pl.BlockSpec((t, qw), lambda p, i: (0, p)),
                pl.BlockSpec((t, LANES), lambda p, i: (0, p))]
    args = [q, k, v]
    if has_sink:
        in_specs.append(pl.BlockSpec((1, LANES), lambda p, i: (0, p)))
        args.append(sink)
    blk_o = pl.BlockSpec((tq, LANES), lambda p, i: (i, p))
    return pl.pallas_call(
        body, name=name, grid=(pairs, t // tq), in_specs=in_specs, out_specs=[blk_o, blk_o],
        out_shape=[jax.ShapeDtypeStruct((t, pairs * LANES), F32)] * 2,
        scratch_shapes=[pltpu.VMEM((t, qw), MXU_DTYPE), pltpu.VMEM((t, LANES), MXU_DTYPE), pltpu.VMEM((t, LANES), MXU_DTYPE)],
        compiler_params=_params(("arbitrary", "arbitrary")),
    )(*args)


def _flash_bwd_call(q, k, v, sink, o, lse, do, *, window, shared_k, scale, blk, blk_q, name):
    t = q.shape[0]
    qw = LANES if shared_k else 2 * LANES
    pairs = v.shape[1] // LANES
    tk = min(blk, t)
    tq = min(blk_q, t)
    assert tk % tq == 0 or tq % tk == 0
    nq = t // tq
    has_sink = sink is not None

    def body(*refs):
        if has_sink:
            q_ref, k_ref, v_ref, o_ref, lse_ref, do_ref, s_ref, dq_ref, dk_ref, dv_ref, ds_ref = refs[:11]
        else:
            q_ref, k_ref, v_ref, o_ref, lse_ref, do_ref, dq_ref, dk_ref, dv_ref = refs[:9]
        q_mxu, do_mxu, lse_h, dsum_h = refs[-8:-6], refs[-6:-4], refs[-4:-2], refs[-2:]
        j = pl.program_id(1)
        masks = _head_masks(tq)
        row_masks = _head_masks(1)

        @pl.when(j == 0)
        def _():
            dq_ref[...] = jnp.zeros_like(dq_ref)
            full_masks = _head_masks(t)
            prod = do_ref[...] * o_ref[...]
            parts = []
            for h in range(2):
                qh = jnp.where(full_masks[h], q_ref[...], 0.0) if shared_k else q_ref[:, h * LANES:(h + 1) * LANES]
                q_mxu[h][...] = qh.astype(MXU_DTYPE)
                do_mxu[h][...] = jnp.where(full_masks[h], do_ref[...], 0.0).astype(MXU_DTYPE)
                dsum = jnp.sum(jnp.where(full_masks[h], prod, 0.0), axis=1, keepdims=True)
                lse = _head_row(lse_ref[...], full_masks[h])
                dsum_h[h][...] = jnp.zeros((t, LANES), F32) + dsum
                lse_h[h][...] = jnp.zeros((t, LANES), F32) + lse
                if has_sink:
                    ps = jnp.exp(_head_row(s_ref[...], row_masks[h]) - lse)
                    parts.append(-jnp.sum(ps * dsum, axis=0, keepdims=True))
            if has_sink:
                ds_ref[...] = jnp.zeros((SUBLANES, LANES), F32) + jnp.where(row_masks[0], parts[0], parts[1])

        kb = k_ref[...].astype(MXU_DTYPE)
        vb = v_ref[...].astype(MXU_DTYPE)
        kh = [kb if shared_k else kb[:, h * LANES:(h + 1) * LANES] for h in range(2)]
        kpos = j * tk + lax.broadcasted_iota(jnp.int32, (tq, tk), 1)
        lanes_of = lambda a: a if tk == LANES else jnp.concatenate([a] * (tk // LANES), axis=1)

        def make_step(masked):
            def step(i, carry):
                dk0, dk1, dv = carry
                start = pl.multiple_of(i * tq, tq)
                if masked:
                    valid = _attn_valid(i * tq + lax.broadcasted_iota(jnp.int32, (tq, tk), 0), kpos, window)
                dks, dqs = [], []
                for h in range(2):
                    qh = q_mxu[h][pl.ds(start, tq), :]
                    doh = do_mxu[h][pl.ds(start, tq), :]
                    s = _dot(qh, kh[h], NT) * scale
                    if masked:
                        s = jnp.where(valid, s, NEG)
                    p = jnp.exp(s - lanes_of(lse_h[h][pl.ds(start, tq), :]))
                    dp = _dot(doh, vb, NT)
                    dsc = p * (dp - lanes_of(dsum_h[h][pl.ds(start, tq), :])) * scale
                    dv = dv + _dot(p, doh, TN)
                    dks.append(_dot(dsc, qh, TN))
                    dq_h = _dot(dsc, kh[h], NN)
                    dqs.append(jnp.where(masks[h], dq_h, 0.0) if shared_k else dq_h)
                if shared_k:
                    dq_ref[pl.ds(start, tq), :] += dqs[0] + dqs[1]
                else:
                    dq_ref[pl.ds(start, tq), :] += jnp.concatenate(dqs, axis=1)
                return dk0 + dks[0], dk1 + dks[1], dv
            return step

        zero = jnp.zeros((tk, LANES), F32)
        carry = (zero, zero, zero)
        first = (j * tk) // tq
        if window is None:
            n_full = jnp.minimum(((j + 1) * tk + tq - 2) // tq, nq)
            carry = lax.fori_loop(first, n_full, make_step(True), carry)
            carry = lax.fori_loop(n_full, nq, make_step(False), carry)
        else:
            carry = lax.fori_loop(first, jnp.minimum(nq, (j * tk + tk - 1 + window - 1) // tq + 1), make_step(True), carry)
        dk0, dk1, dv = carry
        dk_ref[...] = dk0 + dk1 if shared_k else jnp.concatenate([dk0, dk1], axis=1)
        dv_ref[...] = dv

    full = lambda w: pl.BlockSpec((t, w), lambda p, j: (0, p))
    blkspec = lambda w: pl.BlockSpec((tk, w), lambda p, j: (j, p))
    in_specs = [full(qw), blkspec(qw), blkspec(LANES), full(LANES), full(LANES), full(LANES)]
    args = [q, k, v, o, lse, do]
    out_specs = [full(qw), blkspec(qw), blkspec(LANES)]
    out_shape = [jax.ShapeDtypeStruct(q.shape, F32), jax.ShapeDtypeStruct(k.shape, F32), jax.ShapeDtypeStruct(v.shape, F32)]
    if has_sink:
        in_specs.append(pl.BlockSpec((1, LANES), lambda p, j: (0, p)))
        args.append(sink)
        out_specs.append(pl.BlockSpec((SUBLANES, LANES), lambda p, j: (0, p)))
        out_shape.append(jax.ShapeDtypeStruct((SUBLANES, pairs * LANES), F32))
    return pl.pallas_call(
        body, name=name, grid=(pairs, t // tk), in_specs=in_specs, out_specs=out_specs, out_shape=out_shape,
        scratch_shapes=[pltpu.VMEM((t, LANES), MXU_DTYPE)] * 4 + [pltpu.VMEM((t, LANES), F32)] * 4,
        compiler_params=_params(("arbitrary", "arbitrary")),
    )(*args)


_MLA_CFG = dict(window=None, shared_k=False, scale=MLA_SCALE, blk=256)
_SWA_CFG = dict(window=WINDOW, shared_k=True, scale=SWA_SCALE, blk=128)
_MLA_FWD_CFG = dict(_MLA_CFG, blk=512, blk_q=256)
_SWA_FWD_CFG = dict(_SWA_CFG, blk=256, blk_q=256)
_MLA_BWD_CFG = dict(_MLA_CFG, blk=512, blk_q=512)
_SWA_BWD_CFG = dict(_SWA_CFG, blk=256, blk_q=256)


@jax.custom_vjp
def op_mla_attn(q, k, v):
    return _flash_fwd_call(q, k, v, None, name="mla_fwd", **_MLA_FWD_CFG)[0]


def _op_mla_attn_fwd(q, k, v):
    o, lse = _flash_fwd_call(q, k, v, None, name="mla_fwd", **_MLA_FWD_CFG)
    return o, (q, k, v, o, lse)


def _op_mla_attn_bwd(res, do):
    q, k, v, o, lse = res
    return tuple(_flash_bwd_call(q, k, v, None, o, lse, do, name="mla_bwd", **_MLA_BWD_CFG))


op_mla_attn.defvjp(_op_mla_attn_fwd, _op_mla_attn_bwd)


@jax.custom_vjp
def op_swa_attn(q, k, v, sink):
    return _flash_fwd_call(q, k, v, sink, name="swa_fwd", **_SWA_FWD_CFG)[0]


def _op_swa_attn_fwd(q, k, v, sink):
    o, lse = _flash_fwd_call(q, k, v, sink, name="swa_fwd", **_SWA_FWD_CFG)
    return o, (q, k, v, sink, o, lse)


def _op_swa_attn_bwd(res, do):
    q, k, v, sink, o, lse = res
    dq, dk, dv, dsink = _flash_bwd_call(q, k, v, sink, o, lse, do, name="swa_bwd", **_SWA_BWD_CFG)
    first_lane = lax.broadcasted_iota(jnp.int32, (1, dsink.shape[1]), 1) % 64 == 0
    return dq, dk, dv, jnp.where(first_lane, dsink[:1], 0.0)


op_swa_attn.defvjp(_op_swa_attn_fwd, _op_swa_attn_bwd)


def _complex_power(ar, ai, n):
    for _ in range(int(math.log2(n))):
        ar, ai = ar * ar - ai * ai, 2.0 * ar * ai
    return ar, ai


def _scan_passes(load_b, a1r, a1i, n, store, e_ref, c_ref, reverse):
    cb = a1r.shape[1]
    ar = jnp.zeros((SCAN_SEGMENTS, cb), F32) + a1r
    ai = jnp.zeros((SCAN_SEGMENTS, cb), F32) + a1i
    a2r, a2i = ar * ar - ai * ai, 2.0 * ar * ai
    idx = (lambda k: n - 1 - k) if reverse else (lambda k: k)
    mac = lambda pr_, pi__, h, b: (pr_ * h[0] - pi__ * h[1] + b[0], pr_ * h[1] + pi__ * h[0] + b[1])

    def load_pair(ii):
        k = jnp.minimum(2 * ii, n - 2)
        b0, b1 = load_b(idx(k)), load_b(idx(k + 1))
        return b0, mac(ar, ai, b0, b1)

    def local(ii, carry):
        h, c = carry
        return mac(a2r, a2i, h, c), load_pair(ii + 1)[1]

    zero = jnp.zeros((SCAN_SEGMENTS, cb), F32)
    (er, ei), _ = lax.fori_loop(0, n // 2, local, ((zero, zero), load_pair(0)[1]))
    e_ref[:, 0:cb] = er
    e_ref[:, cb:] = ei
    pr, pi_ = _complex_power(a1r, a1i, n)
    cr = jnp.zeros((1, cb), F32)
    ci = jnp.zeros((1, cb), F32)
    order = range(SCAN_SEGMENTS - 1, -1, -1) if reverse else range(SCAN_SEGMENTS)
    for s in order:
        c_ref[s:s + 1, 0:cb] = cr
        c_ref[s:s + 1, cb:] = ci
        er1, ei1 = e_ref[s:s + 1, 0:cb], e_ref[s:s + 1, cb:]
        cr, ci = pr * cr - pi_ * ci + er1, pr * ci + pi_ * cr + ei1

    def second(ii, carry):
        h, b0, c = carry
        h0 = mac(ar, ai, h, b0)
        h1 = mac(a2r, a2i, h, c)
        store(idx(2 * ii), *h0)
        store(idx(2 * ii + 1), *h1)
        nb0, nc = load_pair(ii + 1)
        return h1, nb0, nc

    b0, c0 = load_pair(0)
    lax.fori_loop(0, n // 2, second, ((c_ref[:, 0:cb], c_ref[:, cb:]), b0, c0))


def _scan_fwd_call(bu, lam):
    n = bu.shape[0]
    cb = SCAN_CB
    blk3 = pl.BlockSpec((n, SCAN_SEGMENTS, 2 * cb), lambda c: (0, 0, c))
    blk2 = lambda r: pl.BlockSpec((r, 2 * cb), lambda c: (0, c))

    def body(b_ref, lam_ref, h_ref, cin_ref, e_ref):
        def store(i, hr, hi):
            h_ref[i, :, 0:cb] = hr
            h_ref[i, :, cb:] = hi

        _scan_passes(lambda i: (b_ref[i, :, 0:cb], b_ref[i, :, cb:]), lam_ref[:, 0:cb], lam_ref[:, cb:], n, store,
                     e_ref, cin_ref, False)

    return pl.pallas_call(
        body, name="scan_fwd", grid=(SSM_CH // cb,), in_specs=[blk3, blk2(1)], out_specs=[blk3, blk2(SCAN_SEGMENTS)],
        out_shape=[jax.ShapeDtypeStruct(bu.shape, F32), jax.ShapeDtypeStruct((SCAN_SEGMENTS, 2 * SSM_CH), F32)],
        scratch_shapes=[pltpu.VMEM((SCAN_SEGMENTS, 2 * cb), F32)],
        compiler_params=_params(("parallel",)),
    )(bu, lam)


def _scan_bwd_call(dh, h, cin, lam):
    n = dh.shape[0]
    cb = SCAN_CB
    blk3 = pl.BlockSpec((n, SCAN_SEGMENTS, 2 * cb), lambda c: (0, 0, c))
    blk2 = lambda r: pl.BlockSpec((r, 2 * cb), lambda c: (0, c))

    def body(d_ref, h_ref, cin_ref, lam_ref, g_ref, dlam_ref, e_ref, c_ref, acc_ref):
        acc_ref[...] = jnp.zeros_like(acc_ref)

        def store(i, gr, gi):
            g_ref[i, :, 0:cb] = gr
            g_ref[i, :, cb:] = gi
            ip = jnp.maximum(i - 1, 0)
            hpr = jnp.where(i > 0, h_ref[ip, :, 0:cb], cin_ref[:, 0:cb])
            hpi = jnp.where(i > 0, h_ref[ip, :, cb:], cin_ref[:, cb:])
            acc_ref[:, 0:cb] += gr * hpr + gi * hpi
            acc_ref[:, cb:] += gi * hpr - gr * hpi

        _scan_passes(lambda i: (d_ref[i, :, 0:cb], d_ref[i, :, cb:]), lam_ref[:, 0:cb], -lam_ref[:, cb:], n, store,
                     e_ref, c_ref, True)
        dlam_ref[...] = acc_ref[...]

    return pl.pallas_call(
        body, name="scan_bwd", grid=(SSM_CH // cb,), in_specs=[blk3, blk3, blk2(SCAN_SEGMENTS), blk2(1)],
        out_specs=[blk3, blk2(SCAN_SEGMENTS)],
        out_shape=[jax.ShapeDtypeStruct(dh.shape, F32), jax.ShapeDtypeStruct((SCAN_SEGMENTS, 2 * SSM_CH), F32)],
        scratch_shapes=[pltpu.VMEM((SCAN_SEGMENTS, 2 * cb), F32)] * 3,
        compiler_params=_params(("parallel",)),
    )(dh, h, cin, lam)


@jax.custom_vjp
def op_scan(bu, lam):
    return _scan_fwd_call(bu, lam)[0]


def _op_scan_fwd(bu, lam):
    h, cin = _scan_fwd_call(bu, lam)
    return h, (h, cin, lam)


def _op_scan_bwd(res, dh):
    h, cin, lam = res
    g, dlam = _scan_bwd_call(dh, h, cin, lam)
    return g, jnp.sum(dlam, axis=0, keepdims=True)


op_scan.defvjp(_op_scan_fwd, _op_scan_bwd)


def _loss_call(y, target):
    t, d = y.shape
    tile = min(ROW_TILE, t)

    def body(y_ref, t_ref, dy_ref, acc_ref):
        @pl.when(pl.program_id(0) == 0)
        def _():
            acc_ref[...] = jnp.zeros_like(acc_ref)

        err = y_ref[...] - t_ref[...]
        dy_ref[...] = err * (1.0 / d)
        col = jnp.sum(err * err, axis=0, keepdims=True)
        part = col[:, 0:LANES]
        for c in range(1, d // LANES):
            part = part + col[:, c * LANES:(c + 1) * LANES]
        acc_ref[0:1, :] += part

    blk = pl.BlockSpec((tile, d), lambda i: (i, 0))
    dy, acc = pl.pallas_call(
        body, name="loss_head", grid=(t // tile,), in_specs=[blk, blk],
        out_specs=[blk, pl.BlockSpec((SUBLANES, LANES), lambda i: (0, 0))],
        out_shape=[jax.ShapeDtypeStruct((t, d), F32), jax.ShapeDtypeStruct((SUBLANES, LANES), F32)],
        compiler_params=_params(("arbitrary",)),
    )(y, target)
    return jnp.sum(acc) * (0.5 / d), dy


def _rot_cols(w, xp=jnp):
    return xp.concatenate([-w[:, 16:], w[:, :16]], axis=1)


def _ext_w_in(w, xp=jnp):
    a_val, a_gate, a_z, c_q, c_kv, k_r, b_z, u, c_z, q, k, v, d_z = xp.split(
        w, (256, 512, 768, 1024, 1152, 1184, 1440, 1696, 1952, 2208, 2336, 2464), axis=1)
    dup = lambda m: xp.concatenate([m[:, :64], m[:, :64], m[:, 64:], m[:, 64:]], axis=1)
    krblk = xp.concatenate([xp.zeros((w.shape[0], 64), w.dtype), k_r, _rot_cols(k_r, xp)], axis=1)
    return xp.concatenate([a_val, a_gate, a_z, c_q, b_z, u, c_z, q, dup(k), dup(v), d_z, c_kv, krblk], axis=1)


IN_WIDTH = 2720
IN_SHARD = IN_WIDTH // 4
IN_SHARD_PAD = 768
IN_EXT = 3072


BAND = 256


def _w_in_layout():
    src = _ext_w_in(np.arange(1, IN_WIDTH + 1, dtype=np.float32)[None, :], np)[0]
    col = np.abs(src).astype(np.int64) - 1
    row = np.where(col >= 0, (col // IN_SHARD) * IN_SHARD_PAD + col % IN_SHARD, -1)
    return row, np.sign(src)


def _w_in_layout_matrix():
    row, sign = _w_in_layout()
    rows = lax.broadcasted_iota(jnp.int32, (4 * IN_SHARD_PAD, IN_EXT), 0)
    return jnp.where(rows == jnp.asarray(row, jnp.int32)[None, :], jnp.asarray(sign, F32)[None, :], 0.0).astype(MXU_DTYPE)


def _band_tables():
    row, _ = _w_in_layout()
    nb = IN_EXT // BAND
    hit = np.zeros((nb, nb), bool)
    for c, r in enumerate(row):
        if r >= 0:
            hit[r // BAND, c // BAND] = True

    def table(h):
        depth = int(h.sum(axis=1).max())
        rows = []
        for o in range(nb):
            used = [int(b) for b in np.nonzero(h[o])[0]]
            spare = [b for b in range(nb) if not h[o, b]]
            rows.append(used + spare[:depth - len(used)])
        return np.asarray(rows, np.int32), depth

    return table(hit.T), table(hit)


def _band_mm_call(a, e, table, depth, e_transposed, name, out_dtype):
    m = a.shape[0]
    nb = IN_EXT // BAND
    dims = NT if e_transposed else NN

    def body(t_ref, a_ref, e_ref, o_ref, acc_ref):
        kk = pl.program_id(1)

        @pl.when(kk == 0)
        def _():
            acc_ref[...] = jnp.zeros_like(acc_ref)

        acc_ref[...] += _dot(a_ref[...], e_ref[...], dims)

        @pl.when(kk == depth - 1)
        def _():
            o_ref[...] = acc_ref[...].astype(out_dtype)

    blk = lambda o, kk, t: t[o * depth + kk]
    e_spec = pl.BlockSpec((BAND, BAND), (lambda o, kk, t: (o, blk(o, kk, t))) if e_transposed else (lambda o, kk, t: (blk(o, kk, t), o)))
    return pl.pallas_call(
        body, name=name, out_shape=jax.ShapeDtypeStruct((m, IN_EXT), out_dtype),
        grid_spec=pltpu.PrefetchScalarGridSpec(
            num_scalar_prefetch=1, grid=(nb, depth),
            in_specs=[pl.BlockSpec((m, BAND), lambda o, kk, t: (0, blk(o, kk, t))), e_spec],
            out_specs=pl.BlockSpec((m, BAND), lambda o, kk, t: (0, o)),
            scratch_shapes=[pltpu.VMEM((m, BAND), F32)]),
        compiler_params=_params(("parallel", "arbitrary")),
    )(jnp.asarray(table.reshape(-1)), a, e)


@jax.custom_vjp
def op_w_in_ext(w_pad, e):
    (table, depth), _ = _band_tables()
    return _band_mm_call(w_pad, e, table, depth, False, "w_in_ext", F32)


def _op_w_in_ext_fwd(w_pad, e):
    return op_w_in_ext(w_pad, e), (e, jnp.zeros((0,), w_pad.dtype))


def _op_w_in_ext_bwd(res, g):
    e, w_like = res
    _, (table, depth) = _band_tables()
    return _band_mm_call(g, e, table, depth, True, "w_in_ext_bwd", w_like.dtype), jnp.zeros_like(e)


op_w_in_ext.defvjp(_op_w_in_ext_fwd, _op_w_in_ext_bwd)


H_COLS = dict(a_val=256, a_gate=256, a_z=256, c_q=256, b_z=256, u=256, c_z=256, q=256, kdup=256, vdup=256, d_z=256,
              c_kv=128, krblk=128)
op_in_proj = make_proj(tuple(H_COLS.values()), "in_proj")


def _ext_mla(w_uq, w_ukv):
    zeros = jnp.zeros((w_ukv.shape[0], 64), w_ukv.dtype)
    uq, uk, uv = [], [], []
    for h in range(4):
        nope, rp = w_uq[:, 96 * h:96 * h + 64], w_uq[:, 96 * h + 64:96 * h + 96]
        uq += [nope, rp, _rot_cols(rp)]
        uk += [w_ukv[:, 128 * h:128 * h + 64], zeros]
        uv.append(w_ukv[:, 128 * h + 64:128 * h + 128])
    return jnp.concatenate(uq, axis=1), jnp.concatenate(uk, axis=1), jnp.concatenate(uv, axis=1)


def _scan_cols(re, im):
    parts = []
    for c in range(SSM_CH // SCAN_CB):
        parts += [re[..., c * SCAN_CB:(c + 1) * SCAN_CB], im[..., c * SCAN_CB:(c + 1) * SCAN_CB]]
    return jnp.concatenate(parts, axis=-1)


def _ext_ssm(a_re, a_im, log_dt, b_re, b_im, c_re, c_im):
    dt = jnp.exp(log_dt)[:, None]
    mag = jnp.exp(a_re * dt)
    lb_re, lb_im = mag * jnp.cos(a_im * dt), mag * jnp.sin(a_im * dt)
    den = a_re * a_re + a_im * a_im
    nr, ni = lb_re - 1.0, lb_im
    f_re = ((nr * a_re + ni * a_im) / den)[..., None]
    f_im = ((ni * a_re - nr * a_im) / den)[..., None]
    bb_re = f_re * b_re - f_im * b_im
    bb_im = f_re * b_im + f_im * b_re
    eye = jnp.eye(SSM_GROUPS, dtype=F32)
    spread = lambda a: a.transpose(0, 2, 1)[:, :, None, :] * eye[:, None, :, None]
    bd_in = lambda bb: spread(bb).reshape(SSM_GROUPS * SSM_GROUP, SSM_CH)
    bd_out = lambda cc: spread(cc).reshape(SSM_CH, SSM_GROUPS * SSM_GROUP)
    w_bu = _scan_cols(bd_in(bb_re), bd_in(bb_im))
    w_y = _scan_cols(bd_out(c_re).T, -bd_out(c_im).T).T
    lam = _scan_cols(lb_re.reshape(1, SSM_CH), lb_im.reshape(1, SSM_CH))
    return w_bu, w_y, lam


def _rope_tables(t):
    pos = jnp.arange(t, dtype=F32)
    inv_freq = ROPE_THETA ** (-jnp.arange(0, 32, 2, dtype=F32) / 32)
    ang = pos[:, None] * inv_freq[None, :]
    cos, sin = jnp.cos(ang), jnp.sin(ang)
    ones, z32, z64 = jnp.ones((t, 64), F32), jnp.zeros((t, 32), F32), jnp.zeros((t, 64), F32)
    cos1 = jnp.concatenate([ones, cos, cos, z32], axis=1)
    sin1 = jnp.concatenate([z64, sin, sin, z32], axis=1)
    return jnp.concatenate([cos1] * 4, axis=1), jnp.concatenate([sin1] * 4, axis=1)


def _to_segments(a):
    t, w = a.shape
    return a.reshape(SCAN_SEGMENTS, t // SCAN_SEGMENTS, w).transpose(1, 0, 2)


def _from_segments(a):
    n, s, w = a.shape
    return a.transpose(1, 0, 2).reshape(n * s, w)


def _layer(x, p_i, cos4, sin4, e_mat, w):
    t = x.shape[0]
    row = lambda v: v.reshape(1, -1)
    f32 = lambda v: v.astype(F32)
    hs = dict(zip(H_COLS, op_in_proj(x, op_w_in_ext(w["w_in_pad"], e_mat))))

    cv = op_conv(hs["a_val"], hs["a_gate"], w["conv_w"], row(w["conv_b"]))
    (y_a,) = op_conv_post((cv, hs["a_z"]), (row(w["conv_norm_g"]), row(w["conv_norm_b"]), f32(w["w_pw2"])))

    w_uq, w_uk, w_uv = _ext_mla(w["w_uq"], f32(w["w_ukv"]))
    q, k, v = op_mla_prep((hs["c_q"], hs["c_kv"], hs["krblk"], cos4, sin4),
                          (row(w["mla_q_norm_g"]), row(w["mla_kv_norm_g"]), w_uq, w_uk, w_uv))
    (y_b,) = op_gate((op_mla_attn(q, k, v), hs["b_z"]), ())

    w_bu, w_y, lam = _ext_ssm(w["ssm_a_re"], w["ssm_a_im"], w["ssm_log_dt"], w["ssm_b_re"], w["ssm_b_im"],
                              w["ssm_c_re"], w["ssm_c_im"])
    u_seg = _to_segments(hs["u"]).reshape(t, BRANCH_W)
    bu = op_mm(u_seg, w_bu).reshape(t // SCAN_SEGMENTS, SCAN_SEGMENTS, 2 * SSM_CH)
    hstate = op_scan(bu, lam).reshape(t, 2 * SSM_CH)
    y_ssm = _from_segments(op_mm(hstate, w_y).reshape(t // SCAN_SEGMENTS, SCAN_SEGMENTS, BRANCH_W))
    w_glu = f32(w["w_glu"])
    (y_c,) = op_ssm_post((y_ssm, hs["u"], hs["c_z"]), (row(w["ssm_d"]), w_glu[:, :BRANCH_W], w_glu[:, BRANCH_W:]))

    sink = jnp.repeat(w["attn_sinks"], 64).reshape(1, 2 * LANES)
    (y_d,) = op_gate((op_swa_attn(hs["q"], hs["kdup"], hs["vdup"], sink), hs["d_z"]), ())

    merged = op_merge_block(x, (y_a, y_b, y_c, y_d), w["w_merge"], w["w_branch"], w["b_merge"])
    (x1,) = op_ln((x, op_mm(merged, w["w_out"])), (row(w["ln_g"]), row(w["ln_b"])))
    (out,) = op_ple((x1, op_mm(p_i, w["w_ple"]), op_mm(x1, w["w_ple_gate"])), (row(w["ple_norm_g"]),))
    return out


def _forward(x, p, layers):
    cos4, sin4 = _rope_tables(x.shape[0])
    e_mat = _w_in_layout_matrix()
    for i in range(DEPTH):
        x = _layer(x, p[i], cos4, sin4, e_mat, layers[i])
    return x


SHARD_AXIS = dict(w_in=2, w_merge=2, conv_w=2, w_pw2=1, w_uq=2, w_ukv=2, w_glu=2, w_branch=3, w_out=1, w_ple=2, w_ple_gate=1)
ODD = ("w_uq", "conv_w")
BIG = tuple(n for n in SHARD_AXIS if n not in ODD)
REPLICATED = ("b_merge", "conv_b", "conv_norm_g", "conv_norm_b", "mla_q_norm_g", "mla_kv_norm_g", "ssm_a_re", "ssm_a_im",
              "ssm_log_dt", "ssm_b_re", "ssm_b_im", "ssm_c_re", "ssm_c_im", "ssm_d", "attn_sinks", "ln_g", "ln_b", "ple_norm_g")
WEIGHTS = ("w_in", "w_merge", "b_merge", "conv_w", "conv_b", "conv_norm_g", "conv_norm_b", "w_pw2", "mla_q_norm_g",
           "mla_kv_norm_g", "w_uq", "w_ukv", "ssm_a_re", "ssm_a_im", "ssm_log_dt", "ssm_b_re", "ssm_b_im", "ssm_c_re",
           "ssm_c_im", "ssm_d", "w_glu", "attn_sinks", "w_branch", "w_out", "ln_g", "ln_b", "w_ple", "w_ple_gate", "ple_norm_g")
PACK_COLS = 1024
PACK_ROWS = 16
CHIP_FLIPS = ((1, 0), (0, 1), (1, 1))
N_CHIPS = 4
N_DEV = 8


def _pack_rows(n):
    return -(-n // (SUBLANES * PACK_COLS)) * SUBLANES


def _pack(arrays, dtype):
    blocks, rows = [], 0
    for a in arrays:
        r = _pack_rows(a.size)
        flat = a.reshape(-1).astype(dtype)
        blocks.append(jnp.pad(flat, (0, r * PACK_COLS - a.size)).reshape(r, PACK_COLS))
        rows += r
    pad = -rows % PACK_ROWS
    if pad:
        blocks.append(jnp.zeros((pad, PACK_COLS), dtype))
    return jnp.concatenate(blocks, axis=0)


def _unpack(buf, shapes):
    out, row = [], 0
    for s in shapes:
        n = math.prod(s)
        r = _pack_rows(n)
        out.append(buf[row:row + r].reshape(-1)[:n].reshape(s))
        row += r
    return out


def _flip(v, bit):
    return 1 - v if bit else v


def _window(ref, axis, start, size):
    idx = [slice(None)] * len(ref.shape)
    idx[axis] = pl.ds(start, size)
    return ref.at[tuple(idx)]


def _gather_chips(srcs, axes, stacked):
    units = []
    for k, (s, a) in enumerate(zip(srcs, axes)):
        if stacked[k]:
            units += [(k, l, s.shape[1:], a - 1) for l in range(s.shape[0])]
        else:
            units.append((k, None, s.shape, a))
    nu, nb = len(units), len(srcs)

    def body(*refs):
        ins, outs = refs[:nb], refs[nb:nb + nu]
        ici_send, ici_recv, d2d_send, d2d_recv, local_sems = refs[nb + nu:]
        x, y, c = lax.axis_index("x"), lax.axis_index("y"), lax.axis_index("c")
        me = 2 * x + y

        def mine(u, half=None):
            k, l, shape, _ = units[u]
            ref = ins[k] if l is None else ins[k].at[l]
            return ref if half is None else ref.at[pl.ds(half * (shape[0] // 2), shape[0] // 2)]

        def place(u, chip, half=None):
            _, _, shape, a = units[u]
            size, rows = shape[a], shape[0] // 2
            if half is None:
                return _window(outs[u], a, chip * size, size)
            if a == 0:
                return outs[u].at[pl.ds(chip * size + half * rows, rows)]
            return _window(outs[u].at[pl.ds(half * rows, rows)], a, chip * size, size)

        local = [pltpu.make_async_copy(mine(u), place(u, me), local_sems.at[u]) for u in range(nu)]
        for cp in local:
            cp.start()
        sends = []
        for j, (bx, by) in enumerate(CHIP_FLIPS):
            for u in range(nu):
                cp = pltpu.make_async_remote_copy(src_ref=mine(u, c), dst_ref=place(u, me, c),
                                                  send_sem=ici_send.at[j * nu + u], recv_sem=ici_recv.at[j * nu + u],
                                                  device_id=(_flip(x, bx), _flip(y, by), c), device_id_type=MESH)
                cp.start()
                sends.append(cp)
        for j, (bx, by) in enumerate(CHIP_FLIPS):
            src = 2 * _flip(x, bx) + _flip(y, by)
            for u in range(nu):
                got = place(u, src, c)
                pltpu.make_async_remote_copy(src_ref=got, dst_ref=got, send_sem=ici_send.at[j * nu + u],
                                             recv_sem=ici_recv.at[j * nu + u], device_id=(x, y, c), device_id_type=MESH).wait_recv()
                cp = pltpu.make_async_remote_copy(src_ref=got, dst_ref=got, send_sem=d2d_send.at[j * nu + u],
                                                  recv_sem=d2d_recv.at[j * nu + u], device_id=(x, y, 1 - c), device_id_type=MESH)
                cp.start()
                sends.append(cp)
        for j, (bx, by) in enumerate(CHIP_FLIPS):
            src = 2 * _flip(x, bx) + _flip(y, by)
            for u in range(nu):
                other = place(u, src, 1 - c)
                pltpu.make_async_remote_copy(src_ref=other, dst_ref=other, send_sem=d2d_send.at[j * nu + u],
                                             recv_sem=d2d_recv.at[j * nu + u], device_id=(x, y, c), device_id_type=MESH).wait_recv()
        for cp in sends:
            cp.wait_send()
        for cp in local:
            cp.wait()

    full = lambda shape, a: tuple(N_CHIPS * d if i == a else d for i, d in enumerate(shape))
    res = pl.pallas_call(
        body, name="gather_weights", in_specs=[ANY] * nb, out_specs=[ANY] * nu,
        out_shape=[jax.ShapeDtypeStruct(full(shape, a), srcs[k].dtype) for k, _, shape, a in units],
        scratch_shapes=[pltpu.SemaphoreType.DMA((3 * nu,))] * 4 + [pltpu.SemaphoreType.DMA((nu,))],
    )(*srcs)
    out, it = [], iter(res)
    for k in range(nb):
        out.append([next(it) for _ in range(srcs[k].shape[0])] if stacked[k] else next(it))
    return out


def _exchange_grads(grads, axes, smalls):
    nt, ns = len(grads), len(smalls)
    sizes = [g[0].shape[a] // N_CHIPS for g, a in zip(grads, axes)]
    dev_flips = [(bx, by, bc) for bx in (0, 1) for by in (0, 1) for bc in (0, 1)][1:]
    n_remote = 3 * nt * DEPTH + 7 * ns
    n_local = nt * DEPTH + ns

    def body(*refs):
        g_refs = [refs[k * DEPTH:(k + 1) * DEPTH] for k in range(nt)]
        s_refs = refs[nt * DEPTH:nt * DEPTH + ns]
        outs = refs[nt * DEPTH + ns:nt * DEPTH + ns + nt + ns]
        recv_refs, all_refs = outs[:nt], outs[nt:]
        send_sems, recv_sems, local_sems = refs[-3:]
        x, y, c = lax.axis_index("x"), lax.axis_index("y"), lax.axis_index("c")
        me_chip = 2 * x + y
        me = 4 * x + 2 * y + c
        part = lambda k, i, chip: _window(g_refs[k][i], axes[k], chip * sizes[k], sizes[k])
        started, waits = [], []
        sem, lsem = 0, 0
        for k in range(nt):
            for i in range(DEPTH):
                cp = pltpu.make_async_copy(part(k, i, me_chip), recv_refs[k].at[i, 3], local_sems.at[lsem])
                cp.start()
                started.append(cp.wait)
                lsem += 1
                for j, (bx, by) in enumerate(CHIP_FLIPS):
                    px, py = _flip(x, bx), _flip(y, by)
                    cp = pltpu.make_async_remote_copy(src_ref=part(k, i, 2 * px + py), dst_ref=recv_refs[k].at[i, j],
                                                      send_sem=send_sems.at[sem], recv_sem=recv_sems.at[sem],
                                                      device_id=(px, py, c), device_id_type=MESH)
                    cp.start()
                    started.append(cp.wait_send)
                    waits.append(cp.wait_recv)
                    sem += 1
        for s in range(ns):
            cp = pltpu.make_async_copy(s_refs[s], all_refs[s].at[me], local_sems.at[lsem])
            cp.start()
            started.append(cp.wait)
            lsem += 1
            for bx, by, bc in dev_flips:
                peer = (_flip(x, bx), _flip(y, by), _flip(c, bc))
                cp = pltpu.make_async_remote_copy(src_ref=s_refs[s], dst_ref=all_refs[s].at[me], send_sem=send_sems.at[sem],
                                                  recv_sem=recv_sems.at[sem], device_id=peer, device_id_type=MESH)
                cp.start()
                started.append(cp.wait_send)
                src = 4 * peer[0] + 2 * peer[1] + peer[2]
                waits.append(pltpu.make_async_remote_copy(src_ref=s_refs[s], dst_ref=all_refs[s].at[src], send_sem=send_sems.at[sem],
                                                          recv_sem=recv_sems.at[sem], device_id=peer, device_id_type=MESH).wait_recv)
                sem += 1
        for w in waits + started:
            w()

    shard = lambda g, a: tuple(d // N_CHIPS if i == a else d for i, d in enumerate(g.shape))
    flat = [g for per_layer in grads for g in per_layer]
    return pl.pallas_call(
        body, name="exchange_grads", in_specs=[ANY] * (len(flat) + ns), out_specs=[ANY] * (nt + ns),
        out_shape=[jax.ShapeDtypeStruct((DEPTH, N_CHIPS, *shard(g[0], a)), g[0].dtype) for g, a in zip(grads, axes)]
        + [jax.ShapeDtypeStruct((N_DEV, *s.shape), s.dtype) for s in smalls],
        scratch_shapes=[pltpu.SemaphoreType.DMA((n_remote,)), pltpu.SemaphoreType.DMA((n_remote,)), pltpu.SemaphoreType.DMA((n_local,))],
    )(*flat, *smalls)


def _swap_cores(parts):
    nb = len(parts)

    def body(*refs):
        ins, outs, send_sems, recv_sems = refs[:nb], refs[nb:2 * nb], refs[-2], refs[-1]
        x, y, c = lax.axis_index("x"), lax.axis_index("y"), lax.axis_index("c")
        cps = [pltpu.make_async_remote_copy(src_ref=ins[k], dst_ref=outs[k], send_sem=send_sems.at[k], recv_sem=recv_sems.at[k],
                                            device_id=(x, y, 1 - c), device_id_type=MESH) for k in range(nb)]
        for cp in cps:
            cp.start()
        for cp in cps:
            cp.wait()

    return pl.pallas_call(
        body, name="swap_cores", in_specs=[ANY] * nb, out_specs=[ANY] * nb,
        out_shape=[jax.ShapeDtypeStruct(q.shape, q.dtype) for q in parts],
        scratch_shapes=[pltpu.SemaphoreType.DMA((nb,)), pltpu.SemaphoreType.DMA((nb,))],
    )(*parts)


def _sum_chips_call(recv, cols, name):
    depth, _, r, c = recv.shape
    tile = _pick(r, (512, 256, 128, 64, 32, 16))

    def body(r_ref, o_ref):
        slot = lambda s: r_ref[s, :, pl.ds(0, cols)].astype(F32)
        o_ref[...] = ((slot(3) + slot(0)) + slot(1)) + slot(2)

    return pl.pallas_call(
        body, name=name, grid=(depth, r // tile),
        in_specs=[pl.BlockSpec((None, N_CHIPS, tile, c), lambda l, i: (l, 0, i, 0))],
        out_specs=pl.BlockSpec((None, tile, cols), lambda l, i: (l, i, 0)), out_shape=jax.ShapeDtypeStruct((depth, r, cols), F32),
        compiler_params=_params(("parallel", "parallel")),
    )(recv)


def _sum_slots_call(slots, name):
    n, r, c = slots.shape
    tile = _pick(r, (512, 256, 128, 64, 32, 16, 8))

    def body(s_ref, o_ref):
        acc = s_ref[0]
        for s in range(1, n):
            acc = acc + s_ref[s]
        o_ref[...] = acc

    return pl.pallas_call(
        body, name=name, grid=(r // tile,), in_specs=[pl.BlockSpec((n, tile, c), lambda i: (0, i, 0))],
        out_specs=pl.BlockSpec((tile, c), lambda i: (i, 0)), out_shape=jax.ShapeDtypeStruct((r, c), F32),
        compiler_params=_params(("parallel",)),
    )(slots)


def _adamw_math(w, g, m, v):
    m = ADAM_B1 * m + (1.0 - ADAM_B1) * g
    v = ADAM_B2 * v + (1.0 - ADAM_B2) * (g * g)
    m_hat = m / (1.0 - ADAM_B1 ** ADAM_STEP)
    v_hat = v / (1.0 - ADAM_B2 ** ADAM_STEP)
    return -ADAM_LR * (m_hat / (jnp.sqrt(v_hat) + ADAM_EPS) + ADAM_WD * w), m, v


def _adamw_call(w, m, v, gparts, name):
    r, c = w.shape
    n = len(gparts)
    tile = _pick(r, (512, 256, 128, 64, 32, 16, 8))

    def body(w_ref, m_ref, v_ref, *refs):
        g_refs, (go_ref, d_ref, mo_ref, vo_ref) = refs[:n], refs[n:]
        g = g_refs[0][...]
        for g_ref in g_refs[1:]:
            g = g + g_ref[...]
        go_ref[...] = g
        d_ref[...], mo_ref[...], vo_ref[...] = _adamw_math(w_ref[...], g, m_ref[...], v_ref[...])

    blk = pl.BlockSpec((tile, c), lambda i: (i, 0))
    return pl.pallas_call(
        body, name=name, grid=(r // tile,), in_specs=[blk] * (3 + n),
        out_specs=[blk] * 4, out_shape=[jax.ShapeDtypeStruct((r, c), F32)] * 4,
        compiler_params=_params(("parallel",)),
    )(w, m, v, *gparts)


def _train_local(x, p, layers, target):
    y, vjp = jax.vjp(lambda x_, w_: _forward(x_, p, w_), x, layers)
    loss, dy = _loss_call(y, target)
    dx, dw = vjp(dy)
    return loss, dx, dw


def kernel(x, p, w_in, w_merge, b_merge, conv_w, conv_b, conv_norm_g, conv_norm_b, w_pw2, mla_q_norm_g, mla_kv_norm_g, w_uq, w_ukv, ssm_a_re, ssm_a_im, ssm_log_dt, ssm_b_re, ssm_b_im, ssm_c_re, ssm_c_im, ssm_d, w_glu, attn_sinks, w_branch, w_out, ln_g, ln_b, w_ple, w_ple_gate, ple_norm_g, loss_target, m_w_in, m_w_merge, m_b_merge, m_conv_w, m_conv_b, m_conv_norm_g, m_conv_norm_b, m_w_pw2, m_mla_q_norm_g, m_mla_kv_norm_g, m_w_uq, m_w_ukv, m_ssm_a_re, m_ssm_a_im, m_ssm_log_dt, m_ssm_b_re, m_ssm_b_im, m_ssm_c_re, m_ssm_c_im, m_ssm_d, m_w_glu, m_attn_sinks, m_w_branch, m_w_out, m_ln_g, m_ln_b, m_w_ple, m_w_ple_gate, m_ple_norm_g, v_w_in, v_w_merge, v_b_merge, v_conv_w, v_conv_b, v_conv_norm_g, v_conv_norm_b, v_w_pw2, v_mla_q_norm_g, v_mla_kv_norm_g, v_w_uq, v_w_ukv, v_ssm_a_re, v_ssm_a_im, v_ssm_log_dt, v_ssm_b_re, v_ssm_b_im, v_ssm_c_re, v_ssm_c_im, v_ssm_d, v_w_glu, v_attn_sinks, v_w_branch, v_w_out, v_ln_g, v_ln_b, v_w_ple, v_w_ple_gate, v_ple_norm_g):
    given = dict(locals())
    w_loc = {n: given[n] for n in WEIGHTS}
    m_loc = {n: given["m_" + n] for n in WEIGHTS}
    v_loc = {n: given["v_" + n] for n in WEIGHTS}

    me_chip = 2 * lax.axis_index("x") + lax.axis_index("y")

    wire = {n: w_loc[n].astype(MXU_DTYPE) for n in BIG}
    wire["w_in"] = jnp.pad(wire["w_in"], ((0, 0), (0, 0), (0, IN_SHARD_PAD - IN_SHARD)))
    odd_shapes = [w_loc[n].shape for n in ODD]
    gathered = _gather_chips([wire[n] for n in BIG] + [_pack([w_loc[n] for n in ODD], F32)], [SHARD_AXIS[n] for n in BIG] + [0],
                             [True] * len(BIG) + [False])
    full = dict(zip(BIG, gathered[:-1]))
    odd_parts = [_unpack(part, odd_shapes) for part in jnp.split(gathered[-1], N_CHIPS, axis=0)]
    for k, n in enumerate(ODD):
        full[n] = jnp.concatenate([odd_parts[s][k] for s in range(N_CHIPS)], axis=SHARD_AXIS[n])
    layers = []
    for i in range(DEPTH):
        layer = {n: (full[n][i] if n in full else w_loc[n][i]) for n in WEIGHTS if n != "w_in"}
        layer["w_in_pad"] = full["w_in"][i]
        layers.append(layer)

    loss, dx, dw = _train_local(x[0], p[:, 0], layers, loss_target[0])
    loss = lax.psum(loss, ("x", "y", "c"))

    key = lambda n: "w_in_pad" if n == "w_in" else n
    stacked = lambda n: jnp.stack([dw[i][n] for i in range(DEPTH)])
    small_rep = _pack([stacked(n) for n in REPLICATED], F32)
    small_odd = _pack([stacked(n) for n in ODD], F32)
    *recv, all_rep, all_odd = _exchange_grads([[dw[i][key(n)] for i in range(DEPTH)] for n in BIG],
                                              [SHARD_AXIS[n] - 1 for n in BIG], [small_rep, small_odd])
    parts = []
    for n, r in zip(BIG, recv):
        cols = w_loc[n].shape[-1]
        parts.append(_sum_chips_call(r.reshape(DEPTH, N_CHIPS, -1, r.shape[-1]), cols, "sum_chips_" + n))
    others = _swap_cores(parts)
    g_rep = _sum_slots_call(all_rep, "sum_replicated")
    g_odd = _unpack(_sum_slots_call(all_odd, "sum_odd"), [(DEPTH, *w_loc[n].shape[1:-1], N_CHIPS * w_loc[n].shape[-1]) for n in ODD])

    grads, deltas, new_m, new_v = {}, {}, {}, {}

    def adamw(n, gparts):
        shape = w_loc[n].shape
        two_d = lambda a: a.reshape(-1, shape[-1])
        res = _adamw_call(two_d(w_loc[n]), two_d(m_loc[n]), two_d(v_loc[n]), [two_d(g) for g in gparts], "adamw_" + n)
        grads[n], deltas[n], new_m[n], new_v[n] = [r.reshape(shape) for r in res]

    for n, part, other in zip(BIG, parts, others):
        adamw(n, [part, other])
    for n, g in zip(ODD, g_odd):
        size = w_loc[n].shape[-1]
        adamw(n, [lax.dynamic_slice_in_dim(g, me_chip * size, size, axis=g.ndim - 1)])
    rep_shapes = [w_loc[n].shape for n in REPLICATED]
    res = _adamw_call(_pack([w_loc[n] for n in REPLICATED], F32), _pack([m_loc[n] for n in REPLICATED], F32),
                      _pack([v_loc[n] for n in REPLICATED], F32), [g_rep], "adamw_replicated")
    for dst, buf in zip((grads, deltas, new_m, new_v), res):
        for n, a in zip(REPLICATED, _unpack(buf, rep_shapes)):
            dst[n] = a

    return (loss, dx[None], *[grads[n] for n in WEIGHTS], *[deltas[n] for n in WEIGHTS],
            *[new_m[n] for n in WEIGHTS], *[new_v[n] for n in WEIGHTS])
```

```python
import functools
import math

import jax
import jax.numpy as jnp
import numpy as np
from jax import lax
from jax.experimental import pallas as pl
from jax.experimental.pallas import tpu as pltpu

F32 = jnp.float32
BF16 = jnp.bfloat16
MXU_DTYPE = BF16
V7X_VMEM_BYTES = 64 * 1024 * 1024
VMEM_LIMIT = V7X_VMEM_BYTES * 3 // 4
LANES = 128
SUBLANES = 8

D_MODEL = 1024
DEPTH = 4
BRANCH_W = 256
CONV_W = 31
CONV_HALO = 32
MLA_SCALE = (64 + 32) ** -0.5
SWA_SCALE = 64 ** -0.5
WINDOW = 128
ROPE_THETA = 10000.0
SSM_GROUPS, SSM_GROUP, SSM_STATE = 16, 16, 64
SSM_CH = SSM_GROUPS * SSM_STATE
SCAN_SEGMENTS = SUBLANES
SCAN_CB = 128
DEEPNORM_ALPHA = (2.0 * DEPTH) ** 0.25
LN_EPS = 1e-5
RMS_EPS = 1e-6
ADAM_LR, ADAM_B1, ADAM_B2, ADAM_EPS, ADAM_WD, ADAM_STEP = 0.001, 0.9, 0.999, 1e-08, 0.01, 10
NEG = -1e30
ROW_TILE = 512

NN = (((1,), (0,)), ((), ()))
NT = (((1,), (1,)), ((), ()))
TN = (((0,), (0,)), ((), ()))

MESH = pl.DeviceIdType.MESH
ANY = pl.BlockSpec(memory_space=pl.ANY)


def _dot(a, b, dims):
    return lax.dot_general(a.astype(MXU_DTYPE), b.astype(MXU_DTYPE), dims, preferred_element_type=F32)


def _pick(n, cands):
    for c in cands:
        if n % c == 0:
            return c
    return n


def _params(sem):
    return pltpu.CompilerParams(dimension_semantics=sem, vmem_limit_bytes=VMEM_LIMIT)


def _col_offsets(widths):
    return [sum(widths[:j]) for j in range(len(widths))]


def _silu_gate(x, z):
    return x * (z * jax.nn.sigmoid(z))


def _proj_fwd_call(x, wb, widths, name, z=None):
    t, k = x.shape
    tm = min(ROW_TILE, t)
    offs = _col_offsets(widths)
    ins = [x] if z is None else [x, z]

    def body(*refs):
        w_ref, o_refs = refs[len(ins)], refs[len(ins) + 1:]
        xv = refs[0][...] if z is None else _silu_gate(refs[0][...], refs[1][...])
        xb = xv.astype(MXU_DTYPE)
        for o_ref, off, wd in zip(o_refs, offs, widths):
            o_ref[...] = _dot(xb, w_ref[:, off:off + wd], NN)

    return pl.pallas_call(
        body, name=name, grid=(t // tm,),
        in_specs=[pl.BlockSpec((tm, k), lambda i: (i, 0))] * len(ins) + [pl.BlockSpec(wb.shape, lambda i: (0, 0))],
        out_specs=[pl.BlockSpec((tm, wd), lambda i: (i, 0)) for wd in widths],
        out_shape=[jax.ShapeDtypeStruct((t, wd), F32) for wd in widths],
        compiler_params=_params(("parallel",)),
    )(*ins, wb)


def _proj_dx_call(douts, wb, widths, name, gate=None):
    t = douts[0].shape[0]
    k = wb.shape[0]
    tm = min(ROW_TILE, t)
    offs = _col_offsets(widths)
    nd = len(douts)
    extra = [] if gate is None else list(gate)

    def body(*refs):
        d_refs, w_ref = refs[:nd], refs[nd]
        acc = jnp.zeros((tm, k), F32)
        for d_ref, off, wd in zip(d_refs, offs, widths):
            acc = acc + _dot(d_ref[...], w_ref[:, off:off + wd], NT)
        if gate is None:
            refs[-1][...] = acc
        else:
            xv, zv = refs[nd + 1][...], refs[nd + 2][...]
            sg = jax.nn.sigmoid(zv)
            refs[-2][...] = acc * (zv * sg)
            refs[-1][...] = acc * xv * (sg * (1.0 + zv * (1.0 - sg)))

    row = pl.BlockSpec((tm, k), lambda i: (i, 0))
    n_out = 1 if gate is None else 2
    res = pl.pallas_call(
        body, name=name, grid=(t // tm,),
        in_specs=[pl.BlockSpec((tm, wd), lambda i: (i, 0)) for wd in widths] + [pl.BlockSpec(wb.shape, lambda i: (0, 0))] + [row] * len(extra),
        out_specs=[row] * n_out, out_shape=[jax.ShapeDtypeStruct((t, k), F32)] * n_out,
        compiler_params=_params(("parallel",)),
    )(*douts, wb, *extra)
    return res[0] if gate is None else tuple(res)


def _proj_dw_call(x, douts, widths, name, out_dtype, z=None):
    t, k = x.shape
    n = sum(widths)
    tk = min(ROW_TILE if k * n <= 2 * 1024 * 1024 else ROW_TILE // 2, t)
    nk = t // tk
    offs = _col_offsets(widths)
    ins = [x] if z is None else [x, z]

    def body(*all_refs):
        refs = all_refs[len(ins):]
        d_refs, o_ref, acc_ref = refs[:-2], refs[-2], refs[-1]

        @pl.when(pl.program_id(0) == 0)
        def _():
            acc_ref[...] = jnp.zeros_like(acc_ref)

        xv = all_refs[0][...] if z is None else _silu_gate(all_refs[0][...], all_refs[1][...])
        xb = xv.astype(MXU_DTYPE)
        for d_ref, off, wd in zip(d_refs, offs, widths):
            acc_ref[:, off:off + wd] += _dot(xb, d_ref[...], TN)

        @pl.when(pl.program_id(0) == nk - 1)
        def _():
            o_ref[...] = acc_ref[...].astype(out_dtype)

    return pl.pallas_call(
        body, name=name, grid=(nk,),
        in_specs=[pl.BlockSpec((tk, k), lambda i: (i, 0))] * len(ins) + [pl.BlockSpec((tk, wd), lambda i: (i, 0)) for wd in widths],
        out_specs=pl.BlockSpec((k, n), lambda i: (0, 0)), out_shape=jax.ShapeDtypeStruct((k, n), out_dtype),
        scratch_shapes=[pltpu.VMEM((k, n), F32)],
        compiler_params=_params(("arbitrary",)),
    )(*ins, *douts)


def make_proj(widths, name):
    @jax.custom_vjp
    def op(x, w):
        return tuple(_proj_fwd_call(x, w.astype(MXU_DTYPE), widths, name + "_fwd"))

    def fwd(x, w):
        wb = w.astype(MXU_DTYPE)
        return tuple(_proj_fwd_call(x, wb, widths, name + "_fwd")), (x, wb, jnp.zeros((0,), w.dtype))

    def bwd(res, douts):
        x, wb, w_like = res
        return _proj_dx_call(douts, wb, widths, name + "_dx"), _proj_dw_call(x, douts, widths, name + "_dw", w_like.dtype)

    op.defvjp(fwd, bwd)
    return op


_MM_OPS = {}


def op_mm(a, w):
    n = w.shape[1]
    if n not in _MM_OPS:
        _MM_OPS[n] = make_proj((n,), "mm%d" % n)
    return _MM_OPS[n](a, w)[0]


@jax.custom_vjp
def _mm(a, w):
    return _dot(a, w, NN)


def _mm_f(a, w):
    return _dot(a, w, NN), (a, w)


def _mm_b(res, g):
    a, w = res
    return _dot(g, w, NT), _dot(a, g, TN)


_mm.defvjp(_mm_f, _mm_b)


@functools.partial(jax.custom_vjp, nondiff_argnums=(1,))
def _roll(x, shift):
    return pltpu.roll(x, shift, 1)


def _roll_f(x, shift):
    return pltpu.roll(x, shift, 1), None


def _roll_b(shift, _, g):
    return (pltpu.roll(g, (g.shape[1] - shift) % g.shape[1], 1),)


_roll.defvjp(_roll_f, _roll_b)


def _ln(x, g, b):
    mu = jnp.mean(x, axis=-1, keepdims=True)
    xc = x - mu
    var = jnp.mean(xc * xc, axis=-1, keepdims=True)
    return xc * lax.rsqrt(var + LN_EPS) * g + b


def _rms(x, g):
    ms = jnp.mean(x * x, axis=-1, keepdims=True)
    return x * lax.rsqrt(ms + RMS_EPS) * g


def _sigmoid(x):
    return jax.nn.sigmoid(x)


def _silu(x):
    return x * _sigmoid(x)


def _gelu_tanh(x):
    return x * (0.5 * (1.0 + jnp.tanh(math.sqrt(2.0 / math.pi) * (x + 0.044715 * (x * x * x)))))


def _rowwise_fwd_call(fn, rows, consts, name, tile):
    t = rows[0].shape[0]
    tile = min(tile, t)
    nr = len(rows)
    outs = jax.eval_shape(fn, *[jax.ShapeDtypeStruct((tile, r.shape[1]), F32) for r in rows],
                          *[jax.ShapeDtypeStruct(c.shape, F32) for c in consts])

    def body(*refs):
        vals = [r[...] for r in refs[:nr + len(consts)]]
        res = fn(*vals)
        for o_ref, o in zip(refs[nr + len(consts):], res):
            o_ref[...] = o

    return pl.pallas_call(
        body, name=name, grid=(t // tile,),
        in_specs=[pl.BlockSpec((tile, r.shape[1]), lambda i: (i, 0)) for r in rows]
        + [pl.BlockSpec(c.shape, lambda i: (0, 0)) for c in consts],
        out_specs=[pl.BlockSpec((tile, o.shape[1]), lambda i: (i, 0)) for o in outs],
        out_shape=[jax.ShapeDtypeStruct((t, o.shape[1]), F32) for o in outs],
        compiler_params=_params(("parallel",)),
    )(*rows, *consts)


def _rowwise_bwd_call(fn, rows, consts, douts, row_diff, name, tile, row_grad_dtype=F32):
    t = rows[0].shape[0]
    tile = min(tile, t)
    nr, nc, nd = len(rows), len(consts), len(douts)
    diff_idx = [i for i in range(nr) if row_diff[i]]

    def body(*refs):
        rv = [r[...] for r in refs[:nr]]
        cv = [r[...] for r in refs[nr:nr + nc]]
        dv = [r[...] for r in refs[nr + nc:nr + nc + nd]]
        out_refs = refs[nr + nc + nd:]

        def f(*diff):
            full = list(rv)
            for k, i in enumerate(diff_idx):
                full[i] = diff[k]
            return fn(*full, *diff[len(diff_idx):])

        _, vjp = jax.vjp(f, *[rv[i] for i in diff_idx], *cv)
        grads = vjp(tuple(dv))
        for k in range(len(diff_idx)):
            out_refs[k][...] = grads[k].astype(row_grad_dtype)
        first = pl.program_id(0) == 0
        for k in range(nc):
            acc_ref = out_refs[len(diff_idx) + k]
            g = grads[len(diff_idx) + k]

            @pl.when(first)
            def _(acc_ref=acc_ref, g=g):
                acc_ref[...] = g

            @pl.when(jnp.logical_not(first))
            def _(acc_ref=acc_ref, g=g):
                acc_ref[...] += g

    res = pl.pallas_call(
        body, name=name, grid=(t // tile,),
        in_specs=[pl.BlockSpec((tile, r.shape[1]), lambda i: (i, 0)) for r in rows]
        + [pl.BlockSpec(c.shape, lambda i: (0, 0)) for c in consts]
        + [pl.BlockSpec((tile, d.shape[1]), lambda i: (i, 0)) for d in douts],
        out_specs=[pl.BlockSpec((tile, rows[i].shape[1]), lambda i_: (i_, 0)) for i in diff_idx]
        + [pl.BlockSpec(c.shape, lambda i: (0, 0)) for c in consts],
        out_shape=[jax.ShapeDtypeStruct(rows[i].shape, row_grad_dtype) for i in diff_idx]
        + [jax.ShapeDtypeStruct(c.shape, F32) for c in consts],
        compiler_params=_params(("arbitrary",)),
    )(*rows, *consts, *douts)
    return res[:len(diff_idx)], res[len(diff_idx):]


def make_rowwise(fn, name, row_diff, tile=ROW_TILE):
    @jax.custom_vjp
    def op(rows, consts):
        return tuple(_rowwise_fwd_call(fn, rows, consts, name + "_fwd", tile))

    def fwd(rows, consts):
        return op(rows, consts), (rows, consts)

    def bwd(res, douts):
        rows, consts = res
        drows, dconsts = _rowwise_bwd_call(fn, rows, consts, douts, row_diff, name + "_bwd", tile)
        it = iter(drows)
        full = tuple(next(it) if row_diff[i] else jnp.zeros_like(rows[i]) for i in range(len(rows)))
        return full, tuple(dconsts)

    op.defvjp(fwd, bwd)
    return op


def _conv_post_fn(cv, a_z, ng, nb, w_pw2):
    return (_mm(_silu(_ln(cv, ng, nb)), w_pw2) * _silu(a_z),)


def _mla_prep_fn(c_q, c_kv, krblk, cos4, sin4, qg, kvg, w_uq, w_uk, w_uv):
    qe = _mm(_rms(c_q, qg), w_uq)
    q = qe * cos4 + _roll(qe, qe.shape[1] - 32) * sin4
    cos1, sin1 = cos4[:, :LANES], sin4[:, :LANES]
    kr = krblk * cos1 + _roll(krblk, LANES - 32) * sin1
    kn = _rms(c_kv, kvg)
    k = _mm(kn, w_uk) + jnp.concatenate([kr, kr, kr, kr], axis=1)
    return q, k, _mm(kn, w_uv)


def _ssm_post_fn(y, u, c_z, d, w_a, w_b):
    y2 = _gelu_tanh(y + d * u)
    return (_mm(y2, w_a) * _sigmoid(_mm(y2, w_b)) * _silu(c_z),)


def _merge_fn(br0, br1, br2, br3, gl0, gl1, gl2, gl3, b0, b1, b2, b3):
    return (_sigmoid(gl0 + b0) * br0 + _sigmoid(gl1 + b1) * br1 + _sigmoid(gl2 + b2) * br2 + _sigmoid(gl3 + b3) * br3,)


def _ln_fn(x, mo, g, b):
    return (_ln(DEEPNORM_ALPHA * x + mo, g, b),)


def _ple_fn(x1, pe, gl, g):
    return (x1 + _rms(pe * _sigmoid(gl), g),)


op_conv_post = make_rowwise(_conv_post_fn, "conv_post", (True, True))
op_mla_prep = make_rowwise(_mla_prep_fn, "mla_prep", (True, True, True, False, False))
op_ssm_post = make_rowwise(_ssm_post_fn, "ssm_post", (True, True, True))
MERGE_TILE = ROW_TILE // 2
MERGE_WIDTHS = (D_MODEL,) * 4


@jax.custom_vjp
def op_merge_block(x, ys, zs, w_merge, w_branch, b_merge):
    return _merge_block_fwd(x, ys, zs, w_merge, w_branch, b_merge)[0]


def _merge_block_fwd(x, ys, zs, w_merge, w_branch, b_merge):
    wm, wb = w_merge.astype(MXU_DTYPE), w_branch.astype(MXU_DTYPE)
    gl = _proj_fwd_call(x, wm, MERGE_WIDTHS, "merge_proj_fwd")
    br = [_proj_fwd_call(ys[n], wb[n], (D_MODEL,), "branch_proj_fwd", zs[n])[0] for n in range(4)]
    bm = tuple(b_merge[n * D_MODEL:(n + 1) * D_MODEL].reshape(1, -1) for n in range(4))
    (merged,) = _rowwise_fwd_call(_merge_fn, (*br, *gl), bm, "merge_fwd", MERGE_TILE)
    return merged, (x, ys, zs, wm, wb, tuple(br), tuple(gl), bm, jnp.zeros((0,), w_merge.dtype), jnp.zeros((0,), w_branch.dtype))


def _merge_block_bwd(res, dmerged):
    x, ys, zs, wm, wb, br, gl, bm, wm_like, wb_like = res
    drows, dbm = _rowwise_bwd_call(_merge_fn, (*br, *gl), bm, (dmerged,), (True,) * 8, "merge_bwd", MERGE_TILE, MXU_DTYPE)
    dbr, dgl = drows[:4], drows[4:]
    dx = _proj_dx_call(dgl, wm, MERGE_WIDTHS, "merge_proj_dx")
    dwm = _proj_dw_call(x, dgl, MERGE_WIDTHS, "merge_proj_dw", wm_like.dtype)
    dys, dzs = [], []
    for n in range(4):
        if zs[n] is None:
            dys.append(_proj_dx_call([dbr[n]], wb[n], (D_MODEL,), "branch_proj_dx"))
            dzs.append(None)
        else:
            dy, dz = _proj_dx_call([dbr[n]], wb[n], (D_MODEL,), "branch_proj_dx", (ys[n], zs[n]))
            dys.append(dy)
            dzs.append(dz)
    dwb = jnp.stack([_proj_dw_call(ys[n], [dbr[n]], (D_MODEL,), "branch_proj_dw", wb_like.dtype, zs[n]) for n in range(4)])
    return dx, tuple(dys), tuple(dzs), dwm, dwb, jnp.concatenate([d.reshape(-1) for d in dbm])


op_merge_block.defvjp(_merge_block_fwd, _merge_block_bwd)
op_ln = make_rowwise(_ln_fn, "post_ln", (True, True))
op_ple = make_rowwise(_ple_fn, "ple", (True, True, True))


def _conv_fwd_call(a_val, a_gate, w32, b):
    t, w = a_val.shape
    tile = min(ROW_TILE, t)
    per = tile // CONV_HALO
    cur = pl.BlockSpec((tile, w), lambda i: (i, 0))
    prev = pl.BlockSpec((CONV_HALO, w), lambda i: (jnp.maximum(i * per - 1, 0), 0))

    def body(av_ref, avh_ref, ag_ref, agh_ref, w_ref, b_ref, cv_ref, buf):
        i = pl.program_id(0)
        gh = avh_ref[...] * _sigmoid(agh_ref[...])
        buf[0:CONV_HALO, :] = jnp.where(i > 0, gh, 0.0)
        buf[CONV_HALO:, :] = av_ref[...] * _sigmoid(ag_ref[...])
        acc = jnp.zeros((tile, w), F32) + b_ref[...]
        for j in range(CONV_W):
            acc = acc + w_ref[j:j + 1, :] * buf[pl.ds(CONV_HALO - (CONV_W - 1) + j, tile), :]
        cv_ref[...] = acc

    return pl.pallas_call(
        body, name="conv_fwd", grid=(t // tile,),
        in_specs=[cur, prev, cur, prev, pl.BlockSpec((CONV_HALO, w), lambda i: (0, 0)), pl.BlockSpec((1, w), lambda i: (0, 0))],
        out_specs=cur, out_shape=jax.ShapeDtypeStruct((t, w), F32),
        scratch_shapes=[pltpu.VMEM((tile + CONV_HALO, w), F32)],
        compiler_params=_params(("parallel",)),
    )(a_val, a_val, a_gate, a_gate, w32, b)


def _conv_bwd_call(a_val, a_gate, w32, dcv):
    t, w = a_val.shape
    tile = min(ROW_TILE, t)
    n = t // tile
    per = tile // CONV_HALO
    cur = pl.BlockSpec((tile, w), lambda i: (i, 0))
    prev = pl.BlockSpec((CONV_HALO, w), lambda i: (jnp.maximum(i * per - 1, 0), 0))
    nxt = pl.BlockSpec((CONV_HALO, w), lambda i: (jnp.minimum((i + 1) * per, t // CONV_HALO - 1), 0))
    full = lambda r: pl.BlockSpec((r, w), lambda i: (0, 0))

    def body(av_ref, avh_ref, ag_ref, agh_ref, w_ref, d_ref, dn_ref, dav_ref, dag_ref, dw_ref, db_ref, gbuf, dbuf):
        i = pl.program_id(0)
        gh = avh_ref[...] * _sigmoid(agh_ref[...])
        gbuf[0:CONV_HALO, :] = jnp.where(i > 0, gh, 0.0)
        av = av_ref[...]
        sg = _sigmoid(ag_ref[...])
        gbuf[CONV_HALO:, :] = av * sg
        d = d_ref[...]
        dbuf[0:tile, :] = d
        dbuf[tile:, :] = jnp.where(i < n - 1, dn_ref[...], 0.0)

        @pl.when(i == 0)
        def _():
            dw_ref[...] = jnp.zeros_like(dw_ref)
            db_ref[...] = jnp.zeros_like(db_ref)

        dg = jnp.zeros((tile, w), F32)
        for j in range(CONV_W):
            dg = dg + w_ref[j:j + 1, :] * dbuf[pl.ds(CONV_W - 1 - j, tile), :]
            dw_ref[j:j + 1, :] += jnp.sum(d * gbuf[pl.ds(CONV_HALO - (CONV_W - 1) + j, tile), :], axis=0, keepdims=True)
        db_ref[...] += jnp.sum(d, axis=0, keepdims=True)
        dav_ref[...] = dg * sg
        dag_ref[...] = dg * av * sg * (1.0 - sg)

    return pl.pallas_call(
        body, name="conv_bwd", grid=(n,),
        in_specs=[cur, prev, cur, prev, full(CONV_HALO), cur, nxt],
        out_specs=[cur, cur, full(CONV_HALO), full(1)],
        out_shape=[jax.ShapeDtypeStruct((t, w), F32), jax.ShapeDtypeStruct((t, w), F32),
                   jax.ShapeDtypeStruct((CONV_HALO, w), F32), jax.ShapeDtypeStruct((1, w), F32)],
        scratch_shapes=[pltpu.VMEM((tile + CONV_HALO, w), F32), pltpu.VMEM((tile + CONV_HALO, w), F32)],
        compiler_params=_params(("arbitrary",)),
    )(a_val, a_val, a_gate, a_gate, w32, dcv, dcv)


def _pad_taps(conv_w):
    return jnp.concatenate([conv_w, jnp.zeros((CONV_HALO - CONV_W, conv_w.shape[1]), F32)], axis=0)


@jax.custom_vjp
def op_conv(a_val, a_gate, conv_w, conv_b):
    return _conv_fwd_call(a_val, a_gate, _pad_taps(conv_w), conv_b)


def _op_conv_fwd(a_val, a_gate, conv_w, conv_b):
    return op_conv(a_val, a_gate, conv_w, conv_b), (a_val, a_gate, conv_w)


def _op_conv_bwd(res, dcv):
    a_val, a_gate, conv_w = res
    dav, dag, dw, db = _conv_bwd_call(a_val, a_gate, _pad_taps(conv_w), dcv)
    return dav, dag, dw[:CONV_W], db


op_conv.defvjp(_op_conv_fwd, _op_conv_bwd)


def _head_masks(rows):
    lane = lax.broadcasted_iota(jnp.int32, (rows, LANES), 1)
    return lane < 64, lane >= 64


def _head_row(vals, mask):
    return jnp.max(jnp.where(mask, vals, NEG), axis=1, keepdims=True)


def _attn_valid(qpos, kpos, window):
    valid = kpos <= qpos
    if window is not None:
        valid = jnp.logical_and(valid, qpos - kpos < window)
    return valid


def _flash_fwd_call(q, k, v, sink, *, window, shared_k, scale, blk, blk_q, name):
    t = q.shape[0]
    qw = LANES if shared_k else 2 * LANES
    pairs = v.shape[1] // LANES
    tk = min(blk, t)
    tq = min(blk_q, t)
    has_sink = sink is not None

    def body(*refs):
        if has_sink:
            q_ref, k_ref, v_ref, s_ref, o_ref, lse_ref, k_mxu, v0_mxu, v1_mxu = refs
        else:
            q_ref, k_ref, v_ref, o_ref, lse_ref, k_mxu, v0_mxu, v1_mxu = refs
        v_mxu = (v0_mxu, v1_mxu)
        i = pl.program_id(1)

        @pl.when(i == 0)
        def _():
            full_masks = _head_masks(t)
            k_mxu[...] = k_ref[...].astype(MXU_DTYPE)
            for h in range(2):
                v_mxu[h][...] = jnp.where(full_masks[h], v_ref[...], 0.0).astype(MXU_DTYPE)

        qb = q_ref[...]
        masks = _head_masks(tq)
        row_masks = _head_masks(1)
        qh = [(jnp.where(masks[h], qb, 0.0) if shared_k else qb[:, h * LANES:(h + 1) * LANES]).astype(MXU_DTYPE) for h in range(2)]
        qpos = i * tq + lax.broadcasted_iota(jnp.int32, (tq, tk), 0)
        if has_sink:
            m_init = [jnp.zeros((tq, 1), F32) + _head_row(s_ref[...], row_masks[h]) for h in range(2)]
            l_init = [jnp.ones((tq, 1), F32)] * 2
        else:
            m_init = [jnp.full((tq, 1), NEG, F32)] * 2
            l_init = [jnp.zeros((tq, 1), F32)] * 2

        def make_step(masked):
            def step(j, carry):
                m0, l0, m1, l1, acc = carry
                start = pl.multiple_of(j * tk, tk)
                kb = k_mxu[pl.ds(start, tk), :]
                if masked:
                    valid = _attn_valid(qpos, j * tk + lax.broadcasted_iota(jnp.int32, (tq, tk), 1), window)
                new, alphas, pv = [], [], []
                for h, (m, l) in enumerate(((m0, l0), (m1, l1))):
                    kh = kb if shared_k else kb[:, h * LANES:(h + 1) * LANES]
                    s = _dot(qh[h], kh, NT) * scale
                    if masked:
                        s = jnp.where(valid, s, NEG)
                    m_new = jnp.maximum(m, jnp.max(s, axis=1, keepdims=True))
                    alpha = jnp.exp(m - m_new)
                    p = jnp.exp(s - m_new)
                    new += [m_new, alpha * l + jnp.sum(p, axis=1, keepdims=True)]
                    alphas.append(alpha)
                    pv.append(_dot(p, v_mxu[h][pl.ds(start, tk), :], NN))
                acc = acc * jnp.where(masks[0], alphas[0], alphas[1]) + pv[0] + pv[1]
                return new[0], new[1], new[2], new[3], acc
            return step

        carry = (m_init[0], l_init[0], m_init[1], l_init[1], jnp.zeros((tq, LANES), F32))
        last = (i * tq + tq - 1) // tk
        if window is None:
            n_full = (i * tq + 1) // tk
            carry = lax.fori_loop(0, n_full, make_step(False), carry)
            carry = lax.fori_loop(n_full, last + 1, make_step(True), carry)
        else:
            carry = lax.fori_loop(jnp.maximum(i * tq - (window - 1), 0) // tk, last + 1, make_step(True), carry)
        m0, l0, m1, l1, acc = carry
        o_ref[...] = acc / jnp.where(masks[0], l0, l1)
        lse_ref[...] = jnp.where(masks[0], m0 + jnp.log(l0), m1 + jnp.log(l1))

    in_specs = [pl.BlockSpec((tq, qw), lambda p, i: (i, p)), pl.BlockSpec((t, qw), lambda p, i: (0, p)),
                pl.BlockSpec((t, LANES), lambda p, i: (0, p))]
    args = [q, k, v]
    if has_sink:
        in_specs.append(pl.BlockSpec((1, LANES), lambda p, i: (0, p)))
        args.append(sink)
    blk_o = pl.BlockSpec((tq, LANES), lambda p, i: (i, p))
    return pl.pallas_call(
        body, name=name, grid=(pairs, t // tq), in_specs=in_specs, out_specs=[blk_o, blk_o],
        out_shape=[jax.ShapeDtypeStruct((t, pairs * LANES), F32)] * 2,
        scratch_shapes=[pltpu.VMEM((t, qw), MXU_DTYPE), pltpu.VMEM((t, LANES), MXU_DTYPE), pltpu.VMEM((t, LANES), MXU_DTYPE)],
        compiler_params=_params(("arbitrary", "arbitrary")),
    )(*args)


def _flash_bwd_call(q, k, v, sink, o, lse, do, *, window, shared_k, scale, blk, blk_q, name):
    t = q.shape[0]
    qw = LANES if shared_k else 2 * LANES
    pairs = v.shape[1] // LANES
    tk = min(blk, t)
    tq = min(blk_q, t)
    assert tk % tq == 0 or tq % tk == 0
    nq = t // tq
    has_sink = sink is not None

    def body(*refs):
        if has_sink:
            q_ref, k_ref, v_ref, o_ref, lse_ref, do_ref, s_ref, dq_ref, dk_ref, dv_ref, ds_ref = refs[:11]
        else:
            q_ref, k_ref, v_ref, o_ref, lse_ref, do_ref, dq_ref, dk_ref, dv_ref = refs[:9]
        q_mxu, do_mxu, lse_h, dsum_h = refs[-8:-6], refs[-6:-4], refs[-4:-2], refs[-2:]
        j = pl.program_id(1)
        masks = _head_masks(tq)
        row_masks = _head_masks(1)

        @pl.when(j == 0)
        def _():
            dq_ref[...] = jnp.zeros_like(dq_ref)
            full_masks = _head_masks(t)
            prod = do_ref[...] * o_ref[...]
            parts = []
            for h in range(2):
                qh = jnp.where(full_masks[h], q_ref[...], 0.0) if shared_k else q_ref[:, h * LANES:(h + 1) * LANES]
                q_mxu[h][...] = qh.astype(MXU_DTYPE)
                do_mxu[h][...] = jnp.where(full_masks[h], do_ref[...], 0.0).astype(MXU_DTYPE)
                dsum = jnp.sum(jnp.where(full_masks[h], prod, 0.0), axis=1, keepdims=True)
                lse = _head_row(lse_ref[...], full_masks[h])
                dsum_h[h][...] = jnp.zeros((t, LANES), F32) + dsum
                lse_h[h][...] = jnp.zeros((t, LANES), F32) + lse
                if has_sink:
                    ps = jnp.exp(_head_row(s_ref[...], row_masks[h]) - lse)
                    parts.append(-jnp.sum(ps * dsum, axis=0, keepdims=True))
            if has_sink:
                ds_ref[...] = jnp.zeros((SUBLANES, LANES), F32) + jnp.where(row_masks[0], parts[0], parts[1])

        kb = k_ref[...].astype(MXU_DTYPE)
        vb = v_ref[...].astype(MXU_DTYPE)
        kh = [kb if shared_k else kb[:, h * LANES:(h + 1) * LANES] for h in range(2)]
        kpos = j * tk + lax.broadcasted_iota(jnp.int32, (tq, tk), 1)
        lanes_of = lambda a: a if tk == LANES else jnp.concatenate([a] * (tk // LANES), axis=1)

        def make_step(masked):
            def step(i, carry):
                dk0, dk1, dv = carry
                start = pl.multiple_of(i * tq, tq)
                if masked:
                    valid = _attn_valid(i * tq + lax.broadcasted_iota(jnp.int32, (tq, tk), 0), kpos, window)
                dks, dqs = [], []
                for h in range(2):
                    qh = q_mxu[h][pl.ds(start, tq), :]
                    doh = do_mxu[h][pl.ds(start, tq), :]
                    s = _dot(qh, kh[h], NT) * scale
                    if masked:
                        s = jnp.where(valid, s, NEG)
                    p = jnp.exp(s - lanes_of(lse_h[h][pl.ds(start, tq), :]))
                    dp = _dot(doh, vb, NT)
                    dsc = p * (dp - lanes_of(dsum_h[h][pl.ds(start, tq), :])) * scale
                    dv = dv + _dot(p, doh, TN)
                    dks.append(_dot(dsc, qh, TN))
                    dq_h = _dot(dsc, kh[h], NN)
                    dqs.append(jnp.where(masks[h], dq_h, 0.0) if shared_k else dq_h)
                if shared_k:
                    dq_ref[pl.ds(start, tq), :] += dqs[0] + dqs[1]
                else:
                    dq_ref[pl.ds(start, tq), :] += jnp.concatenate(dqs, axis=1)
                return dk0 + dks[0], dk1 + dks[1], dv
            return step

        zero = jnp.zeros((tk, LANES), F32)
        carry = (zero, zero, zero)
        first = (j * tk) // tq
        if window is None:
            n_full = jnp.minimum(((j + 1) * tk + tq - 2) // tq, nq)
            carry = lax.fori_loop(first, n_full, make_step(True), carry)
            carry = lax.fori_loop(n_full, nq, make_step(False), carry)
        else:
            carry = lax.fori_loop(first, jnp.minimum(nq, (j * tk + tk - 1 + window - 1) // tq + 1), make_step(True), carry)
        dk0, dk1, dv = carry
        dk_ref[...] = dk0 + dk1 if shared_k else jnp.concatenate([dk0, dk1], axis=1)
        dv_ref[...] = dv

    full = lambda w: pl.BlockSpec((t, w), lambda p, j: (0, p))
    blkspec = lambda w: pl.BlockSpec((tk, w), lambda p, j: (j, p))
    in_specs = [full(qw), blkspec(qw), blkspec(LANES), full(LANES), full(LANES), full(LANES)]
    args = [q, k, v, o, lse, do]
    out_specs = [full(qw), blkspec(qw), blkspec(LANES)]
    out_shape = [jax.ShapeDtypeStruct(q.shape, F32), jax.ShapeDtypeStruct(k.shape, F32), jax.ShapeDtypeStruct(v.shape, F32)]
    if has_sink:
        in_specs.append(pl.BlockSpec((1, LANES), lambda p, j: (0, p)))
        args.append(sink)
        out_specs.append(pl.BlockSpec((SUBLANES, LANES), lambda p, j: (0, p)))
        out_shape.append(jax.ShapeDtypeStruct((SUBLANES, pairs * LANES), F32))
    return pl.pallas_call(
        body, name=name, grid=(pairs, t // tk), in_specs=in_specs, out_specs=out_specs, out_shape=out_shape,
        scratch_shapes=[pltpu.VMEM((t, LANES), MXU_DTYPE)] * 4 + [pltpu.VMEM((t, LANES), F32)] * 4,
        compiler_params=_params(("arbitrary", "arbitrary")),
    )(*args)


_MLA_CFG = dict(window=None, shared_k=False, scale=MLA_SCALE, blk=256)
_SWA_CFG = dict(window=WINDOW, shared_k=True, scale=SWA_SCALE, blk=128)
_MLA_FWD_CFG = dict(_MLA_CFG, blk=512, blk_q=256)
_SWA_FWD_CFG = dict(_SWA_CFG, blk=256, blk_q=256)
_MLA_BWD_CFG = dict(_MLA_CFG, blk=512, blk_q=512)
_SWA_BWD_CFG = dict(_SWA_CFG, blk=256, blk_q=256)


@jax.custom_vjp
def op_mla_attn(q, k, v):
    return _flash_fwd_call(q, k, v, None, name="mla_fwd", **_MLA_FWD_CFG)[0]


def _op_mla_attn_fwd(q, k, v):
    o, lse = _flash_fwd_call(q, k, v, None, name="mla_fwd", **_MLA_FWD_CFG)
    return o, (q, k, v, o, lse)


def _op_mla_attn_bwd(res, do):
    q, k, v, o, lse = res
    return tuple(_flash_bwd_call(q, k, v, None, o, lse, do, name="mla_bwd", **_MLA_BWD_CFG))


op_mla_attn.defvjp(_op_mla_attn_fwd, _op_mla_attn_bwd)


@jax.custom_vjp
def op_swa_attn(q, k, v, sink):
    return _flash_fwd_call(q, k, v, sink, name="swa_fwd", **_SWA_FWD_CFG)[0]


def _op_swa_attn_fwd(q, k, v, sink):
    o, lse = _flash_fwd_call(q, k, v, sink, name="swa_fwd", **_SWA_FWD_CFG)
    return o, (q, k, v, sink, o, lse)


def _op_swa_attn_bwd(res, do):
    q, k, v, sink, o, lse = res
    dq, dk, dv, dsink = _flash_bwd_call(q, k, v, sink, o, lse, do, name="swa_bwd", **_SWA_BWD_CFG)
    first_lane = lax.broadcasted_iota(jnp.int32, (1, dsink.shape[1]), 1) % 64 == 0
    return dq, dk, dv, jnp.where(first_lane, dsink[:1], 0.0)


op_swa_attn.defvjp(_op_swa_attn_fwd, _op_swa_attn_bwd)


def _complex_power(ar, ai, n):
    for _ in range(int(math.log2(n))):
        ar, ai = ar * ar - ai * ai, 2.0 * ar * ai
    return ar, ai


def _scan_passes(load_b, a1r, a1i, n, store, e_ref, c_ref, reverse):
    cb = a1r.shape[1]
    ar = jnp.zeros((SCAN_SEGMENTS, cb), F32) + a1r
    ai = jnp.zeros((SCAN_SEGMENTS, cb), F32) + a1i
    a2r, a2i = ar * ar - ai * ai, 2.0 * ar * ai
    idx = (lambda k: n - 1 - k) if reverse else (lambda k: k)
    mac = lambda pr_, pi__, h, b: (pr_ * h[0] - pi__ * h[1] + b[0], pr_ * h[1] + pi__ * h[0] + b[1])

    def load_pair(ii):
        k = jnp.minimum(2 * ii, n - 2)
        b0, b1 = load_b(idx(k)), load_b(idx(k + 1))
        return b0, mac(ar, ai, b0, b1)

    def local(ii, carry):
        h, c = carry
        return mac(a2r, a2i, h, c), load_pair(ii + 1)[1]

    zero = jnp.zeros((SCAN_SEGMENTS, cb), F32)
    (er, ei), _ = lax.fori_loop(0, n // 2, local, ((zero, zero), load_pair(0)[1]))
    e_ref[:, 0:cb] = er
    e_ref[:, cb:] = ei
    pr, pi_ = _complex_power(a1r, a1i, n)
    cr = jnp.zeros((1, cb), F32)
    ci = jnp.zeros((1, cb), F32)
    order = range(SCAN_SEGMENTS - 1, -1, -1) if reverse else range(SCAN_SEGMENTS)
    for s in order:
        c_ref[s:s + 1, 0:cb] = cr
        c_ref[s:s + 1, cb:] = ci
        er1, ei1 = e_ref[s:s + 1, 0:cb], e_ref[s:s + 1, cb:]
        cr, ci = pr * cr - pi_ * ci + er1, pr * ci + pi_ * cr + ei1

    def second(ii, carry):
        h, b0, c = carry
        h0 = mac(ar, ai, h, b0)
        h1 = mac(a2r, a2i, h, c)
        store(idx(2 * ii), *h0)
        store(idx(2 * ii + 1), *h1)
        nb0, nc = load_pair(ii + 1)
        return h1, nb0, nc

    b0, c0 = load_pair(0)
    lax.fori_loop(0, n // 2, second, ((c_ref[:, 0:cb], c_ref[:, cb:]), b0, c0))


def _scan_fwd_call(bu, lam):
    n = bu.shape[0]
    cb = SCAN_CB
    blk3 = pl.BlockSpec((n, SCAN_SEGMENTS, 2 * cb), lambda c: (0, 0, c))
    blk2 = lambda r: pl.BlockSpec((r, 2 * cb), lambda c: (0, c))

    def body(b_ref, lam_ref, h_ref, cin_ref, e_ref):
        def store(i, hr, hi):
            h_ref[i, :, 0:cb] = hr
            h_ref[i, :, cb:] = hi

        _scan_passes(lambda i: (b_ref[i, :, 0:cb], b_ref[i, :, cb:]), lam_ref[:, 0:cb], lam_ref[:, cb:], n, store,
                     e_ref, cin_ref, False)

    return pl.pallas_call(
        body, name="scan_fwd", grid=(SSM_CH // cb,), in_specs=[blk3, blk2(1)], out_specs=[blk3, blk2(SCAN_SEGMENTS)],
        out_shape=[jax.ShapeDtypeStruct(bu.shape, F32), jax.ShapeDtypeStruct((SCAN_SEGMENTS, 2 * SSM_CH), F32)],
        scratch_shapes=[pltpu.VMEM((SCAN_SEGMENTS, 2 * cb), F32)],
        compiler_params=_params(("parallel",)),
    )(bu, lam)


def _scan_bwd_call(dh, h, cin, lam):
    n = dh.shape[0]
    cb = SCAN_CB
    blk3 = pl.BlockSpec((n, SCAN_SEGMENTS, 2 * cb), lambda c: (0, 0, c))
    blk2 = lambda r: pl.BlockSpec((r, 2 * cb), lambda c: (0, c))

    def body(d_ref, h_ref, cin_ref, lam_ref, g_ref, dlam_ref, e_ref, c_ref, acc_ref):
        acc_ref[...] = jnp.zeros_like(acc_ref)

        def store(i, gr, gi):
            g_ref[i, :, 0:cb] = gr
            g_ref[i, :, cb:] = gi
            ip = jnp.maximum(i - 1, 0)
            hpr = jnp.where(i > 0, h_ref[ip, :, 0:cb], cin_ref[:, 0:cb])
            hpi = jnp.where(i > 0, h_ref[ip, :, cb:], cin_ref[:, cb:])
            acc_ref[:, 0:cb] += gr * hpr + gi * hpi
            acc_ref[:, cb:] += gi * hpr - gr * hpi

        _scan_passes(lambda i: (d_ref[i, :, 0:cb], d_ref[i, :, cb:]), lam_ref[:, 0:cb], -lam_ref[:, cb:], n, store,
                     e_ref, c_ref, True)
        dlam_ref[...] = acc_ref[...]

    return pl.pallas_call(
        body, name="scan_bwd", grid=(SSM_CH // cb,), in_specs=[blk3, blk3, blk2(SCAN_SEGMENTS), blk2(1)],
        out_specs=[blk3, blk2(SCAN_SEGMENTS)],
        out_shape=[jax.ShapeDtypeStruct(dh.shape, F32), jax.ShapeDtypeStruct((SCAN_SEGMENTS, 2 * SSM_CH), F32)],
        scratch_shapes=[pltpu.VMEM((SCAN_SEGMENTS, 2 * cb), F32)] * 3,
        compiler_params=_params(("parallel",)),
    )(dh, h, cin, lam)


@jax.custom_vjp
def op_scan(bu, lam):
    return _scan_fwd_call(bu, lam)[0]


def _op_scan_fwd(bu, lam):
    h, cin = _scan_fwd_call(bu, lam)
    return h, (h, cin, lam)


def _op_scan_bwd(res, dh):
    h, cin, lam = res
    g, dlam = _scan_bwd_call(dh, h, cin, lam)
    return g, jnp.sum(dlam, axis=0, keepdims=True)


op_scan.defvjp(_op_scan_fwd, _op_scan_bwd)


def _loss_call(y, target):
    t, d = y.shape
    tile = min(ROW_TILE, t)

    def body(y_ref, t_ref, dy_ref, acc_ref):
        @pl.when(pl.program_id(0) == 0)
        def _():
            acc_ref[...] = jnp.zeros_like(acc_ref)

        err = y_ref[...] - t_ref[...]
        dy_ref[...] = err * (1.0 / d)
        col = jnp.sum(err * err, axis=0, keepdims=True)
        part = col[:, 0:LANES]
        for c in range(1, d // LANES):
            part = part + col[:, c * LANES:(c + 1) * LANES]
        acc_ref[0:1, :] += part

    blk = pl.BlockSpec((tile, d), lambda i: (i, 0))
    dy, acc = pl.pallas_call(
        body, name="loss_head", grid=(t // tile,), in_specs=[blk, blk],
        out_specs=[blk, pl.BlockSpec((SUBLANES, LANES), lambda i: (0, 0))],
        out_shape=[jax.ShapeDtypeStruct((t, d), F32), jax.ShapeDtypeStruct((SUBLANES, LANES), F32)],
        compiler_params=_params(("arbitrary",)),
    )(y, target)
    return jnp.sum(acc) * (0.5 / d), dy


def _rot_cols(w, xp=jnp):
    return xp.concatenate([-w[:, 16:], w[:, :16]], axis=1)


def _ext_w_in(w, xp=jnp):
    a_val, a_gate, a_z, c_q, c_kv, k_r, b_z, u, c_z, q, k, v, d_z = xp.split(
        w, (256, 512, 768, 1024, 1152, 1184, 1440, 1696, 1952, 2208, 2336, 2464), axis=1)
    dup = lambda m: xp.concatenate([m[:, :64], m[:, :64], m[:, 64:], m[:, 64:]], axis=1)
    krblk = xp.concatenate([xp.zeros((w.shape[0], 64), w.dtype), k_r, _rot_cols(k_r, xp)], axis=1)
    return xp.concatenate([a_val, a_gate, a_z, c_q, b_z, u, c_z, q, dup(k), dup(v), d_z, c_kv, krblk], axis=1)


IN_WIDTH = 2720
IN_SHARD = IN_WIDTH // 4
IN_SHARD_PAD = 768
IN_EXT = 3072


BAND = 512


def _w_in_layout():
    src = _ext_w_in(np.arange(1, IN_WIDTH + 1, dtype=np.float32)[None, :], np)[0]
    col = np.abs(src).astype(np.int64) - 1
    row = np.where(col >= 0, (col // IN_SHARD) * IN_SHARD_PAD + col % IN_SHARD, -1)
    return row, np.sign(src)


def _w_in_layout_matrix():
    row, sign = _w_in_layout()
    rows = lax.broadcasted_iota(jnp.int32, (4 * IN_SHARD_PAD, IN_EXT), 0)
    return jnp.where(rows == jnp.asarray(row, jnp.int32)[None, :], jnp.asarray(sign, F32)[None, :], 0.0).astype(MXU_DTYPE)


def _band_tables():
    row, _ = _w_in_layout()
    nb = IN_EXT // BAND
    hit = np.zeros((nb, nb), bool)
    for c, r in enumerate(row):
        if r >= 0:
            hit[r // BAND, c // BAND] = True

    def table(h):
        depth = int(h.sum(axis=1).max())
        rows = []
        for o in range(nb):
            used = [int(b) for b in np.nonzero(h[o])[0]]
            spare = [b for b in range(nb) if not h[o, b]]
            rows.append(used + spare[:depth - len(used)])
        return np.asarray(rows, np.int32), depth

    return table(hit.T), table(hit)


def _band_mm_call(a, e, table, depth, e_transposed, name, out_dtype):
    m = a.shape[0]
    nb = IN_EXT // BAND
    dims = NT if e_transposed else NN

    def body(t_ref, a_ref, e_ref, o_ref, acc_ref):
        kk = pl.program_id(1)

        @pl.when(kk == 0)
        def _():
            acc_ref[...] = jnp.zeros_like(acc_ref)

        acc_ref[...] += _dot(a_ref[...], e_ref[...], dims)

        @pl.when(kk == depth - 1)
        def _():
            o_ref[...] = acc_ref[...].astype(out_dtype)

    blk = lambda o, kk, t: t[o * depth + kk]
    e_spec = pl.BlockSpec((BAND, BAND), (lambda o, kk, t: (o, blk(o, kk, t))) if e_transposed else (lambda o, kk, t: (blk(o, kk, t), o)))
    return pl.pallas_call(
        body, name=name, out_shape=jax.ShapeDtypeStruct((m, IN_EXT), out_dtype),
        grid_spec=pltpu.PrefetchScalarGridSpec(
            num_scalar_prefetch=1, grid=(nb, depth),
            in_specs=[pl.BlockSpec((m, BAND), lambda o, kk, t: (0, blk(o, kk, t))), e_spec],
            out_specs=pl.BlockSpec((m, BAND), lambda o, kk, t: (0, o)),
            scratch_shapes=[pltpu.VMEM((m, BAND), F32)]),
        compiler_params=_params(("parallel", "arbitrary")),
    )(jnp.asarray(table.reshape(-1)), a, e)


@jax.custom_vjp
def op_w_in_ext(w_pad, e):
    (table, depth), _ = _band_tables()
    return _band_mm_call(w_pad, e, table, depth, False, "w_in_ext", F32)


def _op_w_in_ext_fwd(w_pad, e):
    return op_w_in_ext(w_pad, e), (e, jnp.zeros((0,), w_pad.dtype))


def _op_w_in_ext_bwd(res, g):
    e, w_like = res
    _, (table, depth) = _band_tables()
    return _band_mm_call(g, e, table, depth, True, "w_in_ext_bwd", w_like.dtype), jnp.zeros_like(e)


op_w_in_ext.defvjp(_op_w_in_ext_fwd, _op_w_in_ext_bwd)


H_COLS = dict(a_val=256, a_gate=256, a_z=256, c_q=256, b_z=256, u=256, c_z=256, q=256, kdup=256, vdup=256, d_z=256,
              c_kv=128, krblk=128)
op_in_proj = make_proj(tuple(H_COLS.values()), "in_proj")


def _ext_mla(w_uq, w_ukv):
    zeros = jnp.zeros((w_ukv.shape[0], 64), w_ukv.dtype)
    uq, uk, uv = [], [], []
    for h in range(4):
        nope, rp = w_uq[:, 96 * h:96 * h + 64], w_uq[:, 96 * h + 64:96 * h + 96]
        uq += [nope, rp, _rot_cols(rp)]
        uk += [w_ukv[:, 128 * h:128 * h + 64], zeros]
        uv.append(w_ukv[:, 128 * h + 64:128 * h + 128])
    return jnp.concatenate(uq, axis=1), jnp.concatenate(uk, axis=1), jnp.concatenate(uv, axis=1)


def _scan_cols(re, im):
    parts = []
    for c in range(SSM_CH // SCAN_CB):
        parts += [re[..., c * SCAN_CB:(c + 1) * SCAN_CB], im[..., c * SCAN_CB:(c + 1) * SCAN_CB]]
    return jnp.concatenate(parts, axis=-1)


def _ext_ssm(a_re, a_im, log_dt, b_re, b_im, c_re, c_im):
    dt = jnp.exp(log_dt)[:, None]
    mag = jnp.exp(a_re * dt)
    lb_re, lb_im = mag * jnp.cos(a_im * dt), mag * jnp.sin(a_im * dt)
    den = a_re * a_re + a_im * a_im
    nr, ni = lb_re - 1.0, lb_im
    f_re = ((nr * a_re + ni * a_im) / den)[..., None]
    f_im = ((ni * a_re - nr * a_im) / den)[..., None]
    bb_re = f_re * b_re - f_im * b_im
    bb_im = f_re * b_im + f_im * b_re
    eye = jnp.eye(SSM_GROUPS, dtype=F32)
    spread = lambda a: a.transpose(0, 2, 1)[:, :, None, :] * eye[:, None, :, None]
    bd_in = lambda bb: spread(bb).reshape(SSM_GROUPS * SSM_GROUP, SSM_CH)
    bd_out = lambda cc: spread(cc).reshape(SSM_CH, SSM_GROUPS * SSM_GROUP)
    w_bu = _scan_cols(bd_in(bb_re), bd_in(bb_im))
    w_y = _scan_cols(bd_out(c_re).T, -bd_out(c_im).T).T
    lam = _scan_cols(lb_re.reshape(1, SSM_CH), lb_im.reshape(1, SSM_CH))
    return w_bu, w_y, lam


def _rope_tables(t):
    pos = jnp.arange(t, dtype=F32)
    inv_freq = ROPE_THETA ** (-jnp.arange(0, 32, 2, dtype=F32) / 32)
    ang = pos[:, None] * inv_freq[None, :]
    cos, sin = jnp.cos(ang), jnp.sin(ang)
    ones, z32, z64 = jnp.ones((t, 64), F32), jnp.zeros((t, 32), F32), jnp.zeros((t, 64), F32)
    cos1 = jnp.concatenate([ones, cos, cos, z32], axis=1)
    sin1 = jnp.concatenate([z64, sin, sin, z32], axis=1)
    return jnp.concatenate([cos1] * 4, axis=1), jnp.concatenate([sin1] * 4, axis=1)


def _to_segments(a):
    t, w = a.shape
    return a.reshape(SCAN_SEGMENTS, t // SCAN_SEGMENTS, w).transpose(1, 0, 2)


def _from_segments(a):
    n, s, w = a.shape
    return a.transpose(1, 0, 2).reshape(n * s, w)


def _layer(x, p_i, cos4, sin4, e_mat, w):
    t = x.shape[0]
    row = lambda v: v.reshape(1, -1)
    f32 = lambda v: v.astype(F32)
    hs = dict(zip(H_COLS, op_in_proj(x, op_w_in_ext(w["w_in_pad"], e_mat))))

    cv = op_conv(hs["a_val"], hs["a_gate"], w["conv_w"], row(w["conv_b"]))
    (y_a,) = op_conv_post((cv, hs["a_z"]), (row(w["conv_norm_g"]), row(w["conv_norm_b"]), f32(w["w_pw2"])))

    w_uq, w_uk, w_uv = _ext_mla(w["w_uq"], f32(w["w_ukv"]))
    q, k, v = op_mla_prep((hs["c_q"], hs["c_kv"], hs["krblk"], cos4, sin4),
                          (row(w["mla_q_norm_g"]), row(w["mla_kv_norm_g"]), w_uq, w_uk, w_uv))
    o_b = op_mla_attn(q, k, v)

    w_bu, w_y, lam = _ext_ssm(w["ssm_a_re"], w["ssm_a_im"], w["ssm_log_dt"], w["ssm_b_re"], w["ssm_b_im"],
                              w["ssm_c_re"], w["ssm_c_im"])
    u_seg = _to_segments(hs["u"]).reshape(t, BRANCH_W)
    bu = op_mm(u_seg, w_bu).reshape(t // SCAN_SEGMENTS, SCAN_SEGMENTS, 2 * SSM_CH)
    hstate = op_scan(bu, lam).reshape(t, 2 * SSM_CH)
    y_ssm = _from_segments(op_mm(hstate, w_y).reshape(t // SCAN_SEGMENTS, SCAN_SEGMENTS, BRANCH_W))
    w_glu = f32(w["w_glu"])
    (y_c,) = op_ssm_post((y_ssm, hs["u"], hs["c_z"]), (row(w["ssm_d"]), w_glu[:, :BRANCH_W], w_glu[:, BRANCH_W:]))

    sink = jnp.repeat(w["attn_sinks"], 64).reshape(1, 2 * LANES)
    o_d = op_swa_attn(hs["q"], hs["kdup"], hs["vdup"], sink)

    merged = op_merge_block(x, (y_a, o_b, y_c, o_d), (None, hs["b_z"], None, hs["d_z"]), w["w_merge"], w["w_branch"], w["b_merge"])
    (x1,) = op_ln((x, op_mm(merged, w["w_out"])), (row(w["ln_g"]), row(w["ln_b"])))
    (out,) = op_ple((x1, op_mm(p_i, w["w_ple"]), op_mm(x1, w["w_ple_gate"])), (row(w["ple_norm_g"]),))
    return out


def _forward(x, p, layers):
    cos4, sin4 = _rope_tables(x.shape[0])
    e_mat = _w_in_layout_matrix()
    for i in range(DEPTH):
        x = _layer(x, p[i], cos4, sin4, e_mat, layers[i])
    return x


SHARD_AXIS = dict(w_in=2, w_merge=2, conv_w=2, w_pw2=1, w_uq=2, w_ukv=2, w_glu=2, w_branch=3, w_out=1, w_ple=2, w_ple_gate=1)
ODD = ("w_uq", "conv_w")
BIG = tuple(n for n in SHARD_AXIS if n not in ODD)
REPLICATED = ("b_merge", "conv_b", "conv_norm_g", "conv_norm_b", "mla_q_norm_g", "mla_kv_norm_g", "ssm_a_re", "ssm_a_im",
              "ssm_log_dt", "ssm_b_re", "ssm_b_im", "ssm_c_re", "ssm_c_im", "ssm_d", "attn_sinks", "ln_g", "ln_b", "ple_norm_g")
WEIGHTS = ("w_in", "w_merge", "b_merge", "conv_w", "conv_b", "conv_norm_g", "conv_norm_b", "w_pw2", "mla_q_norm_g",
           "mla_kv_norm_g", "w_uq", "w_ukv", "ssm_a_re", "ssm_a_im", "ssm_log_dt", "ssm_b_re", "ssm_b_im", "ssm_c_re",
           "ssm_c_im", "ssm_d", "w_glu", "attn_sinks", "w_branch", "w_out", "ln_g", "ln_b", "w_ple", "w_ple_gate", "ple_norm_g")
PACK_COLS = 1024
PACK_ROWS = 16
CHIP_FLIPS = ((1, 0), (0, 1), (1, 1))
N_CHIPS = 4
N_DEV = 8


def _pack_rows(n):
    return -(-n // (SUBLANES * PACK_COLS)) * SUBLANES


def _pack(arrays, dtype):
    blocks, rows = [], 0
    for a in arrays:
        r = _pack_rows(a.size)
        flat = a.reshape(-1).astype(dtype)
        blocks.append(jnp.pad(flat, (0, r * PACK_COLS - a.size)).reshape(r, PACK_COLS))
        rows += r
    pad = -rows % PACK_ROWS
    if pad:
        blocks.append(jnp.zeros((pad, PACK_COLS), dtype))
    return jnp.concatenate(blocks, axis=0)


def _unpack(buf, shapes):
    out, row = [], 0
    for s in shapes:
        n = math.prod(s)
        r = _pack_rows(n)
        out.append(buf[row:row + r].reshape(-1)[:n].reshape(s))
        row += r
    return out


def _flip(v, bit):
    return 1 - v if bit else v


def _window(ref, axis, start, size):
    idx = [slice(None)] * len(ref.shape)
    idx[axis] = pl.ds(start, size)
    return ref.at[tuple(idx)]


def _gather_chips(srcs, axes, stacked):
    units = []
    for k, (s, a) in enumerate(zip(srcs, axes)):
        if stacked[k]:
            units += [(k, l, s.shape[1:], a - 1) for l in range(s.shape[0])]
        else:
            units.append((k, None, s.shape, a))
    nu, nb = len(units), len(srcs)

    def body(*refs):
        ins, outs = refs[:nb], refs[nb:nb + nu]
        ici_send, ici_recv, d2d_send, d2d_recv, local_sems = refs[nb + nu:]
        x, y, c = lax.axis_index("x"), lax.axis_index("y"), lax.axis_index("c")
        me = 2 * x + y

        def mine(u, half=None):
            k, l, shape, _ = units[u]
            ref = ins[k] if l is None else ins[k].at[l]
            return ref if half is None else ref.at[pl.ds(half * (shape[0] // 2), shape[0] // 2)]

        def place(u, chip, half=None):
            _, _, shape, a = units[u]
            size, rows = shape[a], shape[0] // 2
            if half is None:
                return _window(outs[u], a, chip * size, size)
            if a == 0:
                return outs[u].at[pl.ds(chip * size + half * rows, rows)]
            return _window(outs[u].at[pl.ds(half * rows, rows)], a, chip * size, size)

        local = [pltpu.make_async_copy(mine(u), place(u, me), local_sems.at[u]) for u in range(nu)]
        for cp in local:
            cp.start()
        sends = []
        for j, (bx, by) in enumerate(CHIP_FLIPS):
            for u in range(nu):
                cp = pltpu.make_async_remote_copy(src_ref=mine(u, c), dst_ref=place(u, me, c),
                                                  send_sem=ici_send.at[j * nu + u], recv_sem=ici_recv.at[j * nu + u],
                                                  device_id=(_flip(x, bx), _flip(y, by), c), device_id_type=MESH)
                cp.start()
                sends.append(cp)
        for j, (bx, by) in enumerate(CHIP_FLIPS):
            src = 2 * _flip(x, bx) + _flip(y, by)
            for u in range(nu):
                got = place(u, src, c)
                pltpu.make_async_remote_copy(src_ref=got, dst_ref=got, send_sem=ici_send.at[j * nu + u],
                                             recv_sem=ici_recv.at[j * nu + u], device_id=(x, y, c), device_id_type=MESH).wait_recv()
                cp = pltpu.make_async_remote_copy(src_ref=got, dst_ref=got, send_sem=d2d_send.at[j * nu + u],
                                                  recv_sem=d2d_recv.at[j * nu + u], device_id=(x, y, 1 - c), device_id_type=MESH)
                cp.start()
                sends.append(cp)
        for j, (bx, by) in enumerate(CHIP_FLIPS):
            src = 2 * _flip(x, bx) + _flip(y, by)
            for u in range(nu):
                other = place(u, src, 1 - c)
                pltpu.make_async_remote_copy(src_ref=other, dst_ref=other, send_sem=d2d_send.at[j * nu + u],
                                             recv_sem=d2d_recv.at[j * nu + u], device_id=(x, y, c), device_id_type=MESH).wait_recv()
        for cp in sends:
            cp.wait_send()
        for cp in local:
            cp.wait()

    full = lambda shape, a: tuple(N_CHIPS * d if i == a else d for i, d in enumerate(shape))
    res = pl.pallas_call(
        body, name="gather_weights", in_specs=[ANY] * nb, out_specs=[ANY] * nu,
        out_shape=[jax.ShapeDtypeStruct(full(shape, a), srcs[k].dtype) for k, _, shape, a in units],
        scratch_shapes=[pltpu.SemaphoreType.DMA((3 * nu,))] * 4 + [pltpu.SemaphoreType.DMA((nu,))],
    )(*srcs)
    out, it = [], iter(res)
    for k in range(nb):
        out.append([next(it) for _ in range(srcs[k].shape[0])] if stacked[k] else next(it))
    return out


def _exchange_grads(grads, axes, smalls):
    nt, ns = len(grads), len(smalls)
    sizes = [g[0].shape[a] // N_CHIPS for g, a in zip(grads, axes)]
    dev_flips = [(bx, by, bc) for bx in (0, 1) for by in (0, 1) for bc in (0, 1)][1:]
    n_remote = 3 * nt * DEPTH + 7 * ns
    n_local = nt * DEPTH + ns

    def body(*refs):
        g_refs = [refs[k * DEPTH:(k + 1) * DEPTH] for k in range(nt)]
        s_refs = refs[nt * DEPTH:nt * DEPTH + ns]
        outs = refs[nt * DEPTH + ns:nt * DEPTH + ns + nt + ns]
        recv_refs, all_refs = outs[:nt], outs[nt:]
        send_sems, recv_sems, local_sems = refs[-3:]
        x, y, c = lax.axis_index("x"), lax.axis_index("y"), lax.axis_index("c")
        me_chip = 2 * x + y
        me = 4 * x + 2 * y + c
        part = lambda k, i, chip: _window(g_refs[k][i], axes[k], chip * sizes[k], sizes[k])
        started, waits = [], []
        sem, lsem = 0, 0
        for k in range(nt):
            for i in range(DEPTH):
                cp = pltpu.make_async_copy(part(k, i, me_chip), recv_refs[k].at[i, 3], local_sems.at[lsem])
                cp.start()
                started.append(cp.wait)
                lsem += 1
                for j, (bx, by) in enumerate(CHIP_FLIPS):
                    px, py = _flip(x, bx), _flip(y, by)
                    cp = pltpu.make_async_remote_copy(src_ref=part(k, i, 2 * px + py), dst_ref=recv_refs[k].at[i, j],
                                                      send_sem=send_sems.at[sem], recv_sem=recv_sems.at[sem],
                                                      device_id=(px, py, c), device_id_type=MESH)
                    cp.start()
                    started.append(cp.wait_send)
                    waits.append(cp.wait_recv)
                    sem += 1
        for s in range(ns):
            cp = pltpu.make_async_copy(s_refs[s], all_refs[s].at[me], local_sems.at[lsem])
            cp.start()
            started.append(cp.wait)
            lsem += 1
            for bx, by, bc in dev_flips:
                peer = (_flip(x, bx), _flip(y, by), _flip(c, bc))
                cp = pltpu.make_async_remote_copy(src_ref=s_refs[s], dst_ref=all_refs[s].at[me], send_sem=send_sems.at[sem],
                                                  recv_sem=recv_sems.at[sem], device_id=peer, device_id_type=MESH)
                cp.start()
                started.append(cp.wait_send)
                src = 4 * peer[0] + 2 * peer[1] + peer[2]
                waits.append(pltpu.make_async_remote_copy(src_ref=s_refs[s], dst_ref=all_refs[s].at[src], send_sem=send_sems.at[sem],
                                                          recv_sem=recv_sems.at[sem], device_id=peer, device_id_type=MESH).wait_recv)
                sem += 1
        for w in waits + started:
            w()

    shard = lambda g, a: tuple(d // N_CHIPS if i == a else d for i, d in enumerate(g.shape))
    flat = [g for per_layer in grads for g in per_layer]
    return pl.pallas_call(
        body, name="exchange_grads", in_specs=[ANY] * (len(flat) + ns), out_specs=[ANY] * (nt + ns),
        out_shape=[jax.ShapeDtypeStruct((DEPTH, N_CHIPS, *shard(g[0], a)), g[0].dtype) for g, a in zip(grads, axes)]
        + [jax.ShapeDtypeStruct((N_DEV, *s.shape), s.dtype) for s in smalls],
        scratch_shapes=[pltpu.SemaphoreType.DMA((n_remote,)), pltpu.SemaphoreType.DMA((n_remote,)), pltpu.SemaphoreType.DMA((n_local,))],
    )(*flat, *smalls)


def _swap_cores(parts):
    nb = len(parts)

    def body(*refs):
        ins, outs, send_sems, recv_sems = refs[:nb], refs[nb:2 * nb], refs[-2], refs[-1]
        x, y, c = lax.axis_index("x"), lax.axis_index("y"), lax.axis_index("c")
        cps = [pltpu.make_async_remote_copy(src_ref=ins[k], dst_ref=outs[k], send_sem=send_sems.at[k], recv_sem=recv_sems.at[k],
                                            device_id=(x, y, 1 - c), device_id_type=MESH) for k in range(nb)]
        for cp in cps:
            cp.start()
        for cp in cps:
            cp.wait()

    return pl.pallas_call(
        body, name="swap_cores", in_specs=[ANY] * nb, out_specs=[ANY] * nb,
        out_shape=[jax.ShapeDtypeStruct(q.shape, q.dtype) for q in parts],
        scratch_shapes=[pltpu.SemaphoreType.DMA((nb,)), pltpu.SemaphoreType.DMA((nb,))],
    )(*parts)


def _sum_chips_call(recv, cols, name):
    depth, _, r, c = recv.shape
    tile = _pick(r, (512, 256, 128, 64, 32, 16))

    def body(r_ref, o_ref):
        slot = lambda s: r_ref[s, :, pl.ds(0, cols)].astype(F32)
        o_ref[...] = ((slot(3) + slot(0)) + slot(1)) + slot(2)

    return pl.pallas_call(
        body, name=name, grid=(depth, r // tile),
        in_specs=[pl.BlockSpec((None, N_CHIPS, tile, c), lambda l, i: (l, 0, i, 0))],
        out_specs=pl.BlockSpec((None, tile, cols), lambda l, i: (l, i, 0)), out_shape=jax.ShapeDtypeStruct((depth, r, cols), F32),
        compiler_params=_params(("parallel", "parallel")),
    )(recv)


def _sum_slots_call(slots, name):
    n, r, c = slots.shape
    tile = _pick(r, (512, 256, 128, 64, 32, 16, 8))

    def body(s_ref, o_ref):
        acc = s_ref[0]
        for s in range(1, n):
            acc = acc + s_ref[s]
        o_ref[...] = acc

    return pl.pallas_call(
        body, name=name, grid=(r // tile,), in_specs=[pl.BlockSpec((n, tile, c), lambda i: (0, i, 0))],
        out_specs=pl.BlockSpec((tile, c), lambda i: (i, 0)), out_shape=jax.ShapeDtypeStruct((r, c), F32),
        compiler_params=_params(("parallel",)),
    )(slots)


def _adamw_math(w, g, m, v):
    m = ADAM_B1 * m + (1.0 - ADAM_B1) * g
    v = ADAM_B2 * v + (1.0 - ADAM_B2) * (g * g)
    m_hat = m / (1.0 - ADAM_B1 ** ADAM_STEP)
    v_hat = v / (1.0 - ADAM_B2 ** ADAM_STEP)
    return -ADAM_LR * (m_hat / (jnp.sqrt(v_hat) + ADAM_EPS) + ADAM_WD * w), m, v


def _adamw_call(w, m, v, gparts, name):
    r, c = w.shape
    n = len(gparts)
    tile = _pick(r, (512, 256, 128, 64, 32, 16, 8))

    def body(w_ref, m_ref, v_ref, *refs):
        g_refs, (go_ref, d_ref, mo_ref, vo_ref) = refs[:n], refs[n:]
        g = g_refs[0][...]
        for g_ref in g_refs[1:]:
            g = g + g_ref[...]
        go_ref[...] = g
        d_ref[...], mo_ref[...], vo_ref[...] = _adamw_math(w_ref[...], g, m_ref[...], v_ref[...])

    blk = pl.BlockSpec((tile, c), lambda i: (i, 0))
    return pl.pallas_call(
        body, name=name, grid=(r // tile,), in_specs=[blk] * (3 + n),
        out_specs=[blk] * 4, out_shape=[jax.ShapeDtypeStruct((r, c), F32)] * 4,
        compiler_params=_params(("parallel",)),
    )(w, m, v, *gparts)


def _train_local(x, p, layers, target):
    y, vjp = jax.vjp(lambda x_, w_: _forward(x_, p, w_), x, layers)
    loss, dy = _loss_call(y, target)
    dx, dw = vjp(dy)
    return loss, dx, dw


def kernel(x, p, w_in, w_merge, b_merge, conv_w, conv_b, conv_norm_g, conv_norm_b, w_pw2, mla_q_norm_g, mla_kv_norm_g, w_uq, w_ukv, ssm_a_re, ssm_a_im, ssm_log_dt, ssm_b_re, ssm_b_im, ssm_c_re, ssm_c_im, ssm_d, w_glu, attn_sinks, w_branch, w_out, ln_g, ln_b, w_ple, w_ple_gate, ple_norm_g, loss_target, m_w_in, m_w_merge, m_b_merge, m_conv_w, m_conv_b, m_conv_norm_g, m_conv_norm_b, m_w_pw2, m_mla_q_norm_g, m_mla_kv_norm_g, m_w_uq, m_w_ukv, m_ssm_a_re, m_ssm_a_im, m_ssm_log_dt, m_ssm_b_re, m_ssm_b_im, m_ssm_c_re, m_ssm_c_im, m_ssm_d, m_w_glu, m_attn_sinks, m_w_branch, m_w_out, m_ln_g, m_ln_b, m_w_ple, m_w_ple_gate, m_ple_norm_g, v_w_in, v_w_merge, v_b_merge, v_conv_w, v_conv_b, v_conv_norm_g, v_conv_norm_b, v_w_pw2, v_mla_q_norm_g, v_mla_kv_norm_g, v_w_uq, v_w_ukv, v_ssm_a_re, v_ssm_a_im, v_ssm_log_dt, v_ssm_b_re, v_ssm_b_im, v_ssm_c_re, v_ssm_c_im, v_ssm_d, v_w_glu, v_attn_sinks, v_w_branch, v_w_out, v_ln_g, v_ln_b, v_w_ple, v_w_ple_gate, v_ple_norm_g):
    given = dict(locals())
    w_loc = {n: given[n] for n in WEIGHTS}
    m_loc = {n: given["m_" + n] for n in WEIGHTS}
    v_loc = {n: given["v_" + n] for n in WEIGHTS}

    me_chip = 2 * lax.axis_index("x") + lax.axis_index("y")

    wire = {n: w_loc[n].astype(MXU_DTYPE) for n in BIG}
    wire["w_in"] = jnp.pad(wire["w_in"], ((0, 0), (0, 0), (0, IN_SHARD_PAD - IN_SHARD)))
    odd_shapes = [w_loc[n].shape for n in ODD]
    gathered = _gather_chips([wire[n] for n in BIG] + [_pack([w_loc[n] for n in ODD], F32)], [SHARD_AXIS[n] for n in BIG] + [0],
                             [True] * len(BIG) + [False])
    full = dict(zip(BIG, gathered[:-1]))
    odd_parts = [_unpack(part, odd_shapes) for part in jnp.split(gathered[-1], N_CHIPS, axis=0)]
    for k, n in enumerate(ODD):
        full[n] = jnp.concatenate([odd_parts[s][k] for s in range(N_CHIPS)], axis=SHARD_AXIS[n])
    layers = []
    for i in range(DEPTH):
        layer = {n: (full[n][i] if n in full else w_loc[n][i]) for n in WEIGHTS if n != "w_in"}
        layer["w_in_pad"] = full["w_in"][i]
        layers.append(layer)

    loss, dx, dw = _train_local(x[0], p[:, 0], layers, loss_target[0])
    loss = lax.psum(loss, ("x", "y", "c"))

    key = lambda n: "w_in_pad" if n == "w_in" else n
    stacked = lambda n: jnp.stack([dw[i][n] for i in range(DEPTH)])
    small_rep = _pack([stacked(n) for n in REPLICATED], F32)
    small_odd = _pack([stacked(n) for n in ODD], F32)
    *recv, all_rep, all_odd = _exchange_grads([[dw[i][key(n)] for i in range(DEPTH)] for n in BIG],
                                              [SHARD_AXIS[n] - 1 for n in BIG], [small_rep, small_odd])
    parts = []
    for n, r in zip(BIG, recv):
        cols = w_loc[n].shape[-1]
        parts.append(_sum_chips_call(r.reshape(DEPTH, N_CHIPS, -1, r.shape[-1]), cols, "sum_chips_" + n))
    others = _swap_cores(parts)
    g_rep = _sum_slots_call(all_rep, "sum_replicated")
    g_odd = _unpack(_sum_slots_call(all_odd, "sum_odd"), [(DEPTH, *w_loc[n].shape[1:-1], N_CHIPS * w_loc[n].shape[-1]) for n in ODD])

    grads, deltas, new_m, new_v = {}, {}, {}, {}

    def adamw(n, gparts):
        shape = w_loc[n].shape
        two_d = lambda a: a.reshape(-1, shape[-1])
        res = _adamw_call(two_d(w_loc[n]), two_d(m_loc[n]), two_d(v_loc[n]), [two_d(g) for g in gparts], "adamw_" + n)
        grads[n], deltas[n], new_m[n], new_v[n] = [r.reshape(shape) for r in res]

    for n, part, other in zip(BIG, parts, others):
        adamw(n, [part, other])
    for n, g in zip(ODD, g_odd):
        size = w_loc[n].shape[-1]
        adamw(n, [lax.dynamic_slice_in_dim(g, me_chip * size, size, axis=g.ndim - 1)])
    rep_shapes = [w_loc[n].shape for n in REPLICATED]
    res = _adamw_call(_pack([w_loc[n] for n in REPLICATED], F32), _pack([m_loc[n] for n in REPLICATED], F32),
                      _pack([v_loc[n] for n in REPLICATED], F32), [g_rep], "adamw_replicated")
    for dst, buf in zip((grads, deltas, new_m, new_v), res):
        for n, a in zip(REPLICATED, _unpack(buf, rep_shapes)):
            dst[n] = a

    return (loss, dx[None], *[grads[n] for n in WEIGHTS], *[deltas[n] for n in WEIGHTS],
            *[new_m[n] for n in WEIGHTS], *[new_v[n] for n in WEIGHTS])
```

```python
import functools
import math

import jax
import jax.numpy as jnp
import numpy as np
from jax import lax
from jax.experimental import pallas as pl
from jax.experimental.pallas import tpu as pltpu

F32 = jnp.float32
BF16 = jnp.bfloat16
MXU_DTYPE = BF16
V7X_VMEM_BYTES = 64 * 1024 * 1024
VMEM_LIMIT = V7X_VMEM_BYTES * 3 // 4
LANES = 128
SUBLANES = 8

D_MODEL = 1024
DEPTH = 4
BRANCH_W = 256
CONV_W = 31
CONV_HALO = 32
MLA_SCALE = (64 + 32) ** -0.5
SWA_SCALE = 64 ** -0.5
WINDOW = 128
ROPE_THETA = 10000.0
SSM_GROUPS, SSM_GROUP, SSM_STATE = 16, 16, 64
SSM_CH = SSM_GROUPS * SSM_STATE
SCAN_SEGMENTS = SUBLANES
SCAN_CB = 128
DEEPNORM_ALPHA = (2.0 * DEPTH) ** 0.25
LN_EPS = 1e-5
RMS_EPS = 1e-6
ADAM_LR, ADAM_B1, ADAM_B2, ADAM_EPS, ADAM_WD, ADAM_STEP = 0.001, 0.9, 0.999, 1e-08, 0.01, 10
NEG = -1e30
ROW_TILE = 512

NN = (((1,), (0,)), ((), ()))
NT = (((1,), (1,)), ((), ()))
TN = (((0,), (0,)), ((), ()))

MESH = pl.DeviceIdType.MESH
ANY = pl.BlockSpec(memory_space=pl.ANY)


def _dot(a, b, dims):
    return lax.dot_general(a.astype(MXU_DTYPE), b.astype(MXU_DTYPE), dims, preferred_element_type=F32)


def _pick(n, cands):
    for c in cands:
        if n % c == 0:
            return c
    return n


def _params(sem):
    return pltpu.CompilerParams(dimension_semantics=sem, vmem_limit_bytes=VMEM_LIMIT)


def _col_offsets(widths):
    return [sum(widths[:j]) for j in range(len(widths))]


def _silu_gate(x, z):
    return x * (z * jax.nn.sigmoid(z))


def _proj_fwd_call(x, wb, widths, name, z=None):
    t, k = x.shape
    tm = min(ROW_TILE, t)
    offs = _col_offsets(widths)
    ins = [x] if z is None else [x, z]

    def body(*refs):
        w_ref, o_refs = refs[len(ins)], refs[len(ins) + 1:]
        xv = refs[0][...] if z is None else _silu_gate(refs[0][...], refs[1][...])
        xb = xv.astype(MXU_DTYPE)
        for o_ref, off, wd in zip(o_refs, offs, widths):
            o_ref[...] = _dot(xb, w_ref[:, off:off + wd], NN)

    return pl.pallas_call(
        body, name=name, grid=(t // tm,),
        in_specs=[pl.BlockSpec((tm, k), lambda i: (i, 0))] * len(ins) + [pl.BlockSpec(wb.shape, lambda i: (0, 0))],
        out_specs=[pl.BlockSpec((tm, wd), lambda i: (i, 0)) for wd in widths],
        out_shape=[jax.ShapeDtypeStruct((t, wd), F32) for wd in widths],
        compiler_params=_params(("parallel",)),
    )(*ins, wb)


def _proj_dx_call(douts, wb, widths, name, gate=None):
    t = douts[0].shape[0]
    k = wb.shape[0]
    tm = min(ROW_TILE, t)
    offs = _col_offsets(widths)
    nd = len(douts)
    extra = [] if gate is None else list(gate)

    def body(*refs):
        d_refs, w_ref = refs[:nd], refs[nd]
        acc = jnp.zeros((tm, k), F32)
        for d_ref, off, wd in zip(d_refs, offs, widths):
            acc = acc + _dot(d_ref[...], w_ref[:, off:off + wd], NT)
        if gate is None:
            refs[-1][...] = acc
        else:
            xv, zv = refs[nd + 1][...], refs[nd + 2][...]
            sg = jax.nn.sigmoid(zv)
            refs[-2][...] = acc * (zv * sg)
            refs[-1][...] = acc * xv * (sg * (1.0 + zv * (1.0 - sg)))

    row = pl.BlockSpec((tm, k), lambda i: (i, 0))
    n_out = 1 if gate is None else 2
    res = pl.pallas_call(
        body, name=name, grid=(t // tm,),
        in_specs=[pl.BlockSpec((tm, wd), lambda i: (i, 0)) for wd in widths] + [pl.BlockSpec(wb.shape, lambda i: (0, 0))] + [row] * len(extra),
        out_specs=[row] * n_out, out_shape=[jax.ShapeDtypeStruct((t, k), F32)] * n_out,
        compiler_params=_params(("parallel",)),
    )(*douts, wb, *extra)
    return res[0] if gate is None else tuple(res)


def _proj_dw_call(x, douts, widths, name, out_dtype, z=None):
    t, k = x.shape
    n = sum(widths)
    tk = min(ROW_TILE if k * n <= 2 * 1024 * 1024 else ROW_TILE // 2, t)
    nk = t // tk
    offs = _col_offsets(widths)
    ins = [x] if z is None else [x, z]

    def body(*all_refs):
        refs = all_refs[len(ins):]
        d_refs, o_ref, acc_ref = refs[:-2], refs[-2], refs[-1]

        @pl.when(pl.program_id(0) == 0)
        def _():
            acc_ref[...] = jnp.zeros_like(acc_ref)

        xv = all_refs[0][...] if z is None else _silu_gate(all_refs[0][...], all_refs[1][...])
        xb = xv.astype(MXU_DTYPE)
        for d_ref, off, wd in zip(d_refs, offs, widths):
            acc_ref[:, off:off + wd] += _dot(xb, d_ref[...], TN)

        @pl.when(pl.program_id(0) == nk - 1)
        def _():
            o_ref[...] = acc_ref[...].astype(out_dtype)

    return pl.pallas_call(
        body, name=name, grid=(nk,),
        in_specs=[pl.BlockSpec((tk, k), lambda i: (i, 0))] * len(ins) + [pl.BlockSpec((tk, wd), lambda i: (i, 0)) for wd in widths],
        out_specs=pl.BlockSpec((k, n), lambda i: (0, 0)), out_shape=jax.ShapeDtypeStruct((k, n), out_dtype),
        scratch_shapes=[pltpu.VMEM((k, n), F32)],
        compiler_params=_params(("arbitrary",)),
    )(*ins, *douts)


def make_proj(widths, name):
    @jax.custom_vjp
    def op(x, w):
        return tuple(_proj_fwd_call(x, w.astype(MXU_DTYPE), widths, name + "_fwd"))

    def fwd(x, w):
        wb = w.astype(MXU_DTYPE)
        return tuple(_proj_fwd_call(x, wb, widths, name + "_fwd")), (x, wb, jnp.zeros((0,), w.dtype))

    def bwd(res, douts):
        x, wb, w_like = res
        return _proj_dx_call(douts, wb, widths, name + "_dx"), _proj_dw_call(x, douts, widths, name + "_dw", w_like.dtype)

    op.defvjp(fwd, bwd)
    return op


_MM_OPS = {}


def op_mm(a, w):
    n = w.shape[1]
    if n not in _MM_OPS:
        _MM_OPS[n] = make_proj((n,), "mm%d" % n)
    return _MM_OPS[n](a, w)[0]


@jax.custom_vjp
def _mm(a, w):
    return _dot(a, w, NN)


def _mm_f(a, w):
    return _dot(a, w, NN), (a, w)


def _mm_b(res, g):
    a, w = res
    return _dot(g, w, NT), _dot(a, g, TN)


_mm.defvjp(_mm_f, _mm_b)


@functools.partial(jax.custom_vjp, nondiff_argnums=(1,))
def _roll(x, shift):
    return pltpu.roll(x, shift, 1)


def _roll_f(x, shift):
    return pltpu.roll(x, shift, 1), None


def _roll_b(shift, _, g):
    return (pltpu.roll(g, (g.shape[1] - shift) % g.shape[1], 1),)


_roll.defvjp(_roll_f, _roll_b)


def _ln(x, g, b):
    mu = jnp.mean(x, axis=-1, keepdims=True)
    xc = x - mu
    var = jnp.mean(xc * xc, axis=-1, keepdims=True)
    return xc * lax.rsqrt(var + LN_EPS) * g + b


def _rms(x, g):
    ms = jnp.mean(x * x, axis=-1, keepdims=True)
    return x * lax.rsqrt(ms + RMS_EPS) * g


def _sigmoid(x):
    return jax.nn.sigmoid(x)


def _silu(x):
    return x * _sigmoid(x)


def _gelu_tanh(x):
    return x * (0.5 * (1.0 + jnp.tanh(math.sqrt(2.0 / math.pi) * (x + 0.044715 * (x * x * x)))))


def _rowwise_fwd_call(fn, rows, consts, name, tile):
    t = rows[0].shape[0]
    tile = min(tile, t)
    nr = len(rows)
    outs = jax.eval_shape(fn, *[jax.ShapeDtypeStruct((tile, r.shape[1]), F32) for r in rows],
                          *[jax.ShapeDtypeStruct(c.shape, F32) for c in consts])

    def body(*refs):
        vals = [r[...] for r in refs[:nr + len(consts)]]
        res = fn(*vals)
        for o_ref, o in zip(refs[nr + len(consts):], res):
            o_ref[...] = o

    return pl.pallas_call(
        body, name=name, grid=(t // tile,),
        in_specs=[pl.BlockSpec((tile, r.shape[1]), lambda i: (i, 0)) for r in rows]
        + [pl.BlockSpec(c.shape, lambda i: (0, 0)) for c in consts],
        out_specs=[pl.BlockSpec((tile, o.shape[1]), lambda i: (i, 0)) for o in outs],
        out_shape=[jax.ShapeDtypeStruct((t, o.shape[1]), F32) for o in outs],
        compiler_params=_params(("parallel",)),
    )(*rows, *consts)


def _rowwise_bwd_call(fn, rows, consts, douts, row_diff, name, tile, row_grad_dtype=F32):
    t = rows[0].shape[0]
    tile = min(tile, t)
    nr, nc, nd = len(rows), len(consts), len(douts)
    diff_idx = [i for i in range(nr) if row_diff[i]]

    def body(*refs):
        rv = [r[...] for r in refs[:nr]]
        cv = [r[...] for r in refs[nr:nr + nc]]
        dv = [r[...] for r in refs[nr + nc:nr + nc + nd]]
        out_refs = refs[nr + nc + nd:]

        def f(*diff):
            full = list(rv)
            for k, i in enumerate(diff_idx):
                full[i] = diff[k]
            return fn(*full, *diff[len(diff_idx):])

        _, vjp = jax.vjp(f, *[rv[i] for i in diff_idx], *cv)
        grads = vjp(tuple(dv))
        for k in range(len(diff_idx)):
            out_refs[k][...] = grads[k].astype(row_grad_dtype)
        first = pl.program_id(0) == 0
        for k in range(nc):
            acc_ref = out_refs[len(diff_idx) + k]
            g = grads[len(diff_idx) + k]

            @pl.when(first)
            def _(acc_ref=acc_ref, g=g):
                acc_ref[...] = g

            @pl.when(jnp.logical_not(first))
            def _(acc_ref=acc_ref, g=g):
                acc_ref[...] += g

    res = pl.pallas_call(
        body, name=name, grid=(t // tile,),
        in_specs=[pl.BlockSpec((tile, r.shape[1]), lambda i: (i, 0)) for r in rows]
        + [pl.BlockSpec(c.shape, lambda i: (0, 0)) for c in consts]
        + [pl.BlockSpec((tile, d.shape[1]), lambda i: (i, 0)) for d in douts],
        out_specs=[pl.BlockSpec((tile, rows[i].shape[1]), lambda i_: (i_, 0)) for i in diff_idx]
        + [pl.BlockSpec(c.shape, lambda i: (0, 0)) for c in consts],
        out_shape=[jax.ShapeDtypeStruct(rows[i].shape, row_grad_dtype) for i in diff_idx]
        + [jax.ShapeDtypeStruct(c.shape, F32) for c in consts],
        compiler_params=_params(("arbitrary",)),
    )(*rows, *consts, *douts)
    return res[:len(diff_idx)], res[len(diff_idx):]


def make_rowwise(fn, name, row_diff, tile=ROW_TILE):
    @jax.custom_vjp
    def op(rows, consts):
        return tuple(_rowwise_fwd_call(fn, rows, consts, name + "_fwd", tile))

    def fwd(rows, consts):
        return op(rows, consts), (rows, consts)

    def bwd(res, douts):
        rows, consts = res
        drows, dconsts = _rowwise_bwd_call(fn, rows, consts, douts, row_diff, name + "_bwd", tile)
        it = iter(drows)
        full = tuple(next(it) if row_diff[i] else jnp.zeros_like(rows[i]) for i in range(len(rows)))
        return full, tuple(dconsts)

    op.defvjp(fwd, bwd)
    return op


def _conv_post_fn(cv, a_z, ng, nb, w_pw2):
    return (_mm(_silu(_ln(cv, ng, nb)), w_pw2) * _silu(a_z),)


def _mla_prep_fn(c_q, c_kv, krblk, cos4, sin4, qg, kvg, w_uq, w_uk, w_uv):
    qe = _mm(_rms(c_q, qg), w_uq)
    q = qe * cos4 + _roll(qe, qe.shape[1] - 32) * sin4
    cos1, sin1 = cos4[:, :LANES], sin4[:, :LANES]
    kr = krblk * cos1 + _roll(krblk, LANES - 32) * sin1
    kn = _rms(c_kv, kvg)
    k = _mm(kn, w_uk) + jnp.concatenate([kr, kr, kr, kr], axis=1)
    return q, k, _mm(kn, w_uv)


def _ssm_post_fn(y, u, c_z, d, w_a, w_b):
    y2 = _gelu_tanh(y + d * u)
    return (_mm(y2, w_a) * _sigmoid(_mm(y2, w_b)) * _silu(c_z),)


def _merge_fn(br0, br1, br2, br3, gl0, gl1, gl2, gl3, b0, b1, b2, b3):
    return (_sigmoid(gl0 + b0) * br0 + _sigmoid(gl1 + b1) * br1 + _sigmoid(gl2 + b2) * br2 + _sigmoid(gl3 + b3) * br3,)


def _ln_fn(x, mo, g, b):
    return (_ln(DEEPNORM_ALPHA * x + mo, g, b),)


def _ple_fn(x1, pe, gl, g):
    return (x1 + _rms(pe * _sigmoid(gl), g),)


op_conv_post = make_rowwise(_conv_post_fn, "conv_post", (True, True))
op_mla_prep = make_rowwise(_mla_prep_fn, "mla_prep", (True, True, True, False, False))
op_ssm_post = make_rowwise(_ssm_post_fn, "ssm_post", (True, True, True))
MERGE_TILE = ROW_TILE // 2
MERGE_WIDTHS = (D_MODEL,) * 4


@jax.custom_vjp
def op_merge_block(x, ys, zs, w_merge, w_branch, b_merge):
    return _merge_block_fwd(x, ys, zs, w_merge, w_branch, b_merge)[0]


def _merge_block_fwd(x, ys, zs, w_merge, w_branch, b_merge):
    wm, wb = w_merge.astype(MXU_DTYPE), w_branch.astype(MXU_DTYPE)
    gl = _proj_fwd_call(x, wm, MERGE_WIDTHS, "merge_proj_fwd")
    br = [_proj_fwd_call(ys[n], wb[n], (D_MODEL,), "branch_proj_fwd", zs[n])[0] for n in range(4)]
    bm = tuple(b_merge[n * D_MODEL:(n + 1) * D_MODEL].reshape(1, -1) for n in range(4))
    (merged,) = _rowwise_fwd_call(_merge_fn, (*br, *gl), bm, "merge_fwd", MERGE_TILE)
    return merged, (x, ys, zs, wm, wb, tuple(br), tuple(gl), bm, jnp.zeros((0,), w_merge.dtype), jnp.zeros((0,), w_branch.dtype))


def _merge_block_bwd(res, dmerged):
    x, ys, zs, wm, wb, br, gl, bm, wm_like, wb_like = res
    drows, dbm = _rowwise_bwd_call(_merge_fn, (*br, *gl), bm, (dmerged,), (True,) * 8, "merge_bwd", MERGE_TILE, MXU_DTYPE)
    dbr, dgl = drows[:4], drows[4:]
    dx = _proj_dx_call(dgl, wm, MERGE_WIDTHS, "merge_proj_dx")
    dwm = _proj_dw_call(x, dgl, MERGE_WIDTHS, "merge_proj_dw", wm_like.dtype)
    dys, dzs = [], []
    for n in range(4):
        if zs[n] is None:
            dys.append(_proj_dx_call([dbr[n]], wb[n], (D_MODEL,), "branch_proj_dx"))
            dzs.append(None)
        else:
            dy, dz = _proj_dx_call([dbr[n]], wb[n], (D_MODEL,), "branch_proj_dx", (ys[n], zs[n]))
            dys.append(dy)
            dzs.append(dz)
    dwb = jnp.stack([_proj_dw_call(ys[n], [dbr[n]], (D_MODEL,), "branch_proj_dw", wb_like.dtype, zs[n]) for n in range(4)])
    return dx, tuple(dys), tuple(dzs), dwm, dwb, jnp.concatenate([d.reshape(-1) for d in dbm])


op_merge_block.defvjp(_merge_block_fwd, _merge_block_bwd)
op_ln = make_rowwise(_ln_fn, "post_ln", (True, True))
op_ple = make_rowwise(_ple_fn, "ple", (True, True, True))


def _conv_fwd_call(a_val, a_gate, w32, b):
    t, w = a_val.shape
    tile = min(ROW_TILE, t)
    per = tile // CONV_HALO
    cur = pl.BlockSpec((tile, w), lambda i: (i, 0))
    prev = pl.BlockSpec((CONV_HALO, w), lambda i: (jnp.maximum(i * per - 1, 0), 0))

    def body(av_ref, avh_ref, ag_ref, agh_ref, w_ref, b_ref, cv_ref, buf):
        i = pl.program_id(0)
        gh = avh_ref[...] * _sigmoid(agh_ref[...])
        buf[0:CONV_HALO, :] = jnp.where(i > 0, gh, 0.0)
        buf[CONV_HALO:, :] = av_ref[...] * _sigmoid(ag_ref[...])
        acc = jnp.zeros((tile, w), F32) + b_ref[...]
        for j in range(CONV_W):
            acc = acc + w_ref[j:j + 1, :] * buf[pl.ds(CONV_HALO - (CONV_W - 1) + j, tile), :]
        cv_ref[...] = acc

    return pl.pallas_call(
        body, name="conv_fwd", grid=(t // tile,),
        in_specs=[cur, prev, cur, prev, pl.BlockSpec((CONV_HALO, w), lambda i: (0, 0)), pl.BlockSpec((1, w), lambda i: (0, 0))],
        out_specs=cur, out_shape=jax.ShapeDtypeStruct((t, w), F32),
        scratch_shapes=[pltpu.VMEM((tile + CONV_HALO, w), F32)],
        compiler_params=_params(("parallel",)),
    )(a_val, a_val, a_gate, a_gate, w32, b)


def _conv_bwd_call(a_val, a_gate, w32, dcv):
    t, w = a_val.shape
    tile = min(ROW_TILE, t)
    n = t // tile
    per = tile // CONV_HALO
    cur = pl.BlockSpec((tile, w), lambda i: (i, 0))
    prev = pl.BlockSpec((CONV_HALO, w), lambda i: (jnp.maximum(i * per - 1, 0), 0))
    nxt = pl.BlockSpec((CONV_HALO, w), lambda i: (jnp.minimum((i + 1) * per, t // CONV_HALO - 1), 0))
    full = lambda r: pl.BlockSpec((r, w), lambda i: (0, 0))

    def body(av_ref, avh_ref, ag_ref, agh_ref, w_ref, d_ref, dn_ref, dav_ref, dag_ref, dw_ref, db_ref, gbuf, dbuf):
        i = pl.program_id(0)
        gh = avh_ref[...] * _sigmoid(agh_ref[...])
        gbuf[0:CONV_HALO, :] = jnp.where(i > 0, gh, 0.0)
        av = av_ref[...]
        sg = _sigmoid(ag_ref[...])
        gbuf[CONV_HALO:, :] = av * sg
        d = d_ref[...]
        dbuf[0:tile, :] = d
        dbuf[tile:, :] = jnp.where(i < n - 1, dn_ref[...], 0.0)

        @pl.when(i == 0)
        def _():
            dw_ref[...] = jnp.zeros_like(dw_ref)
            db_ref[...] = jnp.zeros_like(db_ref)

        dg = jnp.zeros((tile, w), F32)
        for j in range(CONV_W):
            dg = dg + w_ref[j:j + 1, :] * dbuf[pl.ds(CONV_W - 1 - j, tile), :]
            dw_ref[j:j + 1, :] += jnp.sum(d * gbuf[pl.ds(CONV_HALO - (CONV_W - 1) + j, tile), :], axis=0, keepdims=True)
        db_ref[...] += jnp.sum(d, axis=0, keepdims=True)
        dav_ref[...] = dg * sg
        dag_ref[...] = dg * av * sg * (1.0 - sg)

    return pl.pallas_call(
        body, name="conv_bwd", grid=(n,),
        in_specs=[cur, prev, cur, prev, full(CONV_HALO), cur, nxt],
        out_specs=[cur, cur, full(CONV_HALO), full(1)],
        out_shape=[jax.ShapeDtypeStruct((t, w), F32), jax.ShapeDtypeStruct((t, w), F32),
                   jax.ShapeDtypeStruct((CONV_HALO, w), F32), jax.ShapeDtypeStruct((1, w), F32)],
        scratch_shapes=[pltpu.VMEM((tile + CONV_HALO, w), F32), pltpu.VMEM((tile + CONV_HALO, w), F32)],
        compiler_params=_params(("arbitrary",)),
    )(a_val, a_val, a_gate, a_gate, w32, dcv, dcv)


def _pad_taps(conv_w):
    return jnp.concatenate([conv_w, jnp.zeros((CONV_HALO - CONV_W, conv_w.shape[1]), F32)], axis=0)


@jax.custom_vjp
def op_conv(a_val, a_gate, conv_w, conv_b):
    return _conv_fwd_call(a_val, a_gate, _pad_taps(conv_w), conv_b)


def _op_conv_fwd(a_val, a_gate, conv_w, conv_b):
    return op_conv(a_val, a_gate, conv_w, conv_b), (a_val, a_gate, conv_w)


def _op_conv_bwd(res, dcv):
    a_val, a_gate, conv_w = res
    dav, dag, dw, db = _conv_bwd_call(a_val, a_gate, _pad_taps(conv_w), dcv)
    return dav, dag, dw[:CONV_W], db


op_conv.defvjp(_op_conv_fwd, _op_conv_bwd)


def _head_masks(rows):
    lane = lax.broadcasted_iota(jnp.int32, (rows, LANES), 1)
    return lane < 64, lane >= 64


def _head_row(vals, mask):
    return jnp.max(jnp.where(mask, vals, NEG), axis=1, keepdims=True)


def _attn_valid(qpos, kpos, window):
    valid = kpos <= qpos
    if window is not None:
        valid = jnp.logical_and(valid, qpos - kpos < window)
    return valid


def _flash_fwd_call(q, k, v, sink, *, window, shared_k, scale, blk, blk_q, name):
    t = q.shape[0]
    qw = LANES if shared_k else 2 * LANES
    pairs = v.shape[1] // LANES
    tk = min(blk, t)
    tq = min(blk_q, t)
    has_sink = sink is not None
    one_step = window is not None and tk == 2 * tq and window <= tq
    kstride = tq if one_step else tk

    def body(*refs):
        if has_sink:
            q_ref, k_ref, v_ref, s_ref, o_ref, lse_ref, k_mxu, v0_mxu, v1_mxu = refs
        else:
            q_ref, k_ref, v_ref, o_ref, lse_ref, k_mxu, v0_mxu, v1_mxu = refs
        v_mxu = (v0_mxu, v1_mxu)
        i = pl.program_id(1)

        @pl.when(i == 0)
        def _():
            full_masks = _head_masks(t)
            k_mxu[...] = k_ref[...].astype(MXU_DTYPE)
            for h in range(2):
                v_mxu[h][...] = jnp.where(full_masks[h], v_ref[...], 0.0).astype(MXU_DTYPE)

        qb = q_ref[...]
        masks = _head_masks(tq)
        row_masks = _head_masks(1)
        qh = [(jnp.where(masks[h], qb, 0.0) if shared_k else qb[:, h * LANES:(h + 1) * LANES]).astype(MXU_DTYPE) for h in range(2)]
        qpos = i * tq + lax.broadcasted_iota(jnp.int32, (tq, tk), 0)
        if has_sink:
            m_init = [jnp.zeros((tq, 1), F32) + _head_row(s_ref[...], row_masks[h]) for h in range(2)]
            l_init = [jnp.ones((tq, 1), F32)] * 2
        else:
            m_init = [jnp.full((tq, 1), NEG, F32)] * 2
            l_init = [jnp.zeros((tq, 1), F32)] * 2

        def make_step(masked):
            def step(j, carry):
                m0, l0, m1, l1, acc = carry
                start = pl.multiple_of(j * kstride, kstride)
                kb = k_mxu[pl.ds(start, tk), :]
                if masked:
                    valid = _attn_valid(qpos, start + lax.broadcasted_iota(jnp.int32, (tq, tk), 1), window)
                new, alphas, pv = [], [], []
                for h, (m, l) in enumerate(((m0, l0), (m1, l1))):
                    kh = kb if shared_k else kb[:, h * LANES:(h + 1) * LANES]
                    s = _dot(qh[h], kh, NT) * scale
                    if masked:
                        s = jnp.where(valid, s, NEG)
                    m_new = jnp.maximum(m, jnp.max(s, axis=1, keepdims=True))
                    alpha = jnp.exp(m - m_new)
                    p = jnp.exp(s - m_new)
                    new += [m_new, alpha * l + jnp.sum(p, axis=1, keepdims=True)]
                    alphas.append(alpha)
                    pv.append(_dot(p, v_mxu[h][pl.ds(start, tk), :], NN))
                acc = acc * jnp.where(masks[0], alphas[0], alphas[1]) + pv[0] + pv[1]
                return new[0], new[1], new[2], new[3], acc
            return step

        carry = (m_init[0], l_init[0], m_init[1], l_init[1], jnp.zeros((tq, LANES), F32))
        last = (i * tq + tq - 1) // tk
        if window is None:
            n_full = (i * tq + 1) // tk
            carry = lax.fori_loop(0, n_full, make_step(False), carry)
            carry = lax.fori_loop(n_full, last + 1, make_step(True), carry)
        elif one_step:
            carry = make_step(True)(jnp.maximum(i - 1, 0), carry)
        else:
            carry = lax.fori_loop(jnp.maximum(i * tq - (window - 1), 0) // tk, last + 1, make_step(True), carry)
        m0, l0, m1, l1, acc = carry
        o_ref[...] = acc / jnp.where(masks[0], l0, l1)
        lse_ref[...] = jnp.where(masks[0], m0 + jnp.log(l0), m1 + jnp.log(l1))

    in_specs = [pl.BlockSpec((tq, qw), lambda p, i: (i, p)), pl.BlockSpec((t, qw), lambda p, i: (0, p)),
                pl.BlockSpec((t, LANES), lambda p, i: (0, p))]
    args = [q, k, v]
    if has_sink:
        in_specs.append(pl.BlockSpec((1, LANES), lambda p, i: (0, p)))
        args.append(sink)
    blk_o = pl.BlockSpec((tq, LANES), lambda p, i: (i, p))
    return pl.pallas_call(
        body, name=name, grid=(pairs, t // tq), in_specs=in_specs, out_specs=[blk_o, blk_o],
        out_shape=[jax.ShapeDtypeStruct((t, pairs * LANES), F32)] * 2,
        scratch_shapes=[pltpu.VMEM((t, qw), MXU_DTYPE), pltpu.VMEM((t, LANES), MXU_DTYPE), pltpu.VMEM((t, LANES), MXU_DTYPE)],
        compiler_params=_params(("arbitrary", "arbitrary")),
    )(*args)


def _flash_bwd_call(q, k, v, sink, o, lse, do, *, window, shared_k, scale, blk, blk_q, name):
    t = q.shape[0]
    qw = LANES if shared_k else 2 * LANES
    pairs = v.shape[1] // LANES
    tk = min(blk, t)
    tq = min(blk_q, t)
    assert tk % tq == 0 or tq % tk == 0
    nq = t // tq
    one_step = window is not None and tq == 2 * tk and window <= tk
    qstride = tk if one_step else tq
    has_sink = sink is not None

    def body(*refs):
        if has_sink:
            q_ref, k_ref, v_ref, o_ref, lse_ref, do_ref, s_ref, dq_ref, dk_ref, dv_ref, ds_ref = refs[:11]
        else:
            q_ref, k_ref, v_ref, o_ref, lse_ref, do_ref, dq_ref, dk_ref, dv_ref = refs[:9]
        q_mxu, do_mxu, lse_h, dsum_h = refs[-8:-6], refs[-6:-4], refs[-4:-2], refs[-2:]
        j = pl.program_id(1)
        masks = _head_masks(tq)
        row_masks = _head_masks(1)

        @pl.when(j == 0)
        def _():
            dq_ref[...] = jnp.zeros_like(dq_ref)
            full_masks = _head_masks(t)
            prod = do_ref[...] * o_ref[...]
            parts = []
            for h in range(2):
                qh = jnp.where(full_masks[h], q_ref[...], 0.0) if shared_k else q_ref[:, h * LANES:(h + 1) * LANES]
                q_mxu[h][...] = qh.astype(MXU_DTYPE)
                do_mxu[h][...] = jnp.where(full_masks[h], do_ref[...], 0.0).astype(MXU_DTYPE)
                dsum = jnp.sum(jnp.where(full_masks[h], prod, 0.0), axis=1, keepdims=True)
                lse = _head_row(lse_ref[...], full_masks[h])
                dsum_h[h][...] = jnp.zeros((t, LANES), F32) + dsum
                lse_h[h][...] = jnp.zeros((t, LANES), F32) + lse
                if has_sink:
                    ps = jnp.exp(_head_row(s_ref[...], row_masks[h]) - lse)
                    parts.append(-jnp.sum(ps * dsum, axis=0, keepdims=True))
            if has_sink:
                ds_ref[...] = jnp.zeros((SUBLANES, LANES), F32) + jnp.where(row_masks[0], parts[0], parts[1])

        kb = k_ref[...].astype(MXU_DTYPE)
        vb = v_ref[...].astype(MXU_DTYPE)
        kh = [kb if shared_k else kb[:, h * LANES:(h + 1) * LANES] for h in range(2)]
        kpos = j * tk + lax.broadcasted_iota(jnp.int32, (tq, tk), 1)
        lanes_of = lambda a: a if tk == LANES else jnp.concatenate([a] * (tk // LANES), axis=1)

        def make_step(masked):
            def step(i, carry):
                dk0, dk1, dv = carry
                start = pl.multiple_of(i * qstride, qstride)
                if masked:
                    valid = _attn_valid(start + lax.broadcasted_iota(jnp.int32, (tq, tk), 0), kpos, window)
                dks, dqs = [], []
                for h in range(2):
                    qh = q_mxu[h][pl.ds(start, tq), :]
                    doh = do_mxu[h][pl.ds(start, tq), :]
                    s = _dot(qh, kh[h], NT) * scale
                    if masked:
                        s = jnp.where(valid, s, NEG)
                    p = jnp.exp(s - lanes_of(lse_h[h][pl.ds(start, tq), :]))
                    dp = _dot(doh, vb, NT)
                    dsc = p * (dp - lanes_of(dsum_h[h][pl.ds(start, tq), :])) * scale
                    dv = dv + _dot(p, doh, TN)
                    dks.append(_dot(dsc, qh, TN))
                    dq_h = _dot(dsc, kh[h], NN)
                    dqs.append(jnp.where(masks[h], dq_h, 0.0) if shared_k else dq_h)
                if shared_k:
                    dq_ref[pl.ds(start, tq), :] += dqs[0] + dqs[1]
                else:
                    dq_ref[pl.ds(start, tq), :] += jnp.concatenate(dqs, axis=1)
                return dk0 + dks[0], dk1 + dks[1], dv
            return step

        zero = jnp.zeros((tk, LANES), F32)
        carry = (zero, zero, zero)
        first = (j * tk) // tq
        if window is None:
            n_full = jnp.minimum(((j + 1) * tk + tq - 2) // tq, nq)
            carry = lax.fori_loop(first, n_full, make_step(True), carry)
            carry = lax.fori_loop(n_full, nq, make_step(False), carry)
        elif one_step:
            carry = make_step(True)(jnp.minimum(j, t // tk - 2), carry)
        else:
            carry = lax.fori_loop(first, jnp.minimum(nq, (j * tk + tk - 1 + window - 1) // tq + 1), make_step(True), carry)
        dk0, dk1, dv = carry
        dk_ref[...] = dk0 + dk1 if shared_k else jnp.concatenate([dk0, dk1], axis=1)
        dv_ref[...] = dv

    full = lambda w: pl.BlockSpec((t, w), lambda p, j: (0, p))
    blkspec = lambda w: pl.BlockSpec((tk, w), lambda p, j: (j, p))
    in_specs = [full(qw), blkspec(qw), blkspec(LANES), full(LANES), full(LANES), full(LANES)]
    args = [q, k, v, o, lse, do]
    out_specs = [full(qw), blkspec(qw), blkspec(LANES)]
    out_shape = [jax.ShapeDtypeStruct(q.shape, F32), jax.ShapeDtypeStruct(k.shape, F32), jax.ShapeDtypeStruct(v.shape, F32)]
    if has_sink:
        in_specs.append(pl.BlockSpec((1, LANES), lambda p, j: (0, p)))
        args.append(sink)
        out_specs.append(pl.BlockSpec((SUBLANES, LANES), lambda p, j: (0, p)))
        out_shape.append(jax.ShapeDtypeStruct((SUBLANES, pairs * LANES), F32))
    return pl.pallas_call(
        body, name=name, grid=(pairs, t // tk), in_specs=in_specs, out_specs=out_specs, out_shape=out_shape,
        scratch_shapes=[pltpu.VMEM((t, LANES), MXU_DTYPE)] * 4 + [pltpu.VMEM((t, LANES), F32)] * 4,
        compiler_params=_params(("arbitrary", "arbitrary")),
    )(*args)


_MLA_CFG = dict(window=None, shared_k=False, scale=MLA_SCALE, blk=256)
_SWA_CFG = dict(window=WINDOW, shared_k=True, scale=SWA_SCALE, blk=128)
_MLA_FWD_CFG = dict(_MLA_CFG, blk=512, blk_q=256)
_SWA_FWD_CFG = dict(_SWA_CFG, blk=512, blk_q=256)
_MLA_BWD_CFG = dict(_MLA_CFG, blk=512, blk_q=512)
_SWA_BWD_CFG = dict(_SWA_CFG, blk=256, blk_q=512)


@jax.custom_vjp
def op_mla_attn(q, k, v):
    return _flash_fwd_call(q, k, v, None, name="mla_fwd", **_MLA_FWD_CFG)[0]


def _op_mla_attn_fwd(q, k, v):
    o, lse = _flash_fwd_call(q, k, v, None, name="mla_fwd", **_MLA_FWD_CFG)
    return o, (q, k, v, o, lse)


def _op_mla_attn_bwd(res, do):
    q, k, v, o, lse = res
    return tuple(_flash_bwd_call(q, k, v, None, o, lse, do, name="mla_bwd", **_MLA_BWD_CFG))


op_mla_attn.defvjp(_op_mla_attn_fwd, _op_mla_attn_bwd)


@jax.custom_vjp
def op_swa_attn(q, k, v, sink):
    return _flash_fwd_call(q, k, v, sink, name="swa_fwd", **_SWA_FWD_CFG)[0]


def _op_swa_attn_fwd(q, k, v, sink):
    o, lse = _flash_fwd_call(q, k, v, sink, name="swa_fwd", **_SWA_FWD_CFG)
    return o, (q, k, v, sink, o, lse)


def _op_swa_attn_bwd(res, do):
    q, k, v, sink, o, lse = res
    dq, dk, dv, dsink = _flash_bwd_call(q, k, v, sink, o, lse, do, name="swa_bwd", **_SWA_BWD_CFG)
    first_lane = lax.broadcasted_iota(jnp.int32, (1, dsink.shape[1]), 1) % 64 == 0
    return dq, dk, dv, jnp.where(first_lane, dsink[:1], 0.0)


op_swa_attn.defvjp(_op_swa_attn_fwd, _op_swa_attn_bwd)


def _complex_power(ar, ai, n):
    for _ in range(int(math.log2(n))):
        ar, ai = ar * ar - ai * ai, 2.0 * ar * ai
    return ar, ai


def _scan_passes(load_b, a1r, a1i, n, store, e_ref, c_ref, reverse):
    cb = a1r.shape[1]
    ar = jnp.zeros((SCAN_SEGMENTS, cb), F32) + a1r
    ai = jnp.zeros((SCAN_SEGMENTS, cb), F32) + a1i
    a2r, a2i = ar * ar - ai * ai, 2.0 * ar * ai
    idx = (lambda k: n - 1 - k) if reverse else (lambda k: k)
    mac = lambda pr_, pi__, h, b: (pr_ * h[0] - pi__ * h[1] + b[0], pr_ * h[1] + pi__ * h[0] + b[1])

    def load_pair(ii):
        k = jnp.minimum(2 * ii, n - 2)
        b0, b1 = load_b(idx(k)), load_b(idx(k + 1))
        return b0, mac(ar, ai, b0, b1)

    def local(ii, carry):
        h, c = carry
        return mac(a2r, a2i, h, c), load_pair(ii + 1)[1]

    zero = jnp.zeros((SCAN_SEGMENTS, cb), F32)
    (er, ei), _ = lax.fori_loop(0, n // 2, local, ((zero, zero), load_pair(0)[1]))
    e_ref[:, 0:cb] = er
    e_ref[:, cb:] = ei
    pr, pi_ = _complex_power(a1r, a1i, n)
    cr = jnp.zeros((1, cb), F32)
    ci = jnp.zeros((1, cb), F32)
    order = range(SCAN_SEGMENTS - 1, -1, -1) if reverse else range(SCAN_SEGMENTS)
    for s in order:
        c_ref[s:s + 1, 0:cb] = cr
        c_ref[s:s + 1, cb:] = ci
        er1, ei1 = e_ref[s:s + 1, 0:cb], e_ref[s:s + 1, cb:]
        cr, ci = pr * cr - pi_ * ci + er1, pr * ci + pi_ * cr + ei1

    def second(ii, carry):
        h, b0, c = carry
        h0 = mac(ar, ai, h, b0)
        h1 = mac(a2r, a2i, h, c)
        store(idx(2 * ii), *h0)
        store(idx(2 * ii + 1), *h1)
        nb0, nc = load_pair(ii + 1)
        return h1, nb0, nc

    b0, c0 = load_pair(0)
    lax.fori_loop(0, n // 2, second, ((c_ref[:, 0:cb], c_ref[:, cb:]), b0, c0))


def _scan_fwd_call(bu, lam):
    n = bu.shape[0]
    cb = SCAN_CB
    blk3 = pl.BlockSpec((n, SCAN_SEGMENTS, 2 * cb), lambda c: (0, 0, c))
    blk2 = lambda r: pl.BlockSpec((r, 2 * cb), lambda c: (0, c))

    def body(b_ref, lam_ref, h_ref, cin_ref, e_ref):
        def store(i, hr, hi):
            h_ref[i, :, 0:cb] = hr
            h_ref[i, :, cb:] = hi

        _scan_passes(lambda i: (b_ref[i, :, 0:cb], b_ref[i, :, cb:]), lam_ref[:, 0:cb], lam_ref[:, cb:], n, store,
                     e_ref, cin_ref, False)

    return pl.pallas_call(
        body, name="scan_fwd", grid=(SSM_CH // cb,), in_specs=[blk3, blk2(1)], out_specs=[blk3, blk2(SCAN_SEGMENTS)],
        out_shape=[jax.ShapeDtypeStruct(bu.shape, F32), jax.ShapeDtypeStruct((SCAN_SEGMENTS, 2 * SSM_CH), F32)],
        scratch_shapes=[pltpu.VMEM((SCAN_SEGMENTS, 2 * cb), F32)],
        compiler_params=_params(("parallel",)),
    )(bu, lam)


def _scan_bwd_call(dh, h, cin, lam):
    n = dh.shape[0]
    cb = SCAN_CB
    blk3 = pl.BlockSpec((n, SCAN_SEGMENTS, 2 * cb), lambda c: (0, 0, c))
    blk2 = lambda r: pl.BlockSpec((r, 2 * cb), lambda c: (0, c))

    def body(d_ref, h_ref, cin_ref, lam_ref, g_ref, dlam_ref, e_ref, c_ref, acc_ref):
        acc_ref[...] = jnp.zeros_like(acc_ref)

        def store(i, gr, gi):
            g_ref[i, :, 0:cb] = gr
            g_ref[i, :, cb:] = gi
            ip = jnp.maximum(i - 1, 0)
            hpr = jnp.where(i > 0, h_ref[ip, :, 0:cb], cin_ref[:, 0:cb])
            hpi = jnp.where(i > 0, h_ref[ip, :, cb:], cin_ref[:, cb:])
            acc_ref[:, 0:cb] += gr * hpr + gi * hpi
            acc_ref[:, cb:] += gi * hpr - gr * hpi

        _scan_passes(lambda i: (d_ref[i, :, 0:cb], d_ref[i, :, cb:]), lam_ref[:, 0:cb], -lam_ref[:, cb:], n, store,
                     e_ref, c_ref, True)
        dlam_ref[...] = acc_ref[...]

    return pl.pallas_call(
        body, name="scan_bwd", grid=(SSM_CH // cb,), in_specs=[blk3, blk3, blk2(SCAN_SEGMENTS), blk2(1)],
        out_specs=[blk3, blk2(SCAN_SEGMENTS)],
        out_shape=[jax.ShapeDtypeStruct(dh.shape, F32), jax.ShapeDtypeStruct((SCAN_SEGMENTS, 2 * SSM_CH), F32)],
        scratch_shapes=[pltpu.VMEM((SCAN_SEGMENTS, 2 * cb), F32)] * 3,
        compiler_params=_params(("parallel",)),
    )(dh, h, cin, lam)


@jax.custom_vjp
def op_scan(bu, lam):
    return _scan_fwd_call(bu, lam)[0]


def _op_scan_fwd(bu, lam):
    h, cin = _scan_fwd_call(bu, lam)
    return h, (h, cin, lam)


def _op_scan_bwd(res, dh):
    h, cin, lam = res
    g, dlam = _scan_bwd_call(dh, h, cin, lam)
    return g, jnp.sum(dlam, axis=0, keepdims=True)


op_scan.defvjp(_op_scan_fwd, _op_scan_bwd)


def _loss_call(y, target):
    t, d = y.shape
    tile = min(ROW_TILE, t)

    def body(y_ref, t_ref, dy_ref, acc_ref):
        @pl.when(pl.program_id(0) == 0)
        def _():
            acc_ref[...] = jnp.zeros_like(acc_ref)

        err = y_ref[...] - t_ref[...]
        dy_ref[...] = err * (1.0 / d)
        col = jnp.sum(err * err, axis=0, keepdims=True)
        part = col[:, 0:LANES]
        for c in range(1, d // LANES):
            part = part + col[:, c * LANES:(c + 1) * LANES]
        acc_ref[0:1, :] += part

    blk = pl.BlockSpec((tile, d), lambda i: (i, 0))
    dy, acc = pl.pallas_call(
        body, name="loss_head", grid=(t // tile,), in_specs=[blk, blk],
        out_specs=[blk, pl.BlockSpec((SUBLANES, LANES), lambda i: (0, 0))],
        out_shape=[jax.ShapeDtypeStruct((t, d), F32), jax.ShapeDtypeStruct((SUBLANES, LANES), F32)],
        compiler_params=_params(("arbitrary",)),
    )(y, target)
    return jnp.sum(acc) * (0.5 / d), dy


def _rot_cols(w, xp=jnp):
    return xp.concatenate([-w[:, 16:], w[:, :16]], axis=1)


def _ext_w_in(w, xp=jnp):
    a_val, a_gate, a_z, c_q, c_kv, k_r, b_z, u, c_z, q, k, v, d_z = xp.split(
        w, (256, 512, 768, 1024, 1152, 1184, 1440, 1696, 1952, 2208, 2336, 2464), axis=1)
    dup = lambda m: xp.concatenate([m[:, :64], m[:, :64], m[:, 64:], m[:, 64:]], axis=1)
    krblk = xp.concatenate([xp.zeros((w.shape[0], 64), w.dtype), k_r, _rot_cols(k_r, xp)], axis=1)
    return xp.concatenate([a_val, a_gate, a_z, c_q, b_z, u, c_z, q, dup(k), dup(v), d_z, c_kv, krblk], axis=1)


IN_WIDTH = 2720
IN_SHARD = IN_WIDTH // 4
IN_SHARD_PAD = 768
IN_EXT = 3072


BAND = 512


def _w_in_layout():
    src = _ext_w_in(np.arange(1, IN_WIDTH + 1, dtype=np.float32)[None, :], np)[0]
    col = np.abs(src).astype(np.int64) - 1
    row = np.where(col >= 0, (col // IN_SHARD) * IN_SHARD_PAD + col % IN_SHARD, -1)
    return row, np.sign(src)


def _w_in_layout_matrix():
    row, sign = _w_in_layout()
    rows = lax.broadcasted_iota(jnp.int32, (4 * IN_SHARD_PAD, IN_EXT), 0)
    return jnp.where(rows == jnp.asarray(row, jnp.int32)[None, :], jnp.asarray(sign, F32)[None, :], 0.0).astype(MXU_DTYPE)


def _band_tables():
    row, _ = _w_in_layout()
    nb = IN_EXT // BAND
    hit = np.zeros((nb, nb), bool)
    for c, r in enumerate(row):
        if r >= 0:
            hit[r // BAND, c // BAND] = True

    def table(h):
        depth = int(h.sum(axis=1).max())
        rows = []
        for o in range(nb):
            used = [int(b) for b in np.nonzero(h[o])[0]]
            spare = [b for b in range(nb) if not h[o, b]]
            rows.append(used + spare[:depth - len(used)])
        return np.asarray(rows, np.int32), depth

    return table(hit.T), table(hit)


def _band_mm_call(a, e, table, depth, e_transposed, name, out_dtype):
    m = a.shape[0]
    nb = IN_EXT // BAND
    dims = NT if e_transposed else NN

    def body(t_ref, a_ref, e_ref, o_ref, acc_ref):
        kk = pl.program_id(1)

        @pl.when(kk == 0)
        def _():
            acc_ref[...] = jnp.zeros_like(acc_ref)

        acc_ref[...] += _dot(a_ref[...], e_ref[...], dims)

        @pl.when(kk == depth - 1)
        def _():
            o_ref[...] = acc_ref[...].astype(out_dtype)

    blk = lambda o, kk, t: t[o * depth + kk]
    e_spec = pl.BlockSpec((BAND, BAND), (lambda o, kk, t: (o, blk(o, kk, t))) if e_transposed else (lambda o, kk, t: (blk(o, kk, t), o)))
    return pl.pallas_call(
        body, name=name, out_shape=jax.ShapeDtypeStruct((m, IN_EXT), out_dtype),
        grid_spec=pltpu.PrefetchScalarGridSpec(
            num_scalar_prefetch=1, grid=(nb, depth),
            in_specs=[pl.BlockSpec((m, BAND), lambda o, kk, t: (0, blk(o, kk, t))), e_spec],
            out_specs=pl.BlockSpec((m, BAND), lambda o, kk, t: (0, o)),
            scratch_shapes=[pltpu.VMEM((m, BAND), F32)]),
        compiler_params=_params(("parallel", "arbitrary")),
    )(jnp.asarray(table.reshape(-1)), a, e)


@jax.custom_vjp
def op_w_in_ext(w_pad, e):
    (table, depth), _ = _band_tables()
    return _band_mm_call(w_pad, e, table, depth, False, "w_in_ext", F32)


def _op_w_in_ext_fwd(w_pad, e):
    return op_w_in_ext(w_pad, e), (e, jnp.zeros((0,), w_pad.dtype))


def _op_w_in_ext_bwd(res, g):
    e, w_like = res
    _, (table, depth) = _band_tables()
    return _band_mm_call(g, e, table, depth, True, "w_in_ext_bwd", w_like.dtype), jnp.zeros_like(e)


op_w_in_ext.defvjp(_op_w_in_ext_fwd, _op_w_in_ext_bwd)


H_COLS = dict(a_val=256, a_gate=256, a_z=256, c_q=256, b_z=256, u=256, c_z=256, q=256, kdup=256, vdup=256, d_z=256,
              c_kv=128, krblk=128)
op_in_proj = make_proj(tuple(H_COLS.values()), "in_proj")


def _ext_mla(w_uq, w_ukv):
    zeros = jnp.zeros((w_ukv.shape[0], 64), w_ukv.dtype)
    uq, uk, uv = [], [], []
    for h in range(4):
        nope, rp = w_uq[:, 96 * h:96 * h + 64], w_uq[:, 96 * h + 64:96 * h + 96]
        uq += [nope, rp, _rot_cols(rp)]
        uk += [w_ukv[:, 128 * h:128 * h + 64], zeros]
        uv.append(w_ukv[:, 128 * h + 64:128 * h + 128])
    return jnp.concatenate(uq, axis=1), jnp.concatenate(uk, axis=1), jnp.concatenate(uv, axis=1)


def _scan_cols(re, im):
    parts = []
    for c in range(SSM_CH // SCAN_CB):
        parts += [re[..., c * SCAN_CB:(c + 1) * SCAN_CB], im[..., c * SCAN_CB:(c + 1) * SCAN_CB]]
    return jnp.concatenate(parts, axis=-1)


def _ext_ssm(a_re, a_im, log_dt, b_re, b_im, c_re, c_im):
    dt = jnp.exp(log_dt)[:, None]
    mag = jnp.exp(a_re * dt)
    lb_re, lb_im = mag * jnp.cos(a_im * dt), mag * jnp.sin(a_im * dt)
    den = a_re * a_re + a_im * a_im
    nr, ni = lb_re - 1.0, lb_im
    f_re = ((nr * a_re + ni * a_im) / den)[..., None]
    f_im = ((ni * a_re - nr * a_im) / den)[..., None]
    bb_re = f_re * b_re - f_im * b_im
    bb_im = f_re * b_im + f_im * b_re
    eye = jnp.eye(SSM_GROUPS, dtype=F32)
    spread = lambda a: a.transpose(0, 2, 1)[:, :, None, :] * eye[:, None, :, None]
    bd_in = lambda bb: spread(bb).reshape(SSM_GROUPS * SSM_GROUP, SSM_CH)
    bd_out = lambda cc: spread(cc).reshape(SSM_CH, SSM_GROUPS * SSM_GROUP)
    w_bu = _scan_cols(bd_in(bb_re), bd_in(bb_im))
    w_y = _scan_cols(bd_out(c_re).T, -bd_out(c_im).T).T
    lam = _scan_cols(lb_re.reshape(1, SSM_CH), lb_im.reshape(1, SSM_CH))
    return w_bu, w_y, lam


def _rope_tables(t):
    pos = jnp.arange(t, dtype=F32)
    inv_freq = ROPE_THETA ** (-jnp.arange(0, 32, 2, dtype=F32) / 32)
    ang = pos[:, None] * inv_freq[None, :]
    cos, sin = jnp.cos(ang), jnp.sin(ang)
    ones, z32, z64 = jnp.ones((t, 64), F32), jnp.zeros((t, 32), F32), jnp.zeros((t, 64), F32)
    cos1 = jnp.concatenate([ones, cos, cos, z32], axis=1)
    sin1 = jnp.concatenate([z64, sin, sin, z32], axis=1)
    return jnp.concatenate([cos1] * 4, axis=1), jnp.concatenate([sin1] * 4, axis=1)


def _to_segments(a):
    t, w = a.shape
    return a.reshape(SCAN_SEGMENTS, t // SCAN_SEGMENTS, w).transpose(1, 0, 2)


def _from_segments(a):
    n, s, w = a.shape
    return a.transpose(1, 0, 2).reshape(n * s, w)


def _layer(x, p_i, cos4, sin4, e_mat, w):
    t = x.shape[0]
    row = lambda v: v.reshape(1, -1)
    f32 = lambda v: v.astype(F32)
    hs = dict(zip(H_COLS, op_in_proj(x, op_w_in_ext(w["w_in_pad"], e_mat))))

    cv = op_conv(hs["a_val"], hs["a_gate"], w["conv_w"], row(w["conv_b"]))
    (y_a,) = op_conv_post((cv, hs["a_z"]), (row(w["conv_norm_g"]), row(w["conv_norm_b"]), f32(w["w_pw2"])))

    w_uq, w_uk, w_uv = _ext_mla(w["w_uq"], f32(w["w_ukv"]))
    q, k, v = op_mla_prep((hs["c_q"], hs["c_kv"], hs["krblk"], cos4, sin4),
                          (row(w["mla_q_norm_g"]), row(w["mla_kv_norm_g"]), w_uq, w_uk, w_uv))
    o_b = op_mla_attn(q, k, v)

    w_bu, w_y, lam = _ext_ssm(w["ssm_a_re"], w["ssm_a_im"], w["ssm_log_dt"], w["ssm_b_re"], w["ssm_b_im"],
                              w["ssm_c_re"], w["ssm_c_im"])
    u_seg = _to_segments(hs["u"]).reshape(t, BRANCH_W)
    bu = op_mm(u_seg, w_bu).reshape(t // SCAN_SEGMENTS, SCAN_SEGMENTS, 2 * SSM_CH)
    hstate = op_scan(bu, lam).reshape(t, 2 * SSM_CH)
    y_ssm = _from_segments(op_mm(hstate, w_y).reshape(t // SCAN_SEGMENTS, SCAN_SEGMENTS, BRANCH_W))
    w_glu = f32(w["w_glu"])
    (y_c,) = op_ssm_post((y_ssm, hs["u"], hs["c_z"]), (row(w["ssm_d"]), w_glu[:, :BRANCH_W], w_glu[:, BRANCH_W:]))

    sink = jnp.repeat(w["attn_sinks"], 64).reshape(1, 2 * LANES)
    o_d = op_swa_attn(hs["q"], hs["kdup"], hs["vdup"], sink)

    merged = op_merge_block(x, (y_a, o_b, y_c, o_d), (None, hs["b_z"], None, hs["d_z"]), w["w_merge"], w["w_branch"], w["b_merge"])
    (x1,) = op_ln((x, op_mm(merged, w["w_out"])), (row(w["ln_g"]), row(w["ln_b"])))
    (out,) = op_ple((x1, op_mm(p_i, w["w_ple"]), op_mm(x1, w["w_ple_gate"])), (row(w["ple_norm_g"]),))
    return out


def _forward(x, p, layers):
    cos4, sin4 = _rope_tables(x.shape[0])
    e_mat = _w_in_layout_matrix()
    for i in range(DEPTH):
        x = _layer(x, p[i], cos4, sin4, e_mat, layers[i])
    return x


SHARD_AXIS = dict(w_in=2, w_merge=2, conv_w=2, w_pw2=1, w_uq=2, w_ukv=2, w_glu=2, w_branch=3, w_out=1, w_ple=2, w_ple_gate=1)
ODD = ("w_uq", "conv_w")
BIG = tuple(n for n in SHARD_AXIS if n not in ODD)
REPLICATED = ("b_merge", "conv_b", "conv_norm_g", "conv_norm_b", "mla_q_norm_g", "mla_kv_norm_g", "ssm_a_re", "ssm_a_im",
              "ssm_log_dt", "ssm_b_re", "ssm_b_im", "ssm_c_re", "ssm_c_im", "ssm_d", "attn_sinks", "ln_g", "ln_b", "ple_norm_g")
WEIGHTS = ("w_in", "w_merge", "b_merge", "conv_w", "conv_b", "conv_norm_g", "conv_norm_b", "w_pw2", "mla_q_norm_g",
           "mla_kv_norm_g", "w_uq", "w_ukv", "ssm_a_re", "ssm_a_im", "ssm_log_dt", "ssm_b_re", "ssm_b_im", "ssm_c_re",
           "ssm_c_im", "ssm_d", "w_glu", "attn_sinks", "w_branch", "w_out", "ln_g", "ln_b", "w_ple", "w_ple_gate", "ple_norm_g")
PACK_COLS = 1024
PACK_ROWS = 16
CHIP_FLIPS = ((1, 0), (0, 1), (1, 1))
N_CHIPS = 4
N_DEV = 8


def _pack_rows(n):
    return -(-n // (SUBLANES * PACK_COLS)) * SUBLANES


def _pack(arrays, dtype):
    blocks, rows = [], 0
    for a in arrays:
        r = _pack_rows(a.size)
        flat = a.reshape(-1).astype(dtype)
        blocks.append(jnp.pad(flat, (0, r * PACK_COLS - a.size)).reshape(r, PACK_COLS))
        rows += r
    pad = -rows % PACK_ROWS
    if pad:
        blocks.append(jnp.zeros((pad, PACK_COLS), dtype))
    return jnp.concatenate(blocks, axis=0)


def _unpack(buf, shapes):
    out, row = [], 0
    for s in shapes:
        n = math.prod(s)
        r = _pack_rows(n)
        out.append(buf[row:row + r].reshape(-1)[:n].reshape(s))
        row += r
    return out


def _flip(v, bit):
    return 1 - v if bit else v


def _window(ref, axis, start, size):
    idx = [slice(None)] * len(ref.shape)
    idx[axis] = pl.ds(start, size)
    return ref.at[tuple(idx)]


def _gather_chips(srcs, axes, stacked):
    units = []
    for k, (s, a) in enumerate(zip(srcs, axes)):
        if stacked[k]:
            units += [(k, l, s.shape[1:], a - 1) for l in range(s.shape[0])]
        else:
            units.append((k, None, s.shape, a))
    nu, nb = len(units), len(srcs)

    def body(*refs):
        ins, outs = refs[:nb], refs[nb:nb + nu]
        ici_send, ici_recv, d2d_send, d2d_recv, local_sems = refs[nb + nu:]
        x, y, c = lax.axis_index("x"), lax.axis_index("y"), lax.axis_index("c")
        me = 2 * x + y

        def mine(u, half=None):
            k, l, shape, _ = units[u]
            ref = ins[k] if l is None else ins[k].at[l]
            return ref if half is None else ref.at[pl.ds(half * (shape[0] // 2), shape[0] // 2)]

        def place(u, chip, half=None):
            _, _, shape, a = units[u]
            size, rows = shape[a], shape[0] // 2
            if half is None:
                return _window(outs[u], a, chip * size, size)
            if a == 0:
                return outs[u].at[pl.ds(chip * size + half * rows, rows)]
            return _window(outs[u].at[pl.ds(half * rows, rows)], a, chip * size, size)

        local = [pltpu.make_async_copy(mine(u), place(u, me), local_sems.at[u]) for u in range(nu)]
        for cp in local:
            cp.start()
        sends = []
        for j, (bx, by) in enumerate(CHIP_FLIPS):
            for u in range(nu):
                cp = pltpu.make_async_remote_copy(src_ref=mine(u, c), dst_ref=place(u, me, c),
                                                  send_sem=ici_send.at[j * nu + u], recv_sem=ici_recv.at[j * nu + u],
                                                  device_id=(_flip(x, bx), _flip(y, by), c), device_id_type=MESH)
                cp.start()
                sends.append(cp)
        for j, (bx, by) in enumerate(CHIP_FLIPS):
            src = 2 * _flip(x, bx) + _flip(y, by)
            for u in range(nu):
                got = place(u, src, c)
                pltpu.make_async_remote_copy(src_ref=got, dst_ref=got, send_sem=ici_send.at[j * nu + u],
                                             recv_sem=ici_recv.at[j * nu + u], device_id=(x, y, c), device_id_type=MESH).wait_recv()
                cp = pltpu.make_async_remote_copy(src_ref=got, dst_ref=got, send_sem=d2d_send.at[j * nu + u],
                                                  recv_sem=d2d_recv.at[j * nu + u], device_id=(x, y, 1 - c), device_id_type=MESH)
                cp.start()
                sends.append(cp)
        for j, (bx, by) in enumerate(CHIP_FLIPS):
            src = 2 * _flip(x, bx) + _flip(y, by)
            for u in range(nu):
                other = place(u, src, 1 - c)
                pltpu.make_async_remote_copy(src_ref=other, dst_ref=other, send_sem=d2d_send.at[j * nu + u],
                                             recv_sem=d2d_recv.at[j * nu + u], device_id=(x, y, c), device_id_type=MESH).wait_recv()
        for cp in sends:
            cp.wait_send()
        for cp in local:
            cp.wait()

    full = lambda shape, a: tuple(N_CHIPS * d if i == a else d for i, d in enumerate(shape))
    res = pl.pallas_call(
        body, name="gather_weights", in_specs=[ANY] * nb, out_specs=[ANY] * nu,
        out_shape=[jax.ShapeDtypeStruct(full(shape, a), srcs[k].dtype) for k, _, shape, a in units],
        scratch_shapes=[pltpu.SemaphoreType.DMA((3 * nu,))] * 4 + [pltpu.SemaphoreType.DMA((nu,))],
    )(*srcs)
    out, it = [], iter(res)
    for k in range(nb):
        out.append([next(it) for _ in range(srcs[k].shape[0])] if stacked[k] else next(it))
    return out


def _exchange_grads(grads, axes, smalls):
    nt, ns = len(grads), len(smalls)
    sizes = [g[0].shape[a] // N_CHIPS for g, a in zip(grads, axes)]
    dev_flips = [(bx, by, bc) for bx in (0, 1) for by in (0, 1) for bc in (0, 1)][1:]
    n_remote = 3 * nt * DEPTH + 7 * ns
    n_local = nt * DEPTH + ns

    def body(*refs):
        g_refs = [refs[k * DEPTH:(k + 1) * DEPTH] for k in range(nt)]
        s_refs = refs[nt * DEPTH:nt * DEPTH + ns]
        outs = refs[nt * DEPTH + ns:nt * DEPTH + ns + nt + ns]
        recv_refs, all_refs = outs[:nt], outs[nt:]
        send_sems, recv_sems, local_sems = refs[-3:]
        x, y, c = lax.axis_index("x"), lax.axis_index("y"), lax.axis_index("c")
        me_chip = 2 * x + y
        me = 4 * x + 2 * y + c
        part = lambda k, i, chip: _window(g_refs[k][i], axes[k], chip * sizes[k], sizes[k])
        started, waits = [], []
        sem, lsem = 0, 0
        for k in range(nt):
            for i in range(DEPTH):
                cp = pltpu.make_async_copy(part(k, i, me_chip), recv_refs[k].at[i, 3], local_sems.at[lsem])
                cp.start()
                started.append(cp.wait)
                lsem += 1
                for j, (bx, by) in enumerate(CHIP_FLIPS):
                    px, py = _flip(x, bx), _flip(y, by)
                    cp = pltpu.make_async_remote_copy(src_ref=part(k, i, 2 * px + py), dst_ref=recv_refs[k].at[i, j],
                                                      send_sem=send_sems.at[sem], recv_sem=recv_sems.at[sem],
                                                      device_id=(px, py, c), device_id_type=MESH)
                    cp.start()
                    started.append(cp.wait_send)
                    waits.append(cp.wait_recv)
                    sem += 1
        for s in range(ns):
            cp = pltpu.make_async_copy(s_refs[s], all_refs[s].at[me], local_sems.at[lsem])
            cp.start()
            started.append(cp.wait)
            lsem += 1
            for bx, by, bc in dev_flips:
                peer = (_flip(x, bx), _flip(y, by), _flip(c, bc))
                cp = pltpu.make_async_remote_copy(src_ref=s_refs[s], dst_ref=all_refs[s].at[me], send_sem=send_sems.at[sem],
                                                  recv_sem=recv_sems.at[sem], device_id=peer, device_id_type=MESH)
                cp.start()
                started.append(cp.wait_send)
                src = 4 * peer[0] + 2 * peer[1] + peer[2]
                waits.append(pltpu.make_async_remote_copy(src_ref=s_refs[s], dst_ref=all_refs[s].at[src], send_sem=send_sems.at[sem],
                                                          recv_sem=recv_sems.at[sem], device_id=peer, device_id_type=MESH).wait_recv)
                sem += 1
        for w in waits + started:
            w()

    shard = lambda g, a: tuple(d // N_CHIPS if i == a else d for i, d in enumerate(g.shape))
    flat = [g for per_layer in grads for g in per_layer]
    return pl.pallas_call(
        body, name="exchange_grads", in_specs=[ANY] * (len(flat) + ns), out_specs=[ANY] * (nt + ns),
        out_shape=[jax.ShapeDtypeStruct((DEPTH, N_CHIPS, *shard(g[0], a)), g[0].dtype) for g, a in zip(grads, axes)]
        + [jax.ShapeDtypeStruct((N_DEV, *s.shape), s.dtype) for s in smalls],
        scratch_shapes=[pltpu.SemaphoreType.DMA((n_remote,)), pltpu.SemaphoreType.DMA((n_remote,)), pltpu.SemaphoreType.DMA((n_local,))],
    )(*flat, *smalls)


def _swap_cores(parts):
    nb = len(parts)

    def body(*refs):
        ins, outs, send_sems, recv_sems = refs[:nb], refs[nb:2 * nb], refs[-2], refs[-1]
        x, y, c = lax.axis_index("x"), lax.axis_index("y"), lax.axis_index("c")
        cps = [pltpu.make_async_remote_copy(src_ref=ins[k], dst_ref=outs[k], send_sem=send_sems.at[k], recv_sem=recv_sems.at[k],
                                            device_id=(x, y, 1 - c), device_id_type=MESH) for k in range(nb)]
        for cp in cps:
            cp.start()
        for cp in cps:
            cp.wait()

    return pl.pallas_call(
        body, name="swap_cores", in_specs=[ANY] * nb, out_specs=[ANY] * nb,
        out_shape=[jax.ShapeDtypeStruct(q.shape, q.dtype) for q in parts],
        scratch_shapes=[pltpu.SemaphoreType.DMA((nb,)), pltpu.SemaphoreType.DMA((nb,))],
    )(*parts)


def _sum_chips_call(recv, cols, name):
    depth, _, r, c = recv.shape
    tile = _pick(r, (512, 256, 128, 64, 32, 16))

    def body(r_ref, o_ref):
        slot = lambda s: r_ref[s, :, pl.ds(0, cols)].astype(F32)
        o_ref[...] = ((slot(3) + slot(0)) + slot(1)) + slot(2)

    return pl.pallas_call(
        body, name=name, grid=(depth, r // tile),
        in_specs=[pl.BlockSpec((None, N_CHIPS, tile, c), lambda l, i: (l, 0, i, 0))],
        out_specs=pl.BlockSpec((None, tile, cols), lambda l, i: (l, i, 0)), out_shape=jax.ShapeDtypeStruct((depth, r, cols), F32),
        compiler_params=_params(("parallel", "parallel")),
    )(recv)


def _sum_slots_call(slots, name):
    n, r, c = slots.shape
    tile = _pick(r, (512, 256, 128, 64, 32, 16, 8))

    def body(s_ref, o_ref):
        acc = s_ref[0]
        for s in range(1, n):
            acc = acc + s_ref[s]
        o_ref[...] = acc

    return pl.pallas_call(
        body, name=name, grid=(r // tile,), in_specs=[pl.BlockSpec((n, tile, c), lambda i: (0, i, 0))],
        out_specs=pl.BlockSpec((tile, c), lambda i: (i, 0)), out_shape=jax.ShapeDtypeStruct((r, c), F32),
        compiler_params=_params(("parallel",)),
    )(slots)


def _adamw_math(w, g, m, v):
    m = ADAM_B1 * m + (1.0 - ADAM_B1) * g
    v = ADAM_B2 * v + (1.0 - ADAM_B2) * (g * g)
    m_hat = m / (1.0 - ADAM_B1 ** ADAM_STEP)
    v_hat = v / (1.0 - ADAM_B2 ** ADAM_STEP)
    return -ADAM_LR * (m_hat / (jnp.sqrt(v_hat) + ADAM_EPS) + ADAM_WD * w), m, v


def _adamw_call(w, m, v, gparts, name):
    r, c = w.shape
    n = len(gparts)
    tile = _pick(r, (512, 256, 128, 64, 32, 16, 8))

    def body(w_ref, m_ref, v_ref, *refs):
        g_refs, (go_ref, d_ref, mo_ref, vo_ref) = refs[:n], refs[n:]
        g = g_refs[0][...]
        for g_ref in g_refs[1:]:
            g = g + g_ref[...]
        go_ref[...] = g
        d_ref[...], mo_ref[...], vo_ref[...] = _adamw_math(w_ref[...], g, m_ref[...], v_ref[...])

    blk = pl.BlockSpec((tile, c), lambda i: (i, 0))
    return pl.pallas_call(
        body, name=name, grid=(r // tile,), in_specs=[blk] * (3 + n),
        out_specs=[blk] * 4, out_shape=[jax.ShapeDtypeStruct((r, c), F32)] * 4,
        compiler_params=_params(("parallel",)),
    )(w, m, v, *gparts)


def _train_local(x, p, layers, target):
    y, vjp = jax.vjp(lambda x_, w_: _forward(x_, p, w_), x, layers)
    loss, dy = _loss_call(y, target)
    dx, dw = vjp(dy)
    return loss, dx, dw


def kernel(x, p, w_in, w_merge, b_merge, conv_w, conv_b, conv_norm_g, conv_norm_b, w_pw2, mla_q_norm_g, mla_kv_norm_g, w_uq, w_ukv, ssm_a_re, ssm_a_im, ssm_log_dt, ssm_b_re, ssm_b_im, ssm_c_re, ssm_c_im, ssm_d, w_glu, attn_sinks, w_branch, w_out, ln_g, ln_b, w_ple, w_ple_gate, ple_norm_g, loss_target, m_w_in, m_w_merge, m_b_merge, m_conv_w, m_conv_b, m_conv_norm_g, m_conv_norm_b, m_w_pw2, m_mla_q_norm_g, m_mla_kv_norm_g, m_w_uq, m_w_ukv, m_ssm_a_re, m_ssm_a_im, m_ssm_log_dt, m_ssm_b_re, m_ssm_b_im, m_ssm_c_re, m_ssm_c_im, m_ssm_d, m_w_glu, m_attn_sinks, m_w_branch, m_w_out, m_ln_g, m_ln_b, m_w_ple, m_w_ple_gate, m_ple_norm_g, v_w_in, v_w_merge, v_b_merge, v_conv_w, v_conv_b, v_conv_norm_g, v_conv_norm_b, v_w_pw2, v_mla_q_norm_g, v_mla_kv_norm_g, v_w_uq, v_w_ukv, v_ssm_a_re, v_ssm_a_im, v_ssm_log_dt, v_ssm_b_re, v_ssm_b_im, v_ssm_c_re, v_ssm_c_im, v_ssm_d, v_w_glu, v_attn_sinks, v_w_branch, v_w_out, v_ln_g, v_ln_b, v_w_ple, v_w_ple_gate, v_ple_norm_g):
    given = dict(locals())
    w_loc = {n: given[n] for n in WEIGHTS}
    m_loc = {n: given["m_" + n] for n in WEIGHTS}
    v_loc = {n: given["v_" + n] for n in WEIGHTS}

    me_chip = 2 * lax.axis_index("x") + lax.axis_index("y")

    wire = {n: w_loc[n].astype(MXU_DTYPE) for n in BIG}
    wire["w_in"] = jnp.pad(wire["w_in"], ((0, 0), (0, 0), (0, IN_SHARD_PAD - IN_SHARD)))
    odd_shapes = [w_loc[n].shape for n in ODD]
    gathered = _gather_chips([wire[n] for n in BIG] + [_pack([w_loc[n] for n in ODD], F32)], [SHARD_AXIS[n] for n in BIG] + [0],
                             [True] * len(BIG) + [False])
    full = dict(zip(BIG, gathered[:-1]))
    odd_parts = [_unpack(part, odd_shapes) for part in jnp.split(gathered[-1], N_CHIPS, axis=0)]
    for k, n in enumerate(ODD):
        full[n] = jnp.concatenate([odd_parts[s][k] for s in range(N_CHIPS)], axis=SHARD_AXIS[n])
    layers = []
    for i in range(DEPTH):
        layer = {n: (full[n][i] if n in full else w_loc[n][i]) for n in WEIGHTS if n != "w_in"}
        layer["w_in_pad"] = full["w_in"][i]
        layers.append(layer)

    loss, dx, dw = _train_local(x[0], p[:, 0], layers, loss_target[0])
    loss = lax.psum(loss, ("x", "y", "c"))

    key = lambda n: "w_in_pad" if n == "w_in" else n
    stacked = lambda n: jnp.stack([dw[i][n] for i in range(DEPTH)])
    small_rep = _pack([stacked(n) for n in REPLICATED], F32)
    small_odd = _pack([stacked(n) for n in ODD], F32)
    *recv, all_rep, all_odd = _exchange_grads([[dw[i][key(n)] for i in range(DEPTH)] for n in BIG],
                                              [SHARD_AXIS[n] - 1 for n in BIG], [small_rep, small_odd])
    parts = []
    for n, r in zip(BIG, recv):
        cols = w_loc[n].shape[-1]
        parts.append(_sum_chips_call(r.reshape(DEPTH, N_CHIPS, -1, r.shape[-1]), cols, "sum_chips_" + n))
    others = _swap_cores(parts)
    g_rep = _sum_slots_call(all_rep, "sum_replicated")
    g_odd = _unpack(_sum_slots_call(all_odd, "sum_odd"), [(DEPTH, *w_loc[n].shape[1:-1], N_CHIPS * w_loc[n].shape[-1]) for n in ODD])

    grads, deltas, new_m, new_v = {}, {}, {}, {}

    def adamw(n, gparts):
        shape = w_loc[n].shape
        two_d = lambda a: a.reshape(-1, shape[-1])
        res = _adamw_call(two_d(w_loc[n]), two_d(m_loc[n]), two_d(v_loc[n]), [two_d(g) for g in gparts], "adamw_" + n)
        grads[n], deltas[n], new_m[n], new_v[n] = [r.reshape(shape) for r in res]

    for n, part, other in zip(BIG, parts, others):
        adamw(n, [part, other])
    for n, g in zip(ODD, g_odd):
        size = w_loc[n].shape[-1]
        adamw(n, [lax.dynamic_slice_in_dim(g, me_chip * size, size, axis=g.ndim - 1)])
    rep_shapes = [w_loc[n].shape for n in REPLICATED]
    res = _adamw_call(_pack([w_loc[n] for n in REPLICATED], F32), _pack([m_loc[n] for n in REPLICATED], F32),
                      _pack([v_loc[n] for n in REPLICATED], F32), [g_rep], "adamw_replicated")
    for dst, buf in zip((grads, deltas, new_m, new_v), res):
        for n, a in zip(REPLICATED, _unpack(buf, rep_shapes)):
            dst[n] = a

    return (loss, dx[None], *[grads[n] for n in WEIGHTS], *[deltas[n] for n in WEIGHTS],
            *[new_m[n] for n in WEIGHTS], *[new_v[n] for n in WEIGHTS])
```

```python
import functools
import math

import jax
import jax.numpy as jnp
import numpy as np
from jax import lax
from jax.experimental import pallas as pl
from jax.experimental.pallas import tpu as pltpu

F32 = jnp.float32
BF16 = jnp.bfloat16
MXU_DTYPE = BF16
V7X_VMEM_BYTES = 64 * 1024 * 1024
VMEM_LIMIT = V7X_VMEM_BYTES * 3 // 4
LANES = 128
SUBLANES = 8

D_MODEL = 1024
DEPTH = 4
BRANCH_W = 256
CONV_W = 31
CONV_HALO = 32
CONV_CHUNK = 64
MLA_SCALE = (64 + 32) ** -0.5
SWA_SCALE = 64 ** -0.5
WINDOW = 128
ROPE_THETA = 10000.0
SSM_GROUPS, SSM_GROUP, SSM_STATE = 16, 16, 64
SSM_CH = SSM_GROUPS * SSM_STATE
SCAN_SEGMENTS = SUBLANES
SCAN_CB = 128
DEEPNORM_ALPHA = (2.0 * DEPTH) ** 0.25
LN_EPS = 1e-5
RMS_EPS = 1e-6
ADAM_LR, ADAM_B1, ADAM_B2, ADAM_EPS, ADAM_WD, ADAM_STEP = 0.001, 0.9, 0.999, 1e-08, 0.01, 10
NEG = -1e30
ROW_TILE = 512

NN = (((1,), (0,)), ((), ()))
NT = (((1,), (1,)), ((), ()))
TN = (((0,), (0,)), ((), ()))

MESH = pl.DeviceIdType.MESH
ANY = pl.BlockSpec(memory_space=pl.ANY)


def _dot(a, b, dims):
    return lax.dot_general(a.astype(MXU_DTYPE), b.astype(MXU_DTYPE), dims, preferred_element_type=F32)


def _pick(n, cands):
    for c in cands:
        if n % c == 0:
            return c
    return n


def _params(sem):
    return pltpu.CompilerParams(dimension_semantics=sem, vmem_limit_bytes=VMEM_LIMIT)


def _col_offsets(widths):
    return [sum(widths[:j]) for j in range(len(widths))]


def _silu_gate(x, z):
    return x * (z * jax.nn.sigmoid(z))


def _proj_fwd_call(x, wb, widths, name, z=None):
    t, k = x.shape
    tm = min(ROW_TILE, t)
    offs = _col_offsets(widths)
    ins = [x] if z is None else [x, z]

    def body(*refs):
        w_ref, o_refs = refs[len(ins)], refs[len(ins) + 1:]
        xv = refs[0][...] if z is None else _silu_gate(refs[0][...], refs[1][...])
        xb = xv.astype(MXU_DTYPE)
        for o_ref, off, wd in zip(o_refs, offs, widths):
            o_ref[...] = _dot(xb, w_ref[:, off:off + wd], NN)

    return pl.pallas_call(
        body, name=name, grid=(t // tm,),
        in_specs=[pl.BlockSpec((tm, k), lambda i: (i, 0))] * len(ins) + [pl.BlockSpec(wb.shape, lambda i: (0, 0))],
        out_specs=[pl.BlockSpec((tm, wd), lambda i: (i, 0)) for wd in widths],
        out_shape=[jax.ShapeDtypeStruct((t, wd), F32) for wd in widths],
        compiler_params=_params(("parallel",)),
    )(*ins, wb)


def _proj_dx_call(douts, wb, widths, name, gate=None):
    t = douts[0].shape[0]
    k = wb.shape[0]
    tm = min(ROW_TILE, t)
    offs = _col_offsets(widths)
    nd = len(douts)
    extra = [] if gate is None else list(gate)

    def body(*refs):
        d_refs, w_ref = refs[:nd], refs[nd]
        acc = jnp.zeros((tm, k), F32)
        for d_ref, off, wd in zip(d_refs, offs, widths):
            acc = acc + _dot(d_ref[...], w_ref[:, off:off + wd], NT)
        if gate is None:
            refs[-1][...] = acc
        else:
            xv, zv = refs[nd + 1][...], refs[nd + 2][...]
            sg = jax.nn.sigmoid(zv)
            refs[-2][...] = acc * (zv * sg)
            refs[-1][...] = acc * xv * (sg * (1.0 + zv * (1.0 - sg)))

    row = pl.BlockSpec((tm, k), lambda i: (i, 0))
    n_out = 1 if gate is None else 2
    res = pl.pallas_call(
        body, name=name, grid=(t // tm,),
        in_specs=[pl.BlockSpec((tm, wd), lambda i: (i, 0)) for wd in widths] + [pl.BlockSpec(wb.shape, lambda i: (0, 0))] + [row] * len(extra),
        out_specs=[row] * n_out, out_shape=[jax.ShapeDtypeStruct((t, k), F32)] * n_out,
        compiler_params=_params(("parallel",)),
    )(*douts, wb, *extra)
    return res[0] if gate is None else tuple(res)


def _proj_dw_call(x, douts, widths, name, out_dtype, z=None):
    t, k = x.shape
    n = sum(widths)
    tk = min(ROW_TILE if k * n <= 2 * 1024 * 1024 else ROW_TILE // 2, t)
    nk = t // tk
    offs = _col_offsets(widths)
    ins = [x] if z is None else [x, z]

    def body(*all_refs):
        refs = all_refs[len(ins):]
        d_refs, o_ref, acc_ref = refs[:-2], refs[-2], refs[-1]

        @pl.when(pl.program_id(0) == 0)
        def _():
            acc_ref[...] = jnp.zeros_like(acc_ref)

        xv = all_refs[0][...] if z is None else _silu_gate(all_refs[0][...], all_refs[1][...])
        xb = xv.astype(MXU_DTYPE)
        for d_ref, off, wd in zip(d_refs, offs, widths):
            acc_ref[:, off:off + wd] += _dot(xb, d_ref[...], TN)

        @pl.when(pl.program_id(0) == nk - 1)
        def _():
            o_ref[...] = acc_ref[...].astype(out_dtype)

    return pl.pallas_call(
        body, name=name, grid=(nk,),
        in_specs=[pl.BlockSpec((tk, k), lambda i: (i, 0))] * len(ins) + [pl.BlockSpec((tk, wd), lambda i: (i, 0)) for wd in widths],
        out_specs=pl.BlockSpec((k, n), lambda i: (0, 0)), out_shape=jax.ShapeDtypeStruct((k, n), out_dtype),
        scratch_shapes=[pltpu.VMEM((k, n), F32)],
        compiler_params=_params(("arbitrary",)),
    )(*ins, *douts)


def make_proj(widths, name):
    @jax.custom_vjp
    def op(x, w):
        return tuple(_proj_fwd_call(x, w.astype(MXU_DTYPE), widths, name + "_fwd"))

    def fwd(x, w):
        wb = w.astype(MXU_DTYPE)
        return tuple(_proj_fwd_call(x, wb, widths, name + "_fwd")), (x, wb, jnp.zeros((0,), w.dtype))

    def bwd(res, douts):
        x, wb, w_like = res
        return _proj_dx_call(douts, wb, widths, name + "_dx"), _proj_dw_call(x, douts, widths, name + "_dw", w_like.dtype)

    op.defvjp(fwd, bwd)
    return op


_MM_OPS = {}


def op_mm(a, w):
    n = w.shape[1]
    if n not in _MM_OPS:
        _MM_OPS[n] = make_proj((n,), "mm%d" % n)
    return _MM_OPS[n](a, w)[0]


@jax.custom_vjp
def _mm(a, w):
    return _dot(a, w, NN)


def _mm_f(a, w):
    return _dot(a, w, NN), (a, w)


def _mm_b(res, g):
    a, w = res
    return _dot(g, w, NT), _dot(a, g, TN)


_mm.defvjp(_mm_f, _mm_b)


@functools.partial(jax.custom_vjp, nondiff_argnums=(1,))
def _roll(x, shift):
    return pltpu.roll(x, shift, 1)


def _roll_f(x, shift):
    return pltpu.roll(x, shift, 1), None


def _roll_b(shift, _, g):
    return (pltpu.roll(g, (g.shape[1] - shift) % g.shape[1], 1),)


_roll.defvjp(_roll_f, _roll_b)


def _ln(x, g, b):
    mu = jnp.mean(x, axis=-1, keepdims=True)
    xc = x - mu
    var = jnp.mean(xc * xc, axis=-1, keepdims=True)
    return xc * lax.rsqrt(var + LN_EPS) * g + b


def _rms(x, g):
    ms = jnp.mean(x * x, axis=-1, keepdims=True)
    return x * lax.rsqrt(ms + RMS_EPS) * g


def _sigmoid(x):
    return jax.nn.sigmoid(x)


def _silu(x):
    return x * _sigmoid(x)


def _gelu_tanh(x):
    return x * (0.5 * (1.0 + jnp.tanh(math.sqrt(2.0 / math.pi) * (x + 0.044715 * (x * x * x)))))


def _rowwise_fwd_call(fn, rows, consts, name, tile):
    t = rows[0].shape[0]
    tile = min(tile, t)
    nr = len(rows)
    outs = jax.eval_shape(fn, *[jax.ShapeDtypeStruct((tile, r.shape[1]), F32) for r in rows],
                          *[jax.ShapeDtypeStruct(c.shape, F32) for c in consts])

    def body(*refs):
        vals = [r[...] for r in refs[:nr + len(consts)]]
        res = fn(*vals)
        for o_ref, o in zip(refs[nr + len(consts):], res):
            o_ref[...] = o

    return pl.pallas_call(
        body, name=name, grid=(t // tile,),
        in_specs=[pl.BlockSpec((tile, r.shape[1]), lambda i: (i, 0)) for r in rows]
        + [pl.BlockSpec(c.shape, lambda i: (0, 0)) for c in consts],
        out_specs=[pl.BlockSpec((tile, o.shape[1]), lambda i: (i, 0)) for o in outs],
        out_shape=[jax.ShapeDtypeStruct((t, o.shape[1]), F32) for o in outs],
        compiler_params=_params(("parallel",)),
    )(*rows, *consts)


def _rowwise_bwd_call(fn, rows, consts, douts, row_diff, name, tile, row_grad_dtype=F32):
    t = rows[0].shape[0]
    tile = min(tile, t)
    nr, nc, nd = len(rows), len(consts), len(douts)
    diff_idx = [i for i in range(nr) if row_diff[i]]

    def body(*refs):
        rv = [r[...] for r in refs[:nr]]
        cv = [r[...] for r in refs[nr:nr + nc]]
        dv = [r[...] for r in refs[nr + nc:nr + nc + nd]]
        out_refs = refs[nr + nc + nd:]

        def f(*diff):
            full = list(rv)
            for k, i in enumerate(diff_idx):
                full[i] = diff[k]
            return fn(*full, *diff[len(diff_idx):])

        _, vjp = jax.vjp(f, *[rv[i] for i in diff_idx], *cv)
        grads = vjp(tuple(dv))
        for k in range(len(diff_idx)):
            out_refs[k][...] = grads[k].astype(row_grad_dtype)
        first = pl.program_id(0) == 0
        for k in range(nc):
            acc_ref = out_refs[len(diff_idx) + k]
            g = grads[len(diff_idx) + k]

            @pl.when(first)
            def _(acc_ref=acc_ref, g=g):
                acc_ref[...] = g

            @pl.when(jnp.logical_not(first))
            def _(acc_ref=acc_ref, g=g):
                acc_ref[...] += g

    res = pl.pallas_call(
        body, name=name, grid=(t // tile,),
        in_specs=[pl.BlockSpec((tile, r.shape[1]), lambda i: (i, 0)) for r in rows]
        + [pl.BlockSpec(c.shape, lambda i: (0, 0)) for c in consts]
        + [pl.BlockSpec((tile, d.shape[1]), lambda i: (i, 0)) for d in douts],
        out_specs=[pl.BlockSpec((tile, rows[i].shape[1]), lambda i_: (i_, 0)) for i in diff_idx]
        + [pl.BlockSpec(c.shape, lambda i: (0, 0)) for c in consts],
        out_shape=[jax.ShapeDtypeStruct(rows[i].shape, row_grad_dtype) for i in diff_idx]
        + [jax.ShapeDtypeStruct(c.shape, F32) for c in consts],
        compiler_params=_params(("arbitrary",)),
    )(*rows, *consts, *douts)
    return res[:len(diff_idx)], res[len(diff_idx):]


def make_rowwise(fn, name, row_diff, tile=ROW_TILE):
    @jax.custom_vjp
    def op(rows, consts):
        return tuple(_rowwise_fwd_call(fn, rows, consts, name + "_fwd", tile))

    def fwd(rows, consts):
        return op(rows, consts), (rows, consts)

    def bwd(res, douts):
        rows, consts = res
        drows, dconsts = _rowwise_bwd_call(fn, rows, consts, douts, row_diff, name + "_bwd", tile)
        it = iter(drows)
        full = tuple(next(it) if row_diff[i] else jnp.zeros_like(rows[i]) for i in range(len(rows)))
        return full, tuple(dconsts)

    op.defvjp(fwd, bwd)
    return op


def _conv_post_fn(cv, a_z, ng, nb, w_pw2):
    return (_mm(_silu(_ln(cv, ng, nb)), w_pw2) * _silu(a_z),)


def _mla_prep_fn(c_q, c_kv, krblk, cos4, sin4, qg, kvg, w_uq, w_uk, w_uv):
    qe = _mm(_rms(c_q, qg), w_uq)
    q = qe * cos4 + _roll(qe, qe.shape[1] - 32) * sin4
    cos1, sin1 = cos4[:, :LANES], sin4[:, :LANES]
    kr = krblk * cos1 + _roll(krblk, LANES - 32) * sin1
    kn = _rms(c_kv, kvg)
    k = _mm(kn, w_uk) + jnp.concatenate([kr, kr, kr, kr], axis=1)
    return q, k, _mm(kn, w_uv)


def _ssm_post_fn(y, u, c_z, d, w_a, w_b):
    y2 = _gelu_tanh(y + d * u)
    return (_mm(y2, w_a) * _sigmoid(_mm(y2, w_b)) * _silu(c_z),)


def _merge_fn(br0, br1, br2, br3, gl0, gl1, gl2, gl3, b0, b1, b2, b3):
    return (_sigmoid(gl0 + b0) * br0 + _sigmoid(gl1 + b1) * br1 + _sigmoid(gl2 + b2) * br2 + _sigmoid(gl3 + b3) * br3,)


def _ln_fn(x, mo, g, b):
    return (_ln(DEEPNORM_ALPHA * x + mo, g, b),)


def _ple_fn(x1, pe, gl, g):
    return (x1 + _rms(pe * _sigmoid(gl), g),)


op_conv_post = make_rowwise(_conv_post_fn, "conv_post", (True, True))
op_mla_prep = make_rowwise(_mla_prep_fn, "mla_prep", (True, True, True, False, False))
op_ssm_post = make_rowwise(_ssm_post_fn, "ssm_post", (True, True, True))
MERGE_TILE = ROW_TILE // 2
MERGE_WIDTHS = (D_MODEL,) * 4


@jax.custom_vjp
def op_merge_block(x, ys, zs, w_merge, w_branch, b_merge):
    return _merge_block_fwd(x, ys, zs, w_merge, w_branch, b_merge)[0]


def _merge_block_fwd(x, ys, zs, w_merge, w_branch, b_merge):
    wm, wb = w_merge.astype(MXU_DTYPE), w_branch.astype(MXU_DTYPE)
    gl = _proj_fwd_call(x, wm, MERGE_WIDTHS, "merge_proj_fwd")
    br = [_proj_fwd_call(ys[n], wb[n], (D_MODEL,), "branch_proj_fwd", zs[n])[0] for n in range(4)]
    bm = tuple(b_merge[n * D_MODEL:(n + 1) * D_MODEL].reshape(1, -1) for n in range(4))
    (merged,) = _rowwise_fwd_call(_merge_fn, (*br, *gl), bm, "merge_fwd", MERGE_TILE)
    return merged, (x, ys, zs, wm, wb, tuple(br), tuple(gl), bm, jnp.zeros((0,), w_merge.dtype), jnp.zeros((0,), w_branch.dtype))


def _merge_block_bwd(res, dmerged):
    x, ys, zs, wm, wb, br, gl, bm, wm_like, wb_like = res
    drows, dbm = _rowwise_bwd_call(_merge_fn, (*br, *gl), bm, (dmerged,), (True,) * 8, "merge_bwd", MERGE_TILE, MXU_DTYPE)
    dbr, dgl = drows[:4], drows[4:]
    dx = _proj_dx_call(dgl, wm, MERGE_WIDTHS, "merge_proj_dx")
    dwm = _proj_dw_call(x, dgl, MERGE_WIDTHS, "merge_proj_dw", wm_like.dtype)
    dys, dzs = [], []
    for n in range(4):
        if zs[n] is None:
            dys.append(_proj_dx_call([dbr[n]], wb[n], (D_MODEL,), "branch_proj_dx"))
            dzs.append(None)
        else:
            dy, dz = _proj_dx_call([dbr[n]], wb[n], (D_MODEL,), "branch_proj_dx", (ys[n], zs[n]))
            dys.append(dy)
            dzs.append(dz)
    dwb = jnp.stack([_proj_dw_call(ys[n], [dbr[n]], (D_MODEL,), "branch_proj_dw", wb_like.dtype, zs[n]) for n in range(4)])
    return dx, tuple(dys), tuple(dzs), dwm, dwb, jnp.concatenate([d.reshape(-1) for d in dbm])


op_merge_block.defvjp(_merge_block_fwd, _merge_block_bwd)
op_ln = make_rowwise(_ln_fn, "post_ln", (True, True))
op_ple = make_rowwise(_ple_fn, "ple", (True, True, True))


def _shift_copies(buf, shifted, tile):
    for r in range(1, SUBLANES):
        shifted[r - 1, :, :] = buf[pl.ds(r, tile + CONV_HALO - SUBLANES), :]


def _tap(buf, shifted, off, tile):
    r = off % SUBLANES
    return buf[pl.ds(off, tile), :] if r == 0 else shifted[r - 1, pl.ds(off - r, tile), :]


def _shift_scratch(tile, w):
    return pltpu.VMEM((SUBLANES - 1, tile + CONV_HALO - SUBLANES, w), F32)


def _conv_fwd_call(a_val, a_gate, w32, b):
    t, w = a_val.shape
    tile = min(ROW_TILE, t)
    per = tile // CONV_HALO
    cur = pl.BlockSpec((tile, w), lambda i: (i, 0))
    prev = pl.BlockSpec((CONV_HALO, w), lambda i: (jnp.maximum(i * per - 1, 0), 0))

    def body(av_ref, avh_ref, ag_ref, agh_ref, w_ref, b_ref, cv_ref, buf, shifted):
        i = pl.program_id(0)
        gh = avh_ref[...] * _sigmoid(agh_ref[...])
        buf[0:CONV_HALO, :] = jnp.where(i > 0, gh, 0.0)
        buf[CONV_HALO:, :] = av_ref[...] * _sigmoid(ag_ref[...])
        _shift_copies(buf, shifted, tile)
        for c0 in range(0, tile, CONV_CHUNK):
            acc = jnp.zeros((CONV_CHUNK, w), F32) + b_ref[...]
            for j in range(CONV_W):
                acc = acc + w_ref[j:j + 1, :] * _tap(buf, shifted, c0 + CONV_HALO - (CONV_W - 1) + j, CONV_CHUNK)
            cv_ref[c0:c0 + CONV_CHUNK, :] = acc

    return pl.pallas_call(
        body, name="conv_fwd", grid=(t // tile,),
        in_specs=[cur, prev, cur, prev, pl.BlockSpec((CONV_HALO, w), lambda i: (0, 0)), pl.BlockSpec((1, w), lambda i: (0, 0))],
        out_specs=cur, out_shape=jax.ShapeDtypeStruct((t, w), F32),
        scratch_shapes=[pltpu.VMEM((tile + CONV_HALO, w), F32), _shift_scratch(tile, w)],
        compiler_params=_params(("parallel",)),
    )(a_val, a_val, a_gate, a_gate, w32, b)


def _conv_bwd_call(a_val, a_gate, w32, dcv):
    t, w = a_val.shape
    tile = min(ROW_TILE, t)
    n = t // tile
    per = tile // CONV_HALO
    cur = pl.BlockSpec((tile, w), lambda i: (i, 0))
    prev = pl.BlockSpec((CONV_HALO, w), lambda i: (jnp.maximum(i * per - 1, 0), 0))
    nxt = pl.BlockSpec((CONV_HALO, w), lambda i: (jnp.minimum((i + 1) * per, t // CONV_HALO - 1), 0))
    full = lambda r: pl.BlockSpec((r, w), lambda i: (0, 0))

    def body(av_ref, avh_ref, ag_ref, agh_ref, w_ref, d_ref, dn_ref, dav_ref, dag_ref, dw_ref, db_ref, gbuf, dbuf, gsh, dsh):
        i = pl.program_id(0)
        gh = avh_ref[...] * _sigmoid(agh_ref[...])
        gbuf[0:CONV_HALO, :] = jnp.where(i > 0, gh, 0.0)
        gbuf[CONV_HALO:, :] = av_ref[...] * _sigmoid(ag_ref[...])
        dbuf[0:tile, :] = d_ref[...]
        dbuf[tile:, :] = jnp.where(i < n - 1, dn_ref[...], 0.0)

        @pl.when(i == 0)
        def _():
            dw_ref[...] = jnp.zeros_like(dw_ref)
            db_ref[...] = jnp.zeros_like(db_ref)

        _shift_copies(gbuf, gsh, tile)
        _shift_copies(dbuf, dsh, tile)
        for c0 in range(0, tile, CONV_CHUNK):
            rows = slice(c0, c0 + CONV_CHUNK)
            d = d_ref[rows, :]
            dg = jnp.zeros((CONV_CHUNK, w), F32)
            for j in range(CONV_W):
                dg = dg + w_ref[j:j + 1, :] * _tap(dbuf, dsh, c0 + CONV_W - 1 - j, CONV_CHUNK)
                dw_ref[j:j + 1, :] += jnp.sum(d * _tap(gbuf, gsh, c0 + CONV_HALO - (CONV_W - 1) + j, CONV_CHUNK), axis=0, keepdims=True)
            db_ref[...] += jnp.sum(d, axis=0, keepdims=True)
            av = av_ref[rows, :]
            sg = _sigmoid(ag_ref[rows, :])
            dav_ref[rows, :] = dg * sg
            dag_ref[rows, :] = dg * av * sg * (1.0 - sg)

    return pl.pallas_call(
        body, name="conv_bwd", grid=(n,),
        in_specs=[cur, prev, cur, prev, full(CONV_HALO), cur, nxt],
        out_specs=[cur, cur, full(CONV_HALO), full(1)],
        out_shape=[jax.ShapeDtypeStruct((t, w), F32), jax.ShapeDtypeStruct((t, w), F32),
                   jax.ShapeDtypeStruct((CONV_HALO, w), F32), jax.ShapeDtypeStruct((1, w), F32)],
        scratch_shapes=[pltpu.VMEM((tile + CONV_HALO, w), F32), pltpu.VMEM((tile + CONV_HALO, w), F32),
                        _shift_scratch(tile, w), _shift_scratch(tile, w)],
        compiler_params=_params(("arbitrary",)),
    )(a_val, a_val, a_gate, a_gate, w32, dcv, dcv)


def _pad_taps(conv_w):
    return jnp.concatenate([conv_w, jnp.zeros((CONV_HALO - CONV_W, conv_w.shape[1]), F32)], axis=0)


@jax.custom_vjp
def op_conv(a_val, a_gate, conv_w, conv_b):
    return _conv_fwd_call(a_val, a_gate, _pad_taps(conv_w), conv_b)


def _op_conv_fwd(a_val, a_gate, conv_w, conv_b):
    return op_conv(a_val, a_gate, conv_w, conv_b), (a_val, a_gate, conv_w)


def _op_conv_bwd(res, dcv):
    a_val, a_gate, conv_w = res
    dav, dag, dw, db = _conv_bwd_call(a_val, a_gate, _pad_taps(conv_w), dcv)
    return dav, dag, dw[:CONV_W], db


op_conv.defvjp(_op_conv_fwd, _op_conv_bwd)


def _head_masks(rows):
    lane = lax.broadcasted_iota(jnp.int32, (rows, LANES), 1)
    return lane < 64, lane >= 64


def _head_row(vals, mask):
    return jnp.max(jnp.where(mask, vals, NEG), axis=1, keepdims=True)


def _attn_valid(qpos, kpos, window):
    valid = kpos <= qpos
    if window is not None:
        valid = jnp.logical_and(valid, qpos - kpos < window)
    return valid


def _flash_fwd_call(q, k, v, sink, *, window, shared_k, scale, blk, blk_q, name):
    t = q.shape[0]
    qw = LANES if shared_k else 2 * LANES
    pairs = v.shape[1] // LANES
    tk = min(blk, t)
    tq = min(blk_q, t)
    has_sink = sink is not None
    one_step = window is not None and tk == 2 * tq and window <= tq
    kstride = tq if one_step else tk

    def body(*refs):
        if has_sink:
            q_ref, k_ref, v_ref, s_ref, o_ref, lse_ref, k_mxu, v0_mxu, v1_mxu = refs
        else:
            q_ref, k_ref, v_ref, o_ref, lse_ref, k_mxu, v0_mxu, v1_mxu = refs
        v_mxu = (v0_mxu, v1_mxu)
        i = pl.program_id(1)

        @pl.when(i == 0)
        def _():
            full_masks = _head_masks(t)
            k_mxu[...] = k_ref[...].astype(MXU_DTYPE)
            for h in range(2):
                v_mxu[h][...] = jnp.where(full_masks[h], v_ref[...], 0.0).astype(MXU_DTYPE)

        qb = q_ref[...]
        masks = _head_masks(tq)
        row_masks = _head_masks(1)
        qh = [(jnp.where(masks[h], qb, 0.0) if shared_k else qb[:, h * LANES:(h + 1) * LANES]).astype(MXU_DTYPE) for h in range(2)]
        qpos = i * tq + lax.broadcasted_iota(jnp.int32, (tq, tk), 0)
        if has_sink:
            m_init = [jnp.zeros((tq, 1), F32) + _head_row(s_ref[...], row_masks[h]) for h in range(2)]
            l_init = [jnp.ones((tq, 1), F32)] * 2
        else:
            m_init = [jnp.full((tq, 1), NEG, F32)] * 2
            l_init = [jnp.zeros((tq, 1), F32)] * 2

        def make_step(masked):
            def step(j, carry):
                m0, l0, m1, l1, acc = carry
                start = pl.multiple_of(j * kstride, kstride)
                kb = k_mxu[pl.ds(start, tk), :]
                if masked:
                    valid = _attn_valid(qpos, start + lax.broadcasted_iota(jnp.int32, (tq, tk), 1), window)
                new, alphas, pv = [], [], []
                for h, (m, l) in enumerate(((m0, l0), (m1, l1))):
                    kh = kb if shared_k else kb[:, h * LANES:(h + 1) * LANES]
                    s = _dot(qh[h], kh, NT) * scale
                    if masked:
                        s = jnp.where(valid, s, NEG)
                    m_new = jnp.maximum(m, jnp.max(s, axis=1, keepdims=True))
                    alpha = jnp.exp(m - m_new)
                    p = jnp.exp(s - m_new)
                    new += [m_new, alpha * l + jnp.sum(p, axis=1, keepdims=True)]
                    alphas.append(alpha)
                    pv.append(_dot(p, v_mxu[h][pl.ds(start, tk), :], NN))
                acc = acc * jnp.where(masks[0], alphas[0], alphas[1]) + pv[0] + pv[1]
                return new[0], new[1], new[2], new[3], acc
            return step

        carry = (m_init[0], l_init[0], m_init[1], l_init[1], jnp.zeros((tq, LANES), F32))
        last = (i * tq + tq - 1) // tk
        if window is None:
            n_full = (i * tq + 1) // tk
            carry = lax.fori_loop(0, n_full, make_step(False), carry)
            carry = lax.fori_loop(n_full, last + 1, make_step(True), carry)
        elif one_step:
            carry = make_step(True)(jnp.maximum(i - 1, 0), carry)
        else:
            carry = lax.fori_loop(jnp.maximum(i * tq - (window - 1), 0) // tk, last + 1, make_step(True), carry)
        m0, l0, m1, l1, acc = carry
        o_ref[...] = acc / jnp.where(masks[0], l0, l1)
        lse_ref[...] = jnp.where(masks[0], m0 + jnp.log(l0), m1 + jnp.log(l1))

    in_specs = [pl.BlockSpec((tq, qw), lambda p, i: (i, p)), pl.BlockSpec((t, qw), lambda p, i: (0, p)),
                pl.BlockSpec((t, LANES), lambda p, i: (0, p))]
    args = [q, k, v]
    if has_sink:
        in_specs.append(pl.BlockSpec((1, LANES), lambda p, i: (0, p)))
        args.append(sink)
    blk_o = pl.BlockSpec((tq, LANES), lambda p, i: (i, p))
    return pl.pallas_call(
        body, name=name, grid=(pairs, t // tq), in_specs=in_specs, out_specs=[blk_o, blk_o],
        out_shape=[jax.ShapeDtypeStruct((t, pairs * LANES), F32)] * 2,
        scratch_shapes=[pltpu.VMEM((t, qw), MXU_DTYPE), pltpu.VMEM((t, LANES), MXU_DTYPE), pltpu.VMEM((t, LANES), MXU_DTYPE)],
        compiler_params=_params(("arbitrary", "arbitrary")),
    )(*args)


def _flash_bwd_call(q, k, v, sink, o, lse, do, *, window, shared_k, scale, blk, blk_q, name):
    t = q.shape[0]
    qw = LANES if shared_k else 2 * LANES
    pairs = v.shape[1] // LANES
    tk = min(blk, t)
    tq = min(blk_q, t)
    assert tk % tq == 0 or tq % tk == 0
    nq = t // tq
    one_step = window is not None and tq == 2 * tk and window <= tk
    qstride = tk if one_step else tq
    has_sink = sink is not None

    def body(*refs):
        if has_sink:
            q_ref, k_ref, v_ref, o_ref, lse_ref, do_ref, s_ref, dq_ref, dk_ref, dv_ref, ds_ref = refs[:11]
        else:
            q_ref, k_ref, v_ref, o_ref, lse_ref, do_ref, dq_ref, dk_ref, dv_ref = refs[:9]
        q_mxu, do_mxu, lse_h, dsum_h = refs[-8:-6], refs[-6:-4], refs[-4:-2], refs[-2:]
        j = pl.program_id(1)
        masks = _head_masks(tq)
        row_masks = _head_masks(1)

        @pl.when(j == 0)
        def _():
            dq_ref[...] = jnp.zeros_like(dq_ref)
            full_masks = _head_masks(t)
            prod = do_ref[...] * o_ref[...]
            parts = []
            for h in range(2):
                qh = jnp.where(full_masks[h], q_ref[...], 0.0) if shared_k else q_ref[:, h * LANES:(h + 1) * LANES]
                q_mxu[h][...] = qh.astype(MXU_DTYPE)
                do_mxu[h][...] = jnp.where(full_masks[h], do_ref[...], 0.0).astype(MXU_DTYPE)
                dsum = jnp.sum(jnp.where(full_masks[h], prod, 0.0), axis=1, keepdims=True)
                lse = _head_row(lse_ref[...], full_masks[h])
                dsum_h[h][...] = jnp.zeros((t, LANES), F32) + dsum
                lse_h[h][...] = jnp.zeros((t, LANES), F32) + lse
                if has_sink:
                    ps = jnp.exp(_head_row(s_ref[...], row_masks[h]) - lse)
                    parts.append(-jnp.sum(ps * dsum, axis=0, keepdims=True))
            if has_sink:
                ds_ref[...] = jnp.zeros((SUBLANES, LANES), F32) + jnp.where(row_masks[0], parts[0], parts[1])

        kb = k_ref[...].astype(MXU_DTYPE)
        vb = v_ref[...].astype(MXU_DTYPE)
        kh = [kb if shared_k else kb[:, h * LANES:(h + 1) * LANES] for h in range(2)]
        kpos = j * tk + lax.broadcasted_iota(jnp.int32, (tq, tk), 1)
        lanes_of = lambda a: a if tk == LANES else jnp.concatenate([a] * (tk // LANES), axis=1)

        def make_step(masked):
            def step(i, carry):
                dk0, dk1, dv = carry
                start = pl.multiple_of(i * qstride, qstride)
                if masked:
                    valid = _attn_valid(start + lax.broadcasted_iota(jnp.int32, (tq, tk), 0), kpos, window)
                dks, dqs = [], []
                for h in range(2):
                    qh = q_mxu[h][pl.ds(start, tq), :]
                    doh = do_mxu[h][pl.ds(start, tq), :]
                    s = _dot(qh, kh[h], NT) * scale
                    if masked:
                        s = jnp.where(valid, s, NEG)
                    p = jnp.exp(s - lanes_of(lse_h[h][pl.ds(start, tq), :]))
                    dp = _dot(doh, vb, NT)
                    dsc = p * (dp - lanes_of(dsum_h[h][pl.ds(start, tq), :])) * scale
                    dv = dv + _dot(p, doh, TN)
                    dks.append(_dot(dsc, qh, TN))
                    dq_h = _dot(dsc, kh[h], NN)
                    dqs.append(jnp.where(masks[h], dq_h, 0.0) if shared_k else dq_h)
                if shared_k:
                    dq_ref[pl.ds(start, tq), :] += dqs[0] + dqs[1]
                else:
                    dq_ref[pl.ds(start, tq), :] += jnp.concatenate(dqs, axis=1)
                return dk0 + dks[0], dk1 + dks[1], dv
            return step

        zero = jnp.zeros((tk, LANES), F32)
        carry = (zero, zero, zero)
        first = (j * tk) // tq
        if window is None:
            n_full = jnp.minimum(((j + 1) * tk + tq - 2) // tq, nq)
            carry = lax.fori_loop(first, n_full, make_step(True), carry)
            carry = lax.fori_loop(n_full, nq, make_step(False), carry)
        elif one_step:
            carry = make_step(True)(jnp.minimum(j, t // tk - 2), carry)
        else:
            carry = lax.fori_loop(first, jnp.minimum(nq, (j * tk + tk - 1 + window - 1) // tq + 1), make_step(True), carry)
        dk0, dk1, dv = carry
        dk_ref[...] = dk0 + dk1 if shared_k else jnp.concatenate([dk0, dk1], axis=1)
        dv_ref[...] = dv

    full = lambda w: pl.BlockSpec((t, w), lambda p, j: (0, p))
    blkspec = lambda w: pl.BlockSpec((tk, w), lambda p, j: (j, p))
    in_specs = [full(qw), blkspec(qw), blkspec(LANES), full(LANES), full(LANES), full(LANES)]
    args = [q, k, v, o, lse, do]
    out_specs = [full(qw), blkspec(qw), blkspec(LANES)]
    out_shape = [jax.ShapeDtypeStruct(q.shape, F32), jax.ShapeDtypeStruct(k.shape, F32), jax.ShapeDtypeStruct(v.shape, F32)]
    if has_sink:
        in_specs.append(pl.BlockSpec((1, LANES), lambda p, j: (0, p)))
        args.append(sink)
        out_specs.append(pl.BlockSpec((SUBLANES, LANES), lambda p, j: (0, p)))
        out_shape.append(jax.ShapeDtypeStruct((SUBLANES, pairs * LANES), F32))
    return pl.pallas_call(
        body, name=name, grid=(pairs, t // tk), in_specs=in_specs, out_specs=out_specs, out_shape=out_shape,
        scratch_shapes=[pltpu.VMEM((t, LANES), MXU_DTYPE)] * 4 + [pltpu.VMEM((t, LANES), F32)] * 4,
        compiler_params=_params(("arbitrary", "arbitrary")),
    )(*args)


_MLA_CFG = dict(window=None, shared_k=False, scale=MLA_SCALE, blk=256)
_SWA_CFG = dict(window=WINDOW, shared_k=True, scale=SWA_SCALE, blk=128)
_MLA_FWD_CFG = dict(_MLA_CFG, blk=512, blk_q=256)
_SWA_FWD_CFG = dict(_SWA_CFG, blk=512, blk_q=256)
_MLA_BWD_CFG = dict(_MLA_CFG, blk=512, blk_q=512)
_SWA_BWD_CFG = dict(_SWA_CFG, blk=256, blk_q=512)


@jax.custom_vjp
def op_mla_attn(q, k, v):
    return _flash_fwd_call(q, k, v, None, name="mla_fwd", **_MLA_FWD_CFG)[0]


def _op_mla_attn_fwd(q, k, v):
    o, lse = _flash_fwd_call(q, k, v, None, name="mla_fwd", **_MLA_FWD_CFG)
    return o, (q, k, v, o, lse)


def _op_mla_attn_bwd(res, do):
    q, k, v, o, lse = res
    return tuple(_flash_bwd_call(q, k, v, None, o, lse, do, name="mla_bwd", **_MLA_BWD_CFG))


op_mla_attn.defvjp(_op_mla_attn_fwd, _op_mla_attn_bwd)


@jax.custom_vjp
def op_swa_attn(q, k, v, sink):
    return _flash_fwd_call(q, k, v, sink, name="swa_fwd", **_SWA_FWD_CFG)[0]


def _op_swa_attn_fwd(q, k, v, sink):
    o, lse = _flash_fwd_call(q, k, v, sink, name="swa_fwd", **_SWA_FWD_CFG)
    return o, (q, k, v, sink, o, lse)


def _op_swa_attn_bwd(res, do):
    q, k, v, sink, o, lse = res
    dq, dk, dv, dsink = _flash_bwd_call(q, k, v, sink, o, lse, do, name="swa_bwd", **_SWA_BWD_CFG)
    first_lane = lax.broadcasted_iota(jnp.int32, (1, dsink.shape[1]), 1) % 64 == 0
    return dq, dk, dv, jnp.where(first_lane, dsink[:1], 0.0)


op_swa_attn.defvjp(_op_swa_attn_fwd, _op_swa_attn_bwd)


def _complex_power(ar, ai, n):
    for _ in range(int(math.log2(n))):
        ar, ai = ar * ar - ai * ai, 2.0 * ar * ai
    return ar, ai


def _scan_passes(load_b, a1r, a1i, n, store, e_ref, c_ref, reverse):
    cb = a1r.shape[1]
    ar = jnp.zeros((SCAN_SEGMENTS, cb), F32) + a1r
    ai = jnp.zeros((SCAN_SEGMENTS, cb), F32) + a1i
    a2r, a2i = ar * ar - ai * ai, 2.0 * ar * ai
    idx = (lambda k: n - 1 - k) if reverse else (lambda k: k)
    mac = lambda pr_, pi__, h, b: (pr_ * h[0] - pi__ * h[1] + b[0], pr_ * h[1] + pi__ * h[0] + b[1])

    def load_pair(ii):
        k = jnp.minimum(2 * ii, n - 2)
        b0, b1 = load_b(idx(k)), load_b(idx(k + 1))
        return b0, mac(ar, ai, b0, b1)

    def local(ii, carry):
        h, c = carry
        return mac(a2r, a2i, h, c), load_pair(ii + 1)[1]

    zero = jnp.zeros((SCAN_SEGMENTS, cb), F32)
    (er, ei), _ = lax.fori_loop(0, n // 2, local, ((zero, zero), load_pair(0)[1]))
    e_ref[:, 0:cb] = er
    e_ref[:, cb:] = ei
    pr, pi_ = _complex_power(a1r, a1i, n)
    cr = jnp.zeros((1, cb), F32)
    ci = jnp.zeros((1, cb), F32)
    order = range(SCAN_SEGMENTS - 1, -1, -1) if reverse else range(SCAN_SEGMENTS)
    for s in order:
        c_ref[s:s + 1, 0:cb] = cr
        c_ref[s:s + 1, cb:] = ci
        er1, ei1 = e_ref[s:s + 1, 0:cb], e_ref[s:s + 1, cb:]
        cr, ci = pr * cr - pi_ * ci + er1, pr * ci + pi_ * cr + ei1

    def second(ii, carry):
        h, b0, c = carry
        h0 = mac(ar, ai, h, b0)
        h1 = mac(a2r, a2i, h, c)
        store(idx(2 * ii), *h0)
        store(idx(2 * ii + 1), *h1)
        nb0, nc = load_pair(ii + 1)
        return h1, nb0, nc

    b0, c0 = load_pair(0)
    lax.fori_loop(0, n // 2, second, ((c_ref[:, 0:cb], c_ref[:, cb:]), b0, c0))


def _scan_fwd_call(bu, lam):
    n = bu.shape[0]
    cb = SCAN_CB
    blk3 = pl.BlockSpec((n, SCAN_SEGMENTS, 2 * cb), lambda c: (0, 0, c))
    blk2 = lambda r: pl.BlockSpec((r, 2 * cb), lambda c: (0, c))

    def body(b_ref, lam_ref, h_ref, cin_ref, e_ref):
        def store(i, hr, hi):
            h_ref[i, :, 0:cb] = hr
            h_ref[i, :, cb:] = hi

        _scan_passes(lambda i: (b_ref[i, :, 0:cb], b_ref[i, :, cb:]), lam_ref[:, 0:cb], lam_ref[:, cb:], n, store,
                     e_ref, cin_ref, False)

    return pl.pallas_call(
        body, name="scan_fwd", grid=(SSM_CH // cb,), in_specs=[blk3, blk2(1)], out_specs=[blk3, blk2(SCAN_SEGMENTS)],
        out_shape=[jax.ShapeDtypeStruct(bu.shape, F32), jax.ShapeDtypeStruct((SCAN_SEGMENTS, 2 * SSM_CH), F32)],
        scratch_shapes=[pltpu.VMEM((SCAN_SEGMENTS, 2 * cb), F32)],
        compiler_params=_params(("parallel",)),
    )(bu, lam)


def _scan_bwd_call(dh, h, cin, lam):
    n = dh.shape[0]
    cb = SCAN_CB
    blk3 = pl.BlockSpec((n, SCAN_SEGMENTS, 2 * cb), lambda c: (0, 0, c))
    blk2 = lambda r: pl.BlockSpec((r, 2 * cb), lambda c: (0, c))

    def body(d_ref, h_ref, cin_ref, lam_ref, g_ref, dlam_ref, e_ref, c_ref, acc_ref):
        acc_ref[...] = jnp.zeros_like(acc_ref)

        def store(i, gr, gi):
            g_ref[i, :, 0:cb] = gr
            g_ref[i, :, cb:] = gi
            ip = jnp.maximum(i - 1, 0)
            hpr = jnp.where(i > 0, h_ref[ip, :, 0:cb], cin_ref[:, 0:cb])
            hpi = jnp.where(i > 0, h_ref[ip, :, cb:], cin_ref[:, cb:])
            acc_ref[:, 0:cb] += gr * hpr + gi * hpi
            acc_ref[:, cb:] += gi * hpr - gr * hpi

        _scan_passes(lambda i: (d_ref[i, :, 0:cb], d_ref[i, :, cb:]), lam_ref[:, 0:cb], -lam_ref[:, cb:], n, store,
                     e_ref, c_ref, True)
        dlam_ref[...] = acc_ref[...]

    return pl.pallas_call(
        body, name="scan_bwd", grid=(SSM_CH // cb,), in_specs=[blk3, blk3, blk2(SCAN_SEGMENTS), blk2(1)],
        out_specs=[blk3, blk2(SCAN_SEGMENTS)],
        out_shape=[jax.ShapeDtypeStruct(dh.shape, F32), jax.ShapeDtypeStruct((SCAN_SEGMENTS, 2 * SSM_CH), F32)],
        scratch_shapes=[pltpu.VMEM((SCAN_SEGMENTS, 2 * cb), F32)] * 3,
        compiler_params=_params(("parallel",)),
    )(dh, h, cin, lam)


@jax.custom_vjp
def op_scan(bu, lam):
    return _scan_fwd_call(bu, lam)[0]


def _op_scan_fwd(bu, lam):
    h, cin = _scan_fwd_call(bu, lam)
    return h, (h, cin, lam)


def _op_scan_bwd(res, dh):
    h, cin, lam = res
    g, dlam = _scan_bwd_call(dh, h, cin, lam)
    return g, jnp.sum(dlam, axis=0, keepdims=True)


op_scan.defvjp(_op_scan_fwd, _op_scan_bwd)


def _loss_call(y, target):
    t, d = y.shape
    tile = min(ROW_TILE, t)

    def body(y_ref, t_ref, dy_ref, acc_ref):
        @pl.when(pl.program_id(0) == 0)
        def _():
            acc_ref[...] = jnp.zeros_like(acc_ref)

        err = y_ref[...] - t_ref[...]
        dy_ref[...] = err * (1.0 / d)
        col = jnp.sum(err * err, axis=0, keepdims=True)
        part = col[:, 0:LANES]
        for c in range(1, d // LANES):
            part = part + col[:, c * LANES:(c + 1) * LANES]
        acc_ref[0:1, :] += part

    blk = pl.BlockSpec((tile, d), lambda i: (i, 0))
    dy, acc = pl.pallas_call(
        body, name="loss_head", grid=(t // tile,), in_specs=[blk, blk],
        out_specs=[blk, pl.BlockSpec((SUBLANES, LANES), lambda i: (0, 0))],
        out_shape=[jax.ShapeDtypeStruct((t, d), F32), jax.ShapeDtypeStruct((SUBLANES, LANES), F32)],
        compiler_params=_params(("arbitrary",)),
    )(y, target)
    return jnp.sum(acc) * (0.5 / d), dy


def _rot_cols(w, xp=jnp):
    return xp.concatenate([-w[:, 16:], w[:, :16]], axis=1)


def _ext_w_in(w, xp=jnp):
    a_val, a_gate, a_z, c_q, c_kv, k_r, b_z, u, c_z, q, k, v, d_z = xp.split(
        w, (256, 512, 768, 1024, 1152, 1184, 1440, 1696, 1952, 2208, 2336, 2464), axis=1)
    dup = lambda m: xp.concatenate([m[:, :64], m[:, :64], m[:, 64:], m[:, 64:]], axis=1)
    krblk = xp.concatenate([xp.zeros((w.shape[0], 64), w.dtype), k_r, _rot_cols(k_r, xp)], axis=1)
    return xp.concatenate([a_val, a_gate, a_z, c_q, b_z, u, c_z, q, dup(k), dup(v), d_z, c_kv, krblk], axis=1)


IN_WIDTH = 2720
IN_SHARD = IN_WIDTH // 4
IN_SHARD_PAD = 768
IN_EXT = 3072


BAND = 512


def _w_in_layout():
    src = _ext_w_in(np.arange(1, IN_WIDTH + 1, dtype=np.float32)[None, :], np)[0]
    col = np.abs(src).astype(np.int64) - 1
    row = np.where(col >= 0, (col // IN_SHARD) * IN_SHARD_PAD + col % IN_SHARD, -1)
    return row, np.sign(src)


def _w_in_layout_matrix():
    row, sign = _w_in_layout()
    rows = lax.broadcasted_iota(jnp.int32, (4 * IN_SHARD_PAD, IN_EXT), 0)
    return jnp.where(rows == jnp.asarray(row, jnp.int32)[None, :], jnp.asarray(sign, F32)[None, :], 0.0).astype(MXU_DTYPE)


def _band_tables():
    row, _ = _w_in_layout()
    nb = IN_EXT // BAND
    hit = np.zeros((nb, nb), bool)
    for c, r in enumerate(row):
        if r >= 0:
            hit[r // BAND, c // BAND] = True

    def table(h):
        depth = int(h.sum(axis=1).max())
        rows = []
        for o in range(nb):
            used = [int(b) for b in np.nonzero(h[o])[0]]
            spare = [b for b in range(nb) if not h[o, b]]
            rows.append(used + spare[:depth - len(used)])
        return np.asarray(rows, np.int32), depth

    return table(hit.T), table(hit)


def _band_mm_call(a, e, table, depth, e_transposed, name, out_dtype):
    m = a.shape[0]
    nb = IN_EXT // BAND
    dims = NT if e_transposed else NN

    def body(t_ref, a_ref, e_ref, o_ref, acc_ref):
        kk = pl.program_id(1)

        @pl.when(kk == 0)
        def _():
            acc_ref[...] = jnp.zeros_like(acc_ref)

        acc_ref[...] += _dot(a_ref[...], e_ref[...], dims)

        @pl.when(kk == depth - 1)
        def _():
            o_ref[...] = acc_ref[...].astype(out_dtype)

    blk = lambda o, kk, t: t[o * depth + kk]
    e_spec = pl.BlockSpec((BAND, BAND), (lambda o, kk, t: (o, blk(o, kk, t))) if e_transposed else (lambda o, kk, t: (blk(o, kk, t), o)))
    return pl.pallas_call(
        body, name=name, out_shape=jax.ShapeDtypeStruct((m, IN_EXT), out_dtype),
        grid_spec=pltpu.PrefetchScalarGridSpec(
            num_scalar_prefetch=1, grid=(nb, depth),
            in_specs=[pl.BlockSpec((m, BAND), lambda o, kk, t: (0, blk(o, kk, t))), e_spec],
            out_specs=pl.BlockSpec((m, BAND), lambda o, kk, t: (0, o)),
            scratch_shapes=[pltpu.VMEM((m, BAND), F32)]),
        compiler_params=_params(("parallel", "arbitrary")),
    )(jnp.asarray(table.reshape(-1)), a, e)


@jax.custom_vjp
def op_w_in_ext(w_pad, e):
    (table, depth), _ = _band_tables()
    return _band_mm_call(w_pad, e, table, depth, False, "w_in_ext", F32)


def _op_w_in_ext_fwd(w_pad, e):
    return op_w_in_ext(w_pad, e), (e, jnp.zeros((0,), w_pad.dtype))


def _op_w_in_ext_bwd(res, g):
    e, w_like = res
    _, (table, depth) = _band_tables()
    return _band_mm_call(g, e, table, depth, True, "w_in_ext_bwd", w_like.dtype), jnp.zeros_like(e)


op_w_in_ext.defvjp(_op_w_in_ext_fwd, _op_w_in_ext_bwd)


H_COLS = dict(a_val=256, a_gate=256, a_z=256, c_q=256, b_z=256, u=256, c_z=256, q=256, kdup=256, vdup=256, d_z=256,
              c_kv=128, krblk=128)
op_in_proj = make_proj(tuple(H_COLS.values()), "in_proj")


def _ext_mla(w_uq, w_ukv):
    zeros = jnp.zeros((w_ukv.shape[0], 64), w_ukv.dtype)
    uq, uk, uv = [], [], []
    for h in range(4):
        nope, rp = w_uq[:, 96 * h:96 * h + 64], w_uq[:, 96 * h + 64:96 * h + 96]
        uq += [nope, rp, _rot_cols(rp)]
        uk += [w_ukv[:, 128 * h:128 * h + 64], zeros]
        uv.append(w_ukv[:, 128 * h + 64:128 * h + 128])
    return jnp.concatenate(uq, axis=1), jnp.concatenate(uk, axis=1), jnp.concatenate(uv, axis=1)


def _scan_cols(re, im):
    parts = []
    for c in range(SSM_CH // SCAN_CB):
        parts += [re[..., c * SCAN_CB:(c + 1) * SCAN_CB], im[..., c * SCAN_CB:(c + 1) * SCAN_CB]]
    return jnp.concatenate(parts, axis=-1)


def _ext_ssm(a_re, a_im, log_dt, b_re, b_im, c_re, c_im):
    dt = jnp.exp(log_dt)[:, None]
    mag = jnp.exp(a_re * dt)
    lb_re, lb_im = mag * jnp.cos(a_im * dt), mag * jnp.sin(a_im * dt)
    den = a_re * a_re + a_im * a_im
    nr, ni = lb_re - 1.0, lb_im
    f_re = ((nr * a_re + ni * a_im) / den)[..., None]
    f_im = ((ni * a_re - nr * a_im) / den)[..., None]
    bb_re = f_re * b_re - f_im * b_im
    bb_im = f_re * b_im + f_im * b_re
    eye = jnp.eye(SSM_GROUPS, dtype=F32)
    spread = lambda a: a.transpose(0, 2, 1)[:, :, None, :] * eye[:, None, :, None]
    bd_in = lambda bb: spread(bb).reshape(SSM_GROUPS * SSM_GROUP, SSM_CH)
    bd_out = lambda cc: spread(cc).reshape(SSM_CH, SSM_GROUPS * SSM_GROUP)
    w_bu = _scan_cols(bd_in(bb_re), bd_in(bb_im))
    w_y = _scan_cols(bd_out(c_re).T, -bd_out(c_im).T).T
    lam = _scan_cols(lb_re.reshape(1, SSM_CH), lb_im.reshape(1, SSM_CH))
    return w_bu, w_y, lam


def _rope_tables(t):
    pos = jnp.arange(t, dtype=F32)
    inv_freq = ROPE_THETA ** (-jnp.arange(0, 32, 2, dtype=F32) / 32)
    ang = pos[:, None] * inv_freq[None, :]
    cos, sin = jnp.cos(ang), jnp.sin(ang)
    ones, z32, z64 = jnp.ones((t, 64), F32), jnp.zeros((t, 32), F32), jnp.zeros((t, 64), F32)
    cos1 = jnp.concatenate([ones, cos, cos, z32], axis=1)
    sin1 = jnp.concatenate([z64, sin, sin, z32], axis=1)
    return jnp.concatenate([cos1] * 4, axis=1), jnp.concatenate([sin1] * 4, axis=1)


def _to_segments(a):
    t, w = a.shape
    return a.reshape(SCAN_SEGMENTS, t // SCAN_SEGMENTS, w).transpose(1, 0, 2)


def _from_segments(a):
    n, s, w = a.shape
    return a.transpose(1, 0, 2).reshape(n * s, w)


def _layer(x, p_i, cos4, sin4, e_mat, w):
    t = x.shape[0]
    row = lambda v: v.reshape(1, -1)
    f32 = lambda v: v.astype(F32)
    hs = dict(zip(H_COLS, op_in_proj(x, op_w_in_ext(w["w_in_pad"], e_mat))))

    cv = op_conv(hs["a_val"], hs["a_gate"], w["conv_w"], row(w["conv_b"]))
    (y_a,) = op_conv_post((cv, hs["a_z"]), (row(w["conv_norm_g"]), row(w["conv_norm_b"]), f32(w["w_pw2"])))

    w_uq, w_uk, w_uv = _ext_mla(w["w_uq"], f32(w["w_ukv"]))
    q, k, v = op_mla_prep((hs["c_q"], hs["c_kv"], hs["krblk"], cos4, sin4),
                          (row(w["mla_q_norm_g"]), row(w["mla_kv_norm_g"]), w_uq, w_uk, w_uv))
    o_b = op_mla_attn(q, k, v)

    w_bu, w_y, lam = _ext_ssm(w["ssm_a_re"], w["ssm_a_im"], w["ssm_log_dt"], w["ssm_b_re"], w["ssm_b_im"],
                              w["ssm_c_re"], w["ssm_c_im"])
    u_seg = _to_segments(hs["u"]).reshape(t, BRANCH_W)
    bu = op_mm(u_seg, w_bu).reshape(t // SCAN_SEGMENTS, SCAN_SEGMENTS, 2 * SSM_CH)
    hstate = op_scan(bu, lam).reshape(t, 2 * SSM_CH)
    y_ssm = _from_segments(op_mm(hstate, w_y).reshape(t // SCAN_SEGMENTS, SCAN_SEGMENTS, BRANCH_W))
    w_glu = f32(w["w_glu"])
    (y_c,) = op_ssm_post((y_ssm, hs["u"], hs["c_z"]), (row(w["ssm_d"]), w_glu[:, :BRANCH_W], w_glu[:, BRANCH_W:]))

    sink = jnp.repeat(w["attn_sinks"], 64).reshape(1, 2 * LANES)
    o_d = op_swa_attn(hs["q"], hs["kdup"], hs["vdup"], sink)

    merged = op_merge_block(x, (y_a, o_b, y_c, o_d), (None, hs["b_z"], None, hs["d_z"]), w["w_merge"], w["w_branch"], w["b_merge"])
    (x1,) = op_ln((x, op_mm(merged, w["w_out"])), (row(w["ln_g"]), row(w["ln_b"])))
    (out,) = op_ple((x1, op_mm(p_i, w["w_ple"]), op_mm(x1, w["w_ple_gate"])), (row(w["ple_norm_g"]),))
    return out


def _forward(x, p, layers):
    cos4, sin4 = _rope_tables(x.shape[0])
    e_mat = _w_in_layout_matrix()
    for i in range(DEPTH):
        x = _layer(x, p[i], cos4, sin4, e_mat, layers[i])
    return x


SHARD_AXIS = dict(w_in=2, w_merge=2, conv_w=2, w_pw2=1, w_uq=2, w_ukv=2, w_glu=2, w_branch=3, w_out=1, w_ple=2, w_ple_gate=1)
ODD = ("w_uq", "conv_w")
BIG = tuple(n for n in SHARD_AXIS if n not in ODD)
REPLICATED = ("b_merge", "conv_b", "conv_norm_g", "conv_norm_b", "mla_q_norm_g", "mla_kv_norm_g", "ssm_a_re", "ssm_a_im",
              "ssm_log_dt", "ssm_b_re", "ssm_b_im", "ssm_c_re", "ssm_c_im", "ssm_d", "attn_sinks", "ln_g", "ln_b", "ple_norm_g")
WEIGHTS = ("w_in", "w_merge", "b_merge", "conv_w", "conv_b", "conv_norm_g", "conv_norm_b", "w_pw2", "mla_q_norm_g",
           "mla_kv_norm_g", "w_uq", "w_ukv", "ssm_a_re", "ssm_a_im", "ssm_log_dt", "ssm_b_re", "ssm_b_im", "ssm_c_re",
           "ssm_c_im", "ssm_d", "w_glu", "attn_sinks", "w_branch", "w_out", "ln_g", "ln_b", "w_ple", "w_ple_gate", "ple_norm_g")
PACK_COLS = 1024
PACK_ROWS = 16
CHIP_FLIPS = ((1, 0), (0, 1), (1, 1))
N_CHIPS = 4
N_DEV = 8


def _pack_rows(n):
    return -(-n // (SUBLANES * PACK_COLS)) * SUBLANES


def _pack(arrays, dtype):
    blocks, rows = [], 0
    for a in arrays:
        r = _pack_rows(a.size)
        flat = a.reshape(-1).astype(dtype)
        blocks.append(jnp.pad(flat, (0, r * PACK_COLS - a.size)).reshape(r, PACK_COLS))
        rows += r
    pad = -rows % PACK_ROWS
    if pad:
        blocks.append(jnp.zeros((pad, PACK_COLS), dtype))
    return jnp.concatenate(blocks, axis=0)


def _unpack(buf, shapes):
    out, row = [], 0
    for s in shapes:
        n = math.prod(s)
        r = _pack_rows(n)
        out.append(buf[row:row + r].reshape(-1)[:n].reshape(s))
        row += r
    return out


def _flip(v, bit):
    return 1 - v if bit else v


def _window(ref, axis, start, size):
    idx = [slice(None)] * len(ref.shape)
    idx[axis] = pl.ds(start, size)
    return ref.at[tuple(idx)]


def _gather_chips(srcs, axes, stacked):
    units = []
    for k, (s, a) in enumerate(zip(srcs, axes)):
        if stacked[k]:
            units += [(k, l, s.shape[1:], a - 1) for l in range(s.shape[0])]
        else:
            units.append((k, None, s.shape, a))
    nu, nb = len(units), len(srcs)

    def body(*refs):
        ins, outs = refs[:nb], refs[nb:nb + nu]
        ici_send, ici_recv, d2d_send, d2d_recv, local_sems = refs[nb + nu:]
        x, y, c = lax.axis_index("x"), lax.axis_index("y"), lax.axis_index("c")
        me = 2 * x + y

        def mine(u, half=None):
            k, l, shape, _ = units[u]
            ref = ins[k] if l is None else ins[k].at[l]
            return ref if half is None else ref.at[pl.ds(half * (shape[0] // 2), shape[0] // 2)]

        def place(u, chip, half=None):
            _, _, shape, a = units[u]
            size, rows = shape[a], shape[0] // 2
            if half is None:
                return _window(outs[u], a, chip * size, size)
            if a == 0:
                return outs[u].at[pl.ds(chip * size + half * rows, rows)]
            return _window(outs[u].at[pl.ds(half * rows, rows)], a, chip * size, size)

        local = [pltpu.make_async_copy(mine(u), place(u, me), local_sems.at[u]) for u in range(nu)]
        for cp in local:
            cp.start()
        sends = []
        for j, (bx, by) in enumerate(CHIP_FLIPS):
            for u in range(nu):
                cp = pltpu.make_async_remote_copy(src_ref=mine(u, c), dst_ref=place(u, me, c),
                                                  send_sem=ici_send.at[j * nu + u], recv_sem=ici_recv.at[j * nu + u],
                                                  device_id=(_flip(x, bx), _flip(y, by), c), device_id_type=MESH)
                cp.start()
                sends.append(cp)
        for j, (bx, by) in enumerate(CHIP_FLIPS):
            src = 2 * _flip(x, bx) + _flip(y, by)
            for u in range(nu):
                got = place(u, src, c)
                pltpu.make_async_remote_copy(src_ref=got, dst_ref=got, send_sem=ici_send.at[j * nu + u],
                                             recv_sem=ici_recv.at[j * nu + u], device_id=(x, y, c), device_id_type=MESH).wait_recv()
                cp = pltpu.make_async_remote_copy(src_ref=got, dst_ref=got, send_sem=d2d_send.at[j * nu + u],
                                                  recv_sem=d2d_recv.at[j * nu + u], device_id=(x, y, 1 - c), device_id_type=MESH)
                cp.start()
                sends.append(cp)
        for j, (bx, by) in enumerate(CHIP_FLIPS):
            src = 2 * _flip(x, bx) + _flip(y, by)
            for u in range(nu):
                other = place(u, src, 1 - c)
                pltpu.make_async_remote_copy(src_ref=other, dst_ref=other, send_sem=d2d_send.at[j * nu + u],
                                             recv_sem=d2d_recv.at[j * nu + u], device_id=(x, y, c), device_id_type=MESH).wait_recv()
        for cp in sends:
            cp.wait_send()
        for cp in local:
            cp.wait()

    full = lambda shape, a: tuple(N_CHIPS * d if i == a else d for i, d in enumerate(shape))
    res = pl.pallas_call(
        body, name="gather_weights", in_specs=[ANY] * nb, out_specs=[ANY] * nu,
        out_shape=[jax.ShapeDtypeStruct(full(shape, a), srcs[k].dtype) for k, _, shape, a in units],
        scratch_shapes=[pltpu.SemaphoreType.DMA((3 * nu,))] * 4 + [pltpu.SemaphoreType.DMA((nu,))],
    )(*srcs)
    out, it = [], iter(res)
    for k in range(nb):
        out.append([next(it) for _ in range(srcs[k].shape[0])] if stacked[k] else next(it))
    return out


def _exchange_grads(grads, axes, smalls):
    nt, ns = len(grads), len(smalls)
    sizes = [g[0].shape[a] // N_CHIPS for g, a in zip(grads, axes)]
    dev_flips = [(bx, by, bc) for bx in (0, 1) for by in (0, 1) for bc in (0, 1)][1:]
    n_remote = 3 * nt * DEPTH + 7 * ns
    n_local = nt * DEPTH + ns

    def body(*refs):
        g_refs = [refs[k * DEPTH:(k + 1) * DEPTH] for k in range(nt)]
        s_refs = refs[nt * DEPTH:nt * DEPTH + ns]
        outs = refs[nt * DEPTH + ns:nt * DEPTH + ns + nt + ns]
        recv_refs, all_refs = outs[:nt], outs[nt:]
        send_sems, recv_sems, local_sems = refs[-3:]
        x, y, c = lax.axis_index("x"), lax.axis_index("y"), lax.axis_index("c")
        me_chip = 2 * x + y
        me = 4 * x + 2 * y + c
        part = lambda k, i, chip: _window(g_refs[k][i], axes[k], chip * sizes[k], sizes[k])
        started, waits = [], []
        sem, lsem = 0, 0
        for k in range(nt):
            for i in range(DEPTH):
                cp = pltpu.make_async_copy(part(k, i, me_chip), recv_refs[k].at[i, 3], local_sems.at[lsem])
                cp.start()
                started.append(cp.wait)
                lsem += 1
                for j, (bx, by) in enumerate(CHIP_FLIPS):
                    px, py = _flip(x, bx), _flip(y, by)
                    cp = pltpu.make_async_remote_copy(src_ref=part(k, i, 2 * px + py), dst_ref=recv_refs[k].at[i, j],
                                                      send_sem=send_sems.at[sem], recv_sem=recv_sems.at[sem],
                                                      device_id=(px, py, c), device_id_type=MESH)
                    cp.start()
                    started.append(cp.wait_send)
                    waits.append(cp.wait_recv)
                    sem += 1
        for s in range(ns):
            cp = pltpu.make_async_copy(s_refs[s], all_refs[s].at[me], local_sems.at[lsem])
            cp.start()
            started.append(cp.wait)
            lsem += 1
            for bx, by, bc in dev_flips:
                peer = (_flip(x, bx), _flip(y, by), _flip(c, bc))
                cp = pltpu.make_async_remote_copy(src_ref=s_refs[s], dst_ref=all_refs[s].at[me], send_sem=send_sems.at[sem],
                                                  recv_sem=recv_sems.at[sem], device_id=peer, device_id_type=MESH)
                cp.start()
                started.append(cp.wait_send)
                src = 4 * peer[0] + 2 * peer[1] + peer[2]
                waits.append(pltpu.make_async_remote_copy(src_ref=s_refs[s], dst_ref=all_refs[s].at[src], send_sem=send_sems.at[sem],
                                                          recv_sem=recv_sems.at[sem], device_id=peer, device_id_type=MESH).wait_recv)
                sem += 1
        for w in waits + started:
            w()

    shard = lambda g, a: tuple(d // N_CHIPS if i == a else d for i, d in enumerate(g.shape))
    flat = [g for per_layer in grads for g in per_layer]
    return pl.pallas_call(
        body, name="exchange_grads", in_specs=[ANY] * (len(flat) + ns), out_specs=[ANY] * (nt + ns),
        out_shape=[jax.ShapeDtypeStruct((DEPTH, N_CHIPS, *shard(g[0], a)), g[0].dtype) for g, a in zip(grads, axes)]
        + [jax.ShapeDtypeStruct((N_DEV, *s.shape), s.dtype) for s in smalls],
        scratch_shapes=[pltpu.SemaphoreType.DMA((n_remote,)), pltpu.SemaphoreType.DMA((n_remote,)), pltpu.SemaphoreType.DMA((n_local,))],
    )(*flat, *smalls)


def _swap_cores(parts):
    nb = len(parts)

    def body(*refs):
        ins, outs, send_sems, recv_sems = refs[:nb], refs[nb:2 * nb], refs[-2], refs[-1]
        x, y, c = lax.axis_index("x"), lax.axis_index("y"), lax.axis_index("c")
        cps = [pltpu.make_async_remote_copy(src_ref=ins[k], dst_ref=outs[k], send_sem=send_sems.at[k], recv_sem=recv_sems.at[k],
                                            device_id=(x, y, 1 - c), device_id_type=MESH) for k in range(nb)]
        for cp in cps:
            cp.start()
        for cp in cps:
            cp.wait()

    return pl.pallas_call(
        body, name="swap_cores", in_specs=[ANY] * nb, out_specs=[ANY] * nb,
        out_shape=[jax.ShapeDtypeStruct(q.shape, q.dtype) for q in parts],
        scratch_shapes=[pltpu.SemaphoreType.DMA((nb,)), pltpu.SemaphoreType.DMA((nb,))],
    )(*parts)


def _sum_chips_call(recv, cols, name):
    depth, _, r, c = recv.shape
    tile = _pick(r, (512, 256, 128, 64, 32, 16))

    def body(r_ref, o_ref):
        slot = lambda s: r_ref[s, :, pl.ds(0, cols)].astype(F32)
        o_ref[...] = ((slot(3) + slot(0)) + slot(1)) + slot(2)

    return pl.pallas_call(
        body, name=name, grid=(depth, r // tile),
        in_specs=[pl.BlockSpec((None, N_CHIPS, tile, c), lambda l, i: (l, 0, i, 0))],
        out_specs=pl.BlockSpec((None, tile, cols), lambda l, i: (l, i, 0)), out_shape=jax.ShapeDtypeStruct((depth, r, cols), F32),
        compiler_params=_params(("parallel", "parallel")),
    )(recv)


def _sum_slots_call(slots, name):
    n, r, c = slots.shape
    tile = _pick(r, (512, 256, 128, 64, 32, 16, 8))

    def body(s_ref, o_ref):
        acc = s_ref[0]
        for s in range(1, n):
            acc = acc + s_ref[s]
        o_ref[...] = acc

    return pl.pallas_call(
        body, name=name, grid=(r // tile,), in_specs=[pl.BlockSpec((n, tile, c), lambda i: (0, i, 0))],
        out_specs=pl.BlockSpec((tile, c), lambda i: (i, 0)), out_shape=jax.ShapeDtypeStruct((r, c), F32),
        compiler_params=_params(("parallel",)),
    )(slots)


def _adamw_math(w, g, m, v):
    m = ADAM_B1 * m + (1.0 - ADAM_B1) * g
    v = ADAM_B2 * v + (1.0 - ADAM_B2) * (g * g)
    m_hat = m / (1.0 - ADAM_B1 ** ADAM_STEP)
    v_hat = v / (1.0 - ADAM_B2 ** ADAM_STEP)
    return -ADAM_LR * (m_hat / (jnp.sqrt(v_hat) + ADAM_EPS) + ADAM_WD * w), m, v


def _adamw_call(w, m, v, gparts, name):
    r, c = w.shape
    n = len(gparts)
    tile = _pick(r, (512, 256, 128, 64, 32, 16, 8))

    def body(w_ref, m_ref, v_ref, *refs):
        g_refs, (go_ref, d_ref, mo_ref, vo_ref) = refs[:n], refs[n:]
        g = g_refs[0][...]
        for g_ref in g_refs[1:]:
            g = g + g_ref[...]
        go_ref[...] = g
        d_ref[...], mo_ref[...], vo_ref[...] = _adamw_math(w_ref[...], g, m_ref[...], v_ref[...])

    blk = pl.BlockSpec((tile, c), lambda i: (i, 0))
    return pl.pallas_call(
        body, name=name, grid=(r // tile,), in_specs=[blk] * (3 + n),
        out_specs=[blk] * 4, out_shape=[jax.ShapeDtypeStruct((r, c), F32)] * 4,
        compiler_params=_params(("parallel",)),
    )(w, m, v, *gparts)


def _train_local(x, p, layers, target):
    y, vjp = jax.vjp(lambda x_, w_: _forward(x_, p, w_), x, layers)
    loss, dy = _loss_call(y, target)
    dx, dw = vjp(dy)
    return loss, dx, dw


def kernel(x, p, w_in, w_merge, b_merge, conv_w, conv_b, conv_norm_g, conv_norm_b, w_pw2, mla_q_norm_g, mla_kv_norm_g, w_uq, w_ukv, ssm_a_re, ssm_a_im, ssm_log_dt, ssm_b_re, ssm_b_im, ssm_c_re, ssm_c_im, ssm_d, w_glu, attn_sinks, w_branch, w_out, ln_g, ln_b, w_ple, w_ple_gate, ple_norm_g, loss_target, m_w_in, m_w_merge, m_b_merge, m_conv_w, m_conv_b, m_conv_norm_g, m_conv_norm_b, m_w_pw2, m_mla_q_norm_g, m_mla_kv_norm_g, m_w_uq, m_w_ukv, m_ssm_a_re, m_ssm_a_im, m_ssm_log_dt, m_ssm_b_re, m_ssm_b_im, m_ssm_c_re, m_ssm_c_im, m_ssm_d, m_w_glu, m_attn_sinks, m_w_branch, m_w_out, m_ln_g, m_ln_b, m_w_ple, m_w_ple_gate, m_ple_norm_g, v_w_in, v_w_merge, v_b_merge, v_conv_w, v_conv_b, v_conv_norm_g, v_conv_norm_b, v_w_pw2, v_mla_q_norm_g, v_mla_kv_norm_g, v_w_uq, v_w_ukv, v_ssm_a_re, v_ssm_a_im, v_ssm_log_dt, v_ssm_b_re, v_ssm_b_im, v_ssm_c_re, v_ssm_c_im, v_ssm_d, v_w_glu, v_attn_sinks, v_w_branch, v_w_out, v_ln_g, v_ln_b, v_w_ple, v_w_ple_gate, v_ple_norm_g):
    given = dict(locals())
    w_loc = {n: given[n] for n in WEIGHTS}
    m_loc = {n: given["m_" + n] for n in WEIGHTS}
    v_loc = {n: given["v_" + n] for n in WEIGHTS}

    me_chip = 2 * lax.axis_index("x") + lax.axis_index("y")

    wire = {n: w_loc[n].astype(MXU_DTYPE) for n in BIG}
    wire["w_in"] = jnp.pad(wire["w_in"], ((0, 0), (0, 0), (0, IN_SHARD_PAD - IN_SHARD)))
    odd_shapes = [w_loc[n].shape for n in ODD]
    gathered = _gather_chips([wire[n] for n in BIG] + [_pack([w_loc[n] for n in ODD], F32)], [SHARD_AXIS[n] for n in BIG] + [0],
                             [True] * len(BIG) + [False])
    full = dict(zip(BIG, gathered[:-1]))
    odd_parts = [_unpack(part, odd_shapes) for part in jnp.split(gathered[-1], N_CHIPS, axis=0)]
    for k, n in enumerate(ODD):
        full[n] = jnp.concatenate([odd_parts[s][k] for s in range(N_CHIPS)], axis=SHARD_AXIS[n])
    layers = []
    for i in range(DEPTH):
        layer = {n: (full[n][i] if n in full else w_loc[n][i]) for n in WEIGHTS if n != "w_in"}
        layer["w_in_pad"] = full["w_in"][i]
        layers.append(layer)

    loss, dx, dw = _train_local(x[0], p[:, 0], layers, loss_target[0])
    loss = lax.psum(loss, ("x", "y", "c"))

    key = lambda n: "w_in_pad" if n == "w_in" else n
    stacked = lambda n: jnp.stack([dw[i][n] for i in range(DEPTH)])
    small_rep = _pack([stacked(n) for n in REPLICATED], F32)
    small_odd = _pack([stacked(n) for n in ODD], F32)
    *recv, all_rep, all_odd = _exchange_grads([[dw[i][key(n)] for i in range(DEPTH)] for n in BIG],
                                              [SHARD_AXIS[n] - 1 for n in BIG], [small_rep, small_odd])
    parts = []
    for n, r in zip(BIG, recv):
        cols = w_loc[n].shape[-1]
        parts.append(_sum_chips_call(r.reshape(DEPTH, N_CHIPS, -1, r.shape[-1]), cols, "sum_chips_" + n))
    others = _swap_cores(parts)
    g_rep = _sum_slots_call(all_rep, "sum_replicated")
    g_odd = _unpack(_sum_slots_call(all_odd, "sum_odd"), [(DEPTH, *w_loc[n].shape[1:-1], N_CHIPS * w_loc[n].shape[-1]) for n in ODD])

    grads, deltas, new_m, new_v = {}, {}, {}, {}

    def adamw(n, gparts):
        shape = w_loc[n].shape
        two_d = lambda a: a.reshape(-1, shape[-1])
        res = _adamw_call(two_d(w_loc[n]), two_d(m_loc[n]), two_d(v_loc[n]), [two_d(g) for g in gparts], "adamw_" + n)
        grads[n], deltas[n], new_m[n], new_v[n] = [r.reshape(shape) for r in res]

    for n, part, other in zip(BIG, parts, others):
        adamw(n, [part, other])
    for n, g in zip(ODD, g_odd):
        size = w_loc[n].shape[-1]
        adamw(n, [lax.dynamic_slice_in_dim(g, me_chip * size, size, axis=g.ndim - 1)])
    rep_shapes = [w_loc[n].shape for n in REPLICATED]
    res = _adamw_call(_pack([w_loc[n] for n in REPLICATED], F32), _pack([m_loc[n] for n in REPLICATED], F32),
                      _pack([v_loc[n] for n in REPLICATED], F32), [g_rep], "adamw_replicated")
    for dst, buf in zip((grads, deltas, new_m, new_v), res):
        for n, a in zip(REPLICATED, _unpack(buf, rep_shapes)):
            dst[n] = a

    return (loss, dx[None], *[grads[n] for n in WEIGHTS], *[deltas[n] for n in WEIGHTS],
            *[new_m[n] for n in WEIGHTS], *[new_v[n] for n in WEIGHTS])
```

```python
import functools
import math

import jax
import jax.numpy as jnp
import numpy as np
from jax import lax
from jax.experimental import pallas as pl
from jax.experimental.pallas import tpu as pltpu

F32 = jnp.float32
BF16 = jnp.bfloat16
MXU_DTYPE = BF16
V7X_VMEM_BYTES = 64 * 1024 * 1024
VMEM_LIMIT = V7X_VMEM_BYTES * 3 // 4
LANES = 128
SUBLANES = 8

D_MODEL = 1024
DEPTH = 4
BRANCH_W = 256
CONV_W = 31
CONV_HALO = 32
CONV_CHUNK = 64
MLA_SCALE = (64 + 32) ** -0.5
SWA_SCALE = 64 ** -0.5
WINDOW = 128
ROPE_THETA = 10000.0
SSM_GROUPS, SSM_GROUP, SSM_STATE = 16, 16, 64
SSM_CH = SSM_GROUPS * SSM_STATE
SCAN_SEGMENTS = SUBLANES
SCAN_CB = 256
DEEPNORM_ALPHA = (2.0 * DEPTH) ** 0.25
LN_EPS = 1e-5
RMS_EPS = 1e-6
ADAM_LR, ADAM_B1, ADAM_B2, ADAM_EPS, ADAM_WD, ADAM_STEP = 0.001, 0.9, 0.999, 1e-08, 0.01, 10
NEG = -1e30
ROW_TILE = 512

NN = (((1,), (0,)), ((), ()))
NT = (((1,), (1,)), ((), ()))
TN = (((0,), (0,)), ((), ()))

MESH = pl.DeviceIdType.MESH
ANY = pl.BlockSpec(memory_space=pl.ANY)


def _dot(a, b, dims):
    return lax.dot_general(a.astype(MXU_DTYPE), b.astype(MXU_DTYPE), dims, preferred_element_type=F32)


def _pick(n, cands):
    for c in cands:
        if n % c == 0:
            return c
    return n


def _params(sem, vmem_limit=VMEM_LIMIT):
    return pltpu.CompilerParams(dimension_semantics=sem, vmem_limit_bytes=vmem_limit)


def _col_offsets(widths):
    return [sum(widths[:j]) for j in range(len(widths))]


def _silu_gate(x, z):
    return x * (z * jax.nn.sigmoid(z))


def _proj_fwd_call(x, wb, widths, name, z=None):
    t, k = x.shape
    tm = min(ROW_TILE, t)
    offs = _col_offsets(widths)
    ins = [x] if z is None else [x, z]

    def body(*refs):
        w_ref, o_refs = refs[len(ins)], refs[len(ins) + 1:]
        xv = refs[0][...] if z is None else _silu_gate(refs[0][...], refs[1][...])
        xb = xv.astype(MXU_DTYPE)
        for o_ref, off, wd in zip(o_refs, offs, widths):
            o_ref[...] = _dot(xb, w_ref[:, off:off + wd], NN)

    return pl.pallas_call(
        body, name=name, grid=(t // tm,),
        in_specs=[pl.BlockSpec((tm, k), lambda i: (i, 0))] * len(ins) + [pl.BlockSpec(wb.shape, lambda i: (0, 0))],
        out_specs=[pl.BlockSpec((tm, wd), lambda i: (i, 0)) for wd in widths],
        out_shape=[jax.ShapeDtypeStruct((t, wd), F32) for wd in widths],
        compiler_params=_params(("parallel",)),
    )(*ins, wb)


def _proj_dx_call(douts, wb, widths, name, gate=None):
    t = douts[0].shape[0]
    k = wb.shape[0]
    tm = min(ROW_TILE, t)
    offs = _col_offsets(widths)
    nd = len(douts)
    extra = [] if gate is None else list(gate)

    def body(*refs):
        d_refs, w_ref = refs[:nd], refs[nd]
        acc = jnp.zeros((tm, k), F32)
        for d_ref, off, wd in zip(d_refs, offs, widths):
            acc = acc + _dot(d_ref[...], w_ref[:, off:off + wd], NT)
        if gate is None:
            refs[-1][...] = acc
        else:
            xv, zv = refs[nd + 1][...], refs[nd + 2][...]
            sg = jax.nn.sigmoid(zv)
            refs[-2][...] = acc * (zv * sg)
            refs[-1][...] = acc * xv * (sg * (1.0 + zv * (1.0 - sg)))

    row = pl.BlockSpec((tm, k), lambda i: (i, 0))
    n_out = 1 if gate is None else 2
    res = pl.pallas_call(
        body, name=name, grid=(t // tm,),
        in_specs=[pl.BlockSpec((tm, wd), lambda i: (i, 0)) for wd in widths] + [pl.BlockSpec(wb.shape, lambda i: (0, 0))] + [row] * len(extra),
        out_specs=[row] * n_out, out_shape=[jax.ShapeDtypeStruct((t, k), F32)] * n_out,
        compiler_params=_params(("parallel",)),
    )(*douts, wb, *extra)
    return res[0] if gate is None else tuple(res)


def _proj_dw_call(x, douts, widths, name, out_dtype, z=None):
    t, k = x.shape
    n = sum(widths)
    tk = min(ROW_TILE if k * n <= 2 * 1024 * 1024 else ROW_TILE // 2, t)
    nk = t // tk
    offs = _col_offsets(widths)
    ins = [x] if z is None else [x, z]

    def body(*all_refs):
        refs = all_refs[len(ins):]
        d_refs, o_ref, acc_ref = refs[:-2], refs[-2], refs[-1]

        @pl.when(pl.program_id(0) == 0)
        def _():
            acc_ref[...] = jnp.zeros_like(acc_ref)

        xv = all_refs[0][...] if z is None else _silu_gate(all_refs[0][...], all_refs[1][...])
        xb = xv.astype(MXU_DTYPE)
        for d_ref, off, wd in zip(d_refs, offs, widths):
            acc_ref[:, off:off + wd] += _dot(xb, d_ref[...], TN)

        @pl.when(pl.program_id(0) == nk - 1)
        def _():
            o_ref[...] = acc_ref[...].astype(out_dtype)

    return pl.pallas_call(
        body, name=name, grid=(nk,),
        in_specs=[pl.BlockSpec((tk, k), lambda i: (i, 0))] * len(ins) + [pl.BlockSpec((tk, wd), lambda i: (i, 0)) for wd in widths],
        out_specs=pl.BlockSpec((k, n), lambda i: (0, 0)), out_shape=jax.ShapeDtypeStruct((k, n), out_dtype),
        scratch_shapes=[pltpu.VMEM((k, n), F32)],
        compiler_params=_params(("arbitrary",)),
    )(*ins, *douts)


def make_proj(widths, name):
    @jax.custom_vjp
    def op(x, w):
        return tuple(_proj_fwd_call(x, w.astype(MXU_DTYPE), widths, name + "_fwd"))

    def fwd(x, w):
        wb = w.astype(MXU_DTYPE)
        return tuple(_proj_fwd_call(x, wb, widths, name + "_fwd")), (x, wb, jnp.zeros((0,), w.dtype))

    def bwd(res, douts):
        x, wb, w_like = res
        return _proj_dx_call(douts, wb, widths, name + "_dx"), _proj_dw_call(x, douts, widths, name + "_dw", w_like.dtype)

    op.defvjp(fwd, bwd)
    return op


_MM_OPS = {}


def op_mm(a, w):
    n = w.shape[1]
    if n not in _MM_OPS:
        _MM_OPS[n] = make_proj((n,), "mm%d" % n)
    return _MM_OPS[n](a, w)[0]


@jax.custom_vjp
def _mm(a, w):
    return _dot(a, w, NN)


def _mm_f(a, w):
    return _dot(a, w, NN), (a, w)


def _mm_b(res, g):
    a, w = res
    return _dot(g, w, NT), _dot(a, g, TN)


_mm.defvjp(_mm_f, _mm_b)


@functools.partial(jax.custom_vjp, nondiff_argnums=(1,))
def _roll(x, shift):
    return pltpu.roll(x, shift, 1)


def _roll_f(x, shift):
    return pltpu.roll(x, shift, 1), None


def _roll_b(shift, _, g):
    return (pltpu.roll(g, (g.shape[1] - shift) % g.shape[1], 1),)


_roll.defvjp(_roll_f, _roll_b)


def _ln(x, g, b):
    mu = jnp.mean(x, axis=-1, keepdims=True)
    xc = x - mu
    var = jnp.mean(xc * xc, axis=-1, keepdims=True)
    return xc * lax.rsqrt(var + LN_EPS) * g + b


def _rms(x, g):
    ms = jnp.mean(x * x, axis=-1, keepdims=True)
    return x * lax.rsqrt(ms + RMS_EPS) * g


def _sigmoid(x):
    return jax.nn.sigmoid(x)


def _silu(x):
    return x * _sigmoid(x)


def _gelu_tanh(x):
    return x * (0.5 * (1.0 + jnp.tanh(math.sqrt(2.0 / math.pi) * (x + 0.044715 * (x * x * x)))))


def _rowwise_fwd_call(fn, rows, consts, name, tile):
    t = rows[0].shape[0]
    tile = min(tile, t)
    nr = len(rows)
    outs = jax.eval_shape(fn, *[jax.ShapeDtypeStruct((tile, r.shape[1]), F32) for r in rows],
                          *[jax.ShapeDtypeStruct(c.shape, F32) for c in consts])

    def body(*refs):
        vals = [r[...] for r in refs[:nr + len(consts)]]
        res = fn(*vals)
        for o_ref, o in zip(refs[nr + len(consts):], res):
            o_ref[...] = o

    return pl.pallas_call(
        body, name=name, grid=(t // tile,),
        in_specs=[pl.BlockSpec((tile, r.shape[1]), lambda i: (i, 0)) for r in rows]
        + [pl.BlockSpec(c.shape, lambda i: (0, 0)) for c in consts],
        out_specs=[pl.BlockSpec((tile, o.shape[1]), lambda i: (i, 0)) for o in outs],
        out_shape=[jax.ShapeDtypeStruct((t, o.shape[1]), F32) for o in outs],
        compiler_params=_params(("parallel",)),
    )(*rows, *consts)


def _rowwise_bwd_call(fn, rows, consts, douts, row_diff, name, tile, row_grad_dtype=F32):
    t = rows[0].shape[0]
    tile = min(tile, t)
    nr, nc, nd = len(rows), len(consts), len(douts)
    diff_idx = [i for i in range(nr) if row_diff[i]]

    def body(*refs):
        rv = [r[...] for r in refs[:nr]]
        cv = [r[...] for r in refs[nr:nr + nc]]
        dv = [r[...] for r in refs[nr + nc:nr + nc + nd]]
        out_refs = refs[nr + nc + nd:]

        def f(*diff):
            full = list(rv)
            for k, i in enumerate(diff_idx):
                full[i] = diff[k]
            return fn(*full, *diff[len(diff_idx):])

        _, vjp = jax.vjp(f, *[rv[i] for i in diff_idx], *cv)
        grads = vjp(tuple(dv))
        for k in range(len(diff_idx)):
            out_refs[k][...] = grads[k].astype(row_grad_dtype)
        first = pl.program_id(0) == 0
        for k in range(nc):
            acc_ref = out_refs[len(diff_idx) + k]
            g = grads[len(diff_idx) + k]

            @pl.when(first)
            def _(acc_ref=acc_ref, g=g):
                acc_ref[...] = g

            @pl.when(jnp.logical_not(first))
            def _(acc_ref=acc_ref, g=g):
                acc_ref[...] += g

    res = pl.pallas_call(
        body, name=name, grid=(t // tile,),
        in_specs=[pl.BlockSpec((tile, r.shape[1]), lambda i: (i, 0)) for r in rows]
        + [pl.BlockSpec(c.shape, lambda i: (0, 0)) for c in consts]
        + [pl.BlockSpec((tile, d.shape[1]), lambda i: (i, 0)) for d in douts],
        out_specs=[pl.BlockSpec((tile, rows[i].shape[1]), lambda i_: (i_, 0)) for i in diff_idx]
        + [pl.BlockSpec(c.shape, lambda i: (0, 0)) for c in consts],
        out_shape=[jax.ShapeDtypeStruct(rows[i].shape, row_grad_dtype) for i in diff_idx]
        + [jax.ShapeDtypeStruct(c.shape, F32) for c in consts],
        compiler_params=_params(("arbitrary",)),
    )(*rows, *consts, *douts)
    return res[:len(diff_idx)], res[len(diff_idx):]


def make_rowwise(fn, name, row_diff, tile=ROW_TILE):
    @jax.custom_vjp
    def op(rows, consts):
        return tuple(_rowwise_fwd_call(fn, rows, consts, name + "_fwd", tile))

    def fwd(rows, consts):
        return op(rows, consts), (rows, consts)

    def bwd(res, douts):
        rows, consts = res
        drows, dconsts = _rowwise_bwd_call(fn, rows, consts, douts, row_diff, name + "_bwd", tile)
        it = iter(drows)
        full = tuple(next(it) if row_diff[i] else jnp.zeros_like(rows[i]) for i in range(len(rows)))
        return full, tuple(dconsts)

    op.defvjp(fwd, bwd)
    return op


def _conv_post_fn(cv, a_z, ng, nb, w_pw2):
    return (_mm(_silu(_ln(cv, ng, nb)), w_pw2) * _silu(a_z),)


def _mla_prep_fn(c_q, c_kv, krblk, cos4, sin4, qg, kvg, w_uq, w_uk, w_uv):
    qe = _mm(_rms(c_q, qg), w_uq)
    q = qe * cos4 + _roll(qe, qe.shape[1] - 32) * sin4
    cos1, sin1 = cos4[:, :LANES], sin4[:, :LANES]
    kr = krblk * cos1 + _roll(krblk, LANES - 32) * sin1
    kn = _rms(c_kv, kvg)
    k = _mm(kn, w_uk) + jnp.concatenate([kr, kr, kr, kr], axis=1)
    return q, k, _mm(kn, w_uv)


def _ssm_post_fn(y, u, c_z, d, w_a, w_b):
    y2 = _gelu_tanh(y + d * u)
    return (_mm(y2, w_a) * _sigmoid(_mm(y2, w_b)) * _silu(c_z),)


def _merge_fn(br0, br1, br2, br3, gl0, gl1, gl2, gl3, b0, b1, b2, b3):
    return (_sigmoid(gl0 + b0) * br0 + _sigmoid(gl1 + b1) * br1 + _sigmoid(gl2 + b2) * br2 + _sigmoid(gl3 + b3) * br3,)


def _ln_fn(x, mo, g, b):
    return (_ln(DEEPNORM_ALPHA * x + mo, g, b),)


def _ple_fn(x1, pe, gl, g):
    return (x1 + _rms(pe * _sigmoid(gl), g),)


op_conv_post = make_rowwise(_conv_post_fn, "conv_post", (True, True))
op_mla_prep = make_rowwise(_mla_prep_fn, "mla_prep", (True, True, True, False, False))
op_ssm_post = make_rowwise(_ssm_post_fn, "ssm_post", (True, True, True))
MERGE_TILE = ROW_TILE // 2
MERGE_WIDTHS = (D_MODEL,) * 4


@jax.custom_vjp
def op_merge_block(x, ys, zs, w_merge, w_branch, b_merge):
    return _merge_block_fwd(x, ys, zs, w_merge, w_branch, b_merge)[0]


def _merge_block_fwd(x, ys, zs, w_merge, w_branch, b_merge):
    wm, wb = w_merge.astype(MXU_DTYPE), w_branch.astype(MXU_DTYPE)
    gl = _proj_fwd_call(x, wm, MERGE_WIDTHS, "merge_proj_fwd")
    br = [_proj_fwd_call(ys[n], wb[n], (D_MODEL,), "branch_proj_fwd", zs[n])[0] for n in range(4)]
    bm = tuple(b_merge[n * D_MODEL:(n + 1) * D_MODEL].reshape(1, -1) for n in range(4))
    (merged,) = _rowwise_fwd_call(_merge_fn, (*br, *gl), bm, "merge_fwd", MERGE_TILE)
    return merged, (x, ys, zs, wm, wb, tuple(br), tuple(gl), bm, jnp.zeros((0,), w_merge.dtype), jnp.zeros((0,), w_branch.dtype))


def _merge_block_bwd(res, dmerged):
    x, ys, zs, wm, wb, br, gl, bm, wm_like, wb_like = res
    drows, dbm = _rowwise_bwd_call(_merge_fn, (*br, *gl), bm, (dmerged,), (True,) * 8, "merge_bwd", MERGE_TILE, MXU_DTYPE)
    dbr, dgl = drows[:4], drows[4:]
    dx = _proj_dx_call(dgl, wm, MERGE_WIDTHS, "merge_proj_dx")
    dwm = _proj_dw_call(x, dgl, MERGE_WIDTHS, "merge_proj_dw", wm_like.dtype)
    dys, dzs = [], []
    for n in range(4):
        if zs[n] is None:
            dys.append(_proj_dx_call([dbr[n]], wb[n], (D_MODEL,), "branch_proj_dx"))
            dzs.append(None)
        else:
            dy, dz = _proj_dx_call([dbr[n]], wb[n], (D_MODEL,), "branch_proj_dx", (ys[n], zs[n]))
            dys.append(dy)
            dzs.append(dz)
    dwb = jnp.stack([_proj_dw_call(ys[n], [dbr[n]], (D_MODEL,), "branch_proj_dw", wb_like.dtype, zs[n]) for n in range(4)])
    return dx, tuple(dys), tuple(dzs), dwm, dwb, jnp.concatenate([d.reshape(-1) for d in dbm])


op_merge_block.defvjp(_merge_block_fwd, _merge_block_bwd)
op_ln = make_rowwise(_ln_fn, "post_ln", (True, True))
op_ple = make_rowwise(_ple_fn, "ple", (True, True, True))


def _shift_copies(buf, shifted, tile):
    for r in range(1, SUBLANES):
        shifted[r - 1, :, :] = buf[pl.ds(r, tile + CONV_HALO - SUBLANES), :]


def _tap(buf, shifted, off, tile):
    r = off % SUBLANES
    return buf[pl.ds(off, tile), :] if r == 0 else shifted[r - 1, pl.ds(off - r, tile), :]


def _shift_scratch(tile, w):
    return pltpu.VMEM((SUBLANES - 1, tile + CONV_HALO - SUBLANES, w), F32)


def _conv_fwd_call(a_val, a_gate, w32, b):
    t, w = a_val.shape
    tile = min(ROW_TILE, t)
    per = tile // CONV_HALO
    cur = pl.BlockSpec((tile, w), lambda i: (i, 0))
    prev = pl.BlockSpec((CONV_HALO, w), lambda i: (jnp.maximum(i * per - 1, 0), 0))

    def body(av_ref, avh_ref, ag_ref, agh_ref, w_ref, b_ref, cv_ref, buf, shifted):
        i = pl.program_id(0)
        gh = avh_ref[...] * _sigmoid(agh_ref[...])
        buf[0:CONV_HALO, :] = jnp.where(i > 0, gh, 0.0)
        buf[CONV_HALO:, :] = av_ref[...] * _sigmoid(ag_ref[...])
        _shift_copies(buf, shifted, tile)
        for c0 in range(0, tile, CONV_CHUNK):
            acc = jnp.zeros((CONV_CHUNK, w), F32) + b_ref[...]
            for j in range(CONV_W):
                acc = acc + w_ref[j:j + 1, :] * _tap(buf, shifted, c0 + CONV_HALO - (CONV_W - 1) + j, CONV_CHUNK)
            cv_ref[c0:c0 + CONV_CHUNK, :] = acc

    return pl.pallas_call(
        body, name="conv_fwd", grid=(t // tile,),
        in_specs=[cur, prev, cur, prev, pl.BlockSpec((CONV_HALO, w), lambda i: (0, 0)), pl.BlockSpec((1, w), lambda i: (0, 0))],
        out_specs=cur, out_shape=jax.ShapeDtypeStruct((t, w), F32),
        scratch_shapes=[pltpu.VMEM((tile + CONV_HALO, w), F32), _shift_scratch(tile, w)],
        compiler_params=_params(("parallel",)),
    )(a_val, a_val, a_gate, a_gate, w32, b)


def _conv_bwd_call(a_val, a_gate, w32, dcv):
    t, w = a_val.shape
    tile = min(ROW_TILE, t)
    n = t // tile
    per = tile // CONV_HALO
    cur = pl.BlockSpec((tile, w), lambda i: (i, 0))
    prev = pl.BlockSpec((CONV_HALO, w), lambda i: (jnp.maximum(i * per - 1, 0), 0))
    nxt = pl.BlockSpec((CONV_HALO, w), lambda i: (jnp.minimum((i + 1) * per, t // CONV_HALO - 1), 0))
    full = lambda r: pl.BlockSpec((r, w), lambda i: (0, 0))

    def body(av_ref, avh_ref, ag_ref, agh_ref, w_ref, d_ref, dn_ref, dav_ref, dag_ref, dw_ref, db_ref, gbuf, dbuf, gsh, dsh):
        i = pl.program_id(0)
        gh = avh_ref[...] * _sigmoid(agh_ref[...])
        gbuf[0:CONV_HALO, :] = jnp.where(i > 0, gh, 0.0)
        gbuf[CONV_HALO:, :] = av_ref[...] * _sigmoid(ag_ref[...])
        dbuf[0:tile, :] = d_ref[...]
        dbuf[tile:, :] = jnp.where(i < n - 1, dn_ref[...], 0.0)

        @pl.when(i == 0)
        def _():
            dw_ref[...] = jnp.zeros_like(dw_ref)
            db_ref[...] = jnp.zeros_like(db_ref)

        _shift_copies(gbuf, gsh, tile)
        _shift_copies(dbuf, dsh, tile)
        for c0 in range(0, tile, CONV_CHUNK):
            rows = slice(c0, c0 + CONV_CHUNK)
            d = d_ref[rows, :]
            dg = jnp.zeros((CONV_CHUNK, w), F32)
            for j in range(CONV_W):
                dg = dg + w_ref[j:j + 1, :] * _tap(dbuf, dsh, c0 + CONV_W - 1 - j, CONV_CHUNK)
                dw_ref[j:j + 1, :] += jnp.sum(d * _tap(gbuf, gsh, c0 + CONV_HALO - (CONV_W - 1) + j, CONV_CHUNK), axis=0, keepdims=True)
            db_ref[...] += jnp.sum(d, axis=0, keepdims=True)
            av = av_ref[rows, :]
            sg = _sigmoid(ag_ref[rows, :])
            dav_ref[rows, :] = dg * sg
            dag_ref[rows, :] = dg * av * sg * (1.0 - sg)

    return pl.pallas_call(
        body, name="conv_bwd", grid=(n,),
        in_specs=[cur, prev, cur, prev, full(CONV_HALO), cur, nxt],
        out_specs=[cur, cur, full(CONV_HALO), full(1)],
        out_shape=[jax.ShapeDtypeStruct((t, w), F32), jax.ShapeDtypeStruct((t, w), F32),
                   jax.ShapeDtypeStruct((CONV_HALO, w), F32), jax.ShapeDtypeStruct((1, w), F32)],
        scratch_shapes=[pltpu.VMEM((tile + CONV_HALO, w), F32), pltpu.VMEM((tile + CONV_HALO, w), F32),
                        _shift_scratch(tile, w), _shift_scratch(tile, w)],
        compiler_params=_params(("arbitrary",)),
    )(a_val, a_val, a_gate, a_gate, w32, dcv, dcv)


def _pad_taps(conv_w):
    return jnp.concatenate([conv_w, jnp.zeros((CONV_HALO - CONV_W, conv_w.shape[1]), F32)], axis=0)


@jax.custom_vjp
def op_conv(a_val, a_gate, conv_w, conv_b):
    return _conv_fwd_call(a_val, a_gate, _pad_taps(conv_w), conv_b)


def _op_conv_fwd(a_val, a_gate, conv_w, conv_b):
    return op_conv(a_val, a_gate, conv_w, conv_b), (a_val, a_gate, conv_w)


def _op_conv_bwd(res, dcv):
    a_val, a_gate, conv_w = res
    dav, dag, dw, db = _conv_bwd_call(a_val, a_gate, _pad_taps(conv_w), dcv)
    return dav, dag, dw[:CONV_W], db


op_conv.defvjp(_op_conv_fwd, _op_conv_bwd)


def _head_masks(rows):
    lane = lax.broadcasted_iota(jnp.int32, (rows, LANES), 1)
    return lane < 64, lane >= 64


def _head_row(vals, mask):
    return jnp.max(jnp.where(mask, vals, NEG), axis=1, keepdims=True)


def _attn_valid(qpos, kpos, window):
    valid = kpos <= qpos
    if window is not None:
        valid = jnp.logical_and(valid, qpos - kpos < window)
    return valid


def _flash_fwd_call(q, k, v, sink, *, window, shared_k, scale, blk, blk_q, name):
    t = q.shape[0]
    qw = LANES if shared_k else 2 * LANES
    pairs = v.shape[1] // LANES
    tk = min(blk, t)
    tq = min(blk_q, t)
    has_sink = sink is not None
    one_step = window is not None and tk == 2 * tq and window <= tq
    kstride = tq if one_step else tk

    def body(*refs):
        if has_sink:
            q_ref, k_ref, v_ref, s_ref, o_ref, lse_ref, k_mxu, v0_mxu, v1_mxu = refs
        else:
            q_ref, k_ref, v_ref, o_ref, lse_ref, k_mxu, v0_mxu, v1_mxu = refs
        v_mxu = (v0_mxu, v1_mxu)
        i = pl.program_id(1)

        @pl.when(i == 0)
        def _():
            full_masks = _head_masks(t)
            k_mxu[...] = k_ref[...].astype(MXU_DTYPE)
            for h in range(2):
                v_mxu[h][...] = jnp.where(full_masks[h], v_ref[...], 0.0).astype(MXU_DTYPE)

        qb = q_ref[...]
        masks = _head_masks(tq)
        row_masks = _head_masks(1)
        qh = [(jnp.where(masks[h], qb, 0.0) if shared_k else qb[:, h * LANES:(h + 1) * LANES]).astype(MXU_DTYPE) for h in range(2)]
        qpos = i * tq + lax.broadcasted_iota(jnp.int32, (tq, tk), 0)
        if has_sink:
            m_init = [jnp.zeros((tq, 1), F32) + _head_row(s_ref[...], row_masks[h]) for h in range(2)]
            l_init = [jnp.ones((tq, 1), F32)] * 2
        else:
            m_init = [jnp.full((tq, 1), NEG, F32)] * 2
            l_init = [jnp.zeros((tq, 1), F32)] * 2

        def make_step(masked):
            def step(j, carry):
                m0, l0, m1, l1, acc = carry
                start = pl.multiple_of(j * kstride, kstride)
                kb = k_mxu[pl.ds(start, tk), :]
                if masked:
                    valid = _attn_valid(qpos, start + lax.broadcasted_iota(jnp.int32, (tq, tk), 1), window)
                new, alphas, pv = [], [], []
                for h, (m, l) in enumerate(((m0, l0), (m1, l1))):
                    kh = kb if shared_k else kb[:, h * LANES:(h + 1) * LANES]
                    s = _dot(qh[h], kh, NT) * scale
                    if masked:
                        s = jnp.where(valid, s, NEG)
                    m_new = jnp.maximum(m, jnp.max(s, axis=1, keepdims=True))
                    alpha = jnp.exp(m - m_new)
                    p = jnp.exp(s - m_new)
                    new += [m_new, alpha * l + jnp.sum(p, axis=1, keepdims=True)]
                    alphas.append(alpha)
                    pv.append(_dot(p, v_mxu[h][pl.ds(start, tk), :], NN))
                acc = acc * jnp.where(masks[0], alphas[0], alphas[1]) + pv[0] + pv[1]
                return new[0], new[1], new[2], new[3], acc
            return step

        carry = (m_init[0], l_init[0], m_init[1], l_init[1], jnp.zeros((tq, LANES), F32))
        last = (i * tq + tq - 1) // tk
        if window is None:
            n_full = (i * tq + 1) // tk
            carry = lax.fori_loop(0, n_full, make_step(False), carry)
            carry = lax.fori_loop(n_full, last + 1, make_step(True), carry)
        elif one_step:
            carry = make_step(True)(jnp.maximum(i - 1, 0), carry)
        else:
            carry = lax.fori_loop(jnp.maximum(i * tq - (window - 1), 0) // tk, last + 1, make_step(True), carry)
        m0, l0, m1, l1, acc = carry
        o_ref[...] = acc / jnp.where(masks[0], l0, l1)
        lse_ref[...] = jnp.where(masks[0], m0 + jnp.log(l0), m1 + jnp.log(l1))

    in_specs = [pl.BlockSpec((tq, qw), lambda p, i: (i, p)), pl.BlockSpec((t, qw), lambda p, i: (0, p)),
                pl.BlockSpec((t, LANES), lambda p, i: (0, p))]
    args = [q, k, v]
    if has_sink:
        in_specs.append(pl.BlockSpec((1, LANES), lambda p, i: (0, p)))
        args.append(sink)
    blk_o = pl.BlockSpec((tq, LANES), lambda p, i: (i, p))
    return pl.pallas_call(
        body, name=name, grid=(pairs, t // tq), in_specs=in_specs, out_specs=[blk_o, blk_o],
        out_shape=[jax.ShapeDtypeStruct((t, pairs * LANES), F32)] * 2,
        scratch_shapes=[pltpu.VMEM((t, qw), MXU_DTYPE), pltpu.VMEM((t, LANES), MXU_DTYPE), pltpu.VMEM((t, LANES), MXU_DTYPE)],
        compiler_params=_params(("arbitrary", "arbitrary")),
    )(*args)


def _flash_bwd_call(q, k, v, sink, o, lse, do, *, window, shared_k, scale, blk, blk_q, name):
    t = q.shape[0]
    qw = LANES if shared_k else 2 * LANES
    pairs = v.shape[1] // LANES
    tk = min(blk, t)
    tq = min(blk_q, t)
    assert tk % tq == 0 or tq % tk == 0
    nq = t // tq
    one_step = window is not None and tq == 2 * tk and window <= tk
    qstride = tk if one_step else tq
    has_sink = sink is not None

    def body(*refs):
        if has_sink:
            q_ref, k_ref, v_ref, o_ref, lse_ref, do_ref, s_ref, dq_ref, dk_ref, dv_ref, ds_ref = refs[:11]
        else:
            q_ref, k_ref, v_ref, o_ref, lse_ref, do_ref, dq_ref, dk_ref, dv_ref = refs[:9]
        q_mxu, do_mxu, lse_h, dsum_h = refs[-8:-6], refs[-6:-4], refs[-4:-2], refs[-2:]
        j = pl.program_id(1)
        masks = _head_masks(tq)
        row_masks = _head_masks(1)

        @pl.when(j == 0)
        def _():
            dq_ref[...] = jnp.zeros_like(dq_ref)
            full_masks = _head_masks(t)
            prod = do_ref[...] * o_ref[...]
            parts = []
            for h in range(2):
                qh = jnp.where(full_masks[h], q_ref[...], 0.0) if shared_k else q_ref[:, h * LANES:(h + 1) * LANES]
                q_mxu[h][...] = qh.astype(MXU_DTYPE)
                do_mxu[h][...] = jnp.where(full_masks[h], do_ref[...], 0.0).astype(MXU_DTYPE)
                dsum = jnp.sum(jnp.where(full_masks[h], prod, 0.0), axis=1, keepdims=True)
                lse = _head_row(lse_ref[...], full_masks[h])
                dsum_h[h][...] = jnp.zeros((t, LANES), F32) + dsum
                lse_h[h][...] = jnp.zeros((t, LANES), F32) + lse
                if has_sink:
                    ps = jnp.exp(_head_row(s_ref[...], row_masks[h]) - lse)
                    parts.append(-jnp.sum(ps * dsum, axis=0, keepdims=True))
            if has_sink:
                ds_ref[...] = jnp.zeros((SUBLANES, LANES), F32) + jnp.where(row_masks[0], parts[0], parts[1])

        kb = k_ref[...].astype(MXU_DTYPE)
        vb = v_ref[...].astype(MXU_DTYPE)
        kh = [kb if shared_k else kb[:, h * LANES:(h + 1) * LANES] for h in range(2)]
        kpos = j * tk + lax.broadcasted_iota(jnp.int32, (tq, tk), 1)
        lanes_of = lambda a: a if tk == LANES else jnp.concatenate([a] * (tk // LANES), axis=1)

        def make_step(masked):
            def step(i, carry):
                dk0, dk1, dv = carry
                start = pl.multiple_of(i * qstride, qstride)
                if masked:
                    valid = _attn_valid(start + lax.broadcasted_iota(jnp.int32, (tq, tk), 0), kpos, window)
                dks, dqs = [], []
                for h in range(2):
                    qh = q_mxu[h][pl.ds(start, tq), :]
                    doh = do_mxu[h][pl.ds(start, tq), :]
                    s = _dot(qh, kh[h], NT) * scale
                    if masked:
                        s = jnp.where(valid, s, NEG)
                    p = jnp.exp(s - lanes_of(lse_h[h][pl.ds(start, tq), :]))
                    dp = _dot(doh, vb, NT)
                    dsc = p * (dp - lanes_of(dsum_h[h][pl.ds(start, tq), :])) * scale
                    dv = dv + _dot(p, doh, TN)
                    dks.append(_dot(dsc, qh, TN))
                    dq_h = _dot(dsc, kh[h], NN)
                    dqs.append(jnp.where(masks[h], dq_h, 0.0) if shared_k else dq_h)
                if shared_k:
                    dq_ref[pl.ds(start, tq), :] += dqs[0] + dqs[1]
                else:
                    dq_ref[pl.ds(start, tq), :] += jnp.concatenate(dqs, axis=1)
                return dk0 + dks[0], dk1 + dks[1], dv
            return step

        zero = jnp.zeros((tk, LANES), F32)
        carry = (zero, zero, zero)
        first = (j * tk) // tq
        if window is None:
            n_full = jnp.minimum(((j + 1) * tk + tq - 2) // tq, nq)
            carry = lax.fori_loop(first, n_full, make_step(True), carry)
            carry = lax.fori_loop(n_full, nq, make_step(False), carry)
        elif one_step:
            carry = make_step(True)(jnp.minimum(j, t // tk - 2), carry)
        else:
            carry = lax.fori_loop(first, jnp.minimum(nq, (j * tk + tk - 1 + window - 1) // tq + 1), make_step(True), carry)
        dk0, dk1, dv = carry
        dk_ref[...] = dk0 + dk1 if shared_k else jnp.concatenate([dk0, dk1], axis=1)
        dv_ref[...] = dv

    full = lambda w: pl.BlockSpec((t, w), lambda p, j: (0, p))
    blkspec = lambda w: pl.BlockSpec((tk, w), lambda p, j: (j, p))
    in_specs = [full(qw), blkspec(qw), blkspec(LANES), full(LANES), full(LANES), full(LANES)]
    args = [q, k, v, o, lse, do]
    out_specs = [full(qw), blkspec(qw), blkspec(LANES)]
    out_shape = [jax.ShapeDtypeStruct(q.shape, F32), jax.ShapeDtypeStruct(k.shape, F32), jax.ShapeDtypeStruct(v.shape, F32)]
    if has_sink:
        in_specs.append(pl.BlockSpec((1, LANES), lambda p, j: (0, p)))
        args.append(sink)
        out_specs.append(pl.BlockSpec((SUBLANES, LANES), lambda p, j: (0, p)))
        out_shape.append(jax.ShapeDtypeStruct((SUBLANES, pairs * LANES), F32))
    return pl.pallas_call(
        body, name=name, grid=(pairs, t // tk), in_specs=in_specs, out_specs=out_specs, out_shape=out_shape,
        scratch_shapes=[pltpu.VMEM((t, LANES), MXU_DTYPE)] * 4 + [pltpu.VMEM((t, LANES), F32)] * 4,
        compiler_params=_params(("arbitrary", "arbitrary")),
    )(*args)


_MLA_CFG = dict(window=None, shared_k=False, scale=MLA_SCALE, blk=256)
_SWA_CFG = dict(window=WINDOW, shared_k=True, scale=SWA_SCALE, blk=128)
_MLA_FWD_CFG = dict(_MLA_CFG, blk=512, blk_q=256)
_SWA_FWD_CFG = dict(_SWA_CFG, blk=512, blk_q=256)
_MLA_BWD_CFG = dict(_MLA_CFG, blk=512, blk_q=512)
_SWA_BWD_CFG = dict(_SWA_CFG, blk=256, blk_q=512)


@jax.custom_vjp
def op_mla_attn(q, k, v):
    return _flash_fwd_call(q, k, v, None, name="mla_fwd", **_MLA_FWD_CFG)[0]


def _op_mla_attn_fwd(q, k, v):
    o, lse = _flash_fwd_call(q, k, v, None, name="mla_fwd", **_MLA_FWD_CFG)
    return o, (q, k, v, o, lse)


def _op_mla_attn_bwd(res, do):
    q, k, v, o, lse = res
    return tuple(_flash_bwd_call(q, k, v, None, o, lse, do, name="mla_bwd", **_MLA_BWD_CFG))


op_mla_attn.defvjp(_op_mla_attn_fwd, _op_mla_attn_bwd)


@jax.custom_vjp
def op_swa_attn(q, k, v, sink):
    return _flash_fwd_call(q, k, v, sink, name="swa_fwd", **_SWA_FWD_CFG)[0]


def _op_swa_attn_fwd(q, k, v, sink):
    o, lse = _flash_fwd_call(q, k, v, sink, name="swa_fwd", **_SWA_FWD_CFG)
    return o, (q, k, v, sink, o, lse)


def _op_swa_attn_bwd(res, do):
    q, k, v, sink, o, lse = res
    dq, dk, dv, dsink = _flash_bwd_call(q, k, v, sink, o, lse, do, name="swa_bwd", **_SWA_BWD_CFG)
    first_lane = lax.broadcasted_iota(jnp.int32, (1, dsink.shape[1]), 1) % 64 == 0
    return dq, dk, dv, jnp.where(first_lane, dsink[:1], 0.0)


op_swa_attn.defvjp(_op_swa_attn_fwd, _op_swa_attn_bwd)


def _complex_power(ar, ai, n):
    for _ in range(int(math.log2(n))):
        ar, ai = ar * ar - ai * ai, 2.0 * ar * ai
    return ar, ai


def _scan_passes(load_b, a1r, a1i, n, store, e_ref, c_ref, reverse):
    cb = a1r.shape[1]
    ar = jnp.zeros((SCAN_SEGMENTS, cb), F32) + a1r
    ai = jnp.zeros((SCAN_SEGMENTS, cb), F32) + a1i
    a2r, a2i = ar * ar - ai * ai, 2.0 * ar * ai
    idx = (lambda k: n - 1 - k) if reverse else (lambda k: k)
    mac = lambda pr_, pi__, h, b: (pr_ * h[0] - pi__ * h[1] + b[0], pr_ * h[1] + pi__ * h[0] + b[1])

    def load_pair(ii):
        k = jnp.minimum(2 * ii, n - 2)
        b0, b1 = load_b(idx(k)), load_b(idx(k + 1))
        return b0, mac(ar, ai, b0, b1)

    def local(ii, carry):
        h, c = carry
        return mac(a2r, a2i, h, c), load_pair(ii + 1)[1]

    zero = jnp.zeros((SCAN_SEGMENTS, cb), F32)
    (er, ei), _ = lax.fori_loop(0, n // 2, local, ((zero, zero), load_pair(0)[1]))
    e_ref[:, 0:cb] = er
    e_ref[:, cb:] = ei
    pr, pi_ = _complex_power(a1r, a1i, n)
    cr = jnp.zeros((1, cb), F32)
    ci = jnp.zeros((1, cb), F32)
    order = range(SCAN_SEGMENTS - 1, -1, -1) if reverse else range(SCAN_SEGMENTS)
    for s in order:
        c_ref[s:s + 1, 0:cb] = cr
        c_ref[s:s + 1, cb:] = ci
        er1, ei1 = e_ref[s:s + 1, 0:cb], e_ref[s:s + 1, cb:]
        cr, ci = pr * cr - pi_ * ci + er1, pr * ci + pi_ * cr + ei1

    def second(ii, carry):
        h, b0, c = carry
        h0 = mac(ar, ai, h, b0)
        h1 = mac(a2r, a2i, h, c)
        store(idx(2 * ii), *h0)
        store(idx(2 * ii + 1), *h1)
        nb0, nc = load_pair(ii + 1)
        return h1, nb0, nc

    b0, c0 = load_pair(0)
    lax.fori_loop(0, n // 2, second, ((c_ref[:, 0:cb], c_ref[:, cb:]), b0, c0))


def _scan_fwd_call(bu, lam):
    n = bu.shape[0]
    cb = SCAN_CB
    blk3 = pl.BlockSpec((n, SCAN_SEGMENTS, 2 * cb), lambda c: (0, 0, c))
    blk2 = lambda r: pl.BlockSpec((r, 2 * cb), lambda c: (0, c))

    def body(b_ref, lam_ref, h_ref, cin_ref, e_ref):
        def store(i, hr, hi):
            h_ref[i, :, 0:cb] = hr
            h_ref[i, :, cb:] = hi

        _scan_passes(lambda i: (b_ref[i, :, 0:cb], b_ref[i, :, cb:]), lam_ref[:, 0:cb], lam_ref[:, cb:], n, store,
                     e_ref, cin_ref, False)

    return pl.pallas_call(
        body, name="scan_fwd", grid=(SSM_CH // cb,), in_specs=[blk3, blk2(1)], out_specs=[blk3, blk2(SCAN_SEGMENTS)],
        out_shape=[jax.ShapeDtypeStruct(bu.shape, F32), jax.ShapeDtypeStruct((SCAN_SEGMENTS, 2 * SSM_CH), F32)],
        scratch_shapes=[pltpu.VMEM((SCAN_SEGMENTS, 2 * cb), F32)],
        compiler_params=_params(("parallel",)),
    )(bu, lam)


def _scan_bwd_call(dh, h, cin, lam):
    n = dh.shape[0]
    cb = SCAN_CB
    blk3 = pl.BlockSpec((n, SCAN_SEGMENTS, 2 * cb), lambda c: (0, 0, c))
    blk2 = lambda r: pl.BlockSpec((r, 2 * cb), lambda c: (0, c))

    def body(d_ref, h_ref, cin_ref, lam_ref, g_ref, dlam_ref, e_ref, c_ref, acc_ref):
        acc_ref[...] = jnp.zeros_like(acc_ref)

        def store(i, gr, gi):
            g_ref[i, :, 0:cb] = gr
            g_ref[i, :, cb:] = gi
            ip = jnp.maximum(i - 1, 0)
            hpr = jnp.where(i > 0, h_ref[ip, :, 0:cb], cin_ref[:, 0:cb])
            hpi = jnp.where(i > 0, h_ref[ip, :, cb:], cin_ref[:, cb:])
            acc_ref[:, 0:cb] += gr * hpr + gi * hpi
            acc_ref[:, cb:] += gi * hpr - gr * hpi

        _scan_passes(lambda i: (d_ref[i, :, 0:cb], d_ref[i, :, cb:]), lam_ref[:, 0:cb], -lam_ref[:, cb:], n, store,
                     e_ref, c_ref, True)
        dlam_ref[...] = acc_ref[...]

    return pl.pallas_call(
        body, name="scan_bwd", grid=(SSM_CH // cb,), in_specs=[blk3, blk3, blk2(SCAN_SEGMENTS), blk2(1)],
        out_specs=[blk3, blk2(SCAN_SEGMENTS)],
        out_shape=[jax.ShapeDtypeStruct(dh.shape, F32), jax.ShapeDtypeStruct((SCAN_SEGMENTS, 2 * SSM_CH), F32)],
        scratch_shapes=[pltpu.VMEM((SCAN_SEGMENTS, 2 * cb), F32)] * 3,
        compiler_params=_params(("parallel",), 6 * n * SCAN_SEGMENTS * 2 * cb * 4 + 4 * 1024 * 1024),
    )(dh, h, cin, lam)


@jax.custom_vjp
def op_scan(bu, lam):
    return _scan_fwd_call(bu, lam)[0]


def _op_scan_fwd(bu, lam):
    h, cin = _scan_fwd_call(bu, lam)
    return h, (h, cin, lam)


def _op_scan_bwd(res, dh):
    h, cin, lam = res
    g, dlam = _scan_bwd_call(dh, h, cin, lam)
    return g, jnp.sum(dlam, axis=0, keepdims=True)


op_scan.defvjp(_op_scan_fwd, _op_scan_bwd)


def _loss_call(y, target):
    t, d = y.shape
    tile = min(ROW_TILE, t)

    def body(y_ref, t_ref, dy_ref, acc_ref):
        @pl.when(pl.program_id(0) == 0)
        def _():
            acc_ref[...] = jnp.zeros_like(acc_ref)

        err = y_ref[...] - t_ref[...]
        dy_ref[...] = err * (1.0 / d)
        col = jnp.sum(err * err, axis=0, keepdims=True)
        part = col[:, 0:LANES]
        for c in range(1, d // LANES):
            part = part + col[:, c * LANES:(c + 1) * LANES]
        acc_ref[0:1, :] += part

    blk = pl.BlockSpec((tile, d), lambda i: (i, 0))
    dy, acc = pl.pallas_call(
        body, name="loss_head", grid=(t // tile,), in_specs=[blk, blk],
        out_specs=[blk, pl.BlockSpec((SUBLANES, LANES), lambda i: (0, 0))],
        out_shape=[jax.ShapeDtypeStruct((t, d), F32), jax.ShapeDtypeStruct((SUBLANES, LANES), F32)],
        compiler_params=_params(("arbitrary",)),
    )(y, target)
    return jnp.sum(acc) * (0.5 / d), dy


def _rot_cols(w, xp=jnp):
    return xp.concatenate([-w[:, 16:], w[:, :16]], axis=1)


def _ext_w_in(w, xp=jnp):
    a_val, a_gate, a_z, c_q, c_kv, k_r, b_z, u, c_z, q, k, v, d_z = xp.split(
        w, (256, 512, 768, 1024, 1152, 1184, 1440, 1696, 1952, 2208, 2336, 2464), axis=1)
    dup = lambda m: xp.concatenate([m[:, :64], m[:, :64], m[:, 64:], m[:, 64:]], axis=1)
    krblk = xp.concatenate([xp.zeros((w.shape[0], 64), w.dtype), k_r, _rot_cols(k_r, xp)], axis=1)
    return xp.concatenate([a_val, a_gate, a_z, c_q, b_z, u, c_z, q, dup(k), dup(v), d_z, c_kv, krblk], axis=1)


IN_WIDTH = 2720
IN_SHARD = IN_WIDTH // 4
IN_SHARD_PAD = 768
IN_EXT = 3072


BAND = 512


def _w_in_layout():
    src = _ext_w_in(np.arange(1, IN_WIDTH + 1, dtype=np.float32)[None, :], np)[0]
    col = np.abs(src).astype(np.int64) - 1
    row = np.where(col >= 0, (col // IN_SHARD) * IN_SHARD_PAD + col % IN_SHARD, -1)
    return row, np.sign(src)


def _w_in_layout_matrix():
    row, sign = _w_in_layout()
    rows = lax.broadcasted_iota(jnp.int32, (4 * IN_SHARD_PAD, IN_EXT), 0)
    return jnp.where(rows == jnp.asarray(row, jnp.int32)[None, :], jnp.asarray(sign, F32)[None, :], 0.0).astype(MXU_DTYPE)


def _band_tables():
    row, _ = _w_in_layout()
    nb = IN_EXT // BAND
    hit = np.zeros((nb, nb), bool)
    for c, r in enumerate(row):
        if r >= 0:
            hit[r // BAND, c // BAND] = True

    def table(h):
        depth = int(h.sum(axis=1).max())
        rows = []
        for o in range(nb):
            used = [int(b) for b in np.nonzero(h[o])[0]]
            spare = [b for b in range(nb) if not h[o, b]]
            rows.append(used + spare[:depth - len(used)])
        return np.asarray(rows, np.int32), depth

    return table(hit.T), table(hit)


def _band_mm_call(a, e, table, depth, e_transposed, name, out_dtype):
    m = a.shape[0]
    nb = IN_EXT // BAND
    dims = NT if e_transposed else NN

    def body(t_ref, a_ref, e_ref, o_ref, acc_ref):
        kk = pl.program_id(1)

        @pl.when(kk == 0)
        def _():
            acc_ref[...] = jnp.zeros_like(acc_ref)

        acc_ref[...] += _dot(a_ref[...], e_ref[...], dims)

        @pl.when(kk == depth - 1)
        def _():
            o_ref[...] = acc_ref[...].astype(out_dtype)

    blk = lambda o, kk, t: t[o * depth + kk]
    e_spec = pl.BlockSpec((BAND, BAND), (lambda o, kk, t: (o, blk(o, kk, t))) if e_transposed else (lambda o, kk, t: (blk(o, kk, t), o)))
    return pl.pallas_call(
        body, name=name, out_shape=jax.ShapeDtypeStruct((m, IN_EXT), out_dtype),
        grid_spec=pltpu.PrefetchScalarGridSpec(
            num_scalar_prefetch=1, grid=(nb, depth),
            in_specs=[pl.BlockSpec((m, BAND), lambda o, kk, t: (0, blk(o, kk, t))), e_spec],
            out_specs=pl.BlockSpec((m, BAND), lambda o, kk, t: (0, o)),
            scratch_shapes=[pltpu.VMEM((m, BAND), F32)]),
        compiler_params=_params(("parallel", "arbitrary")),
    )(jnp.asarray(table.reshape(-1)), a, e)


@jax.custom_vjp
def op_w_in_ext(w_pad, e):
    (table, depth), _ = _band_tables()
    return _band_mm_call(w_pad, e, table, depth, False, "w_in_ext", F32)


def _op_w_in_ext_fwd(w_pad, e):
    return op_w_in_ext(w_pad, e), (e, jnp.zeros((0,), w_pad.dtype))


def _op_w_in_ext_bwd(res, g):
    e, w_like = res
    _, (table, depth) = _band_tables()
    return _band_mm_call(g, e, table, depth, True, "w_in_ext_bwd", w_like.dtype), jnp.zeros_like(e)


op_w_in_ext.defvjp(_op_w_in_ext_fwd, _op_w_in_ext_bwd)


H_COLS = dict(a_val=256, a_gate=256, a_z=256, c_q=256, b_z=256, u=256, c_z=256, q=256, kdup=256, vdup=256, d_z=256,
              c_kv=128, krblk=128)
op_in_proj = make_proj(tuple(H_COLS.values()), "in_proj")


def _ext_mla(w_uq, w_ukv):
    zeros = jnp.zeros((w_ukv.shape[0], 64), w_ukv.dtype)
    uq, uk, uv = [], [], []
    for h in range(4):
        nope, rp = w_uq[:, 96 * h:96 * h + 64], w_uq[:, 96 * h + 64:96 * h + 96]
        uq += [nope, rp, _rot_cols(rp)]
        uk += [w_ukv[:, 128 * h:128 * h + 64], zeros]
        uv.append(w_ukv[:, 128 * h + 64:128 * h + 128])
    return jnp.concatenate(uq, axis=1), jnp.concatenate(uk, axis=1), jnp.concatenate(uv, axis=1)


def _scan_cols(re, im):
    parts = []
    for c in range(SSM_CH // SCAN_CB):
        parts += [re[..., c * SCAN_CB:(c + 1) * SCAN_CB], im[..., c * SCAN_CB:(c + 1) * SCAN_CB]]
    return jnp.concatenate(parts, axis=-1)


def _ext_ssm(a_re, a_im, log_dt, b_re, b_im, c_re, c_im):
    dt = jnp.exp(log_dt)[:, None]
    mag = jnp.exp(a_re * dt)
    lb_re, lb_im = mag * jnp.cos(a_im * dt), mag * jnp.sin(a_im * dt)
    den = a_re * a_re + a_im * a_im
    nr, ni = lb_re - 1.0, lb_im
    f_re = ((nr * a_re + ni * a_im) / den)[..., None]
    f_im = ((ni * a_re - nr * a_im) / den)[..., None]
    bb_re = f_re * b_re - f_im * b_im
    bb_im = f_re * b_im + f_im * b_re
    eye = jnp.eye(SSM_GROUPS, dtype=F32)
    spread = lambda a: a.transpose(0, 2, 1)[:, :, None, :] * eye[:, None, :, None]
    bd_in = lambda bb: spread(bb).reshape(SSM_GROUPS * SSM_GROUP, SSM_CH)
    bd_out = lambda cc: spread(cc).reshape(SSM_CH, SSM_GROUPS * SSM_GROUP)
    w_bu = _scan_cols(bd_in(bb_re), bd_in(bb_im))
    w_y = _scan_cols(bd_out(c_re).T, -bd_out(c_im).T).T
    lam = _scan_cols(lb_re.reshape(1, SSM_CH), lb_im.reshape(1, SSM_CH))
    return w_bu, w_y, lam


def _rope_tables(t):
    pos = jnp.arange(t, dtype=F32)
    inv_freq = ROPE_THETA ** (-jnp.arange(0, 32, 2, dtype=F32) / 32)
    ang = pos[:, None] * inv_freq[None, :]
    cos, sin = jnp.cos(ang), jnp.sin(ang)
    ones, z32, z64 = jnp.ones((t, 64), F32), jnp.zeros((t, 32), F32), jnp.zeros((t, 64), F32)
    cos1 = jnp.concatenate([ones, cos, cos, z32], axis=1)
    sin1 = jnp.concatenate([z64, sin, sin, z32], axis=1)
    return jnp.concatenate([cos1] * 4, axis=1), jnp.concatenate([sin1] * 4, axis=1)


def _to_segments(a):
    t, w = a.shape
    return a.reshape(SCAN_SEGMENTS, t // SCAN_SEGMENTS, w).transpose(1, 0, 2)


def _from_segments(a):
    n, s, w = a.shape
    return a.transpose(1, 0, 2).reshape(n * s, w)


def _layer(x, p_i, cos4, sin4, e_mat, w):
    t = x.shape[0]
    row = lambda v: v.reshape(1, -1)
    f32 = lambda v: v.astype(F32)
    hs = dict(zip(H_COLS, op_in_proj(x, op_w_in_ext(w["w_in_pad"], e_mat))))

    cv = op_conv(hs["a_val"], hs["a_gate"], w["conv_w"], row(w["conv_b"]))
    (y_a,) = op_conv_post((cv, hs["a_z"]), (row(w["conv_norm_g"]), row(w["conv_norm_b"]), f32(w["w_pw2"])))

    w_uq, w_uk, w_uv = _ext_mla(w["w_uq"], f32(w["w_ukv"]))
    q, k, v = op_mla_prep((hs["c_q"], hs["c_kv"], hs["krblk"], cos4, sin4),
                          (row(w["mla_q_norm_g"]), row(w["mla_kv_norm_g"]), w_uq, w_uk, w_uv))
    o_b = op_mla_attn(q, k, v)

    w_bu, w_y, lam = _ext_ssm(w["ssm_a_re"], w["ssm_a_im"], w["ssm_log_dt"], w["ssm_b_re"], w["ssm_b_im"],
                              w["ssm_c_re"], w["ssm_c_im"])
    u_seg = _to_segments(hs["u"]).reshape(t, BRANCH_W)
    bu = op_mm(u_seg, w_bu).reshape(t // SCAN_SEGMENTS, SCAN_SEGMENTS, 2 * SSM_CH)
    hstate = op_scan(bu, lam).reshape(t, 2 * SSM_CH)
    y_ssm = _from_segments(op_mm(hstate, w_y).reshape(t // SCAN_SEGMENTS, SCAN_SEGMENTS, BRANCH_W))
    w_glu = f32(w["w_glu"])
    (y_c,) = op_ssm_post((y_ssm, hs["u"], hs["c_z"]), (row(w["ssm_d"]), w_glu[:, :BRANCH_W], w_glu[:, BRANCH_W:]))

    sink = jnp.repeat(w["attn_sinks"], 64).reshape(1, 2 * LANES)
    o_d = op_swa_attn(hs["q"], hs["kdup"], hs["vdup"], sink)

    merged = op_merge_block(x, (y_a, o_b, y_c, o_d), (None, hs["b_z"], None, hs["d_z"]), w["w_merge"], w["w_branch"], w["b_merge"])
    (x1,) = op_ln((x, op_mm(merged, w["w_out"])), (row(w["ln_g"]), row(w["ln_b"])))
    (out,) = op_ple((x1, op_mm(p_i, w["w_ple"]), op_mm(x1, w["w_ple_gate"])), (row(w["ple_norm_g"]),))
    return out


def _forward(x, p, layers):
    cos4, sin4 = _rope_tables(x.shape[0])
    e_mat = _w_in_layout_matrix()
    for i in range(DEPTH):
        x = _layer(x, p[i], cos4, sin4, e_mat, layers[i])
    return x


SHARD_AXIS = dict(w_in=2, w_merge=2, conv_w=2, w_pw2=1, w_uq=2, w_ukv=2, w_glu=2, w_branch=3, w_out=1, w_ple=2, w_ple_gate=1)
ODD = ("w_uq", "conv_w")
BIG = tuple(n for n in SHARD_AXIS if n not in ODD)
REPLICATED = ("b_merge", "conv_b", "conv_norm_g", "conv_norm_b", "mla_q_norm_g", "mla_kv_norm_g", "ssm_a_re", "ssm_a_im",
              "ssm_log_dt", "ssm_b_re", "ssm_b_im", "ssm_c_re", "ssm_c_im", "ssm_d", "attn_sinks", "ln_g", "ln_b", "ple_norm_g")
WEIGHTS = ("w_in", "w_merge", "b_merge", "conv_w", "conv_b", "conv_norm_g", "conv_norm_b", "w_pw2", "mla_q_norm_g",
           "mla_kv_norm_g", "w_uq", "w_ukv", "ssm_a_re", "ssm_a_im", "ssm_log_dt", "ssm_b_re", "ssm_b_im", "ssm_c_re",
           "ssm_c_im", "ssm_d", "w_glu", "attn_sinks", "w_branch", "w_out", "ln_g", "ln_b", "w_ple", "w_ple_gate", "ple_norm_g")
PACK_COLS = 1024
PACK_ROWS = 16
CHIP_FLIPS = ((1, 0), (0, 1), (1, 1))
N_CHIPS = 4
N_DEV = 8


def _pack_rows(n):
    return -(-n // (SUBLANES * PACK_COLS)) * SUBLANES


def _pack(arrays, dtype):
    blocks, rows = [], 0
    for a in arrays:
        r = _pack_rows(a.size)
        flat = a.reshape(-1).astype(dtype)
        blocks.append(jnp.pad(flat, (0, r * PACK_COLS - a.size)).reshape(r, PACK_COLS))
        rows += r
    pad = -rows % PACK_ROWS
    if pad:
        blocks.append(jnp.zeros((pad, PACK_COLS), dtype))
    return jnp.concatenate(blocks, axis=0)


def _unpack(buf, shapes):
    out, row = [], 0
    for s in shapes:
        n = math.prod(s)
        r = _pack_rows(n)
        out.append(buf[row:row + r].reshape(-1)[:n].reshape(s))
        row += r
    return out


def _flip(v, bit):
    return 1 - v if bit else v


def _window(ref, axis, start, size):
    idx = [slice(None)] * len(ref.shape)
    idx[axis] = pl.ds(start, size)
    return ref.at[tuple(idx)]


def _gather_chips(srcs, axes, stacked):
    units = []
    for k, (s, a) in enumerate(zip(srcs, axes)):
        if stacked[k]:
            units += [(k, l, s.shape[1:], a - 1) for l in range(s.shape[0])]
        else:
            units.append((k, None, s.shape, a))
    nu, nb = len(units), len(srcs)

    def body(*refs):
        ins, outs = refs[:nb], refs[nb:nb + nu]
        ici_send, ici_recv, d2d_send, d2d_recv, local_sems = refs[nb + nu:]
        x, y, c = lax.axis_index("x"), lax.axis_index("y"), lax.axis_index("c")
        me = 2 * x + y

        def mine(u, half=None):
            k, l, shape, _ = units[u]
            ref = ins[k] if l is None else ins[k].at[l]
            return ref if half is None else ref.at[pl.ds(half * (shape[0] // 2), shape[0] // 2)]

        def place(u, chip, half=None):
            _, _, shape, a = units[u]
            size, rows = shape[a], shape[0] // 2
            if half is None:
                return _window(outs[u], a, chip * size, size)
            if a == 0:
                return outs[u].at[pl.ds(chip * size + half * rows, rows)]
            return _window(outs[u].at[pl.ds(half * rows, rows)], a, chip * size, size)

        local = [pltpu.make_async_copy(mine(u), place(u, me), local_sems.at[u]) for u in range(nu)]
        for cp in local:
            cp.start()
        sends = []
        for j, (bx, by) in enumerate(CHIP_FLIPS):
            for u in range(nu):
                cp = pltpu.make_async_remote_copy(src_ref=mine(u, c), dst_ref=place(u, me, c),
                                                  send_sem=ici_send.at[j * nu + u], recv_sem=ici_recv.at[j * nu + u],
                                                  device_id=(_flip(x, bx), _flip(y, by), c), device_id_type=MESH)
                cp.start()
                sends.append(cp)
        for j, (bx, by) in enumerate(CHIP_FLIPS):
            src = 2 * _flip(x, bx) + _flip(y, by)
            for u in range(nu):
                got = place(u, src, c)
                pltpu.make_async_remote_copy(src_ref=got, dst_ref=got, send_sem=ici_send.at[j * nu + u],
                                             recv_sem=ici_recv.at[j * nu + u], device_id=(x, y, c), device_id_type=MESH).wait_recv()
                cp = pltpu.make_async_remote_copy(src_ref=got, dst_ref=got, send_sem=d2d_send.at[j * nu + u],
                                                  recv_sem=d2d_recv.at[j * nu + u], device_id=(x, y, 1 - c), device_id_type=MESH)
                cp.start()
                sends.append(cp)
        for j, (bx, by) in enumerate(CHIP_FLIPS):
            src = 2 * _flip(x, bx) + _flip(y, by)
            for u in range(nu):
                other = place(u, src, 1 - c)
                pltpu.make_async_remote_copy(src_ref=other, dst_ref=other, send_sem=d2d_send.at[j * nu + u],
                                             recv_sem=d2d_recv.at[j * nu + u], device_id=(x, y, c), device_id_type=MESH).wait_recv()
        for cp in sends:
            cp.wait_send()
        for cp in local:
            cp.wait()

    full = lambda shape, a: tuple(N_CHIPS * d if i == a else d for i, d in enumerate(shape))
    res = pl.pallas_call(
        body, name="gather_weights", in_specs=[ANY] * nb, out_specs=[ANY] * nu,
        out_shape=[jax.ShapeDtypeStruct(full(shape, a), srcs[k].dtype) for k, _, shape, a in units],
        scratch_shapes=[pltpu.SemaphoreType.DMA((3 * nu,))] * 4 + [pltpu.SemaphoreType.DMA((nu,))],
    )(*srcs)
    out, it = [], iter(res)
    for k in range(nb):
        out.append([next(it) for _ in range(srcs[k].shape[0])] if stacked[k] else next(it))
    return out


def _exchange_grads(grads, axes, smalls):
    nt, ns = len(grads), len(smalls)
    sizes = [g[0].shape[a] // N_CHIPS for g, a in zip(grads, axes)]
    dev_flips = [(bx, by, bc) for bx in (0, 1) for by in (0, 1) for bc in (0, 1)][1:]
    n_remote = 3 * nt * DEPTH + 7 * ns
    n_local = nt * DEPTH + ns

    def body(*refs):
        g_refs = [refs[k * DEPTH:(k + 1) * DEPTH] for k in range(nt)]
        s_refs = refs[nt * DEPTH:nt * DEPTH + ns]
        outs = refs[nt * DEPTH + ns:nt * DEPTH + ns + nt + ns]
        recv_refs, all_refs = outs[:nt], outs[nt:]
        send_sems, recv_sems, local_sems = refs[-3:]
        x, y, c = lax.axis_index("x"), lax.axis_index("y"), lax.axis_index("c")
        me_chip = 2 * x + y
        me = 4 * x + 2 * y + c
        part = lambda k, i, chip: _window(g_refs[k][i], axes[k], chip * sizes[k], sizes[k])
        started, waits = [], []
        sem, lsem = 0, 0
        for k in range(nt):
            for i in range(DEPTH):
                cp = pltpu.make_async_copy(part(k, i, me_chip), recv_refs[k].at[i, 3], local_sems.at[lsem])
                cp.start()
                started.append(cp.wait)
                lsem += 1
                for j, (bx, by) in enumerate(CHIP_FLIPS):
                    px, py = _flip(x, bx), _flip(y, by)
                    cp = pltpu.make_async_remote_copy(src_ref=part(k, i, 2 * px + py), dst_ref=recv_refs[k].at[i, j],
                                                      send_sem=send_sems.at[sem], recv_sem=recv_sems.at[sem],
                                                      device_id=(px, py, c), device_id_type=MESH)
                    cp.start()
                    started.append(cp.wait_send)
                    waits.append(cp.wait_recv)
                    sem += 1
        for s in range(ns):
            cp = pltpu.make_async_copy(s_refs[s], all_refs[s].at[me], local_sems.at[lsem])
            cp.start()
            started.append(cp.wait)
            lsem += 1
            for bx, by, bc in dev_flips:
                peer = (_flip(x, bx), _flip(y, by), _flip(c, bc))
                cp = pltpu.make_async_remote_copy(src_ref=s_refs[s], dst_ref=all_refs[s].at[me], send_sem=send_sems.at[sem],
                                                  recv_sem=recv_sems.at[sem], device_id=peer, device_id_type=MESH)
                cp.start()
                started.append(cp.wait_send)
                src = 4 * peer[0] + 2 * peer[1] + peer[2]
                waits.append(pltpu.make_async_remote_copy(src_ref=s_refs[s], dst_ref=all_refs[s].at[src], send_sem=send_sems.at[sem],
                                                          recv_sem=recv_sems.at[sem], device_id=peer, device_id_type=MESH).wait_recv)
                sem += 1
        for w in waits + started:
            w()

    shard = lambda g, a: tuple(d // N_CHIPS if i == a else d for i, d in enumerate(g.shape))
    flat = [g for per_layer in grads for g in per_layer]
    return pl.pallas_call(
        body, name="exchange_grads", in_specs=[ANY] * (len(flat) + ns), out_specs=[ANY] * (nt + ns),
        out_shape=[jax.ShapeDtypeStruct((DEPTH, N_CHIPS, *shard(g[0], a)), g[0].dtype) for g, a in zip(grads, axes)]
        + [jax.ShapeDtypeStruct((N_DEV, *s.shape), s.dtype) for s in smalls],
        scratch_shapes=[pltpu.SemaphoreType.DMA((n_remote,)), pltpu.SemaphoreType.DMA((n_remote,)), pltpu.SemaphoreType.DMA((n_local,))],
    )(*flat, *smalls)


def _swap_cores(parts):
    nb = len(parts)

    def body(*refs):
        ins, outs, send_sems, recv_sems = refs[:nb], refs[nb:2 * nb], refs[-2], refs[-1]
        x, y, c = lax.axis_index("x"), lax.axis_index("y"), lax.axis_index("c")
        cps = [pltpu.make_async_remote_copy(src_ref=ins[k], dst_ref=outs[k], send_sem=send_sems.at[k], recv_sem=recv_sems.at[k],
                                            device_id=(x, y, 1 - c), device_id_type=MESH) for k in range(nb)]
        for cp in cps:
            cp.start()
        for cp in cps:
            cp.wait()

    return pl.pallas_call(
        body, name="swap_cores", in_specs=[ANY] * nb, out_specs=[ANY] * nb,
        out_shape=[jax.ShapeDtypeStruct(q.shape, q.dtype) for q in parts],
        scratch_shapes=[pltpu.SemaphoreType.DMA((nb,)), pltpu.SemaphoreType.DMA((nb,))],
    )(*parts)


def _sum_chips_call(recv, cols, name):
    depth, _, r, c = recv.shape
    tile = _pick(r, (512, 256, 128, 64, 32, 16))

    def body(r_ref, o_ref):
        slot = lambda s: r_ref[s, :, pl.ds(0, cols)].astype(F32)
        o_ref[...] = ((slot(3) + slot(0)) + slot(1)) + slot(2)

    return pl.pallas_call(
        body, name=name, grid=(depth, r // tile),
        in_specs=[pl.BlockSpec((None, N_CHIPS, tile, c), lambda l, i: (l, 0, i, 0))],
        out_specs=pl.BlockSpec((None, tile, cols), lambda l, i: (l, i, 0)), out_shape=jax.ShapeDtypeStruct((depth, r, cols), F32),
        compiler_params=_params(("parallel", "parallel")),
    )(recv)


def _sum_slots_call(slots, name):
    n, r, c = slots.shape
    tile = _pick(r, (512, 256, 128, 64, 32, 16, 8))

    def body(s_ref, o_ref):
        acc = s_ref[0]
        for s in range(1, n):
            acc = acc + s_ref[s]
        o_ref[...] = acc

    return pl.pallas_call(
        body, name=name, grid=(r // tile,), in_specs=[pl.BlockSpec((n, tile, c), lambda i: (0, i, 0))],
        out_specs=pl.BlockSpec((tile, c), lambda i: (i, 0)), out_shape=jax.ShapeDtypeStruct((r, c), F32),
        compiler_params=_params(("parallel",)),
    )(slots)


def _adamw_math(w, g, m, v):
    m = ADAM_B1 * m + (1.0 - ADAM_B1) * g
    v = ADAM_B2 * v + (1.0 - ADAM_B2) * (g * g)
    m_hat = m / (1.0 - ADAM_B1 ** ADAM_STEP)
    v_hat = v / (1.0 - ADAM_B2 ** ADAM_STEP)
    return -ADAM_LR * (m_hat / (jnp.sqrt(v_hat) + ADAM_EPS) + ADAM_WD * w), m, v


def _adamw_call(w, m, v, gparts, name):
    r, c = w.shape
    n = len(gparts)
    tile = _pick(r, (512, 256, 128, 64, 32, 16, 8))

    def body(w_ref, m_ref, v_ref, *refs):
        g_refs, (go_ref, d_ref, mo_ref, vo_ref) = refs[:n], refs[n:]
        g = g_refs[0][...]
        for g_ref in g_refs[1:]:
            g = g + g_ref[...]
        go_ref[...] = g
        d_ref[...], mo_ref[...], vo_ref[...] = _adamw_math(w_ref[...], g, m_ref[...], v_ref[...])

    blk = pl.BlockSpec((tile, c), lambda i: (i, 0))
    return pl.pallas_call(
        body, name=name, grid=(r // tile,), in_specs=[blk] * (3 + n),
        out_specs=[blk] * 4, out_shape=[jax.ShapeDtypeStruct((r, c), F32)] * 4,
        compiler_params=_params(("parallel",)),
    )(w, m, v, *gparts)


def _train_local(x, p, layers, target):
    y, vjp = jax.vjp(lambda x_, w_: _forward(x_, p, w_), x, layers)
    loss, dy = _loss_call(y, target)
    dx, dw = vjp(dy)
    return loss, dx, dw


def kernel(x, p, w_in, w_merge, b_merge, conv_w, conv_b, conv_norm_g, conv_norm_b, w_pw2, mla_q_norm_g, mla_kv_norm_g, w_uq, w_ukv, ssm_a_re, ssm_a_im, ssm_log_dt, ssm_b_re, ssm_b_im, ssm_c_re, ssm_c_im, ssm_d, w_glu, attn_sinks, w_branch, w_out, ln_g, ln_b, w_ple, w_ple_gate, ple_norm_g, loss_target, m_w_in, m_w_merge, m_b_merge, m_conv_w, m_conv_b, m_conv_norm_g, m_conv_norm_b, m_w_pw2, m_mla_q_norm_g, m_mla_kv_norm_g, m_w_uq, m_w_ukv, m_ssm_a_re, m_ssm_a_im, m_ssm_log_dt, m_ssm_b_re, m_ssm_b_im, m_ssm_c_re, m_ssm_c_im, m_ssm_d, m_w_glu, m_attn_sinks, m_w_branch, m_w_out, m_ln_g, m_ln_b, m_w_ple, m_w_ple_gate, m_ple_norm_g, v_w_in, v_w_merge, v_b_merge, v_conv_w, v_conv_b, v_conv_norm_g, v_conv_norm_b, v_w_pw2, v_mla_q_norm_g, v_mla_kv_norm_g, v_w_uq, v_w_ukv, v_ssm_a_re, v_ssm_a_im, v_ssm_log_dt, v_ssm_b_re, v_ssm_b_im, v_ssm_c_re, v_ssm_c_im, v_ssm_d, v_w_glu, v_attn_sinks, v_w_branch, v_w_out, v_ln_g, v_ln_b, v_w_ple, v_w_ple_gate, v_ple_norm_g):
    given = dict(locals())
    w_loc = {n: given[n] for n in WEIGHTS}
    m_loc = {n: given["m_" + n] for n in WEIGHTS}
    v_loc = {n: given["v_" + n] for n in WEIGHTS}

    me_chip = 2 * lax.axis_index("x") + lax.axis_index("y")

    wire = {n: w_loc[n].astype(MXU_DTYPE) for n in BIG}
    wire["w_in"] = jnp.pad(wire["w_in"], ((0, 0), (0, 0), (0, IN_SHARD_PAD - IN_SHARD)))
    odd_shapes = [w_loc[n].shape for n in ODD]
    gathered = _gather_chips([wire[n] for n in BIG] + [_pack([w_loc[n] for n in ODD], F32)], [SHARD_AXIS[n] for n in BIG] + [0],
                             [True] * len(BIG) + [False])
    full = dict(zip(BIG, gathered[:-1]))
    odd_parts = [_unpack(part, odd_shapes) for part in jnp.split(gathered[-1], N_CHIPS, axis=0)]
    for k, n in enumerate(ODD):
        full[n] = jnp.concatenate([odd_parts[s][k] for s in range(N_CHIPS)], axis=SHARD_AXIS[n])
    layers = []
    for i in range(DEPTH):
        layer = {n: (full[n][i] if n in full else w_loc[n][i]) for n in WEIGHTS if n != "w_in"}
        layer["w_in_pad"] = full["w_in"][i]
        layers.append(layer)

    loss, dx, dw = _train_local(x[0], p[:, 0], layers, loss_target[0])
    loss = lax.psum(loss, ("x", "y", "c"))

    key = lambda n: "w_in_pad" if n == "w_in" else n
    stacked = lambda n: jnp.stack([dw[i][n] for i in range(DEPTH)])
    small_rep = _pack([stacked(n) for n in REPLICATED], F32)
    small_odd = _pack([stacked(n) for n in ODD], F32)
    *recv, all_rep, all_odd = _exchange_grads([[dw[i][key(n)] for i in range(DEPTH)] for n in BIG],
                                              [SHARD_AXIS[n] - 1 for n in BIG], [small_rep, small_odd])
    parts = []
    for n, r in zip(BIG, recv):
        cols = w_loc[n].shape[-1]
        parts.append(_sum_chips_call(r.reshape(DEPTH, N_CHIPS, -1, r.shape[-1]), cols, "sum_chips_" + n))
    others = _swap_cores(parts)
    g_rep = _sum_slots_call(all_rep, "sum_replicated")
    g_odd = _unpack(_sum_slots_call(all_odd, "sum_odd"), [(DEPTH, *w_loc[n].shape[1:-1], N_CHIPS * w_loc[n].shape[-1]) for n in ODD])

    grads, deltas, new_m, new_v = {}, {}, {}, {}

    def adamw(n, gparts):
        shape = w_loc[n].shape
        two_d = lambda a: a.reshape(-1, shape[-1])
        res = _adamw_call(two_d(w_loc[n]), two_d(m_loc[n]), two_d(v_loc[n]), [two_d(g) for g in gparts], "adamw_" + n)
        grads[n], deltas[n], new_m[n], new_v[n] = [r.reshape(shape) for r in res]

    for n, part, other in zip(BIG, parts, others):
        adamw(n, [part, other])
    for n, g in zip(ODD, g_odd):
        size = w_loc[n].shape[-1]
        adamw(n, [lax.dynamic_slice_in_dim(g, me_chip * size, size, axis=g.ndim - 1)])
    rep_shapes = [w_loc[n].shape for n in REPLICATED]
    res = _adamw_call(_pack([w_loc[n] for n in REPLICATED], F32), _pack([m_loc[n] for n in REPLICATED], F32),
                      _pack([v_loc[n] for n in REPLICATED], F32), [g_rep], "adamw_replicated")
    for dst, buf in zip((grads, deltas, new_m, new_v), res):
        for n, a in zip(REPLICATED, _unpack(buf, rep_shapes)):
            dst[n] = a

    return (loss, dx[None], *[grads[n] for n in WEIGHTS], *[deltas[n] for n in WEIGHTS],
            *[new_m[n] for n in WEIGHTS], *[new_v[n] for n in WEIGHTS])
```

```python
import functools
import math

import jax
import jax.numpy as jnp
import numpy as np
from jax import lax
from jax.experimental import pallas as pl
from jax.experimental.pallas import tpu as pltpu

F32 = jnp.float32
BF16 = jnp.bfloat16
MXU_DTYPE = BF16
V7X_VMEM_BYTES = 64 * 1024 * 1024
VMEM_LIMIT = V7X_VMEM_BYTES * 3 // 4
LANES = 128
SUBLANES = 8

D_MODEL = 1024
DEPTH = 4
BRANCH_W = 256
CONV_W = 31
CONV_HALO = 32
CONV_CHUNK = 64
MLA_SCALE = (64 + 32) ** -0.5
SWA_SCALE = 64 ** -0.5
WINDOW = 128
ROPE_THETA = 10000.0
SSM_GROUPS, SSM_GROUP, SSM_STATE = 16, 16, 64
SSM_CH = SSM_GROUPS * SSM_STATE
SCAN_SEGMENTS = SUBLANES
SCAN_CB = 256
DEEPNORM_ALPHA = (2.0 * DEPTH) ** 0.25
LN_EPS = 1e-5
RMS_EPS = 1e-6
ADAM_LR, ADAM_B1, ADAM_B2, ADAM_EPS, ADAM_WD, ADAM_STEP = 0.001, 0.9, 0.999, 1e-08, 0.01, 10
NEG = -1e30
ROW_TILE = 512

NN = (((1,), (0,)), ((), ()))
NT = (((1,), (1,)), ((), ()))
TN = (((0,), (0,)), ((), ()))

MESH = pl.DeviceIdType.MESH
ANY = pl.BlockSpec(memory_space=pl.ANY)


def _dot(a, b, dims):
    return lax.dot_general(a.astype(MXU_DTYPE), b.astype(MXU_DTYPE), dims, preferred_element_type=F32)


def _pick(n, cands):
    for c in cands:
        if n % c == 0:
            return c
    return n


def _params(sem, vmem_limit=VMEM_LIMIT):
    return pltpu.CompilerParams(dimension_semantics=sem, vmem_limit_bytes=vmem_limit)


def _col_offsets(widths):
    return [sum(widths[:j]) for j in range(len(widths))]


def _silu_gate(x, z):
    return x * (z * jax.nn.sigmoid(z))


def _proj_fwd_call(x, wb, widths, name, z=None):
    t, k = x.shape
    tm = min(ROW_TILE, t)
    offs = _col_offsets(widths)
    ins = [x] if z is None else [x, z]

    def body(*refs):
        w_ref, o_refs = refs[len(ins)], refs[len(ins) + 1:]
        xv = refs[0][...] if z is None else _silu_gate(refs[0][...], refs[1][...])
        xb = xv.astype(MXU_DTYPE)
        for o_ref, off, wd in zip(o_refs, offs, widths):
            o_ref[...] = _dot(xb, w_ref[:, off:off + wd], NN)

    return pl.pallas_call(
        body, name=name, grid=(t // tm,),
        in_specs=[pl.BlockSpec((tm, k), lambda i: (i, 0))] * len(ins) + [pl.BlockSpec(wb.shape, lambda i: (0, 0))],
        out_specs=[pl.BlockSpec((tm, wd), lambda i: (i, 0)) for wd in widths],
        out_shape=[jax.ShapeDtypeStruct((t, wd), F32) for wd in widths],
        compiler_params=_params(("parallel",)),
    )(*ins, wb)


def _proj_dx_call(douts, wb, widths, name, gate=None):
    t = douts[0].shape[0]
    k = wb.shape[0]
    tm = min(ROW_TILE, t)
    offs = _col_offsets(widths)
    nd = len(douts)
    extra = [] if gate is None else list(gate)

    def body(*refs):
        d_refs, w_ref = refs[:nd], refs[nd]
        acc = jnp.zeros((tm, k), F32)
        for d_ref, off, wd in zip(d_refs, offs, widths):
            acc = acc + _dot(d_ref[...], w_ref[:, off:off + wd], NT)
        if gate is None:
            refs[-1][...] = acc
        else:
            xv, zv = refs[nd + 1][...], refs[nd + 2][...]
            sg = jax.nn.sigmoid(zv)
            refs[-2][...] = acc * (zv * sg)
            refs[-1][...] = acc * xv * (sg * (1.0 + zv * (1.0 - sg)))

    row = pl.BlockSpec((tm, k), lambda i: (i, 0))
    n_out = 1 if gate is None else 2
    res = pl.pallas_call(
        body, name=name, grid=(t // tm,),
        in_specs=[pl.BlockSpec((tm, wd), lambda i: (i, 0)) for wd in widths] + [pl.BlockSpec(wb.shape, lambda i: (0, 0))] + [row] * len(extra),
        out_specs=[row] * n_out, out_shape=[jax.ShapeDtypeStruct((t, k), F32)] * n_out,
        compiler_params=_params(("parallel",)),
    )(*douts, wb, *extra)
    return res[0] if gate is None else tuple(res)


def _proj_dw_call(x, douts, widths, name, out_dtype, z=None):
    t, k = x.shape
    n = sum(widths)
    tk = min(ROW_TILE if k * n <= 2 * 1024 * 1024 else ROW_TILE // 2, t)
    nk = t // tk
    offs = _col_offsets(widths)
    ins = [x] if z is None else [x, z]

    def body(*all_refs):
        refs = all_refs[len(ins):]
        d_refs, o_ref, acc_ref = refs[:-2], refs[-2], refs[-1]

        @pl.when(pl.program_id(0) == 0)
        def _():
            acc_ref[...] = jnp.zeros_like(acc_ref)

        xv = all_refs[0][...] if z is None else _silu_gate(all_refs[0][...], all_refs[1][...])
        xb = xv.astype(MXU_DTYPE)
        for d_ref, off, wd in zip(d_refs, offs, widths):
            acc_ref[:, off:off + wd] += _dot(xb, d_ref[...], TN)

        @pl.when(pl.program_id(0) == nk - 1)
        def _():
            o_ref[...] = acc_ref[...].astype(out_dtype)

    return pl.pallas_call(
        body, name=name, grid=(nk,),
        in_specs=[pl.BlockSpec((tk, k), lambda i: (i, 0))] * len(ins) + [pl.BlockSpec((tk, wd), lambda i: (i, 0)) for wd in widths],
        out_specs=pl.BlockSpec((k, n), lambda i: (0, 0)), out_shape=jax.ShapeDtypeStruct((k, n), out_dtype),
        scratch_shapes=[pltpu.VMEM((k, n), F32)],
        compiler_params=_params(("arbitrary",)),
    )(*ins, *douts)


def make_proj(widths, name):
    @jax.custom_vjp
    def op(x, w):
        return tuple(_proj_fwd_call(x, w.astype(MXU_DTYPE), widths, name + "_fwd"))

    def fwd(x, w):
        wb = w.astype(MXU_DTYPE)
        return tuple(_proj_fwd_call(x, wb, widths, name + "_fwd")), (x, wb, jnp.zeros((0,), w.dtype))

    def bwd(res, douts):
        x, wb, w_like = res
        return _proj_dx_call(douts, wb, widths, name + "_dx"), _proj_dw_call(x, douts, widths, name + "_dw", w_like.dtype)

    op.defvjp(fwd, bwd)
    return op


_MM_OPS = {}


def op_mm(a, w):
    n = w.shape[1]
    if n not in _MM_OPS:
        _MM_OPS[n] = make_proj((n,), "mm%d" % n)
    return _MM_OPS[n](a, w)[0]


@jax.custom_vjp
def _mm(a, w):
    return _dot(a, w, NN)


def _mm_f(a, w):
    return _dot(a, w, NN), (a, w)


def _mm_b(res, g):
    a, w = res
    return _dot(g, w, NT), _dot(a, g, TN)


_mm.defvjp(_mm_f, _mm_b)


@functools.partial(jax.custom_vjp, nondiff_argnums=(1,))
def _roll(x, shift):
    return pltpu.roll(x, shift, 1)


def _roll_f(x, shift):
    return pltpu.roll(x, shift, 1), None


def _roll_b(shift, _, g):
    return (pltpu.roll(g, (g.shape[1] - shift) % g.shape[1], 1),)


_roll.defvjp(_roll_f, _roll_b)


def _ln(x, g, b):
    mu = jnp.mean(x, axis=-1, keepdims=True)
    xc = x - mu
    var = jnp.mean(xc * xc, axis=-1, keepdims=True)
    return xc * lax.rsqrt(var + LN_EPS) * g + b


def _rms(x, g):
    ms = jnp.mean(x * x, axis=-1, keepdims=True)
    return x * lax.rsqrt(ms + RMS_EPS) * g


def _sigmoid(x):
    return jax.nn.sigmoid(x)


def _silu(x):
    return x * _sigmoid(x)


def _gelu_tanh(x):
    return x * (0.5 * (1.0 + jnp.tanh(math.sqrt(2.0 / math.pi) * (x + 0.044715 * (x * x * x)))))


def _rowwise_fwd_call(fn, rows, consts, name, tile):
    t = rows[0].shape[0]
    tile = min(tile, t)
    nr = len(rows)
    outs = jax.eval_shape(fn, *[jax.ShapeDtypeStruct((tile, r.shape[1]), F32) for r in rows],
                          *[jax.ShapeDtypeStruct(c.shape, F32) for c in consts])

    def body(*refs):
        vals = [r[...] for r in refs[:nr + len(consts)]]
        res = fn(*vals)
        for o_ref, o in zip(refs[nr + len(consts):], res):
            o_ref[...] = o

    return pl.pallas_call(
        body, name=name, grid=(t // tile,),
        in_specs=[pl.BlockSpec((tile, r.shape[1]), lambda i: (i, 0)) for r in rows]
        + [pl.BlockSpec(c.shape, lambda i: (0, 0)) for c in consts],
        out_specs=[pl.BlockSpec((tile, o.shape[1]), lambda i: (i, 0)) for o in outs],
        out_shape=[jax.ShapeDtypeStruct((t, o.shape[1]), F32) for o in outs],
        compiler_params=_params(("parallel",)),
    )(*rows, *consts)


def _rowwise_bwd_call(fn, rows, consts, douts, row_diff, name, tile, row_grad_dtype=F32):
    t = rows[0].shape[0]
    tile = min(tile, t)
    nr, nc, nd = len(rows), len(consts), len(douts)
    diff_idx = [i for i in range(nr) if row_diff[i]]

    def body(*refs):
        rv = [r[...] for r in refs[:nr]]
        cv = [r[...] for r in refs[nr:nr + nc]]
        dv = [r[...] for r in refs[nr + nc:nr + nc + nd]]
        out_refs = refs[nr + nc + nd:]

        def f(*diff):
            full = list(rv)
            for k, i in enumerate(diff_idx):
                full[i] = diff[k]
            return fn(*full, *diff[len(diff_idx):])

        _, vjp = jax.vjp(f, *[rv[i] for i in diff_idx], *cv)
        grads = vjp(tuple(dv))
        for k in range(len(diff_idx)):
            out_refs[k][...] = grads[k].astype(row_grad_dtype)
        first = pl.program_id(0) == 0
        for k in range(nc):
            acc_ref = out_refs[len(diff_idx) + k]
            g = grads[len(diff_idx) + k]

            @pl.when(first)
            def _(acc_ref=acc_ref, g=g):
                acc_ref[...] = g

            @pl.when(jnp.logical_not(first))
            def _(acc_ref=acc_ref, g=g):
                acc_ref[...] += g

    res = pl.pallas_call(
        body, name=name, grid=(t // tile,),
        in_specs=[pl.BlockSpec((tile, r.shape[1]), lambda i: (i, 0)) for r in rows]
        + [pl.BlockSpec(c.shape, lambda i: (0, 0)) for c in consts]
        + [pl.BlockSpec((tile, d.shape[1]), lambda i: (i, 0)) for d in douts],
        out_specs=[pl.BlockSpec((tile, rows[i].shape[1]), lambda i_: (i_, 0)) for i in diff_idx]
        + [pl.BlockSpec(c.shape, lambda i: (0, 0)) for c in consts],
        out_shape=[jax.ShapeDtypeStruct(rows[i].shape, row_grad_dtype) for i in diff_idx]
        + [jax.ShapeDtypeStruct(c.shape, F32) for c in consts],
        compiler_params=_params(("arbitrary",)),
    )(*rows, *consts, *douts)
    return res[:len(diff_idx)], res[len(diff_idx):]


def make_rowwise(fn, name, row_diff, tile=ROW_TILE):
    @jax.custom_vjp
    def op(rows, consts):
        return tuple(_rowwise_fwd_call(fn, rows, consts, name + "_fwd", tile))

    def fwd(rows, consts):
        return op(rows, consts), (rows, consts)

    def bwd(res, douts):
        rows, consts = res
        drows, dconsts = _rowwise_bwd_call(fn, rows, consts, douts, row_diff, name + "_bwd", tile)
        it = iter(drows)
        full = tuple(next(it) if row_diff[i] else jnp.zeros_like(rows[i]) for i in range(len(rows)))
        return full, tuple(dconsts)

    op.defvjp(fwd, bwd)
    return op


def _conv_post_fn(cv, a_z, ng, nb, w_pw2):
    return (_mm(_silu(_ln(cv, ng, nb)), w_pw2) * _silu(a_z),)


def _mla_prep_fn(c_q, c_kv, krblk, cos4, sin4, qg, kvg, w_uq, w_uk, w_uv):
    qe = _mm(_rms(c_q, qg), w_uq)
    q = qe * cos4 + _roll(qe, qe.shape[1] - 32) * sin4
    cos1, sin1 = cos4[:, :LANES], sin4[:, :LANES]
    kr = krblk * cos1 + _roll(krblk, LANES - 32) * sin1
    kn = _rms(c_kv, kvg)
    k = _mm(kn, w_uk) + jnp.concatenate([kr, kr, kr, kr], axis=1)
    return q, k, _mm(kn, w_uv)


def _ssm_post_fn(y, u, c_z, d, w_a, w_b):
    y2 = _gelu_tanh(y + d * u)
    return (_mm(y2, w_a) * _sigmoid(_mm(y2, w_b)) * _silu(c_z),)


def _merge_fn(br0, br1, br2, br3, gl0, gl1, gl2, gl3, b0, b1, b2, b3):
    return (_sigmoid(gl0 + b0) * br0 + _sigmoid(gl1 + b1) * br1 + _sigmoid(gl2 + b2) * br2 + _sigmoid(gl3 + b3) * br3,)


def _ln_fn(x, mo, g, b):
    return (_ln(DEEPNORM_ALPHA * x + mo, g, b),)


def _ple_fn(x1, pe, gl, g):
    return (x1 + _rms(pe * _sigmoid(gl), g),)


op_conv_post = make_rowwise(_conv_post_fn, "conv_post", (True, True))
op_mla_prep = make_rowwise(_mla_prep_fn, "mla_prep", (True, True, True, False, False))
op_ssm_post = make_rowwise(_ssm_post_fn, "ssm_post", (True, True, True))
MERGE_TILE = ROW_TILE // 2
MERGE_WIDTHS = (D_MODEL,) * 4


@jax.custom_vjp
def op_merge_block(x, ys, zs, w_merge, w_branch, b_merge):
    return _merge_block_fwd(x, ys, zs, w_merge, w_branch, b_merge)[0]


def _merge_block_fwd(x, ys, zs, w_merge, w_branch, b_merge):
    wm, wb = w_merge.astype(MXU_DTYPE), w_branch.astype(MXU_DTYPE)
    gl = _proj_fwd_call(x, wm, MERGE_WIDTHS, "merge_proj_fwd")
    br = [_proj_fwd_call(ys[n], wb[n], (D_MODEL,), "branch_proj_fwd", zs[n])[0] for n in range(4)]
    bm = tuple(b_merge[n * D_MODEL:(n + 1) * D_MODEL].reshape(1, -1) for n in range(4))
    (merged,) = _rowwise_fwd_call(_merge_fn, (*br, *gl), bm, "merge_fwd", MERGE_TILE)
    return merged, (x, ys, zs, wm, wb, tuple(br), tuple(gl), bm, jnp.zeros((0,), w_merge.dtype), jnp.zeros((0,), w_branch.dtype))


def _merge_block_bwd(res, dmerged):
    x, ys, zs, wm, wb, br, gl, bm, wm_like, wb_like = res
    drows, dbm = _rowwise_bwd_call(_merge_fn, (*br, *gl), bm, (dmerged,), (True,) * 8, "merge_bwd", MERGE_TILE, MXU_DTYPE)
    dbr, dgl = drows[:4], drows[4:]
    dx = _proj_dx_call(dgl, wm, MERGE_WIDTHS, "merge_proj_dx")
    dwm = _proj_dw_call(x, dgl, MERGE_WIDTHS, "merge_proj_dw", wm_like.dtype)
    dys, dzs = [], []
    for n in range(4):
        if zs[n] is None:
            dys.append(_proj_dx_call([dbr[n]], wb[n], (D_MODEL,), "branch_proj_dx"))
            dzs.append(None)
        else:
            dy, dz = _proj_dx_call([dbr[n]], wb[n], (D_MODEL,), "branch_proj_dx", (ys[n], zs[n]))
            dys.append(dy)
            dzs.append(dz)
    dwb = jnp.stack([_proj_dw_call(ys[n], [dbr[n]], (D_MODEL,), "branch_proj_dw", wb_like.dtype, zs[n]) for n in range(4)])
    return dx, tuple(dys), tuple(dzs), dwm, dwb, jnp.concatenate([d.reshape(-1) for d in dbm])


op_merge_block.defvjp(_merge_block_fwd, _merge_block_bwd)
op_ln = make_rowwise(_ln_fn, "post_ln", (True, True))
op_ple = make_rowwise(_ple_fn, "ple", (True, True, True))


def _shift_copies(buf, shifted, tile):
    for r in range(1, SUBLANES):
        shifted[r - 1, :, :] = buf[pl.ds(r, tile + CONV_HALO - SUBLANES), :]


def _tap(buf, shifted, off, tile):
    r = off % SUBLANES
    return buf[pl.ds(off, tile), :] if r == 0 else shifted[r - 1, pl.ds(off - r, tile), :]


def _shift_scratch(tile, w):
    return pltpu.VMEM((SUBLANES - 1, tile + CONV_HALO - SUBLANES, w), F32)


def _conv_fwd_call(a_val, a_gate, w32, b):
    t, w = a_val.shape
    tile = min(ROW_TILE, t)
    per = tile // CONV_HALO
    cur = pl.BlockSpec((tile, w), lambda i: (i, 0))
    prev = pl.BlockSpec((CONV_HALO, w), lambda i: (jnp.maximum(i * per - 1, 0), 0))

    def body(av_ref, avh_ref, ag_ref, agh_ref, w_ref, b_ref, cv_ref, buf, shifted):
        i = pl.program_id(0)
        gh = avh_ref[...] * _sigmoid(agh_ref[...])
        buf[0:CONV_HALO, :] = jnp.where(i > 0, gh, 0.0)
        buf[CONV_HALO:, :] = av_ref[...] * _sigmoid(ag_ref[...])
        _shift_copies(buf, shifted, tile)
        for c0 in range(0, tile, CONV_CHUNK):
            acc = jnp.zeros((CONV_CHUNK, w), F32) + b_ref[...]
            for j in range(CONV_W):
                acc = acc + w_ref[j:j + 1, :] * _tap(buf, shifted, c0 + CONV_HALO - (CONV_W - 1) + j, CONV_CHUNK)
            cv_ref[c0:c0 + CONV_CHUNK, :] = acc

    return pl.pallas_call(
        body, name="conv_fwd", grid=(t // tile,),
        in_specs=[cur, prev, cur, prev, pl.BlockSpec((CONV_HALO, w), lambda i: (0, 0)), pl.BlockSpec((1, w), lambda i: (0, 0))],
        out_specs=cur, out_shape=jax.ShapeDtypeStruct((t, w), F32),
        scratch_shapes=[pltpu.VMEM((tile + CONV_HALO, w), F32), _shift_scratch(tile, w)],
        compiler_params=_params(("parallel",)),
    )(a_val, a_val, a_gate, a_gate, w32, b)


def _conv_bwd_call(a_val, a_gate, w32, dcv):
    t, w = a_val.shape
    tile = min(ROW_TILE, t)
    n = t // tile
    per = tile // CONV_HALO
    cur = pl.BlockSpec((tile, w), lambda i: (i, 0))
    prev = pl.BlockSpec((CONV_HALO, w), lambda i: (jnp.maximum(i * per - 1, 0), 0))
    nxt = pl.BlockSpec((CONV_HALO, w), lambda i: (jnp.minimum((i + 1) * per, t // CONV_HALO - 1), 0))
    full = lambda r: pl.BlockSpec((r, w), lambda i: (0, 0))

    def body(av_ref, avh_ref, ag_ref, agh_ref, w_ref, d_ref, dn_ref, dav_ref, dag_ref, dw_ref, db_ref, gbuf, dbuf, gsh, dsh):
        i = pl.program_id(0)
        gh = avh_ref[...] * _sigmoid(agh_ref[...])
        gbuf[0:CONV_HALO, :] = jnp.where(i > 0, gh, 0.0)
        gbuf[CONV_HALO:, :] = av_ref[...] * _sigmoid(ag_ref[...])
        dbuf[0:tile, :] = d_ref[...]
        dbuf[tile:, :] = jnp.where(i < n - 1, dn_ref[...], 0.0)

        @pl.when(i == 0)
        def _():
            dw_ref[...] = jnp.zeros_like(dw_ref)
            db_ref[...] = jnp.zeros_like(db_ref)

        _shift_copies(gbuf, gsh, tile)
        _shift_copies(dbuf, dsh, tile)
        for c0 in range(0, tile, CONV_CHUNK):
            rows = slice(c0, c0 + CONV_CHUNK)
            d = d_ref[rows, :]
            dg = jnp.zeros((CONV_CHUNK, w), F32)
            for j in range(CONV_W):
                dg = dg + w_ref[j:j + 1, :] * _tap(dbuf, dsh, c0 + CONV_W - 1 - j, CONV_CHUNK)
                dw_ref[j:j + 1, :] += jnp.sum(d * _tap(gbuf, gsh, c0 + CONV_HALO - (CONV_W - 1) + j, CONV_CHUNK), axis=0, keepdims=True)
            db_ref[...] += jnp.sum(d, axis=0, keepdims=True)
            av = av_ref[rows, :]
            sg = _sigmoid(ag_ref[rows, :])
            dav_ref[rows, :] = dg * sg
            dag_ref[rows, :] = dg * av * sg * (1.0 - sg)

    return pl.pallas_call(
        body, name="conv_bwd", grid=(n,),
        in_specs=[cur, prev, cur, prev, full(CONV_HALO), cur, nxt],
        out_specs=[cur, cur, full(CONV_HALO), full(1)],
        out_shape=[jax.ShapeDtypeStruct((t, w), F32), jax.ShapeDtypeStruct((t, w), F32),
                   jax.ShapeDtypeStruct((CONV_HALO, w), F32), jax.ShapeDtypeStruct((1, w), F32)],
        scratch_shapes=[pltpu.VMEM((tile + CONV_HALO, w), F32), pltpu.VMEM((tile + CONV_HALO, w), F32),
                        _shift_scratch(tile, w), _shift_scratch(tile, w)],
        compiler_params=_params(("arbitrary",)),
    )(a_val, a_val, a_gate, a_gate, w32, dcv, dcv)


def _pad_taps(conv_w):
    return jnp.concatenate([conv_w, jnp.zeros((CONV_HALO - CONV_W, conv_w.shape[1]), F32)], axis=0)


@jax.custom_vjp
def op_conv(a_val, a_gate, conv_w, conv_b):
    return _conv_fwd_call(a_val, a_gate, _pad_taps(conv_w), conv_b)


def _op_conv_fwd(a_val, a_gate, conv_w, conv_b):
    return op_conv(a_val, a_gate, conv_w, conv_b), (a_val, a_gate, conv_w)


def _op_conv_bwd(res, dcv):
    a_val, a_gate, conv_w = res
    dav, dag, dw, db = _conv_bwd_call(a_val, a_gate, _pad_taps(conv_w), dcv)
    return dav, dag, dw[:CONV_W], db


op_conv.defvjp(_op_conv_fwd, _op_conv_bwd)


def _head_masks(rows):
    lane = lax.broadcasted_iota(jnp.int32, (rows, LANES), 1)
    return lane < 64, lane >= 64


def _head_row(vals, mask):
    return jnp.max(jnp.where(mask, vals, NEG), axis=1, keepdims=True)


def _attn_valid(qpos, kpos, window):
    valid = kpos <= qpos
    if window is not None:
        valid = jnp.logical_and(valid, qpos - kpos < window)
    return valid


def _flash_fwd_call(q, k, v, sink, *, window, shared_k, scale, blk, blk_q, name):
    t = q.shape[0]
    qw = LANES if shared_k else 2 * LANES
    pairs = v.shape[1] // LANES
    tk = min(blk, t)
    tq = min(blk_q, t)
    has_sink = sink is not None
    one_step = window is not None and tk == 2 * tq and window <= tq
    kstride = tq if one_step else tk

    def body(*refs):
        if has_sink:
            q_ref, k_ref, v_ref, s_ref, o_ref, lse_ref, k_mxu, v0_mxu, v1_mxu = refs
        else:
            q_ref, k_ref, v_ref, o_ref, lse_ref, k_mxu, v0_mxu, v1_mxu = refs
        v_mxu = (v0_mxu, v1_mxu)
        i = pl.program_id(1)

        @pl.when(i == 0)
        def _():
            full_masks = _head_masks(t)
            k_mxu[...] = k_ref[...].astype(MXU_DTYPE)
            for h in range(2):
                v_mxu[h][...] = jnp.where(full_masks[h], v_ref[...], 0.0).astype(MXU_DTYPE)

        qb = q_ref[...]
        masks = _head_masks(tq)
        row_masks = _head_masks(1)
        qh = [(jnp.where(masks[h], qb, 0.0) if shared_k else qb[:, h * LANES:(h + 1) * LANES]).astype(MXU_DTYPE) for h in range(2)]
        qpos = i * tq + lax.broadcasted_iota(jnp.int32, (tq, tk), 0)
        if has_sink:
            m_init = [jnp.zeros((tq, 1), F32) + _head_row(s_ref[...], row_masks[h]) for h in range(2)]
            l_init = [jnp.ones((tq, 1), F32)] * 2
        else:
            m_init = [jnp.full((tq, 1), NEG, F32)] * 2
            l_init = [jnp.zeros((tq, 1), F32)] * 2

        def make_step(masked):
            def step(j, carry):
                m0, l0, m1, l1, acc = carry
                start = pl.multiple_of(j * kstride, kstride)
                kb = k_mxu[pl.ds(start, tk), :]
                if masked:
                    valid = _attn_valid(qpos, start + lax.broadcasted_iota(jnp.int32, (tq, tk), 1), window)
                new, alphas, pv = [], [], []
                for h, (m, l) in enumerate(((m0, l0), (m1, l1))):
                    kh = kb if shared_k else kb[:, h * LANES:(h + 1) * LANES]
                    s = _dot(qh[h], kh, NT) * scale
                    if masked:
                        s = jnp.where(valid, s, NEG)
                    m_new = jnp.maximum(m, jnp.max(s, axis=1, keepdims=True))
                    alpha = jnp.exp(m - m_new)
                    p = jnp.exp(s - m_new)
                    new += [m_new, alpha * l + jnp.sum(p, axis=1, keepdims=True)]
                    alphas.append(alpha)
                    pv.append(_dot(p, v_mxu[h][pl.ds(start, tk), :], NN))
                acc = acc * jnp.where(masks[0], alphas[0], alphas[1]) + pv[0] + pv[1]
                return new[0], new[1], new[2], new[3], acc
            return step

        carry = (m_init[0], l_init[0], m_init[1], l_init[1], jnp.zeros((tq, LANES), F32))
        last = (i * tq + tq - 1) // tk
        if window is None:
            n_full = (i * tq + 1) // tk
            carry = lax.fori_loop(0, n_full, make_step(False), carry)
            carry = lax.fori_loop(n_full, last + 1, make_step(True), carry)
        elif one_step:
            carry = make_step(True)(jnp.maximum(i - 1, 0), carry)
        else:
            carry = lax.fori_loop(jnp.maximum(i * tq - (window - 1), 0) // tk, last + 1, make_step(True), carry)
        m0, l0, m1, l1, acc = carry
        o_ref[...] = acc / jnp.where(masks[0], l0, l1)
        lse_ref[...] = jnp.where(masks[0], m0 + jnp.log(l0), m1 + jnp.log(l1))

    in_specs = [pl.BlockSpec((tq, qw), lambda p, i: (i, p)), pl.BlockSpec((t, qw), lambda p, i: (0, p)),
                pl.BlockSpec((t, LANES), lambda p, i: (0, p))]
    args = [q, k, v]
    if has_sink:
        in_specs.append(pl.BlockSpec((1, LANES), lambda p, i: (0, p)))
        args.append(sink)
    blk_o = pl.BlockSpec((tq, LANES), lambda p, i: (i, p))
    return pl.pallas_call(
        body, name=name, grid=(pairs, t // tq), in_specs=in_specs, out_specs=[blk_o, blk_o],
        out_shape=[jax.ShapeDtypeStruct((t, pairs * LANES), F32)] * 2,
        scratch_shapes=[pltpu.VMEM((t, qw), MXU_DTYPE), pltpu.VMEM((t, LANES), MXU_DTYPE), pltpu.VMEM((t, LANES), MXU_DTYPE)],
        compiler_params=_params(("arbitrary", "arbitrary")),
    )(*args)


def _flash_bwd_call(q, k, v, sink, o, lse, do, *, window, shared_k, scale, blk, blk_q, name):
    t = q.shape[0]
    qw = LANES if shared_k else 2 * LANES
    pairs = v.shape[1] // LANES
    tk = min(blk, t)
    tq = min(blk_q, t)
    assert tk % tq == 0 or tq % tk == 0
    nq = t // tq
    one_step = window is not None and tq == 2 * tk and window <= tk
    qstride = tk if one_step else tq
    has_sink = sink is not None

    def body(*refs):
        if has_sink:
            q_ref, k_ref, v_ref, o_ref, lse_ref, do_ref, s_ref, dq_ref, dk_ref, dv_ref, ds_ref = refs[:11]
        else:
            q_ref, k_ref, v_ref, o_ref, lse_ref, do_ref, dq_ref, dk_ref, dv_ref = refs[:9]
        q_mxu, do_mxu, lse_h, dsum_h = refs[-8:-6], refs[-6:-4], refs[-4:-2], refs[-2:]
        j = pl.program_id(1)
        masks = _head_masks(tq)
        row_masks = _head_masks(1)

        @pl.when(j == 0)
        def _():
            dq_ref[...] = jnp.zeros_like(dq_ref)
            full_masks = _head_masks(t)
            prod = do_ref[...] * o_ref[...]
            parts = []
            for h in range(2):
                qh = jnp.where(full_masks[h], q_ref[...], 0.0) if shared_k else q_ref[:, h * LANES:(h + 1) * LANES]
                q_mxu[h][...] = qh.astype(MXU_DTYPE)
                do_mxu[h][...] = jnp.where(full_masks[h], do_ref[...], 0.0).astype(MXU_DTYPE)
                dsum = jnp.sum(jnp.where(full_masks[h], prod, 0.0), axis=1, keepdims=True)
                lse = _head_row(lse_ref[...], full_masks[h])
                dsum_h[h][...] = jnp.zeros((t, LANES), F32) + dsum
                lse_h[h][...] = jnp.zeros((t, LANES), F32) + lse
                if has_sink:
                    ps = jnp.exp(_head_row(s_ref[...], row_masks[h]) - lse)
                    parts.append(-jnp.sum(ps * dsum, axis=0, keepdims=True))
            if has_sink:
                ds_ref[...] = jnp.zeros((SUBLANES, LANES), F32) + jnp.where(row_masks[0], parts[0], parts[1])

        kb = k_ref[...].astype(MXU_DTYPE)
        vb = v_ref[...].astype(MXU_DTYPE)
        kh = [kb if shared_k else kb[:, h * LANES:(h + 1) * LANES] for h in range(2)]
        kpos = j * tk + lax.broadcasted_iota(jnp.int32, (tq, tk), 1)
        lanes_of = lambda a: a if tk == LANES else jnp.concatenate([a] * (tk // LANES), axis=1)

        def make_step(masked):
            def step(i, carry):
                dk0, dk1, dv = carry
                start = pl.multiple_of(i * qstride, qstride)
                if masked:
                    valid = _attn_valid(start + lax.broadcasted_iota(jnp.int32, (tq, tk), 0), kpos, window)
                dks, dqs = [], []
                for h in range(2):
                    qh = q_mxu[h][pl.ds(start, tq), :]
                    doh = do_mxu[h][pl.ds(start, tq), :]
                    s = _dot(qh, kh[h], NT) * scale
                    if masked:
                        s = jnp.where(valid, s, NEG)
                    p = jnp.exp(s - lanes_of(lse_h[h][pl.ds(start, tq), :]))
                    dp = _dot(doh, vb, NT)
                    dsc = p * (dp - lanes_of(dsum_h[h][pl.ds(start, tq), :])) * scale
                    dv = dv + _dot(p, doh, TN)
                    dks.append(_dot(dsc, qh, TN))
                    dq_h = _dot(dsc, kh[h], NN)
                    dqs.append(jnp.where(masks[h], dq_h, 0.0) if shared_k else dq_h)
                if shared_k:
                    dq_ref[pl.ds(start, tq), :] += dqs[0] + dqs[1]
                else:
                    dq_ref[pl.ds(start, tq), :] += jnp.concatenate(dqs, axis=1)
                return dk0 + dks[0], dk1 + dks[1], dv
            return step

        zero = jnp.zeros((tk, LANES), F32)
        carry = (zero, zero, zero)
        first = (j * tk) // tq
        if window is None:
            n_full = jnp.minimum(((j + 1) * tk + tq - 2) // tq, nq)
            carry = lax.fori_loop(first, n_full, make_step(True), carry)
            carry = lax.fori_loop(n_full, nq, make_step(False), carry)
        elif one_step:
            carry = make_step(True)(jnp.minimum(j, t // tk - 2), carry)
        else:
            carry = lax.fori_loop(first, jnp.minimum(nq, (j * tk + tk - 1 + window - 1) // tq + 1), make_step(True), carry)
        dk0, dk1, dv = carry
        dk_ref[...] = dk0 + dk1 if shared_k else jnp.concatenate([dk0, dk1], axis=1)
        dv_ref[...] = dv

    full = lambda w: pl.BlockSpec((t, w), lambda p, j: (0, p))
    blkspec = lambda w: pl.BlockSpec((tk, w), lambda p, j: (j, p))
    in_specs = [full(qw), blkspec(qw), blkspec(LANES), full(LANES), full(LANES), full(LANES)]
    args = [q, k, v, o, lse, do]
    out_specs = [full(qw), blkspec(qw), blkspec(LANES)]
    out_shape = [jax.ShapeDtypeStruct(q.shape, F32), jax.ShapeDtypeStruct(k.shape, F32), jax.ShapeDtypeStruct(v.shape, F32)]
    if has_sink:
        in_specs.append(pl.BlockSpec((1, LANES), lambda p, j: (0, p)))
        args.append(sink)
        out_specs.append(pl.BlockSpec((SUBLANES, LANES), lambda p, j: (0, p)))
        out_shape.append(jax.ShapeDtypeStruct((SUBLANES, pairs * LANES), F32))
    return pl.pallas_call(
        body, name=name, grid=(pairs, t // tk), in_specs=in_specs, out_specs=out_specs, out_shape=out_shape,
        scratch_shapes=[pltpu.VMEM((t, LANES), MXU_DTYPE)] * 4 + [pltpu.VMEM((t, LANES), F32)] * 4,
        compiler_params=_params(("arbitrary", "arbitrary")),
    )(*args)


_MLA_CFG = dict(window=None, shared_k=False, scale=MLA_SCALE, blk=256)
_SWA_CFG = dict(window=WINDOW, shared_k=True, scale=SWA_SCALE, blk=128)
_MLA_FWD_CFG = dict(_MLA_CFG, blk=512, blk_q=512)
_SWA_FWD_CFG = dict(_SWA_CFG, blk=512, blk_q=256)
_MLA_BWD_CFG = dict(_MLA_CFG, blk=512, blk_q=512)
_SWA_BWD_CFG = dict(_SWA_CFG, blk=256, blk_q=512)


@jax.custom_vjp
def op_mla_attn(q, k, v):
    return _flash_fwd_call(q, k, v, None, name="mla_fwd", **_MLA_FWD_CFG)[0]


def _op_mla_attn_fwd(q, k, v):
    o, lse = _flash_fwd_call(q, k, v, None, name="mla_fwd", **_MLA_FWD_CFG)
    return o, (q, k, v, o, lse)


def _op_mla_attn_bwd(res, do):
    q, k, v, o, lse = res
    return tuple(_flash_bwd_call(q, k, v, None, o, lse, do, name="mla_bwd", **_MLA_BWD_CFG))


op_mla_attn.defvjp(_op_mla_attn_fwd, _op_mla_attn_bwd)


@jax.custom_vjp
def op_swa_attn(q, k, v, sink):
    return _flash_fwd_call(q, k, v, sink, name="swa_fwd", **_SWA_FWD_CFG)[0]


def _op_swa_attn_fwd(q, k, v, sink):
    o, lse = _flash_fwd_call(q, k, v, sink, name="swa_fwd", **_SWA_FWD_CFG)
    return o, (q, k, v, sink, o, lse)


def _op_swa_attn_bwd(res, do):
    q, k, v, sink, o, lse = res
    dq, dk, dv, dsink = _flash_bwd_call(q, k, v, sink, o, lse, do, name="swa_bwd", **_SWA_BWD_CFG)
    first_lane = lax.broadcasted_iota(jnp.int32, (1, dsink.shape[1]), 1) % 64 == 0
    return dq, dk, dv, jnp.where(first_lane, dsink[:1], 0.0)


op_swa_attn.defvjp(_op_swa_attn_fwd, _op_swa_attn_bwd)


def _complex_power(ar, ai, n):
    for _ in range(int(math.log2(n))):
        ar, ai = ar * ar - ai * ai, 2.0 * ar * ai
    return ar, ai


def _scan_passes(load_b, a1r, a1i, n, store, e_ref, c_ref, reverse):
    cb = a1r.shape[1]
    ar = jnp.zeros((SCAN_SEGMENTS, cb), F32) + a1r
    ai = jnp.zeros((SCAN_SEGMENTS, cb), F32) + a1i
    a2r, a2i = ar * ar - ai * ai, 2.0 * ar * ai
    idx = (lambda k: n - 1 - k) if reverse else (lambda k: k)
    mac = lambda pr_, pi__, h, b: (pr_ * h[0] - pi__ * h[1] + b[0], pr_ * h[1] + pi__ * h[0] + b[1])

    def load_pair(ii):
        k = jnp.minimum(2 * ii, n - 2)
        b0, b1 = load_b(idx(k)), load_b(idx(k + 1))
        return b0, mac(ar, ai, b0, b1)

    def local(ii, carry):
        h, c = carry
        return mac(a2r, a2i, h, c), load_pair(ii + 1)[1]

    zero = jnp.zeros((SCAN_SEGMENTS, cb), F32)
    (er, ei), _ = lax.fori_loop(0, n // 2, local, ((zero, zero), load_pair(0)[1]))
    e_ref[:, 0:cb] = er
    e_ref[:, cb:] = ei
    pr, pi_ = _complex_power(a1r, a1i, n)
    cr = jnp.zeros((1, cb), F32)
    ci = jnp.zeros((1, cb), F32)
    order = range(SCAN_SEGMENTS - 1, -1, -1) if reverse else range(SCAN_SEGMENTS)
    for s in order:
        c_ref[s:s + 1, 0:cb] = cr
        c_ref[s:s + 1, cb:] = ci
        er1, ei1 = e_ref[s:s + 1, 0:cb], e_ref[s:s + 1, cb:]
        cr, ci = pr * cr - pi_ * ci + er1, pr * ci + pi_ * cr + ei1

    def second(ii, carry):
        h, b0, c = carry
        h0 = mac(ar, ai, h, b0)
        h1 = mac(a2r, a2i, h, c)
        store(idx(2 * ii), *h0)
        store(idx(2 * ii + 1), *h1)
        nb0, nc = load_pair(ii + 1)
        return h1, nb0, nc

    b0, c0 = load_pair(0)
    lax.fori_loop(0, n // 2, second, ((c_ref[:, 0:cb], c_ref[:, cb:]), b0, c0))


def _scan_fwd_call(bu, lam):
    n = bu.shape[0]
    cb = SCAN_CB
    blk3 = pl.BlockSpec((n, SCAN_SEGMENTS, 2 * cb), lambda c: (0, 0, c))
    blk2 = lambda r: pl.BlockSpec((r, 2 * cb), lambda c: (0, c))

    def body(b_ref, lam_ref, h_ref, cin_ref, e_ref):
        def store(i, hr, hi):
            h_ref[i, :, 0:cb] = hr
            h_ref[i, :, cb:] = hi

        _scan_passes(lambda i: (b_ref[i, :, 0:cb], b_ref[i, :, cb:]), lam_ref[:, 0:cb], lam_ref[:, cb:], n, store,
                     e_ref, cin_ref, False)

    return pl.pallas_call(
        body, name="scan_fwd", grid=(SSM_CH // cb,), in_specs=[blk3, blk2(1)], out_specs=[blk3, blk2(SCAN_SEGMENTS)],
        out_shape=[jax.ShapeDtypeStruct(bu.shape, F32), jax.ShapeDtypeStruct((SCAN_SEGMENTS, 2 * SSM_CH), F32)],
        scratch_shapes=[pltpu.VMEM((SCAN_SEGMENTS, 2 * cb), F32)],
        compiler_params=_params(("parallel",)),
    )(bu, lam)


def _scan_bwd_call(dh, h, cin, lam):
    n = dh.shape[0]
    cb = SCAN_CB
    blk3 = pl.BlockSpec((n, SCAN_SEGMENTS, 2 * cb), lambda c: (0, 0, c))
    blk2 = lambda r: pl.BlockSpec((r, 2 * cb), lambda c: (0, c))

    def body(d_ref, h_ref, cin_ref, lam_ref, g_ref, dlam_ref, e_ref, c_ref, acc_ref):
        acc_ref[...] = jnp.zeros_like(acc_ref)

        def store(i, gr, gi):
            g_ref[i, :, 0:cb] = gr
            g_ref[i, :, cb:] = gi
            ip = jnp.maximum(i - 1, 0)
            hpr = jnp.where(i > 0, h_ref[ip, :, 0:cb], cin_ref[:, 0:cb])
            hpi = jnp.where(i > 0, h_ref[ip, :, cb:], cin_ref[:, cb:])
            acc_ref[:, 0:cb] += gr * hpr + gi * hpi
            acc_ref[:, cb:] += gi * hpr - gr * hpi

        _scan_passes(lambda i: (d_ref[i, :, 0:cb], d_ref[i, :, cb:]), lam_ref[:, 0:cb], -lam_ref[:, cb:], n, store,
                     e_ref, c_ref, True)
        dlam_ref[...] = acc_ref[...]

    return pl.pallas_call(
        body, name="scan_bwd", grid=(SSM_CH // cb,), in_specs=[blk3, blk3, blk2(SCAN_SEGMENTS), blk2(1)],
        out_specs=[blk3, blk2(SCAN_SEGMENTS)],
        out_shape=[jax.ShapeDtypeStruct(dh.shape, F32), jax.ShapeDtypeStruct((SCAN_SEGMENTS, 2 * SSM_CH), F32)],
        scratch_shapes=[pltpu.VMEM((SCAN_SEGMENTS, 2 * cb), F32)] * 3,
        compiler_params=_params(("parallel",), 6 * n * SCAN_SEGMENTS * 2 * cb * 4 + 4 * 1024 * 1024),
    )(dh, h, cin, lam)


@jax.custom_vjp
def op_scan(bu, lam):
    return _scan_fwd_call(bu, lam)[0]


def _op_scan_fwd(bu, lam):
    h, cin = _scan_fwd_call(bu, lam)
    return h, (h, cin, lam)


def _op_scan_bwd(res, dh):
    h, cin, lam = res
    g, dlam = _scan_bwd_call(dh, h, cin, lam)
    return g, jnp.sum(dlam, axis=0, keepdims=True)


op_scan.defvjp(_op_scan_fwd, _op_scan_bwd)


def _loss_call(y, target):
    t, d = y.shape
    tile = min(ROW_TILE, t)

    def body(y_ref, t_ref, dy_ref, acc_ref):
        @pl.when(pl.program_id(0) == 0)
        def _():
            acc_ref[...] = jnp.zeros_like(acc_ref)

        err = y_ref[...] - t_ref[...]
        dy_ref[...] = err * (1.0 / d)
        col = jnp.sum(err * err, axis=0, keepdims=True)
        part = col[:, 0:LANES]
        for c in range(1, d // LANES):
            part = part + col[:, c * LANES:(c + 1) * LANES]
        acc_ref[0:1, :] += part

    blk = pl.BlockSpec((tile, d), lambda i: (i, 0))
    dy, acc = pl.pallas_call(
        body, name="loss_head", grid=(t // tile,), in_specs=[blk, blk],
        out_specs=[blk, pl.BlockSpec((SUBLANES, LANES), lambda i: (0, 0))],
        out_shape=[jax.ShapeDtypeStruct((t, d), F32), jax.ShapeDtypeStruct((SUBLANES, LANES), F32)],
        compiler_params=_params(("arbitrary",)),
    )(y, target)
    return jnp.sum(acc) * (0.5 / d), dy


def _rot_cols(w, xp=jnp):
    return xp.concatenate([-w[:, 16:], w[:, :16]], axis=1)


def _ext_w_in(w, xp=jnp):
    a_val, a_gate, a_z, c_q, c_kv, k_r, b_z, u, c_z, q, k, v, d_z = xp.split(
        w, (256, 512, 768, 1024, 1152, 1184, 1440, 1696, 1952, 2208, 2336, 2464), axis=1)
    dup = lambda m: xp.concatenate([m[:, :64], m[:, :64], m[:, 64:], m[:, 64:]], axis=1)
    krblk = xp.concatenate([xp.zeros((w.shape[0], 64), w.dtype), k_r, _rot_cols(k_r, xp)], axis=1)
    return xp.concatenate([a_val, a_gate, a_z, c_q, b_z, u, c_z, q, dup(k), dup(v), d_z, c_kv, krblk], axis=1)


IN_WIDTH = 2720
IN_SHARD = IN_WIDTH // 4
IN_SHARD_PAD = 768
IN_EXT = 3072


BAND = 512


def _w_in_layout():
    src = _ext_w_in(np.arange(1, IN_WIDTH + 1, dtype=np.float32)[None, :], np)[0]
    col = np.abs(src).astype(np.int64) - 1
    row = np.where(col >= 0, (col // IN_SHARD) * IN_SHARD_PAD + col % IN_SHARD, -1)
    return row, np.sign(src)


def _w_in_layout_matrix():
    row, sign = _w_in_layout()
    rows = lax.broadcasted_iota(jnp.int32, (4 * IN_SHARD_PAD, IN_EXT), 0)
    return jnp.where(rows == jnp.asarray(row, jnp.int32)[None, :], jnp.asarray(sign, F32)[None, :], 0.0).astype(MXU_DTYPE)


def _band_tables():
    row, _ = _w_in_layout()
    nb = IN_EXT // BAND
    hit = np.zeros((nb, nb), bool)
    for c, r in enumerate(row):
        if r >= 0:
            hit[r // BAND, c // BAND] = True

    def table(h):
        depth = int(h.sum(axis=1).max())
        rows = []
        for o in range(nb):
            used = [int(b) for b in np.nonzero(h[o])[0]]
            spare = [b for b in range(nb) if not h[o, b]]
            rows.append(used + spare[:depth - len(used)])
        return np.asarray(rows, np.int32), depth

    return table(hit.T), table(hit)


def _band_mm_call(a, e, table, depth, e_transposed, name, out_dtype):
    m = a.shape[0]
    nb = IN_EXT // BAND
    dims = NT if e_transposed else NN

    def body(t_ref, a_ref, e_ref, o_ref, acc_ref):
        kk = pl.program_id(1)

        @pl.when(kk == 0)
        def _():
            acc_ref[...] = jnp.zeros_like(acc_ref)

        acc_ref[...] += _dot(a_ref[...], e_ref[...], dims)

        @pl.when(kk == depth - 1)
        def _():
            o_ref[...] = acc_ref[...].astype(out_dtype)

    blk = lambda o, kk, t: t[o * depth + kk]
    e_spec = pl.BlockSpec((BAND, BAND), (lambda o, kk, t: (o, blk(o, kk, t))) if e_transposed else (lambda o, kk, t: (blk(o, kk, t), o)))
    return pl.pallas_call(
        body, name=name, out_shape=jax.ShapeDtypeStruct((m, IN_EXT), out_dtype),
        grid_spec=pltpu.PrefetchScalarGridSpec(
            num_scalar_prefetch=1, grid=(nb, depth),
            in_specs=[pl.BlockSpec((m, BAND), lambda o, kk, t: (0, blk(o, kk, t))), e_spec],
            out_specs=pl.BlockSpec((m, BAND), lambda o, kk, t: (0, o)),
            scratch_shapes=[pltpu.VMEM((m, BAND), F32)]),
        compiler_params=_params(("parallel", "arbitrary")),
    )(jnp.asarray(table.reshape(-1)), a, e)


@jax.custom_vjp
def op_w_in_ext(w_pad, e):
    (table, depth), _ = _band_tables()
    return _band_mm_call(w_pad, e, table, depth, False, "w_in_ext", F32)


def _op_w_in_ext_fwd(w_pad, e):
    return op_w_in_ext(w_pad, e), (e, jnp.zeros((0,), w_pad.dtype))


def _op_w_in_ext_bwd(res, g):
    e, w_like = res
    _, (table, depth) = _band_tables()
    return _band_mm_call(g, e, table, depth, True, "w_in_ext_bwd", w_like.dtype), jnp.zeros_like(e)


op_w_in_ext.defvjp(_op_w_in_ext_fwd, _op_w_in_ext_bwd)


H_COLS = dict(a_val=256, a_gate=256, a_z=256, c_q=256, b_z=256, u=256, c_z=256, q=256, kdup=256, vdup=256, d_z=256,
              c_kv=128, krblk=128)
op_in_proj = make_proj(tuple(H_COLS.values()), "in_proj")


def _ext_mla(w_uq, w_ukv):
    zeros = jnp.zeros((w_ukv.shape[0], 64), w_ukv.dtype)
    uq, uk, uv = [], [], []
    for h in range(4):
        nope, rp = w_uq[:, 96 * h:96 * h + 64], w_uq[:, 96 * h + 64:96 * h + 96]
        uq += [nope, rp, _rot_cols(rp)]
        uk += [w_ukv[:, 128 * h:128 * h + 64], zeros]
        uv.append(w_ukv[:, 128 * h + 64:128 * h + 128])
    return jnp.concatenate(uq, axis=1), jnp.concatenate(uk, axis=1), jnp.concatenate(uv, axis=1)


def _scan_cols(re, im):
    parts = []
    for c in range(SSM_CH // SCAN_CB):
        parts += [re[..., c * SCAN_CB:(c + 1) * SCAN_CB], im[..., c * SCAN_CB:(c + 1) * SCAN_CB]]
    return jnp.concatenate(parts, axis=-1)


def _ext_ssm(a_re, a_im, log_dt, b_re, b_im, c_re, c_im):
    dt = jnp.exp(log_dt)[:, None]
    mag = jnp.exp(a_re * dt)
    lb_re, lb_im = mag * jnp.cos(a_im * dt), mag * jnp.sin(a_im * dt)
    den = a_re * a_re + a_im * a_im
    nr, ni = lb_re - 1.0, lb_im
    f_re = ((nr * a_re + ni * a_im) / den)[..., None]
    f_im = ((ni * a_re - nr * a_im) / den)[..., None]
    bb_re = f_re * b_re - f_im * b_im
    bb_im = f_re * b_im + f_im * b_re
    eye = jnp.eye(SSM_GROUPS, dtype=F32)
    spread = lambda a: a.transpose(0, 2, 1)[:, :, None, :] * eye[:, None, :, None]
    bd_in = lambda bb: spread(bb).reshape(SSM_GROUPS * SSM_GROUP, SSM_CH)
    bd_out = lambda cc: spread(cc).reshape(SSM_CH, SSM_GROUPS * SSM_GROUP)
    w_bu = _scan_cols(bd_in(bb_re), bd_in(bb_im))
    w_y = _scan_cols(bd_out(c_re).T, -bd_out(c_im).T).T
    lam = _scan_cols(lb_re.reshape(1, SSM_CH), lb_im.reshape(1, SSM_CH))
    return w_bu, w_y, lam


def _rope_tables(t):
    pos = jnp.arange(t, dtype=F32)
    inv_freq = ROPE_THETA ** (-jnp.arange(0, 32, 2, dtype=F32) / 32)
    ang = pos[:, None] * inv_freq[None, :]
    cos, sin = jnp.cos(ang), jnp.sin(ang)
    ones, z32, z64 = jnp.ones((t, 64), F32), jnp.zeros((t, 32), F32), jnp.zeros((t, 64), F32)
    cos1 = jnp.concatenate([ones, cos, cos, z32], axis=1)
    sin1 = jnp.concatenate([z64, sin, sin, z32], axis=1)
    return jnp.concatenate([cos1] * 4, axis=1), jnp.concatenate([sin1] * 4, axis=1)


def _to_segments(a):
    t, w = a.shape
    return a.reshape(SCAN_SEGMENTS, t // SCAN_SEGMENTS, w).transpose(1, 0, 2)


def _from_segments(a):
    n, s, w = a.shape
    return a.transpose(1, 0, 2).reshape(n * s, w)


def _layer(x, p_i, cos4, sin4, e_mat, w):
    t = x.shape[0]
    row = lambda v: v.reshape(1, -1)
    f32 = lambda v: v.astype(F32)
    hs = dict(zip(H_COLS, op_in_proj(x, op_w_in_ext(w["w_in_pad"], e_mat))))

    cv = op_conv(hs["a_val"], hs["a_gate"], w["conv_w"], row(w["conv_b"]))
    (y_a,) = op_conv_post((cv, hs["a_z"]), (row(w["conv_norm_g"]), row(w["conv_norm_b"]), f32(w["w_pw2"])))

    w_uq, w_uk, w_uv = _ext_mla(w["w_uq"], f32(w["w_ukv"]))
    q, k, v = op_mla_prep((hs["c_q"], hs["c_kv"], hs["krblk"], cos4, sin4),
                          (row(w["mla_q_norm_g"]), row(w["mla_kv_norm_g"]), w_uq, w_uk, w_uv))
    o_b = op_mla_attn(q, k, v)

    w_bu, w_y, lam = _ext_ssm(w["ssm_a_re"], w["ssm_a_im"], w["ssm_log_dt"], w["ssm_b_re"], w["ssm_b_im"],
                              w["ssm_c_re"], w["ssm_c_im"])
    u_seg = _to_segments(hs["u"]).reshape(t, BRANCH_W)
    bu = op_mm(u_seg, w_bu).reshape(t // SCAN_SEGMENTS, SCAN_SEGMENTS, 2 * SSM_CH)
    hstate = op_scan(bu, lam).reshape(t, 2 * SSM_CH)
    y_ssm = _from_segments(op_mm(hstate, w_y).reshape(t // SCAN_SEGMENTS, SCAN_SEGMENTS, BRANCH_W))
    w_glu = f32(w["w_glu"])
    (y_c,) = op_ssm_post((y_ssm, hs["u"], hs["c_z"]), (row(w["ssm_d"]), w_glu[:, :BRANCH_W], w_glu[:, BRANCH_W:]))

    sink = jnp.repeat(w["attn_sinks"], 64).reshape(1, 2 * LANES)
    o_d = op_swa_attn(hs["q"], hs["kdup"], hs["vdup"], sink)

    merged = op_merge_block(x, (y_a, o_b, y_c, o_d), (None, hs["b_z"], None, hs["d_z"]), w["w_merge"], w["w_branch"], w["b_merge"])
    (x1,) = op_ln((x, op_mm(merged, w["w_out"])), (row(w["ln_g"]), row(w["ln_b"])))
    (out,) = op_ple((x1, op_mm(p_i, w["w_ple"]), op_mm(x1, w["w_ple_gate"])), (row(w["ple_norm_g"]),))
    return out


def _forward(x, p, layers):
    cos4, sin4 = _rope_tables(x.shape[0])
    e_mat = _w_in_layout_matrix()
    for i in range(DEPTH):
        x = _layer(x, p[i], cos4, sin4, e_mat, layers[i])
    return x


SHARD_AXIS = dict(w_in=2, w_merge=2, conv_w=2, w_pw2=1, w_uq=2, w_ukv=2, w_glu=2, w_branch=3, w_out=1, w_ple=2, w_ple_gate=1)
ODD = ("w_uq", "conv_w")
BIG = tuple(n for n in SHARD_AXIS if n not in ODD)
REPLICATED = ("b_merge", "conv_b", "conv_norm_g", "conv_norm_b", "mla_q_norm_g", "mla_kv_norm_g", "ssm_a_re", "ssm_a_im",
              "ssm_log_dt", "ssm_b_re", "ssm_b_im", "ssm_c_re", "ssm_c_im", "ssm_d", "attn_sinks", "ln_g", "ln_b", "ple_norm_g")
WEIGHTS = ("w_in", "w_merge", "b_merge", "conv_w", "conv_b", "conv_norm_g", "conv_norm_b", "w_pw2", "mla_q_norm_g",
           "mla_kv_norm_g", "w_uq", "w_ukv", "ssm_a_re", "ssm_a_im", "ssm_log_dt", "ssm_b_re", "ssm_b_im", "ssm_c_re",
           "ssm_c_im", "ssm_d", "w_glu", "attn_sinks", "w_branch", "w_out", "ln_g", "ln_b", "w_ple", "w_ple_gate", "ple_norm_g")
PACK_COLS = 1024
PACK_ROWS = 16
CHIP_FLIPS = ((1, 0), (0, 1), (1, 1))
N_CHIPS = 4
N_DEV = 8


def _pack_rows(n):
    return -(-n // (SUBLANES * PACK_COLS)) * SUBLANES


def _pack(arrays, dtype):
    blocks, rows = [], 0
    for a in arrays:
        r = _pack_rows(a.size)
        flat = a.reshape(-1).astype(dtype)
        blocks.append(jnp.pad(flat, (0, r * PACK_COLS - a.size)).reshape(r, PACK_COLS))
        rows += r
    pad = -rows % PACK_ROWS
    if pad:
        blocks.append(jnp.zeros((pad, PACK_COLS), dtype))
    return jnp.concatenate(blocks, axis=0)


def _unpack(buf, shapes):
    out, row = [], 0
    for s in shapes:
        n = math.prod(s)
        r = _pack_rows(n)
        out.append(buf[row:row + r].reshape(-1)[:n].reshape(s))
        row += r
    return out


def _flip(v, bit):
    return 1 - v if bit else v


def _window(ref, axis, start, size):
    idx = [slice(None)] * len(ref.shape)
    idx[axis] = pl.ds(start, size)
    return ref.at[tuple(idx)]


def _gather_chips(srcs, axes, stacked):
    units = []
    for k, (s, a) in enumerate(zip(srcs, axes)):
        if stacked[k]:
            units += [(k, l, s.shape[1:], a - 1) for l in range(s.shape[0])]
        else:
            units.append((k, None, s.shape, a))
    nu, nb = len(units), len(srcs)

    def body(*refs):
        ins, outs = refs[:nb], refs[nb:nb + nu]
        ici_send, ici_recv, d2d_send, d2d_recv, local_sems = refs[nb + nu:]
        x, y, c = lax.axis_index("x"), lax.axis_index("y"), lax.axis_index("c")
        me = 2 * x + y

        def mine(u, half=None):
            k, l, shape, _ = units[u]
            ref = ins[k] if l is None else ins[k].at[l]
            return ref if half is None else ref.at[pl.ds(half * (shape[0] // 2), shape[0] // 2)]

        def place(u, chip, half=None):
            _, _, shape, a = units[u]
            size, rows = shape[a], shape[0] // 2
            if half is None:
                return _window(outs[u], a, chip * size, size)
            if a == 0:
                return outs[u].at[pl.ds(chip * size + half * rows, rows)]
            return _window(outs[u].at[pl.ds(half * rows, rows)], a, chip * size, size)

        local = [pltpu.make_async_copy(mine(u), place(u, me), local_sems.at[u]) for u in range(nu)]
        for cp in local:
            cp.start()
        sends = []
        for j, (bx, by) in enumerate(CHIP_FLIPS):
            for u in range(nu):
                cp = pltpu.make_async_remote_copy(src_ref=mine(u, c), dst_ref=place(u, me, c),
                                                  send_sem=ici_send.at[j * nu + u], recv_sem=ici_recv.at[j * nu + u],
                                                  device_id=(_flip(x, bx), _flip(y, by), c), device_id_type=MESH)
                cp.start()
                sends.append(cp)
        for j, (bx, by) in enumerate(CHIP_FLIPS):
            src = 2 * _flip(x, bx) + _flip(y, by)
            for u in range(nu):
                got = place(u, src, c)
                pltpu.make_async_remote_copy(src_ref=got, dst_ref=got, send_sem=ici_send.at[j * nu + u],
                                             recv_sem=ici_recv.at[j * nu + u], device_id=(x, y, c), device_id_type=MESH).wait_recv()
                cp = pltpu.make_async_remote_copy(src_ref=got, dst_ref=got, send_sem=d2d_send.at[j * nu + u],
                                                  recv_sem=d2d_recv.at[j * nu + u], device_id=(x, y, 1 - c), device_id_type=MESH)
                cp.start()
                sends.append(cp)
        for j, (bx, by) in enumerate(CHIP_FLIPS):
            src = 2 * _flip(x, bx) + _flip(y, by)
            for u in range(nu):
                other = place(u, src, 1 - c)
                pltpu.make_async_remote_copy(src_ref=other, dst_ref=other, send_sem=d2d_send.at[j * nu + u],
                                             recv_sem=d2d_recv.at[j * nu + u], device_id=(x, y, c), device_id_type=MESH).wait_recv()
        for cp in sends:
            cp.wait_send()
        for cp in local:
            cp.wait()

    full = lambda shape, a: tuple(N_CHIPS * d if i == a else d for i, d in enumerate(shape))
    res = pl.pallas_call(
        body, name="gather_weights", in_specs=[ANY] * nb, out_specs=[ANY] * nu,
        out_shape=[jax.ShapeDtypeStruct(full(shape, a), srcs[k].dtype) for k, _, shape, a in units],
        scratch_shapes=[pltpu.SemaphoreType.DMA((3 * nu,))] * 4 + [pltpu.SemaphoreType.DMA((nu,))],
    )(*srcs)
    out, it = [], iter(res)
    for k in range(nb):
        out.append([next(it) for _ in range(srcs[k].shape[0])] if stacked[k] else next(it))
    return out


def _exchange_grads(grads, axes, smalls):
    nt, ns = len(grads), len(smalls)
    sizes = [g[0].shape[a] // N_CHIPS for g, a in zip(grads, axes)]
    dev_flips = [(bx, by, bc) for bx in (0, 1) for by in (0, 1) for bc in (0, 1)][1:]
    n_remote = 3 * nt * DEPTH + 7 * ns
    n_local = nt * DEPTH + ns

    def body(*refs):
        g_refs = [refs[k * DEPTH:(k + 1) * DEPTH] for k in range(nt)]
        s_refs = refs[nt * DEPTH:nt * DEPTH + ns]
        outs = refs[nt * DEPTH + ns:nt * DEPTH + ns + nt + ns]
        recv_refs, all_refs = outs[:nt], outs[nt:]
        send_sems, recv_sems, local_sems = refs[-3:]
        x, y, c = lax.axis_index("x"), lax.axis_index("y"), lax.axis_index("c")
        me_chip = 2 * x + y
        me = 4 * x + 2 * y + c
        part = lambda k, i, chip: _window(g_refs[k][i], axes[k], chip * sizes[k], sizes[k])
        started, waits = [], []
        sem, lsem = 0, 0
        for k in range(nt):
            for i in range(DEPTH):
                cp = pltpu.make_async_copy(part(k, i, me_chip), recv_refs[k].at[i, 3], local_sems.at[lsem])
                cp.start()
                started.append(cp.wait)
                lsem += 1
                for j, (bx, by) in enumerate(CHIP_FLIPS):
                    px, py = _flip(x, bx), _flip(y, by)
                    cp = pltpu.make_async_remote_copy(src_ref=part(k, i, 2 * px + py), dst_ref=recv_refs[k].at[i, j],
                                                      send_sem=send_sems.at[sem], recv_sem=recv_sems.at[sem],
                                                      device_id=(px, py, c), device_id_type=MESH)
                    cp.start()
                    started.append(cp.wait_send)
                    waits.append(cp.wait_recv)
                    sem += 1
        for s in range(ns):
            cp = pltpu.make_async_copy(s_refs[s], all_refs[s].at[me], local_sems.at[lsem])
            cp.start()
            started.append(cp.wait)
            lsem += 1
            for bx, by, bc in dev_flips:
                peer = (_flip(x, bx), _flip(y, by), _flip(c, bc))
                cp = pltpu.make_async_remote_copy(src_ref=s_refs[s], dst_ref=all_refs[s].at[me], send_sem=send_sems.at[sem],
                                                  recv_sem=recv_sems.at[sem], device_id=peer, device_id_type=MESH)
                cp.start()
                started.append(cp.wait_send)
                src = 4 * peer[0] + 2 * peer[1] + peer[2]
                waits.append(pltpu.make_async_remote_copy(src_ref=s_refs[s], dst_ref=all_refs[s].at[src], send_sem=send_sems.at[sem],
                                                          recv_sem=recv_sems.at[sem], device_id=peer, device_id_type=MESH).wait_recv)
                sem += 1
        for w in waits + started:
            w()

    shard = lambda g, a: tuple(d // N_CHIPS if i == a else d for i, d in enumerate(g.shape))
    flat = [g for per_layer in grads for g in per_layer]
    return pl.pallas_call(
        body, name="exchange_grads", in_specs=[ANY] * (len(flat) + ns), out_specs=[ANY] * (nt + ns),
        out_shape=[jax.ShapeDtypeStruct((DEPTH, N_CHIPS, *shard(g[0], a)), g[0].dtype) for g, a in zip(grads, axes)]
        + [jax.ShapeDtypeStruct((N_DEV, *s.shape), s.dtype) for s in smalls],
        scratch_shapes=[pltpu.SemaphoreType.DMA((n_remote,)), pltpu.SemaphoreType.DMA((n_remote,)), pltpu.SemaphoreType.DMA((n_local,))],
    )(*flat, *smalls)


def _swap_cores(parts):
    nb = len(parts)

    def body(*refs):
        ins, outs, send_sems, recv_sems = refs[:nb], refs[nb:2 * nb], refs[-2], refs[-1]
        x, y, c = lax.axis_index("x"), lax.axis_index("y"), lax.axis_index("c")
        cps = [pltpu.make_async_remote_copy(src_ref=ins[k], dst_ref=outs[k], send_sem=send_sems.at[k], recv_sem=recv_sems.at[k],
                                            device_id=(x, y, 1 - c), device_id_type=MESH) for k in range(nb)]
        for cp in cps:
            cp.start()
        for cp in cps:
            cp.wait()

    return pl.pallas_call(
        body, name="swap_cores", in_specs=[ANY] * nb, out_specs=[ANY] * nb,
        out_shape=[jax.ShapeDtypeStruct(q.shape, q.dtype) for q in parts],
        scratch_shapes=[pltpu.SemaphoreType.DMA((nb,)), pltpu.SemaphoreType.DMA((nb,))],
    )(*parts)


def _sum_chips_call(recv, cols, name):
    depth, _, r, c = recv.shape
    tile = _pick(r, (512, 256, 128, 64, 32, 16))

    def body(r_ref, o_ref):
        slot = lambda s: r_ref[s, :, pl.ds(0, cols)].astype(F32)
        o_ref[...] = ((slot(3) + slot(0)) + slot(1)) + slot(2)

    return pl.pallas_call(
        body, name=name, grid=(depth, r // tile),
        in_specs=[pl.BlockSpec((None, N_CHIPS, tile, c), lambda l, i: (l, 0, i, 0))],
        out_specs=pl.BlockSpec((None, tile, cols), lambda l, i: (l, i, 0)), out_shape=jax.ShapeDtypeStruct((depth, r, cols), F32),
        compiler_params=_params(("parallel", "parallel")),
    )(recv)


def _sum_slots_call(slots, name):
    n, r, c = slots.shape
    tile = _pick(r, (512, 256, 128, 64, 32, 16, 8))

    def body(s_ref, o_ref):
        acc = s_ref[0]
        for s in range(1, n):
            acc = acc + s_ref[s]
        o_ref[...] = acc

    return pl.pallas_call(
        body, name=name, grid=(r // tile,), in_specs=[pl.BlockSpec((n, tile, c), lambda i: (0, i, 0))],
        out_specs=pl.BlockSpec((tile, c), lambda i: (i, 0)), out_shape=jax.ShapeDtypeStruct((r, c), F32),
        compiler_params=_params(("parallel",)),
    )(slots)


def _adamw_math(w, g, m, v):
    m = ADAM_B1 * m + (1.0 - ADAM_B1) * g
    v = ADAM_B2 * v + (1.0 - ADAM_B2) * (g * g)
    m_hat = m / (1.0 - ADAM_B1 ** ADAM_STEP)
    v_hat = v / (1.0 - ADAM_B2 ** ADAM_STEP)
    return -ADAM_LR * (m_hat / (jnp.sqrt(v_hat) + ADAM_EPS) + ADAM_WD * w), m, v


def _adamw_call(w, m, v, gparts, name):
    r, c = w.shape
    n = len(gparts)
    tile = _pick(r, (512, 256, 128, 64, 32, 16, 8))

    def body(w_ref, m_ref, v_ref, *refs):
        g_refs, (go_ref, d_ref, mo_ref, vo_ref) = refs[:n], refs[n:]
        g = g_refs[0][...]
        for g_ref in g_refs[1:]:
            g = g + g_ref[...]
        go_ref[...] = g
        d_ref[...], mo_ref[...], vo_ref[...] = _adamw_math(w_ref[...], g, m_ref[...], v_ref[...])

    blk = pl.BlockSpec((tile, c), lambda i: (i, 0))
    return pl.pallas_call(
        body, name=name, grid=(r // tile,), in_specs=[blk] * (3 + n),
        out_specs=[blk] * 4, out_shape=[jax.ShapeDtypeStruct((r, c), F32)] * 4,
        compiler_params=_params(("parallel",)),
    )(w, m, v, *gparts)


def _train_local(x, p, layers, target):
    y, vjp = jax.vjp(lambda x_, w_: _forward(x_, p, w_), x, layers)
    loss, dy = _loss_call(y, target)
    dx, dw = vjp(dy)
    return loss, dx, dw


def kernel(x, p, w_in, w_merge, b_merge, conv_w, conv_b, conv_norm_g, conv_norm_b, w_pw2, mla_q_norm_g, mla_kv_norm_g, w_uq, w_ukv, ssm_a_re, ssm_a_im, ssm_log_dt, ssm_b_re, ssm_b_im, ssm_c_re, ssm_c_im, ssm_d, w_glu, attn_sinks, w_branch, w_out, ln_g, ln_b, w_ple, w_ple_gate, ple_norm_g, loss_target, m_w_in, m_w_merge, m_b_merge, m_conv_w, m_conv_b, m_conv_norm_g, m_conv_norm_b, m_w_pw2, m_mla_q_norm_g, m_mla_kv_norm_g, m_w_uq, m_w_ukv, m_ssm_a_re, m_ssm_a_im, m_ssm_log_dt, m_ssm_b_re, m_ssm_b_im, m_ssm_c_re, m_ssm_c_im, m_ssm_d, m_w_glu, m_attn_sinks, m_w_branch, m_w_out, m_ln_g, m_ln_b, m_w_ple, m_w_ple_gate, m_ple_norm_g, v_w_in, v_w_merge, v_b_merge, v_conv_w, v_conv_b, v_conv_norm_g, v_conv_norm_b, v_w_pw2, v_mla_q_norm_g, v_mla_kv_norm_g, v_w_uq, v_w_ukv, v_ssm_a_re, v_ssm_a_im, v_ssm_log_dt, v_ssm_b_re, v_ssm_b_im, v_ssm_c_re, v_ssm_c_im, v_ssm_d, v_w_glu, v_attn_sinks, v_w_branch, v_w_out, v_ln_g, v_ln_b, v_w_ple, v_w_ple_gate, v_ple_norm_g):
    given = dict(locals())
    w_loc = {n: given[n] for n in WEIGHTS}
    m_loc = {n: given["m_" + n] for n in WEIGHTS}
    v_loc = {n: given["v_" + n] for n in WEIGHTS}

    me_chip = 2 * lax.axis_index("x") + lax.axis_index("y")

    wire = {n: w_loc[n].astype(MXU_DTYPE) for n in BIG}
    wire["w_in"] = jnp.pad(wire["w_in"], ((0, 0), (0, 0), (0, IN_SHARD_PAD - IN_SHARD)))
    odd_shapes = [w_loc[n].shape for n in ODD]
    gathered = _gather_chips([wire[n] for n in BIG] + [_pack([w_loc[n] for n in ODD], F32)], [SHARD_AXIS[n] for n in BIG] + [0],
                             [True] * len(BIG) + [False])
    full = dict(zip(BIG, gathered[:-1]))
    odd_parts = [_unpack(part, odd_shapes) for part in jnp.split(gathered[-1], N_CHIPS, axis=0)]
    for k, n in enumerate(ODD):
        full[n] = jnp.concatenate([odd_parts[s][k] for s in range(N_CHIPS)], axis=SHARD_AXIS[n])
    layers = []
    for i in range(DEPTH):
        layer = {n: (full[n][i] if n in full else w_loc[n][i]) for n in WEIGHTS if n != "w_in"}
        layer["w_in_pad"] = full["w_in"][i]
        layers.append(layer)

    loss, dx, dw = _train_local(x[0], p[:, 0], layers, loss_target[0])
    loss = lax.psum(loss, ("x", "y", "c"))

    key = lambda n: "w_in_pad" if n == "w_in" else n
    stacked = lambda n: jnp.stack([dw[i][n] for i in range(DEPTH)])
    small_rep = _pack([stacked(n) for n in REPLICATED], F32)
    small_odd = _pack([stacked(n) for n in ODD], F32)
    *recv, all_rep, all_odd = _exchange_grads([[dw[i][key(n)] for i in range(DEPTH)] for n in BIG],
                                              [SHARD_AXIS[n] - 1 for n in BIG], [small_rep, small_odd])
    parts = []
    for n, r in zip(BIG, recv):
        cols = w_loc[n].shape[-1]
        parts.append(_sum_chips_call(r.reshape(DEPTH, N_CHIPS, -1, r.shape[-1]), cols, "sum_chips_" + n))
    others = _swap_cores(parts)
    g_rep = _sum_slots_call(all_rep, "sum_replicated")
    g_odd = _unpack(_sum_slots_call(all_odd, "sum_odd"), [(DEPTH, *w_loc[n].shape[1:-1], N_CHIPS * w_loc[n].shape[-1]) for n in ODD])

    grads, deltas, new_m, new_v = {}, {}, {}, {}

    def adamw(n, gparts):
        shape = w_loc[n].shape
        two_d = lambda a: a.reshape(-1, shape[-1])
        res = _adamw_call(two_d(w_loc[n]), two_d(m_loc[n]), two_d(v_loc[n]), [two_d(g) for g in gparts], "adamw_" + n)
        grads[n], deltas[n], new_m[n], new_v[n] = [r.reshape(shape) for r in res]

    for n, part, other in zip(BIG, parts, others):
        adamw(n, [part, other])
    for n, g in zip(ODD, g_odd):
        size = w_loc[n].shape[-1]
        adamw(n, [lax.dynamic_slice_in_dim(g, me_chip * size, size, axis=g.ndim - 1)])
    rep_shapes = [w_loc[n].shape for n in REPLICATED]
    res = _adamw_call(_pack([w_loc[n] for n in REPLICATED], F32), _pack([m_loc[n] for n in REPLICATED], F32),
                      _pack([v_loc[n] for n in REPLICATED], F32), [g_rep], "adamw_replicated")
    for dst, buf in zip((grads, deltas, new_m, new_v), res):
        for n, a in zip(REPLICATED, _unpack(buf, rep_shapes)):
            dst[n] = a

    return (loss, dx[None], *[grads[n] for n in WEIGHTS], *[deltas[n] for n in WEIGHTS],
            *[new_m[n] for n in WEIGHTS], *[new_v[n] for n in WEIGHTS])
```

```python
import functools
import math

import jax
import jax.numpy as jnp
import numpy as np
from jax import lax
from jax.experimental import pallas as pl
from jax.experimental.pallas import tpu as pltpu

F32 = jnp.float32
BF16 = jnp.bfloat16
MXU_DTYPE = BF16
V7X_VMEM_BYTES = 64 * 1024 * 1024
VMEM_LIMIT = V7X_VMEM_BYTES * 3 // 4
LANES = 128
SUBLANES = 8

D_MODEL = 1024
DEPTH = 4
BRANCH_W = 256
CONV_W = 31
CONV_HALO = 32
CONV_CHUNK = 64
MLA_SCALE = (64 + 32) ** -0.5
SWA_SCALE = 64 ** -0.5
WINDOW = 128
ROPE_THETA = 10000.0
SSM_GROUPS, SSM_GROUP, SSM_STATE = 16, 16, 64
SSM_CH = SSM_GROUPS * SSM_STATE
SCAN_SEGMENTS = SUBLANES
SCAN_CB = 256
DEEPNORM_ALPHA = (2.0 * DEPTH) ** 0.25
LN_EPS = 1e-5
RMS_EPS = 1e-6
ADAM_LR, ADAM_B1, ADAM_B2, ADAM_EPS, ADAM_WD, ADAM_STEP = 0.001, 0.9, 0.999, 1e-08, 0.01, 10
NEG = -1e30
ROW_TILE = 512

NN = (((1,), (0,)), ((), ()))
NT = (((1,), (1,)), ((), ()))
TN = (((0,), (0,)), ((), ()))

MESH = pl.DeviceIdType.MESH
ANY = pl.BlockSpec(memory_space=pl.ANY)


def _dot(a, b, dims):
    return lax.dot_general(a.astype(MXU_DTYPE), b.astype(MXU_DTYPE), dims, preferred_element_type=F32)


def _pick(n, cands):
    for c in cands:
        if n % c == 0:
            return c
    return n


def _params(sem, vmem_limit=VMEM_LIMIT):
    return pltpu.CompilerParams(dimension_semantics=sem, vmem_limit_bytes=vmem_limit)


def _col_offsets(widths):
    return [sum(widths[:j]) for j in range(len(widths))]


def _silu_gate(x, z):
    return x * (z * jax.nn.sigmoid(z))


def _proj_fwd_call(x, wb, widths, name, z=None):
    t, k = x.shape
    tm = min(ROW_TILE, t)
    offs = _col_offsets(widths)
    ins = [x] if z is None else [x, z]

    def body(*refs):
        w_ref, o_refs = refs[len(ins)], refs[len(ins) + 1:]
        xv = refs[0][...] if z is None else _silu_gate(refs[0][...], refs[1][...])
        xb = xv.astype(MXU_DTYPE)
        for o_ref, off, wd in zip(o_refs, offs, widths):
            o_ref[...] = _dot(xb, w_ref[:, off:off + wd], NN)

    return pl.pallas_call(
        body, name=name, grid=(t // tm,),
        in_specs=[pl.BlockSpec((tm, k), lambda i: (i, 0))] * len(ins) + [pl.BlockSpec(wb.shape, lambda i: (0, 0))],
        out_specs=[pl.BlockSpec((tm, wd), lambda i: (i, 0)) for wd in widths],
        out_shape=[jax.ShapeDtypeStruct((t, wd), F32) for wd in widths],
        compiler_params=_params(("parallel",)),
    )(*ins, wb)


def _proj_dx_call(douts, wb, widths, name, gate=None):
    t = douts[0].shape[0]
    k = wb.shape[0]
    tm = min(ROW_TILE, t)
    offs = _col_offsets(widths)
    nd = len(douts)
    extra = [] if gate is None else list(gate)

    def body(*refs):
        d_refs, w_ref = refs[:nd], refs[nd]
        acc = jnp.zeros((tm, k), F32)
        for d_ref, off, wd in zip(d_refs, offs, widths):
            acc = acc + _dot(d_ref[...], w_ref[:, off:off + wd], NT)
        if gate is None:
            refs[-1][...] = acc
        else:
            xv, zv = refs[nd + 1][...], refs[nd + 2][...]
            sg = jax.nn.sigmoid(zv)
            refs[-2][...] = acc * (zv * sg)
            refs[-1][...] = acc * xv * (sg * (1.0 + zv * (1.0 - sg)))

    row = pl.BlockSpec((tm, k), lambda i: (i, 0))
    n_out = 1 if gate is None else 2
    res = pl.pallas_call(
        body, name=name, grid=(t // tm,),
        in_specs=[pl.BlockSpec((tm, wd), lambda i: (i, 0)) for wd in widths] + [pl.BlockSpec(wb.shape, lambda i: (0, 0))] + [row] * len(extra),
        out_specs=[row] * n_out, out_shape=[jax.ShapeDtypeStruct((t, k), F32)] * n_out,
        compiler_params=_params(("parallel",)),
    )(*douts, wb, *extra)
    return res[0] if gate is None else tuple(res)


def _proj_dw_call(x, douts, widths, name, out_dtype, z=None):
    t, k = x.shape
    n = sum(widths)
    tk = min(ROW_TILE if k * n <= 2 * 1024 * 1024 else ROW_TILE // 2, t)
    nk = t // tk
    offs = _col_offsets(widths)
    ins = [x] if z is None else [x, z]

    def body(*all_refs):
        refs = all_refs[len(ins):]
        d_refs, o_ref, acc_ref = refs[:-2], refs[-2], refs[-1]

        @pl.when(pl.program_id(0) == 0)
        def _():
            acc_ref[...] = jnp.zeros_like(acc_ref)

        xv = all_refs[0][...] if z is None else _silu_gate(all_refs[0][...], all_refs[1][...])
        xb = xv.astype(MXU_DTYPE)
        for d_ref, off, wd in zip(d_refs, offs, widths):
            acc_ref[:, off:off + wd] += _dot(xb, d_ref[...], TN)

        @pl.when(pl.program_id(0) == nk - 1)
        def _():
            o_ref[...] = acc_ref[...].astype(out_dtype)

    return pl.pallas_call(
        body, name=name, grid=(nk,),
        in_specs=[pl.BlockSpec((tk, k), lambda i: (i, 0))] * len(ins) + [pl.BlockSpec((tk, wd), lambda i: (i, 0)) for wd in widths],
        out_specs=pl.BlockSpec((k, n), lambda i: (0, 0)), out_shape=jax.ShapeDtypeStruct((k, n), out_dtype),
        scratch_shapes=[pltpu.VMEM((k, n), F32)],
        compiler_params=_params(("arbitrary",)),
    )(*ins, *douts)


def make_proj(widths, name):
    @jax.custom_vjp
    def op(x, w):
        return tuple(_proj_fwd_call(x, w.astype(MXU_DTYPE), widths, name + "_fwd"))

    def fwd(x, w):
        wb = w.astype(MXU_DTYPE)
        return tuple(_proj_fwd_call(x, wb, widths, name + "_fwd")), (x, wb, jnp.zeros((0,), w.dtype))

    def bwd(res, douts):
        x, wb, w_like = res
        return _proj_dx_call(douts, wb, widths, name + "_dx"), _proj_dw_call(x, douts, widths, name + "_dw", w_like.dtype)

    op.defvjp(fwd, bwd)
    return op


_MM_OPS = {}


def op_mm(a, w):
    n = w.shape[1]
    if n not in _MM_OPS:
        _MM_OPS[n] = make_proj((n,), "mm%d" % n)
    return _MM_OPS[n](a, w)[0]


@jax.custom_vjp
def _mm(a, w):
    return _dot(a, w, NN)


def _mm_f(a, w):
    return _dot(a, w, NN), (a, w)


def _mm_b(res, g):
    a, w = res
    return _dot(g, w, NT), _dot(a, g, TN)


_mm.defvjp(_mm_f, _mm_b)


@functools.partial(jax.custom_vjp, nondiff_argnums=(1,))
def _roll(x, shift):
    return pltpu.roll(x, shift, 1)


def _roll_f(x, shift):
    return pltpu.roll(x, shift, 1), None


def _roll_b(shift, _, g):
    return (pltpu.roll(g, (g.shape[1] - shift) % g.shape[1], 1),)


_roll.defvjp(_roll_f, _roll_b)


def _ln(x, g, b):
    mu = jnp.mean(x, axis=-1, keepdims=True)
    xc = x - mu
    var = jnp.mean(xc * xc, axis=-1, keepdims=True)
    return xc * lax.rsqrt(var + LN_EPS) * g + b


def _rms(x, g):
    ms = jnp.mean(x * x, axis=-1, keepdims=True)
    return x * lax.rsqrt(ms + RMS_EPS) * g


def _sigmoid(x):
    return jax.nn.sigmoid(x)


def _silu(x):
    return x * _sigmoid(x)


def _gelu_tanh(x):
    return x * (0.5 * (1.0 + jnp.tanh(math.sqrt(2.0 / math.pi) * (x + 0.044715 * (x * x * x)))))


def _rowwise_fwd_call(fn, rows, consts, name, tile):
    t = rows[0].shape[0]
    tile = min(tile, t)
    nr = len(rows)
    outs = jax.eval_shape(fn, *[jax.ShapeDtypeStruct((tile, r.shape[1]), F32) for r in rows],
                          *[jax.ShapeDtypeStruct(c.shape, F32) for c in consts])

    def body(*refs):
        vals = [r[...] for r in refs[:nr + len(consts)]]
        res = fn(*vals)
        for o_ref, o in zip(refs[nr + len(consts):], res):
            o_ref[...] = o

    return pl.pallas_call(
        body, name=name, grid=(t // tile,),
        in_specs=[pl.BlockSpec((tile, r.shape[1]), lambda i: (i, 0)) for r in rows]
        + [pl.BlockSpec(c.shape, lambda i: (0, 0)) for c in consts],
        out_specs=[pl.BlockSpec((tile, o.shape[1]), lambda i: (i, 0)) for o in outs],
        out_shape=[jax.ShapeDtypeStruct((t, o.shape[1]), F32) for o in outs],
        compiler_params=_params(("parallel",)),
    )(*rows, *consts)


def _rowwise_bwd_call(fn, rows, consts, douts, row_diff, name, tile, row_grad_dtype=F32):
    t = rows[0].shape[0]
    tile = min(tile, t)
    nr, nc, nd = len(rows), len(consts), len(douts)
    diff_idx = [i for i in range(nr) if row_diff[i]]

    def body(*refs):
        rv = [r[...] for r in refs[:nr]]
        cv = [r[...] for r in refs[nr:nr + nc]]
        dv = [r[...] for r in refs[nr + nc:nr + nc + nd]]
        out_refs = refs[nr + nc + nd:]

        def f(*diff):
            full = list(rv)
            for k, i in enumerate(diff_idx):
                full[i] = diff[k]
            return fn(*full, *diff[len(diff_idx):])

        _, vjp = jax.vjp(f, *[rv[i] for i in diff_idx], *cv)
        grads = vjp(tuple(dv))
        for k in range(len(diff_idx)):
            out_refs[k][...] = grads[k].astype(row_grad_dtype)
        first = pl.program_id(0) == 0
        for k in range(nc):
            acc_ref = out_refs[len(diff_idx) + k]
            g = grads[len(diff_idx) + k]

            @pl.when(first)
            def _(acc_ref=acc_ref, g=g):
                acc_ref[...] = g

            @pl.when(jnp.logical_not(first))
            def _(acc_ref=acc_ref, g=g):
                acc_ref[...] += g

    res = pl.pallas_call(
        body, name=name, grid=(t // tile,),
        in_specs=[pl.BlockSpec((tile, r.shape[1]), lambda i: (i, 0)) for r in rows]
        + [pl.BlockSpec(c.shape, lambda i: (0, 0)) for c in consts]
        + [pl.BlockSpec((tile, d.shape[1]), lambda i: (i, 0)) for d in douts],
        out_specs=[pl.BlockSpec((tile, rows[i].shape[1]), lambda i_: (i_, 0)) for i in diff_idx]
        + [pl.BlockSpec(c.shape, lambda i: (0, 0)) for c in consts],
        out_shape=[jax.ShapeDtypeStruct(rows[i].shape, row_grad_dtype) for i in diff_idx]
        + [jax.ShapeDtypeStruct(c.shape, F32) for c in consts],
        compiler_params=_params(("arbitrary",)),
    )(*rows, *consts, *douts)
    return res[:len(diff_idx)], res[len(diff_idx):]


def make_rowwise(fn, name, row_diff, tile=ROW_TILE):
    @jax.custom_vjp
    def op(rows, consts):
        return tuple(_rowwise_fwd_call(fn, rows, consts, name + "_fwd", tile))

    def fwd(rows, consts):
        return op(rows, consts), (rows, consts)

    def bwd(res, douts):
        rows, consts = res
        drows, dconsts = _rowwise_bwd_call(fn, rows, consts, douts, row_diff, name + "_bwd", tile)
        it = iter(drows)
        full = tuple(next(it) if row_diff[i] else jnp.zeros_like(rows[i]) for i in range(len(rows)))
        return full, tuple(dconsts)

    op.defvjp(fwd, bwd)
    return op


def _conv_post_fn(cv, a_z, ng, nb, w_pw2):
    return (_mm(_silu(_ln(cv, ng, nb)), w_pw2) * _silu(a_z),)


def _mla_prep_fn(c_q, c_kv, krblk, cos4, sin4, qg, kvg, w_uq, w_uk, w_uv):
    qe = _mm(_rms(c_q, qg), w_uq)
    q = qe * cos4 + _roll(qe, qe.shape[1] - 32) * sin4
    cos1, sin1 = cos4[:, :LANES], sin4[:, :LANES]
    kr = krblk * cos1 + _roll(krblk, LANES - 32) * sin1
    kn = _rms(c_kv, kvg)
    k = _mm(kn, w_uk) + jnp.concatenate([kr, kr, kr, kr], axis=1)
    return q, k, _mm(kn, w_uv)


def _ssm_post_fn(y, u, c_z, d, w_a, w_b):
    y2 = _gelu_tanh(y + d * u)
    return (_mm(y2, w_a) * _sigmoid(_mm(y2, w_b)) * _silu(c_z),)


def _merge_fn(br0, br1, br2, br3, gl0, gl1, gl2, gl3, b0, b1, b2, b3):
    return (_sigmoid(gl0 + b0) * br0 + _sigmoid(gl1 + b1) * br1 + _sigmoid(gl2 + b2) * br2 + _sigmoid(gl3 + b3) * br3,)


def _ln_fn(x, mo, g, b):
    return (_ln(DEEPNORM_ALPHA * x + mo, g, b),)


def _ple_fn(x1, pe, gl, g):
    return (x1 + _rms(pe * _sigmoid(gl), g),)


op_conv_post = make_rowwise(_conv_post_fn, "conv_post", (True, True))
op_mla_prep = make_rowwise(_mla_prep_fn, "mla_prep", (True, True, True, False, False))
op_ssm_post = make_rowwise(_ssm_post_fn, "ssm_post", (True, True, True))
MERGE_TILE = ROW_TILE // 2
MERGE_WIDTHS = (D_MODEL,) * 4


@jax.custom_vjp
def op_merge_block(x, ys, zs, w_merge, w_branch, b_merge):
    return _merge_block_fwd(x, ys, zs, w_merge, w_branch, b_merge)[0]


def _merge_block_fwd(x, ys, zs, w_merge, w_branch, b_merge):
    wm, wb = w_merge.astype(MXU_DTYPE), w_branch.astype(MXU_DTYPE)
    gl = _proj_fwd_call(x, wm, MERGE_WIDTHS, "merge_proj_fwd")
    br = [_proj_fwd_call(ys[n], wb[n], (D_MODEL,), "branch_proj_fwd", zs[n])[0] for n in range(4)]
    bm = tuple(b_merge[n * D_MODEL:(n + 1) * D_MODEL].reshape(1, -1) for n in range(4))
    (merged,) = _rowwise_fwd_call(_merge_fn, (*br, *gl), bm, "merge_fwd", MERGE_TILE)
    return merged, (x, ys, zs, wm, wb, tuple(br), tuple(gl), bm, jnp.zeros((0,), w_merge.dtype), jnp.zeros((0,), w_branch.dtype))


def _merge_block_bwd(res, dmerged):
    x, ys, zs, wm, wb, br, gl, bm, wm_like, wb_like = res
    drows, dbm = _rowwise_bwd_call(_merge_fn, (*br, *gl), bm, (dmerged,), (True,) * 8, "merge_bwd", MERGE_TILE, MXU_DTYPE)
    dbr, dgl = drows[:4], drows[4:]
    dx = _proj_dx_call(dgl, wm, MERGE_WIDTHS, "merge_proj_dx")
    dwm = _proj_dw_call(x, dgl, MERGE_WIDTHS, "merge_proj_dw", wm_like.dtype)
    dys, dzs = [], []
    for n in range(4):
        if zs[n] is None:
            dys.append(_proj_dx_call([dbr[n]], wb[n], (D_MODEL,), "branch_proj_dx"))
            dzs.append(None)
        else:
            dy, dz = _proj_dx_call([dbr[n]], wb[n], (D_MODEL,), "branch_proj_dx", (ys[n], zs[n]))
            dys.append(dy)
            dzs.append(dz)
    dwb = jnp.stack([_proj_dw_call(ys[n], [dbr[n]], (D_MODEL,), "branch_proj_dw", wb_like.dtype, zs[n]) for n in range(4)])
    return dx, tuple(dys), tuple(dzs), dwm, dwb, jnp.concatenate([d.reshape(-1) for d in dbm])


op_merge_block.defvjp(_merge_block_fwd, _merge_block_bwd)
op_ln = make_rowwise(_ln_fn, "post_ln", (True, True))
op_ple = make_rowwise(_ple_fn, "ple", (True, True, True))


def _shift_copies(buf, shifted, tile):
    for r in range(1, SUBLANES):
        shifted[r - 1, :, :] = buf[pl.ds(r, tile + CONV_HALO - SUBLANES), :]


def _tap(buf, shifted, off, tile):
    r = off % SUBLANES
    return buf[pl.ds(off, tile), :] if r == 0 else shifted[r - 1, pl.ds(off - r, tile), :]


def _shift_scratch(tile, w):
    return pltpu.VMEM((SUBLANES - 1, tile + CONV_HALO - SUBLANES, w), F32)


def _conv_fwd_call(a_val, a_gate, w32, b):
    t, w = a_val.shape
    tile = min(ROW_TILE, t)
    per = tile // CONV_HALO
    cur = pl.BlockSpec((tile, w), lambda i: (i, 0))
    prev = pl.BlockSpec((CONV_HALO, w), lambda i: (jnp.maximum(i * per - 1, 0), 0))

    def body(av_ref, avh_ref, ag_ref, agh_ref, w_ref, b_ref, cv_ref, buf, shifted):
        i = pl.program_id(0)
        gh = avh_ref[...] * _sigmoid(agh_ref[...])
        buf[0:CONV_HALO, :] = jnp.where(i > 0, gh, 0.0)
        buf[CONV_HALO:, :] = av_ref[...] * _sigmoid(ag_ref[...])
        _shift_copies(buf, shifted, tile)
        for c0 in range(0, tile, CONV_CHUNK):
            acc = jnp.zeros((CONV_CHUNK, w), F32) + b_ref[...]
            for j in range(CONV_W):
                acc = acc + w_ref[j:j + 1, :] * _tap(buf, shifted, c0 + CONV_HALO - (CONV_W - 1) + j, CONV_CHUNK)
            cv_ref[c0:c0 + CONV_CHUNK, :] = acc

    return pl.pallas_call(
        body, name="conv_fwd", grid=(t // tile,),
        in_specs=[cur, prev, cur, prev, pl.BlockSpec((CONV_HALO, w), lambda i: (0, 0)), pl.BlockSpec((1, w), lambda i: (0, 0))],
        out_specs=cur, out_shape=jax.ShapeDtypeStruct((t, w), F32),
        scratch_shapes=[pltpu.VMEM((tile + CONV_HALO, w), F32), _shift_scratch(tile, w)],
        compiler_params=_params(("parallel",)),
    )(a_val, a_val, a_gate, a_gate, w32, b)


def _conv_bwd_call(a_val, a_gate, w32, dcv):
    t, w = a_val.shape
    tile = min(ROW_TILE, t)
    n = t // tile
    per = tile // CONV_HALO
    cur = pl.BlockSpec((tile, w), lambda i: (i, 0))
    prev = pl.BlockSpec((CONV_HALO, w), lambda i: (jnp.maximum(i * per - 1, 0), 0))
    nxt = pl.BlockSpec((CONV_HALO, w), lambda i: (jnp.minimum((i + 1) * per, t // CONV_HALO - 1), 0))
    full = lambda r: pl.BlockSpec((r, w), lambda i: (0, 0))

    def body(av_ref, avh_ref, ag_ref, agh_ref, w_ref, d_ref, dn_ref, dav_ref, dag_ref, dw_ref, db_ref, gbuf, dbuf, gsh, dsh):
        i = pl.program_id(0)
        gh = avh_ref[...] * _sigmoid(agh_ref[...])
        gbuf[0:CONV_HALO, :] = jnp.where(i > 0, gh, 0.0)
        gbuf[CONV_HALO:, :] = av_ref[...] * _sigmoid(ag_ref[...])
        dbuf[0:tile, :] = d_ref[...]
        dbuf[tile:, :] = jnp.where(i < n - 1, dn_ref[...], 0.0)

        @pl.when(i == 0)
        def _():
            dw_ref[...] = jnp.zeros_like(dw_ref)
            db_ref[...] = jnp.zeros_like(db_ref)

        _shift_copies(gbuf, gsh, tile)
        _shift_copies(dbuf, dsh, tile)
        for c0 in range(0, tile, CONV_CHUNK):
            rows = slice(c0, c0 + CONV_CHUNK)
            d = d_ref[rows, :]
            dg = jnp.zeros((CONV_CHUNK, w), F32)
            for j in range(CONV_W):
                dg = dg + w_ref[j:j + 1, :] * _tap(dbuf, dsh, c0 + CONV_W - 1 - j, CONV_CHUNK)
                dw_ref[j:j + 1, :] += jnp.sum(d * _tap(gbuf, gsh, c0 + CONV_HALO - (CONV_W - 1) + j, CONV_CHUNK), axis=0, keepdims=True)
            db_ref[...] += jnp.sum(d, axis=0, keepdims=True)
            av = av_ref[rows, :]
            sg = _sigmoid(ag_ref[rows, :])
            dav_ref[rows, :] = dg * sg
            dag_ref[rows, :] = dg * av * sg * (1.0 - sg)

    return pl.pallas_call(
        body, name="conv_bwd", grid=(n,),
        in_specs=[cur, prev, cur, prev, full(CONV_HALO), cur, nxt],
        out_specs=[cur, cur, full(CONV_HALO), full(1)],
        out_shape=[jax.ShapeDtypeStruct((t, w), F32), jax.ShapeDtypeStruct((t, w), F32),
                   jax.ShapeDtypeStruct((CONV_HALO, w), F32), jax.ShapeDtypeStruct((1, w), F32)],
        scratch_shapes=[pltpu.VMEM((tile + CONV_HALO, w), F32), pltpu.VMEM((tile + CONV_HALO, w), F32),
                        _shift_scratch(tile, w), _shift_scratch(tile, w)],
        compiler_params=_params(("arbitrary",)),
    )(a_val, a_val, a_gate, a_gate, w32, dcv, dcv)


def _pad_taps(conv_w):
    return jnp.concatenate([conv_w, jnp.zeros((CONV_HALO - CONV_W, conv_w.shape[1]), F32)], axis=0)


@jax.custom_vjp
def op_conv(a_val, a_gate, conv_w, conv_b):
    return _conv_fwd_call(a_val, a_gate, _pad_taps(conv_w), conv_b)


def _op_conv_fwd(a_val, a_gate, conv_w, conv_b):
    return op_conv(a_val, a_gate, conv_w, conv_b), (a_val, a_gate, conv_w)


def _op_conv_bwd(res, dcv):
    a_val, a_gate, conv_w = res
    dav, dag, dw, db = _conv_bwd_call(a_val, a_gate, _pad_taps(conv_w), dcv)
    return dav, dag, dw[:CONV_W], db


op_conv.defvjp(_op_conv_fwd, _op_conv_bwd)


def _head_masks(rows):
    lane = lax.broadcasted_iota(jnp.int32, (rows, LANES), 1)
    return lane < 64, lane >= 64


def _head_row(vals, mask):
    return jnp.max(jnp.where(mask, vals, NEG), axis=1, keepdims=True)


def _attn_valid(qpos, kpos, window):
    valid = kpos <= qpos
    if window is not None:
        valid = jnp.logical_and(valid, qpos - kpos < window)
    return valid


def _flash_fwd_call(q, k, v, sink, *, window, shared_k, scale, blk, blk_q, name):
    t = q.shape[0]
    qw = LANES if shared_k else 2 * LANES
    pairs = v.shape[1] // LANES
    tk = min(blk, t)
    tq = min(blk_q, t)
    has_sink = sink is not None
    one_step = window is not None and tk == 2 * tq and window <= tq
    kstride = tq if one_step else tk

    def body(*refs):
        if has_sink:
            q_ref, k_ref, v_ref, s_ref, o_ref, lse_ref, k_mxu, v0_mxu, v1_mxu = refs
        else:
            q_ref, k_ref, v_ref, o_ref, lse_ref, k_mxu, v0_mxu, v1_mxu = refs
        v_mxu = (v0_mxu, v1_mxu)
        i = pl.program_id(1)

        @pl.when(i == 0)
        def _():
            full_masks = _head_masks(t)
            k_mxu[...] = k_ref[...].astype(MXU_DTYPE)
            for h in range(2):
                v_mxu[h][...] = jnp.where(full_masks[h], v_ref[...], 0.0).astype(MXU_DTYPE)

        qb = q_ref[...]
        masks = _head_masks(tq)
        row_masks = _head_masks(1)
        qh = [(jnp.where(masks[h], qb, 0.0) if shared_k else qb[:, h * LANES:(h + 1) * LANES]).astype(MXU_DTYPE) for h in range(2)]
        qpos = i * tq + lax.broadcasted_iota(jnp.int32, (tq, tk), 0)
        if has_sink:
            m_init = [jnp.zeros((tq, 1), F32) + _head_row(s_ref[...], row_masks[h]) for h in range(2)]
            l_init = [jnp.ones((tq, 1), F32)] * 2
        else:
            m_init = [jnp.full((tq, 1), NEG, F32)] * 2
            l_init = [jnp.zeros((tq, 1), F32)] * 2

        def make_step(masked):
            def step(j, carry):
                m0, l0, m1, l1, acc = carry
                start = pl.multiple_of(j * kstride, kstride)
                kb = k_mxu[pl.ds(start, tk), :]
                if masked:
                    valid = _attn_valid(qpos, start + lax.broadcasted_iota(jnp.int32, (tq, tk), 1), window)
                new, alphas, pv = [], [], []
                for h, (m, l) in enumerate(((m0, l0), (m1, l1))):
                    kh = kb if shared_k else kb[:, h * LANES:(h + 1) * LANES]
                    s = _dot(qh[h], kh, NT) * scale
                    if masked:
                        s = jnp.where(valid, s, NEG)
                    m_new = jnp.maximum(m, jnp.max(s, axis=1, keepdims=True))
                    alpha = jnp.exp(m - m_new)
                    p = jnp.exp(s - m_new)
                    new += [m_new, alpha * l + jnp.sum(p, axis=1, keepdims=True)]
                    alphas.append(alpha)
                    pv.append(_dot(p, v_mxu[h][pl.ds(start, tk), :], NN))
                acc = acc * jnp.where(masks[0], alphas[0], alphas[1]) + pv[0] + pv[1]
                return new[0], new[1], new[2], new[3], acc
            return step

        carry = (m_init[0], l_init[0], m_init[1], l_init[1], jnp.zeros((tq, LANES), F32))
        last = (i * tq + tq - 1) // tk
        if window is None:
            n_full = (i * tq + 1) // tk
            carry = lax.fori_loop(0, n_full, make_step(False), carry)
            carry = lax.fori_loop(n_full, last + 1, make_step(True), carry)
        elif one_step:
            carry = make_step(True)(jnp.maximum(i - 1, 0), carry)
        else:
            carry = lax.fori_loop(jnp.maximum(i * tq - (window - 1), 0) // tk, last + 1, make_step(True), carry)
        m0, l0, m1, l1, acc = carry
        o_ref[...] = acc / jnp.where(masks[0], l0, l1)
        lse_ref[...] = jnp.where(masks[0], m0 + jnp.log(l0), m1 + jnp.log(l1))

    in_specs = [pl.BlockSpec((tq, qw), lambda p, i: (i, p)), pl.BlockSpec((t, qw), lambda p, i: (0, p)),
                pl.BlockSpec((t, LANES), lambda p, i: (0, p))]
    args = [q, k, v]
    if has_sink:
        in_specs.append(pl.BlockSpec((1, LANES), lambda p, i: (0, p)))
        args.append(sink)
    blk_o = pl.BlockSpec((tq, LANES), lambda p, i: (i, p))
    return pl.pallas_call(
        body, name=name, grid=(pairs, t // tq), in_specs=in_specs, out_specs=[blk_o, blk_o],
        out_shape=[jax.ShapeDtypeStruct((t, pairs * LANES), F32)] * 2,
        scratch_shapes=[pltpu.VMEM((t, qw), MXU_DTYPE), pltpu.VMEM((t, LANES), MXU_DTYPE), pltpu.VMEM((t, LANES), MXU_DTYPE)],
        compiler_params=_params(("arbitrary", "arbitrary")),
    )(*args)


def _flash_bwd_call(q, k, v, sink, o, lse, do, *, window, shared_k, scale, blk, blk_q, name):
    t = q.shape[0]
    qw = LANES if shared_k else 2 * LANES
    pairs = v.shape[1] // LANES
    tk = min(blk, t)
    tq = min(blk_q, t)
    assert tk % tq == 0 or tq % tk == 0
    nq = t // tq
    one_step = window is not None and tq == 2 * tk and window <= tk
    qstride = tk if one_step else tq
    has_sink = sink is not None

    def body(*refs):
        if has_sink:
            q_ref, k_ref, v_ref, o_ref, lse_ref, do_ref, s_ref, dq_ref, dk_ref, dv_ref, ds_ref = refs[:11]
        else:
            q_ref, k_ref, v_ref, o_ref, lse_ref, do_ref, dq_ref, dk_ref, dv_ref = refs[:9]
        q_mxu, do_mxu, lse_h, dsum_h = refs[-8:-6], refs[-6:-4], refs[-4:-2], refs[-2:]
        j = pl.program_id(1)
        masks = _head_masks(tq)
        row_masks = _head_masks(1)

        @pl.when(j == 0)
        def _():
            dq_ref[...] = jnp.zeros_like(dq_ref)
            full_masks = _head_masks(t)
            prod = do_ref[...] * o_ref[...]
            parts = []
            for h in range(2):
                qh = jnp.where(full_masks[h], q_ref[...], 0.0) if shared_k else q_ref[:, h * LANES:(h + 1) * LANES]
                q_mxu[h][...] = qh.astype(MXU_DTYPE)
                do_mxu[h][...] = jnp.where(full_masks[h], do_ref[...], 0.0).astype(MXU_DTYPE)
                dsum = jnp.sum(jnp.where(full_masks[h], prod, 0.0), axis=1, keepdims=True)
                lse = _head_row(lse_ref[...], full_masks[h])
                dsum_h[h][...] = jnp.zeros((t, LANES), F32) + dsum
                lse_h[h][...] = jnp.zeros((t, LANES), F32) + lse
                if has_sink:
                    ps = jnp.exp(_head_row(s_ref[...], row_masks[h]) - lse)
                    parts.append(-jnp.sum(ps * dsum, axis=0, keepdims=True))
            if has_sink:
                ds_ref[...] = jnp.zeros((SUBLANES, LANES), F32) + jnp.where(row_masks[0], parts[0], parts[1])

        kb = k_ref[...].astype(MXU_DTYPE)
        vb = v_ref[...].astype(MXU_DTYPE)
        kh = [kb if shared_k else kb[:, h * LANES:(h + 1) * LANES] for h in range(2)]
        kpos = j * tk + lax.broadcasted_iota(jnp.int32, (tq, tk), 1)
        lanes_of = lambda a: a if tk == LANES else jnp.concatenate([a] * (tk // LANES), axis=1)

        def make_step(masked):
            def step(i, carry):
                dk0, dk1, dv = carry
                start = pl.multiple_of(i * qstride, qstride)
                if masked:
                    valid = _attn_valid(start + lax.broadcasted_iota(jnp.int32, (tq, tk), 0), kpos, window)
                dks, dqs = [], []
                for h in range(2):
                    qh = q_mxu[h][pl.ds(start, tq), :]
                    doh = do_mxu[h][pl.ds(start, tq), :]
                    s = _dot(qh, kh[h], NT) * scale
                    if masked:
                        s = jnp.where(valid, s, NEG)
                    p = jnp.exp(s - lanes_of(lse_h[h][pl.ds(start, tq), :]))
                    dp = _dot(doh, vb, NT)
                    dsc = p * (dp - lanes_of(dsum_h[h][pl.ds(start, tq), :])) * scale
                    dv = dv + _dot(p, doh, TN)
                    dks.append(_dot(dsc, qh, TN))
                    dq_h = _dot(dsc, kh[h], NN)
                    dqs.append(jnp.where(masks[h], dq_h, 0.0) if shared_k else dq_h)
                if shared_k:
                    dq_ref[pl.ds(start, tq), :] += dqs[0] + dqs[1]
                else:
                    dq_ref[pl.ds(start, tq), :] += jnp.concatenate(dqs, axis=1)
                return dk0 + dks[0], dk1 + dks[1], dv
            return step

        zero = jnp.zeros((tk, LANES), F32)
        carry = (zero, zero, zero)
        first = (j * tk) // tq
        if window is None:
            n_full = jnp.minimum(((j + 1) * tk + tq - 2) // tq, nq)
            carry = lax.fori_loop(first, n_full, make_step(True), carry)
            carry = lax.fori_loop(n_full, nq, make_step(False), carry)
        elif one_step:
            carry = make_step(True)(jnp.minimum(j, t // tk - 2), carry)
        else:
            carry = lax.fori_loop(first, jnp.minimum(nq, (j * tk + tk - 1 + window - 1) // tq + 1), make_step(True), carry)
        dk0, dk1, dv = carry
        dk_ref[...] = dk0 + dk1 if shared_k else jnp.concatenate([dk0, dk1], axis=1)
        dv_ref[...] = dv

    full = lambda w: pl.BlockSpec((t, w), lambda p, j: (0, p))
    blkspec = lambda w: pl.BlockSpec((tk, w), lambda p, j: (j, p))
    in_specs = [full(qw), blkspec(qw), blkspec(LANES), full(LANES), full(LANES), full(LANES)]
    args = [q, k, v, o, lse, do]
    out_specs = [full(qw), blkspec(qw), blkspec(LANES)]
    out_shape = [jax.ShapeDtypeStruct(q.shape, F32), jax.ShapeDtypeStruct(k.shape, F32), jax.ShapeDtypeStruct(v.shape, F32)]
    if has_sink:
        in_specs.append(pl.BlockSpec((1, LANES), lambda p, j: (0, p)))
        args.append(sink)
        out_specs.append(pl.BlockSpec((SUBLANES, LANES), lambda p, j: (0, p)))
        out_shape.append(jax.ShapeDtypeStruct((SUBLANES, pairs * LANES), F32))
    return pl.pallas_call(
        body, name=name, grid=(pairs, t // tk), in_specs=in_specs, out_specs=out_specs, out_shape=out_shape,
        scratch_shapes=[pltpu.VMEM((t, LANES), MXU_DTYPE)] * 4 + [pltpu.VMEM((t, LANES), F32)] * 4,
        compiler_params=_params(("arbitrary", "arbitrary")),
    )(*args)


_MLA_CFG = dict(window=None, shared_k=False, scale=MLA_SCALE, blk=256)
_SWA_CFG = dict(window=WINDOW, shared_k=True, scale=SWA_SCALE, blk=128)
_MLA_FWD_CFG = dict(_MLA_CFG, blk=512, blk_q=512)
_SWA_FWD_CFG = dict(_SWA_CFG, blk=1024, blk_q=512)
_MLA_BWD_CFG = dict(_MLA_CFG, blk=512, blk_q=512)
_SWA_BWD_CFG = dict(_SWA_CFG, blk=512, blk_q=1024)


@jax.custom_vjp
def op_mla_attn(q, k, v):
    return _flash_fwd_call(q, k, v, None, name="mla_fwd", **_MLA_FWD_CFG)[0]


def _op_mla_attn_fwd(q, k, v):
    o, lse = _flash_fwd_call(q, k, v, None, name="mla_fwd", **_MLA_FWD_CFG)
    return o, (q, k, v, o, lse)


def _op_mla_attn_bwd(res, do):
    q, k, v, o, lse = res
    return tuple(_flash_bwd_call(q, k, v, None, o, lse, do, name="mla_bwd", **_MLA_BWD_CFG))


op_mla_attn.defvjp(_op_mla_attn_fwd, _op_mla_attn_bwd)


@jax.custom_vjp
def op_swa_attn(q, k, v, sink):
    return _flash_fwd_call(q, k, v, sink, name="swa_fwd", **_SWA_FWD_CFG)[0]


def _op_swa_attn_fwd(q, k, v, sink):
    o, lse = _flash_fwd_call(q, k, v, sink, name="swa_fwd", **_SWA_FWD_CFG)
    return o, (q, k, v, sink, o, lse)


def _op_swa_attn_bwd(res, do):
    q, k, v, sink, o, lse = res
    dq, dk, dv, dsink = _flash_bwd_call(q, k, v, sink, o, lse, do, name="swa_bwd", **_SWA_BWD_CFG)
    first_lane = lax.broadcasted_iota(jnp.int32, (1, dsink.shape[1]), 1) % 64 == 0
    return dq, dk, dv, jnp.where(first_lane, dsink[:1], 0.0)


op_swa_attn.defvjp(_op_swa_attn_fwd, _op_swa_attn_bwd)


def _complex_power(ar, ai, n):
    for _ in range(int(math.log2(n))):
        ar, ai = ar * ar - ai * ai, 2.0 * ar * ai
    return ar, ai


def _scan_passes(load_b, a1r, a1i, n, store, e_ref, c_ref, reverse):
    cb = a1r.shape[1]
    ar = jnp.zeros((SCAN_SEGMENTS, cb), F32) + a1r
    ai = jnp.zeros((SCAN_SEGMENTS, cb), F32) + a1i
    a2r, a2i = ar * ar - ai * ai, 2.0 * ar * ai
    idx = (lambda k: n - 1 - k) if reverse else (lambda k: k)
    mac = lambda pr_, pi__, h, b: (pr_ * h[0] - pi__ * h[1] + b[0], pr_ * h[1] + pi__ * h[0] + b[1])

    def load_pair(ii):
        k = jnp.minimum(2 * ii, n - 2)
        b0, b1 = load_b(idx(k)), load_b(idx(k + 1))
        return b0, mac(ar, ai, b0, b1)

    def local(ii, carry):
        h, c = carry
        return mac(a2r, a2i, h, c), load_pair(ii + 1)[1]

    zero = jnp.zeros((SCAN_SEGMENTS, cb), F32)
    (er, ei), _ = lax.fori_loop(0, n // 2, local, ((zero, zero), load_pair(0)[1]))
    e_ref[:, 0:cb] = er
    e_ref[:, cb:] = ei
    pr, pi_ = _complex_power(a1r, a1i, n)
    cr = jnp.zeros((1, cb), F32)
    ci = jnp.zeros((1, cb), F32)
    order = range(SCAN_SEGMENTS - 1, -1, -1) if reverse else range(SCAN_SEGMENTS)
    for s in order:
        c_ref[s:s + 1, 0:cb] = cr
        c_ref[s:s + 1, cb:] = ci
        er1, ei1 = e_ref[s:s + 1, 0:cb], e_ref[s:s + 1, cb:]
        cr, ci = pr * cr - pi_ * ci + er1, pr * ci + pi_ * cr + ei1

    def second(ii, carry):
        h, b0, c = carry
        h0 = mac(ar, ai, h, b0)
        h1 = mac(a2r, a2i, h, c)
        store(idx(2 * ii), *h0)
        store(idx(2 * ii + 1), *h1)
        nb0, nc = load_pair(ii + 1)
        return h1, nb0, nc

    b0, c0 = load_pair(0)
    lax.fori_loop(0, n // 2, second, ((c_ref[:, 0:cb], c_ref[:, cb:]), b0, c0))


def _scan_fwd_call(bu, lam):
    n = bu.shape[0]
    cb = SCAN_CB
    blk3 = pl.BlockSpec((n, SCAN_SEGMENTS, 2 * cb), lambda c: (0, 0, c))
    blk2 = lambda r: pl.BlockSpec((r, 2 * cb), lambda c: (0, c))

    def body(b_ref, lam_ref, h_ref, cin_ref, e_ref):
        def store(i, hr, hi):
            h_ref[i, :, 0:cb] = hr
            h_ref[i, :, cb:] = hi

        _scan_passes(lambda i: (b_ref[i, :, 0:cb], b_ref[i, :, cb:]), lam_ref[:, 0:cb], lam_ref[:, cb:], n, store,
                     e_ref, cin_ref, False)

    return pl.pallas_call(
        body, name="scan_fwd", grid=(SSM_CH // cb,), in_specs=[blk3, blk2(1)], out_specs=[blk3, blk2(SCAN_SEGMENTS)],
        out_shape=[jax.ShapeDtypeStruct(bu.shape, F32), jax.ShapeDtypeStruct((SCAN_SEGMENTS, 2 * SSM_CH), F32)],
        scratch_shapes=[pltpu.VMEM((SCAN_SEGMENTS, 2 * cb), F32)],
        compiler_params=_params(("parallel",)),
    )(bu, lam)


def _scan_bwd_call(dh, h, cin, lam):
    n = dh.shape[0]
    cb = SCAN_CB
    blk3 = pl.BlockSpec((n, SCAN_SEGMENTS, 2 * cb), lambda c: (0, 0, c))
    blk2 = lambda r: pl.BlockSpec((r, 2 * cb), lambda c: (0, c))

    def body(d_ref, h_ref, cin_ref, lam_ref, g_ref, dlam_ref, e_ref, c_ref, acc_ref):
        acc_ref[...] = jnp.zeros_like(acc_ref)

        def store(i, gr, gi):
            g_ref[i, :, 0:cb] = gr
            g_ref[i, :, cb:] = gi
            ip = jnp.maximum(i - 1, 0)
            hpr = jnp.where(i > 0, h_ref[ip, :, 0:cb], cin_ref[:, 0:cb])
            hpi = jnp.where(i > 0, h_ref[ip, :, cb:], cin_ref[:, cb:])
            acc_ref[:, 0:cb] += gr * hpr + gi * hpi
            acc_ref[:, cb:] += gi * hpr - gr * hpi

        _scan_passes(lambda i: (d_ref[i, :, 0:cb], d_ref[i, :, cb:]), lam_ref[:, 0:cb], -lam_ref[:, cb:], n, store,
                     e_ref, c_ref, True)
        dlam_ref[...] = acc_ref[...]

    return pl.pallas_call(
        body, name="scan_bwd", grid=(SSM_CH // cb,), in_specs=[blk3, blk3, blk2(SCAN_SEGMENTS), blk2(1)],
        out_specs=[blk3, blk2(SCAN_SEGMENTS)],
        out_shape=[jax.ShapeDtypeStruct(dh.shape, F32), jax.ShapeDtypeStruct((SCAN_SEGMENTS, 2 * SSM_CH), F32)],
        scratch_shapes=[pltpu.VMEM((SCAN_SEGMENTS, 2 * cb), F32)] * 3,
        compiler_params=_params(("parallel",), 6 * n * SCAN_SEGMENTS * 2 * cb * 4 + 4 * 1024 * 1024),
    )(dh, h, cin, lam)


@jax.custom_vjp
def op_scan(bu, lam):
    return _scan_fwd_call(bu, lam)[0]


def _op_scan_fwd(bu, lam):
    h, cin = _scan_fwd_call(bu, lam)
    return h, (h, cin, lam)


def _op_scan_bwd(res, dh):
    h, cin, lam = res
    g, dlam = _scan_bwd_call(dh, h, cin, lam)
    return g, jnp.sum(dlam, axis=0, keepdims=True)


op_scan.defvjp(_op_scan_fwd, _op_scan_bwd)


def _loss_call(y, target):
    t, d = y.shape
    tile = min(ROW_TILE, t)

    def body(y_ref, t_ref, dy_ref, acc_ref):
        @pl.when(pl.program_id(0) == 0)
        def _():
            acc_ref[...] = jnp.zeros_like(acc_ref)

        err = y_ref[...] - t_ref[...]
        dy_ref[...] = err * (1.0 / d)
        col = jnp.sum(err * err, axis=0, keepdims=True)
        part = col[:, 0:LANES]
        for c in range(1, d // LANES):
            part = part + col[:, c * LANES:(c + 1) * LANES]
        acc_ref[0:1, :] += part

    blk = pl.BlockSpec((tile, d), lambda i: (i, 0))
    dy, acc = pl.pallas_call(
        body, name="loss_head", grid=(t // tile,), in_specs=[blk, blk],
        out_specs=[blk, pl.BlockSpec((SUBLANES, LANES), lambda i: (0, 0))],
        out_shape=[jax.ShapeDtypeStruct((t, d), F32), jax.ShapeDtypeStruct((SUBLANES, LANES), F32)],
        compiler_params=_params(("arbitrary",)),
    )(y, target)
    return jnp.sum(acc) * (0.5 / d), dy


def _rot_cols(w, xp=jnp):
    return xp.concatenate([-w[:, 16:], w[:, :16]], axis=1)


def _ext_w_in(w, xp=jnp):
    a_val, a_gate, a_z, c_q, c_kv, k_r, b_z, u, c_z, q, k, v, d_z = xp.split(
        w, (256, 512, 768, 1024, 1152, 1184, 1440, 1696, 1952, 2208, 2336, 2464), axis=1)
    dup = lambda m: xp.concatenate([m[:, :64], m[:, :64], m[:, 64:], m[:, 64:]], axis=1)
    krblk = xp.concatenate([xp.zeros((w.shape[0], 64), w.dtype), k_r, _rot_cols(k_r, xp)], axis=1)
    return xp.concatenate([a_val, a_gate, a_z, c_q, b_z, u, c_z, q, dup(k), dup(v), d_z, c_kv, krblk], axis=1)


IN_WIDTH = 2720
IN_SHARD = IN_WIDTH // 4
IN_SHARD_PAD = 768
IN_EXT = 3072


BAND = 512


def _w_in_layout():
    src = _ext_w_in(np.arange(1, IN_WIDTH + 1, dtype=np.float32)[None, :], np)[0]
    col = np.abs(src).astype(np.int64) - 1
    row = np.where(col >= 0, (col // IN_SHARD) * IN_SHARD_PAD + col % IN_SHARD, -1)
    return row, np.sign(src)


def _w_in_layout_matrix():
    row, sign = _w_in_layout()
    rows = lax.broadcasted_iota(jnp.int32, (4 * IN_SHARD_PAD, IN_EXT), 0)
    return jnp.where(rows == jnp.asarray(row, jnp.int32)[None, :], jnp.asarray(sign, F32)[None, :], 0.0).astype(MXU_DTYPE)


def _band_tables():
    row, _ = _w_in_layout()
    nb = IN_EXT // BAND
    hit = np.zeros((nb, nb), bool)
    for c, r in enumerate(row):
        if r >= 0:
            hit[r // BAND, c // BAND] = True

    def table(h):
        depth = int(h.sum(axis=1).max())
        rows = []
        for o in range(nb):
            used = [int(b) for b in np.nonzero(h[o])[0]]
            spare = [b for b in range(nb) if not h[o, b]]
            rows.append(used + spare[:depth - len(used)])
        return np.asarray(rows, np.int32), depth

    return table(hit.T), table(hit)


def _band_mm_call(a, e, table, depth, e_transposed, name, out_dtype):
    m = a.shape[0]
    nb = IN_EXT // BAND
    dims = NT if e_transposed else NN

    def body(t_ref, a_ref, e_ref, o_ref, acc_ref):
        kk = pl.program_id(1)

        @pl.when(kk == 0)
        def _():
            acc_ref[...] = jnp.zeros_like(acc_ref)

        acc_ref[...] += _dot(a_ref[...], e_ref[...], dims)

        @pl.when(kk == depth - 1)
        def _():
            o_ref[...] = acc_ref[...].astype(out_dtype)

    blk = lambda o, kk, t: t[o * depth + kk]
    e_spec = pl.BlockSpec((BAND, BAND), (lambda o, kk, t: (o, blk(o, kk, t))) if e_transposed else (lambda o, kk, t: (blk(o, kk, t), o)))
    return pl.pallas_call(
        body, name=name, out_shape=jax.ShapeDtypeStruct((m, IN_EXT), out_dtype),
        grid_spec=pltpu.PrefetchScalarGridSpec(
            num_scalar_prefetch=1, grid=(nb, depth),
            in_specs=[pl.BlockSpec((m, BAND), lambda o, kk, t: (0, blk(o, kk, t))), e_spec],
            out_specs=pl.BlockSpec((m, BAND), lambda o, kk, t: (0, o)),
            scratch_shapes=[pltpu.VMEM((m, BAND), F32)]),
        compiler_params=_params(("parallel", "arbitrary")),
    )(jnp.asarray(table.reshape(-1)), a, e)


@jax.custom_vjp
def op_w_in_ext(w_pad, e):
    (table, depth), _ = _band_tables()
    return _band_mm_call(w_pad, e, table, depth, False, "w_in_ext", F32)


def _op_w_in_ext_fwd(w_pad, e):
    return op_w_in_ext(w_pad, e), (e, jnp.zeros((0,), w_pad.dtype))


def _op_w_in_ext_bwd(res, g):
    e, w_like = res
    _, (table, depth) = _band_tables()
    return _band_mm_call(g, e, table, depth, True, "w_in_ext_bwd", w_like.dtype), jnp.zeros_like(e)


op_w_in_ext.defvjp(_op_w_in_ext_fwd, _op_w_in_ext_bwd)


H_COLS = dict(a_val=256, a_gate=256, a_z=256, c_q=256, b_z=256, u=256, c_z=256, q=256, kdup=256, vdup=256, d_z=256,
              c_kv=128, krblk=128)
op_in_proj = make_proj(tuple(H_COLS.values()), "in_proj")


def _ext_mla(w_uq, w_ukv):
    zeros = jnp.zeros((w_ukv.shape[0], 64), w_ukv.dtype)
    uq, uk, uv = [], [], []
    for h in range(4):
        nope, rp = w_uq[:, 96 * h:96 * h + 64], w_uq[:, 96 * h + 64:96 * h + 96]
        uq += [nope, rp, _rot_cols(rp)]
        uk += [w_ukv[:, 128 * h:128 * h + 64], zeros]
        uv.append(w_ukv[:, 128 * h + 64:128 * h + 128])
    return jnp.concatenate(uq, axis=1), jnp.concatenate(uk, axis=1), jnp.concatenate(uv, axis=1)


def _scan_cols(re, im):
    parts = []
    for c in range(SSM_CH // SCAN_CB):
        parts += [re[..., c * SCAN_CB:(c + 1) * SCAN_CB], im[..., c * SCAN_CB:(c + 1) * SCAN_CB]]
    return jnp.concatenate(parts, axis=-1)


def _ext_ssm(a_re, a_im, log_dt, b_re, b_im, c_re, c_im):
    dt = jnp.exp(log_dt)[:, None]
    mag = jnp.exp(a_re * dt)
    lb_re, lb_im = mag * jnp.cos(a_im * dt), mag * jnp.sin(a_im * dt)
    den = a_re * a_re + a_im * a_im
    nr, ni = lb_re - 1.0, lb_im
    f_re = ((nr * a_re + ni * a_im) / den)[..., None]
    f_im = ((ni * a_re - nr * a_im) / den)[..., None]
    bb_re = f_re * b_re - f_im * b_im
    bb_im = f_re * b_im + f_im * b_re
    eye = jnp.eye(SSM_GROUPS, dtype=F32)
    spread = lambda a: a.transpose(0, 2, 1)[:, :, None, :] * eye[:, None, :, None]
    bd_in = lambda bb: spread(bb).reshape(SSM_GROUPS * SSM_GROUP, SSM_CH)
    bd_out = lambda cc: spread(cc).reshape(SSM_CH, SSM_GROUPS * SSM_GROUP)
    w_bu = _scan_cols(bd_in(bb_re), bd_in(bb_im))
    w_y = _scan_cols(bd_out(c_re).T, -bd_out(c_im).T).T
    lam = _scan_cols(lb_re.reshape(1, SSM_CH), lb_im.reshape(1, SSM_CH))
    return w_bu, w_y, lam


def _rope_tables(t):
    pos = jnp.arange(t, dtype=F32)
    inv_freq = ROPE_THETA ** (-jnp.arange(0, 32, 2, dtype=F32) / 32)
    ang = pos[:, None] * inv_freq[None, :]
    cos, sin = jnp.cos(ang), jnp.sin(ang)
    ones, z32, z64 = jnp.ones((t, 64), F32), jnp.zeros((t, 32), F32), jnp.zeros((t, 64), F32)
    cos1 = jnp.concatenate([ones, cos, cos, z32], axis=1)
    sin1 = jnp.concatenate([z64, sin, sin, z32], axis=1)
    return jnp.concatenate([cos1] * 4, axis=1), jnp.concatenate([sin1] * 4, axis=1)


def _to_segments(a):
    t, w = a.shape
    return a.reshape(SCAN_SEGMENTS, t // SCAN_SEGMENTS, w).transpose(1, 0, 2)


def _from_segments(a):
    n, s, w = a.shape
    return a.transpose(1, 0, 2).reshape(n * s, w)


def _layer(x, p_i, cos4, sin4, e_mat, w):
    t = x.shape[0]
    row = lambda v: v.reshape(1, -1)
    f32 = lambda v: v.astype(F32)
    hs = dict(zip(H_COLS, op_in_proj(x, op_w_in_ext(w["w_in_pad"], e_mat))))

    cv = op_conv(hs["a_val"], hs["a_gate"], w["conv_w"], row(w["conv_b"]))
    (y_a,) = op_conv_post((cv, hs["a_z"]), (row(w["conv_norm_g"]), row(w["conv_norm_b"]), f32(w["w_pw2"])))

    w_uq, w_uk, w_uv = _ext_mla(w["w_uq"], f32(w["w_ukv"]))
    q, k, v = op_mla_prep((hs["c_q"], hs["c_kv"], hs["krblk"], cos4, sin4),
                          (row(w["mla_q_norm_g"]), row(w["mla_kv_norm_g"]), w_uq, w_uk, w_uv))
    o_b = op_mla_attn(q, k, v)

    w_bu, w_y, lam = _ext_ssm(w["ssm_a_re"], w["ssm_a_im"], w["ssm_log_dt"], w["ssm_b_re"], w["ssm_b_im"],
                              w["ssm_c_re"], w["ssm_c_im"])
    u_seg = _to_segments(hs["u"]).reshape(t, BRANCH_W)
    bu = op_mm(u_seg, w_bu).reshape(t // SCAN_SEGMENTS, SCAN_SEGMENTS, 2 * SSM_CH)
    hstate = op_scan(bu, lam).reshape(t, 2 * SSM_CH)
    y_ssm = _from_segments(op_mm(hstate, w_y).reshape(t // SCAN_SEGMENTS, SCAN_SEGMENTS, BRANCH_W))
    w_glu = f32(w["w_glu"])
    (y_c,) = op_ssm_post((y_ssm, hs["u"], hs["c_z"]), (row(w["ssm_d"]), w_glu[:, :BRANCH_W], w_glu[:, BRANCH_W:]))

    sink = jnp.repeat(w["attn_sinks"], 64).reshape(1, 2 * LANES)
    o_d = op_swa_attn(hs["q"], hs["kdup"], hs["vdup"], sink)

    merged = op_merge_block(x, (y_a, o_b, y_c, o_d), (None, hs["b_z"], None, hs["d_z"]), w["w_merge"], w["w_branch"], w["b_merge"])
    (x1,) = op_ln((x, op_mm(merged, w["w_out"])), (row(w["ln_g"]), row(w["ln_b"])))
    (out,) = op_ple((x1, op_mm(p_i, w["w_ple"]), op_mm(x1, w["w_ple_gate"])), (row(w["ple_norm_g"]),))
    return out


def _forward(x, p, layers):
    cos4, sin4 = _rope_tables(x.shape[0])
    e_mat = _w_in_layout_matrix()
    for i in range(DEPTH):
        x = _layer(x, p[i], cos4, sin4, e_mat, layers[i])
    return x


SHARD_AXIS = dict(w_in=2, w_merge=2, conv_w=2, w_pw2=1, w_uq=2, w_ukv=2, w_glu=2, w_branch=3, w_out=1, w_ple=2, w_ple_gate=1)
ODD = ("w_uq", "conv_w")
BIG = tuple(n for n in SHARD_AXIS if n not in ODD)
REPLICATED = ("b_merge", "conv_b", "conv_norm_g", "conv_norm_b", "mla_q_norm_g", "mla_kv_norm_g", "ssm_a_re", "ssm_a_im",
              "ssm_log_dt", "ssm_b_re", "ssm_b_im", "ssm_c_re", "ssm_c_im", "ssm_d", "attn_sinks", "ln_g", "ln_b", "ple_norm_g")
WEIGHTS = ("w_in", "w_merge", "b_merge", "conv_w", "conv_b", "conv_norm_g", "conv_norm_b", "w_pw2", "mla_q_norm_g",
           "mla_kv_norm_g", "w_uq", "w_ukv", "ssm_a_re", "ssm_a_im", "ssm_log_dt", "ssm_b_re", "ssm_b_im", "ssm_c_re",
           "ssm_c_im", "ssm_d", "w_glu", "attn_sinks", "w_branch", "w_out", "ln_g", "ln_b", "w_ple", "w_ple_gate", "ple_norm_g")
PACK_COLS = 1024
PACK_ROWS = 16
CHIP_FLIPS = ((1, 0), (0, 1), (1, 1))
N_CHIPS = 4
N_DEV = 8


def _pack_rows(n):
    return -(-n // (SUBLANES * PACK_COLS)) * SUBLANES


def _pack(arrays, dtype):
    blocks, rows = [], 0
    for a in arrays:
        r = _pack_rows(a.size)
        flat = a.reshape(-1).astype(dtype)
        blocks.append(jnp.pad(flat, (0, r * PACK_COLS - a.size)).reshape(r, PACK_COLS))
        rows += r
    pad = -rows % PACK_ROWS
    if pad:
        blocks.append(jnp.zeros((pad, PACK_COLS), dtype))
    return jnp.concatenate(blocks, axis=0)


def _unpack(buf, shapes):
    out, row = [], 0
    for s in shapes:
        n = math.prod(s)
        r = _pack_rows(n)
        out.append(buf[row:row + r].reshape(-1)[:n].reshape(s))
        row += r
    return out


def _flip(v, bit):
    return 1 - v if bit else v


def _window(ref, axis, start, size):
    idx = [slice(None)] * len(ref.shape)
    idx[axis] = pl.ds(start, size)
    return ref.at[tuple(idx)]


def _gather_chips(srcs, axes, stacked):
    units = []
    for k, (s, a) in enumerate(zip(srcs, axes)):
        if stacked[k]:
            units += [(k, l, s.shape[1:], a - 1) for l in range(s.shape[0])]
        else:
            units.append((k, None, s.shape, a))
    nu, nb = len(units), len(srcs)

    def body(*refs):
        ins, outs = refs[:nb], refs[nb:nb + nu]
        ici_send, ici_recv, d2d_send, d2d_recv, local_sems = refs[nb + nu:]
        x, y, c = lax.axis_index("x"), lax.axis_index("y"), lax.axis_index("c")
        me = 2 * x + y

        def mine(u, half=None):
            k, l, shape, _ = units[u]
            ref = ins[k] if l is None else ins[k].at[l]
            return ref if half is None else ref.at[pl.ds(half * (shape[0] // 2), shape[0] // 2)]

        def place(u, chip, half=None):
            _, _, shape, a = units[u]
            size, rows = shape[a], shape[0] // 2
            if half is None:
                return _window(outs[u], a, chip * size, size)
            if a == 0:
                return outs[u].at[pl.ds(chip * size + half * rows, rows)]
            return _window(outs[u].at[pl.ds(half * rows, rows)], a, chip * size, size)

        local = [pltpu.make_async_copy(mine(u), place(u, me), local_sems.at[u]) for u in range(nu)]
        for cp in local:
            cp.start()
        sends = []
        for j, (bx, by) in enumerate(CHIP_FLIPS):
            for u in range(nu):
                cp = pltpu.make_async_remote_copy(src_ref=mine(u, c), dst_ref=place(u, me, c),
                                                  send_sem=ici_send.at[j * nu + u], recv_sem=ici_recv.at[j * nu + u],
                                                  device_id=(_flip(x, bx), _flip(y, by), c), device_id_type=MESH)
                cp.start()
                sends.append(cp)
        for j, (bx, by) in enumerate(CHIP_FLIPS):
            src = 2 * _flip(x, bx) + _flip(y, by)
            for u in range(nu):
                got = place(u, src, c)
                pltpu.make_async_remote_copy(src_ref=got, dst_ref=got, send_sem=ici_send.at[j * nu + u],
                                             recv_sem=ici_recv.at[j * nu + u], device_id=(x, y, c), device_id_type=MESH).wait_recv()
                cp = pltpu.make_async_remote_copy(src_ref=got, dst_ref=got, send_sem=d2d_send.at[j * nu + u],
                                                  recv_sem=d2d_recv.at[j * nu + u], device_id=(x, y, 1 - c), device_id_type=MESH)
                cp.start()
                sends.append(cp)
        for j, (bx, by) in enumerate(CHIP_FLIPS):
            src = 2 * _flip(x, bx) + _flip(y, by)
            for u in range(nu):
                other = place(u, src, 1 - c)
                pltpu.make_async_remote_copy(src_ref=other, dst_ref=other, send_sem=d2d_send.at[j * nu + u],
                                             recv_sem=d2d_recv.at[j * nu + u], device_id=(x, y, c), device_id_type=MESH).wait_recv()
        for cp in sends:
            cp.wait_send()
        for cp in local:
            cp.wait()

    full = lambda shape, a: tuple(N_CHIPS * d if i == a else d for i, d in enumerate(shape))
    res = pl.pallas_call(
        body, name="gather_weights", in_specs=[ANY] * nb, out_specs=[ANY] * nu,
        out_shape=[jax.ShapeDtypeStruct(full(shape, a), srcs[k].dtype) for k, _, shape, a in units],
        scratch_shapes=[pltpu.SemaphoreType.DMA((3 * nu,))] * 4 + [pltpu.SemaphoreType.DMA((nu,))],
    )(*srcs)
    out, it = [], iter(res)
    for k in range(nb):
        out.append([next(it) for _ in range(srcs[k].shape[0])] if stacked[k] else next(it))
    return out


def _exchange_grads(grads, axes, smalls):
    nt, ns = len(grads), len(smalls)
    sizes = [g[0].shape[a] // N_CHIPS for g, a in zip(grads, axes)]
    dev_flips = [(bx, by, bc) for bx in (0, 1) for by in (0, 1) for bc in (0, 1)][1:]
    n_remote = 3 * nt * DEPTH + 7 * ns
    n_local = nt * DEPTH + ns

    def body(*refs):
        g_refs = [refs[k * DEPTH:(k + 1) * DEPTH] for k in range(nt)]
        s_refs = refs[nt * DEPTH:nt * DEPTH + ns]
        outs = refs[nt * DEPTH + ns:nt * DEPTH + ns + nt + ns]
        recv_refs, all_refs = outs[:nt], outs[nt:]
        send_sems, recv_sems, local_sems = refs[-3:]
        x, y, c = lax.axis_index("x"), lax.axis_index("y"), lax.axis_index("c")
        me_chip = 2 * x + y
        me = 4 * x + 2 * y + c
        part = lambda k, i, chip: _window(g_refs[k][i], axes[k], chip * sizes[k], sizes[k])
        started, waits = [], []
        sem, lsem = 0, 0
        for k in range(nt):
            for i in range(DEPTH):
                cp = pltpu.make_async_copy(part(k, i, me_chip), recv_refs[k].at[i, 3], local_sems.at[lsem])
                cp.start()
                started.append(cp.wait)
                lsem += 1
                for j, (bx, by) in enumerate(CHIP_FLIPS):
                    px, py = _flip(x, bx), _flip(y, by)
                    cp = pltpu.make_async_remote_copy(src_ref=part(k, i, 2 * px + py), dst_ref=recv_refs[k].at[i, j],
                                                      send_sem=send_sems.at[sem], recv_sem=recv_sems.at[sem],
                                                      device_id=(px, py, c), device_id_type=MESH)
                    cp.start()
                    started.append(cp.wait_send)
                    waits.append(cp.wait_recv)
                    sem += 1
        for s in range(ns):
            cp = pltpu.make_async_copy(s_refs[s], all_refs[s].at[me], local_sems.at[lsem])
            cp.start()
            started.append(cp.wait)
            lsem += 1
            for bx, by, bc in dev_flips:
                peer = (_flip(x, bx), _flip(y, by), _flip(c, bc))
                cp = pltpu.make_async_remote_copy(src_ref=s_refs[s], dst_ref=all_refs[s].at[me], send_sem=send_sems.at[sem],
                                                  recv_sem=recv_sems.at[sem], device_id=peer, device_id_type=MESH)
                cp.start()
                started.append(cp.wait_send)
                src = 4 * peer[0] + 2 * peer[1] + peer[2]
                waits.append(pltpu.make_async_remote_copy(src_ref=s_refs[s], dst_ref=all_refs[s].at[src], send_sem=send_sems.at[sem],
                                                          recv_sem=recv_sems.at[sem], device_id=peer, device_id_type=MESH).wait_recv)
                sem += 1
        for w in waits + started:
            w()

    shard = lambda g, a: tuple(d // N_CHIPS if i == a else d for i, d in enumerate(g.shape))
    flat = [g for per_layer in grads for g in per_layer]
    return pl.pallas_call(
        body, name="exchange_grads", in_specs=[ANY] * (len(flat) + ns), out_specs=[ANY] * (nt + ns),
        out_shape=[jax.ShapeDtypeStruct((DEPTH, N_CHIPS, *shard(g[0], a)), g[0].dtype) for g, a in zip(grads, axes)]
        + [jax.ShapeDtypeStruct((N_DEV, *s.shape), s.dtype) for s in smalls],
        scratch_shapes=[pltpu.SemaphoreType.DMA((n_remote,)), pltpu.SemaphoreType.DMA((n_remote,)), pltpu.SemaphoreType.DMA((n_local,))],
    )(*flat, *smalls)


def _swap_cores(parts):
    nb = len(parts)

    def body(*refs):
        ins, outs, send_sems, recv_sems = refs[:nb], refs[nb:2 * nb], refs[-2], refs[-1]
        x, y, c = lax.axis_index("x"), lax.axis_index("y"), lax.axis_index("c")
        cps = [pltpu.make_async_remote_copy(src_ref=ins[k], dst_ref=outs[k], send_sem=send_sems.at[k], recv_sem=recv_sems.at[k],
                                            device_id=(x, y, 1 - c), device_id_type=MESH) for k in range(nb)]
        for cp in cps:
            cp.start()
        for cp in cps:
            cp.wait()

    return pl.pallas_call(
        body, name="swap_cores", in_specs=[ANY] * nb, out_specs=[ANY] * nb,
        out_shape=[jax.ShapeDtypeStruct(q.shape, q.dtype) for q in parts],
        scratch_shapes=[pltpu.SemaphoreType.DMA((nb,)), pltpu.SemaphoreType.DMA((nb,))],
    )(*parts)


def _sum_chips_call(recv, cols, name):
    depth, _, r, c = recv.shape
    tile = _pick(r, (512, 256, 128, 64, 32, 16))

    def body(r_ref, o_ref):
        slot = lambda s: r_ref[s, :, pl.ds(0, cols)].astype(F32)
        o_ref[...] = ((slot(3) + slot(0)) + slot(1)) + slot(2)

    return pl.pallas_call(
        body, name=name, grid=(depth, r // tile),
        in_specs=[pl.BlockSpec((None, N_CHIPS, tile, c), lambda l, i: (l, 0, i, 0))],
        out_specs=pl.BlockSpec((None, tile, cols), lambda l, i: (l, i, 0)), out_shape=jax.ShapeDtypeStruct((depth, r, cols), F32),
        compiler_params=_params(("parallel", "parallel")),
    )(recv)


def _sum_slots_call(slots, name):
    n, r, c = slots.shape
    tile = _pick(r, (512, 256, 128, 64, 32, 16, 8))

    def body(s_ref, o_ref):
        acc = s_ref[0]
        for s in range(1, n):
            acc = acc + s_ref[s]
        o_ref[...] = acc

    return pl.pallas_call(
        body, name=name, grid=(r // tile,), in_specs=[pl.BlockSpec((n, tile, c), lambda i: (0, i, 0))],
        out_specs=pl.BlockSpec((tile, c), lambda i: (i, 0)), out_shape=jax.ShapeDtypeStruct((r, c), F32),
        compiler_params=_params(("parallel",)),
    )(slots)


def _adamw_math(w, g, m, v):
    m = ADAM_B1 * m + (1.0 - ADAM_B1) * g
    v = ADAM_B2 * v + (1.0 - ADAM_B2) * (g * g)
    m_hat = m / (1.0 - ADAM_B1 ** ADAM_STEP)
    v_hat = v / (1.0 - ADAM_B2 ** ADAM_STEP)
    return -ADAM_LR * (m_hat / (jnp.sqrt(v_hat) + ADAM_EPS) + ADAM_WD * w), m, v


def _adamw_call(w, m, v, gparts, name):
    r, c = w.shape
    n = len(gparts)
    tile = _pick(r, (512, 256, 128, 64, 32, 16, 8))

    def body(w_ref, m_ref, v_ref, *refs):
        g_refs, (go_ref, d_ref, mo_ref, vo_ref) = refs[:n], refs[n:]
        g = g_refs[0][...]
        for g_ref in g_refs[1:]:
            g = g + g_ref[...]
        go_ref[...] = g
        d_ref[...], mo_ref[...], vo_ref[...] = _adamw_math(w_ref[...], g, m_ref[...], v_ref[...])

    blk = pl.BlockSpec((tile, c), lambda i: (i, 0))
    return pl.pallas_call(
        body, name=name, grid=(r // tile,), in_specs=[blk] * (3 + n),
        out_specs=[blk] * 4, out_shape=[jax.ShapeDtypeStruct((r, c), F32)] * 4,
        compiler_params=_params(("parallel",)),
    )(w, m, v, *gparts)


def _train_local(x, p, layers, target):
    y, vjp = jax.vjp(lambda x_, w_: _forward(x_, p, w_), x, layers)
    loss, dy = _loss_call(y, target)
    dx, dw = vjp(dy)
    return loss, dx, dw


def kernel(x, p, w_in, w_merge, b_merge, conv_w, conv_b, conv_norm_g, conv_norm_b, w_pw2, mla_q_norm_g, mla_kv_norm_g, w_uq, w_ukv, ssm_a_re, ssm_a_im, ssm_log_dt, ssm_b_re, ssm_b_im, ssm_c_re, ssm_c_im, ssm_d, w_glu, attn_sinks, w_branch, w_out, ln_g, ln_b, w_ple, w_ple_gate, ple_norm_g, loss_target, m_w_in, m_w_merge, m_b_merge, m_conv_w, m_conv_b, m_conv_norm_g, m_conv_norm_b, m_w_pw2, m_mla_q_norm_g, m_mla_kv_norm_g, m_w_uq, m_w_ukv, m_ssm_a_re, m_ssm_a_im, m_ssm_log_dt, m_ssm_b_re, m_ssm_b_im, m_ssm_c_re, m_ssm_c_im, m_ssm_d, m_w_glu, m_attn_sinks, m_w_branch, m_w_out, m_ln_g, m_ln_b, m_w_ple, m_w_ple_gate, m_ple_norm_g, v_w_in, v_w_merge, v_b_merge, v_conv_w, v_conv_b, v_conv_norm_g, v_conv_norm_b, v_w_pw2, v_mla_q_norm_g, v_mla_kv_norm_g, v_w_uq, v_w_ukv, v_ssm_a_re, v_ssm_a_im, v_ssm_log_dt, v_ssm_b_re, v_ssm_b_im, v_ssm_c_re, v_ssm_c_im, v_ssm_d, v_w_glu, v_attn_sinks, v_w_branch, v_w_out, v_ln_g, v_ln_b, v_w_ple, v_w_ple_gate, v_ple_norm_g):
    given = dict(locals())
    w_loc = {n: given[n] for n in WEIGHTS}
    m_loc = {n: given["m_" + n] for n in WEIGHTS}
    v_loc = {n: given["v_" + n] for n in WEIGHTS}

    me_chip = 2 * lax.axis_index("x") + lax.axis_index("y")

    wire = {n: w_loc[n].astype(MXU_DTYPE) for n in BIG}
    wire["w_in"] = jnp.pad(wire["w_in"], ((0, 0), (0, 0), (0, IN_SHARD_PAD - IN_SHARD)))
    odd_shapes = [w_loc[n].shape for n in ODD]
    gathered = _gather_chips([wire[n] for n in BIG] + [_pack([w_loc[n] for n in ODD], F32)], [SHARD_AXIS[n] for n in BIG] + [0],
                             [True] * len(BIG) + [False])
    full = dict(zip(BIG, gathered[:-1]))
    odd_parts = [_unpack(part, odd_shapes) for part in jnp.split(gathered[-1], N_CHIPS, axis=0)]
    for k, n in enumerate(ODD):
        full[n] = jnp.concatenate([odd_parts[s][k] for s in range(N_CHIPS)], axis=SHARD_AXIS[n])
    layers = []
    for i in range(DEPTH):
        layer = {n: (full[n][i] if n in full else w_loc[n][i]) for n in WEIGHTS if n != "w_in"}
        layer["w_in_pad"] = full["w_in"][i]
        layers.append(layer)

    loss, dx, dw = _train_local(x[0], p[:, 0], layers, loss_target[0])
    loss = lax.psum(loss, ("x", "y", "c"))

    key = lambda n: "w_in_pad" if n == "w_in" else n
    stacked = lambda n: jnp.stack([dw[i][n] for i in range(DEPTH)])
    small_rep = _pack([stacked(n) for n in REPLICATED], F32)
    small_odd = _pack([stacked(n) for n in ODD], F32)
    *recv, all_rep, all_odd = _exchange_grads([[dw[i][key(n)] for i in range(DEPTH)] for n in BIG],
                                              [SHARD_AXIS[n] - 1 for n in BIG], [small_rep, small_odd])
    parts = []
    for n, r in zip(BIG, recv):
        cols = w_loc[n].shape[-1]
        parts.append(_sum_chips_call(r.reshape(DEPTH, N_CHIPS, -1, r.shape[-1]), cols, "sum_chips_" + n))
    others = _swap_cores(parts)
    g_rep = _sum_slots_call(all_rep, "sum_replicated")
    g_odd = _unpack(_sum_slots_call(all_odd, "sum_odd"), [(DEPTH, *w_loc[n].shape[1:-1], N_CHIPS * w_loc[n].shape[-1]) for n in ODD])

    grads, deltas, new_m, new_v = {}, {}, {}, {}

    def adamw(n, gparts):
        shape = w_loc[n].shape
        two_d = lambda a: a.reshape(-1, shape[-1])
        res = _adamw_call(two_d(w_loc[n]), two_d(m_loc[n]), two_d(v_loc[n]), [two_d(g) for g in gparts], "adamw_" + n)
        grads[n], deltas[n], new_m[n], new_v[n] = [r.reshape(shape) for r in res]

    for n, part, other in zip(BIG, parts, others):
        adamw(n, [part, other])
    for n, g in zip(ODD, g_odd):
        size = w_loc[n].shape[-1]
        adamw(n, [lax.dynamic_slice_in_dim(g, me_chip * size, size, axis=g.ndim - 1)])
    rep_shapes = [w_loc[n].shape for n in REPLICATED]
    res = _adamw_call(_pack([w_loc[n] for n in REPLICATED], F32), _pack([m_loc[n] for n in REPLICATED], F32),
                      _pack([v_loc[n] for n in REPLICATED], F32), [g_rep], "adamw_replicated")
    for dst, buf in zip((grads, deltas, new_m, new_v), res):
        for n, a in zip(REPLICATED, _unpack(buf, rep_shapes)):
            dst[n] = a

    return (loss, dx[None], *[grads[n] for n in WEIGHTS], *[deltas[n] for n in WEIGHTS],
            *[new_m[n] for n in WEIGHTS], *[new_v[n] for n in WEIGHTS])
```
